```python
import math
import jax
import jax.numpy as jnp
from jax import lax
import numpy as np

D_MODEL = 2048
BATCH = 8
SEQ = 2048
DEPTH = 1
DEC_BATCH = 4
DEC_SEQ = 2048
PAST_LEN = 128

GRID_W = 64
NA_HEADS = 8
NA_HEAD_DIM = 128
NA_WIDTH = NA_HEADS * NA_HEAD_DIM
NA_KH = 8
NA_KW = 16
DIFF_HEADS = 8
DIFF_QK_DIM = 64
DIFF_V_DIM = 128
DIFF_QK_WIDTH = DIFF_HEADS * 2 * DIFF_QK_DIM
DIFF_V_WIDTH = DIFF_HEADS * DIFF_V_DIM
Q_BLOCK = 128
IN_COLS = 3 * NA_WIDTH + 2 * DIFF_QK_WIDTH + DIFF_V_WIDTH + 2 * D_MODEL
MEM_LEN = 256
XA_HEADS = 4
XA_HEAD_DIM = 128
XA_WIDTH = XA_HEADS * XA_HEAD_DIM
N_GROUPS = 4
EXPERTS_PER_GROUP = 8
N_EXPERTS = N_GROUPS * EXPERTS_PER_GROUP
TOP_K = 2
D_EXPERT = 512
MOE_BLOCK = 128
NORM_EPS = 1e-6
SUBLN_EPS = 1e-5
NEG_INF = -1e30

kernel_name = 'hybrid_natten_diffattn_hmoe_encoder'


def rmsnorm(x, g, eps=NORM_EPS):
    xf = x.astype(jnp.float32)
    y = xf * lax.rsqrt(jnp.mean(xf * xf, axis=-1, keepdims=True) + eps)
    return (y * g.astype(jnp.float32)).astype(x.dtype)


def alibi_slopes(n_heads):
    return jnp.exp2(-8.0 * jnp.arange(1, n_heads + 1, dtype=jnp.float32) / n_heads)


def neighborhood_attention(q, k, v, rpb):
    B, S, H, d = q.shape
    rows = S // GRID_W
    kh = min(NA_KH, rows)
    qg = q.reshape(B, rows, GRID_W, H, d)
    kg = k.reshape(B, rows, GRID_W, H, d)
    vg = v.reshape(B, rows, GRID_W, H, d)
    cols = jnp.arange(GRID_W)
    cstart = jnp.clip(cols - NA_KW // 2, 0, GRID_W - NA_KW)
    col_ok = (cols[None, :] >= cstart[:, None]) & (cols[None, :] < cstart[:, None] + NA_KW)
    dc = jnp.clip(cols[None, :] - cols[:, None], 1 - NA_KW, NA_KW - 1) + NA_KW - 1
    scale = d ** -0.5

    def row_block(r):
        rstart = jnp.clip(r - kh // 2, 0, rows - kh)
        qb = lax.dynamic_index_in_dim(qg, r, axis=1, keepdims=False)
        kb = lax.dynamic_slice_in_dim(kg, rstart, kh, axis=1)
        vb = lax.dynamic_slice_in_dim(vg, rstart, kh, axis=1)
        dr = rstart + jnp.arange(kh) - r + NA_KH - 1
        bias = rpb[:, dr[None, :, None], dc[:, None, :]].astype(jnp.float32)
        s = jnp.einsum('bchd,bijhd->bhcij', qb, kb).astype(jnp.float32) * scale + bias
        s = jnp.where(col_ok[None, None, :, None, :], s, NEG_INF)
        p = jax.nn.softmax(s.reshape(B, H, GRID_W, kh * GRID_W), axis=-1)
        p = p.reshape(B, H, GRID_W, kh, GRID_W).astype(v.dtype)
        return jnp.einsum('bhcij,bijhd->bchd', p, vb)

    o = lax.map(row_block, jnp.arange(rows))
    return jnp.moveaxis(o, 0, 1).reshape(B, S, H * d)


def differential_attention(q, k, v, lq1, lk1, lq2, lk2, subln_g, lambda_init):
    B, S = q.shape[0], q.shape[1]
    lam = (jnp.exp(jnp.sum(lq1.astype(jnp.float32) * lk1.astype(jnp.float32)))
           - jnp.exp(jnp.sum(lq2.astype(jnp.float32) * lk2.astype(jnp.float32))) + lambda_init)
    slopes = alibi_slopes(DIFF_HEADS)
    scale = DIFF_QK_DIM ** -0.5
    kpos = jnp.arange(S)

    def block(i):
        q0 = i * Q_BLOCK
        qb = lax.dynamic_slice_in_dim(q, q0, Q_BLOCK, axis=1)
        s = jnp.einsum('bqhmd,bkhmd->bmhqk', qb, k).astype(jnp.float32) * scale
        dist = jnp.abs(q0 + jnp.arange(Q_BLOCK)[:, None] - kpos[None, :]).astype(jnp.float32)
        s = s - slopes[:, None, None] * dist
        p = jax.nn.softmax(s, axis=-1)
        a = (p[:, 0] - lam * p[:, 1]).astype(v.dtype)
        return jnp.einsum('bhqk,bkhd->bqhd', a, v)

    o = lax.map(block, jnp.arange(S // Q_BLOCK))
    o = jnp.moveaxis(o, 0, 1).reshape(B, S, DIFF_HEADS, DIFF_V_DIM)
    o = rmsnorm(o, subln_g, SUBLN_EPS) * (1.0 - lambda_init)
    return o.reshape(B, S, DIFF_V_WIDTH)


def memory_cross_attention(h, mem, w_q, w_kv, w_o):
    B, S, _ = h.shape
    M = mem.shape[1]
    q = (h @ w_q).reshape(B, S, XA_HEADS, XA_HEAD_DIM)
    kv = (mem @ w_kv).reshape(B, M, 2, XA_HEADS, XA_HEAD_DIM)
    s = jnp.einsum('bshd,bmhd->bhsm', q, kv[:, :, 0]).astype(jnp.float32) * (XA_HEAD_DIM ** -0.5)
    p = jax.nn.softmax(s, axis=-1).astype(h.dtype)
    o = jnp.einsum('bhsm,bmhd->bshd', p, kv[:, :, 1]).reshape(B, S, XA_WIDTH)
    return o @ w_o


def hierarchical_moe(x, wg, bg, we, be, w_gate, w_up, w_down):
    B, S, D = x.shape
    T = B * S
    xt = x.reshape(T, D)
    glog = (xt @ wg).astype(jnp.float32) + bg.astype(jnp.float32)
    g = jnp.argmax(glog, axis=-1)
    pg = jnp.take_along_axis(jax.nn.softmax(glog, axis=-1), g[:, None], axis=1)
    elog = ((xt @ we).astype(jnp.float32) + be.astype(jnp.float32)).reshape(T, N_GROUPS, EXPERTS_PER_GROUP)
    elog = jnp.take_along_axis(elog, g[:, None, None], axis=1)[:, 0]
    top_val, top_idx = lax.top_k(elog, TOP_K)
    gate = (pg * jax.nn.softmax(top_val, axis=-1)).reshape(-1)
    eid = (g[:, None] * EXPERTS_PER_GROUP + top_idx).reshape(-1)
    A = T * TOP_K
    tok = jnp.repeat(jnp.arange(T), TOP_K)
    order = jnp.argsort(eid)
    eid_s = eid[order]
    tok_s = tok[order]
    gate_s = gate[order]
    counts = jnp.bincount(eid, length=N_EXPERTS)
    starts = jnp.cumsum(counts) - counts
    padded = (counts + MOE_BLOCK - 1) // MOE_BLOCK * MOE_BLOCK
    pends = jnp.cumsum(padded)
    pstarts = pends - padded
    dest = pstarts[eid_s] + jnp.arange(A) - starts[eid_s]
    n_blocks = A // MOE_BLOCK + N_EXPERTS
    P = n_blocks * MOE_BLOCK
    slot_tok = jnp.full((P,), T, dtype=jnp.int32).at[dest].set(tok_s.astype(jnp.int32))
    slot_gate = jnp.zeros((P,), x.dtype).at[dest].set(gate_s.astype(x.dtype))
    blk_expert = jnp.minimum(jnp.searchsorted(pends, jnp.arange(n_blocks) * MOE_BLOCK, side='right'),
                             N_EXPERTS - 1)
    x_pad = jnp.concatenate([xt, jnp.zeros((1, D), xt.dtype)], axis=0)
    xs = x_pad[slot_tok].reshape(n_blocks, MOE_BLOCK, D)

    def expert_block(args):
        xb, e = args
        hdn = jax.nn.silu(xb @ w_gate[e]) * (xb @ w_up[e])
        return hdn @ w_down[e]

    ys = lax.map(expert_block, (xs, blk_expert)).reshape(P, D)
    out = jnp.zeros((T + 1, D), ys.dtype).at[slot_tok].add(ys * slot_gate[:, None])[:T]
    return out.reshape(B, S, D)


def encoder_layer(x, mem, l, p):
    B, S, _ = x.shape
    h = rmsnorm(x, p['mix_norm_g'][l])
    proj = h @ p['w_in'][l]
    splits = np.cumsum([NA_WIDTH, NA_WIDTH, NA_WIDTH, DIFF_QK_WIDTH, DIFF_QK_WIDTH,
                        DIFF_V_WIDTH, D_MODEL]).tolist()
    q_na, k_na, v_na, q_df, k_df, v_df, g_na, g_df = jnp.split(proj, splits, axis=-1)
    hs = (B, S, NA_HEADS, NA_HEAD_DIM)
    a = neighborhood_attention(q_na.reshape(hs), k_na.reshape(hs), v_na.reshape(hs), p['na_rpb'][l])
    lambda_init = 0.8 - 0.6 * math.exp(-0.3 * l)
    qk_shape = (B, S, DIFF_HEADS, 2, DIFF_QK_DIM)
    b = differential_attention(q_df.reshape(qk_shape), k_df.reshape(qk_shape),
                               v_df.reshape(B, S, DIFF_HEADS, DIFF_V_DIM),
                               p['diff_lambda_q1'][l], p['diff_lambda_k1'][l],
                               p['diff_lambda_q2'][l], p['diff_lambda_k2'][l],
                               p['diff_subln_g'][l], lambda_init)
    merged = (jax.nn.sigmoid(g_na) * (a @ p['w_branch_na'][l])
              + jax.nn.sigmoid(g_df) * (b @ p['w_branch_diff'][l]))
    x = x + merged @ p['w_out'][l]
    x = x + memory_cross_attention(rmsnorm(x, p['xa_norm_g'][l]), rmsnorm(mem, p['mem_norm_g'][l]),
                                   p['xa_w_q'][l], p['xa_w_kv'][l], p['xa_w_o'][l])
    x = x + hierarchical_moe(rmsnorm(x, p['ffn_norm_g'][l]),
                             p['router_group_w'][l], p['router_group_b'][l],
                             p['router_expert_w'][l], p['router_expert_b'][l],
                             p['w_gate'][l], p['w_up'][l], p['w_down'][l])
    return x


def encoder_trunk(x, mem, p, final_norm_g):
    for l in range(DEPTH):
        x = encoder_layer(x, mem, l, p)
    return rmsnorm(x, final_norm_g)


def setup_inputs(seed: int = 0) -> dict:
    key = jax.random.key(seed)
    ks = jax.random.split(key, 32)
    L = DEPTH

    def nrm(k, shape, scale):
        return jax.random.normal(k, shape, jnp.float32) * scale

    def gain(k, shape):
        return 1.0 + 0.02 * jax.random.normal(k, shape, jnp.float32)

    return {
        'x_prompt': nrm(ks[0], (BATCH, SEQ, D_MODEL), 1.0),
        'x_sample': nrm(ks[1], (DEC_BATCH, DEC_SEQ, D_MODEL), 1.0),
        'mem_prompt': nrm(ks[2], (BATCH, MEM_LEN, D_MODEL), 1.0),
        'mem_sample': nrm(ks[3], (DEC_BATCH, MEM_LEN, D_MODEL), 1.0),
        'mix_norm_g': gain(ks[4], (L, D_MODEL)),
        'w_in': nrm(ks[5], (L, D_MODEL, IN_COLS), D_MODEL ** -0.5),
        'na_rpb': nrm(ks[6], (L, NA_HEADS, 2 * NA_KH - 1, 2 * NA_KW - 1), 0.1),
        'diff_lambda_q1': nrm(ks[7], (L, DIFF_QK_DIM), 0.1),
        'diff_lambda_k1': nrm(ks[8], (L, DIFF_QK_DIM), 0.1),
        'diff_lambda_q2': nrm(ks[9], (L, DIFF_QK_DIM), 0.1),
        'diff_lambda_k2': nrm(ks[10], (L, DIFF_QK_DIM), 0.1),
        'diff_subln_g': gain(ks[11], (L, DIFF_V_DIM)),
        'w_branch_na': nrm(ks[12], (L, NA_WIDTH, D_MODEL), NA_WIDTH ** -0.5),
        'w_branch_diff': nrm(ks[13], (L, DIFF_V_WIDTH, D_MODEL), DIFF_V_WIDTH ** -0.5),
        'w_out': nrm(ks[14], (L, D_MODEL, D_MODEL), D_MODEL ** -0.5),
        'xa_norm_g': gain(ks[15], (L, D_MODEL)),
        'mem_norm_g': gain(ks[16], (L, D_MODEL)),
        'xa_w_q': nrm(ks[17], (L, D_MODEL, XA_WIDTH), D_MODEL ** -0.5),
        'xa_w_kv': nrm(ks[18], (L, D_MODEL, 2 * XA_WIDTH), D_MODEL ** -0.5),
        'xa_w_o': nrm(ks[19], (L, XA_WIDTH, D_MODEL), XA_WIDTH ** -0.5),
        'ffn_norm_g': gain(ks[20], (L, D_MODEL)),
        'router_group_w': nrm(ks[21], (L, D_MODEL, N_GROUPS), D_MODEL ** -0.5),
        'router_group_b': nrm(ks[22], (L, N_GROUPS), 0.01),
        'router_expert_w': nrm(ks[23], (L, D_MODEL, N_EXPERTS), D_MODEL ** -0.5),
        'router_expert_b': nrm(ks[24], (L, N_EXPERTS), 0.01),
        'w_gate': nrm(ks[25], (L, N_EXPERTS, D_MODEL, D_EXPERT), D_MODEL ** -0.5),
        'w_up': nrm(ks[26], (L, N_EXPERTS, D_MODEL, D_EXPERT), D_MODEL ** -0.5),
        'w_down': nrm(ks[27], (L, N_EXPERTS, D_EXPERT, D_MODEL), D_EXPERT ** -0.5),
        'final_norm_g': gain(ks[28], (D_MODEL,)),
    }


def reference(x_prompt, x_sample, mem_prompt, mem_sample, mix_norm_g, w_in, na_rpb,
              diff_lambda_q1, diff_lambda_k1, diff_lambda_q2, diff_lambda_k2, diff_subln_g,
              w_branch_na, w_branch_diff, w_out, xa_norm_g, mem_norm_g, xa_w_q, xa_w_kv, xa_w_o,
              ffn_norm_g, router_group_w, router_group_b, router_expert_w, router_expert_b,
              w_gate, w_up, w_down, final_norm_g):
    p = {
        'mix_norm_g': mix_norm_g, 'w_in': w_in, 'na_rpb': na_rpb,
        'diff_lambda_q1': diff_lambda_q1, 'diff_lambda_k1': diff_lambda_k1,
        'diff_lambda_q2': diff_lambda_q2, 'diff_lambda_k2': diff_lambda_k2,
        'diff_subln_g': diff_subln_g, 'w_branch_na': w_branch_na, 'w_branch_diff': w_branch_diff,
        'w_out': w_out, 'xa_norm_g': xa_norm_g, 'mem_norm_g': mem_norm_g,
        'xa_w_q': xa_w_q, 'xa_w_kv': xa_w_kv, 'xa_w_o': xa_w_o, 'ffn_norm_g': ffn_norm_g,
        'router_group_w': router_group_w, 'router_group_b': router_group_b,
        'router_expert_w': router_expert_w, 'router_expert_b': router_expert_b,
        'w_gate': w_gate, 'w_up': w_up, 'w_down': w_down,
    }
    y_prompt = encoder_trunk(x_prompt, mem_prompt, p, final_norm_g)
    y_sample = encoder_trunk(x_sample, mem_sample, p, final_norm_g)
    return (y_prompt, y_sample)
```

```python
import functools
import math

import jax
import jax.numpy as jnp
from jax import lax
from jax.experimental import pallas as pl
from jax.experimental.pallas import tpu as pltpu

F32 = jnp.float32
BF16 = jnp.bfloat16
I32 = jnp.int32
U32 = jnp.uint32

D_MODEL = 2048
SEQ = 2048
GRID_W = 64
ROWS = SEQ // GRID_W
NA_HEADS = 8
NA_HEAD_DIM = 128
NA_WIDTH = NA_HEADS * NA_HEAD_DIM
NA_KH = 8
NA_KW = 16
DIFF_HEADS = 8
DIFF_QK_DIM = 64
DIFF_V_DIM = 128
DIFF_QK_WIDTH = DIFF_HEADS * 2 * DIFF_QK_DIM
DIFF_V_WIDTH = DIFF_HEADS * DIFF_V_DIM
IN_COLS = 3 * NA_WIDTH + 2 * DIFF_QK_WIDTH + DIFF_V_WIDTH + 2 * D_MODEL
MEM_LEN = 256
XA_HEADS = 4
XA_HEAD_DIM = 128
XA_WIDTH = XA_HEADS * XA_HEAD_DIM
N_GROUPS = 4
EXPERTS_PER_GROUP = 8
N_EXPERTS = N_GROUPS * EXPERTS_PER_GROUP
TOP_K = 2
D_EXPERT = 512
NORM_EPS = 1e-6
SUBLN_EPS = 1e-5
NEG_INF = -1e30
LAMBDA_INIT = 0.8 - 0.6 * math.exp(-0.3 * 0)

LANES = 128
VMEM_LIMIT = 56 * 1024 * 1024

INPROJ_TM = 1024
INPROJ_TN = 1024
NORM_CHUNK = 128
DIFF_TQ = 256
MERGE_TM = 256
XA_TM = 256
RANK_TM = 512
MOE_TM = 256
ROW_TM = 256


def _cparams(sem):
    return pltpu.CompilerParams(dimension_semantics=sem, vmem_limit_bytes=VMEM_LIMIT)


def _rms_scale(x, g, eps):
    ms = jnp.mean(x * x, axis=-1, keepdims=True)
    return x * lax.rsqrt(ms + eps) * g


def _dot_nt(a, b):
    return lax.dot_general(a, b, (((1,), (1,)), ((), ())), preferred_element_type=F32)


def _inproj_kernel(x_ref, g_ref, w_ref, o_ref, h_ref):
    @pl.when(pl.program_id(1) == 0)
    def _():
        def chunk(c, carry):
            rows = pl.ds(pl.multiple_of(c * NORM_CHUNK, NORM_CHUNK), NORM_CHUNK)
            h_ref[rows, :] = _rms_scale(x_ref[rows, :], g_ref[...], NORM_EPS).astype(BF16)
            return carry
        lax.fori_loop(0, x_ref.shape[0] // NORM_CHUNK, chunk, 0)

    o_ref[...] = jnp.dot(h_ref[...], w_ref[...], preferred_element_type=F32).astype(o_ref.dtype)


def _inproj(x, g, w):
    t, d = x.shape
    n = w.shape[1]
    tm = min(INPROJ_TM, t)
    return pl.pallas_call(
        _inproj_kernel,
        grid=(t // tm, n // INPROJ_TN),
        in_specs=[pl.BlockSpec((tm, d), lambda i, j: (i, 0)),
                  pl.BlockSpec((1, d), lambda i, j: (0, 0)),
                  pl.BlockSpec((d, INPROJ_TN), lambda i, j: (0, j))],
        out_specs=pl.BlockSpec((tm, INPROJ_TN), lambda i, j: (i, j)),
        out_shape=jax.ShapeDtypeStruct((t, n), BF16),
        scratch_shapes=[pltpu.VMEM((tm, d), BF16)],
        compiler_params=_cparams(("parallel", "arbitrary")),
        name="inproj",
    )(x, g, w)


def _na_bias_table(rpb):
    cols = jnp.arange(GRID_W)
    cstart = jnp.clip(cols - NA_KW // 2, 0, GRID_W - NA_KW)
    col_ok = (cols[None, :] >= cstart[:, None]) & (cols[None, :] < cstart[:, None] + NA_KW)
    dc = jnp.clip(cols[None, :] - cols[:, None], 1 - NA_KW, NA_KW - 1) + NA_KW - 1
    t = rpb.astype(F32)[:, :, dc]
    t = jnp.where(col_ok[None, None], t, NEG_INF)
    idx = jnp.arange(NA_KH)[:, None] + jnp.arange(NA_KH)[None, :]
    b = t[:, idx]
    return b.transpose(0, 1, 3, 2, 4).reshape(NA_HEADS, NA_KH, GRID_W, NA_KH * GRID_W)


def _na_kernel(q_ref, k_ref, v_ref, bias_ref, o_ref):
    scale = NA_HEAD_DIM ** -0.5
    win = NA_KH * GRID_W

    def row(r, carry):
        rstart = jnp.clip(r - NA_KH // 2, 0, ROWS - NA_KH)
        u = rstart - r + NA_KH - 1
        qrows = pl.ds(pl.multiple_of(r * GRID_W, GRID_W), GRID_W)
        krows = pl.ds(pl.multiple_of(rstart * GRID_W, GRID_W), win)
        s = _dot_nt(q_ref[0, qrows, :], k_ref[0, krows, :]) * scale + bias_ref[0, u]
        m = jnp.max(s, axis=-1, keepdims=True)
        e = jnp.exp(s - m)
        l = jnp.sum(e, axis=-1, keepdims=True)
        o = jnp.dot(e.astype(BF16), v_ref[0, krows, :], preferred_element_type=F32)
        o_ref[0, qrows, :] = (o / l).astype(o_ref.dtype)
        return carry

    lax.fori_loop(0, ROWS, row, 0)


def _na_attention(proj3, bias):
    b = proj3.shape[0]
    hd = NA_HEAD_DIM
    return pl.pallas_call(
        _na_kernel,
        grid=(b, NA_HEADS),
        in_specs=[pl.BlockSpec((1, SEQ, hd), lambda i, h: (i, 0, h)),
                  pl.BlockSpec((1, SEQ, hd), lambda i, h: (i, 0, NA_HEADS + h)),
                  pl.BlockSpec((1, SEQ, hd), lambda i, h: (i, 0, 2 * NA_HEADS + h)),
                  pl.BlockSpec((1, NA_KH, GRID_W, NA_KH * GRID_W), lambda i, h: (h, 0, 0, 0))],
        out_specs=pl.BlockSpec((1, SEQ, hd), lambda i, h: (i, 0, h)),
        out_shape=jax.ShapeDtypeStruct((b, SEQ, NA_WIDTH), BF16),
        compiler_params=_cparams(("parallel", "arbitrary")),
        name="na_attn",
    )(proj3, proj3, proj3, bias)


def _diff_kernel(slopes_ref, lq1_ref, lk1_ref, lq2_ref, lk2_ref, q_ref, k_ref, v_ref, g_ref, o_ref):
    h = pl.program_id(1)
    qi = pl.program_id(2)
    tq = q_ref.shape[1]
    slope = slopes_ref[h]
    lam = (jnp.exp(jnp.sum(lq1_ref[...] * lk1_ref[...], axis=-1, keepdims=True))
           - jnp.exp(jnp.sum(lq2_ref[...] * lk2_ref[...], axis=-1, keepdims=True)) + LAMBDA_INIT)
    q = q_ref[0] * (DIFF_QK_DIM ** -0.5)
    k = k_ref[0]
    row = lax.broadcasted_iota(I32, (tq, SEQ), 0) + qi * tq
    col = lax.broadcasted_iota(I32, (tq, SEQ), 1)
    bias = slope * jnp.abs(row - col).astype(F32)

    def probs(mi):
        sl = slice(mi * DIFF_QK_DIM, (mi + 1) * DIFF_QK_DIM)
        s = _dot_nt(q[:, sl], k[:, sl]) - bias
        m = jnp.max(s, axis=-1, keepdims=True)
        e = jnp.exp(s - m)
        return e * (1.0 / jnp.sum(e, axis=-1, keepdims=True))

    a = (probs(0) - lam * probs(1)).astype(BF16)
    o = jnp.dot(a, v_ref[0], preferred_element_type=F32)
    o = _rms_scale(o, g_ref[...], SUBLN_EPS) * (1.0 - LAMBDA_INIT)
    o_ref[0] = o.astype(o_ref.dtype)


def _diff_attention(proj3, slopes, lq1, lk1, lq2, lk2, subln_g):
    b = proj3.shape[0]
    qoff = 3 * NA_WIDTH // LANES
    koff = qoff + DIFF_QK_WIDTH // LANES
    voff = koff + DIFF_QK_WIDTH // LANES
    vec = lambda n: pl.BlockSpec((1, n), lambda i, h, j: (0, 0))
    return pl.pallas_call(
        _diff_kernel,
        grid=(b, DIFF_HEADS, SEQ // DIFF_TQ),
        in_specs=[pl.BlockSpec(memory_space=pltpu.SMEM),
                  vec(DIFF_QK_DIM), vec(DIFF_QK_DIM), vec(DIFF_QK_DIM), vec(DIFF_QK_DIM),
                  pl.BlockSpec((1, DIFF_TQ, LANES), lambda i, h, j: (i, j, qoff + h)),
                  pl.BlockSpec((1, SEQ, LANES), lambda i, h, j: (i, 0, koff + h)),
                  pl.BlockSpec((1, SEQ, LANES), lambda i, h, j: (i, 0, voff + h)),
                  vec(DIFF_V_DIM)],
        out_specs=pl.BlockSpec((1, DIFF_TQ, DIFF_V_DIM), lambda i, h, j: (i, j, h)),
        out_shape=jax.ShapeDtypeStruct((b, SEQ, DIFF_V_WIDTH), BF16),
        compiler_params=_cparams(("parallel", "arbitrary", "arbitrary")),
        name="diff_attn",
    )(slopes, lq1, lk1, lq2, lk2, proj3, proj3, proj3, subln_g)


def _merge_kernel(a_ref, b_ref, gna_ref, gdf_ref, x_ref, wna_ref, wdf_ref, wout_ref, o_ref):
    pa = jnp.dot(a_ref[...], wna_ref[...], preferred_element_type=F32)
    pb = jnp.dot(b_ref[...], wdf_ref[...], preferred_element_type=F32)
    merged = (jax.nn.sigmoid(gna_ref[...].astype(F32)) * pa
              + jax.nn.sigmoid(gdf_ref[...].astype(F32)) * pb)
    o_ref[...] = x_ref[...] + jnp.dot(merged.astype(BF16), wout_ref[...], preferred_element_type=F32)


def _merge(a, b, proj, x, wna, wdf, wout):
    t = x.shape[0]
    tm = MERGE_TM
    gna_blk = (3 * NA_WIDTH + 2 * DIFF_QK_WIDTH + DIFF_V_WIDTH) // D_MODEL
    const = lambda shape: pl.BlockSpec(shape, lambda i: (0, 0))
    return pl.pallas_call(
        _merge_kernel,
        grid=(t // tm,),
        in_specs=[pl.BlockSpec((tm, NA_WIDTH), lambda i: (i, 0)),
                  pl.BlockSpec((tm, DIFF_V_WIDTH), lambda i: (i, 0)),
                  pl.BlockSpec((tm, D_MODEL), lambda i: (i, gna_blk)),
                  pl.BlockSpec((tm, D_MODEL), lambda i: (i, gna_blk + 1)),
                  pl.BlockSpec((tm, D_MODEL), lambda i: (i, 0)),
                  const((NA_WIDTH, D_MODEL)), const((DIFF_V_WIDTH, D_MODEL)), const((D_MODEL, D_MODEL))],
        out_specs=pl.BlockSpec((tm, D_MODEL), lambda i: (i, 0)),
        out_shape=jax.ShapeDtypeStruct((t, D_MODEL), F32),
        compiler_params=_cparams(("parallel",)),
        name="merge_outproj",
    )(a, b, proj, proj, x, wna, wdf, wout)


def _memkv_kernel(m_ref, g_ref, w_ref, o_ref):
    h = _rms_scale(m_ref[...], g_ref[...], NORM_EPS).astype(BF16)
    o_ref[...] = jnp.dot(h, w_ref[...], preferred_element_type=F32).astype(o_ref.dtype)


def _memkv(mem, g, w):
    t = mem.shape[0]
    return pl.pallas_call(
        _memkv_kernel,
        grid=(t // MEM_LEN,),
        in_specs=[pl.BlockSpec((MEM_LEN, D_MODEL), lambda i: (i, 0)),
                  pl.BlockSpec((1, D_MODEL), lambda i: (0, 0)),
                  pl.BlockSpec((D_MODEL, 2 * XA_WIDTH), lambda i: (0, 0))],
        out_specs=pl.BlockSpec((MEM_LEN, 2 * XA_WIDTH), lambda i: (i, 0)),
        out_shape=jax.ShapeDtypeStruct((t, 2 * XA_WIDTH), BF16),
        compiler_params=_cparams(("parallel",)),
        name="mem_kv",
    )(mem, g, w)


def _route(logits):
    lane = lax.broadcasted_iota(I32, logits.shape, 1).astype(F32)
    ninf = -jnp.inf
    big = float(LANES)
    gl = jnp.where(lane < N_GROUPS, logits, ninf)
    gmax = jnp.max(gl, axis=-1, keepdims=True)
    g = jnp.min(jnp.where(gl == gmax, lane, big), axis=-1, keepdims=True)
    pg = 1.0 / jnp.sum(jnp.exp(gl - gmax), axis=-1, keepdims=True)
    lo = N_GROUPS + EXPERTS_PER_GROUP * g
    el = jnp.where((lane >= lo) & (lane < lo + EXPERTS_PER_GROUP), logits, ninf)
    v1 = jnp.max(el, axis=-1, keepdims=True)
    i1 = jnp.min(jnp.where(el == v1, lane, big), axis=-1, keepdims=True)
    el2 = jnp.where(lane == i1, ninf, el)
    v2 = jnp.max(el2, axis=-1, keepdims=True)
    i2 = jnp.min(jnp.where(el2 == v2, lane, big), axis=-1, keepdims=True)
    t = jnp.exp(v2 - v1)
    den = 1.0 + t
    gate1 = pg * (1.0 / den)
    gate2 = pg * (t / den)
    gates = jnp.where(lane == 0.0, gate1, jnp.where(lane == 1.0, gate2, 0.0))
    eids = jnp.where(lane == 0.0, i1 - N_GROUPS, jnp.where(lane == 1.0, i2 - N_GROUPS, 0.0)).astype(I32)
    return gates, eids


def _pack_halves(h):
    half = h.shape[1] // 2
    hi = pltpu.bitcast(h[:, :half].astype(BF16).astype(F32), U32)
    lo = pltpu.bitcast(h[:, half:].astype(BF16).astype(F32), U32)
    return hi | (lo >> 16)


def _unpack_halves(w):
    hi = pltpu.bitcast(w & jnp.uint32(0xFFFF0000), F32).astype(BF16)
    lo = pltpu.bitcast(w << 16, F32).astype(BF16)
    return jnp.concatenate([hi, lo], axis=1)


def _xattn_kernel(x_ref, kv_ref, gx_ref, wq_ref, wo_ref, gf_ref, wr_ref, br_ref,
                  x2_ref, hp_ref, gate_ref, eid_ref):
    x1 = x_ref[...]
    hq = _rms_scale(x1, gx_ref[...], NORM_EPS).astype(BF16)
    q = jnp.dot(hq, wq_ref[...], preferred_element_type=F32).astype(BF16)
    scale = XA_HEAD_DIM ** -0.5
    outs = []
    for h in range(XA_HEADS):
        sl = slice(h * XA_HEAD_DIM, (h + 1) * XA_HEAD_DIM)
        vsl = slice(XA_WIDTH + h * XA_HEAD_DIM, XA_WIDTH + (h + 1) * XA_HEAD_DIM)
        s = _dot_nt(q[:, sl], kv_ref[:, sl]) * scale
        m = jnp.max(s, axis=-1, keepdims=True)
        e = jnp.exp(s - m)
        l = jnp.sum(e, axis=-1, keepdims=True)
        o = jnp.dot(e.astype(BF16), kv_ref[:, vsl], preferred_element_type=F32)
        outs.append((o / l).astype(BF16))
    o = jnp.concatenate(outs, axis=1)
    x2 = x1 + jnp.dot(o, wo_ref[...], preferred_element_type=F32)
    x2_ref[...] = x2
    h3 = _rms_scale(x2, gf_ref[...], NORM_EPS)
    hp_ref[...] = _pack_halves(h3)
    logits = jnp.dot(h3.astype(BF16), wr_ref[...], preferred_element_type=F32) + br_ref[...]
    gates, eids = _route(logits)
    gate_ref[...] = gates
    eid_ref[...] = eids


def _xattn(x1, kv, gx, wq, wo, gf, wr, br):
    t = x1.shape[0]
    tm = XA_TM
    per_batch = SEQ // tm
    const = lambda shape: pl.BlockSpec(shape, lambda i: (0, 0))
    tile = lambda n: pl.BlockSpec((tm, n), lambda i: (i, 0))
    return pl.pallas_call(
        _xattn_kernel,
        grid=(t // tm,),
        in_specs=[tile(D_MODEL),
                  pl.BlockSpec((MEM_LEN, 2 * XA_WIDTH), lambda i: (i // per_batch, 0)),
                  const((1, D_MODEL)), const((D_MODEL, XA_WIDTH)), const((XA_WIDTH, D_MODEL)),
                  const((1, D_MODEL)), const((D_MODEL, LANES)), const((1, LANES))],
        out_specs=[tile(D_MODEL), tile(D_MODEL // 2), tile(LANES), tile(LANES)],
        out_shape=[jax.ShapeDtypeStruct((t, D_MODEL), F32),
                   jax.ShapeDtypeStruct((t, D_MODEL // 2), U32),
                   jax.ShapeDtypeStruct((t, LANES), F32),
                   jax.ShapeDtypeStruct((t, LANES), I32)],
        compiler_params=_cparams(("parallel",)),
        name="xattn_router",
    )(x1, kv, gx, wq, wo, gf, wr, br)


def _rank_kernel(eid_ref, rank_ref, cnt_ref, carry_ref):
    @pl.when(pl.program_id(0) == 0)
    def _():
        carry_ref[...] = jnp.zeros_like(carry_ref)

    tm = eid_ref.shape[0]
    eid = eid_ref[...]
    lane = lax.broadcasted_iota(I32, (tm, LANES), 1)
    m1 = lane == eid[:, 0:1]
    m2 = lane == eid[:, 1:2]
    onehot = jnp.where(m1, 1.0, jnp.where(m2, 1.0, 0.0))
    r = lax.broadcasted_iota(I32, (tm, tm), 0)
    c = lax.broadcasted_iota(I32, (tm, tm), 1)
    tri = jnp.where(c < r, 1.0, 0.0).astype(BF16)
    before = jnp.dot(tri, onehot.astype(BF16), preferred_element_type=F32) + carry_ref[0:1, :]
    r1 = jnp.sum(jnp.where(m1, before, 0.0), axis=-1, keepdims=True)
    r2 = jnp.sum(jnp.where(m2, before, 0.0), axis=-1, keepdims=True)
    rank_ref[...] = jnp.where(lane == 0, r1, jnp.where(lane == 1, r2, 0.0)).astype(I32)
    carry_ref[...] = carry_ref[...] + jnp.sum(onehot, axis=0, keepdims=True)
    cnt_ref[...] = carry_ref[...]


def _rank(eid):
    t = eid.shape[0]
    tm = RANK_TM
    return pl.pallas_call(
        _rank_kernel,
        grid=(t // tm,),
        in_specs=[pl.BlockSpec((tm, LANES), lambda i: (i, 0))],
        out_specs=[pl.BlockSpec((tm, LANES), lambda i: (i, 0)),
                   pl.BlockSpec((8, LANES), lambda i: (0, 0))],
        out_shape=[jax.ShapeDtypeStruct((t, LANES), I32),
                   jax.ShapeDtypeStruct((8, LANES), F32)],
        scratch_shapes=[pltpu.VMEM((8, LANES), F32)],
        compiler_params=_cparams(("arbitrary",)),
        name="expert_rank",
    )(eid)


def _dispatch_kernel(cnt_ref, pend_ref, nused_ref, dest_ref, hp_ref, xs_ref, zero_ref, sem, zsem):
    tm = hp_ref.shape[0]
    zb = zero_ref.shape[0]
    n_blocks = xs_ref.shape[0] // zb

    @pl.when(pl.program_id(0) == 0)
    def _():
        zero_ref[...] = jnp.zeros_like(zero_ref)

        def zero_copy(start):
            return pltpu.make_async_copy(zero_ref, xs_ref.at[pl.ds(pl.multiple_of(start, zb), zb)], zsem)

        def tail_start(j, carry):
            zero_copy(j * zb).start()
            return carry

        def tail_wait(j, carry):
            zero_copy(j * zb).wait()
            return carry

        for e in range(N_EXPERTS):
            @pl.when(cnt_ref[e] > 0)
            def _():
                zero_copy(pend_ref[e] - zb).start()
        lax.fori_loop(nused_ref[0], n_blocks, tail_start, 0)
        for e in range(N_EXPERTS):
            @pl.when(cnt_ref[e] > 0)
            def _():
                zero_copy(pend_ref[e] - zb).wait()
        lax.fori_loop(nused_ref[0], n_blocks, tail_wait, 0)

    def row_copy(r, d):
        return pltpu.make_async_copy(hp_ref.at[pl.ds(r, 1)], xs_ref.at[pl.ds(d, 1)], sem)

    def issue(r, carry):
        for k in range(TOP_K):
            row_copy(r, dest_ref[0, 0, TOP_K * r + k]).start()
        return carry

    def drain(r, carry):
        for k in range(TOP_K):
            row_copy(r, dest_ref[0, 0, TOP_K * r + k]).wait()
        return carry

    lax.fori_loop(0, tm, issue, 0)
    lax.fori_loop(0, tm, drain, 0)


def _dispatch(cnt, pends, nused, dest, hp, n_slots):
    t, w = hp.shape
    tm = ROW_TM
    dest3 = dest.reshape(t // tm, 1, TOP_K * tm)
    grid_spec = pltpu.PrefetchScalarGridSpec(
        num_scalar_prefetch=3,
        grid=(t // tm,),
        in_specs=[pl.BlockSpec((1, 1, TOP_K * tm), lambda i, *_: (i, 0, 0), memory_space=pltpu.SMEM),
                  pl.BlockSpec((tm, w), lambda i, *_: (i, 0))],
        out_specs=pl.BlockSpec(memory_space=pl.ANY),
        scratch_shapes=[pltpu.VMEM((MOE_TM, w), U32), pltpu.SemaphoreType.DMA(()),
                        pltpu.SemaphoreType.DMA(())],
    )
    return pl.pallas_call(
        _dispatch_kernel,
        grid_spec=grid_spec,
        out_shape=jax.ShapeDtypeStruct((n_slots, w), U32),
        compiler_params=_cparams(("arbitrary",)),
        name="moe_dispatch",
    )(cnt, pends, nused, dest3, hp)


def _moe_kernel(bexp_ref, nused_ref, xs_ref, wg_ref, wu_ref, wd_ref, o_ref):
    i = pl.program_id(0)

    @pl.when(i < nused_ref[0])
    def _():
        x = _unpack_halves(xs_ref[...])
        hg = jnp.dot(x, wg_ref[0], preferred_element_type=F32)
        hu = jnp.dot(x, wu_ref[0], preferred_element_type=F32)
        hdn = (hg * jax.nn.sigmoid(hg)) * hu
        o_ref[...] = jnp.dot(hdn.astype(BF16), wd_ref[0], preferred_element_type=F32)

    @pl.when(i >= nused_ref[0])
    def _():
        o_ref[...] = jnp.zeros_like(o_ref)


def _moe(bexp, nused, xs, wg, wu, wd):
    n_slots, w = xs.shape
    tm = MOE_TM
    n_blocks = n_slots // tm
    grid_spec = pltpu.PrefetchScalarGridSpec(
        num_scalar_prefetch=2,
        grid=(n_blocks,),
        in_specs=[pl.BlockSpec((tm, w), lambda i, be, nu: (jnp.minimum(i, nu[0] - 1), 0)),
                  pl.BlockSpec((1, D_MODEL, D_EXPERT), lambda i, be, nu: (be[i], 0, 0)),
                  pl.BlockSpec((1, D_MODEL, D_EXPERT), lambda i, be, nu: (be[i], 0, 0)),
                  pl.BlockSpec((1, D_EXPERT, D_MODEL), lambda i, be, nu: (be[i], 0, 0))],
        out_specs=pl.BlockSpec((tm, D_MODEL), lambda i, be, nu: (i, 0)),
    )
    return pl.pallas_call(
        _moe_kernel,
        grid_spec=grid_spec,
        out_shape=jax.ShapeDtypeStruct((n_slots, D_MODEL), F32),
        compiler_params=_cparams(("arbitrary",)),
        name="moe_experts",
    )(bexp, nused, xs, wg, wu, wd)


def _combine_kernel(dest_ref, x_ref, gate_ref, g_ref, ys_ref, o_ref, y1_ref, y2_ref, sem):
    tm = x_ref.shape[0]
    bufs = (y1_ref, y2_ref)

    def row_copy(r, k, d):
        return pltpu.make_async_copy(ys_ref.at[pl.ds(d, 1)], bufs[k].at[pl.ds(r, 1)], sem)

    def issue(r, carry):
        for k in range(TOP_K):
            row_copy(r, k, dest_ref[0, 0, TOP_K * r + k]).start()
        return carry

    def drain(r, carry):
        for k in range(TOP_K):
            row_copy(r, k, dest_ref[0, 0, TOP_K * r + k]).wait()
        return carry

    lax.fori_loop(0, tm, issue, 0)
    lax.fori_loop(0, tm, drain, 0)
    gates = gate_ref[...]
    moe = gates[:, 0:1] * y1_ref[...] + gates[:, 1:2] * y2_ref[...]
    o_ref[...] = _rms_scale(x_ref[...] + moe, g_ref[...], NORM_EPS)


def _combine(dest, x2, gates, g, ys, row0, n_rows):
    tm = ROW_TM
    t = x2.shape[0]
    dest3 = dest.reshape(t // tm, 1, TOP_K * tm)
    blk0 = row0 // tm
    return pl.pallas_call(
        _combine_kernel,
        grid=(n_rows // tm,),
        in_specs=[pl.BlockSpec((1, 1, TOP_K * tm), lambda i: (blk0 + i, 0, 0), memory_space=pltpu.SMEM),
                  pl.BlockSpec((tm, D_MODEL), lambda i: (blk0 + i, 0)),
                  pl.BlockSpec((tm, LANES), lambda i: (blk0 + i, 0)),
                  pl.BlockSpec((1, D_MODEL), lambda i: (0, 0)),
                  pl.BlockSpec(memory_space=pl.ANY)],
        out_specs=pl.BlockSpec((tm, D_MODEL), lambda i: (i, 0)),
        out_shape=jax.ShapeDtypeStruct((n_rows, D_MODEL), F32),
        scratch_shapes=[pltpu.VMEM((tm, D_MODEL), F32), pltpu.VMEM((tm, D_MODEL), F32),
                        pltpu.SemaphoreType.DMA(())],
        compiler_params=_cparams(("arbitrary",)),
        name="moe_combine",
    )(dest3, x2, gates, g, ys)


def _routing_plan(eid, rank, counts):
    cnt = counts[0, :N_EXPERTS].astype(I32)
    padded = (cnt + MOE_TM - 1) // MOE_TM * MOE_TM
    pends = jnp.cumsum(padded)
    pstarts = pends - padded
    dest = pstarts[eid[:, :TOP_K]] + rank[:, :TOP_K]
    t = eid.shape[0]
    n_blocks = t * TOP_K // MOE_TM + N_EXPERTS
    blk0 = jnp.arange(n_blocks, dtype=I32) * MOE_TM
    bexp = jnp.minimum(jnp.searchsorted(pends, blk0, side="right"), N_EXPERTS - 1).astype(I32)
    nused = (pends[-1:] // MOE_TM).astype(I32)
    return dest.reshape(-1).astype(I32), cnt, pends.astype(I32), bexp, nused, n_blocks * MOE_TM


def kernel(x_prompt, x_sample, mem_prompt, mem_sample, mix_norm_g, w_in, na_rpb, diff_lambda_q1, diff_lambda_k1, diff_lambda_q2, diff_lambda_k2, diff_subln_g, w_branch_na, w_branch_diff, w_out, xa_norm_g, mem_norm_g, xa_w_q, xa_w_kv, xa_w_o, ffn_norm_g, router_group_w, router_group_b, router_expert_w, router_expert_b, w_gate, w_up, w_down, final_norm_g):
    nb_p, nb_s = x_prompt.shape[0], x_sample.shape[0]
    nb = nb_p + nb_s
    t = nb * SEQ
    x = jnp.concatenate([x_prompt, x_sample], axis=0).reshape(t, D_MODEL)
    mem = jnp.concatenate([mem_prompt, mem_sample], axis=0).reshape(nb * MEM_LEN, D_MODEL)
    row = lambda v: v.reshape(1, -1).astype(F32)
    bf = lambda w: w.astype(BF16)

    proj = _inproj(x, row(mix_norm_g[0]), bf(w_in[0]))
    proj3 = proj.reshape(nb, SEQ, IN_COLS)
    a = _na_attention(proj3, _na_bias_table(na_rpb[0]))
    slopes = jnp.exp2(-8.0 * jnp.arange(1, DIFF_HEADS + 1, dtype=F32) / DIFF_HEADS)
    b = _diff_attention(proj3, slopes, row(diff_lambda_q1[0]), row(diff_lambda_k1[0]),
                        row(diff_lambda_q2[0]), row(diff_lambda_k2[0]), row(diff_subln_g[0]))
    x1 = _merge(a.reshape(t, NA_WIDTH), b.reshape(t, DIFF_V_WIDTH), proj, x,
                bf(w_branch_na[0]), bf(w_branch_diff[0]), bf(w_out[0]))
    kv = _memkv(mem, row(mem_norm_g[0]), bf(xa_w_kv[0]))
    w_router = jnp.zeros((D_MODEL, LANES), F32)
    w_router = w_router.at[:, :N_GROUPS].set(router_group_w[0]).at[:, N_GROUPS:N_GROUPS + N_EXPERTS].set(router_expert_w[0])
    b_router = jnp.zeros((1, LANES), F32)
    b_router = b_router.at[0, :N_GROUPS].set(router_group_b[0]).at[0, N_GROUPS:N_GROUPS + N_EXPERTS].set(router_expert_b[0])
    x2, hp, gates, eid = _xattn(x1, kv, row(xa_norm_g[0]), bf(xa_w_q[0]), bf(xa_w_o[0]),
                                row(ffn_norm_g[0]), bf(w_router), b_router)
    rank, counts = _rank(eid)
    dest, cnt, pends, bexp, nused, n_slots = _routing_plan(eid, rank, counts)
    xs = _dispatch(cnt, pends, nused, dest, hp, n_slots)
    ys = _moe(bexp, nused, xs, bf(w_gate[0]), bf(w_up[0]), bf(w_down[0]))
    fg = row(final_norm_g)
    y_p = _combine(dest, x2, gates, fg, ys, 0, nb_p * SEQ).reshape(nb_p, SEQ, D_MODEL)
    y_s = _combine(dest, x2, gates, fg, ys, nb_p * SEQ, nb_s * SEQ).reshape(nb_s, SEQ, D_MODEL)
    return (y_p, y_s)
```

```python
import functools
import math

import jax
import jax.numpy as jnp
from jax import lax
from jax.experimental import pallas as pl
from jax.experimental.pallas import tpu as pltpu

F32 = jnp.float32
BF16 = jnp.bfloat16
I32 = jnp.int32
U32 = jnp.uint32

D_MODEL = 2048
SEQ = 2048
GRID_W = 64
ROWS = SEQ // GRID_W
NA_HEADS = 8
NA_HEAD_DIM = 128
NA_WIDTH = NA_HEADS * NA_HEAD_DIM
NA_KH = 8
NA_KW = 16
DIFF_HEADS = 8
DIFF_QK_DIM = 64
DIFF_V_DIM = 128
DIFF_QK_WIDTH = DIFF_HEADS * 2 * DIFF_QK_DIM
DIFF_V_WIDTH = DIFF_HEADS * DIFF_V_DIM
IN_COLS = 3 * NA_WIDTH + 2 * DIFF_QK_WIDTH + DIFF_V_WIDTH + 2 * D_MODEL
MEM_LEN = 256
XA_HEADS = 4
XA_HEAD_DIM = 128
XA_WIDTH = XA_HEADS * XA_HEAD_DIM
N_GROUPS = 4
EXPERTS_PER_GROUP = 8
N_EXPERTS = N_GROUPS * EXPERTS_PER_GROUP
TOP_K = 2
D_EXPERT = 512
NORM_EPS = 1e-6
SUBLN_EPS = 1e-5
NEG_INF = -1e30
LAMBDA_INIT = 0.8 - 0.6 * math.exp(-0.3 * 0)

LANES = 128
VMEM_LIMIT = 56 * 1024 * 1024

INPROJ_TM = 1024
INPROJ_TN = 1024
NORM_CHUNK = 128
DIFF_TQ = 256
MERGE_TM = 256
XA_TM = 256
RANK_TM = 512
MOE_TM = 256
ROW_TM = 256


def _cparams(sem):
    return pltpu.CompilerParams(dimension_semantics=sem, vmem_limit_bytes=VMEM_LIMIT)


def _rms_scale(x, g, eps):
    ms = jnp.mean(x * x, axis=-1, keepdims=True)
    return x * lax.rsqrt(ms + eps) * g


def _dot_nt(a, b):
    return lax.dot_general(a, b, (((1,), (1,)), ((), ())), preferred_element_type=F32)


def _inproj_kernel(x_ref, g_ref, w_ref, o_ref, h_ref):
    @pl.when(pl.program_id(1) == 0)
    def _():
        def chunk(c, carry):
            rows = pl.ds(pl.multiple_of(c * NORM_CHUNK, NORM_CHUNK), NORM_CHUNK)
            h_ref[rows, :] = _rms_scale(x_ref[rows, :], g_ref[...], NORM_EPS).astype(BF16)
            return carry
        lax.fori_loop(0, x_ref.shape[0] // NORM_CHUNK, chunk, 0)

    o_ref[...] = jnp.dot(h_ref[...], w_ref[...], preferred_element_type=F32).astype(o_ref.dtype)


def _inproj(x, g, w):
    t, d = x.shape
    n = w.shape[1]
    tm = min(INPROJ_TM, t)
    return pl.pallas_call(
        _inproj_kernel,
        grid=(t // tm, n // INPROJ_TN),
        in_specs=[pl.BlockSpec((tm, d), lambda i, j: (i, 0)),
                  pl.BlockSpec((1, d), lambda i, j: (0, 0)),
                  pl.BlockSpec((d, INPROJ_TN), lambda i, j: (0, j))],
        out_specs=pl.BlockSpec((tm, INPROJ_TN), lambda i, j: (i, j)),
        out_shape=jax.ShapeDtypeStruct((t, n), BF16),
        scratch_shapes=[pltpu.VMEM((tm, d), BF16)],
        compiler_params=_cparams(("parallel", "arbitrary")),
        name="inproj",
    )(x, g, w)


NA_QROWS = 4
NA_KROWS = 12
NA_TILES = ROWS // NA_QROWS


def _na_tile_key_row(ti):
    return min(max(NA_QROWS * ti - NA_KH // 2, 0), ROWS - NA_KROWS)


def _na_bias_table(rpb):
    cols = jnp.arange(GRID_W)
    cstart = jnp.clip(cols - NA_KW // 2, 0, GRID_W - NA_KW)
    col_ok = (cols[None, :] >= cstart[:, None]) & (cols[None, :] < cstart[:, None] + NA_KW)
    dc = jnp.clip(cols[None, :] - cols[:, None], 1 - NA_KW, NA_KW - 1) + NA_KW - 1
    pick = (dc[None] == jnp.arange(2 * NA_KW - 1)[:, None, None]).astype(F32)
    t = jnp.einsum("hdx,xck->hdck", rpb.astype(F32), pick, precision=lax.Precision.HIGHEST)
    t = jnp.where(col_ok[None, None], t, NEG_INF)
    tiles = jnp.array([0, 1, NA_TILES - 1])
    r = NA_QROWS * tiles[:, None, None] + jnp.arange(NA_QROWS)[None, :, None]
    ks = jnp.clip(NA_QROWS * tiles - NA_KH // 2, 0, ROWS - NA_KROWS)[:, None, None]
    kr = ks + jnp.arange(NA_KROWS)[None, None, :]
    rs = jnp.clip(r - NA_KH // 2, 0, ROWS - NA_KH)
    valid = (kr >= rs) & (kr < rs + NA_KH)
    dr = jnp.clip(kr - r + NA_KH - 1, 0, 2 * NA_KH - 2)
    b = jnp.where(valid[None, :, :, :, None, None], t[:, dr], NEG_INF)
    return b.transpose(0, 1, 2, 4, 3, 5).reshape(NA_HEADS, 3, NA_QROWS * GRID_W, NA_KROWS * GRID_W)


def _na_kernel(q_ref, k_ref, v_ref, bias_ref, o_ref):
    scale = NA_HEAD_DIM ** -0.5
    tq = NA_QROWS * GRID_W
    tk = NA_KROWS * GRID_W
    for ti in range(NA_TILES):
        variant = 0 if ti == 0 else (2 if ti == NA_TILES - 1 else 1)
        k0 = _na_tile_key_row(ti) * GRID_W
        qrows = slice(ti * tq, (ti + 1) * tq)
        krows = slice(k0, k0 + tk)
        s = _dot_nt(q_ref[0, qrows, :], k_ref[0, krows, :]) * scale + bias_ref[0, variant]
        m = jnp.max(s, axis=-1, keepdims=True)
        e = jnp.exp(s - m)
        l = jnp.sum(e, axis=-1, keepdims=True)
        o = jnp.dot(e.astype(BF16), v_ref[0, krows, :], preferred_element_type=F32)
        o_ref[0, qrows, :] = (o / l).astype(o_ref.dtype)


def _na_attention(proj3, bias):
    b = proj3.shape[0]
    hd = NA_HEAD_DIM
    return pl.pallas_call(
        _na_kernel,
        grid=(NA_HEADS, b),
        in_specs=[pl.BlockSpec((1, SEQ, hd), lambda h, i: (i, 0, h)),
                  pl.BlockSpec((1, SEQ, hd), lambda h, i: (i, 0, NA_HEADS + h)),
                  pl.BlockSpec((1, SEQ, hd), lambda h, i: (i, 0, 2 * NA_HEADS + h)),
                  pl.BlockSpec((1,) + bias.shape[1:], lambda h, i: (h, 0, 0, 0))],
        out_specs=pl.BlockSpec((1, SEQ, hd), lambda h, i: (i, 0, h)),
        out_shape=jax.ShapeDtypeStruct((b, SEQ, NA_WIDTH), BF16),
        compiler_params=_cparams(("arbitrary", "arbitrary")),
        name="na_attn",
    )(proj3, proj3, proj3, bias)


def _alibi_table(tq):
    slopes = jnp.exp2(-8.0 * jnp.arange(1, DIFF_HEADS + 1, dtype=F32) / DIFF_HEADS)
    r = jnp.arange(tq)[:, None]
    x = jnp.arange(2 * SEQ - tq)[None, :]
    dist = jnp.abs(r - x + (SEQ - tq)).astype(F32)
    return slopes[:, None, None] * dist[None]


def _diff_kernel(lq1_ref, lk1_ref, lq2_ref, lk2_ref, q_ref, k_ref, v_ref, g_ref, alibi_ref, o_ref):
    qi = pl.program_id(2)
    tq = q_ref.shape[1]
    lam = (jnp.exp(jnp.sum(lq1_ref[...] * lk1_ref[...], axis=-1, keepdims=True))
           - jnp.exp(jnp.sum(lq2_ref[...] * lk2_ref[...], axis=-1, keepdims=True)) + LAMBDA_INIT)
    q = q_ref[0] * (DIFF_QK_DIM ** -0.5)
    k = k_ref[0]
    bias = alibi_ref[0, :, pl.ds(pl.multiple_of((SEQ - tq) - qi * tq, tq), SEQ)]

    def expo(mi):
        sl = slice(mi * DIFF_QK_DIM, (mi + 1) * DIFF_QK_DIM)
        s = _dot_nt(q[:, sl], k[:, sl]) - bias
        m = jnp.max(s, axis=-1, keepdims=True)
        e = jnp.exp(s - m)
        return e, jnp.sum(e, axis=-1, keepdims=True)

    e0, l0 = expo(0)
    e1, l1 = expo(1)
    a = (e0 - e1 * (lam * l0 / l1)).astype(BF16)
    o = jnp.dot(a, v_ref[0], preferred_element_type=F32) / l0
    o = _rms_scale(o, g_ref[...], SUBLN_EPS) * (1.0 - LAMBDA_INIT)
    o_ref[0] = o.astype(o_ref.dtype)


def _diff_attention(proj3, lq1, lk1, lq2, lk2, subln_g):
    b = proj3.shape[0]
    tq = DIFF_TQ
    qoff = 3 * NA_WIDTH // LANES
    koff = qoff + DIFF_QK_WIDTH // LANES
    voff = koff + DIFF_QK_WIDTH // LANES
    vec = lambda n: pl.BlockSpec((1, n), lambda h, i, j: (0, 0))
    return pl.pallas_call(
        _diff_kernel,
        grid=(DIFF_HEADS, b, SEQ // tq),
        in_specs=[vec(DIFF_QK_DIM), vec(DIFF_QK_DIM), vec(DIFF_QK_DIM), vec(DIFF_QK_DIM),
                  pl.BlockSpec((1, tq, LANES), lambda h, i, j: (i, j, qoff + h)),
                  pl.BlockSpec((1, SEQ, LANES), lambda h, i, j: (i, 0, koff + h)),
                  pl.BlockSpec((1, SEQ, LANES), lambda h, i, j: (i, 0, voff + h)),
                  vec(DIFF_V_DIM),
                  pl.BlockSpec((1, tq, 2 * SEQ - tq), lambda h, i, j: (h, 0, 0))],
        out_specs=pl.BlockSpec((1, tq, DIFF_V_DIM), lambda h, i, j: (i, j, h)),
        out_shape=jax.ShapeDtypeStruct((b, SEQ, DIFF_V_WIDTH), BF16),
        compiler_params=_cparams(("arbitrary", "arbitrary", "arbitrary")),
        name="diff_attn",
    )(lq1, lk1, lq2, lk2, proj3, proj3, proj3, subln_g, _alibi_table(tq))


def _merge_kernel(a_ref, b_ref, gna_ref, gdf_ref, x_ref, wna_ref, wdf_ref, wout_ref, o_ref):
    pa = jnp.dot(a_ref[...], wna_ref[...], preferred_element_type=F32)
    pb = jnp.dot(b_ref[...], wdf_ref[...], preferred_element_type=F32)
    merged = (jax.nn.sigmoid(gna_ref[...].astype(F32)) * pa
              + jax.nn.sigmoid(gdf_ref[...].astype(F32)) * pb)
    o_ref[...] = x_ref[...] + jnp.dot(merged.astype(BF16), wout_ref[...], preferred_element_type=F32)


def _merge(a, b, proj, x, wna, wdf, wout):
    t = x.shape[0]
    tm = MERGE_TM
    gna_blk = (3 * NA_WIDTH + 2 * DIFF_QK_WIDTH + DIFF_V_WIDTH) // D_MODEL
    const = lambda shape: pl.BlockSpec(shape, lambda i: (0, 0))
    return pl.pallas_call(
        _merge_kernel,
        grid=(t // tm,),
        in_specs=[pl.BlockSpec((tm, NA_WIDTH), lambda i: (i, 0)),
                  pl.BlockSpec((tm, DIFF_V_WIDTH), lambda i: (i, 0)),
                  pl.BlockSpec((tm, D_MODEL), lambda i: (i, gna_blk)),
                  pl.BlockSpec((tm, D_MODEL), lambda i: (i, gna_blk + 1)),
                  pl.BlockSpec((tm, D_MODEL), lambda i: (i, 0)),
                  const((NA_WIDTH, D_MODEL)), const((DIFF_V_WIDTH, D_MODEL)), const((D_MODEL, D_MODEL))],
        out_specs=pl.BlockSpec((tm, D_MODEL), lambda i: (i, 0)),
        out_shape=jax.ShapeDtypeStruct((t, D_MODEL), F32),
        compiler_params=_cparams(("parallel",)),
        name="merge_outproj",
    )(a, b, proj, proj, x, wna, wdf, wout)


def _memkv_kernel(m_ref, g_ref, w_ref, o_ref):
    h = _rms_scale(m_ref[...], g_ref[...], NORM_EPS).astype(BF16)
    o_ref[...] = jnp.dot(h, w_ref[...], preferred_element_type=F32).astype(o_ref.dtype)


def _memkv(mem, g, w):
    t = mem.shape[0]
    return pl.pallas_call(
        _memkv_kernel,
        grid=(t // MEM_LEN,),
        in_specs=[pl.BlockSpec((MEM_LEN, D_MODEL), lambda i: (i, 0)),
                  pl.BlockSpec((1, D_MODEL), lambda i: (0, 0)),
                  pl.BlockSpec((D_MODEL, 2 * XA_WIDTH), lambda i: (0, 0))],
        out_specs=pl.BlockSpec((MEM_LEN, 2 * XA_WIDTH), lambda i: (i, 0)),
        out_shape=jax.ShapeDtypeStruct((t, 2 * XA_WIDTH), BF16),
        compiler_params=_cparams(("parallel",)),
        name="mem_kv",
    )(mem, g, w)


def _route(logits):
    lane = lax.broadcasted_iota(I32, logits.shape, 1).astype(F32)
    ninf = -jnp.inf
    big = float(LANES)
    gl = jnp.where(lane < N_GROUPS, logits, ninf)
    gmax = jnp.max(gl, axis=-1, keepdims=True)
    g = jnp.min(jnp.where(gl == gmax, lane, big), axis=-1, keepdims=True)
    pg = 1.0 / jnp.sum(jnp.exp(gl - gmax), axis=-1, keepdims=True)
    lo = N_GROUPS + EXPERTS_PER_GROUP * g
    el = jnp.where((lane >= lo) & (lane < lo + EXPERTS_PER_GROUP), logits, ninf)
    v1 = jnp.max(el, axis=-1, keepdims=True)
    i1 = jnp.min(jnp.where(el == v1, lane, big), axis=-1, keepdims=True)
    el2 = jnp.where(lane == i1, ninf, el)
    v2 = jnp.max(el2, axis=-1, keepdims=True)
    i2 = jnp.min(jnp.where(el2 == v2, lane, big), axis=-1, keepdims=True)
    t = jnp.exp(v2 - v1)
    den = 1.0 + t
    gate1 = pg * (1.0 / den)
    gate2 = pg * (t / den)
    gates = jnp.where(lane == 0.0, gate1, jnp.where(lane == 1.0, gate2, 0.0))
    eids = jnp.where(lane == 0.0, i1 - N_GROUPS, jnp.where(lane == 1.0, i2 - N_GROUPS, 0.0)).astype(I32)
    return gates, eids


def _pack_halves(h):
    half = h.shape[1] // 2
    hi = pltpu.bitcast(h[:, :half].astype(BF16).astype(F32), U32)
    lo = pltpu.bitcast(h[:, half:].astype(BF16).astype(F32), U32)
    return hi | (lo >> 16)


def _unpack_halves(w):
    hi = pltpu.bitcast(w & jnp.uint32(0xFFFF0000), F32).astype(BF16)
    lo = pltpu.bitcast(w << 16, F32).astype(BF16)
    return jnp.concatenate([hi, lo], axis=1)


def _xattn_kernel(x_ref, kv_ref, gx_ref, wq_ref, wo_ref, gf_ref, wr_ref, br_ref,
                  x2_ref, hp_ref, gate_ref, eid_ref):
    x1 = x_ref[...]
    hq = _rms_scale(x1, gx_ref[...], NORM_EPS).astype(BF16)
    q = jnp.dot(hq, wq_ref[...], preferred_element_type=F32).astype(BF16)
    scale = XA_HEAD_DIM ** -0.5
    outs = []
    for h in range(XA_HEADS):
        sl = slice(h * XA_HEAD_DIM, (h + 1) * XA_HEAD_DIM)
        vsl = slice(XA_WIDTH + h * XA_HEAD_DIM, XA_WIDTH + (h + 1) * XA_HEAD_DIM)
        s = _dot_nt(q[:, sl], kv_ref[:, sl]) * scale
        m = jnp.max(s, axis=-1, keepdims=True)
        e = jnp.exp(s - m)
        l = jnp.sum(e, axis=-1, keepdims=True)
        o = jnp.dot(e.astype(BF16), kv_ref[:, vsl], preferred_element_type=F32)
        outs.append((o / l).astype(BF16))
    o = jnp.concatenate(outs, axis=1)
    x2 = x1 + jnp.dot(o, wo_ref[...], preferred_element_type=F32)
    x2_ref[...] = x2
    h3 = _rms_scale(x2, gf_ref[...], NORM_EPS)
    hp_ref[...] = _pack_halves(h3)
    logits = jnp.dot(h3.astype(BF16), wr_ref[...], preferred_element_type=F32) + br_ref[...]
    gates, eids = _route(logits)
    gate_ref[...] = gates
    eid_ref[...] = eids


def _xattn(x1, kv, gx, wq, wo, gf, wr, br):
    t = x1.shape[0]
    tm = XA_TM
    per_batch = SEQ // tm
    const = lambda shape: pl.BlockSpec(shape, lambda i: (0, 0))
    tile = lambda n: pl.BlockSpec((tm, n), lambda i: (i, 0))
    return pl.pallas_call(
        _xattn_kernel,
        grid=(t // tm,),
        in_specs=[tile(D_MODEL),
                  pl.BlockSpec((MEM_LEN, 2 * XA_WIDTH), lambda i: (i // per_batch, 0)),
                  const((1, D_MODEL)), const((D_MODEL, XA_WIDTH)), const((XA_WIDTH, D_MODEL)),
                  const((1, D_MODEL)), const((D_MODEL, LANES)), const((1, LANES))],
        out_specs=[tile(D_MODEL), tile(D_MODEL // 2), tile(LANES), tile(LANES)],
        out_shape=[jax.ShapeDtypeStruct((t, D_MODEL), F32),
                   jax.ShapeDtypeStruct((t, D_MODEL // 2), U32),
                   jax.ShapeDtypeStruct((t, LANES), F32),
                   jax.ShapeDtypeStruct((t, LANES), I32)],
        compiler_params=_cparams(("parallel",)),
        name="xattn_router",
    )(x1, kv, gx, wq, wo, gf, wr, br)


def _rank_kernel(eid_ref, rank_ref, cnt_ref, carry_ref):
    @pl.when(pl.program_id(0) == 0)
    def _():
        carry_ref[...] = jnp.zeros_like(carry_ref)

    tm = eid_ref.shape[0]
    eid = eid_ref[...]
    lane = lax.broadcasted_iota(I32, (tm, LANES), 1)
    m1 = lane == eid[:, 0:1]
    m2 = lane == eid[:, 1:2]
    onehot = jnp.where(m1, 1.0, jnp.where(m2, 1.0, 0.0))
    r = lax.broadcasted_iota(I32, (tm, tm), 0)
    c = lax.broadcasted_iota(I32, (tm, tm), 1)
    tri = jnp.where(c < r, 1.0, 0.0).astype(BF16)
    before = jnp.dot(tri, onehot.astype(BF16), preferred_element_type=F32) + carry_ref[0:1, :]
    r1 = jnp.sum(jnp.where(m1, before, 0.0), axis=-1, keepdims=True)
    r2 = jnp.sum(jnp.where(m2, before, 0.0), axis=-1, keepdims=True)
    rank_ref[...] = jnp.where(lane == 0, r1, jnp.where(lane == 1, r2, 0.0)).astype(I32)
    carry_ref[...] = carry_ref[...] + jnp.sum(onehot, axis=0, keepdims=True)
    cnt_ref[...] = carry_ref[...]


def _rank(eid):
    t = eid.shape[0]
    tm = RANK_TM
    return pl.pallas_call(
        _rank_kernel,
        grid=(t // tm,),
        in_specs=[pl.BlockSpec((tm, LANES), lambda i: (i, 0))],
        out_specs=[pl.BlockSpec((tm, LANES), lambda i: (i, 0)),
                   pl.BlockSpec((8, LANES), lambda i: (0, 0))],
        out_shape=[jax.ShapeDtypeStruct((t, LANES), I32),
                   jax.ShapeDtypeStruct((8, LANES), F32)],
        scratch_shapes=[pltpu.VMEM((8, LANES), F32)],
        compiler_params=_cparams(("arbitrary",)),
        name="expert_rank",
    )(eid)


def _dispatch_kernel(cnt_ref, pend_ref, nused_ref, dest_ref, hp_ref, xs_ref, zero_ref, sem, zsem):
    tm = hp_ref.shape[0]
    zb = zero_ref.shape[0]
    n_blocks = xs_ref.shape[0] // zb

    @pl.when(pl.program_id(0) == 0)
    def _():
        zero_ref[...] = jnp.zeros_like(zero_ref)

        def zero_copy(start):
            return pltpu.make_async_copy(zero_ref, xs_ref.at[pl.ds(pl.multiple_of(start, zb), zb)], zsem)

        def tail_start(j, carry):
            zero_copy(j * zb).start()
            return carry

        def tail_wait(j, carry):
            zero_copy(j * zb).wait()
            return carry

        for e in range(N_EXPERTS):
            @pl.when(cnt_ref[e] > 0)
            def _():
                zero_copy(pend_ref[e] - zb).start()
        lax.fori_loop(nused_ref[0], n_blocks, tail_start, 0)
        for e in range(N_EXPERTS):
            @pl.when(cnt_ref[e] > 0)
            def _():
                zero_copy(pend_ref[e] - zb).wait()
        lax.fori_loop(nused_ref[0], n_blocks, tail_wait, 0)

    def row_copy(r, d):
        return pltpu.make_async_copy(hp_ref.at[pl.ds(r, 1)], xs_ref.at[pl.ds(d, 1)], sem)

    def issue(r, carry):
        for k in range(TOP_K):
            row_copy(r, dest_ref[0, 0, TOP_K * r + k]).start()
        return carry

    def drain(r, carry):
        for k in range(TOP_K):
            row_copy(r, dest_ref[0, 0, TOP_K * r + k]).wait()
        return carry

    lax.fori_loop(0, tm, issue, 0)
    lax.fori_loop(0, tm, drain, 0)


def _dispatch(cnt, pends, nused, dest, hp, n_slots):
    t, w = hp.shape
    tm = ROW_TM
    dest3 = dest.reshape(t // tm, 1, TOP_K * tm)
    grid_spec = pltpu.PrefetchScalarGridSpec(
        num_scalar_prefetch=3,
        grid=(t // tm,),
        in_specs=[pl.BlockSpec((1, 1, TOP_K * tm), lambda i, *_: (i, 0, 0), memory_space=pltpu.SMEM),
                  pl.BlockSpec((tm, w), lambda i, *_: (i, 0))],
        out_specs=pl.BlockSpec(memory_space=pl.ANY),
        scratch_shapes=[pltpu.VMEM((MOE_TM, w), U32), pltpu.SemaphoreType.DMA(()),
                        pltpu.SemaphoreType.DMA(())],
    )
    return pl.pallas_call(
        _dispatch_kernel,
        grid_spec=grid_spec,
        out_shape=jax.ShapeDtypeStruct((n_slots, w), U32),
        compiler_params=_cparams(("arbitrary",)),
        name="moe_dispatch",
    )(cnt, pends, nused, dest3, hp)


def _moe_kernel(bexp_ref, nused_ref, xs_ref, wg_ref, wu_ref, wd_ref, o_ref):
    i = pl.program_id(0)

    @pl.when(i < nused_ref[0])
    def _():
        x = _unpack_halves(xs_ref[...])
        hg = jnp.dot(x, wg_ref[0], preferred_element_type=F32)
        hu = jnp.dot(x, wu_ref[0], preferred_element_type=F32)
        hdn = (hg * jax.nn.sigmoid(hg)) * hu
        o_ref[...] = jnp.dot(hdn.astype(BF16), wd_ref[0], preferred_element_type=F32)

    @pl.when(i >= nused_ref[0])
    def _():
        o_ref[...] = jnp.zeros_like(o_ref)


def _moe(bexp, nused, xs, wg, wu, wd):
    n_slots, w = xs.shape
    tm = MOE_TM
    n_blocks = n_slots // tm
    grid_spec = pltpu.PrefetchScalarGridSpec(
        num_scalar_prefetch=2,
        grid=(n_blocks,),
        in_specs=[pl.BlockSpec((tm, w), lambda i, be, nu: (jnp.minimum(i, nu[0] - 1), 0)),
                  pl.BlockSpec((1, D_MODEL, D_EXPERT), lambda i, be, nu: (be[i], 0, 0)),
                  pl.BlockSpec((1, D_MODEL, D_EXPERT), lambda i, be, nu: (be[i], 0, 0)),
                  pl.BlockSpec((1, D_EXPERT, D_MODEL), lambda i, be, nu: (be[i], 0, 0))],
        out_specs=pl.BlockSpec((tm, D_MODEL), lambda i, be, nu: (i, 0)),
    )
    return pl.pallas_call(
        _moe_kernel,
        grid_spec=grid_spec,
        out_shape=jax.ShapeDtypeStruct((n_slots, D_MODEL), F32),
        compiler_params=_cparams(("arbitrary",)),
        name="moe_experts",
    )(bexp, nused, xs, wg, wu, wd)


def _combine_kernel(dest_ref, x_ref, gate_ref, g_ref, ys_ref, o_ref, y1_ref, y2_ref, sem):
    tm = x_ref.shape[0]
    bufs = (y1_ref, y2_ref)

    def row_copy(r, k, d):
        return pltpu.make_async_copy(ys_ref.at[pl.ds(d, 1)], bufs[k].at[pl.ds(r, 1)], sem)

    def issue(r, carry):
        for k in range(TOP_K):
            row_copy(r, k, dest_ref[0, 0, TOP_K * r + k]).start()
        return carry

    def drain(r, carry):
        for k in range(TOP_K):
            row_copy(r, k, dest_ref[0, 0, TOP_K * r + k]).wait()
        return carry

    lax.fori_loop(0, tm, issue, 0)
    lax.fori_loop(0, tm, drain, 0)
    gates = gate_ref[...]
    moe = gates[:, 0:1] * y1_ref[...] + gates[:, 1:2] * y2_ref[...]
    o_ref[...] = _rms_scale(x_ref[...] + moe, g_ref[...], NORM_EPS)


def _combine(dest, x2, gates, g, ys, row0, n_rows):
    tm = ROW_TM
    t = x2.shape[0]
    dest3 = dest.reshape(t // tm, 1, TOP_K * tm)
    blk0 = row0 // tm
    return pl.pallas_call(
        _combine_kernel,
        grid=(n_rows // tm,),
        in_specs=[pl.BlockSpec((1, 1, TOP_K * tm), lambda i: (blk0 + i, 0, 0), memory_space=pltpu.SMEM),
                  pl.BlockSpec((tm, D_MODEL), lambda i: (blk0 + i, 0)),
                  pl.BlockSpec((tm, LANES), lambda i: (blk0 + i, 0)),
                  pl.BlockSpec((1, D_MODEL), lambda i: (0, 0)),
                  pl.BlockSpec(memory_space=pl.ANY)],
        out_specs=pl.BlockSpec((tm, D_MODEL), lambda i: (i, 0)),
        out_shape=jax.ShapeDtypeStruct((n_rows, D_MODEL), F32),
        scratch_shapes=[pltpu.VMEM((tm, D_MODEL), F32), pltpu.VMEM((tm, D_MODEL), F32),
                        pltpu.SemaphoreType.DMA(())],
        compiler_params=_cparams(("arbitrary",)),
        name="moe_combine",
    )(dest3, x2, gates, g, ys)


def _routing_plan(eid, rank, counts):
    cnt = counts[0, :N_EXPERTS].astype(I32)
    padded = (cnt + MOE_TM - 1) // MOE_TM * MOE_TM
    pends = jnp.cumsum(padded)
    pstarts = pends - padded
    dest = pstarts[eid[:, :TOP_K]] + rank[:, :TOP_K]
    t = eid.shape[0]
    n_blocks = t * TOP_K // MOE_TM + N_EXPERTS
    blk0 = jnp.arange(n_blocks, dtype=I32) * MOE_TM
    bexp = jnp.minimum(jnp.sum(pends[None, :] <= blk0[:, None], axis=1), N_EXPERTS - 1).astype(I32)
    nused = (pends[-1:] // MOE_TM).astype(I32)
    return dest.reshape(-1).astype(I32), cnt, pends.astype(I32), bexp, nused, n_blocks * MOE_TM


def kernel(x_prompt, x_sample, mem_prompt, mem_sample, mix_norm_g, w_in, na_rpb, diff_lambda_q1, diff_lambda_k1, diff_lambda_q2, diff_lambda_k2, diff_subln_g, w_branch_na, w_branch_diff, w_out, xa_norm_g, mem_norm_g, xa_w_q, xa_w_kv, xa_w_o, ffn_norm_g, router_group_w, router_group_b, router_expert_w, router_expert_b, w_gate, w_up, w_down, final_norm_g):
    nb_p, nb_s = x_prompt.shape[0], x_sample.shape[0]
    nb = nb_p + nb_s
    t = nb * SEQ
    x = jnp.concatenate([x_prompt, x_sample], axis=0).reshape(t, D_MODEL)
    mem = jnp.concatenate([mem_prompt, mem_sample], axis=0).reshape(nb * MEM_LEN, D_MODEL)
    row = lambda v: v.reshape(1, -1).astype(F32)
    bf = lambda w: w.astype(BF16)

    proj = _inproj(x, row(mix_norm_g[0]), bf(w_in[0]))
    proj3 = proj.reshape(nb, SEQ, IN_COLS)
    a = _na_attention(proj3, _na_bias_table(na_rpb[0]))
    b = _diff_attention(proj3, row(diff_lambda_q1[0]), row(diff_lambda_k1[0]),
                        row(diff_lambda_q2[0]), row(diff_lambda_k2[0]), row(diff_subln_g[0]))
    x1 = _merge(a.reshape(t, NA_WIDTH), b.reshape(t, DIFF_V_WIDTH), proj, x,
                bf(w_branch_na[0]), bf(w_branch_diff[0]), bf(w_out[0]))
    kv = _memkv(mem, row(mem_norm_g[0]), bf(xa_w_kv[0]))
    w_router = jnp.zeros((D_MODEL, LANES), F32)
    w_router = w_router.at[:, :N_GROUPS].set(router_group_w[0]).at[:, N_GROUPS:N_GROUPS + N_EXPERTS].set(router_expert_w[0])
    b_router = jnp.zeros((1, LANES), F32)
    b_router = b_router.at[0, :N_GROUPS].set(router_group_b[0]).at[0, N_GROUPS:N_GROUPS + N_EXPERTS].set(router_expert_b[0])
    x2, hp, gates, eid = _xattn(x1, kv, row(xa_norm_g[0]), bf(xa_w_q[0]), bf(xa_w_o[0]),
                                row(ffn_norm_g[0]), bf(w_router), b_router)
    rank, counts = _rank(eid)
    dest, cnt, pends, bexp, nused, n_slots = _routing_plan(eid, rank, counts)
    xs = _dispatch(cnt, pends, nused, dest, hp, n_slots)
    ys = _moe(bexp, nused, xs, bf(w_gate[0]), bf(w_up[0]), bf(w_down[0]))
    fg = row(final_norm_g)
    y_p = _combine(dest, x2, gates, fg, ys, 0, nb_p * SEQ).reshape(nb_p, SEQ, D_MODEL)
    y_s = _combine(dest, x2, gates, fg, ys, nb_p * SEQ, nb_s * SEQ).reshape(nb_s, SEQ, D_MODEL)
    return (y_p, y_s)
```

```python
import functools
import math

import jax
import jax.numpy as jnp
from jax import lax
from jax.experimental import pallas as pl
from jax.experimental.pallas import tpu as pltpu

F32 = jnp.float32
BF16 = jnp.bfloat16
I32 = jnp.int32
U32 = jnp.uint32

D_MODEL = 2048
SEQ = 2048
GRID_W = 64
ROWS = SEQ // GRID_W
NA_HEADS = 8
NA_HEAD_DIM = 128
NA_WIDTH = NA_HEADS * NA_HEAD_DIM
NA_KH = 8
NA_KW = 16
DIFF_HEADS = 8
DIFF_QK_DIM = 64
DIFF_V_DIM = 128
DIFF_QK_WIDTH = DIFF_HEADS * 2 * DIFF_QK_DIM
DIFF_V_WIDTH = DIFF_HEADS * DIFF_V_DIM
IN_COLS = 3 * NA_WIDTH + 2 * DIFF_QK_WIDTH + DIFF_V_WIDTH + 2 * D_MODEL
MEM_LEN = 256
XA_HEADS = 4
XA_HEAD_DIM = 128
XA_WIDTH = XA_HEADS * XA_HEAD_DIM
N_GROUPS = 4
EXPERTS_PER_GROUP = 8
N_EXPERTS = N_GROUPS * EXPERTS_PER_GROUP
TOP_K = 2
D_EXPERT = 512
NORM_EPS = 1e-6
SUBLN_EPS = 1e-5
NEG_INF = -1e30
LAMBDA_INIT = 0.8 - 0.6 * math.exp(-0.3 * 0)

LANES = 128
VMEM_LIMIT = 56 * 1024 * 1024

INPROJ_TM = 1024
INPROJ_TN = 1024
NORM_CHUNK = 128
DIFF_TQ = 256
MERGE_TM = 256
XA_TM = 256
RANK_TM = 512
MOE_TM = 256
ROW_TM = 256


def _cparams(sem):
    return pltpu.CompilerParams(dimension_semantics=sem, vmem_limit_bytes=VMEM_LIMIT)


def _rms_scale(x, g, eps):
    ms = jnp.mean(x * x, axis=-1, keepdims=True)
    return x * lax.rsqrt(ms + eps) * g


def _dot_nt(a, b):
    return lax.dot_general(a, b, (((1,), (1,)), ((), ())), preferred_element_type=F32)


def _two_group_specs(tm, width, n_first):
    first = pl.BlockSpec((tm, width), lambda i, *_: (jnp.minimum(i, n_first - 1), 0))
    second = pl.BlockSpec((tm, width), lambda i, *_: (jnp.maximum(i - n_first, 0), 0))
    return first, second


def _inproj_kernel(n_first, xp_ref, xs_ref, g_ref, w_ref, o_ref, h_ref):
    def normalise(x_ref):
        def chunk(c, carry):
            rows = pl.ds(pl.multiple_of(c * NORM_CHUNK, NORM_CHUNK), NORM_CHUNK)
            h_ref[rows, :] = _rms_scale(x_ref[rows, :], g_ref[...], NORM_EPS).astype(BF16)
            return carry
        lax.fori_loop(0, x_ref.shape[0] // NORM_CHUNK, chunk, 0)

    first_col = pl.program_id(1) == 0
    in_first = pl.program_id(0) < n_first
    pl.when(first_col & in_first)(lambda: normalise(xp_ref))
    pl.when(first_col & jnp.logical_not(in_first))(lambda: normalise(xs_ref))
    o_ref[...] = jnp.dot(h_ref[...], w_ref[...], preferred_element_type=F32).astype(o_ref.dtype)


def _inproj(xp, xs, g, w):
    d = xp.shape[1]
    t = xp.shape[0] + xs.shape[0]
    n = w.shape[1]
    tm = INPROJ_TM
    n_first = xp.shape[0] // tm
    spec_p, spec_s = _two_group_specs(tm, d, n_first)
    return pl.pallas_call(
        functools.partial(_inproj_kernel, n_first),
        grid=(t // tm, n // INPROJ_TN),
        in_specs=[spec_p, spec_s,
                  pl.BlockSpec((1, d), lambda i, j: (0, 0)),
                  pl.BlockSpec((d, INPROJ_TN), lambda i, j: (0, j))],
        out_specs=pl.BlockSpec((tm, INPROJ_TN), lambda i, j: (i, j)),
        out_shape=jax.ShapeDtypeStruct((t, n), BF16),
        scratch_shapes=[pltpu.VMEM((tm, d), BF16)],
        compiler_params=_cparams(("parallel", "arbitrary")),
        name="inproj",
    )(xp, xs, g, w)


NA_QROWS = 4
NA_KROWS = 12
NA_TILES = ROWS // NA_QROWS


def _na_tile_key_row(ti):
    return min(max(NA_QROWS * ti - NA_KH // 2, 0), ROWS - NA_KROWS)


def _na_bias_table(rpb):
    cols = jnp.arange(GRID_W)
    cstart = jnp.clip(cols - NA_KW // 2, 0, GRID_W - NA_KW)
    col_ok = (cols[None, :] >= cstart[:, None]) & (cols[None, :] < cstart[:, None] + NA_KW)
    dc = jnp.clip(cols[None, :] - cols[:, None], 1 - NA_KW, NA_KW - 1) + NA_KW - 1
    pick = (dc[None] == jnp.arange(2 * NA_KW - 1)[:, None, None]).astype(F32)
    t = jnp.einsum("hdx,xck->hdck", rpb.astype(F32), pick, precision=lax.Precision.HIGHEST)
    t = jnp.where(col_ok[None, None], t, NEG_INF)
    tiles = jnp.array([0, 1, NA_TILES - 1])
    r = NA_QROWS * tiles[:, None, None] + jnp.arange(NA_QROWS)[None, :, None]
    ks = jnp.clip(NA_QROWS * tiles - NA_KH // 2, 0, ROWS - NA_KROWS)[:, None, None]
    kr = ks + jnp.arange(NA_KROWS)[None, None, :]
    rs = jnp.clip(r - NA_KH // 2, 0, ROWS - NA_KH)
    valid = (kr >= rs) & (kr < rs + NA_KH)
    dr = jnp.clip(kr - r + NA_KH - 1, 0, 2 * NA_KH - 2)
    b = jnp.where(valid[None, :, :, :, None, None], t[:, dr], NEG_INF)
    return b.transpose(0, 1, 2, 4, 3, 5).reshape(NA_HEADS, 3, NA_QROWS * GRID_W, NA_KROWS * GRID_W)


def _na_kernel(q_ref, k_ref, v_ref, bias_ref, o_ref):
    scale = NA_HEAD_DIM ** -0.5
    tq = NA_QROWS * GRID_W
    tk = NA_KROWS * GRID_W
    for ti in range(NA_TILES):
        variant = 0 if ti == 0 else (2 if ti == NA_TILES - 1 else 1)
        k0 = _na_tile_key_row(ti) * GRID_W
        qrows = slice(ti * tq, (ti + 1) * tq)
        krows = slice(k0, k0 + tk)
        s = _dot_nt(q_ref[0, qrows, :], k_ref[0, krows, :]) * scale + bias_ref[0, variant]
        m = jnp.max(s, axis=-1, keepdims=True)
        e = jnp.exp(s - m)
        l = jnp.sum(e, axis=-1, keepdims=True)
        o = jnp.dot(e.astype(BF16), v_ref[0, krows, :], preferred_element_type=F32)
        o_ref[0, qrows, :] = (o / l).astype(o_ref.dtype)


def _na_attention(proj3, bias):
    b = proj3.shape[0]
    hd = NA_HEAD_DIM
    return pl.pallas_call(
        _na_kernel,
        grid=(NA_HEADS, b),
        in_specs=[pl.BlockSpec((1, SEQ, hd), lambda h, i: (i, 0, h)),
                  pl.BlockSpec((1, SEQ, hd), lambda h, i: (i, 0, NA_HEADS + h)),
                  pl.BlockSpec((1, SEQ, hd), lambda h, i: (i, 0, 2 * NA_HEADS + h)),
                  pl.BlockSpec((1,) + bias.shape[1:], lambda h, i: (h, 0, 0, 0))],
        out_specs=pl.BlockSpec((1, SEQ, hd), lambda h, i: (i, 0, h)),
        out_shape=jax.ShapeDtypeStruct((b, SEQ, NA_WIDTH), BF16),
        compiler_params=_cparams(("arbitrary", "arbitrary")),
        name="na_attn",
    )(proj3, proj3, proj3, bias)


def _alibi_table(tq):
    slopes = jnp.exp2(-8.0 * jnp.arange(1, DIFF_HEADS + 1, dtype=F32) / DIFF_HEADS)
    r = jnp.arange(tq)[:, None]
    x = jnp.arange(2 * SEQ - tq)[None, :]
    dist = jnp.abs(r - x + (SEQ - tq)).astype(F32)
    return slopes[:, None, None] * dist[None]


def _diff_kernel(lq1_ref, lk1_ref, lq2_ref, lk2_ref, q_ref, k_ref, v_ref, g_ref, alibi_ref, o_ref):
    qi = pl.program_id(2)
    tq = q_ref.shape[1]
    lam = (jnp.exp(jnp.sum(lq1_ref[...] * lk1_ref[...], axis=-1, keepdims=True))
           - jnp.exp(jnp.sum(lq2_ref[...] * lk2_ref[...], axis=-1, keepdims=True)) + LAMBDA_INIT)
    q = q_ref[0] * (DIFF_QK_DIM ** -0.5)
    k = k_ref[0]
    bias = alibi_ref[0, :, pl.ds(pl.multiple_of((SEQ - tq) - qi * tq, tq), SEQ)]

    def expo(mi):
        sl = slice(mi * DIFF_QK_DIM, (mi + 1) * DIFF_QK_DIM)
        s = _dot_nt(q[:, sl], k[:, sl]) - bias
        m = jnp.max(s, axis=-1, keepdims=True)
        e = jnp.exp(s - m)
        return e, jnp.sum(e, axis=-1, keepdims=True)

    e0, l0 = expo(0)
    e1, l1 = expo(1)
    a = (e0 - e1 * (lam * l0 / l1)).astype(BF16)
    o = jnp.dot(a, v_ref[0], preferred_element_type=F32) / l0
    o = _rms_scale(o, g_ref[...], SUBLN_EPS) * (1.0 - LAMBDA_INIT)
    o_ref[0] = o.astype(o_ref.dtype)


def _diff_attention(proj3, lq1, lk1, lq2, lk2, subln_g):
    b = proj3.shape[0]
    tq = DIFF_TQ
    qoff = 3 * NA_WIDTH // LANES
    koff = qoff + DIFF_QK_WIDTH // LANES
    voff = koff + DIFF_QK_WIDTH // LANES
    vec = lambda n: pl.BlockSpec((1, n), lambda h, i, j: (0, 0))
    return pl.pallas_call(
        _diff_kernel,
        grid=(DIFF_HEADS, b, SEQ // tq),
        in_specs=[vec(DIFF_QK_DIM), vec(DIFF_QK_DIM), vec(DIFF_QK_DIM), vec(DIFF_QK_DIM),
                  pl.BlockSpec((1, tq, LANES), lambda h, i, j: (i, j, qoff + h)),
                  pl.BlockSpec((1, SEQ, LANES), lambda h, i, j: (i, 0, koff + h)),
                  pl.BlockSpec((1, SEQ, LANES), lambda h, i, j: (i, 0, voff + h)),
                  vec(DIFF_V_DIM),
                  pl.BlockSpec((1, tq, 2 * SEQ - tq), lambda h, i, j: (h, 0, 0))],
        out_specs=pl.BlockSpec((1, tq, DIFF_V_DIM), lambda h, i, j: (i, j, h)),
        out_shape=jax.ShapeDtypeStruct((b, SEQ, DIFF_V_WIDTH), BF16),
        compiler_params=_cparams(("arbitrary", "arbitrary", "arbitrary")),
        name="diff_attn",
    )(lq1, lk1, lq2, lk2, proj3, proj3, proj3, subln_g, _alibi_table(tq))


def _merge_kernel(n_first, a_ref, b_ref, gna_ref, gdf_ref, xp_ref, xs_ref, wna_ref, wdf_ref, wout_ref, o_ref):
    pa = jnp.dot(a_ref[...], wna_ref[...], preferred_element_type=F32)
    pb = jnp.dot(b_ref[...], wdf_ref[...], preferred_element_type=F32)
    merged = (jax.nn.sigmoid(gna_ref[...].astype(F32)) * pa
              + jax.nn.sigmoid(gdf_ref[...].astype(F32)) * pb)
    delta = jnp.dot(merged.astype(BF16), wout_ref[...], preferred_element_type=F32)
    in_first = pl.program_id(0) < n_first

    @pl.when(in_first)
    def _():
        o_ref[...] = xp_ref[...] + delta

    @pl.when(jnp.logical_not(in_first))
    def _():
        o_ref[...] = xs_ref[...] + delta


def _merge(a, b, proj, xp, xs, wna, wdf, wout):
    t = xp.shape[0] + xs.shape[0]
    tm = MERGE_TM
    gna_blk = (3 * NA_WIDTH + 2 * DIFF_QK_WIDTH + DIFF_V_WIDTH) // D_MODEL
    const = lambda shape: pl.BlockSpec(shape, lambda i: (0, 0))
    n_first = xp.shape[0] // tm
    spec_p, spec_s = _two_group_specs(tm, D_MODEL, n_first)
    return pl.pallas_call(
        functools.partial(_merge_kernel, n_first),
        grid=(t // tm,),
        in_specs=[pl.BlockSpec((tm, NA_WIDTH), lambda i: (i, 0)),
                  pl.BlockSpec((tm, DIFF_V_WIDTH), lambda i: (i, 0)),
                  pl.BlockSpec((tm, D_MODEL), lambda i: (i, gna_blk)),
                  pl.BlockSpec((tm, D_MODEL), lambda i: (i, gna_blk + 1)),
                  spec_p, spec_s,
                  const((NA_WIDTH, D_MODEL)), const((DIFF_V_WIDTH, D_MODEL)), const((D_MODEL, D_MODEL))],
        out_specs=pl.BlockSpec((tm, D_MODEL), lambda i: (i, 0)),
        out_shape=jax.ShapeDtypeStruct((t, D_MODEL), F32),
        compiler_params=_cparams(("parallel",)),
        name="merge_outproj",
    )(a, b, proj, proj, xp, xs, wna, wdf, wout)


def _memkv_kernel(m_ref, g_ref, w_ref, o_ref):
    h = _rms_scale(m_ref[...], g_ref[...], NORM_EPS).astype(BF16)
    o_ref[...] = jnp.dot(h, w_ref[...], preferred_element_type=F32).astype(o_ref.dtype)


def _memkv(mem, g, w):
    t = mem.shape[0]
    return pl.pallas_call(
        _memkv_kernel,
        grid=(t // MEM_LEN,),
        in_specs=[pl.BlockSpec((MEM_LEN, D_MODEL), lambda i: (i, 0)),
                  pl.BlockSpec((1, D_MODEL), lambda i: (0, 0)),
                  pl.BlockSpec((D_MODEL, 2 * XA_WIDTH), lambda i: (0, 0))],
        out_specs=pl.BlockSpec((MEM_LEN, 2 * XA_WIDTH), lambda i: (i, 0)),
        out_shape=jax.ShapeDtypeStruct((t, 2 * XA_WIDTH), BF16),
        compiler_params=_cparams(("parallel",)),
        name="mem_kv",
    )(mem, g, w)


def _route(logits):
    lane = lax.broadcasted_iota(I32, logits.shape, 1).astype(F32)
    ninf = -jnp.inf
    big = float(LANES)
    gl = jnp.where(lane < N_GROUPS, logits, ninf)
    gmax = jnp.max(gl, axis=-1, keepdims=True)
    g = jnp.min(jnp.where(gl == gmax, lane, big), axis=-1, keepdims=True)
    pg = 1.0 / jnp.sum(jnp.exp(gl - gmax), axis=-1, keepdims=True)
    lo = N_GROUPS + EXPERTS_PER_GROUP * g
    el = jnp.where((lane >= lo) & (lane < lo + EXPERTS_PER_GROUP), logits, ninf)
    v1 = jnp.max(el, axis=-1, keepdims=True)
    i1 = jnp.min(jnp.where(el == v1, lane, big), axis=-1, keepdims=True)
    el2 = jnp.where(lane == i1, ninf, el)
    v2 = jnp.max(el2, axis=-1, keepdims=True)
    i2 = jnp.min(jnp.where(el2 == v2, lane, big), axis=-1, keepdims=True)
    t = jnp.exp(v2 - v1)
    den = 1.0 + t
    gate1 = pg * (1.0 / den)
    gate2 = pg * (t / den)
    gates = jnp.where(lane == 0.0, gate1, jnp.where(lane == 1.0, gate2, 0.0))
    eids = jnp.where(lane == 0.0, i1 - N_GROUPS, jnp.where(lane == 1.0, i2 - N_GROUPS, 0.0)).astype(I32)
    return gates, eids


def _pack_halves(h):
    half = h.shape[1] // 2
    hi = pltpu.bitcast(h[:, :half].astype(BF16).astype(F32), U32)
    lo = pltpu.bitcast(h[:, half:].astype(BF16).astype(F32), U32)
    return hi | (lo >> 16)


def _unpack_halves(w):
    hi = pltpu.bitcast(w & jnp.uint32(0xFFFF0000), F32).astype(BF16)
    lo = pltpu.bitcast(w << 16, F32).astype(BF16)
    return jnp.concatenate([hi, lo], axis=1)


def _xattn_kernel(x_ref, kv_ref, gx_ref, wq_ref, wo_ref, gf_ref, wr_ref, br_ref,
                  x2_ref, hp_ref, gate_ref, eid_ref):
    x1 = x_ref[...]
    hq = _rms_scale(x1, gx_ref[...], NORM_EPS).astype(BF16)
    q = jnp.dot(hq, wq_ref[...], preferred_element_type=F32).astype(BF16)
    scale = XA_HEAD_DIM ** -0.5
    outs = []
    for h in range(XA_HEADS):
        sl = slice(h * XA_HEAD_DIM, (h + 1) * XA_HEAD_DIM)
        vsl = slice(XA_WIDTH + h * XA_HEAD_DIM, XA_WIDTH + (h + 1) * XA_HEAD_DIM)
        s = _dot_nt(q[:, sl], kv_ref[:, sl]) * scale
        m = jnp.max(s, axis=-1, keepdims=True)
        e = jnp.exp(s - m)
        l = jnp.sum(e, axis=-1, keepdims=True)
        o = jnp.dot(e.astype(BF16), kv_ref[:, vsl], preferred_element_type=F32)
        outs.append((o / l).astype(BF16))
    o = jnp.concatenate(outs, axis=1)
    x2 = x1 + jnp.dot(o, wo_ref[...], preferred_element_type=F32)
    x2_ref[...] = x2
    h3 = _rms_scale(x2, gf_ref[...], NORM_EPS)
    hp_ref[...] = _pack_halves(h3)
    logits = jnp.dot(h3.astype(BF16), wr_ref[...], preferred_element_type=F32) + br_ref[...]
    gates, eids = _route(logits)
    gate_ref[...] = gates
    eid_ref[...] = eids


def _xattn(x1, kv, gx, wq, wo, gf, wr, br):
    t = x1.shape[0]
    tm = XA_TM
    per_batch = SEQ // tm
    const = lambda shape: pl.BlockSpec(shape, lambda i: (0, 0))
    tile = lambda n: pl.BlockSpec((tm, n), lambda i: (i, 0))
    return pl.pallas_call(
        _xattn_kernel,
        grid=(t // tm,),
        in_specs=[tile(D_MODEL),
                  pl.BlockSpec((MEM_LEN, 2 * XA_WIDTH), lambda i: (i // per_batch, 0)),
                  const((1, D_MODEL)), const((D_MODEL, XA_WIDTH)), const((XA_WIDTH, D_MODEL)),
                  const((1, D_MODEL)), const((D_MODEL, LANES)), const((1, LANES))],
        out_specs=[tile(D_MODEL), tile(D_MODEL // 2), tile(LANES), tile(LANES)],
        out_shape=[jax.ShapeDtypeStruct((t, D_MODEL), F32),
                   jax.ShapeDtypeStruct((t, D_MODEL // 2), U32),
                   jax.ShapeDtypeStruct((t, LANES), F32),
                   jax.ShapeDtypeStruct((t, LANES), I32)],
        compiler_params=_cparams(("parallel",)),
        name="xattn_router",
    )(x1, kv, gx, wq, wo, gf, wr, br)


def _rank_kernel(eid_ref, rank_ref, cnt_ref, carry_ref):
    @pl.when(pl.program_id(0) == 0)
    def _():
        carry_ref[...] = jnp.zeros_like(carry_ref)

    tm = eid_ref.shape[0]
    eid = eid_ref[...]
    lane = lax.broadcasted_iota(I32, (tm, LANES), 1)
    m1 = lane == eid[:, 0:1]
    m2 = lane == eid[:, 1:2]
    onehot = jnp.where(m1, 1.0, jnp.where(m2, 1.0, 0.0))
    r = lax.broadcasted_iota(I32, (tm, tm), 0)
    c = lax.broadcasted_iota(I32, (tm, tm), 1)
    tri = jnp.where(c < r, 1.0, 0.0).astype(BF16)
    before = jnp.dot(tri, onehot.astype(BF16), preferred_element_type=F32) + carry_ref[0:1, :]
    r1 = jnp.sum(jnp.where(m1, before, 0.0), axis=-1, keepdims=True)
    r2 = jnp.sum(jnp.where(m2, before, 0.0), axis=-1, keepdims=True)
    rank_ref[...] = jnp.where(lane == 0, r1, jnp.where(lane == 1, r2, 0.0)).astype(I32)
    carry_ref[...] = carry_ref[...] + jnp.sum(onehot, axis=0, keepdims=True)
    cnt_ref[...] = carry_ref[...]


def _rank(eid):
    t = eid.shape[0]
    tm = RANK_TM
    return pl.pallas_call(
        _rank_kernel,
        grid=(t // tm,),
        in_specs=[pl.BlockSpec((tm, LANES), lambda i: (i, 0))],
        out_specs=[pl.BlockSpec((tm, LANES), lambda i: (i, 0)),
                   pl.BlockSpec((8, LANES), lambda i: (0, 0))],
        out_shape=[jax.ShapeDtypeStruct((t, LANES), I32),
                   jax.ShapeDtypeStruct((8, LANES), F32)],
        scratch_shapes=[pltpu.VMEM((8, LANES), F32)],
        compiler_params=_cparams(("arbitrary",)),
        name="expert_rank",
    )(eid)


MOE_DOWN_CHUNKS = 4


def _moe_kernel(bexp_ref, nused_ref, tok_next_ref, tok_first_ref, hp_ref, wg_ref, wu_ref, wd_ref, o_ref,
                xbuf, wgb, wub, wdb, sem):
    i = pl.program_id(0)
    nused = nused_ref[0]
    tm = o_ref.shape[0]
    slot = lax.rem(i, 2)

    def row_copy(tok, buf_slot, r):
        return pltpu.make_async_copy(hp_ref.at[pl.ds(tok, 1)], xbuf.at[buf_slot, pl.ds(r, 1)], sem.at[buf_slot])

    def start_rows(tok_ref, buf_slot, r0, r1):
        for r in range(r0, r1):
            row_copy(tok_ref[0, 0, r], buf_slot, r).start()

    def wait_rows(buf_slot):
        for r in range(tm):
            row_copy(0, buf_slot, r).wait()

    @pl.when(i == 0)
    def _():
        start_rows(tok_first_ref, 0, 0, tm)

    @pl.when((i < nused) & ((i == 0) | (bexp_ref[i] != bexp_ref[jnp.maximum(i - 1, 0)])))
    def _():
        wgb[...] = wg_ref[0].astype(BF16)
        wub[...] = wu_ref[0].astype(BF16)
        wdb[...] = wd_ref[0].astype(BF16)

    @pl.when(i < nused)
    def _():
        wait_rows(slot)
        x = _unpack_halves(xbuf[slot])
        step = tm // (MOE_DOWN_CHUNKS + 2)
        bounds = [k * step for k in range(MOE_DOWN_CHUNKS + 2)] + [tm]
        nxt = 1 - slot
        start_rows(tok_next_ref, nxt, bounds[0], bounds[1])
        hg = jnp.dot(x, wgb[...], preferred_element_type=F32)
        start_rows(tok_next_ref, nxt, bounds[1], bounds[2])
        hu = jnp.dot(x, wub[...], preferred_element_type=F32)
        hdn = ((hg * jax.nn.sigmoid(hg)) * hu).astype(BF16)
        cw = D_MODEL // MOE_DOWN_CHUNKS
        for c in range(MOE_DOWN_CHUNKS):
            start_rows(tok_next_ref, nxt, bounds[2 + c], bounds[3 + c])
            cols = slice(c * cw, (c + 1) * cw)
            o_ref[:, cols] = jnp.dot(hdn, wdb[:, cols], preferred_element_type=F32)

    @pl.when(i == nused - 1)
    def _():
        wait_rows(1 - slot)

    @pl.when(i >= nused)
    def _():
        o_ref[...] = jnp.zeros_like(o_ref)


def _moe(bexp, nused, slot_tok, hp, wg, wu, wd):
    w = hp.shape[1]
    tm = MOE_TM
    n_slots = slot_tok.shape[0]
    n_blocks = n_slots // tm
    tok3 = slot_tok.reshape(n_blocks, 1, tm)
    smem_blk = lambda index_map: pl.BlockSpec((1, 1, tm), index_map, memory_space=pltpu.SMEM)
    grid_spec = pltpu.PrefetchScalarGridSpec(
        num_scalar_prefetch=2,
        grid=(n_blocks,),
        in_specs=[smem_blk(lambda i, be, nu: (jnp.minimum(i + 1, nu[0] - 1), 0, 0)),
                  smem_blk(lambda i, be, nu: (0, 0, 0)),
                  pl.BlockSpec(memory_space=pl.ANY),
                  pl.BlockSpec((1, D_MODEL, D_EXPERT), lambda i, be, nu: (be[i], 0, 0)),
                  pl.BlockSpec((1, D_MODEL, D_EXPERT), lambda i, be, nu: (be[i], 0, 0)),
                  pl.BlockSpec((1, D_EXPERT, D_MODEL), lambda i, be, nu: (be[i], 0, 0))],
        out_specs=pl.BlockSpec((tm, D_MODEL), lambda i, be, nu: (i, 0)),
        scratch_shapes=[pltpu.VMEM((2, tm, w), U32),
                        pltpu.VMEM((D_MODEL, D_EXPERT), BF16), pltpu.VMEM((D_MODEL, D_EXPERT), BF16),
                        pltpu.VMEM((D_EXPERT, D_MODEL), BF16),
                        pltpu.SemaphoreType.DMA((2,))],
    )
    return pl.pallas_call(
        _moe_kernel,
        grid_spec=grid_spec,
        out_shape=jax.ShapeDtypeStruct((n_slots, D_MODEL), F32),
        compiler_params=_cparams(("arbitrary",)),
        name="moe_experts",
    )(bexp, nused, tok3, tok3, hp, wg, wu, wd)


def _combine_kernel(dest_ref, x_ref, gate_ref, g_ref, ys_ref, o_ref, y1_ref, y2_ref, sem):
    tm = x_ref.shape[0]
    bufs = (y1_ref, y2_ref)

    def row_copy(r, k, d):
        return pltpu.make_async_copy(ys_ref.at[pl.ds(d, 1)], bufs[k].at[pl.ds(r, 1)], sem)

    def issue(r, carry):
        for k in range(TOP_K):
            row_copy(r, k, dest_ref[0, 0, TOP_K * r + k]).start()
        return carry

    def drain(r, carry):
        for k in range(TOP_K):
            row_copy(r, k, dest_ref[0, 0, TOP_K * r + k]).wait()
        return carry

    lax.fori_loop(0, tm, issue, 0)
    lax.fori_loop(0, tm, drain, 0)
    gates = gate_ref[...]
    moe = gates[:, 0:1] * y1_ref[...] + gates[:, 1:2] * y2_ref[...]
    o_ref[...] = _rms_scale(x_ref[...] + moe, g_ref[...], NORM_EPS)


def _combine(dest, x2, gates, g, ys, row0, n_rows):
    tm = ROW_TM
    t = x2.shape[0]
    dest3 = dest.reshape(t // tm, 1, TOP_K * tm)
    blk0 = row0 // tm
    return pl.pallas_call(
        _combine_kernel,
        grid=(n_rows // tm,),
        in_specs=[pl.BlockSpec((1, 1, TOP_K * tm), lambda i: (blk0 + i, 0, 0), memory_space=pltpu.SMEM),
                  pl.BlockSpec((tm, D_MODEL), lambda i: (blk0 + i, 0)),
                  pl.BlockSpec((tm, LANES), lambda i: (blk0 + i, 0)),
                  pl.BlockSpec((1, D_MODEL), lambda i: (0, 0)),
                  pl.BlockSpec(memory_space=pl.ANY)],
        out_specs=pl.BlockSpec((tm, D_MODEL), lambda i: (i, 0)),
        out_shape=jax.ShapeDtypeStruct((n_rows, D_MODEL), F32),
        scratch_shapes=[pltpu.VMEM((tm, D_MODEL), F32), pltpu.VMEM((tm, D_MODEL), F32),
                        pltpu.SemaphoreType.DMA(())],
        compiler_params=_cparams(("arbitrary",)),
        name="moe_combine",
    )(dest3, x2, gates, g, ys)


def _routing_plan(eid, rank, counts):
    cnt = counts[0, :N_EXPERTS].astype(I32)
    padded = (cnt + MOE_TM - 1) // MOE_TM * MOE_TM
    pends = jnp.cumsum(padded)
    pstarts = pends - padded
    dest = pstarts[eid[:, :TOP_K]] + rank[:, :TOP_K]
    t = eid.shape[0]
    n_blocks = t * TOP_K // MOE_TM + N_EXPERTS
    blk0 = jnp.arange(n_blocks, dtype=I32) * MOE_TM
    bexp = jnp.minimum(jnp.sum(pends[None, :] <= blk0[:, None], axis=1), N_EXPERTS - 1).astype(I32)
    nused = (pends[-1:] // MOE_TM).astype(I32)
    dest = dest.reshape(-1).astype(I32)
    tok = jnp.arange(t * TOP_K, dtype=I32) // TOP_K
    slot_tok = jnp.zeros((n_blocks * MOE_TM,), I32).at[dest].set(tok, unique_indices=True)
    return dest, slot_tok, bexp, nused


def kernel(x_prompt, x_sample, mem_prompt, mem_sample, mix_norm_g, w_in, na_rpb, diff_lambda_q1, diff_lambda_k1, diff_lambda_q2, diff_lambda_k2, diff_subln_g, w_branch_na, w_branch_diff, w_out, xa_norm_g, mem_norm_g, xa_w_q, xa_w_kv, xa_w_o, ffn_norm_g, router_group_w, router_group_b, router_expert_w, router_expert_b, w_gate, w_up, w_down, final_norm_g):
    nb_p, nb_s = x_prompt.shape[0], x_sample.shape[0]
    nb = nb_p + nb_s
    t = nb * SEQ
    xp = x_prompt.reshape(nb_p * SEQ, D_MODEL)
    xs_in = x_sample.reshape(nb_s * SEQ, D_MODEL)
    mem = jnp.concatenate([mem_prompt, mem_sample], axis=0).reshape(nb * MEM_LEN, D_MODEL)
    row = lambda v: v.reshape(1, -1).astype(F32)
    bf = lambda w: w.astype(BF16)

    proj = _inproj(xp, xs_in, row(mix_norm_g[0]), bf(w_in[0]))
    proj3 = proj.reshape(nb, SEQ, IN_COLS)
    a = _na_attention(proj3, _na_bias_table(na_rpb[0]))
    b = _diff_attention(proj3, row(diff_lambda_q1[0]), row(diff_lambda_k1[0]),
                        row(diff_lambda_q2[0]), row(diff_lambda_k2[0]), row(diff_subln_g[0]))
    x1 = _merge(a.reshape(t, NA_WIDTH), b.reshape(t, DIFF_V_WIDTH), proj, xp, xs_in,
                bf(w_branch_na[0]), bf(w_branch_diff[0]), bf(w_out[0]))
    kv = _memkv(mem, row(mem_norm_g[0]), bf(xa_w_kv[0]))
    w_router = jnp.zeros((D_MODEL, LANES), F32)
    w_router = w_router.at[:, :N_GROUPS].set(router_group_w[0]).at[:, N_GROUPS:N_GROUPS + N_EXPERTS].set(router_expert_w[0])
    b_router = jnp.zeros((1, LANES), F32)
    b_router = b_router.at[0, :N_GROUPS].set(router_group_b[0]).at[0, N_GROUPS:N_GROUPS + N_EXPERTS].set(router_expert_b[0])
    x2, hp, gates, eid = _xattn(x1, kv, row(xa_norm_g[0]), bf(xa_w_q[0]), bf(xa_w_o[0]),
                                row(ffn_norm_g[0]), bf(w_router), b_router)
    rank, counts = _rank(eid)
    dest, slot_tok, bexp, nused = _routing_plan(eid, rank, counts)
    ys = _moe(bexp, nused, slot_tok, hp, w_gate[0], w_up[0], w_down[0])
    fg = row(final_norm_g)
    y_p = _combine(dest, x2, gates, fg, ys, 0, nb_p * SEQ).reshape(nb_p, SEQ, D_MODEL)
    y_s = _combine(dest, x2, gates, fg, ys, nb_p * SEQ, nb_s * SEQ).reshape(nb_s, SEQ, D_MODEL)
    return (y_p, y_s)
```

```python
import functools
import math

import jax
import jax.numpy as jnp
from jax import lax
from jax.experimental import pallas as pl
from jax.experimental.pallas import tpu as pltpu

F32 = jnp.float32
BF16 = jnp.bfloat16
I32 = jnp.int32
U32 = jnp.uint32

D_MODEL = 2048
SEQ = 2048
GRID_W = 64
ROWS = SEQ // GRID_W
NA_HEADS = 8
NA_HEAD_DIM = 128
NA_WIDTH = NA_HEADS * NA_HEAD_DIM
NA_KH = 8
NA_KW = 16
DIFF_HEADS = 8
DIFF_QK_DIM = 64
DIFF_V_DIM = 128
DIFF_QK_WIDTH = DIFF_HEADS * 2 * DIFF_QK_DIM
DIFF_V_WIDTH = DIFF_HEADS * DIFF_V_DIM
IN_COLS = 3 * NA_WIDTH + 2 * DIFF_QK_WIDTH + DIFF_V_WIDTH + 2 * D_MODEL
MEM_LEN = 256
XA_HEADS = 4
XA_HEAD_DIM = 128
XA_WIDTH = XA_HEADS * XA_HEAD_DIM
N_GROUPS = 4
EXPERTS_PER_GROUP = 8
N_EXPERTS = N_GROUPS * EXPERTS_PER_GROUP
TOP_K = 2
D_EXPERT = 512
NORM_EPS = 1e-6
SUBLN_EPS = 1e-5
NEG_INF = -1e30
LAMBDA_INIT = 0.8 - 0.6 * math.exp(-0.3 * 0)
LOG2E = math.log2(math.e)

LANES = 128
VMEM_LIMIT = 56 * 1024 * 1024

INPROJ_TM = 1024
INPROJ_TN = 1024
NORM_CHUNK = 128
DIFF_TQ = 512
DIFF_SUB = 256
MERGE_TM = 256
XA_TM = 256
RANK_TM = 512
MOE_TM = 512
ROW_TM = 256


def _cparams(sem):
    return pltpu.CompilerParams(dimension_semantics=sem, vmem_limit_bytes=VMEM_LIMIT)


def _rms_scale(x, g, eps):
    ms = jnp.mean(x * x, axis=-1, keepdims=True)
    return x * lax.rsqrt(ms + eps) * g


def _dot_nt(a, b):
    return lax.dot_general(a, b, (((1,), (1,)), ((), ())), preferred_element_type=F32)


def _two_group_specs(tm, width, n_first):
    first = pl.BlockSpec((tm, width), lambda i, *_: (jnp.minimum(i, n_first - 1), 0))
    second = pl.BlockSpec((tm, width), lambda i, *_: (jnp.maximum(i - n_first, 0), 0))
    return first, second


def _inproj_kernel(n_first, xp_ref, xs_ref, g_ref, w_ref, cs_ref, o_ref, h_ref):
    def normalise(x_ref):
        def chunk(c, carry):
            rows = pl.ds(pl.multiple_of(c * NORM_CHUNK, NORM_CHUNK), NORM_CHUNK)
            h_ref[rows, :] = _rms_scale(x_ref[rows, :], g_ref[...], NORM_EPS).astype(BF16)
            return carry
        lax.fori_loop(0, x_ref.shape[0] // NORM_CHUNK, chunk, 0)

    first_col = pl.program_id(1) == 0
    in_first = pl.program_id(0) < n_first
    pl.when(first_col & in_first)(lambda: normalise(xp_ref))
    pl.when(first_col & jnp.logical_not(in_first))(lambda: normalise(xs_ref))
    acc = jnp.dot(h_ref[...], w_ref[...], preferred_element_type=F32)
    o_ref[...] = (acc * cs_ref[...]).astype(o_ref.dtype)


def _inproj(xp, xs, g, w, colscale):
    d = xp.shape[1]
    t = xp.shape[0] + xs.shape[0]
    n = w.shape[1]
    tm = INPROJ_TM
    n_first = xp.shape[0] // tm
    spec_p, spec_s = _two_group_specs(tm, d, n_first)
    return pl.pallas_call(
        functools.partial(_inproj_kernel, n_first),
        grid=(t // tm, n // INPROJ_TN),
        in_specs=[spec_p, spec_s,
                  pl.BlockSpec((1, d), lambda i, j: (0, 0)),
                  pl.BlockSpec((d, INPROJ_TN), lambda i, j: (0, j)),
                  pl.BlockSpec((1, INPROJ_TN), lambda i, j: (0, j))],
        out_specs=pl.BlockSpec((tm, INPROJ_TN), lambda i, j: (i, j)),
        out_shape=jax.ShapeDtypeStruct((t, n), BF16),
        scratch_shapes=[pltpu.VMEM((tm, d), BF16)],
        compiler_params=_cparams(("parallel", "arbitrary")),
        name="inproj",
    )(xp, xs, g, w, colscale)


NA_QROWS = 4
NA_KROWS = 12
NA_TILES = ROWS // NA_QROWS


def _na_tile_key_row(ti):
    return min(max(NA_QROWS * ti - NA_KH // 2, 0), ROWS - NA_KROWS)


def _na_bias_table(rpb):
    cols = jnp.arange(GRID_W)
    cstart = jnp.clip(cols - NA_KW // 2, 0, GRID_W - NA_KW)
    col_ok = (cols[None, :] >= cstart[:, None]) & (cols[None, :] < cstart[:, None] + NA_KW)
    dc = jnp.clip(cols[None, :] - cols[:, None], 1 - NA_KW, NA_KW - 1) + NA_KW - 1
    pick = (dc[None] == jnp.arange(2 * NA_KW - 1)[:, None, None]).astype(F32)
    t = jnp.einsum("hdx,xck->hdck", rpb.astype(F32), pick, precision=lax.Precision.HIGHEST)
    t = jnp.where(col_ok[None, None], t, NEG_INF)
    tiles = jnp.array([0, 1, NA_TILES - 1])
    r = NA_QROWS * tiles[:, None, None] + jnp.arange(NA_QROWS)[None, :, None]
    ks = jnp.clip(NA_QROWS * tiles - NA_KH // 2, 0, ROWS - NA_KROWS)[:, None, None]
    kr = ks + jnp.arange(NA_KROWS)[None, None, :]
    rs = jnp.clip(r - NA_KH // 2, 0, ROWS - NA_KH)
    valid = (kr >= rs) & (kr < rs + NA_KH)
    dr = jnp.clip(kr - r + NA_KH - 1, 0, 2 * NA_KH - 2)
    b = jnp.where(valid[None, :, :, :, None, None], t[:, dr], NEG_INF)
    return b.transpose(0, 1, 2, 4, 3, 5).reshape(NA_HEADS, 3, NA_QROWS * GRID_W, NA_KROWS * GRID_W)


def _na_kernel(q_ref, k_ref, v_ref, bias_ref, o_ref):
    scale = NA_HEAD_DIM ** -0.5
    tq = NA_QROWS * GRID_W
    tk = NA_KROWS * GRID_W
    for ti in range(NA_TILES):
        variant = 0 if ti == 0 else (2 if ti == NA_TILES - 1 else 1)
        k0 = _na_tile_key_row(ti) * GRID_W
        qrows = slice(ti * tq, (ti + 1) * tq)
        krows = slice(k0, k0 + tk)
        s = _dot_nt(q_ref[0, qrows, :], k_ref[0, krows, :]) * scale + bias_ref[0, variant]
        m = jnp.max(s, axis=-1, keepdims=True)
        e = jnp.exp(s - m)
        l = jnp.sum(e, axis=-1, keepdims=True)
        o = jnp.dot(e.astype(BF16), v_ref[0, krows, :], preferred_element_type=F32)
        o_ref[0, qrows, :] = (o / l).astype(o_ref.dtype)


def _na_attention(proj3, bias):
    b = proj3.shape[0]
    hd = NA_HEAD_DIM
    return pl.pallas_call(
        _na_kernel,
        grid=(NA_HEADS, b),
        in_specs=[pl.BlockSpec((1, SEQ, hd), lambda h, i: (i, 0, h)),
                  pl.BlockSpec((1, SEQ, hd), lambda h, i: (i, 0, NA_HEADS + h)),
                  pl.BlockSpec((1, SEQ, hd), lambda h, i: (i, 0, 2 * NA_HEADS + h)),
                  pl.BlockSpec((1,) + bias.shape[1:], lambda h, i: (h, 0, 0, 0))],
        out_specs=pl.BlockSpec((1, SEQ, hd), lambda h, i: (i, 0, h)),
        out_shape=jax.ShapeDtypeStruct((b, SEQ, NA_WIDTH), BF16),
        compiler_params=_cparams(("arbitrary", "arbitrary")),
        name="na_attn",
    )(proj3, proj3, proj3, bias)


def _alibi_table(tq):
    slopes = jnp.exp2(-8.0 * jnp.arange(1, DIFF_HEADS + 1, dtype=F32) / DIFF_HEADS)
    r = jnp.arange(tq)[:, None]
    x = jnp.arange(2 * SEQ - tq)[None, :]
    dist = jnp.abs(r - x + (SEQ - tq)).astype(F32)
    return (slopes * LOG2E)[:, None, None] * dist[None]


def _diff_kernel(lq1_ref, lk1_ref, lq2_ref, lk2_ref, q_ref, k_ref, v_ref, g_ref, alibi_ref, o_ref):
    qi = pl.program_id(2)
    ts = alibi_ref.shape[1]
    nsub = q_ref.shape[1] // ts
    lam = (jnp.exp(jnp.sum(lq1_ref[...] * lk1_ref[...], axis=-1, keepdims=True))
           - jnp.exp(jnp.sum(lq2_ref[...] * lk2_ref[...], axis=-1, keepdims=True)) + LAMBDA_INIT)
    k = k_ref[0]

    def scores(u):
        q = q_ref[0, u * ts:(u + 1) * ts, :]
        return [_dot_nt(q[:, mi * DIFF_QK_DIM:(mi + 1) * DIFF_QK_DIM], k[:, mi * DIFF_QK_DIM:(mi + 1) * DIFF_QK_DIM])
                for mi in range(2)]

    def expo(s, bias):
        s = s - bias
        m = jnp.max(s, axis=-1, keepdims=True)
        e = jnp.exp2(s - m)
        return e, jnp.sum(e, axis=-1, keepdims=True)

    pending = scores(0)
    for u in range(nsub):
        s0, s1 = pending
        if u + 1 < nsub:
            pending = scores(u + 1)
        off = (SEQ - ts) - (qi * nsub + u) * ts
        bias = alibi_ref[0, :, pl.ds(pl.multiple_of(off, ts), SEQ)]
        e0, l0 = expo(s0, bias)
        e1, l1 = expo(s1, bias)
        a = (e0 - e1 * (lam * l0 / l1)).astype(BF16)
        o = jnp.dot(a, v_ref[0], preferred_element_type=F32) / l0
        o = _rms_scale(o, g_ref[...], SUBLN_EPS) * (1.0 - LAMBDA_INIT)
        o_ref[0, u * ts:(u + 1) * ts, :] = o.astype(o_ref.dtype)


def _diff_attention(proj3, lq1, lk1, lq2, lk2, subln_g):
    b = proj3.shape[0]
    tq = DIFF_TQ
    qoff = 3 * NA_WIDTH // LANES
    koff = qoff + DIFF_QK_WIDTH // LANES
    voff = koff + DIFF_QK_WIDTH // LANES
    vec = lambda n: pl.BlockSpec((1, n), lambda h, i, j: (0, 0))
    return pl.pallas_call(
        _diff_kernel,
        grid=(DIFF_HEADS, b, SEQ // tq),
        in_specs=[vec(DIFF_QK_DIM), vec(DIFF_QK_DIM), vec(DIFF_QK_DIM), vec(DIFF_QK_DIM),
                  pl.BlockSpec((1, tq, LANES), lambda h, i, j: (i, j, qoff + h)),
                  pl.BlockSpec((1, SEQ, LANES), lambda h, i, j: (i, 0, koff + h)),
                  pl.BlockSpec((1, SEQ, LANES), lambda h, i, j: (i, 0, voff + h)),
                  vec(DIFF_V_DIM),
                  pl.BlockSpec((1, DIFF_SUB, 2 * SEQ - DIFF_SUB), lambda h, i, j: (h, 0, 0))],
        out_specs=pl.BlockSpec((1, tq, DIFF_V_DIM), lambda h, i, j: (i, j, h)),
        out_shape=jax.ShapeDtypeStruct((b, SEQ, DIFF_V_WIDTH), BF16),
        compiler_params=_cparams(("arbitrary", "arbitrary", "arbitrary")),
        name="diff_attn",
    )(lq1, lk1, lq2, lk2, proj3, proj3, proj3, subln_g, _alibi_table(DIFF_SUB))


def _merge_kernel(n_first, a_ref, b_ref, gna_ref, gdf_ref, xp_ref, xs_ref, wna_ref, wdf_ref, wout_ref, o_ref):
    pa = jnp.dot(a_ref[...], wna_ref[...], preferred_element_type=F32)
    pb = jnp.dot(b_ref[...], wdf_ref[...], preferred_element_type=F32)
    merged = (jax.nn.sigmoid(gna_ref[...].astype(F32)) * pa
              + jax.nn.sigmoid(gdf_ref[...].astype(F32)) * pb)
    delta = jnp.dot(merged.astype(BF16), wout_ref[...], preferred_element_type=F32)
    in_first = pl.program_id(0) < n_first

    @pl.when(in_first)
    def _():
        o_ref[...] = xp_ref[...] + delta

    @pl.when(jnp.logical_not(in_first))
    def _():
        o_ref[...] = xs_ref[...] + delta


def _merge(a, b, proj, xp, xs, wna, wdf, wout):
    t = xp.shape[0] + xs.shape[0]
    tm = MERGE_TM
    gna_blk = (3 * NA_WIDTH + 2 * DIFF_QK_WIDTH + DIFF_V_WIDTH) // D_MODEL
    const = lambda shape: pl.BlockSpec(shape, lambda i: (0, 0))
    n_first = xp.shape[0] // tm
    spec_p, spec_s = _two_group_specs(tm, D_MODEL, n_first)
    return pl.pallas_call(
        functools.partial(_merge_kernel, n_first),
        grid=(t // tm,),
        in_specs=[pl.BlockSpec((tm, NA_WIDTH), lambda i: (i, 0)),
                  pl.BlockSpec((tm, DIFF_V_WIDTH), lambda i: (i, 0)),
                  pl.BlockSpec((tm, D_MODEL), lambda i: (i, gna_blk)),
                  pl.BlockSpec((tm, D_MODEL), lambda i: (i, gna_blk + 1)),
                  spec_p, spec_s,
                  const((NA_WIDTH, D_MODEL)), const((DIFF_V_WIDTH, D_MODEL)), const((D_MODEL, D_MODEL))],
        out_specs=pl.BlockSpec((tm, D_MODEL), lambda i: (i, 0)),
        out_shape=jax.ShapeDtypeStruct((t, D_MODEL), F32),
        compiler_params=_cparams(("parallel",)),
        name="merge_outproj",
    )(a, b, proj, proj, xp, xs, wna, wdf, wout)


def _memkv_kernel(m_ref, g_ref, w_ref, o_ref):
    h = _rms_scale(m_ref[...], g_ref[...], NORM_EPS).astype(BF16)
    o_ref[...] = jnp.dot(h, w_ref[...], preferred_element_type=F32).astype(o_ref.dtype)


def _memkv(mem, g, w):
    t = mem.shape[0]
    return pl.pallas_call(
        _memkv_kernel,
        grid=(t // MEM_LEN,),
        in_specs=[pl.BlockSpec((MEM_LEN, D_MODEL), lambda i: (i, 0)),
                  pl.BlockSpec((1, D_MODEL), lambda i: (0, 0)),
                  pl.BlockSpec((D_MODEL, 2 * XA_WIDTH), lambda i: (0, 0))],
        out_specs=pl.BlockSpec((MEM_LEN, 2 * XA_WIDTH), lambda i: (i, 0)),
        out_shape=jax.ShapeDtypeStruct((t, 2 * XA_WIDTH), BF16),
        compiler_params=_cparams(("parallel",)),
        name="mem_kv",
    )(mem, g, w)


def _route(logits):
    lane = lax.broadcasted_iota(I32, logits.shape, 1).astype(F32)
    ninf = -jnp.inf
    big = float(LANES)
    gl = jnp.where(lane < N_GROUPS, logits, ninf)
    gmax = jnp.max(gl, axis=-1, keepdims=True)
    g = jnp.min(jnp.where(gl == gmax, lane, big), axis=-1, keepdims=True)
    pg = 1.0 / jnp.sum(jnp.exp(gl - gmax), axis=-1, keepdims=True)
    lo = N_GROUPS + EXPERTS_PER_GROUP * g
    el = jnp.where((lane >= lo) & (lane < lo + EXPERTS_PER_GROUP), logits, ninf)
    v1 = jnp.max(el, axis=-1, keepdims=True)
    i1 = jnp.min(jnp.where(el == v1, lane, big), axis=-1, keepdims=True)
    el2 = jnp.where(lane == i1, ninf, el)
    v2 = jnp.max(el2, axis=-1, keepdims=True)
    i2 = jnp.min(jnp.where(el2 == v2, lane, big), axis=-1, keepdims=True)
    t = jnp.exp(v2 - v1)
    den = 1.0 + t
    gate1 = pg * (1.0 / den)
    gate2 = pg * (t / den)
    gates = jnp.where(lane == 0.0, gate1, jnp.where(lane == 1.0, gate2, 0.0))
    eids = jnp.where(lane == 0.0, i1 - N_GROUPS, jnp.where(lane == 1.0, i2 - N_GROUPS, 0.0)).astype(I32)
    return gates, eids


def _pack_halves(h):
    half = h.shape[1] // 2
    hi = pltpu.bitcast(h[:, :half].astype(BF16).astype(F32), U32)
    lo = pltpu.bitcast(h[:, half:].astype(BF16).astype(F32), U32)
    return hi | (lo >> 16)


def _unpack_halves(w):
    hi = pltpu.bitcast(w & jnp.uint32(0xFFFF0000), F32).astype(BF16)
    lo = pltpu.bitcast(w << 16, F32).astype(BF16)
    return jnp.concatenate([hi, lo], axis=1)


def _xattn_kernel(x_ref, kv_ref, gx_ref, wq_ref, wo_ref, gf_ref, wr_ref, br_ref,
                  x2_ref, hp_ref, gate_ref, eid_ref):
    x1 = x_ref[...]
    hq = _rms_scale(x1, gx_ref[...], NORM_EPS).astype(BF16)
    q = jnp.dot(hq, wq_ref[...], preferred_element_type=F32).astype(BF16)
    scale = XA_HEAD_DIM ** -0.5
    outs = []
    for h in range(XA_HEADS):
        sl = slice(h * XA_HEAD_DIM, (h + 1) * XA_HEAD_DIM)
        vsl = slice(XA_WIDTH + h * XA_HEAD_DIM, XA_WIDTH + (h + 1) * XA_HEAD_DIM)
        s = _dot_nt(q[:, sl], kv_ref[:, sl]) * scale
        m = jnp.max(s, axis=-1, keepdims=True)
        e = jnp.exp(s - m)
        l = jnp.sum(e, axis=-1, keepdims=True)
        o = jnp.dot(e.astype(BF16), kv_ref[:, vsl], preferred_element_type=F32)
        outs.append((o / l).astype(BF16))
    o = jnp.concatenate(outs, axis=1)
    x2 = x1 + jnp.dot(o, wo_ref[...], preferred_element_type=F32)
    x2_ref[...] = x2
    h3 = _rms_scale(x2, gf_ref[...], NORM_EPS)
    hp_ref[...] = _pack_halves(h3)
    logits = jnp.dot(h3.astype(BF16), wr_ref[...], preferred_element_type=F32) + br_ref[...]
    gates, eids = _route(logits)
    gate_ref[...] = gates
    eid_ref[...] = eids


def _xattn(x1, kv, gx, wq, wo, gf, wr, br):
    t = x1.shape[0]
    tm = XA_TM
    per_batch = SEQ // tm
    const = lambda shape: pl.BlockSpec(shape, lambda i: (0, 0))
    tile = lambda n: pl.BlockSpec((tm, n), lambda i: (i, 0))
    return pl.pallas_call(
        _xattn_kernel,
        grid=(t // tm,),
        in_specs=[tile(D_MODEL),
                  pl.BlockSpec((MEM_LEN, 2 * XA_WIDTH), lambda i: (i // per_batch, 0)),
                  const((1, D_MODEL)), const((D_MODEL, XA_WIDTH)), const((XA_WIDTH, D_MODEL)),
                  const((1, D_MODEL)), const((D_MODEL, LANES)), const((1, LANES))],
        out_specs=[tile(D_MODEL), tile(D_MODEL // 2), tile(LANES), tile(LANES)],
        out_shape=[jax.ShapeDtypeStruct((t, D_MODEL), F32),
                   jax.ShapeDtypeStruct((t, D_MODEL // 2), U32),
                   jax.ShapeDtypeStruct((t, LANES), F32),
                   jax.ShapeDtypeStruct((t, LANES), I32)],
        compiler_params=_cparams(("parallel",)),
        name="xattn_router",
    )(x1, kv, gx, wq, wo, gf, wr, br)


def _rank_kernel(eid_ref, rank_ref, cnt_ref, carry_ref):
    @pl.when(pl.program_id(0) == 0)
    def _():
        carry_ref[...] = jnp.zeros_like(carry_ref)

    tm = eid_ref.shape[0]
    eid = eid_ref[...]
    lane = lax.broadcasted_iota(I32, (tm, LANES), 1)
    m1 = lane == eid[:, 0:1]
    m2 = lane == eid[:, 1:2]
    onehot = jnp.where(m1, 1.0, jnp.where(m2, 1.0, 0.0))
    r = lax.broadcasted_iota(I32, (tm, tm), 0)
    c = lax.broadcasted_iota(I32, (tm, tm), 1)
    tri = jnp.where(c < r, 1.0, 0.0).astype(BF16)
    before = jnp.dot(tri, onehot.astype(BF16), preferred_element_type=F32) + carry_ref[0:1, :]
    r1 = jnp.sum(jnp.where(m1, before, 0.0), axis=-1, keepdims=True)
    r2 = jnp.sum(jnp.where(m2, before, 0.0), axis=-1, keepdims=True)
    rank_ref[...] = jnp.where(lane == 0, r1, jnp.where(lane == 1, r2, 0.0)).astype(I32)
    carry_ref[...] = carry_ref[...] + jnp.sum(onehot, axis=0, keepdims=True)
    cnt_ref[...] = carry_ref[...]


def _rank(eid):
    t = eid.shape[0]
    tm = RANK_TM
    return pl.pallas_call(
        _rank_kernel,
        grid=(t // tm,),
        in_specs=[pl.BlockSpec((tm, LANES), lambda i: (i, 0))],
        out_specs=[pl.BlockSpec((tm, LANES), lambda i: (i, 0)),
                   pl.BlockSpec((8, LANES), lambda i: (0, 0))],
        out_shape=[jax.ShapeDtypeStruct((t, LANES), I32),
                   jax.ShapeDtypeStruct((8, LANES), F32)],
        scratch_shapes=[pltpu.VMEM((8, LANES), F32)],
        compiler_params=_cparams(("arbitrary",)),
        name="expert_rank",
    )(eid)


MOE_DOWN_CHUNKS = 4


def _moe_kernel(bexp_ref, nused_ref, tok_next_ref, tok_first_ref, hp_ref, wg_ref, wu_ref, wd_ref, o_ref,
                xbuf, wgb, wub, wdb, sem):
    i = pl.program_id(0)
    nused = nused_ref[0]
    tm = o_ref.shape[0]
    slot = lax.rem(i, 2)

    def row_copy(tok, buf_slot, r):
        return pltpu.make_async_copy(hp_ref.at[pl.ds(tok, 1)], xbuf.at[buf_slot, pl.ds(r, 1)], sem.at[buf_slot])

    def start_rows(tok_ref, buf_slot, r0, r1):
        for r in range(r0, r1):
            row_copy(tok_ref[0, 0, r], buf_slot, r).start(priority=r % 2)

    def wait_rows(buf_slot):
        for r in range(tm):
            row_copy(0, buf_slot, r).wait()

    @pl.when(i == 0)
    def _():
        start_rows(tok_first_ref, 0, 0, tm)

    @pl.when((i < nused) & ((i == 0) | (bexp_ref[i] != bexp_ref[jnp.maximum(i - 1, 0)])))
    def _():
        wgb[...] = wg_ref[0].astype(BF16)
        wub[...] = wu_ref[0].astype(BF16)
        wdb[...] = wd_ref[0].astype(BF16)

    @pl.when(i < nused)
    def _():
        wait_rows(slot)
        x = _unpack_halves(xbuf[slot])
        step = tm // (MOE_DOWN_CHUNKS + 2)
        bounds = [k * step for k in range(MOE_DOWN_CHUNKS + 2)] + [tm]
        nxt = 1 - slot
        start_rows(tok_next_ref, nxt, bounds[0], bounds[1])
        hg = jnp.dot(x, wgb[...], preferred_element_type=F32)
        start_rows(tok_next_ref, nxt, bounds[1], bounds[2])
        hu = jnp.dot(x, wub[...], preferred_element_type=F32)
        hdn = ((hg * jax.nn.sigmoid(hg)) * hu).astype(BF16)
        cw = D_MODEL // MOE_DOWN_CHUNKS
        for c in range(MOE_DOWN_CHUNKS):
            start_rows(tok_next_ref, nxt, bounds[2 + c], bounds[3 + c])
            cols = slice(c * cw, (c + 1) * cw)
            o_ref[:, cols] = jnp.dot(hdn, wdb[:, cols], preferred_element_type=F32)

    @pl.when(i == nused - 1)
    def _():
        wait_rows(1 - slot)

    @pl.when(i >= nused)
    def _():
        o_ref[...] = jnp.zeros_like(o_ref)


def _moe(bexp, nused, slot_tok, hp, wg, wu, wd):
    w = hp.shape[1]
    tm = MOE_TM
    n_slots = slot_tok.shape[0]
    n_blocks = n_slots // tm
    tok3 = slot_tok.reshape(n_blocks, 1, tm)
    smem_blk = lambda index_map: pl.BlockSpec((1, 1, tm), index_map, memory_space=pltpu.SMEM)
    grid_spec = pltpu.PrefetchScalarGridSpec(
        num_scalar_prefetch=2,
        grid=(n_blocks,),
        in_specs=[smem_blk(lambda i, be, nu: (jnp.minimum(i + 1, nu[0] - 1), 0, 0)),
                  smem_blk(lambda i, be, nu: (0, 0, 0)),
                  pl.BlockSpec(memory_space=pl.ANY),
                  pl.BlockSpec((1, D_MODEL, D_EXPERT), lambda i, be, nu: (be[i], 0, 0)),
                  pl.BlockSpec((1, D_MODEL, D_EXPERT), lambda i, be, nu: (be[i], 0, 0)),
                  pl.BlockSpec((1, D_EXPERT, D_MODEL), lambda i, be, nu: (be[i], 0, 0))],
        out_specs=pl.BlockSpec((tm, D_MODEL), lambda i, be, nu: (i, 0)),
        scratch_shapes=[pltpu.VMEM((2, tm, w), U32),
                        pltpu.VMEM((D_MODEL, D_EXPERT), BF16), pltpu.VMEM((D_MODEL, D_EXPERT), BF16),
                        pltpu.VMEM((D_EXPERT, D_MODEL), BF16),
                        pltpu.SemaphoreType.DMA((2,))],
    )
    return pl.pallas_call(
        _moe_kernel,
        grid_spec=grid_spec,
        out_shape=jax.ShapeDtypeStruct((n_slots, D_MODEL), F32),
        compiler_params=_cparams(("arbitrary",)),
        name="moe_experts",
    )(bexp, nused, tok3, tok3, hp, wg, wu, wd)


def _combine_kernel(dest_ref, x_ref, gate_ref, g_ref, ys_ref, o_ref, y1_ref, y2_ref, sem):
    tm = x_ref.shape[0]
    bufs = (y1_ref, y2_ref)

    def row_copy(r, k, d):
        return pltpu.make_async_copy(ys_ref.at[pl.ds(d, 1)], bufs[k].at[pl.ds(r, 1)], sem)

    def issue(r, carry):
        for k in range(TOP_K):
            row_copy(r, k, dest_ref[0, 0, TOP_K * r + k]).start(priority=k)
        return carry

    def drain(r, carry):
        for k in range(TOP_K):
            row_copy(r, k, dest_ref[0, 0, TOP_K * r + k]).wait()
        return carry

    lax.fori_loop(0, tm, issue, 0)
    lax.fori_loop(0, tm, drain, 0)
    gates = gate_ref[...]
    moe = gates[:, 0:1] * y1_ref[...] + gates[:, 1:2] * y2_ref[...]
    o_ref[...] = _rms_scale(x_ref[...] + moe, g_ref[...], NORM_EPS)


def _combine(dest, x2, gates, g, ys, row0, n_rows):
    tm = ROW_TM
    t = x2.shape[0]
    dest3 = dest.reshape(t // tm, 1, TOP_K * tm)
    blk0 = row0 // tm
    return pl.pallas_call(
        _combine_kernel,
        grid=(n_rows // tm,),
        in_specs=[pl.BlockSpec((1, 1, TOP_K * tm), lambda i: (blk0 + i, 0, 0), memory_space=pltpu.SMEM),
                  pl.BlockSpec((tm, D_MODEL), lambda i: (blk0 + i, 0)),
                  pl.BlockSpec((tm, LANES), lambda i: (blk0 + i, 0)),
                  pl.BlockSpec((1, D_MODEL), lambda i: (0, 0)),
                  pl.BlockSpec(memory_space=pl.ANY)],
        out_specs=pl.BlockSpec((tm, D_MODEL), lambda i: (i, 0)),
        out_shape=jax.ShapeDtypeStruct((n_rows, D_MODEL), F32),
        scratch_shapes=[pltpu.VMEM((tm, D_MODEL), F32), pltpu.VMEM((tm, D_MODEL), F32),
                        pltpu.SemaphoreType.DMA(())],
        compiler_params=_cparams(("arbitrary",)),
        name="moe_combine",
    )(dest3, x2, gates, g, ys)


def _routing_plan(eid, rank, counts):
    cnt = counts[0, :N_EXPERTS].astype(I32)
    padded = (cnt + MOE_TM - 1) // MOE_TM * MOE_TM
    pends = jnp.cumsum(padded)
    pstarts = pends - padded
    dest = pstarts[eid[:, :TOP_K]] + rank[:, :TOP_K]
    t = eid.shape[0]
    n_blocks = t * TOP_K // MOE_TM + N_EXPERTS
    blk0 = jnp.arange(n_blocks, dtype=I32) * MOE_TM
    bexp = jnp.minimum(jnp.sum(pends[None, :] <= blk0[:, None], axis=1), N_EXPERTS - 1).astype(I32)
    nused = (pends[-1:] // MOE_TM).astype(I32)
    dest = dest.reshape(-1).astype(I32)
    tok = jnp.arange(t * TOP_K, dtype=I32) // TOP_K
    slot_tok = jnp.zeros((n_blocks * MOE_TM,), I32).at[dest].set(tok, unique_indices=True)
    return dest, slot_tok, bexp, nused


def kernel(x_prompt, x_sample, mem_prompt, mem_sample, mix_norm_g, w_in, na_rpb, diff_lambda_q1, diff_lambda_k1, diff_lambda_q2, diff_lambda_k2, diff_subln_g, w_branch_na, w_branch_diff, w_out, xa_norm_g, mem_norm_g, xa_w_q, xa_w_kv, xa_w_o, ffn_norm_g, router_group_w, router_group_b, router_expert_w, router_expert_b, w_gate, w_up, w_down, final_norm_g):
    nb_p, nb_s = x_prompt.shape[0], x_sample.shape[0]
    nb = nb_p + nb_s
    t = nb * SEQ
    xp = x_prompt.reshape(nb_p * SEQ, D_MODEL)
    xs_in = x_sample.reshape(nb_s * SEQ, D_MODEL)
    mem = jnp.concatenate([mem_prompt, mem_sample], axis=0).reshape(nb * MEM_LEN, D_MODEL)
    row = lambda v: v.reshape(1, -1).astype(F32)
    bf = lambda w: w.astype(BF16)

    qcol = 3 * NA_WIDTH
    colscale = jnp.ones((1, IN_COLS), F32).at[:, qcol:qcol + DIFF_QK_WIDTH].set(DIFF_QK_DIM ** -0.5 * LOG2E)
    proj = _inproj(xp, xs_in, row(mix_norm_g[0]), bf(w_in[0]), colscale)
    proj3 = proj.reshape(nb, SEQ, IN_COLS)
    a = _na_attention(proj3, _na_bias_table(na_rpb[0]))
    b = _diff_attention(proj3, row(diff_lambda_q1[0]), row(diff_lambda_k1[0]),
                        row(diff_lambda_q2[0]), row(diff_lambda_k2[0]), row(diff_subln_g[0]))
    x1 = _merge(a.reshape(t, NA_WIDTH), b.reshape(t, DIFF_V_WIDTH), proj, xp, xs_in,
                bf(w_branch_na[0]), bf(w_branch_diff[0]), bf(w_out[0]))
    kv = _memkv(mem, row(mem_norm_g[0]), bf(xa_w_kv[0]))
    w_router = jnp.zeros((D_MODEL, LANES), F32)
    w_router = w_router.at[:, :N_GROUPS].set(router_group_w[0]).at[:, N_GROUPS:N_GROUPS + N_EXPERTS].set(router_expert_w[0])
    b_router = jnp.zeros((1, LANES), F32)
    b_router = b_router.at[0, :N_GROUPS].set(router_group_b[0]).at[0, N_GROUPS:N_GROUPS + N_EXPERTS].set(router_expert_b[0])
    x2, hp, gates, eid = _xattn(x1, kv, row(xa_norm_g[0]), bf(xa_w_q[0]), bf(xa_w_o[0]),
                                row(ffn_norm_g[0]), bf(w_router), b_router)
    rank, counts = _rank(eid)
    dest, slot_tok, bexp, nused = _routing_plan(eid, rank, counts)
    ys = _moe(bexp, nused, slot_tok, hp, w_gate[0], w_up[0], w_down[0])
    fg = row(final_norm_g)
    y_p = _combine(dest, x2, gates, fg, ys, 0, nb_p * SEQ).reshape(nb_p, SEQ, D_MODEL)
    y_s = _combine(dest, x2, gates, fg, ys, nb_p * SEQ, nb_s * SEQ).reshape(nb_s, SEQ, D_MODEL)
    return (y_p, y_s)
```

```python
import functools
import math

import jax
import jax.numpy as jnp
from jax import lax
from jax.experimental import pallas as pl
from jax.experimental.pallas import tpu as pltpu

F32 = jnp.float32
BF16 = jnp.bfloat16
I32 = jnp.int32
U32 = jnp.uint32

D_MODEL = 2048
SEQ = 2048
GRID_W = 64
ROWS = SEQ // GRID_W
NA_HEADS = 8
NA_HEAD_DIM = 128
NA_WIDTH = NA_HEADS * NA_HEAD_DIM
NA_KH = 8
NA_KW = 16
DIFF_HEADS = 8
DIFF_QK_DIM = 64
DIFF_V_DIM = 128
DIFF_QK_WIDTH = DIFF_HEADS * 2 * DIFF_QK_DIM
DIFF_V_WIDTH = DIFF_HEADS * DIFF_V_DIM
IN_COLS = 3 * NA_WIDTH + 2 * DIFF_QK_WIDTH + DIFF_V_WIDTH + 2 * D_MODEL
MEM_LEN = 256
XA_HEADS = 4
XA_HEAD_DIM = 128
XA_WIDTH = XA_HEADS * XA_HEAD_DIM
N_GROUPS = 4
EXPERTS_PER_GROUP = 8
N_EXPERTS = N_GROUPS * EXPERTS_PER_GROUP
TOP_K = 2
D_EXPERT = 512
NORM_EPS = 1e-6
SUBLN_EPS = 1e-5
NEG_INF = -1e30
LAMBDA_INIT = 0.8 - 0.6 * math.exp(-0.3 * 0)
LOG2E = math.log2(math.e)

LANES = 128
HP_TILES = D_MODEL // 2 // LANES
YS_TILES = D_MODEL // LANES
VMEM_LIMIT = 56 * 1024 * 1024

INPROJ_TM = 1024
INPROJ_TN = 1024
NORM_CHUNK = 128
DIFF_TQ = 512
DIFF_SUB = 256
MERGE_TM = 256
XA_TM = 256
RANK_TM = 512
MOE_TM = 512
ROW_TM = 256


def _cparams(sem):
    return pltpu.CompilerParams(dimension_semantics=sem, vmem_limit_bytes=VMEM_LIMIT)


def _rms_scale(x, g, eps):
    ms = jnp.mean(x * x, axis=-1, keepdims=True)
    return x * lax.rsqrt(ms + eps) * g


def _dot_nt(a, b):
    return lax.dot_general(a, b, (((1,), (1,)), ((), ())), preferred_element_type=F32)


def _two_group_specs(tm, width, n_first):
    first = pl.BlockSpec((tm, width), lambda i, *_: (jnp.minimum(i, n_first - 1), 0))
    second = pl.BlockSpec((tm, width), lambda i, *_: (jnp.maximum(i - n_first, 0), 0))
    return first, second


def _inproj_kernel(n_first, xp_ref, xs_ref, g_ref, w_ref, cs_ref, o_ref, h_ref):
    def normalise(x_ref):
        def chunk(c, carry):
            rows = pl.ds(pl.multiple_of(c * NORM_CHUNK, NORM_CHUNK), NORM_CHUNK)
            h_ref[rows, :] = _rms_scale(x_ref[rows, :], g_ref[...], NORM_EPS).astype(BF16)
            return carry
        lax.fori_loop(0, x_ref.shape[0] // NORM_CHUNK, chunk, 0)

    first_col = pl.program_id(1) == 0
    in_first = pl.program_id(0) < n_first
    pl.when(first_col & in_first)(lambda: normalise(xp_ref))
    pl.when(first_col & jnp.logical_not(in_first))(lambda: normalise(xs_ref))
    acc = jnp.dot(h_ref[...], w_ref[...], preferred_element_type=F32)
    o_ref[...] = (acc * cs_ref[...]).astype(o_ref.dtype)


def _inproj(xp, xs, g, w, colscale):
    d = xp.shape[1]
    t = xp.shape[0] + xs.shape[0]
    n = w.shape[1]
    tm = INPROJ_TM
    n_first = xp.shape[0] // tm
    spec_p, spec_s = _two_group_specs(tm, d, n_first)
    return pl.pallas_call(
        functools.partial(_inproj_kernel, n_first),
        grid=(t // tm, n // INPROJ_TN),
        in_specs=[spec_p, spec_s,
                  pl.BlockSpec((1, d), lambda i, j: (0, 0)),
                  pl.BlockSpec((d, INPROJ_TN), lambda i, j: (0, j)),
                  pl.BlockSpec((1, INPROJ_TN), lambda i, j: (0, j))],
        out_specs=pl.BlockSpec((tm, INPROJ_TN), lambda i, j: (i, j)),
        out_shape=jax.ShapeDtypeStruct((t, n), BF16),
        scratch_shapes=[pltpu.VMEM((tm, d), BF16)],
        compiler_params=_cparams(("parallel", "arbitrary")),
        name="inproj",
    )(xp, xs, g, w, colscale)


NA_QROWS = 4
NA_KROWS = 12
NA_TILES = ROWS // NA_QROWS


def _na_tile_key_row(ti):
    return min(max(NA_QROWS * ti - NA_KH // 2, 0), ROWS - NA_KROWS)


def _na_bias_table(rpb):
    cols = jnp.arange(GRID_W)
    cstart = jnp.clip(cols - NA_KW // 2, 0, GRID_W - NA_KW)
    col_ok = (cols[None, :] >= cstart[:, None]) & (cols[None, :] < cstart[:, None] + NA_KW)
    dc = jnp.clip(cols[None, :] - cols[:, None], 1 - NA_KW, NA_KW - 1) + NA_KW - 1
    pick = (dc[None] == jnp.arange(2 * NA_KW - 1)[:, None, None]).astype(F32)
    t = jnp.einsum("hdx,xck->hdck", rpb.astype(F32), pick, precision=lax.Precision.HIGHEST)
    t = jnp.where(col_ok[None, None], t, NEG_INF)
    tiles = jnp.array([0, 1, NA_TILES - 1])
    r = NA_QROWS * tiles[:, None, None] + jnp.arange(NA_QROWS)[None, :, None]
    ks = jnp.clip(NA_QROWS * tiles - NA_KH // 2, 0, ROWS - NA_KROWS)[:, None, None]
    kr = ks + jnp.arange(NA_KROWS)[None, None, :]
    rs = jnp.clip(r - NA_KH // 2, 0, ROWS - NA_KH)
    valid = (kr >= rs) & (kr < rs + NA_KH)
    dr = jnp.clip(kr - r + NA_KH - 1, 0, 2 * NA_KH - 2)
    b = jnp.where(valid[None, :, :, :, None, None], t[:, dr], NEG_INF)
    return b.transpose(0, 1, 2, 4, 3, 5).reshape(NA_HEADS, 3, NA_QROWS * GRID_W, NA_KROWS * GRID_W)


def _na_kernel(q_ref, k_ref, v_ref, bias_ref, o_ref):
    scale = NA_HEAD_DIM ** -0.5
    tq = NA_QROWS * GRID_W
    tk = NA_KROWS * GRID_W
    for ti in range(NA_TILES):
        variant = 0 if ti == 0 else (2 if ti == NA_TILES - 1 else 1)
        k0 = _na_tile_key_row(ti) * GRID_W
        qrows = slice(ti * tq, (ti + 1) * tq)
        krows = slice(k0, k0 + tk)
        s = _dot_nt(q_ref[0, qrows, :], k_ref[0, krows, :]) * scale + bias_ref[0, variant]
        m = jnp.max(s, axis=-1, keepdims=True)
        e = jnp.exp(s - m)
        l = jnp.sum(e, axis=-1, keepdims=True)
        o = jnp.dot(e.astype(BF16), v_ref[0, krows, :], preferred_element_type=F32)
        o_ref[0, qrows, :] = (o / l).astype(o_ref.dtype)


def _na_attention(proj3, bias):
    b = proj3.shape[0]
    hd = NA_HEAD_DIM
    return pl.pallas_call(
        _na_kernel,
        grid=(NA_HEADS, b),
        in_specs=[pl.BlockSpec((1, SEQ, hd), lambda h, i: (i, 0, h)),
                  pl.BlockSpec((1, SEQ, hd), lambda h, i: (i, 0, NA_HEADS + h)),
                  pl.BlockSpec((1, SEQ, hd), lambda h, i: (i, 0, 2 * NA_HEADS + h)),
                  pl.BlockSpec((1,) + bias.shape[1:], lambda h, i: (h, 0, 0, 0))],
        out_specs=pl.BlockSpec((1, SEQ, hd), lambda h, i: (i, 0, h)),
        out_shape=jax.ShapeDtypeStruct((b, SEQ, NA_WIDTH), BF16),
        compiler_params=_cparams(("arbitrary", "arbitrary")),
        name="na_attn",
    )(proj3, proj3, proj3, bias)


def _alibi_table(tq):
    slopes = jnp.exp2(-8.0 * jnp.arange(1, DIFF_HEADS + 1, dtype=F32) / DIFF_HEADS)
    r = jnp.arange(tq)[:, None]
    x = jnp.arange(2 * SEQ - tq)[None, :]
    dist = jnp.abs(r - x + (SEQ - tq)).astype(F32)
    return (slopes * LOG2E)[:, None, None] * dist[None]


def _diff_kernel(lq1_ref, lk1_ref, lq2_ref, lk2_ref, q_ref, k_ref, v_ref, g_ref, alibi_ref, o_ref):
    qi = pl.program_id(2)
    ts = alibi_ref.shape[1]
    nsub = q_ref.shape[1] // ts
    lam = (jnp.exp(jnp.sum(lq1_ref[...] * lk1_ref[...], axis=-1, keepdims=True))
           - jnp.exp(jnp.sum(lq2_ref[...] * lk2_ref[...], axis=-1, keepdims=True)) + LAMBDA_INIT)
    k = k_ref[0]

    def scores(u):
        q = q_ref[0, u * ts:(u + 1) * ts, :]
        return [_dot_nt(q[:, mi * DIFF_QK_DIM:(mi + 1) * DIFF_QK_DIM], k[:, mi * DIFF_QK_DIM:(mi + 1) * DIFF_QK_DIM])
                for mi in range(2)]

    def expo(s, bias):
        s = s - bias
        m = jnp.max(s, axis=-1, keepdims=True)
        e = jnp.exp2(s - m)
        return e, jnp.sum(e, axis=-1, keepdims=True)

    pending = scores(0)
    for u in range(nsub):
        s0, s1 = pending
        if u + 1 < nsub:
            pending = scores(u + 1)
        off = (SEQ - ts) - (qi * nsub + u) * ts
        bias = alibi_ref[0, :, pl.ds(pl.multiple_of(off, ts), SEQ)]
        e0, l0 = expo(s0, bias)
        e1, l1 = expo(s1, bias)
        a = (e0 - e1 * (lam * l0 / l1)).astype(BF16)
        o = jnp.dot(a, v_ref[0], preferred_element_type=F32) / l0
        o = _rms_scale(o, g_ref[...], SUBLN_EPS) * (1.0 - LAMBDA_INIT)
        o_ref[0, u * ts:(u + 1) * ts, :] = o.astype(o_ref.dtype)


def _diff_attention(proj3, lq1, lk1, lq2, lk2, subln_g):
    b = proj3.shape[0]
    tq = DIFF_TQ
    qoff = 3 * NA_WIDTH // LANES
    koff = qoff + DIFF_QK_WIDTH // LANES
    voff = koff + DIFF_QK_WIDTH // LANES
    vec = lambda n: pl.BlockSpec((1, n), lambda h, i, j: (0, 0))
    return pl.pallas_call(
        _diff_kernel,
        grid=(DIFF_HEADS, b, SEQ // tq),
        in_specs=[vec(DIFF_QK_DIM), vec(DIFF_QK_DIM), vec(DIFF_QK_DIM), vec(DIFF_QK_DIM),
                  pl.BlockSpec((1, tq, LANES), lambda h, i, j: (i, j, qoff + h)),
                  pl.BlockSpec((1, SEQ, LANES), lambda h, i, j: (i, 0, koff + h)),
                  pl.BlockSpec((1, SEQ, LANES), lambda h, i, j: (i, 0, voff + h)),
                  vec(DIFF_V_DIM),
                  pl.BlockSpec((1, DIFF_SUB, 2 * SEQ - DIFF_SUB), lambda h, i, j: (h, 0, 0))],
        out_specs=pl.BlockSpec((1, tq, DIFF_V_DIM), lambda h, i, j: (i, j, h)),
        out_shape=jax.ShapeDtypeStruct((b, SEQ, DIFF_V_WIDTH), BF16),
        compiler_params=_cparams(("arbitrary", "arbitrary", "arbitrary")),
        name="diff_attn",
    )(lq1, lk1, lq2, lk2, proj3, proj3, proj3, subln_g, _alibi_table(DIFF_SUB))


def _merge_kernel(n_first, a_ref, b_ref, gna_ref, gdf_ref, xp_ref, xs_ref, wna_ref, wdf_ref, wout_ref, o_ref):
    pa = jnp.dot(a_ref[...], wna_ref[...], preferred_element_type=F32)
    pb = jnp.dot(b_ref[...], wdf_ref[...], preferred_element_type=F32)
    merged = (jax.nn.sigmoid(gna_ref[...].astype(F32)) * pa
              + jax.nn.sigmoid(gdf_ref[...].astype(F32)) * pb)
    delta = jnp.dot(merged.astype(BF16), wout_ref[...], preferred_element_type=F32)
    in_first = pl.program_id(0) < n_first

    @pl.when(in_first)
    def _():
        o_ref[...] = xp_ref[...] + delta

    @pl.when(jnp.logical_not(in_first))
    def _():
        o_ref[...] = xs_ref[...] + delta


def _merge(a, b, proj, xp, xs, wna, wdf, wout):
    t = xp.shape[0] + xs.shape[0]
    tm = MERGE_TM
    gna_blk = (3 * NA_WIDTH + 2 * DIFF_QK_WIDTH + DIFF_V_WIDTH) // D_MODEL
    const = lambda shape: pl.BlockSpec(shape, lambda i: (0, 0))
    n_first = xp.shape[0] // tm
    spec_p, spec_s = _two_group_specs(tm, D_MODEL, n_first)
    return pl.pallas_call(
        functools.partial(_merge_kernel, n_first),
        grid=(t // tm,),
        in_specs=[pl.BlockSpec((tm, NA_WIDTH), lambda i: (i, 0)),
                  pl.BlockSpec((tm, DIFF_V_WIDTH), lambda i: (i, 0)),
                  pl.BlockSpec((tm, D_MODEL), lambda i: (i, gna_blk)),
                  pl.BlockSpec((tm, D_MODEL), lambda i: (i, gna_blk + 1)),
                  spec_p, spec_s,
                  const((NA_WIDTH, D_MODEL)), const((DIFF_V_WIDTH, D_MODEL)), const((D_MODEL, D_MODEL))],
        out_specs=pl.BlockSpec((tm, D_MODEL), lambda i: (i, 0)),
        out_shape=jax.ShapeDtypeStruct((t, D_MODEL), F32),
        compiler_params=_cparams(("parallel",)),
        name="merge_outproj",
    )(a, b, proj, proj, xp, xs, wna, wdf, wout)


def _memkv_kernel(m_ref, g_ref, w_ref, o_ref):
    h = _rms_scale(m_ref[...], g_ref[...], NORM_EPS).astype(BF16)
    o_ref[...] = jnp.dot(h, w_ref[...], preferred_element_type=F32).astype(o_ref.dtype)


def _memkv(mem, g, w):
    t = mem.shape[0]
    return pl.pallas_call(
        _memkv_kernel,
        grid=(t // MEM_LEN,),
        in_specs=[pl.BlockSpec((MEM_LEN, D_MODEL), lambda i: (i, 0)),
                  pl.BlockSpec((1, D_MODEL), lambda i: (0, 0)),
                  pl.BlockSpec((D_MODEL, 2 * XA_WIDTH), lambda i: (0, 0))],
        out_specs=pl.BlockSpec((MEM_LEN, 2 * XA_WIDTH), lambda i: (i, 0)),
        out_shape=jax.ShapeDtypeStruct((t, 2 * XA_WIDTH), BF16),
        compiler_params=_cparams(("parallel",)),
        name="mem_kv",
    )(mem, g, w)


def _route(logits):
    lane = lax.broadcasted_iota(I32, logits.shape, 1).astype(F32)
    ninf = -jnp.inf
    big = float(LANES)
    gl = jnp.where(lane < N_GROUPS, logits, ninf)
    gmax = jnp.max(gl, axis=-1, keepdims=True)
    g = jnp.min(jnp.where(gl == gmax, lane, big), axis=-1, keepdims=True)
    pg = 1.0 / jnp.sum(jnp.exp(gl - gmax), axis=-1, keepdims=True)
    lo = N_GROUPS + EXPERTS_PER_GROUP * g
    el = jnp.where((lane >= lo) & (lane < lo + EXPERTS_PER_GROUP), logits, ninf)
    v1 = jnp.max(el, axis=-1, keepdims=True)
    i1 = jnp.min(jnp.where(el == v1, lane, big), axis=-1, keepdims=True)
    el2 = jnp.where(lane == i1, ninf, el)
    v2 = jnp.max(el2, axis=-1, keepdims=True)
    i2 = jnp.min(jnp.where(el2 == v2, lane, big), axis=-1, keepdims=True)
    t = jnp.exp(v2 - v1)
    den = 1.0 + t
    gate1 = pg * (1.0 / den)
    gate2 = pg * (t / den)
    gates = jnp.where(lane == 0.0, gate1, jnp.where(lane == 1.0, gate2, 0.0))
    eids = jnp.where(lane == 0.0, i1 - N_GROUPS, jnp.where(lane == 1.0, i2 - N_GROUPS, 0.0)).astype(I32)
    return gates, eids


def _pack_halves(h):
    half = h.shape[1] // 2
    hi = pltpu.bitcast(h[:, :half].astype(BF16).astype(F32), U32)
    lo = pltpu.bitcast(h[:, half:].astype(BF16).astype(F32), U32)
    return hi | (lo >> 16)


def _unpack_halves(w):
    hi = pltpu.bitcast(w & jnp.uint32(0xFFFF0000), F32).astype(BF16)
    lo = pltpu.bitcast(w << 16, F32).astype(BF16)
    return jnp.concatenate([hi, lo], axis=1)


def _store_row_tiles(ref, x):
    n, width = x.shape
    c = width // LANES
    for j in range(c):
        ref[pl.ds(j, n, stride=c), :] = x[:, j * LANES:(j + 1) * LANES]


def _load_row_tiles(ref, n, c):
    return jnp.concatenate([ref[pl.ds(j, n, stride=c), :] for j in range(c)], axis=1)


def _xattn_kernel(x_ref, kv_ref, gx_ref, wq_ref, wo_ref, gf_ref, wr_ref, br_ref,
                  x2_ref, hp_ref, gate_ref, eid_ref):
    x1 = x_ref[...]
    hq = _rms_scale(x1, gx_ref[...], NORM_EPS).astype(BF16)
    q = jnp.dot(hq, wq_ref[...], preferred_element_type=F32).astype(BF16)
    scale = XA_HEAD_DIM ** -0.5
    outs = []
    for h in range(XA_HEADS):
        sl = slice(h * XA_HEAD_DIM, (h + 1) * XA_HEAD_DIM)
        vsl = slice(XA_WIDTH + h * XA_HEAD_DIM, XA_WIDTH + (h + 1) * XA_HEAD_DIM)
        s = _dot_nt(q[:, sl], kv_ref[:, sl]) * scale
        m = jnp.max(s, axis=-1, keepdims=True)
        e = jnp.exp(s - m)
        l = jnp.sum(e, axis=-1, keepdims=True)
        o = jnp.dot(e.astype(BF16), kv_ref[:, vsl], preferred_element_type=F32)
        outs.append((o / l).astype(BF16))
    o = jnp.concatenate(outs, axis=1)
    x2 = x1 + jnp.dot(o, wo_ref[...], preferred_element_type=F32)
    x2_ref[...] = x2
    h3 = _rms_scale(x2, gf_ref[...], NORM_EPS)
    _store_row_tiles(hp_ref, _pack_halves(h3))
    logits = jnp.dot(h3.astype(BF16), wr_ref[...], preferred_element_type=F32) + br_ref[...]
    gates, eids = _route(logits)
    gate_ref[...] = gates
    eid_ref[...] = eids


def _xattn(x1, kv, gx, wq, wo, gf, wr, br):
    t = x1.shape[0]
    tm = XA_TM
    per_batch = SEQ // tm
    const = lambda shape: pl.BlockSpec(shape, lambda i: (0, 0))
    tile = lambda n: pl.BlockSpec((tm, n), lambda i: (i, 0))
    return pl.pallas_call(
        _xattn_kernel,
        grid=(t // tm,),
        in_specs=[tile(D_MODEL),
                  pl.BlockSpec((MEM_LEN, 2 * XA_WIDTH), lambda i: (i // per_batch, 0)),
                  const((1, D_MODEL)), const((D_MODEL, XA_WIDTH)), const((XA_WIDTH, D_MODEL)),
                  const((1, D_MODEL)), const((D_MODEL, LANES)), const((1, LANES))],
        out_specs=[tile(D_MODEL), pl.BlockSpec((tm * HP_TILES, LANES), lambda i: (i, 0)), tile(LANES), tile(LANES)],
        out_shape=[jax.ShapeDtypeStruct((t, D_MODEL), F32),
                   jax.ShapeDtypeStruct((t * HP_TILES, LANES), U32),
                   jax.ShapeDtypeStruct((t, LANES), F32),
                   jax.ShapeDtypeStruct((t, LANES), I32)],
        compiler_params=_cparams(("parallel",)),
        name="xattn_router",
    )(x1, kv, gx, wq, wo, gf, wr, br)


def _rank_kernel(eid_ref, rank_ref, cnt_ref, carry_ref):
    @pl.when(pl.program_id(0) == 0)
    def _():
        carry_ref[...] = jnp.zeros_like(carry_ref)

    tm = eid_ref.shape[0]
    eid = eid_ref[...]
    lane = lax.broadcasted_iota(I32, (tm, LANES), 1)
    m1 = lane == eid[:, 0:1]
    m2 = lane == eid[:, 1:2]
    onehot = jnp.where(m1, 1.0, jnp.where(m2, 1.0, 0.0))
    r = lax.broadcasted_iota(I32, (tm, tm), 0)
    c = lax.broadcasted_iota(I32, (tm, tm), 1)
    tri = jnp.where(c < r, 1.0, 0.0).astype(BF16)
    before = jnp.dot(tri, onehot.astype(BF16), preferred_element_type=F32) + carry_ref[0:1, :]
    r1 = jnp.sum(jnp.where(m1, before, 0.0), axis=-1, keepdims=True)
    r2 = jnp.sum(jnp.where(m2, before, 0.0), axis=-1, keepdims=True)
    rank_ref[...] = jnp.where(lane == 0, r1, jnp.where(lane == 1, r2, 0.0)).astype(I32)
    carry_ref[...] = carry_ref[...] + jnp.sum(onehot, axis=0, keepdims=True)
    cnt_ref[...] = carry_ref[...]


def _rank(eid):
    t = eid.shape[0]
    tm = RANK_TM
    return pl.pallas_call(
        _rank_kernel,
        grid=(t // tm,),
        in_specs=[pl.BlockSpec((tm, LANES), lambda i: (i, 0))],
        out_specs=[pl.BlockSpec((tm, LANES), lambda i: (i, 0)),
                   pl.BlockSpec((8, LANES), lambda i: (0, 0))],
        out_shape=[jax.ShapeDtypeStruct((t, LANES), I32),
                   jax.ShapeDtypeStruct((8, LANES), F32)],
        scratch_shapes=[pltpu.VMEM((8, LANES), F32)],
        compiler_params=_cparams(("arbitrary",)),
        name="expert_rank",
    )(eid)


MOE_DOWN_CHUNKS = 4


def _moe_kernel(bexp_ref, nused_ref, tok_next_ref, tok_first_ref, hp_ref, wg_ref, wu_ref, wd_ref, o_ref,
                xbuf, wgb, wub, wdb, sem):
    i = pl.program_id(0)
    nused = nused_ref[0]
    tm = xbuf.shape[1] // HP_TILES
    slot = lax.rem(i, 2)

    def row_copy(tok, buf_slot, r):
        src = hp_ref.at[pl.ds(pl.multiple_of(tok * HP_TILES, HP_TILES), HP_TILES)]
        return pltpu.make_async_copy(src, xbuf.at[buf_slot, pl.ds(r * HP_TILES, HP_TILES)], sem.at[buf_slot])

    def start_rows(tok_ref, buf_slot, r0, r1):
        for r in range(r0, r1):
            row_copy(tok_ref[0, 0, r], buf_slot, r).start(priority=r % 2)

    def wait_rows(buf_slot):
        for r in range(tm):
            row_copy(0, buf_slot, r).wait()

    @pl.when(i == 0)
    def _():
        start_rows(tok_first_ref, 0, 0, tm)

    @pl.when((i < nused) & ((i == 0) | (bexp_ref[i] != bexp_ref[jnp.maximum(i - 1, 0)])))
    def _():
        wgb[...] = wg_ref[0].astype(BF16)
        wub[...] = wu_ref[0].astype(BF16)
        wdb[...] = wd_ref[0].astype(BF16)

    @pl.when(i < nused)
    def _():
        wait_rows(slot)
        x = _unpack_halves(_load_row_tiles(xbuf.at[slot], tm, HP_TILES))
        step = tm // (MOE_DOWN_CHUNKS + 2)
        bounds = [k * step for k in range(MOE_DOWN_CHUNKS + 2)] + [tm]
        nxt = 1 - slot
        start_rows(tok_next_ref, nxt, bounds[0], bounds[1])
        hg = jnp.dot(x, wgb[...], preferred_element_type=F32)
        start_rows(tok_next_ref, nxt, bounds[1], bounds[2])
        hu = jnp.dot(x, wub[...], preferred_element_type=F32)
        hdn = ((hg * jax.nn.sigmoid(hg)) * hu).astype(BF16)
        cw = D_MODEL // MOE_DOWN_CHUNKS
        for c in range(MOE_DOWN_CHUNKS):
            start_rows(tok_next_ref, nxt, bounds[2 + c], bounds[3 + c])
            cols = slice(c * cw, (c + 1) * cw)
            y = jnp.dot(hdn, wdb[:, cols], preferred_element_type=F32)
            for j in range(cw // LANES):
                o_ref[pl.ds(c * (cw // LANES) + j, tm, stride=YS_TILES), :] = y[:, j * LANES:(j + 1) * LANES]

    @pl.when(i == nused - 1)
    def _():
        wait_rows(1 - slot)

    @pl.when(i >= nused)
    def _():
        o_ref[...] = jnp.zeros_like(o_ref)


def _moe(bexp, nused, slot_tok, hp, wg, wu, wd):
    tm = MOE_TM
    n_slots = slot_tok.shape[0]
    n_blocks = n_slots // tm
    tok3 = slot_tok.reshape(n_blocks, 1, tm)
    smem_blk = lambda index_map: pl.BlockSpec((1, 1, tm), index_map, memory_space=pltpu.SMEM)
    grid_spec = pltpu.PrefetchScalarGridSpec(
        num_scalar_prefetch=2,
        grid=(n_blocks,),
        in_specs=[smem_blk(lambda i, be, nu: (jnp.minimum(i + 1, nu[0] - 1), 0, 0)),
                  smem_blk(lambda i, be, nu: (0, 0, 0)),
                  pl.BlockSpec(memory_space=pl.ANY),
                  pl.BlockSpec((1, D_MODEL, D_EXPERT), lambda i, be, nu: (be[i], 0, 0)),
                  pl.BlockSpec((1, D_MODEL, D_EXPERT), lambda i, be, nu: (be[i], 0, 0)),
                  pl.BlockSpec((1, D_EXPERT, D_MODEL), lambda i, be, nu: (be[i], 0, 0))],
        out_specs=pl.BlockSpec((tm * YS_TILES, LANES), lambda i, be, nu: (i, 0)),
        scratch_shapes=[pltpu.VMEM((2, tm * HP_TILES, LANES), U32),
                        pltpu.VMEM((D_MODEL, D_EXPERT), BF16), pltpu.VMEM((D_MODEL, D_EXPERT), BF16),
                        pltpu.VMEM((D_EXPERT, D_MODEL), BF16),
                        pltpu.SemaphoreType.DMA((2,))],
    )
    return pl.pallas_call(
        _moe_kernel,
        grid_spec=grid_spec,
        out_shape=jax.ShapeDtypeStruct((n_slots * YS_TILES, LANES), F32),
        compiler_params=_cparams(("arbitrary",)),
        name="moe_experts",
    )(bexp, nused, tok3, tok3, hp, wg, wu, wd)


def _combine_kernel(dest_ref, x_ref, gate_ref, g_ref, ys_ref, o_ref, y1_ref, y2_ref, sem):
    tm = x_ref.shape[0]
    bufs = (y1_ref, y2_ref)

    def row_copy(r, k, d):
        src = ys_ref.at[pl.ds(pl.multiple_of(d * YS_TILES, YS_TILES), YS_TILES)]
        return pltpu.make_async_copy(src, bufs[k].at[pl.ds(pl.multiple_of(r * YS_TILES, YS_TILES), YS_TILES)], sem)

    def issue(r, carry):
        for k in range(TOP_K):
            row_copy(r, k, dest_ref[0, 0, TOP_K * r + k]).start(priority=k)
        return carry

    def drain(r, carry):
        for k in range(TOP_K):
            row_copy(r, k, dest_ref[0, 0, TOP_K * r + k]).wait()
        return carry

    lax.fori_loop(0, tm, issue, 0)
    lax.fori_loop(0, tm, drain, 0)
    gates = gate_ref[...]
    moe = (gates[:, 0:1] * _load_row_tiles(y1_ref, tm, YS_TILES)
           + gates[:, 1:2] * _load_row_tiles(y2_ref, tm, YS_TILES))
    o_ref[...] = _rms_scale(x_ref[...] + moe, g_ref[...], NORM_EPS)


def _combine(dest, x2, gates, g, ys, row0, n_rows):
    tm = ROW_TM
    t = x2.shape[0]
    dest3 = dest.reshape(t // tm, 1, TOP_K * tm)
    blk0 = row0 // tm
    return pl.pallas_call(
        _combine_kernel,
        grid=(n_rows // tm,),
        in_specs=[pl.BlockSpec((1, 1, TOP_K * tm), lambda i: (blk0 + i, 0, 0), memory_space=pltpu.SMEM),
                  pl.BlockSpec((tm, D_MODEL), lambda i: (blk0 + i, 0)),
                  pl.BlockSpec((tm, LANES), lambda i: (blk0 + i, 0)),
                  pl.BlockSpec((1, D_MODEL), lambda i: (0, 0)),
                  pl.BlockSpec(memory_space=pl.ANY)],
        out_specs=pl.BlockSpec((tm, D_MODEL), lambda i: (i, 0)),
        out_shape=jax.ShapeDtypeStruct((n_rows, D_MODEL), F32),
        scratch_shapes=[pltpu.VMEM((tm * YS_TILES, LANES), F32), pltpu.VMEM((tm * YS_TILES, LANES), F32),
                        pltpu.SemaphoreType.DMA(())],
        compiler_params=_cparams(("arbitrary",)),
        name="moe_combine",
    )(dest3, x2, gates, g, ys)


def _routing_plan(eid, rank, counts):
    cnt = counts[0, :N_EXPERTS].astype(I32)
    padded = (cnt + MOE_TM - 1) // MOE_TM * MOE_TM
    pends = jnp.cumsum(padded)
    pstarts = pends - padded
    dest = pstarts[eid[:, :TOP_K]] + rank[:, :TOP_K]
    t = eid.shape[0]
    n_blocks = t * TOP_K // MOE_TM + N_EXPERTS
    blk0 = jnp.arange(n_blocks, dtype=I32) * MOE_TM
    bexp = jnp.minimum(jnp.sum(pends[None, :] <= blk0[:, None], axis=1), N_EXPERTS - 1).astype(I32)
    nused = (pends[-1:] // MOE_TM).astype(I32)
    dest = dest.reshape(-1).astype(I32)
    tok = jnp.arange(t * TOP_K, dtype=I32) // TOP_K
    slot_tok = jnp.zeros((n_blocks * MOE_TM,), I32).at[dest].set(tok, unique_indices=True)
    return dest, slot_tok, bexp, nused


def kernel(x_prompt, x_sample, mem_prompt, mem_sample, mix_norm_g, w_in, na_rpb, diff_lambda_q1, diff_lambda_k1, diff_lambda_q2, diff_lambda_k2, diff_subln_g, w_branch_na, w_branch_diff, w_out, xa_norm_g, mem_norm_g, xa_w_q, xa_w_kv, xa_w_o, ffn_norm_g, router_group_w, router_group_b, router_expert_w, router_expert_b, w_gate, w_up, w_down, final_norm_g):
    nb_p, nb_s = x_prompt.shape[0], x_sample.shape[0]
    nb = nb_p + nb_s
    t = nb * SEQ
    xp = x_prompt.reshape(nb_p * SEQ, D_MODEL)
    xs_in = x_sample.reshape(nb_s * SEQ, D_MODEL)
    mem = jnp.concatenate([mem_prompt, mem_sample], axis=0).reshape(nb * MEM_LEN, D_MODEL)
    row = lambda v: v.reshape(1, -1).astype(F32)
    bf = lambda w: w.astype(BF16)

    qcol = 3 * NA_WIDTH
    colscale = jnp.ones((1, IN_COLS), F32).at[:, qcol:qcol + DIFF_QK_WIDTH].set(DIFF_QK_DIM ** -0.5 * LOG2E)
    proj = _inproj(xp, xs_in, row(mix_norm_g[0]), bf(w_in[0]), colscale)
    proj3 = proj.reshape(nb, SEQ, IN_COLS)
    a = _na_attention(proj3, _na_bias_table(na_rpb[0]))
    b = _diff_attention(proj3, row(diff_lambda_q1[0]), row(diff_lambda_k1[0]),
                        row(diff_lambda_q2[0]), row(diff_lambda_k2[0]), row(diff_subln_g[0]))
    x1 = _merge(a.reshape(t, NA_WIDTH), b.reshape(t, DIFF_V_WIDTH), proj, xp, xs_in,
                bf(w_branch_na[0]), bf(w_branch_diff[0]), bf(w_out[0]))
    kv = _memkv(mem, row(mem_norm_g[0]), bf(xa_w_kv[0]))
    w_router = jnp.zeros((D_MODEL, LANES), F32)
    w_router = w_router.at[:, :N_GROUPS].set(router_group_w[0]).at[:, N_GROUPS:N_GROUPS + N_EXPERTS].set(router_expert_w[0])
    b_router = jnp.zeros((1, LANES), F32)
    b_router = b_router.at[0, :N_GROUPS].set(router_group_b[0]).at[0, N_GROUPS:N_GROUPS + N_EXPERTS].set(router_expert_b[0])
    x2, hp, gates, eid = _xattn(x1, kv, row(xa_norm_g[0]), bf(xa_w_q[0]), bf(xa_w_o[0]),
                                row(ffn_norm_g[0]), bf(w_router), b_router)
    rank, counts = _rank(eid)
    dest, slot_tok, bexp, nused = _routing_plan(eid, rank, counts)
    ys = _moe(bexp, nused, slot_tok, hp, w_gate[0], w_up[0], w_down[0])
    fg = row(final_norm_g)
    y_p = _combine(dest, x2, gates, fg, ys, 0, nb_p * SEQ).reshape(nb_p, SEQ, D_MODEL)
    y_s = _combine(dest, x2, gates, fg, ys, nb_p * SEQ, nb_s * SEQ).reshape(nb_s, SEQ, D_MODEL)
    return (y_p, y_s)
```

```python
import functools
import math

import jax
import jax.numpy as jnp
from jax import lax
from jax.experimental import pallas as pl
from jax.experimental.pallas import tpu as pltpu

F32 = jnp.float32
BF16 = jnp.bfloat16
I32 = jnp.int32
U32 = jnp.uint32

D_MODEL = 2048
SEQ = 2048
GRID_W = 64
ROWS = SEQ // GRID_W
NA_HEADS = 8
NA_HEAD_DIM = 128
NA_WIDTH = NA_HEADS * NA_HEAD_DIM
NA_KH = 8
NA_KW = 16
DIFF_HEADS = 8
DIFF_QK_DIM = 64
DIFF_V_DIM = 128
DIFF_QK_WIDTH = DIFF_HEADS * 2 * DIFF_QK_DIM
DIFF_V_WIDTH = DIFF_HEADS * DIFF_V_DIM
IN_COLS = 3 * NA_WIDTH + 2 * DIFF_QK_WIDTH + DIFF_V_WIDTH + 2 * D_MODEL
MEM_LEN = 256
XA_HEADS = 4
XA_HEAD_DIM = 128
XA_WIDTH = XA_HEADS * XA_HEAD_DIM
N_GROUPS = 4
EXPERTS_PER_GROUP = 8
N_EXPERTS = N_GROUPS * EXPERTS_PER_GROUP
TOP_K = 2
D_EXPERT = 512
NORM_EPS = 1e-6
SUBLN_EPS = 1e-5
NEG_INF = -1e30
LAMBDA_INIT = 0.8 - 0.6 * math.exp(-0.3 * 0)
LOG2E = math.log2(math.e)

LANES = 128
HP_TILES = D_MODEL // 2 // LANES
YS_TILES = D_MODEL // LANES
VMEM_LIMIT = 56 * 1024 * 1024

INPROJ_TM = 1024
INPROJ_TN = 1024
NORM_CHUNK = 128
DIFF_TQ = 512
DIFF_SUB = 256
MERGE_TM = 256
XA_TM = 256
RANK_TM = 512
MOE_TM = 512
ROW_TM = 256


def _cparams(sem):
    return pltpu.CompilerParams(dimension_semantics=sem, vmem_limit_bytes=VMEM_LIMIT)


def _rms_scale(x, g, eps):
    ms = jnp.mean(x * x, axis=-1, keepdims=True)
    return x * lax.rsqrt(ms + eps) * g


def _dot_nt(a, b):
    return lax.dot_general(a, b, (((1,), (1,)), ((), ())), preferred_element_type=F32)


def _two_group_specs(tm, width, n_first):
    first = pl.BlockSpec((tm, width), lambda i, *_: (jnp.minimum(i, n_first - 1), 0))
    second = pl.BlockSpec((tm, width), lambda i, *_: (jnp.maximum(i - n_first, 0), 0))
    return first, second


def _inproj_kernel(n_first, xp_ref, xs_ref, g_ref, w_ref, cs_ref, o_ref, h_ref):
    def normalise(x_ref):
        def chunk(c, carry):
            rows = pl.ds(pl.multiple_of(c * NORM_CHUNK, NORM_CHUNK), NORM_CHUNK)
            h_ref[rows, :] = _rms_scale(x_ref[rows, :], g_ref[...], NORM_EPS).astype(BF16)
            return carry
        lax.fori_loop(0, x_ref.shape[0] // NORM_CHUNK, chunk, 0)

    first_col = pl.program_id(1) == 0
    in_first = pl.program_id(0) < n_first
    pl.when(first_col & in_first)(lambda: normalise(xp_ref))
    pl.when(first_col & jnp.logical_not(in_first))(lambda: normalise(xs_ref))
    acc = jnp.dot(h_ref[...], w_ref[...], preferred_element_type=F32)
    o_ref[...] = (acc * cs_ref[...]).astype(o_ref.dtype)


def _inproj(xp, xs, g, w, colscale):
    d = xp.shape[1]
    t = xp.shape[0] + xs.shape[0]
    n = w.shape[1]
    tm = INPROJ_TM
    n_first = xp.shape[0] // tm
    spec_p, spec_s = _two_group_specs(tm, d, n_first)
    return pl.pallas_call(
        functools.partial(_inproj_kernel, n_first),
        grid=(t // tm, n // INPROJ_TN),
        in_specs=[spec_p, spec_s,
                  pl.BlockSpec((1, d), lambda i, j: (0, 0)),
                  pl.BlockSpec((d, INPROJ_TN), lambda i, j: (0, j)),
                  pl.BlockSpec((1, INPROJ_TN), lambda i, j: (0, j))],
        out_specs=pl.BlockSpec((tm, INPROJ_TN), lambda i, j: (i, j)),
        out_shape=jax.ShapeDtypeStruct((t, n), BF16),
        scratch_shapes=[pltpu.VMEM((tm, d), BF16)],
        compiler_params=_cparams(("parallel", "arbitrary")),
        name="inproj",
    )(xp, xs, g, w, colscale)


NA_QROWS = 4
NA_KROWS = 12
NA_TILES = ROWS // NA_QROWS


def _na_tile_key_row(ti):
    return min(max(NA_QROWS * ti - NA_KH // 2, 0), ROWS - NA_KROWS)


def _na_bias_table(rpb):
    cols = jnp.arange(GRID_W)
    cstart = jnp.clip(cols - NA_KW // 2, 0, GRID_W - NA_KW)
    col_ok = (cols[None, :] >= cstart[:, None]) & (cols[None, :] < cstart[:, None] + NA_KW)
    dc = jnp.clip(cols[None, :] - cols[:, None], 1 - NA_KW, NA_KW - 1) + NA_KW - 1
    pick = (dc[None] == jnp.arange(2 * NA_KW - 1)[:, None, None]).astype(F32)
    t = jnp.einsum("hdx,xck->hdck", rpb.astype(F32), pick, precision=lax.Precision.HIGHEST)
    t = jnp.where(col_ok[None, None], t, NEG_INF)
    tiles = jnp.array([0, 1, NA_TILES - 1])
    r = NA_QROWS * tiles[:, None, None] + jnp.arange(NA_QROWS)[None, :, None]
    ks = jnp.clip(NA_QROWS * tiles - NA_KH // 2, 0, ROWS - NA_KROWS)[:, None, None]
    kr = ks + jnp.arange(NA_KROWS)[None, None, :]
    rs = jnp.clip(r - NA_KH // 2, 0, ROWS - NA_KH)
    valid = (kr >= rs) & (kr < rs + NA_KH)
    dr = jnp.clip(kr - r + NA_KH - 1, 0, 2 * NA_KH - 2)
    b = jnp.where(valid[None, :, :, :, None, None], t[:, dr], NEG_INF)
    return b.transpose(0, 1, 2, 4, 3, 5).reshape(NA_HEADS, 3, NA_QROWS * GRID_W, NA_KROWS * GRID_W)


def _na_kernel(q_ref, k_ref, v_ref, bias_ref, o_ref):
    scale = NA_HEAD_DIM ** -0.5
    tq = NA_QROWS * GRID_W
    tk = NA_KROWS * GRID_W
    for ti in range(NA_TILES):
        variant = 0 if ti == 0 else (2 if ti == NA_TILES - 1 else 1)
        k0 = _na_tile_key_row(ti) * GRID_W
        qrows = slice(ti * tq, (ti + 1) * tq)
        krows = slice(k0, k0 + tk)
        s = _dot_nt(q_ref[0, qrows, :], k_ref[0, krows, :]) * scale + bias_ref[0, variant]
        m = jnp.max(s, axis=-1, keepdims=True)
        e = jnp.exp(s - m)
        l = jnp.sum(e, axis=-1, keepdims=True)
        o = jnp.dot(e.astype(BF16), v_ref[0, krows, :], preferred_element_type=F32)
        o_ref[0, qrows, :] = (o / l).astype(o_ref.dtype)


def _na_attention(proj3, bias):
    b = proj3.shape[0]
    hd = NA_HEAD_DIM
    return pl.pallas_call(
        _na_kernel,
        grid=(NA_HEADS, b),
        in_specs=[pl.BlockSpec((1, SEQ, hd), lambda h, i: (i, 0, h)),
                  pl.BlockSpec((1, SEQ, hd), lambda h, i: (i, 0, NA_HEADS + h)),
                  pl.BlockSpec((1, SEQ, hd), lambda h, i: (i, 0, 2 * NA_HEADS + h)),
                  pl.BlockSpec((1,) + bias.shape[1:], lambda h, i: (h, 0, 0, 0))],
        out_specs=pl.BlockSpec((1, SEQ, hd), lambda h, i: (i, 0, h)),
        out_shape=jax.ShapeDtypeStruct((b, SEQ, NA_WIDTH), BF16),
        compiler_params=_cparams(("arbitrary", "arbitrary")),
        name="na_attn",
    )(proj3, proj3, proj3, bias)


def _alibi_table(tq):
    slopes = jnp.exp2(-8.0 * jnp.arange(1, DIFF_HEADS + 1, dtype=F32) / DIFF_HEADS)
    r = jnp.arange(tq)[:, None]
    x = jnp.arange(2 * SEQ - tq)[None, :]
    dist = jnp.abs(r - x + (SEQ - tq)).astype(F32)
    return (slopes * LOG2E)[:, None, None] * dist[None]


def _diff_kernel(lq1_ref, lk1_ref, lq2_ref, lk2_ref, q_ref, k_ref, v_ref, g_ref, alibi_ref, o_ref):
    qi = pl.program_id(2)
    ts = alibi_ref.shape[1]
    nsub = q_ref.shape[1] // ts
    lam = (jnp.exp(jnp.sum(lq1_ref[...] * lk1_ref[...], axis=-1, keepdims=True))
           - jnp.exp(jnp.sum(lq2_ref[...] * lk2_ref[...], axis=-1, keepdims=True)) + LAMBDA_INIT)
    k = k_ref[0]

    def scores(u):
        q = q_ref[0, u * ts:(u + 1) * ts, :]
        return [_dot_nt(q[:, mi * DIFF_QK_DIM:(mi + 1) * DIFF_QK_DIM], k[:, mi * DIFF_QK_DIM:(mi + 1) * DIFF_QK_DIM])
                for mi in range(2)]

    def expo(s, bias):
        s = s - bias
        m = jnp.max(s, axis=-1, keepdims=True)
        e = jnp.exp2(s - m)
        return e, jnp.sum(e, axis=-1, keepdims=True)

    pending = scores(0)
    for u in range(nsub):
        s0, s1 = pending
        if u + 1 < nsub:
            pending = scores(u + 1)
        off = (SEQ - ts) - (qi * nsub + u) * ts
        bias = alibi_ref[0, :, pl.ds(pl.multiple_of(off, ts), SEQ)]
        e0, l0 = expo(s0, bias)
        e1, l1 = expo(s1, bias)
        a = (e0 - e1 * (lam * l0 / l1)).astype(BF16)
        o = jnp.dot(a, v_ref[0], preferred_element_type=F32) / l0
        o = _rms_scale(o, g_ref[...], SUBLN_EPS) * (1.0 - LAMBDA_INIT)
        o_ref[0, u * ts:(u + 1) * ts, :] = o.astype(o_ref.dtype)


def _diff_attention(proj3, lq1, lk1, lq2, lk2, subln_g):
    b = proj3.shape[0]
    tq = DIFF_TQ
    qoff = 3 * NA_WIDTH // LANES
    koff = qoff + DIFF_QK_WIDTH // LANES
    voff = koff + DIFF_QK_WIDTH // LANES
    vec = lambda n: pl.BlockSpec((1, n), lambda h, i, j: (0, 0))
    return pl.pallas_call(
        _diff_kernel,
        grid=(DIFF_HEADS, b, SEQ // tq),
        in_specs=[vec(DIFF_QK_DIM), vec(DIFF_QK_DIM), vec(DIFF_QK_DIM), vec(DIFF_QK_DIM),
                  pl.BlockSpec((1, tq, LANES), lambda h, i, j: (i, j, qoff + h)),
                  pl.BlockSpec((1, SEQ, LANES), lambda h, i, j: (i, 0, koff + h)),
                  pl.BlockSpec((1, SEQ, LANES), lambda h, i, j: (i, 0, voff + h)),
                  vec(DIFF_V_DIM),
                  pl.BlockSpec((1, DIFF_SUB, 2 * SEQ - DIFF_SUB), lambda h, i, j: (h, 0, 0))],
        out_specs=pl.BlockSpec((1, tq, DIFF_V_DIM), lambda h, i, j: (i, j, h)),
        out_shape=jax.ShapeDtypeStruct((b, SEQ, DIFF_V_WIDTH), BF16),
        compiler_params=_cparams(("arbitrary", "arbitrary", "arbitrary")),
        name="diff_attn",
    )(lq1, lk1, lq2, lk2, proj3, proj3, proj3, subln_g, _alibi_table(DIFF_SUB))


def _merge_kernel(n_first, a_ref, b_ref, gna_ref, gdf_ref, xp_ref, xs_ref, wna_ref, wdf_ref, wout_ref, o_ref):
    pa = jnp.dot(a_ref[...], wna_ref[...], preferred_element_type=F32)
    pb = jnp.dot(b_ref[...], wdf_ref[...], preferred_element_type=F32)
    merged = (jax.nn.sigmoid(gna_ref[...].astype(F32)) * pa
              + jax.nn.sigmoid(gdf_ref[...].astype(F32)) * pb)
    delta = jnp.dot(merged.astype(BF16), wout_ref[...], preferred_element_type=F32)
    in_first = pl.program_id(0) < n_first

    @pl.when(in_first)
    def _():
        o_ref[...] = xp_ref[...] + delta

    @pl.when(jnp.logical_not(in_first))
    def _():
        o_ref[...] = xs_ref[...] + delta


def _merge(a, b, proj, xp, xs, wna, wdf, wout):
    t = xp.shape[0] + xs.shape[0]
    tm = MERGE_TM
    gna_blk = (3 * NA_WIDTH + 2 * DIFF_QK_WIDTH + DIFF_V_WIDTH) // D_MODEL
    const = lambda shape: pl.BlockSpec(shape, lambda i: (0, 0))
    n_first = xp.shape[0] // tm
    spec_p, spec_s = _two_group_specs(tm, D_MODEL, n_first)
    return pl.pallas_call(
        functools.partial(_merge_kernel, n_first),
        grid=(t // tm,),
        in_specs=[pl.BlockSpec((tm, NA_WIDTH), lambda i: (i, 0)),
                  pl.BlockSpec((tm, DIFF_V_WIDTH), lambda i: (i, 0)),
                  pl.BlockSpec((tm, D_MODEL), lambda i: (i, gna_blk)),
                  pl.BlockSpec((tm, D_MODEL), lambda i: (i, gna_blk + 1)),
                  spec_p, spec_s,
                  const((NA_WIDTH, D_MODEL)), const((DIFF_V_WIDTH, D_MODEL)), const((D_MODEL, D_MODEL))],
        out_specs=pl.BlockSpec((tm, D_MODEL), lambda i: (i, 0)),
        out_shape=jax.ShapeDtypeStruct((t, D_MODEL), F32),
        compiler_params=_cparams(("parallel",)),
        name="merge_outproj",
    )(a, b, proj, proj, xp, xs, wna, wdf, wout)


def _memkv_kernel(m_ref, g_ref, w_ref, o_ref):
    h = _rms_scale(m_ref[...], g_ref[...], NORM_EPS).astype(BF16)
    o_ref[...] = jnp.dot(h, w_ref[...], preferred_element_type=F32).astype(o_ref.dtype)


def _memkv(mem, g, w):
    t = mem.shape[0]
    return pl.pallas_call(
        _memkv_kernel,
        grid=(t // MEM_LEN,),
        in_specs=[pl.BlockSpec((MEM_LEN, D_MODEL), lambda i: (i, 0)),
                  pl.BlockSpec((1, D_MODEL), lambda i: (0, 0)),
                  pl.BlockSpec((D_MODEL, 2 * XA_WIDTH), lambda i: (0, 0))],
        out_specs=pl.BlockSpec((MEM_LEN, 2 * XA_WIDTH), lambda i: (i, 0)),
        out_shape=jax.ShapeDtypeStruct((t, 2 * XA_WIDTH), BF16),
        compiler_params=_cparams(("parallel",)),
        name="mem_kv",
    )(mem, g, w)


def _route(logits):
    lane = lax.broadcasted_iota(I32, logits.shape, 1).astype(F32)
    ninf = -jnp.inf
    big = float(LANES)
    gl = jnp.where(lane < N_GROUPS, logits, ninf)
    gmax = jnp.max(gl, axis=-1, keepdims=True)
    g = jnp.min(jnp.where(gl == gmax, lane, big), axis=-1, keepdims=True)
    pg = 1.0 / jnp.sum(jnp.exp(gl - gmax), axis=-1, keepdims=True)
    lo = N_GROUPS + EXPERTS_PER_GROUP * g
    el = jnp.where((lane >= lo) & (lane < lo + EXPERTS_PER_GROUP), logits, ninf)
    v1 = jnp.max(el, axis=-1, keepdims=True)
    i1 = jnp.min(jnp.where(el == v1, lane, big), axis=-1, keepdims=True)
    el2 = jnp.where(lane == i1, ninf, el)
    v2 = jnp.max(el2, axis=-1, keepdims=True)
    i2 = jnp.min(jnp.where(el2 == v2, lane, big), axis=-1, keepdims=True)
    t = jnp.exp(v2 - v1)
    den = 1.0 + t
    gate1 = pg * (1.0 / den)
    gate2 = pg * (t / den)
    gates = jnp.where(lane == 0.0, gate1, jnp.where(lane == 1.0, gate2, 0.0))
    eids = jnp.where(lane == 0.0, i1 - N_GROUPS, jnp.where(lane == 1.0, i2 - N_GROUPS, 0.0)).astype(I32)
    return gates, eids


def _pack_halves(h):
    half = h.shape[1] // 2
    hi = pltpu.bitcast(h[:, :half].astype(BF16).astype(F32), U32)
    lo = pltpu.bitcast(h[:, half:].astype(BF16).astype(F32), U32)
    return hi | (lo >> 16)


def _unpack_halves(w):
    hi = pltpu.bitcast(w & jnp.uint32(0xFFFF0000), F32).astype(BF16)
    lo = pltpu.bitcast(w << 16, F32).astype(BF16)
    return jnp.concatenate([hi, lo], axis=1)


def _store_row_tiles(ref, x):
    n, width = x.shape
    c = width // LANES
    for j in range(c):
        ref[pl.ds(j, n, stride=c), :] = x[:, j * LANES:(j + 1) * LANES]


def _load_row_tiles(ref, n, c):
    return jnp.concatenate([ref[pl.ds(j, n, stride=c), :] for j in range(c)], axis=1)


def _xattn_kernel(x_ref, kv_ref, gx_ref, wq_ref, wo_ref, gf_ref, wr_ref, br_ref,
                  x2_ref, hp_ref, gate_ref, eid_ref):
    x1 = x_ref[...]
    hq = _rms_scale(x1, gx_ref[...], NORM_EPS).astype(BF16)
    q = jnp.dot(hq, wq_ref[...], preferred_element_type=F32).astype(BF16)
    scale = XA_HEAD_DIM ** -0.5
    outs = []
    for h in range(XA_HEADS):
        sl = slice(h * XA_HEAD_DIM, (h + 1) * XA_HEAD_DIM)
        vsl = slice(XA_WIDTH + h * XA_HEAD_DIM, XA_WIDTH + (h + 1) * XA_HEAD_DIM)
        s = _dot_nt(q[:, sl], kv_ref[:, sl]) * scale
        m = jnp.max(s, axis=-1, keepdims=True)
        e = jnp.exp(s - m)
        l = jnp.sum(e, axis=-1, keepdims=True)
        o = jnp.dot(e.astype(BF16), kv_ref[:, vsl], preferred_element_type=F32)
        outs.append((o / l).astype(BF16))
    o = jnp.concatenate(outs, axis=1)
    x2 = x1 + jnp.dot(o, wo_ref[...], preferred_element_type=F32)
    x2_ref[...] = x2
    h3 = _rms_scale(x2, gf_ref[...], NORM_EPS)
    _store_row_tiles(hp_ref, _pack_halves(h3))
    logits = jnp.dot(h3.astype(BF16), wr_ref[...], preferred_element_type=F32) + br_ref[...]
    gates, eids = _route(logits)
    gate_ref[...] = gates
    eid_ref[...] = eids


def _xattn(x1, kv, gx, wq, wo, gf, wr, br):
    t = x1.shape[0]
    tm = XA_TM
    per_batch = SEQ // tm
    const = lambda shape: pl.BlockSpec(shape, lambda i: (0, 0))
    tile = lambda n: pl.BlockSpec((tm, n), lambda i: (i, 0))
    return pl.pallas_call(
        _xattn_kernel,
        grid=(t // tm,),
        in_specs=[tile(D_MODEL),
                  pl.BlockSpec((MEM_LEN, 2 * XA_WIDTH), lambda i: (i // per_batch, 0)),
                  const((1, D_MODEL)), const((D_MODEL, XA_WIDTH)), const((XA_WIDTH, D_MODEL)),
                  const((1, D_MODEL)), const((D_MODEL, LANES)), const((1, LANES))],
        out_specs=[tile(D_MODEL), pl.BlockSpec((tm * HP_TILES, LANES), lambda i: (i, 0)), tile(LANES), tile(LANES)],
        out_shape=[jax.ShapeDtypeStruct((t, D_MODEL), F32),
                   jax.ShapeDtypeStruct((t * HP_TILES, LANES), U32),
                   jax.ShapeDtypeStruct((t, LANES), F32),
                   jax.ShapeDtypeStruct((t, LANES), I32)],
        compiler_params=_cparams(("parallel",)),
        name="xattn_router",
    )(x1, kv, gx, wq, wo, gf, wr, br)


IDX_ROWS = 8


def _rank_kernel(eid_ref, idx_ref, tile_ref, cnt_ref, carry_ref):
    @pl.when(pl.program_id(0) == 0)
    def _():
        carry_ref[...] = jnp.zeros_like(carry_ref)

    tm = eid_ref.shape[0]
    eid = eid_ref[...]
    lane = lax.broadcasted_iota(I32, (tm, LANES), 1)
    e1 = eid[:, 0:1]
    e2 = eid[:, 1:2]
    m1 = lane == e1
    m2 = lane == e2
    onehot = jnp.where(m1, 1.0, jnp.where(m2, 1.0, 0.0))
    r = lax.broadcasted_iota(I32, (tm, tm), 0)
    c = lax.broadcasted_iota(I32, (tm, tm), 1)
    tri = jnp.where(c < r, 1.0, 0.0).astype(BF16)
    local = jnp.dot(tri, onehot.astype(BF16), preferred_element_type=F32)
    carry = carry_ref[0:1, :]
    n_tile = jnp.sum(onehot, axis=0, keepdims=True)

    def assignment(m, e):
        r_local = jnp.sum(jnp.where(m, local, 0.0), axis=-1, keepdims=True)
        r_global = r_local + jnp.sum(jnp.where(m, carry, 0.0), axis=-1, keepdims=True)
        first = jnp.sum(jnp.where(lane < e, n_tile, 0.0), axis=-1, keepdims=True)
        return first + r_local, r_global

    p1, g1 = assignment(m1, e1)
    p2, g2 = assignment(m2, e2)
    cols = [p1, p2, g1, g2, e1.astype(F32), e2.astype(F32)]
    table = jnp.zeros((tm, LANES), F32)
    for j, col in enumerate(cols):
        table = jnp.where(lane == j, col, table)
    idx_ref[0] = jnp.transpose(table)[0:IDX_ROWS, :].astype(I32)
    tile_ref[0] = jnp.where(lax.broadcasted_iota(I32, (8, LANES), 0) == 0, carry, n_tile)
    carry_ref[...] = carry_ref[...] + n_tile
    cnt_ref[...] = carry_ref[...]


def _rank(eid):
    t = eid.shape[0]
    tm = RANK_TM
    n_tiles = t // tm
    return pl.pallas_call(
        _rank_kernel,
        grid=(n_tiles,),
        in_specs=[pl.BlockSpec((tm, LANES), lambda i: (i, 0))],
        out_specs=[pl.BlockSpec((1, IDX_ROWS, tm), lambda i: (i, 0, 0)),
                   pl.BlockSpec((1, 8, LANES), lambda i: (i, 0, 0)),
                   pl.BlockSpec((8, LANES), lambda i: (0, 0))],
        out_shape=[jax.ShapeDtypeStruct((n_tiles, IDX_ROWS, tm), I32),
                   jax.ShapeDtypeStruct((n_tiles, 8, LANES), F32),
                   jax.ShapeDtypeStruct((8, LANES), F32)],
        scratch_shapes=[pltpu.VMEM((8, LANES), F32)],
        compiler_params=_cparams(("arbitrary",)),
        name="expert_rank",
    )(eid)


PIECE = 8


def _dispatch_kernel(cnt_ref, pend_ref, nused_ref, pos_ref, run_ref, hp_ref, xs_ref, sbuf, zero_ref, sem, zsem):
    tm = hp_ref.shape[0] // HP_TILES
    zb = zero_ref.shape[0]
    n_blocks = xs_ref.shape[0] // zb

    @pl.when(pl.program_id(0) == 0)
    def _():
        zero_ref[...] = jnp.zeros_like(zero_ref)
        sbuf[pl.ds(2 * tm * HP_TILES, PIECE * HP_TILES), :] = jnp.zeros((PIECE * HP_TILES, LANES), U32)

        def zero_copy(block):
            return pltpu.make_async_copy(zero_ref, xs_ref.at[pl.ds(pl.multiple_of(block * zb, zb), zb)], zsem)

        def tail_start(j, carry):
            zero_copy(j).start()
            return carry

        def tail_wait(j, carry):
            zero_copy(j).wait()
            return carry

        for e in range(N_EXPERTS):
            @pl.when(cnt_ref[e] > 0)
            def _():
                zero_copy(pend_ref[e] - 1).start()
        lax.fori_loop(nused_ref[0], n_blocks, tail_start, 0)
        for e in range(N_EXPERTS):
            @pl.when(cnt_ref[e] > 0)
            def _():
                zero_copy(pend_ref[e] - 1).wait()
        lax.fori_loop(nused_ref[0], n_blocks, tail_wait, 0)

    def place(t, carry):
        row = hp_ref[pl.ds(pl.multiple_of(t * HP_TILES, HP_TILES), HP_TILES), :]
        for k in range(TOP_K):
            sbuf[pl.ds(pl.multiple_of(pos_ref[0, k, t] * HP_TILES, HP_TILES), HP_TILES), :] = row
        return carry

    lax.fori_loop(0, tm, place, 0, unroll=8)

    def piece_copy(src_tok, dst_slot):
        rows = PIECE * HP_TILES
        return pltpu.make_async_copy(sbuf.at[pl.ds(pl.multiple_of(src_tok * HP_TILES, HP_TILES), rows)],
                                     xs_ref.at[pl.ds(pl.multiple_of(dst_slot * HP_TILES, HP_TILES), rows)], sem)

    for e in range(N_EXPERTS):
        slot0 = run_ref[0, 0, e]
        first = run_ref[0, 0, 2 * N_EXPERTS + e]

        def run_piece(j, carry):
            piece_copy(first + PIECE * j, slot0 + PIECE * j).start()
            return carry

        lax.fori_loop(0, run_ref[0, 0, N_EXPERTS + e], run_piece, 0)

    def drain(j, carry):
        piece_copy(0, 0).wait()
        return carry

    lax.fori_loop(0, run_ref[0, 0, 3 * N_EXPERTS], drain, 0)


def _dispatch(cnt, pend_blocks, nused, pos, runs, hp, n_slots):
    n_tiles, _, tm = pos.shape
    grid_spec = pltpu.PrefetchScalarGridSpec(
        num_scalar_prefetch=3,
        grid=(n_tiles,),
        in_specs=[pl.BlockSpec((1, TOP_K, tm), lambda i, *_: (i, 0, 0), memory_space=pltpu.SMEM),
                  pl.BlockSpec((1, 1, LANES), lambda i, *_: (i, 0, 0), memory_space=pltpu.SMEM),
                  pl.BlockSpec((tm * HP_TILES, LANES), lambda i, *_: (i, 0))],
        out_specs=pl.BlockSpec(memory_space=pl.ANY),
        scratch_shapes=[pltpu.VMEM(((TOP_K * tm + PIECE) * HP_TILES, LANES), U32),
                        pltpu.VMEM((MOE_TM * HP_TILES, LANES), U32),
                        pltpu.SemaphoreType.DMA(()), pltpu.SemaphoreType.DMA(())],
    )
    return pl.pallas_call(
        _dispatch_kernel,
        grid_spec=grid_spec,
        out_shape=jax.ShapeDtypeStruct((n_slots * HP_TILES, LANES), U32),
        compiler_params=_cparams(("arbitrary",)),
        name="moe_dispatch",
    )(cnt, pend_blocks, nused, pos, runs, hp)


MOE_DOWN_CHUNKS = 4


def _moe_kernel(bexp_ref, nused_ref, xs_ref, wg_ref, wu_ref, wd_ref, o_ref, wgb, wub, wdb):
    i = pl.program_id(0)
    nused = nused_ref[0]
    tm = xs_ref.shape[0] // HP_TILES

    @pl.when((i < nused) & ((i == 0) | (bexp_ref[i] != bexp_ref[jnp.maximum(i - 1, 0)])))
    def _():
        wgb[...] = wg_ref[0].astype(BF16)
        wub[...] = wu_ref[0].astype(BF16)
        wdb[...] = wd_ref[0].astype(BF16)

    @pl.when(i < nused)
    def _():
        x = _unpack_halves(_load_row_tiles(xs_ref, tm, HP_TILES))
        hg = jnp.dot(x, wgb[...], preferred_element_type=F32)
        hu = jnp.dot(x, wub[...], preferred_element_type=F32)
        hdn = ((hg * jax.nn.sigmoid(hg)) * hu).astype(BF16)
        cw = D_MODEL // MOE_DOWN_CHUNKS
        for c in range(MOE_DOWN_CHUNKS):
            y = jnp.dot(hdn, wdb[:, c * cw:(c + 1) * cw], preferred_element_type=F32)
            for j in range(cw // LANES):
                o_ref[pl.ds(c * (cw // LANES) + j, tm, stride=YS_TILES), :] = y[:, j * LANES:(j + 1) * LANES]

    @pl.when(i >= nused)
    def _():
        o_ref[...] = jnp.zeros_like(o_ref)


def _moe(bexp, nused, xs, wg, wu, wd):
    tm = MOE_TM
    n_slots = xs.shape[0] // HP_TILES
    n_blocks = n_slots // tm
    grid_spec = pltpu.PrefetchScalarGridSpec(
        num_scalar_prefetch=2,
        grid=(n_blocks,),
        in_specs=[pl.BlockSpec((tm * HP_TILES, LANES), lambda i, be, nu: (jnp.minimum(i, nu[0] - 1), 0)),
                  pl.BlockSpec((1, D_MODEL, D_EXPERT), lambda i, be, nu: (be[i], 0, 0)),
                  pl.BlockSpec((1, D_MODEL, D_EXPERT), lambda i, be, nu: (be[i], 0, 0)),
                  pl.BlockSpec((1, D_EXPERT, D_MODEL), lambda i, be, nu: (be[i], 0, 0))],
        out_specs=pl.BlockSpec((tm * YS_TILES, LANES), lambda i, be, nu: (i, 0)),
        scratch_shapes=[pltpu.VMEM((D_MODEL, D_EXPERT), BF16), pltpu.VMEM((D_MODEL, D_EXPERT), BF16),
                        pltpu.VMEM((D_EXPERT, D_MODEL), BF16)],
    )
    return pl.pallas_call(
        _moe_kernel,
        grid_spec=grid_spec,
        out_shape=jax.ShapeDtypeStruct((n_slots * YS_TILES, LANES), F32),
        compiler_params=_cparams(("arbitrary",)),
        name="moe_experts",
    )(bexp, nused, xs, wg, wu, wd)


def _combine_kernel(dest_ref, x_ref, gate_ref, g_ref, ys_ref, o_ref, y1_ref, y2_ref, sem):
    tm = x_ref.shape[0]
    bufs = (y1_ref, y2_ref)

    def row_copy(r, k, d):
        src = ys_ref.at[pl.ds(pl.multiple_of(d * YS_TILES, YS_TILES), YS_TILES)]
        return pltpu.make_async_copy(src, bufs[k].at[pl.ds(pl.multiple_of(r * YS_TILES, YS_TILES), YS_TILES)], sem)

    def issue(r, carry):
        for k in range(TOP_K):
            row_copy(r, k, dest_ref[0, k, r]).start(priority=k)
        return carry

    def drain(r, carry):
        for k in range(TOP_K):
            row_copy(r, k, dest_ref[0, k, r]).wait()
        return carry

    lax.fori_loop(0, tm, issue, 0)
    lax.fori_loop(0, tm, drain, 0)
    gates = gate_ref[...]
    moe = (gates[:, 0:1] * _load_row_tiles(y1_ref, tm, YS_TILES)
           + gates[:, 1:2] * _load_row_tiles(y2_ref, tm, YS_TILES))
    o_ref[...] = _rms_scale(x_ref[...] + moe, g_ref[...], NORM_EPS)


def _combine(dest, x2, gates, g, ys, row0, n_rows):
    tm = ROW_TM
    blk0 = row0 // tm
    per_tile = dest.shape[2] // tm
    return pl.pallas_call(
        _combine_kernel,
        grid=(n_rows // tm,),
        in_specs=[pl.BlockSpec((1, TOP_K, tm), lambda i: ((blk0 + i) // per_tile, 0, (blk0 + i) % per_tile),
                               memory_space=pltpu.SMEM),
                  pl.BlockSpec((tm, D_MODEL), lambda i: (blk0 + i, 0)),
                  pl.BlockSpec((tm, LANES), lambda i: (blk0 + i, 0)),
                  pl.BlockSpec((1, D_MODEL), lambda i: (0, 0)),
                  pl.BlockSpec(memory_space=pl.ANY)],
        out_specs=pl.BlockSpec((tm, D_MODEL), lambda i: (i, 0)),
        out_shape=jax.ShapeDtypeStruct((n_rows, D_MODEL), F32),
        scratch_shapes=[pltpu.VMEM((tm * YS_TILES, LANES), F32), pltpu.VMEM((tm * YS_TILES, LANES), F32),
                        pltpu.SemaphoreType.DMA(())],
        compiler_params=_cparams(("arbitrary",)),
        name="moe_combine",
    )(dest, x2, gates, g, ys)


def _routing_plan(idx, tiles, counts):
    cnt = counts[0, :N_EXPERTS].astype(I32)
    padded = jnp.where(cnt > 0, (cnt + PIECE - 1 + MOE_TM - 1) // MOE_TM * MOE_TM, 0)
    pends = jnp.cumsum(padded)
    pstarts = pends - padded
    n_tiles, _, tm = idx.shape
    n_blocks = n_tiles * tm * TOP_K // MOE_TM + N_EXPERTS + 1
    blk0 = jnp.arange(n_blocks, dtype=I32) * MOE_TM
    bexp = jnp.minimum(jnp.sum(pends[None, :] <= blk0[:, None], axis=1), N_EXPERTS - 1).astype(I32)
    nused = (pends[-1:] // MOE_TM).astype(I32)
    before = tiles[:, 0, :N_EXPERTS].astype(I32)
    inside = tiles[:, 1, :N_EXPERTS].astype(I32)
    first = jnp.cumsum(inside, axis=1) - inside
    pieces = (inside + PIECE - 1) // PIECE
    runs = jnp.concatenate([pstarts[None, :] + before, pieces, first,
                            jnp.sum(pieces, axis=1, keepdims=True),
                            jnp.zeros((n_tiles, LANES - 3 * N_EXPERTS - 1), I32)], axis=1).reshape(n_tiles, 1, LANES)
    pos = idx[:, 0:TOP_K, :]
    dest = jnp.take(pstarts, idx[:, 2 * TOP_K:3 * TOP_K, :]) + idx[:, TOP_K:2 * TOP_K, :]
    return pos, runs, dest, cnt, (pends // MOE_TM).astype(I32), bexp, nused, n_blocks * MOE_TM


def kernel(x_prompt, x_sample, mem_prompt, mem_sample, mix_norm_g, w_in, na_rpb, diff_lambda_q1, diff_lambda_k1, diff_lambda_q2, diff_lambda_k2, diff_subln_g, w_branch_na, w_branch_diff, w_out, xa_norm_g, mem_norm_g, xa_w_q, xa_w_kv, xa_w_o, ffn_norm_g, router_group_w, router_group_b, router_expert_w, router_expert_b, w_gate, w_up, w_down, final_norm_g):
    nb_p, nb_s = x_prompt.shape[0], x_sample.shape[0]
    nb = nb_p + nb_s
    t = nb * SEQ
    xp = x_prompt.reshape(nb_p * SEQ, D_MODEL)
    xs_in = x_sample.reshape(nb_s * SEQ, D_MODEL)
    mem = jnp.concatenate([mem_prompt, mem_sample], axis=0).reshape(nb * MEM_LEN, D_MODEL)
    row = lambda v: v.reshape(1, -1).astype(F32)
    bf = lambda w: w.astype(BF16)

    qcol = 3 * NA_WIDTH
    colscale = jnp.ones((1, IN_COLS), F32).at[:, qcol:qcol + DIFF_QK_WIDTH].set(DIFF_QK_DIM ** -0.5 * LOG2E)
    proj = _inproj(xp, xs_in, row(mix_norm_g[0]), bf(w_in[0]), colscale)
    proj3 = proj.reshape(nb, SEQ, IN_COLS)
    a = _na_attention(proj3, _na_bias_table(na_rpb[0]))
    b = _diff_attention(proj3, row(diff_lambda_q1[0]), row(diff_lambda_k1[0]),
                        row(diff_lambda_q2[0]), row(diff_lambda_k2[0]), row(diff_subln_g[0]))
    x1 = _merge(a.reshape(t, NA_WIDTH), b.reshape(t, DIFF_V_WIDTH), proj, xp, xs_in,
                bf(w_branch_na[0]), bf(w_branch_diff[0]), bf(w_out[0]))
    kv = _memkv(mem, row(mem_norm_g[0]), bf(xa_w_kv[0]))
    w_router = jnp.zeros((D_MODEL, LANES), F32)
    w_router = w_router.at[:, :N_GROUPS].set(router_group_w[0]).at[:, N_GROUPS:N_GROUPS + N_EXPERTS].set(router_expert_w[0])
    b_router = jnp.zeros((1, LANES), F32)
    b_router = b_router.at[0, :N_GROUPS].set(router_group_b[0]).at[0, N_GROUPS:N_GROUPS + N_EXPERTS].set(router_expert_b[0])
    x2, hp, gates, eid = _xattn(x1, kv, row(xa_norm_g[0]), bf(xa_w_q[0]), bf(xa_w_o[0]),
                                row(ffn_norm_g[0]), bf(w_router), b_router)
    idx, tiles, counts = _rank(eid)
    pos, runs, dest, cnt, pend_blocks, bexp, nused, n_slots = _routing_plan(idx, tiles, counts)
    xs = _dispatch(cnt, pend_blocks, nused, pos, runs, hp, n_slots)
    ys = _moe(bexp, nused, xs, w_gate[0], w_up[0], w_down[0])
    fg = row(final_norm_g)
    y_p = _combine(dest, x2, gates, fg, ys, 0, nb_p * SEQ).reshape(nb_p, SEQ, D_MODEL)
    y_s = _combine(dest, x2, gates, fg, ys, nb_p * SEQ, nb_s * SEQ).reshape(nb_s, SEQ, D_MODEL)
    return (y_p, y_s)
```

```python
import functools
import math

import jax
import jax.numpy as jnp
from jax import lax
from jax.experimental import pallas as pl
from jax.experimental.pallas import tpu as pltpu

F32 = jnp.float32
BF16 = jnp.bfloat16
I32 = jnp.int32
U32 = jnp.uint32

D_MODEL = 2048
SEQ = 2048
GRID_W = 64
ROWS = SEQ // GRID_W
NA_HEADS = 8
NA_HEAD_DIM = 128
NA_WIDTH = NA_HEADS * NA_HEAD_DIM
NA_KH = 8
NA_KW = 16
DIFF_HEADS = 8
DIFF_QK_DIM = 64
DIFF_V_DIM = 128
DIFF_QK_WIDTH = DIFF_HEADS * 2 * DIFF_QK_DIM
DIFF_V_WIDTH = DIFF_HEADS * DIFF_V_DIM
IN_COLS = 3 * NA_WIDTH + 2 * DIFF_QK_WIDTH + DIFF_V_WIDTH + 2 * D_MODEL
MEM_LEN = 256
XA_HEADS = 4
XA_HEAD_DIM = 128
XA_WIDTH = XA_HEADS * XA_HEAD_DIM
N_GROUPS = 4
EXPERTS_PER_GROUP = 8
N_EXPERTS = N_GROUPS * EXPERTS_PER_GROUP
TOP_K = 2
D_EXPERT = 512
NORM_EPS = 1e-6
SUBLN_EPS = 1e-5
NEG_INF = -1e30
LAMBDA_INIT = 0.8 - 0.6 * math.exp(-0.3 * 0)
LOG2E = math.log2(math.e)

LANES = 128
HP_TILES = D_MODEL // 2 // LANES
YS_TILES = D_MODEL // LANES
VMEM_LIMIT = 56 * 1024 * 1024

INPROJ_TM = 1024
INPROJ_TN = 1024
NORM_CHUNK = 128
DIFF_TQ = 512
DIFF_SUB = 256
MERGE_TM = 256
XA_TM = 256
RANK_TM = 512
MOE_TM = 512
ROW_TM = 256


def _cparams(sem):
    return pltpu.CompilerParams(dimension_semantics=sem, vmem_limit_bytes=VMEM_LIMIT)


def _rms_scale(x, g, eps):
    ms = jnp.mean(x * x, axis=-1, keepdims=True)
    return x * lax.rsqrt(ms + eps) * g


def _dot_nt(a, b):
    return lax.dot_general(a, b, (((1,), (1,)), ((), ())), preferred_element_type=F32)


def _two_group_specs(tm, width, n_first):
    first = pl.BlockSpec((tm, width), lambda i, *_: (jnp.minimum(i, n_first - 1), 0))
    second = pl.BlockSpec((tm, width), lambda i, *_: (jnp.maximum(i - n_first, 0), 0))
    return first, second


def _inproj_kernel(n_first, xp_ref, xs_ref, g_ref, w_ref, cs_ref, o_ref, h_ref):
    def normalise(x_ref):
        def chunk(c, carry):
            rows = pl.ds(pl.multiple_of(c * NORM_CHUNK, NORM_CHUNK), NORM_CHUNK)
            h_ref[rows, :] = _rms_scale(x_ref[rows, :], g_ref[...], NORM_EPS).astype(BF16)
            return carry
        lax.fori_loop(0, x_ref.shape[0] // NORM_CHUNK, chunk, 0)

    first_col = pl.program_id(1) == 0
    in_first = pl.program_id(0) < n_first
    pl.when(first_col & in_first)(lambda: normalise(xp_ref))
    pl.when(first_col & jnp.logical_not(in_first))(lambda: normalise(xs_ref))
    acc = jnp.dot(h_ref[...], w_ref[...], preferred_element_type=F32)
    o_ref[...] = (acc * cs_ref[...]).astype(o_ref.dtype)


def _inproj(xp, xs, g, w, colscale):
    d = xp.shape[1]
    t = xp.shape[0] + xs.shape[0]
    n = w.shape[1]
    tm = INPROJ_TM
    n_first = xp.shape[0] // tm
    spec_p, spec_s = _two_group_specs(tm, d, n_first)
    return pl.pallas_call(
        functools.partial(_inproj_kernel, n_first),
        grid=(t // tm, n // INPROJ_TN),
        in_specs=[spec_p, spec_s,
                  pl.BlockSpec((1, d), lambda i, j: (0, 0)),
                  pl.BlockSpec((d, INPROJ_TN), lambda i, j: (0, j)),
                  pl.BlockSpec((1, INPROJ_TN), lambda i, j: (0, j))],
        out_specs=pl.BlockSpec((tm, INPROJ_TN), lambda i, j: (i, j)),
        out_shape=jax.ShapeDtypeStruct((t, n), BF16),
        scratch_shapes=[pltpu.VMEM((tm, d), BF16)],
        compiler_params=_cparams(("parallel", "arbitrary")),
        name="inproj",
    )(xp, xs, g, w, colscale)


NA_QROWS = 4
NA_KROWS = 12
NA_TILES = ROWS // NA_QROWS


def _na_tile_key_row(ti):
    return min(max(NA_QROWS * ti - NA_KH // 2, 0), ROWS - NA_KROWS)


def _na_bias_blocks(rpb):
    cols = jnp.arange(GRID_W)
    cstart = jnp.clip(cols - NA_KW // 2, 0, GRID_W - NA_KW)
    col_ok = (cols[None, :] >= cstart[:, None]) & (cols[None, :] < cstart[:, None] + NA_KW)
    dc = jnp.clip(cols[None, :] - cols[:, None], 1 - NA_KW, NA_KW - 1) + NA_KW - 1
    pick = (dc[None] == jnp.arange(2 * NA_KW - 1)[:, None, None]).astype(F32)
    t = jnp.einsum("hdx,xck->hdck", rpb.astype(F32), pick, precision=lax.Precision.HIGHEST)
    return jnp.where(col_ok[None, None], t, NEG_INF)


def _na_kernel(q_ref, k_ref, v_ref, t_ref, o_ref, bias_ref):
    scale = NA_HEAD_DIM ** -0.5
    tq = NA_QROWS * GRID_W
    tk = NA_KROWS * GRID_W

    @pl.when(pl.program_id(1) == 0)
    def _():
        masked = jnp.full((GRID_W, GRID_W), NEG_INF, F32)
        for variant, ti in enumerate((0, 1, NA_TILES - 1)):
            for jr in range(NA_QROWS):
                r = NA_QROWS * ti + jr
                rs = min(max(r - NA_KH // 2, 0), ROWS - NA_KH)
                blocks = []
                for i in range(NA_KROWS):
                    kr = _na_tile_key_row(ti) + i
                    blocks.append(t_ref[0, kr - r + NA_KH - 1] if rs <= kr < rs + NA_KH else masked)
                bias_ref[variant, jr * GRID_W:(jr + 1) * GRID_W, :] = jnp.concatenate(blocks, axis=1)

    for ti in range(NA_TILES):
        variant = 0 if ti == 0 else (2 if ti == NA_TILES - 1 else 1)
        k0 = _na_tile_key_row(ti) * GRID_W
        qrows = slice(ti * tq, (ti + 1) * tq)
        krows = slice(k0, k0 + tk)
        s = _dot_nt(q_ref[0, qrows, :], k_ref[0, krows, :]) * scale + bias_ref[variant]
        m = jnp.max(s, axis=-1, keepdims=True)
        e = jnp.exp(s - m)
        l = jnp.sum(e, axis=-1, keepdims=True)
        o = jnp.dot(e.astype(BF16), v_ref[0, krows, :], preferred_element_type=F32)
        o_ref[0, qrows, :] = (o / l).astype(o_ref.dtype)


def _na_attention(proj3, blocks):
    b = proj3.shape[0]
    hd = NA_HEAD_DIM
    return pl.pallas_call(
        _na_kernel,
        grid=(NA_HEADS, b),
        in_specs=[pl.BlockSpec((1, SEQ, hd), lambda h, i: (i, 0, h)),
                  pl.BlockSpec((1, SEQ, hd), lambda h, i: (i, 0, NA_HEADS + h)),
                  pl.BlockSpec((1, SEQ, hd), lambda h, i: (i, 0, 2 * NA_HEADS + h)),
                  pl.BlockSpec((1,) + blocks.shape[1:], lambda h, i: (h, 0, 0, 0))],
        out_specs=pl.BlockSpec((1, SEQ, hd), lambda h, i: (i, 0, h)),
        out_shape=jax.ShapeDtypeStruct((b, SEQ, NA_WIDTH), BF16),
        scratch_shapes=[pltpu.VMEM((3, NA_QROWS * GRID_W, NA_KROWS * GRID_W), F32)],
        compiler_params=_cparams(("arbitrary", "arbitrary")),
        name="na_attn",
    )(proj3, proj3, proj3, blocks)


def _alibi_table(tq):
    slopes = jnp.exp2(-8.0 * jnp.arange(1, DIFF_HEADS + 1, dtype=F32) / DIFF_HEADS)
    r = jnp.arange(tq)[:, None]
    x = jnp.arange(2 * SEQ - tq)[None, :]
    dist = jnp.abs(r - x + (SEQ - tq)).astype(F32)
    return (slopes * LOG2E)[:, None, None] * dist[None]


def _diff_kernel(lq1_ref, lk1_ref, lq2_ref, lk2_ref, q_ref, k_ref, v_ref, g_ref, alibi_ref, o_ref):
    qi = pl.program_id(2)
    ts = alibi_ref.shape[1]
    nsub = q_ref.shape[1] // ts
    lam = (jnp.exp(jnp.sum(lq1_ref[...] * lk1_ref[...], axis=-1, keepdims=True))
           - jnp.exp(jnp.sum(lq2_ref[...] * lk2_ref[...], axis=-1, keepdims=True)) + LAMBDA_INIT)
    k = k_ref[0]

    def scores(u):
        q = q_ref[0, u * ts:(u + 1) * ts, :]
        return [_dot_nt(q[:, mi * DIFF_QK_DIM:(mi + 1) * DIFF_QK_DIM], k[:, mi * DIFF_QK_DIM:(mi + 1) * DIFF_QK_DIM])
                for mi in range(2)]

    def expo(s, bias):
        s = s - bias
        m = jnp.max(s, axis=-1, keepdims=True)
        e = jnp.exp2(s - m)
        return e, jnp.sum(e, axis=-1, keepdims=True)

    pending = scores(0)
    for u in range(nsub):
        s0, s1 = pending
        if u + 1 < nsub:
            pending = scores(u + 1)
        off = (SEQ - ts) - (qi * nsub + u) * ts
        bias = alibi_ref[0, :, pl.ds(pl.multiple_of(off, ts), SEQ)]
        e0, l0 = expo(s0, bias)
        e1, l1 = expo(s1, bias)
        a = (e0 - e1 * (lam * l0 / l1)).astype(BF16)
        o = jnp.dot(a, v_ref[0], preferred_element_type=F32) / l0
        o = _rms_scale(o, g_ref[...], SUBLN_EPS) * (1.0 - LAMBDA_INIT)
        o_ref[0, u * ts:(u + 1) * ts, :] = o.astype(o_ref.dtype)


def _diff_attention(proj3, lq1, lk1, lq2, lk2, subln_g):
    b = proj3.shape[0]
    tq = DIFF_TQ
    qoff = 3 * NA_WIDTH // LANES
    koff = qoff + DIFF_QK_WIDTH // LANES
    voff = koff + DIFF_QK_WIDTH // LANES
    vec = lambda n: pl.BlockSpec((1, n), lambda h, i, j: (0, 0))
    return pl.pallas_call(
        _diff_kernel,
        grid=(DIFF_HEADS, b, SEQ // tq),
        in_specs=[vec(DIFF_QK_DIM), vec(DIFF_QK_DIM), vec(DIFF_QK_DIM), vec(DIFF_QK_DIM),
                  pl.BlockSpec((1, tq, LANES), lambda h, i, j: (i, j, qoff + h)),
                  pl.BlockSpec((1, SEQ, LANES), lambda h, i, j: (i, 0, koff + h)),
                  pl.BlockSpec((1, SEQ, LANES), lambda h, i, j: (i, 0, voff + h)),
                  vec(DIFF_V_DIM),
                  pl.BlockSpec((1, DIFF_SUB, 2 * SEQ - DIFF_SUB), lambda h, i, j: (h, 0, 0))],
        out_specs=pl.BlockSpec((1, tq, DIFF_V_DIM), lambda h, i, j: (i, j, h)),
        out_shape=jax.ShapeDtypeStruct((b, SEQ, DIFF_V_WIDTH), BF16),
        compiler_params=_cparams(("arbitrary", "arbitrary", "arbitrary")),
        name="diff_attn",
    )(lq1, lk1, lq2, lk2, proj3, proj3, proj3, subln_g, _alibi_table(DIFF_SUB))


def _merge_kernel(n_first, a_ref, b_ref, gna_ref, gdf_ref, xp_ref, xs_ref, wna_ref, wdf_ref, wout_ref, o_ref):
    pa = jnp.dot(a_ref[...], wna_ref[...], preferred_element_type=F32)
    pb = jnp.dot(b_ref[...], wdf_ref[...], preferred_element_type=F32)
    merged = (jax.nn.sigmoid(gna_ref[...].astype(F32)) * pa
              + jax.nn.sigmoid(gdf_ref[...].astype(F32)) * pb)
    delta = jnp.dot(merged.astype(BF16), wout_ref[...], preferred_element_type=F32)
    in_first = pl.program_id(0) < n_first

    @pl.when(in_first)
    def _():
        o_ref[...] = xp_ref[...] + delta

    @pl.when(jnp.logical_not(in_first))
    def _():
        o_ref[...] = xs_ref[...] + delta


def _merge(a, b, proj, xp, xs, wna, wdf, wout):
    t = xp.shape[0] + xs.shape[0]
    tm = MERGE_TM
    gna_blk = (3 * NA_WIDTH + 2 * DIFF_QK_WIDTH + DIFF_V_WIDTH) // D_MODEL
    const = lambda shape: pl.BlockSpec(shape, lambda i: (0, 0))
    n_first = xp.shape[0] // tm
    spec_p, spec_s = _two_group_specs(tm, D_MODEL, n_first)
    return pl.pallas_call(
        functools.partial(_merge_kernel, n_first),
        grid=(t // tm,),
        in_specs=[pl.BlockSpec((tm, NA_WIDTH), lambda i: (i, 0)),
                  pl.BlockSpec((tm, DIFF_V_WIDTH), lambda i: (i, 0)),
                  pl.BlockSpec((tm, D_MODEL), lambda i: (i, gna_blk)),
                  pl.BlockSpec((tm, D_MODEL), lambda i: (i, gna_blk + 1)),
                  spec_p, spec_s,
                  const((NA_WIDTH, D_MODEL)), const((DIFF_V_WIDTH, D_MODEL)), const((D_MODEL, D_MODEL))],
        out_specs=pl.BlockSpec((tm, D_MODEL), lambda i: (i, 0)),
        out_shape=jax.ShapeDtypeStruct((t, D_MODEL), F32),
        compiler_params=_cparams(("parallel",)),
        name="merge_outproj",
    )(a, b, proj, proj, xp, xs, wna, wdf, wout)


def _memkv_kernel(m_ref, g_ref, w_ref, o_ref):
    h = _rms_scale(m_ref[...], g_ref[...], NORM_EPS).astype(BF16)
    o_ref[...] = jnp.dot(h, w_ref[...], preferred_element_type=F32).astype(o_ref.dtype)


def _memkv(mem, g, w):
    t = mem.shape[0]
    return pl.pallas_call(
        _memkv_kernel,
        grid=(t // MEM_LEN,),
        in_specs=[pl.BlockSpec((MEM_LEN, D_MODEL), lambda i: (i, 0)),
                  pl.BlockSpec((1, D_MODEL), lambda i: (0, 0)),
                  pl.BlockSpec((D_MODEL, 2 * XA_WIDTH), lambda i: (0, 0))],
        out_specs=pl.BlockSpec((MEM_LEN, 2 * XA_WIDTH), lambda i: (i, 0)),
        out_shape=jax.ShapeDtypeStruct((t, 2 * XA_WIDTH), BF16),
        compiler_params=_cparams(("parallel",)),
        name="mem_kv",
    )(mem, g, w)


def _route(logits):
    lane = lax.broadcasted_iota(I32, logits.shape, 1).astype(F32)
    ninf = -jnp.inf
    big = float(LANES)
    gl = jnp.where(lane < N_GROUPS, logits, ninf)
    gmax = jnp.max(gl, axis=-1, keepdims=True)
    g = jnp.min(jnp.where(gl == gmax, lane, big), axis=-1, keepdims=True)
    pg = 1.0 / jnp.sum(jnp.exp(gl - gmax), axis=-1, keepdims=True)
    lo = N_GROUPS + EXPERTS_PER_GROUP * g
    el = jnp.where((lane >= lo) & (lane < lo + EXPERTS_PER_GROUP), logits, ninf)
    v1 = jnp.max(el, axis=-1, keepdims=True)
    i1 = jnp.min(jnp.where(el == v1, lane, big), axis=-1, keepdims=True)
    el2 = jnp.where(lane == i1, ninf, el)
    v2 = jnp.max(el2, axis=-1, keepdims=True)
    i2 = jnp.min(jnp.where(el2 == v2, lane, big), axis=-1, keepdims=True)
    t = jnp.exp(v2 - v1)
    den = 1.0 + t
    gate1 = pg * (1.0 / den)
    gate2 = pg * (t / den)
    gates = jnp.where(lane == 0.0, gate1, jnp.where(lane == 1.0, gate2, 0.0))
    eids = jnp.where(lane == 0.0, i1 - N_GROUPS, jnp.where(lane == 1.0, i2 - N_GROUPS, 0.0)).astype(I32)
    return gates, eids


def _pack_halves(h):
    half = h.shape[1] // 2
    hi = pltpu.bitcast(h[:, :half].astype(BF16).astype(F32), U32)
    lo = pltpu.bitcast(h[:, half:].astype(BF16).astype(F32), U32)
    return hi | (lo >> 16)


def _unpack_halves(w):
    hi = pltpu.bitcast(w & jnp.uint32(0xFFFF0000), F32).astype(BF16)
    lo = pltpu.bitcast(w << 16, F32).astype(BF16)
    return jnp.concatenate([hi, lo], axis=1)


def _store_row_tiles(ref, x):
    n, width = x.shape
    c = width // LANES
    for j in range(c):
        ref[pl.ds(j, n, stride=c), :] = x[:, j * LANES:(j + 1) * LANES]


def _load_row_tiles(ref, n, c):
    return jnp.concatenate([ref[pl.ds(j, n, stride=c), :] for j in range(c)], axis=1)


def _xattn_kernel(x_ref, kv_ref, gx_ref, wq_ref, wo_ref, gf_ref, wr_ref, br_ref,
                  x2_ref, hp_ref, gate_ref, eid_ref):
    x1 = x_ref[...]
    hq = _rms_scale(x1, gx_ref[...], NORM_EPS).astype(BF16)
    q = jnp.dot(hq, wq_ref[...], preferred_element_type=F32).astype(BF16)
    scale = XA_HEAD_DIM ** -0.5
    outs = []
    for h in range(XA_HEADS):
        sl = slice(h * XA_HEAD_DIM, (h + 1) * XA_HEAD_DIM)
        vsl = slice(XA_WIDTH + h * XA_HEAD_DIM, XA_WIDTH + (h + 1) * XA_HEAD_DIM)
        s = _dot_nt(q[:, sl], kv_ref[:, sl]) * scale
        m = jnp.max(s, axis=-1, keepdims=True)
        e = jnp.exp(s - m)
        l = jnp.sum(e, axis=-1, keepdims=True)
        o = jnp.dot(e.astype(BF16), kv_ref[:, vsl], preferred_element_type=F32)
        outs.append((o / l).astype(BF16))
    o = jnp.concatenate(outs, axis=1)
    x2 = x1 + jnp.dot(o, wo_ref[...], preferred_element_type=F32)
    x2_ref[...] = x2
    h3 = _rms_scale(x2, gf_ref[...], NORM_EPS)
    _store_row_tiles(hp_ref, _pack_halves(h3))
    logits = jnp.dot(h3.astype(BF16), wr_ref[...], preferred_element_type=F32) + br_ref[...]
    gates, eids = _route(logits)
    gate_ref[...] = gates
    eid_ref[...] = eids


def _xattn(x1, kv, gx, wq, wo, gf, wr, br):
    t = x1.shape[0]
    tm = XA_TM
    per_batch = SEQ // tm
    const = lambda shape: pl.BlockSpec(shape, lambda i: (0, 0))
    tile = lambda n: pl.BlockSpec((tm, n), lambda i: (i, 0))
    return pl.pallas_call(
        _xattn_kernel,
        grid=(t // tm,),
        in_specs=[tile(D_MODEL),
                  pl.BlockSpec((MEM_LEN, 2 * XA_WIDTH), lambda i: (i // per_batch, 0)),
                  const((1, D_MODEL)), const((D_MODEL, XA_WIDTH)), const((XA_WIDTH, D_MODEL)),
                  const((1, D_MODEL)), const((D_MODEL, LANES)), const((1, LANES))],
        out_specs=[tile(D_MODEL), pl.BlockSpec((tm * HP_TILES, LANES), lambda i: (i, 0)), tile(LANES), tile(LANES)],
        out_shape=[jax.ShapeDtypeStruct((t, D_MODEL), F32),
                   jax.ShapeDtypeStruct((t * HP_TILES, LANES), U32),
                   jax.ShapeDtypeStruct((t, LANES), F32),
                   jax.ShapeDtypeStruct((t, LANES), I32)],
        compiler_params=_cparams(("parallel",)),
        name="xattn_router",
    )(x1, kv, gx, wq, wo, gf, wr, br)


IDX_ROWS = 8


def _rank_kernel(eid_ref, idx_ref, tile_ref, cnt_ref, carry_ref):
    @pl.when(pl.program_id(0) == 0)
    def _():
        carry_ref[...] = jnp.zeros_like(carry_ref)

    tm = eid_ref.shape[0]
    eid = eid_ref[...]
    lane = lax.broadcasted_iota(I32, (tm, LANES), 1)
    e1 = eid[:, 0:1]
    e2 = eid[:, 1:2]
    m1 = lane == e1
    m2 = lane == e2
    onehot = jnp.where(m1, 1.0, jnp.where(m2, 1.0, 0.0))
    r = lax.broadcasted_iota(I32, (tm, tm), 0)
    c = lax.broadcasted_iota(I32, (tm, tm), 1)
    tri = jnp.where(c < r, 1.0, 0.0).astype(BF16)
    local = jnp.dot(tri, onehot.astype(BF16), preferred_element_type=F32)
    carry = carry_ref[0:1, :]
    n_tile = jnp.sum(onehot, axis=0, keepdims=True)

    def assignment(m, e):
        r_local = jnp.sum(jnp.where(m, local, 0.0), axis=-1, keepdims=True)
        r_global = r_local + jnp.sum(jnp.where(m, carry, 0.0), axis=-1, keepdims=True)
        first = jnp.sum(jnp.where(lane < e, n_tile, 0.0), axis=-1, keepdims=True)
        return first + r_local, r_global

    p1, g1 = assignment(m1, e1)
    p2, g2 = assignment(m2, e2)
    cols = [p1, p2, g1, g2, e1.astype(F32), e2.astype(F32)]
    table = jnp.zeros((tm, LANES), F32)
    for j, col in enumerate(cols):
        table = jnp.where(lane == j, col, table)
    idx_ref[0] = jnp.transpose(table)[0:IDX_ROWS, :].astype(I32)
    tile_ref[0] = jnp.where(lax.broadcasted_iota(I32, (8, LANES), 0) == 0, carry, n_tile)
    carry_ref[...] = carry_ref[...] + n_tile
    cnt_ref[...] = carry_ref[...]


def _rank(eid):
    t = eid.shape[0]
    tm = RANK_TM
    n_tiles = t // tm
    return pl.pallas_call(
        _rank_kernel,
        grid=(n_tiles,),
        in_specs=[pl.BlockSpec((tm, LANES), lambda i: (i, 0))],
        out_specs=[pl.BlockSpec((1, IDX_ROWS, tm), lambda i: (i, 0, 0)),
                   pl.BlockSpec((1, 8, LANES), lambda i: (i, 0, 0)),
                   pl.BlockSpec((8, LANES), lambda i: (0, 0))],
        out_shape=[jax.ShapeDtypeStruct((n_tiles, IDX_ROWS, tm), I32),
                   jax.ShapeDtypeStruct((n_tiles, 8, LANES), F32),
                   jax.ShapeDtypeStruct((8, LANES), F32)],
        scratch_shapes=[pltpu.VMEM((8, LANES), F32)],
        compiler_params=_cparams(("arbitrary",)),
        name="expert_rank",
    )(eid)


PIECE = 8


def _dispatch_kernel(cnt_ref, pend_ref, nused_ref, pos_ref, run_ref, hp_ref, xs_ref, sbuf, zero_ref, sem, zsem):
    tm = hp_ref.shape[0] // HP_TILES
    zb = zero_ref.shape[0]
    n_blocks = xs_ref.shape[0] // zb

    @pl.when(pl.program_id(0) == 0)
    def _():
        zero_ref[...] = jnp.zeros_like(zero_ref)
        sbuf[pl.ds(2 * tm * HP_TILES, PIECE * HP_TILES), :] = jnp.zeros((PIECE * HP_TILES, LANES), U32)

        def zero_copy(block):
            return pltpu.make_async_copy(zero_ref, xs_ref.at[pl.ds(pl.multiple_of(block * zb, zb), zb)], zsem)

        def tail_start(j, carry):
            zero_copy(j).start()
            return carry

        def tail_wait(j, carry):
            zero_copy(j).wait()
            return carry

        for e in range(N_EXPERTS):
            @pl.when(cnt_ref[e] > 0)
            def _():
                zero_copy(pend_ref[e] - 1).start()
        lax.fori_loop(nused_ref[0], n_blocks, tail_start, 0)
        for e in range(N_EXPERTS):
            @pl.when(cnt_ref[e] > 0)
            def _():
                zero_copy(pend_ref[e] - 1).wait()
        lax.fori_loop(nused_ref[0], n_blocks, tail_wait, 0)

    def place(t, carry):
        row = hp_ref[pl.ds(pl.multiple_of(t * HP_TILES, HP_TILES), HP_TILES), :]
        for k in range(TOP_K):
            sbuf[pl.ds(pl.multiple_of(pos_ref[0, k, t] * HP_TILES, HP_TILES), HP_TILES), :] = row
        return carry

    lax.fori_loop(0, tm, place, 0, unroll=8)

    def piece_copy(src_tok, dst_slot):
        rows = PIECE * HP_TILES
        return pltpu.make_async_copy(sbuf.at[pl.ds(pl.multiple_of(src_tok * HP_TILES, HP_TILES), rows)],
                                     xs_ref.at[pl.ds(pl.multiple_of(dst_slot * HP_TILES, HP_TILES), rows)], sem)

    for e in range(N_EXPERTS):
        slot0 = run_ref[0, 0, e]
        first = run_ref[0, 0, 2 * N_EXPERTS + e]

        def run_piece(j, carry):
            piece_copy(first + PIECE * j, slot0 + PIECE * j).start()
            return carry

        lax.fori_loop(0, run_ref[0, 0, N_EXPERTS + e], run_piece, 0)

    def drain(j, carry):
        piece_copy(0, 0).wait()
        return carry

    lax.fori_loop(0, run_ref[0, 0, 3 * N_EXPERTS], drain, 0)


def _dispatch(cnt, pend_blocks, nused, pos, runs, hp, n_slots):
    n_tiles, _, tm = pos.shape
    grid_spec = pltpu.PrefetchScalarGridSpec(
        num_scalar_prefetch=3,
        grid=(n_tiles,),
        in_specs=[pl.BlockSpec((1, TOP_K, tm), lambda i, *_: (i, 0, 0), memory_space=pltpu.SMEM),
                  pl.BlockSpec((1, 1, LANES), lambda i, *_: (i, 0, 0), memory_space=pltpu.SMEM),
                  pl.BlockSpec((tm * HP_TILES, LANES), lambda i, *_: (i, 0))],
        out_specs=pl.BlockSpec(memory_space=pl.ANY),
        scratch_shapes=[pltpu.VMEM(((TOP_K * tm + PIECE) * HP_TILES, LANES), U32),
                        pltpu.VMEM((MOE_TM * HP_TILES, LANES), U32),
                        pltpu.SemaphoreType.DMA(()), pltpu.SemaphoreType.DMA(())],
    )
    return pl.pallas_call(
        _dispatch_kernel,
        grid_spec=grid_spec,
        out_shape=jax.ShapeDtypeStruct((n_slots * HP_TILES, LANES), U32),
        compiler_params=_cparams(("arbitrary",)),
        name="moe_dispatch",
    )(cnt, pend_blocks, nused, pos, runs, hp)


MOE_DOWN_CHUNKS = 4


def _moe_kernel(bexp_ref, nused_ref, xs_ref, wg_ref, wu_ref, wd_ref, o_ref, wgb, wub, wdb):
    i = pl.program_id(0)
    nused = nused_ref[0]
    tm = xs_ref.shape[0] // HP_TILES

    @pl.when((i < nused) & ((i == 0) | (bexp_ref[i] != bexp_ref[jnp.maximum(i - 1, 0)])))
    def _():
        wgb[...] = wg_ref[0].astype(BF16)
        wub[...] = wu_ref[0].astype(BF16)
        wdb[...] = wd_ref[0].astype(BF16)

    @pl.when(i < nused)
    def _():
        x = _unpack_halves(_load_row_tiles(xs_ref, tm, HP_TILES))
        hg = jnp.dot(x, wgb[...], preferred_element_type=F32)
        hu = jnp.dot(x, wub[...], preferred_element_type=F32)
        hdn = ((hg * jax.nn.sigmoid(hg)) * hu).astype(BF16)
        cw = D_MODEL // MOE_DOWN_CHUNKS
        for c in range(MOE_DOWN_CHUNKS):
            y = jnp.dot(hdn, wdb[:, c * cw:(c + 1) * cw], preferred_element_type=F32)
            for j in range(cw // LANES):
                o_ref[pl.ds(c * (cw // LANES) + j, tm, stride=YS_TILES), :] = y[:, j * LANES:(j + 1) * LANES]

    @pl.when(i >= nused)
    def _():
        o_ref[...] = jnp.zeros_like(o_ref)


def _moe(bexp, nused, xs, wg, wu, wd):
    tm = MOE_TM
    n_slots = xs.shape[0] // HP_TILES
    n_blocks = n_slots // tm
    grid_spec = pltpu.PrefetchScalarGridSpec(
        num_scalar_prefetch=2,
        grid=(n_blocks,),
        in_specs=[pl.BlockSpec((tm * HP_TILES, LANES), lambda i, be, nu: (jnp.minimum(i, nu[0] - 1), 0)),
                  pl.BlockSpec((1, D_MODEL, D_EXPERT), lambda i, be, nu: (be[i], 0, 0)),
                  pl.BlockSpec((1, D_MODEL, D_EXPERT), lambda i, be, nu: (be[i], 0, 0)),
                  pl.BlockSpec((1, D_EXPERT, D_MODEL), lambda i, be, nu: (be[i], 0, 0))],
        out_specs=pl.BlockSpec((tm * YS_TILES, LANES), lambda i, be, nu: (i, 0)),
        scratch_shapes=[pltpu.VMEM((D_MODEL, D_EXPERT), BF16), pltpu.VMEM((D_MODEL, D_EXPERT), BF16),
                        pltpu.VMEM((D_EXPERT, D_MODEL), BF16)],
    )
    return pl.pallas_call(
        _moe_kernel,
        grid_spec=grid_spec,
        out_shape=jax.ShapeDtypeStruct((n_slots * YS_TILES, LANES), F32),
        compiler_params=_cparams(("arbitrary",)),
        name="moe_experts",
    )(bexp, nused, xs, wg, wu, wd)


def _combine_kernel(dest_ref, x_ref, gate_ref, g_ref, ys_ref, o_ref, y1_ref, y2_ref, sem):
    tm = x_ref.shape[0]
    bufs = (y1_ref, y2_ref)

    def row_copy(r, k, d):
        src = ys_ref.at[pl.ds(pl.multiple_of(d * YS_TILES, YS_TILES), YS_TILES)]
        return pltpu.make_async_copy(src, bufs[k].at[pl.ds(pl.multiple_of(r * YS_TILES, YS_TILES), YS_TILES)], sem)

    def issue(r, carry):
        for k in range(TOP_K):
            row_copy(r, k, dest_ref[0, k, r]).start(priority=k)
        return carry

    def drain(r, carry):
        for k in range(TOP_K):
            row_copy(r, k, dest_ref[0, k, r]).wait()
        return carry

    lax.fori_loop(0, tm, issue, 0)
    lax.fori_loop(0, tm, drain, 0)
    gates = gate_ref[...]
    moe = (gates[:, 0:1] * _load_row_tiles(y1_ref, tm, YS_TILES)
           + gates[:, 1:2] * _load_row_tiles(y2_ref, tm, YS_TILES))
    o_ref[...] = _rms_scale(x_ref[...] + moe, g_ref[...], NORM_EPS)


def _combine(dest, x2, gates, g, ys, row0, n_rows):
    tm = ROW_TM
    blk0 = row0 // tm
    per_tile = dest.shape[2] // tm
    return pl.pallas_call(
        _combine_kernel,
        grid=(n_rows // tm,),
        in_specs=[pl.BlockSpec((1, TOP_K, tm), lambda i: ((blk0 + i) // per_tile, 0, (blk0 + i) % per_tile),
                               memory_space=pltpu.SMEM),
                  pl.BlockSpec((tm, D_MODEL), lambda i: (blk0 + i, 0)),
                  pl.BlockSpec((tm, LANES), lambda i: (blk0 + i, 0)),
                  pl.BlockSpec((1, D_MODEL), lambda i: (0, 0)),
                  pl.BlockSpec(memory_space=pl.ANY)],
        out_specs=pl.BlockSpec((tm, D_MODEL), lambda i: (i, 0)),
        out_shape=jax.ShapeDtypeStruct((n_rows, D_MODEL), F32),
        scratch_shapes=[pltpu.VMEM((tm * YS_TILES, LANES), F32), pltpu.VMEM((tm * YS_TILES, LANES), F32),
                        pltpu.SemaphoreType.DMA(())],
        compiler_params=_cparams(("arbitrary",)),
        name="moe_combine",
    )(dest, x2, gates, g, ys)


def _routing_plan(idx, tiles, counts):
    cnt = counts[0, :N_EXPERTS].astype(I32)
    padded = jnp.where(cnt > 0, (cnt + PIECE - 1 + MOE_TM - 1) // MOE_TM * MOE_TM, 0)
    pends = jnp.cumsum(padded)
    pstarts = pends - padded
    n_tiles, _, tm = idx.shape
    n_blocks = n_tiles * tm * TOP_K // MOE_TM + N_EXPERTS + 1
    blk0 = jnp.arange(n_blocks, dtype=I32) * MOE_TM
    bexp = jnp.minimum(jnp.sum(pends[None, :] <= blk0[:, None], axis=1), N_EXPERTS - 1).astype(I32)
    nused = (pends[-1:] // MOE_TM).astype(I32)
    before = tiles[:, 0, :N_EXPERTS].astype(I32)
    inside = tiles[:, 1, :N_EXPERTS].astype(I32)
    first = jnp.cumsum(inside, axis=1) - inside
    pieces = (inside + PIECE - 1) // PIECE
    runs = jnp.concatenate([pstarts[None, :] + before, pieces, first,
                            jnp.sum(pieces, axis=1, keepdims=True),
                            jnp.zeros((n_tiles, LANES - 3 * N_EXPERTS - 1), I32)], axis=1).reshape(n_tiles, 1, LANES)
    pos = idx[:, 0:TOP_K, :]
    eids = idx[:, 2 * TOP_K:3 * TOP_K, :]
    dest = idx[:, TOP_K:2 * TOP_K, :]
    for e in range(N_EXPERTS):
        dest = dest + jnp.where(eids == e, pstarts[e], 0)
    return pos, runs, dest, cnt, (pends // MOE_TM).astype(I32), bexp, nused, n_blocks * MOE_TM


def kernel(x_prompt, x_sample, mem_prompt, mem_sample, mix_norm_g, w_in, na_rpb, diff_lambda_q1, diff_lambda_k1, diff_lambda_q2, diff_lambda_k2, diff_subln_g, w_branch_na, w_branch_diff, w_out, xa_norm_g, mem_norm_g, xa_w_q, xa_w_kv, xa_w_o, ffn_norm_g, router_group_w, router_group_b, router_expert_w, router_expert_b, w_gate, w_up, w_down, final_norm_g):
    nb_p, nb_s = x_prompt.shape[0], x_sample.shape[0]
    nb = nb_p + nb_s
    t = nb * SEQ
    xp = x_prompt.reshape(nb_p * SEQ, D_MODEL)
    xs_in = x_sample.reshape(nb_s * SEQ, D_MODEL)
    mem = jnp.concatenate([mem_prompt, mem_sample], axis=0).reshape(nb * MEM_LEN, D_MODEL)
    row = lambda v: v.reshape(1, -1).astype(F32)
    bf = lambda w: w.astype(BF16)

    qcol = 3 * NA_WIDTH
    colscale = jnp.ones((1, IN_COLS), F32).at[:, qcol:qcol + DIFF_QK_WIDTH].set(DIFF_QK_DIM ** -0.5 * LOG2E)
    proj = _inproj(xp, xs_in, row(mix_norm_g[0]), bf(w_in[0]), colscale)
    proj3 = proj.reshape(nb, SEQ, IN_COLS)
    a = _na_attention(proj3, _na_bias_blocks(na_rpb[0]))
    b = _diff_attention(proj3, row(diff_lambda_q1[0]), row(diff_lambda_k1[0]),
                        row(diff_lambda_q2[0]), row(diff_lambda_k2[0]), row(diff_subln_g[0]))
    x1 = _merge(a.reshape(t, NA_WIDTH), b.reshape(t, DIFF_V_WIDTH), proj, xp, xs_in,
                bf(w_branch_na[0]), bf(w_branch_diff[0]), bf(w_out[0]))
    kv = _memkv(mem, row(mem_norm_g[0]), bf(xa_w_kv[0]))
    w_router = jnp.zeros((D_MODEL, LANES), F32)
    w_router = w_router.at[:, :N_GROUPS].set(router_group_w[0]).at[:, N_GROUPS:N_GROUPS + N_EXPERTS].set(router_expert_w[0])
    b_router = jnp.zeros((1, LANES), F32)
    b_router = b_router.at[0, :N_GROUPS].set(router_group_b[0]).at[0, N_GROUPS:N_GROUPS + N_EXPERTS].set(router_expert_b[0])
    x2, hp, gates, eid = _xattn(x1, kv, row(xa_norm_g[0]), bf(xa_w_q[0]), bf(xa_w_o[0]),
                                row(ffn_norm_g[0]), bf(w_router), b_router)
    idx, tiles, counts = _rank(eid)
    pos, runs, dest, cnt, pend_blocks, bexp, nused, n_slots = _routing_plan(idx, tiles, counts)
    xs = _dispatch(cnt, pend_blocks, nused, pos, runs, hp, n_slots)
    ys = _moe(bexp, nused, xs, w_gate[0], w_up[0], w_down[0])
    fg = row(final_norm_g)
    y_p = _combine(dest, x2, gates, fg, ys, 0, nb_p * SEQ).reshape(nb_p, SEQ, D_MODEL)
    y_s = _combine(dest, x2, gates, fg, ys, nb_p * SEQ, nb_s * SEQ).reshape(nb_s, SEQ, D_MODEL)
    return (y_p, y_s)
```

```python
import functools
import math

import jax
import jax.numpy as jnp
from jax import lax
from jax.experimental import pallas as pl
from jax.experimental.pallas import tpu as pltpu

F32 = jnp.float32
BF16 = jnp.bfloat16
I32 = jnp.int32
U32 = jnp.uint32

D_MODEL = 2048
SEQ = 2048
GRID_W = 64
ROWS = SEQ // GRID_W
NA_HEADS = 8
NA_HEAD_DIM = 128
NA_WIDTH = NA_HEADS * NA_HEAD_DIM
NA_KH = 8
NA_KW = 16
DIFF_HEADS = 8
DIFF_QK_DIM = 64
DIFF_V_DIM = 128
DIFF_QK_WIDTH = DIFF_HEADS * 2 * DIFF_QK_DIM
DIFF_V_WIDTH = DIFF_HEADS * DIFF_V_DIM
IN_COLS = 3 * NA_WIDTH + 2 * DIFF_QK_WIDTH + DIFF_V_WIDTH + 2 * D_MODEL
MEM_LEN = 256
XA_HEADS = 4
XA_HEAD_DIM = 128
XA_WIDTH = XA_HEADS * XA_HEAD_DIM
N_GROUPS = 4
EXPERTS_PER_GROUP = 8
N_EXPERTS = N_GROUPS * EXPERTS_PER_GROUP
TOP_K = 2
D_EXPERT = 512
NORM_EPS = 1e-6
SUBLN_EPS = 1e-5
NEG_INF = -1e30
LAMBDA_INIT = 0.8 - 0.6 * math.exp(-0.3 * 0)
LOG2E = math.log2(math.e)

LANES = 128
HP_TILES = D_MODEL // 2 // LANES
YS_TILES = D_MODEL // LANES
VMEM_LIMIT = 56 * 1024 * 1024

INPROJ_TM = 1024
INPROJ_TN = 1024
NORM_CHUNK = 128
DIFF_TQ = 512
DIFF_SUB = 256
MERGE_TM = 256
XA_TM = 256
RANK_TM = 512
MOE_TM = 512
ROW_TM = 256


def _cparams(sem):
    return pltpu.CompilerParams(dimension_semantics=sem, vmem_limit_bytes=VMEM_LIMIT)


def _rms_scale(x, g, eps):
    ms = jnp.mean(x * x, axis=-1, keepdims=True)
    return x * lax.rsqrt(ms + eps) * g


def _dot_nt(a, b):
    return lax.dot_general(a, b, (((1,), (1,)), ((), ())), preferred_element_type=F32)


def _two_group_specs(tm, width, n_first):
    first = pl.BlockSpec((tm, width), lambda i, *_: (jnp.minimum(i, n_first - 1), 0))
    second = pl.BlockSpec((tm, width), lambda i, *_: (jnp.maximum(i - n_first, 0), 0))
    return first, second


def _inproj_kernel(n_first, xp_ref, xs_ref, g_ref, w_ref, cs_ref, o_ref, h_ref):
    def normalise(x_ref):
        def chunk(c, carry):
            rows = pl.ds(pl.multiple_of(c * NORM_CHUNK, NORM_CHUNK), NORM_CHUNK)
            h_ref[rows, :] = _rms_scale(x_ref[rows, :], g_ref[...], NORM_EPS).astype(BF16)
            return carry
        lax.fori_loop(0, x_ref.shape[0] // NORM_CHUNK, chunk, 0)

    first_col = pl.program_id(1) == 0
    in_first = pl.program_id(0) < n_first
    pl.when(first_col & in_first)(lambda: normalise(xp_ref))
    pl.when(first_col & jnp.logical_not(in_first))(lambda: normalise(xs_ref))
    acc = jnp.dot(h_ref[...], w_ref[...], preferred_element_type=F32)
    o_ref[...] = (acc * cs_ref[...]).astype(o_ref.dtype)


def _inproj(xp, xs, g, w, colscale):
    d = xp.shape[1]
    t = xp.shape[0] + xs.shape[0]
    n = w.shape[1]
    tm = INPROJ_TM
    n_first = xp.shape[0] // tm
    spec_p, spec_s = _two_group_specs(tm, d, n_first)
    return pl.pallas_call(
        functools.partial(_inproj_kernel, n_first),
        grid=(t // tm, n // INPROJ_TN),
        in_specs=[spec_p, spec_s,
                  pl.BlockSpec((1, d), lambda i, j: (0, 0)),
                  pl.BlockSpec((d, INPROJ_TN), lambda i, j: (0, j)),
                  pl.BlockSpec((1, INPROJ_TN), lambda i, j: (0, j))],
        out_specs=pl.BlockSpec((tm, INPROJ_TN), lambda i, j: (i, j)),
        out_shape=jax.ShapeDtypeStruct((t, n), BF16),
        scratch_shapes=[pltpu.VMEM((tm, d), BF16)],
        compiler_params=_cparams(("parallel", "arbitrary")),
        name="inproj",
    )(xp, xs, g, w, colscale)


NA_QROWS = 4
NA_KROWS = 12
NA_TILES = ROWS // NA_QROWS


def _na_tile_key_row(ti):
    return min(max(NA_QROWS * ti - NA_KH // 2, 0), ROWS - NA_KROWS)


def _na_bias_blocks(rpb):
    cols = jnp.arange(GRID_W)
    cstart = jnp.clip(cols - NA_KW // 2, 0, GRID_W - NA_KW)
    col_ok = (cols[None, :] >= cstart[:, None]) & (cols[None, :] < cstart[:, None] + NA_KW)
    dc = jnp.clip(cols[None, :] - cols[:, None], 1 - NA_KW, NA_KW - 1) + NA_KW - 1
    pick = (dc[None] == jnp.arange(2 * NA_KW - 1)[:, None, None]).astype(F32)
    t = jnp.einsum("hdx,xck->hdck", rpb.astype(F32), pick, precision=lax.Precision.HIGHEST)
    return jnp.where(col_ok[None, None], t, NEG_INF)


def _na_kernel(q_ref, k_ref, v_ref, t_ref, o_ref, bias_ref):
    scale = NA_HEAD_DIM ** -0.5
    tq = NA_QROWS * GRID_W
    tk = NA_KROWS * GRID_W

    @pl.when(pl.program_id(1) == 0)
    def _():
        masked = jnp.full((GRID_W, GRID_W), NEG_INF, F32)
        for variant, ti in enumerate((0, 1, NA_TILES - 1)):
            for jr in range(NA_QROWS):
                r = NA_QROWS * ti + jr
                rs = min(max(r - NA_KH // 2, 0), ROWS - NA_KH)
                blocks = []
                for i in range(NA_KROWS):
                    kr = _na_tile_key_row(ti) + i
                    blocks.append(t_ref[0, kr - r + NA_KH - 1] if rs <= kr < rs + NA_KH else masked)
                bias_ref[variant, jr * GRID_W:(jr + 1) * GRID_W, :] = jnp.concatenate(blocks, axis=1)

    for ti in range(NA_TILES):
        variant = 0 if ti == 0 else (2 if ti == NA_TILES - 1 else 1)
        k0 = _na_tile_key_row(ti) * GRID_W
        qrows = slice(ti * tq, (ti + 1) * tq)
        krows = slice(k0, k0 + tk)
        s = _dot_nt(q_ref[0, qrows, :], k_ref[0, krows, :]) * scale + bias_ref[variant]
        m = jnp.max(s, axis=-1, keepdims=True)
        e = jnp.exp(s - m)
        l = jnp.sum(e, axis=-1, keepdims=True)
        o = jnp.dot(e.astype(BF16), v_ref[0, krows, :], preferred_element_type=F32)
        o_ref[0, qrows, :] = (o / l).astype(o_ref.dtype)


def _na_attention(proj3, blocks):
    b = proj3.shape[0]
    hd = NA_HEAD_DIM
    return pl.pallas_call(
        _na_kernel,
        grid=(NA_HEADS, b),
        in_specs=[pl.BlockSpec((1, SEQ, hd), lambda h, i: (i, 0, h)),
                  pl.BlockSpec((1, SEQ, hd), lambda h, i: (i, 0, NA_HEADS + h)),
                  pl.BlockSpec((1, SEQ, hd), lambda h, i: (i, 0, 2 * NA_HEADS + h)),
                  pl.BlockSpec((1,) + blocks.shape[1:], lambda h, i: (h, 0, 0, 0))],
        out_specs=pl.BlockSpec((1, SEQ, hd), lambda h, i: (i, 0, h)),
        out_shape=jax.ShapeDtypeStruct((b, SEQ, NA_WIDTH), BF16),
        scratch_shapes=[pltpu.VMEM((3, NA_QROWS * GRID_W, NA_KROWS * GRID_W), F32)],
        compiler_params=_cparams(("arbitrary", "arbitrary")),
        name="na_attn",
    )(proj3, proj3, proj3, blocks)


def _alibi_table(tq):
    slopes = jnp.exp2(-8.0 * jnp.arange(1, DIFF_HEADS + 1, dtype=F32) / DIFF_HEADS)
    r = jnp.arange(tq)[:, None]
    x = jnp.arange(2 * SEQ - tq)[None, :]
    dist = jnp.abs(r - x + (SEQ - tq)).astype(F32)
    return (slopes * LOG2E)[:, None, None] * dist[None]


def _diff_kernel(lq1_ref, lk1_ref, lq2_ref, lk2_ref, q_ref, k_ref, v_ref, g_ref, alibi_ref, o_ref):
    qi = pl.program_id(2)
    ts = alibi_ref.shape[1]
    nsub = q_ref.shape[1] // ts
    lam = (jnp.exp(jnp.sum(lq1_ref[...] * lk1_ref[...], axis=-1, keepdims=True))
           - jnp.exp(jnp.sum(lq2_ref[...] * lk2_ref[...], axis=-1, keepdims=True)) + LAMBDA_INIT)
    k = k_ref[0]

    def scores(u):
        q = q_ref[0, u * ts:(u + 1) * ts, :]
        return [_dot_nt(q[:, mi * DIFF_QK_DIM:(mi + 1) * DIFF_QK_DIM], k[:, mi * DIFF_QK_DIM:(mi + 1) * DIFF_QK_DIM])
                for mi in range(2)]

    def expo(s, bias):
        s = s - bias
        m = jnp.max(s, axis=-1, keepdims=True)
        e = jnp.exp2(s - m)
        return e, jnp.sum(e, axis=-1, keepdims=True)

    pending = scores(0)
    for u in range(nsub):
        s0, s1 = pending
        if u + 1 < nsub:
            pending = scores(u + 1)
        off = (SEQ - ts) - (qi * nsub + u) * ts
        bias = alibi_ref[0, :, pl.ds(pl.multiple_of(off, ts), SEQ)]
        e0, l0 = expo(s0, bias)
        e1, l1 = expo(s1, bias)
        a = (e0 - e1 * (lam * l0 / l1)).astype(BF16)
        o = jnp.dot(a, v_ref[0], preferred_element_type=F32) / l0
        o = _rms_scale(o, g_ref[...], SUBLN_EPS) * (1.0 - LAMBDA_INIT)
        o_ref[0, u * ts:(u + 1) * ts, :] = o.astype(o_ref.dtype)


def _diff_attention(proj3, lq1, lk1, lq2, lk2, subln_g):
    b = proj3.shape[0]
    tq = DIFF_TQ
    qoff = 3 * NA_WIDTH // LANES
    koff = qoff + DIFF_QK_WIDTH // LANES
    voff = koff + DIFF_QK_WIDTH // LANES
    vec = lambda n: pl.BlockSpec((1, n), lambda h, i, j: (0, 0))
    return pl.pallas_call(
        _diff_kernel,
        grid=(DIFF_HEADS, b, SEQ // tq),
        in_specs=[vec(DIFF_QK_DIM), vec(DIFF_QK_DIM), vec(DIFF_QK_DIM), vec(DIFF_QK_DIM),
                  pl.BlockSpec((1, tq, LANES), lambda h, i, j: (i, j, qoff + h)),
                  pl.BlockSpec((1, SEQ, LANES), lambda h, i, j: (i, 0, koff + h)),
                  pl.BlockSpec((1, SEQ, LANES), lambda h, i, j: (i, 0, voff + h)),
                  vec(DIFF_V_DIM),
                  pl.BlockSpec((1, DIFF_SUB, 2 * SEQ - DIFF_SUB), lambda h, i, j: (h, 0, 0))],
        out_specs=pl.BlockSpec((1, tq, DIFF_V_DIM), lambda h, i, j: (i, j, h)),
        out_shape=jax.ShapeDtypeStruct((b, SEQ, DIFF_V_WIDTH), BF16),
        compiler_params=_cparams(("arbitrary", "arbitrary", "arbitrary")),
        name="diff_attn",
    )(lq1, lk1, lq2, lk2, proj3, proj3, proj3, subln_g, _alibi_table(DIFF_SUB))


def _merge_kernel(n_first, a_ref, b_ref, gna_ref, gdf_ref, xp_ref, xs_ref, wna_ref, wdf_ref, wout_ref, o_ref):
    pa = jnp.dot(a_ref[...], wna_ref[...], preferred_element_type=F32)
    pb = jnp.dot(b_ref[...], wdf_ref[...], preferred_element_type=F32)
    merged = (jax.nn.sigmoid(gna_ref[...].astype(F32)) * pa
              + jax.nn.sigmoid(gdf_ref[...].astype(F32)) * pb)
    delta = jnp.dot(merged.astype(BF16), wout_ref[...], preferred_element_type=F32)
    in_first = pl.program_id(0) < n_first

    @pl.when(in_first)
    def _():
        o_ref[...] = xp_ref[...] + delta

    @pl.when(jnp.logical_not(in_first))
    def _():
        o_ref[...] = xs_ref[...] + delta


def _merge(a, b, proj, xp, xs, wna, wdf, wout):
    t = xp.shape[0] + xs.shape[0]
    tm = MERGE_TM
    gna_blk = (3 * NA_WIDTH + 2 * DIFF_QK_WIDTH + DIFF_V_WIDTH) // D_MODEL
    const = lambda shape: pl.BlockSpec(shape, lambda i: (0, 0))
    n_first = xp.shape[0] // tm
    spec_p, spec_s = _two_group_specs(tm, D_MODEL, n_first)
    return pl.pallas_call(
        functools.partial(_merge_kernel, n_first),
        grid=(t // tm,),
        in_specs=[pl.BlockSpec((tm, NA_WIDTH), lambda i: (i, 0)),
                  pl.BlockSpec((tm, DIFF_V_WIDTH), lambda i: (i, 0)),
                  pl.BlockSpec((tm, D_MODEL), lambda i: (i, gna_blk)),
                  pl.BlockSpec((tm, D_MODEL), lambda i: (i, gna_blk + 1)),
                  spec_p, spec_s,
                  const((NA_WIDTH, D_MODEL)), const((DIFF_V_WIDTH, D_MODEL)), const((D_MODEL, D_MODEL))],
        out_specs=pl.BlockSpec((tm, D_MODEL), lambda i: (i, 0)),
        out_shape=jax.ShapeDtypeStruct((t, D_MODEL), F32),
        compiler_params=_cparams(("parallel",)),
        name="merge_outproj",
    )(a, b, proj, proj, xp, xs, wna, wdf, wout)


def _memkv_kernel(m_ref, g_ref, w_ref, o_ref):
    h = _rms_scale(m_ref[...], g_ref[...], NORM_EPS).astype(BF16)
    o_ref[...] = jnp.dot(h, w_ref[...], preferred_element_type=F32).astype(o_ref.dtype)


def _memkv(mem, g, w):
    t = mem.shape[0]
    return pl.pallas_call(
        _memkv_kernel,
        grid=(t // MEM_LEN,),
        in_specs=[pl.BlockSpec((MEM_LEN, D_MODEL), lambda i: (i, 0)),
                  pl.BlockSpec((1, D_MODEL), lambda i: (0, 0)),
                  pl.BlockSpec((D_MODEL, 2 * XA_WIDTH), lambda i: (0, 0))],
        out_specs=pl.BlockSpec((MEM_LEN, 2 * XA_WIDTH), lambda i: (i, 0)),
        out_shape=jax.ShapeDtypeStruct((t, 2 * XA_WIDTH), BF16),
        compiler_params=_cparams(("parallel",)),
        name="mem_kv",
    )(mem, g, w)


def _route(logits):
    lane = lax.broadcasted_iota(I32, logits.shape, 1).astype(F32)
    ninf = -jnp.inf
    big = float(LANES)
    gl = jnp.where(lane < N_GROUPS, logits, ninf)
    gmax = jnp.max(gl, axis=-1, keepdims=True)
    g = jnp.min(jnp.where(gl == gmax, lane, big), axis=-1, keepdims=True)
    pg = 1.0 / jnp.sum(jnp.exp(gl - gmax), axis=-1, keepdims=True)
    lo = N_GROUPS + EXPERTS_PER_GROUP * g
    el = jnp.where((lane >= lo) & (lane < lo + EXPERTS_PER_GROUP), logits, ninf)
    v1 = jnp.max(el, axis=-1, keepdims=True)
    i1 = jnp.min(jnp.where(el == v1, lane, big), axis=-1, keepdims=True)
    el2 = jnp.where(lane == i1, ninf, el)
    v2 = jnp.max(el2, axis=-1, keepdims=True)
    i2 = jnp.min(jnp.where(el2 == v2, lane, big), axis=-1, keepdims=True)
    t = jnp.exp(v2 - v1)
    den = 1.0 + t
    gate1 = pg * (1.0 / den)
    gate2 = pg * (t / den)
    gates = jnp.where(lane == 0.0, gate1, jnp.where(lane == 1.0, gate2, 0.0))
    eids = jnp.where(lane == 0.0, i1 - N_GROUPS, jnp.where(lane == 1.0, i2 - N_GROUPS, 0.0)).astype(I32)
    return gates, eids


def _pack_halves(h):
    half = h.shape[1] // 2
    hi = pltpu.bitcast(h[:, :half].astype(BF16).astype(F32), U32)
    lo = pltpu.bitcast(h[:, half:].astype(BF16).astype(F32), U32)
    return hi | (lo >> 16)


def _unpack_halves(w):
    hi = pltpu.bitcast(w & jnp.uint32(0xFFFF0000), F32).astype(BF16)
    lo = pltpu.bitcast(w << 16, F32).astype(BF16)
    return jnp.concatenate([hi, lo], axis=1)


def _store_row_tiles(ref, x):
    n, width = x.shape
    c = width // LANES
    for j in range(c):
        ref[pl.ds(j, n, stride=c), :] = x[:, j * LANES:(j + 1) * LANES]


def _load_row_tiles(ref, n, c):
    return jnp.concatenate([ref[pl.ds(j, n, stride=c), :] for j in range(c)], axis=1)


def _xattn_kernel(x_ref, kv_ref, gx_ref, wq_ref, wo_ref, gf_ref, wr_ref, br_ref,
                  x2_ref, hp_ref, gate_ref, eid_ref):
    x1 = x_ref[...]
    hq = _rms_scale(x1, gx_ref[...], NORM_EPS).astype(BF16)
    q = jnp.dot(hq, wq_ref[...], preferred_element_type=F32).astype(BF16)
    scale = XA_HEAD_DIM ** -0.5
    outs = []
    for h in range(XA_HEADS):
        sl = slice(h * XA_HEAD_DIM, (h + 1) * XA_HEAD_DIM)
        vsl = slice(XA_WIDTH + h * XA_HEAD_DIM, XA_WIDTH + (h + 1) * XA_HEAD_DIM)
        s = _dot_nt(q[:, sl], kv_ref[:, sl]) * scale
        m = jnp.max(s, axis=-1, keepdims=True)
        e = jnp.exp(s - m)
        l = jnp.sum(e, axis=-1, keepdims=True)
        o = jnp.dot(e.astype(BF16), kv_ref[:, vsl], preferred_element_type=F32)
        outs.append((o / l).astype(BF16))
    o = jnp.concatenate(outs, axis=1)
    x2 = x1 + jnp.dot(o, wo_ref[...], preferred_element_type=F32)
    x2_ref[...] = x2
    h3 = _rms_scale(x2, gf_ref[...], NORM_EPS)
    _store_row_tiles(hp_ref, _pack_halves(h3))
    logits = jnp.dot(h3.astype(BF16), wr_ref[...], preferred_element_type=F32) + br_ref[...]
    gates, eids = _route(logits)
    gate_ref[...] = gates
    eid_ref[...] = eids


def _xattn(x1, kv, gx, wq, wo, gf, wr, br):
    t = x1.shape[0]
    tm = XA_TM
    per_batch = SEQ // tm
    const = lambda shape: pl.BlockSpec(shape, lambda i: (0, 0))
    tile = lambda n: pl.BlockSpec((tm, n), lambda i: (i, 0))
    return pl.pallas_call(
        _xattn_kernel,
        grid=(t // tm,),
        in_specs=[tile(D_MODEL),
                  pl.BlockSpec((MEM_LEN, 2 * XA_WIDTH), lambda i: (i // per_batch, 0)),
                  const((1, D_MODEL)), const((D_MODEL, XA_WIDTH)), const((XA_WIDTH, D_MODEL)),
                  const((1, D_MODEL)), const((D_MODEL, LANES)), const((1, LANES))],
        out_specs=[tile(D_MODEL), pl.BlockSpec((tm * HP_TILES, LANES), lambda i: (i, 0)), tile(LANES), tile(LANES)],
        out_shape=[jax.ShapeDtypeStruct((t, D_MODEL), F32),
                   jax.ShapeDtypeStruct((t * HP_TILES, LANES), U32),
                   jax.ShapeDtypeStruct((t, LANES), F32),
                   jax.ShapeDtypeStruct((t, LANES), I32)],
        compiler_params=_cparams(("parallel",)),
        name="xattn_router",
    )(x1, kv, gx, wq, wo, gf, wr, br)


IDX_ROWS = 8
PIECE = 8


def _rank_kernel(eid_ref, gate_ref, idx_ref, gidx_ref, tile_ref, cnt_ref, carry_ref):
    @pl.when(pl.program_id(0) == 0)
    def _():
        carry_ref[...] = jnp.zeros_like(carry_ref)

    tm = eid_ref.shape[0]
    eid = eid_ref[...]
    lane = lax.broadcasted_iota(I32, (tm, LANES), 1)
    e1 = eid[:, 0:1]
    e2 = eid[:, 1:2]
    m1 = lane == e1
    m2 = lane == e2
    onehot = jnp.where(m1, 1.0, jnp.where(m2, 1.0, 0.0))
    r = lax.broadcasted_iota(I32, (tm, tm), 0)
    c = lax.broadcasted_iota(I32, (tm, tm), 1)
    tri = jnp.where(c < r, 1.0, 0.0).astype(BF16)
    local = jnp.dot(tri, onehot.astype(BF16), preferred_element_type=F32)
    carry = carry_ref[0:1, :]
    n_tile = jnp.sum(onehot, axis=0, keepdims=True)

    n_piece = jnp.floor((n_tile + (PIECE - 1)) * (1.0 / PIECE)) * PIECE

    def assignment(m, e):
        r_local = jnp.sum(jnp.where(m, local, 0.0), axis=-1, keepdims=True)
        r_global = r_local + jnp.sum(jnp.where(m, carry, 0.0), axis=-1, keepdims=True)
        first = jnp.sum(jnp.where(lane < e, n_tile, 0.0), axis=-1, keepdims=True)
        first_piece = jnp.sum(jnp.where(lane < e, n_piece, 0.0), axis=-1, keepdims=True)
        return first + r_local, r_global, first_piece + r_local

    p1, g1, q1 = assignment(m1, e1)
    p2, g2, q2 = assignment(m2, e2)
    cols = [p1, p2, g1, g2, e1.astype(F32), e2.astype(F32), q1, q2]
    table = jnp.zeros((tm, LANES), F32)
    for j, col in enumerate(cols):
        table = jnp.where(lane == j, col, table)
    idx_ref[0] = jnp.transpose(table)[0:IDX_ROWS, :].astype(I32)
    gidx_ref[0] = jnp.transpose(gate_ref[...])[0:IDX_ROWS, :]
    tile_ref[0] = jnp.where(lax.broadcasted_iota(I32, (8, LANES), 0) == 0, carry, n_tile)
    carry_ref[...] = carry_ref[...] + n_tile
    cnt_ref[...] = carry_ref[...]


def _rank(eid, gates):
    t = eid.shape[0]
    tm = RANK_TM
    n_tiles = t // tm
    return pl.pallas_call(
        _rank_kernel,
        grid=(n_tiles,),
        in_specs=[pl.BlockSpec((tm, LANES), lambda i: (i, 0)), pl.BlockSpec((tm, LANES), lambda i: (i, 0))],
        out_specs=[pl.BlockSpec((1, IDX_ROWS, tm), lambda i: (i, 0, 0)),
                   pl.BlockSpec((1, IDX_ROWS, tm), lambda i: (i, 0, 0)),
                   pl.BlockSpec((1, 8, LANES), lambda i: (i, 0, 0)),
                   pl.BlockSpec((8, LANES), lambda i: (0, 0))],
        out_shape=[jax.ShapeDtypeStruct((n_tiles, IDX_ROWS, tm), I32),
                   jax.ShapeDtypeStruct((n_tiles, IDX_ROWS, tm), F32),
                   jax.ShapeDtypeStruct((n_tiles, 8, LANES), F32),
                   jax.ShapeDtypeStruct((8, LANES), F32)],
        scratch_shapes=[pltpu.VMEM((8, LANES), F32)],
        compiler_params=_cparams(("arbitrary",)),
        name="expert_rank",
    )(eid, gates)


def _dispatch_kernel(cnt_ref, pend_ref, nused_ref, pos_ref, run_ref, hp_ref, xs_ref, sbuf, zero_ref, sem, zsem):
    tm = hp_ref.shape[0] // HP_TILES
    zb = zero_ref.shape[0]
    n_blocks = xs_ref.shape[0] // zb

    @pl.when(pl.program_id(0) == 0)
    def _():
        zero_ref[...] = jnp.zeros_like(zero_ref)
        sbuf[pl.ds(2 * tm * HP_TILES, PIECE * HP_TILES), :] = jnp.zeros((PIECE * HP_TILES, LANES), U32)

        def zero_copy(block):
            return pltpu.make_async_copy(zero_ref, xs_ref.at[pl.ds(pl.multiple_of(block * zb, zb), zb)], zsem)

        def tail_start(j, carry):
            zero_copy(j).start()
            return carry

        def tail_wait(j, carry):
            zero_copy(j).wait()
            return carry

        for e in range(N_EXPERTS):
            @pl.when(cnt_ref[e] > 0)
            def _():
                zero_copy(pend_ref[e] - 1).start()
        lax.fori_loop(nused_ref[0], n_blocks, tail_start, 0)
        for e in range(N_EXPERTS):
            @pl.when(cnt_ref[e] > 0)
            def _():
                zero_copy(pend_ref[e] - 1).wait()
        lax.fori_loop(nused_ref[0], n_blocks, tail_wait, 0)

    def place(t, carry):
        row = hp_ref[pl.ds(pl.multiple_of(t * HP_TILES, HP_TILES), HP_TILES), :]
        for k in range(TOP_K):
            sbuf[pl.ds(pl.multiple_of(pos_ref[0, k, t] * HP_TILES, HP_TILES), HP_TILES), :] = row
        return carry

    lax.fori_loop(0, tm, place, 0, unroll=8)

    def piece_copy(src_tok, dst_slot):
        rows = PIECE * HP_TILES
        return pltpu.make_async_copy(sbuf.at[pl.ds(pl.multiple_of(src_tok * HP_TILES, HP_TILES), rows)],
                                     xs_ref.at[pl.ds(pl.multiple_of(dst_slot * HP_TILES, HP_TILES), rows)], sem)

    for e in range(N_EXPERTS):
        slot0 = run_ref[0, 0, e]
        first = run_ref[0, 0, 2 * N_EXPERTS + e]

        def run_piece(j, carry):
            piece_copy(first + PIECE * j, slot0 + PIECE * j).start()
            return carry

        lax.fori_loop(0, run_ref[0, 0, N_EXPERTS + e], run_piece, 0)

    def drain(j, carry):
        piece_copy(0, 0).wait()
        return carry

    lax.fori_loop(0, run_ref[0, 0, 3 * N_EXPERTS], drain, 0)


def _dispatch(cnt, pend_blocks, nused, pos, runs, hp, n_slots):
    n_tiles, _, tm = pos.shape
    grid_spec = pltpu.PrefetchScalarGridSpec(
        num_scalar_prefetch=3,
        grid=(n_tiles,),
        in_specs=[pl.BlockSpec((1, TOP_K, tm), lambda i, *_: (i, 0, 0), memory_space=pltpu.SMEM),
                  pl.BlockSpec((1, 1, LANES), lambda i, *_: (i, 0, 0), memory_space=pltpu.SMEM),
                  pl.BlockSpec((tm * HP_TILES, LANES), lambda i, *_: (i, 0))],
        out_specs=pl.BlockSpec(memory_space=pl.ANY),
        scratch_shapes=[pltpu.VMEM(((TOP_K * tm + PIECE) * HP_TILES, LANES), U32),
                        pltpu.VMEM((MOE_TM * HP_TILES, LANES), U32),
                        pltpu.SemaphoreType.DMA(()), pltpu.SemaphoreType.DMA(())],
    )
    return pl.pallas_call(
        _dispatch_kernel,
        grid_spec=grid_spec,
        out_shape=jax.ShapeDtypeStruct((n_slots * HP_TILES, LANES), U32),
        compiler_params=_cparams(("arbitrary",)),
        name="moe_dispatch",
    )(cnt, pend_blocks, nused, pos, runs, hp)


MOE_DOWN_CHUNKS = 4


def _moe_kernel(bexp_ref, nused_ref, xs_ref, wg_ref, wu_ref, wd_ref, o_ref, wgb, wub, wdb):
    i = pl.program_id(0)
    nused = nused_ref[0]
    tm = xs_ref.shape[0] // HP_TILES

    @pl.when((i < nused) & ((i == 0) | (bexp_ref[i] != bexp_ref[jnp.maximum(i - 1, 0)])))
    def _():
        wgb[...] = wg_ref[0].astype(BF16)
        wub[...] = wu_ref[0].astype(BF16)
        wdb[...] = wd_ref[0].astype(BF16)

    @pl.when(i < nused)
    def _():
        x = _unpack_halves(_load_row_tiles(xs_ref, tm, HP_TILES))
        hg = jnp.dot(x, wgb[...], preferred_element_type=F32)
        hu = jnp.dot(x, wub[...], preferred_element_type=F32)
        hdn = ((hg * jax.nn.sigmoid(hg)) * hu).astype(BF16)
        cw = D_MODEL // MOE_DOWN_CHUNKS
        for c in range(MOE_DOWN_CHUNKS):
            y = jnp.dot(hdn, wdb[:, c * cw:(c + 1) * cw], preferred_element_type=F32)
            for j in range(cw // LANES):
                o_ref[pl.ds(c * (cw // LANES) + j, tm, stride=YS_TILES), :] = y[:, j * LANES:(j + 1) * LANES]

    @pl.when(i >= nused)
    def _():
        o_ref[...] = jnp.zeros_like(o_ref)


def _moe(bexp, nused, xs, wg, wu, wd):
    tm = MOE_TM
    n_slots = xs.shape[0] // HP_TILES
    n_blocks = n_slots // tm
    grid_spec = pltpu.PrefetchScalarGridSpec(
        num_scalar_prefetch=2,
        grid=(n_blocks,),
        in_specs=[pl.BlockSpec((tm * HP_TILES, LANES), lambda i, be, nu: (jnp.minimum(i, nu[0] - 1), 0)),
                  pl.BlockSpec((1, D_MODEL, D_EXPERT), lambda i, be, nu: (be[i], 0, 0)),
                  pl.BlockSpec((1, D_MODEL, D_EXPERT), lambda i, be, nu: (be[i], 0, 0)),
                  pl.BlockSpec((1, D_EXPERT, D_MODEL), lambda i, be, nu: (be[i], 0, 0))],
        out_specs=pl.BlockSpec((tm * YS_TILES, LANES), lambda i, be, nu: (i, 0)),
        scratch_shapes=[pltpu.VMEM((D_MODEL, D_EXPERT), BF16), pltpu.VMEM((D_MODEL, D_EXPERT), BF16),
                        pltpu.VMEM((D_EXPERT, D_MODEL), BF16)],
    )
    return pl.pallas_call(
        _moe_kernel,
        grid_spec=grid_spec,
        out_shape=jax.ShapeDtypeStruct((n_slots * YS_TILES, LANES), F32),
        compiler_params=_cparams(("arbitrary",)),
        name="moe_experts",
    )(bexp, nused, xs, wg, wu, wd)


def _combine_kernel(pos_ref, run_ref, gate_ref, x_ref, g_ref, ys_ref, o_ref, ybuf, mbuf, sem):
    tm = x_ref.shape[0]

    def piece_copy(src_slot, dst_pos):
        rows = PIECE * YS_TILES
        return pltpu.make_async_copy(ys_ref.at[pl.ds(pl.multiple_of(src_slot * YS_TILES, YS_TILES), rows)],
                                     ybuf.at[pl.ds(pl.multiple_of(dst_pos * YS_TILES, YS_TILES), rows)], sem)

    for e in range(N_EXPERTS):
        slot0 = run_ref[0, 0, e]
        first = run_ref[0, 0, 2 * N_EXPERTS + e]

        def run_piece(j, carry):
            piece_copy(slot0 + PIECE * j, first + PIECE * j).start()
            return carry

        lax.fori_loop(0, run_ref[0, 0, N_EXPERTS + e], run_piece, 0)

    def drain(j, carry):
        piece_copy(0, 0).wait()
        return carry

    lax.fori_loop(0, run_ref[0, 0, 3 * N_EXPERTS], drain, 0)

    def token(t, carry):
        acc = None
        for k in range(TOP_K):
            rows = pl.ds(pl.multiple_of(pos_ref[0, k, t] * YS_TILES, YS_TILES), YS_TILES)
            term = ybuf[rows, :] * gate_ref[0, k, t]
            acc = term if acc is None else acc + term
        mbuf[pl.ds(pl.multiple_of(t * YS_TILES, YS_TILES), YS_TILES), :] = acc
        return carry

    lax.fori_loop(0, tm, token, 0, unroll=8)
    moe = _load_row_tiles(mbuf, tm, YS_TILES)
    o_ref[...] = _rms_scale(x_ref[...] + moe, g_ref[...], NORM_EPS)


def _combine(pos, runs, gidx, x2, g, ys, row0, n_rows):
    _, _, tm = pos.shape
    tile0 = row0 // tm
    smem = lambda rows: pl.BlockSpec((1, rows, tm), lambda i: (tile0 + i, 0, 0), memory_space=pltpu.SMEM)
    return pl.pallas_call(
        _combine_kernel,
        grid=(n_rows // tm,),
        in_specs=[smem(TOP_K),
                  pl.BlockSpec((1, 1, LANES), lambda i: (tile0 + i, 0, 0), memory_space=pltpu.SMEM),
                  smem(TOP_K),
                  pl.BlockSpec((tm, D_MODEL), lambda i: (tile0 + i, 0)),
                  pl.BlockSpec((1, D_MODEL), lambda i: (0, 0)),
                  pl.BlockSpec(memory_space=pl.ANY)],
        out_specs=pl.BlockSpec((tm, D_MODEL), lambda i: (i, 0)),
        out_shape=jax.ShapeDtypeStruct((n_rows, D_MODEL), F32),
        scratch_shapes=[pltpu.VMEM(((TOP_K * tm + N_EXPERTS * PIECE) * YS_TILES, LANES), F32),
                        pltpu.VMEM((tm * YS_TILES, LANES), F32),
                        pltpu.SemaphoreType.DMA(())],
        compiler_params=_cparams(("arbitrary",)),
        name="moe_combine",
    )(pos, runs, gidx[:, 0:TOP_K, :], x2, g, ys)


def _routing_plan(idx, tiles, counts):
    cnt = counts[0, :N_EXPERTS].astype(I32)
    padded = jnp.where(cnt > 0, (cnt + PIECE - 1 + MOE_TM - 1) // MOE_TM * MOE_TM, 0)
    pends = jnp.cumsum(padded)
    pstarts = pends - padded
    n_tiles, _, tm = idx.shape
    n_blocks = n_tiles * tm * TOP_K // MOE_TM + N_EXPERTS + 1
    blk0 = jnp.arange(n_blocks, dtype=I32) * MOE_TM
    bexp = jnp.minimum(jnp.sum(pends[None, :] <= blk0[:, None], axis=1), N_EXPERTS - 1).astype(I32)
    nused = (pends[-1:] // MOE_TM).astype(I32)
    before = tiles[:, 0, :N_EXPERTS].astype(I32)
    inside = tiles[:, 1, :N_EXPERTS].astype(I32)
    first = jnp.cumsum(inside, axis=1) - inside
    pieces = (inside + PIECE - 1) // PIECE
    first_piece = (jnp.cumsum(pieces, axis=1) - pieces) * PIECE

    def run_table(first_pos):
        return jnp.concatenate([pstarts[None, :] + before, pieces, first_pos, jnp.sum(pieces, axis=1, keepdims=True),
                                jnp.zeros((n_tiles, LANES - 3 * N_EXPERTS - 1), I32)], axis=1).reshape(n_tiles, 1, LANES)

    pos = idx[:, 0:TOP_K, :]
    pos_piece = idx[:, 3 * TOP_K:4 * TOP_K, :]
    return pos, run_table(first), pos_piece, run_table(first_piece), cnt, (pends // MOE_TM).astype(I32), bexp, nused, n_blocks * MOE_TM


def kernel(x_prompt, x_sample, mem_prompt, mem_sample, mix_norm_g, w_in, na_rpb, diff_lambda_q1, diff_lambda_k1, diff_lambda_q2, diff_lambda_k2, diff_subln_g, w_branch_na, w_branch_diff, w_out, xa_norm_g, mem_norm_g, xa_w_q, xa_w_kv, xa_w_o, ffn_norm_g, router_group_w, router_group_b, router_expert_w, router_expert_b, w_gate, w_up, w_down, final_norm_g):
    nb_p, nb_s = x_prompt.shape[0], x_sample.shape[0]
    nb = nb_p + nb_s
    t = nb * SEQ
    xp = x_prompt.reshape(nb_p * SEQ, D_MODEL)
    xs_in = x_sample.reshape(nb_s * SEQ, D_MODEL)
    mem = jnp.concatenate([mem_prompt, mem_sample], axis=0).reshape(nb * MEM_LEN, D_MODEL)
    row = lambda v: v.reshape(1, -1).astype(F32)
    bf = lambda w: w.astype(BF16)

    qcol = 3 * NA_WIDTH
    colscale = jnp.ones((1, IN_COLS), F32).at[:, qcol:qcol + DIFF_QK_WIDTH].set(DIFF_QK_DIM ** -0.5 * LOG2E)
    proj = _inproj(xp, xs_in, row(mix_norm_g[0]), bf(w_in[0]), colscale)
    proj3 = proj.reshape(nb, SEQ, IN_COLS)
    a = _na_attention(proj3, _na_bias_blocks(na_rpb[0]))
    b = _diff_attention(proj3, row(diff_lambda_q1[0]), row(diff_lambda_k1[0]),
                        row(diff_lambda_q2[0]), row(diff_lambda_k2[0]), row(diff_subln_g[0]))
    x1 = _merge(a.reshape(t, NA_WIDTH), b.reshape(t, DIFF_V_WIDTH), proj, xp, xs_in,
                bf(w_branch_na[0]), bf(w_branch_diff[0]), bf(w_out[0]))
    kv = _memkv(mem, row(mem_norm_g[0]), bf(xa_w_kv[0]))
    w_router = jnp.zeros((D_MODEL, LANES), F32)
    w_router = w_router.at[:, :N_GROUPS].set(router_group_w[0]).at[:, N_GROUPS:N_GROUPS + N_EXPERTS].set(router_expert_w[0])
    b_router = jnp.zeros((1, LANES), F32)
    b_router = b_router.at[0, :N_GROUPS].set(router_group_b[0]).at[0, N_GROUPS:N_GROUPS + N_EXPERTS].set(router_expert_b[0])
    x2, hp, gates, eid = _xattn(x1, kv, row(xa_norm_g[0]), bf(xa_w_q[0]), bf(xa_w_o[0]),
                                row(ffn_norm_g[0]), bf(w_router), b_router)
    idx, gidx, tiles, counts = _rank(eid, gates)
    pos, runs, pos_c, runs_c, cnt, pend_blocks, bexp, nused, n_slots = _routing_plan(idx, tiles, counts)
    xs = _dispatch(cnt, pend_blocks, nused, pos, runs, hp, n_slots)
    ys = _moe(bexp, nused, xs, w_gate[0], w_up[0], w_down[0])
    fg = row(final_norm_g)
    y_p = _combine(pos_c, runs_c, gidx, x2, fg, ys, 0, nb_p * SEQ).reshape(nb_p, SEQ, D_MODEL)
    y_s = _combine(pos_c, runs_c, gidx, x2, fg, ys, nb_p * SEQ, nb_s * SEQ).reshape(nb_s, SEQ, D_MODEL)
    return (y_p, y_s)
```

```python
import functools
import math

import jax
import jax.numpy as jnp
from jax import lax
from jax.experimental import pallas as pl
from jax.experimental.pallas import tpu as pltpu

F32 = jnp.float32
BF16 = jnp.bfloat16
I32 = jnp.int32
U32 = jnp.uint32

D_MODEL = 2048
SEQ = 2048
GRID_W = 64
ROWS = SEQ // GRID_W
NA_HEADS = 8
NA_HEAD_DIM = 128
NA_WIDTH = NA_HEADS * NA_HEAD_DIM
NA_KH = 8
NA_KW = 16
DIFF_HEADS = 8
DIFF_QK_DIM = 64
DIFF_V_DIM = 128
DIFF_QK_WIDTH = DIFF_HEADS * 2 * DIFF_QK_DIM
DIFF_V_WIDTH = DIFF_HEADS * DIFF_V_DIM
IN_COLS = 3 * NA_WIDTH + 2 * DIFF_QK_WIDTH + DIFF_V_WIDTH + 2 * D_MODEL
MEM_LEN = 256
XA_HEADS = 4
XA_HEAD_DIM = 128
XA_WIDTH = XA_HEADS * XA_HEAD_DIM
N_GROUPS = 4
EXPERTS_PER_GROUP = 8
N_EXPERTS = N_GROUPS * EXPERTS_PER_GROUP
TOP_K = 2
D_EXPERT = 512
NORM_EPS = 1e-6
SUBLN_EPS = 1e-5
NEG_INF = -1e30
LAMBDA_INIT = 0.8 - 0.6 * math.exp(-0.3 * 0)
LOG2E = math.log2(math.e)

LANES = 128
HP_TILES = D_MODEL // 2 // LANES
YS_TILES = D_MODEL // LANES
VMEM_LIMIT = 56 * 1024 * 1024

INPROJ_TM = 1024
INPROJ_TN = 1024
NORM_CHUNK = 128
DIFF_TQ = 512
DIFF_SUB = 256
MERGE_TM = 256
XA_TM = 256
RANK_TM = 512
MOE_TM = 512
ROW_TM = 256


def _cparams(sem):
    return pltpu.CompilerParams(dimension_semantics=sem, vmem_limit_bytes=VMEM_LIMIT)


def _rms_scale(x, g, eps):
    ms = jnp.mean(x * x, axis=-1, keepdims=True)
    return x * lax.rsqrt(ms + eps) * g


def _dot_nt(a, b):
    return lax.dot_general(a, b, (((1,), (1,)), ((), ())), preferred_element_type=F32)


def _two_group_specs(tm, width, n_first):
    first = pl.BlockSpec((tm, width), lambda i, *_: (jnp.minimum(i, n_first - 1), 0))
    second = pl.BlockSpec((tm, width), lambda i, *_: (jnp.maximum(i - n_first, 0), 0))
    return first, second


def _inproj_kernel(n_first, xp_ref, xs_ref, g_ref, w_ref, cs_ref, o_ref, h_ref):
    def normalise(x_ref):
        def chunk(c, carry):
            rows = pl.ds(pl.multiple_of(c * NORM_CHUNK, NORM_CHUNK), NORM_CHUNK)
            h_ref[rows, :] = _rms_scale(x_ref[rows, :], g_ref[...], NORM_EPS).astype(BF16)
            return carry
        lax.fori_loop(0, x_ref.shape[0] // NORM_CHUNK, chunk, 0)

    first_col = pl.program_id(1) == 0
    in_first = pl.program_id(0) < n_first
    pl.when(first_col & in_first)(lambda: normalise(xp_ref))
    pl.when(first_col & jnp.logical_not(in_first))(lambda: normalise(xs_ref))
    acc = jnp.dot(h_ref[...], w_ref[...], preferred_element_type=F32)
    o_ref[...] = (acc * cs_ref[...]).astype(o_ref.dtype)


def _inproj(xp, xs, g, w, colscale):
    d = xp.shape[1]
    t = xp.shape[0] + xs.shape[0]
    n = w.shape[1]
    tm = INPROJ_TM
    n_first = xp.shape[0] // tm
    spec_p, spec_s = _two_group_specs(tm, d, n_first)
    return pl.pallas_call(
        functools.partial(_inproj_kernel, n_first),
        grid=(t // tm, n // INPROJ_TN),
        in_specs=[spec_p, spec_s,
                  pl.BlockSpec((1, d), lambda i, j: (0, 0)),
                  pl.BlockSpec((d, INPROJ_TN), lambda i, j: (0, j)),
                  pl.BlockSpec((1, INPROJ_TN), lambda i, j: (0, j))],
        out_specs=pl.BlockSpec((tm, INPROJ_TN), lambda i, j: (i, j)),
        out_shape=jax.ShapeDtypeStruct((t, n), BF16),
        scratch_shapes=[pltpu.VMEM((tm, d), BF16)],
        compiler_params=_cparams(("parallel", "arbitrary")),
        name="inproj",
    )(xp, xs, g, w, colscale)


NA_QROWS = 4
NA_KROWS = 12
NA_TILES = ROWS // NA_QROWS


def _na_tile_key_row(ti):
    return min(max(NA_QROWS * ti - NA_KH // 2, 0), ROWS - NA_KROWS)


def _na_bias_blocks(rpb):
    cols = jnp.arange(GRID_W)
    cstart = jnp.clip(cols - NA_KW // 2, 0, GRID_W - NA_KW)
    col_ok = (cols[None, :] >= cstart[:, None]) & (cols[None, :] < cstart[:, None] + NA_KW)
    dc = jnp.clip(cols[None, :] - cols[:, None], 1 - NA_KW, NA_KW - 1) + NA_KW - 1
    pick = (dc[None] == jnp.arange(2 * NA_KW - 1)[:, None, None]).astype(F32)
    t = jnp.einsum("hdx,xck->hdck", rpb.astype(F32), pick, precision=lax.Precision.HIGHEST)
    return jnp.where(col_ok[None, None], t, NEG_INF)


def _na_kernel(q_ref, k_ref, v_ref, t_ref, o_ref, bias_ref):
    scale = NA_HEAD_DIM ** -0.5
    tq = NA_QROWS * GRID_W
    tk = NA_KROWS * GRID_W

    @pl.when(pl.program_id(1) == 0)
    def _():
        masked = jnp.full((GRID_W, GRID_W), NEG_INF, F32)
        for variant, ti in enumerate((0, 1, NA_TILES - 1)):
            for jr in range(NA_QROWS):
                r = NA_QROWS * ti + jr
                rs = min(max(r - NA_KH // 2, 0), ROWS - NA_KH)
                blocks = []
                for i in range(NA_KROWS):
                    kr = _na_tile_key_row(ti) + i
                    blocks.append(t_ref[0, kr - r + NA_KH - 1] if rs <= kr < rs + NA_KH else masked)
                bias_ref[variant, jr * GRID_W:(jr + 1) * GRID_W, :] = jnp.concatenate(blocks, axis=1)

    for ti in range(NA_TILES):
        variant = 0 if ti == 0 else (2 if ti == NA_TILES - 1 else 1)
        k0 = _na_tile_key_row(ti) * GRID_W
        qrows = slice(ti * tq, (ti + 1) * tq)
        krows = slice(k0, k0 + tk)
        s = _dot_nt(q_ref[0, qrows, :], k_ref[0, krows, :]) * scale + bias_ref[variant]
        m = jnp.max(s, axis=-1, keepdims=True)
        e = jnp.exp(s - m)
        l = jnp.sum(e, axis=-1, keepdims=True)
        o = jnp.dot(e.astype(BF16), v_ref[0, krows, :], preferred_element_type=F32)
        o_ref[0, qrows, :] = (o / l).astype(o_ref.dtype)


def _na_attention(proj3, blocks):
    b = proj3.shape[0]
    hd = NA_HEAD_DIM
    return pl.pallas_call(
        _na_kernel,
        grid=(NA_HEADS, b),
        in_specs=[pl.BlockSpec((1, SEQ, hd), lambda h, i: (i, 0, h)),
                  pl.BlockSpec((1, SEQ, hd), lambda h, i: (i, 0, NA_HEADS + h)),
                  pl.BlockSpec((1, SEQ, hd), lambda h, i: (i, 0, 2 * NA_HEADS + h)),
                  pl.BlockSpec((1,) + blocks.shape[1:], lambda h, i: (h, 0, 0, 0))],
        out_specs=pl.BlockSpec((1, SEQ, hd), lambda h, i: (i, 0, h)),
        out_shape=jax.ShapeDtypeStruct((b, SEQ, NA_WIDTH), BF16),
        scratch_shapes=[pltpu.VMEM((3, NA_QROWS * GRID_W, NA_KROWS * GRID_W), F32)],
        compiler_params=_cparams(("arbitrary", "arbitrary")),
        name="na_attn",
    )(proj3, proj3, proj3, blocks)


def _alibi_table(tq):
    slopes = jnp.exp2(-8.0 * jnp.arange(1, DIFF_HEADS + 1, dtype=F32) / DIFF_HEADS)
    r = jnp.arange(tq)[:, None]
    x = jnp.arange(2 * SEQ - tq)[None, :]
    dist = jnp.abs(r - x + (SEQ - tq)).astype(F32)
    return (slopes * LOG2E)[:, None, None] * dist[None]


def _diff_kernel(lq1_ref, lk1_ref, lq2_ref, lk2_ref, q_ref, k_ref, v_ref, g_ref, alibi_ref, o_ref):
    qi = pl.program_id(2)
    ts = alibi_ref.shape[1]
    nsub = q_ref.shape[1] // ts
    lam = (jnp.exp(jnp.sum(lq1_ref[...] * lk1_ref[...], axis=-1, keepdims=True))
           - jnp.exp(jnp.sum(lq2_ref[...] * lk2_ref[...], axis=-1, keepdims=True)) + LAMBDA_INIT)
    k = k_ref[0]

    def scores(u):
        q = q_ref[0, u * ts:(u + 1) * ts, :]
        return [_dot_nt(q[:, mi * DIFF_QK_DIM:(mi + 1) * DIFF_QK_DIM], k[:, mi * DIFF_QK_DIM:(mi + 1) * DIFF_QK_DIM])
                for mi in range(2)]

    def expo(s, bias):
        s = s - bias
        m = jnp.max(s, axis=-1, keepdims=True)
        e = jnp.exp2(s - m)
        return e, jnp.sum(e, axis=-1, keepdims=True)

    pending = scores(0)
    for u in range(nsub):
        s0, s1 = pending
        if u + 1 < nsub:
            pending = scores(u + 1)
        off = (SEQ - ts) - (qi * nsub + u) * ts
        bias = alibi_ref[0, :, pl.ds(pl.multiple_of(off, ts), SEQ)]
        e0, l0 = expo(s0, bias)
        e1, l1 = expo(s1, bias)
        a = (e0 - e1 * (lam * l0 / l1)).astype(BF16)
        o = jnp.dot(a, v_ref[0], preferred_element_type=F32) / l0
        o = _rms_scale(o, g_ref[...], SUBLN_EPS) * (1.0 - LAMBDA_INIT)
        o_ref[0, u * ts:(u + 1) * ts, :] = o.astype(o_ref.dtype)


def _diff_attention(proj3, lq1, lk1, lq2, lk2, subln_g):
    b = proj3.shape[0]
    tq = DIFF_TQ
    qoff = 3 * NA_WIDTH // LANES
    koff = qoff + DIFF_QK_WIDTH // LANES
    voff = koff + DIFF_QK_WIDTH // LANES
    vec = lambda n: pl.BlockSpec((1, n), lambda h, i, j: (0, 0))
    return pl.pallas_call(
        _diff_kernel,
        grid=(DIFF_HEADS, b, SEQ // tq),
        in_specs=[vec(DIFF_QK_DIM), vec(DIFF_QK_DIM), vec(DIFF_QK_DIM), vec(DIFF_QK_DIM),
                  pl.BlockSpec((1, tq, LANES), lambda h, i, j: (i, j, qoff + h)),
                  pl.BlockSpec((1, SEQ, LANES), lambda h, i, j: (i, 0, koff + h)),
                  pl.BlockSpec((1, SEQ, LANES), lambda h, i, j: (i, 0, voff + h)),
                  vec(DIFF_V_DIM),
                  pl.BlockSpec((1, DIFF_SUB, 2 * SEQ - DIFF_SUB), lambda h, i, j: (h, 0, 0))],
        out_specs=pl.BlockSpec((1, tq, DIFF_V_DIM), lambda h, i, j: (i, j, h)),
        out_shape=jax.ShapeDtypeStruct((b, SEQ, DIFF_V_WIDTH), BF16),
        compiler_params=_cparams(("arbitrary", "arbitrary", "arbitrary")),
        name="diff_attn",
    )(lq1, lk1, lq2, lk2, proj3, proj3, proj3, subln_g, _alibi_table(DIFF_SUB))


def _merge_kernel(n_first, a_ref, b_ref, gna_ref, gdf_ref, xp_ref, xs_ref, wna_ref, wdf_ref, wout_ref, o_ref):
    pa = jnp.dot(a_ref[...], wna_ref[...], preferred_element_type=F32)
    pb = jnp.dot(b_ref[...], wdf_ref[...], preferred_element_type=F32)
    merged = (jax.nn.sigmoid(gna_ref[...].astype(F32)) * pa
              + jax.nn.sigmoid(gdf_ref[...].astype(F32)) * pb)
    delta = jnp.dot(merged.astype(BF16), wout_ref[...], preferred_element_type=F32)
    in_first = pl.program_id(0) < n_first

    @pl.when(in_first)
    def _():
        o_ref[...] = xp_ref[...] + delta

    @pl.when(jnp.logical_not(in_first))
    def _():
        o_ref[...] = xs_ref[...] + delta


def _merge(a, b, proj, xp, xs, wna, wdf, wout):
    t = xp.shape[0] + xs.shape[0]
    tm = MERGE_TM
    gna_blk = (3 * NA_WIDTH + 2 * DIFF_QK_WIDTH + DIFF_V_WIDTH) // D_MODEL
    const = lambda shape: pl.BlockSpec(shape, lambda i: (0, 0))
    n_first = xp.shape[0] // tm
    spec_p, spec_s = _two_group_specs(tm, D_MODEL, n_first)
    return pl.pallas_call(
        functools.partial(_merge_kernel, n_first),
        grid=(t // tm,),
        in_specs=[pl.BlockSpec((tm, NA_WIDTH), lambda i: (i, 0)),
                  pl.BlockSpec((tm, DIFF_V_WIDTH), lambda i: (i, 0)),
                  pl.BlockSpec((tm, D_MODEL), lambda i: (i, gna_blk)),
                  pl.BlockSpec((tm, D_MODEL), lambda i: (i, gna_blk + 1)),
                  spec_p, spec_s,
                  const((NA_WIDTH, D_MODEL)), const((DIFF_V_WIDTH, D_MODEL)), const((D_MODEL, D_MODEL))],
        out_specs=pl.BlockSpec((tm, D_MODEL), lambda i: (i, 0)),
        out_shape=jax.ShapeDtypeStruct((t, D_MODEL), F32),
        compiler_params=_cparams(("parallel",)),
        name="merge_outproj",
    )(a, b, proj, proj, xp, xs, wna, wdf, wout)


def _memkv_kernel(m_ref, g_ref, w_ref, o_ref):
    h = _rms_scale(m_ref[...], g_ref[...], NORM_EPS).astype(BF16)
    o_ref[...] = jnp.dot(h, w_ref[...], preferred_element_type=F32).astype(o_ref.dtype)


def _memkv(mem, g, w):
    t = mem.shape[0]
    return pl.pallas_call(
        _memkv_kernel,
        grid=(t // MEM_LEN,),
        in_specs=[pl.BlockSpec((MEM_LEN, D_MODEL), lambda i: (i, 0)),
                  pl.BlockSpec((1, D_MODEL), lambda i: (0, 0)),
                  pl.BlockSpec((D_MODEL, 2 * XA_WIDTH), lambda i: (0, 0))],
        out_specs=pl.BlockSpec((MEM_LEN, 2 * XA_WIDTH), lambda i: (i, 0)),
        out_shape=jax.ShapeDtypeStruct((t, 2 * XA_WIDTH), BF16),
        compiler_params=_cparams(("parallel",)),
        name="mem_kv",
    )(mem, g, w)


def _route(logits):
    lane = lax.broadcasted_iota(I32, logits.shape, 1).astype(F32)
    ninf = -jnp.inf
    big = float(LANES)
    gl = jnp.where(lane < N_GROUPS, logits, ninf)
    gmax = jnp.max(gl, axis=-1, keepdims=True)
    g = jnp.min(jnp.where(gl == gmax, lane, big), axis=-1, keepdims=True)
    pg = 1.0 / jnp.sum(jnp.exp(gl - gmax), axis=-1, keepdims=True)
    lo = N_GROUPS + EXPERTS_PER_GROUP * g
    el = jnp.where((lane >= lo) & (lane < lo + EXPERTS_PER_GROUP), logits, ninf)
    v1 = jnp.max(el, axis=-1, keepdims=True)
    i1 = jnp.min(jnp.where(el == v1, lane, big), axis=-1, keepdims=True)
    el2 = jnp.where(lane == i1, ninf, el)
    v2 = jnp.max(el2, axis=-1, keepdims=True)
    i2 = jnp.min(jnp.where(el2 == v2, lane, big), axis=-1, keepdims=True)
    t = jnp.exp(v2 - v1)
    den = 1.0 + t
    gate1 = pg * (1.0 / den)
    gate2 = pg * (t / den)
    gates = jnp.where(lane == 0.0, gate1, jnp.where(lane == 1.0, gate2, 0.0))
    eids = jnp.where(lane == 0.0, i1 - N_GROUPS, jnp.where(lane == 1.0, i2 - N_GROUPS, 0.0)).astype(I32)
    return gates, eids


def _pack_halves(h):
    half = h.shape[1] // 2
    hi = pltpu.bitcast(h[:, :half].astype(BF16).astype(F32), U32)
    lo = pltpu.bitcast(h[:, half:].astype(BF16).astype(F32), U32)
    return hi | (lo >> 16)


def _unpack_halves(w):
    hi = pltpu.bitcast(w & jnp.uint32(0xFFFF0000), F32).astype(BF16)
    lo = pltpu.bitcast(w << 16, F32).astype(BF16)
    return jnp.concatenate([hi, lo], axis=1)


def _store_row_tiles(ref, x):
    n, width = x.shape
    c = width // LANES
    for j in range(c):
        ref[pl.ds(j, n, stride=c), :] = x[:, j * LANES:(j + 1) * LANES]


def _load_row_tiles(ref, n, c):
    return jnp.concatenate([ref[pl.ds(j, n, stride=c), :] for j in range(c)], axis=1)


def _xattn_kernel(x_ref, kv_ref, gx_ref, wq_ref, wo_ref, gf_ref, wr_ref, br_ref,
                  x2_ref, hp_ref, gate_ref, eid_ref):
    x1 = x_ref[...]
    hq = _rms_scale(x1, gx_ref[...], NORM_EPS).astype(BF16)
    q = jnp.dot(hq, wq_ref[...], preferred_element_type=F32).astype(BF16)
    scale = XA_HEAD_DIM ** -0.5
    outs = []
    for h in range(XA_HEADS):
        sl = slice(h * XA_HEAD_DIM, (h + 1) * XA_HEAD_DIM)
        vsl = slice(XA_WIDTH + h * XA_HEAD_DIM, XA_WIDTH + (h + 1) * XA_HEAD_DIM)
        s = _dot_nt(q[:, sl], kv_ref[:, sl]) * scale
        m = jnp.max(s, axis=-1, keepdims=True)
        e = jnp.exp(s - m)
        l = jnp.sum(e, axis=-1, keepdims=True)
        o = jnp.dot(e.astype(BF16), kv_ref[:, vsl], preferred_element_type=F32)
        outs.append((o / l).astype(BF16))
    o = jnp.concatenate(outs, axis=1)
    x2 = x1 + jnp.dot(o, wo_ref[...], preferred_element_type=F32)
    x2_ref[...] = x2
    h3 = _rms_scale(x2, gf_ref[...], NORM_EPS)
    _store_row_tiles(hp_ref, _pack_halves(h3))
    logits = jnp.dot(h3.astype(BF16), wr_ref[...], preferred_element_type=F32) + br_ref[...]
    gates, eids = _route(logits)
    gate_ref[...] = gates
    eid_ref[...] = eids


def _xattn(x1, kv, gx, wq, wo, gf, wr, br):
    t = x1.shape[0]
    tm = XA_TM
    per_batch = SEQ // tm
    const = lambda shape: pl.BlockSpec(shape, lambda i: (0, 0))
    tile = lambda n: pl.BlockSpec((tm, n), lambda i: (i, 0))
    return pl.pallas_call(
        _xattn_kernel,
        grid=(t // tm,),
        in_specs=[tile(D_MODEL),
                  pl.BlockSpec((MEM_LEN, 2 * XA_WIDTH), lambda i: (i // per_batch, 0)),
                  const((1, D_MODEL)), const((D_MODEL, XA_WIDTH)), const((XA_WIDTH, D_MODEL)),
                  const((1, D_MODEL)), const((D_MODEL, LANES)), const((1, LANES))],
        out_specs=[tile(D_MODEL), pl.BlockSpec((tm * HP_TILES, LANES), lambda i: (i, 0)), tile(LANES), tile(LANES)],
        out_shape=[jax.ShapeDtypeStruct((t, D_MODEL), F32),
                   jax.ShapeDtypeStruct((t * HP_TILES, LANES), U32),
                   jax.ShapeDtypeStruct((t, LANES), F32),
                   jax.ShapeDtypeStruct((t, LANES), I32)],
        compiler_params=_cparams(("parallel",)),
        name="xattn_router",
    )(x1, kv, gx, wq, wo, gf, wr, br)


IDX_ROWS = 8
PIECE = 8


def _rank_kernel(eid_ref, gate_ref, idx_ref, gidx_ref, tile_ref, cnt_ref, carry_ref):
    @pl.when(pl.program_id(0) == 0)
    def _():
        carry_ref[...] = jnp.zeros_like(carry_ref)

    tm = eid_ref.shape[0]
    eid = eid_ref[...]
    lane = lax.broadcasted_iota(I32, (tm, LANES), 1)
    e1 = eid[:, 0:1]
    e2 = eid[:, 1:2]
    m1 = lane == e1
    m2 = lane == e2
    onehot = jnp.where(m1, 1.0, jnp.where(m2, 1.0, 0.0))
    r = lax.broadcasted_iota(I32, (tm, tm), 0)
    c = lax.broadcasted_iota(I32, (tm, tm), 1)
    tri = jnp.where(c < r, 1.0, 0.0).astype(BF16)
    local = jnp.dot(tri, onehot.astype(BF16), preferred_element_type=F32)
    carry = carry_ref[0:1, :]
    n_tile = jnp.sum(onehot, axis=0, keepdims=True)

    n_piece = jnp.floor((n_tile + (PIECE - 1)) * (1.0 / PIECE)) * PIECE

    def assignment(m, e):
        r_local = jnp.sum(jnp.where(m, local, 0.0), axis=-1, keepdims=True)
        r_global = r_local + jnp.sum(jnp.where(m, carry, 0.0), axis=-1, keepdims=True)
        first = jnp.sum(jnp.where(lane < e, n_tile, 0.0), axis=-1, keepdims=True)
        first_piece = jnp.sum(jnp.where(lane < e, n_piece, 0.0), axis=-1, keepdims=True)
        return first + r_local, r_global, first_piece + r_local

    p1, g1, q1 = assignment(m1, e1)
    p2, g2, q2 = assignment(m2, e2)
    cols = [p1, p2, g1, g2, e1.astype(F32), e2.astype(F32), q1, q2]
    table = jnp.zeros((tm, LANES), F32)
    for j, col in enumerate(cols):
        table = jnp.where(lane == j, col, table)
    idx_ref[0] = jnp.transpose(table)[0:IDX_ROWS, :].astype(I32)
    gidx_ref[0] = jnp.transpose(gate_ref[...])[0:IDX_ROWS, :]
    tile_ref[0] = jnp.where(lax.broadcasted_iota(I32, (8, LANES), 0) == 0, carry, n_tile)
    carry_ref[...] = carry_ref[...] + n_tile
    cnt_ref[...] = carry_ref[...]


def _rank(eid, gates):
    t = eid.shape[0]
    tm = RANK_TM
    n_tiles = t // tm
    return pl.pallas_call(
        _rank_kernel,
        grid=(n_tiles,),
        in_specs=[pl.BlockSpec((tm, LANES), lambda i: (i, 0)), pl.BlockSpec((tm, LANES), lambda i: (i, 0))],
        out_specs=[pl.BlockSpec((1, IDX_ROWS, tm), lambda i: (i, 0, 0)),
                   pl.BlockSpec((1, IDX_ROWS, tm), lambda i: (i, 0, 0)),
                   pl.BlockSpec((1, 8, LANES), lambda i: (i, 0, 0)),
                   pl.BlockSpec((8, LANES), lambda i: (0, 0))],
        out_shape=[jax.ShapeDtypeStruct((n_tiles, IDX_ROWS, tm), I32),
                   jax.ShapeDtypeStruct((n_tiles, IDX_ROWS, tm), F32),
                   jax.ShapeDtypeStruct((n_tiles, 8, LANES), F32),
                   jax.ShapeDtypeStruct((8, LANES), F32)],
        scratch_shapes=[pltpu.VMEM((8, LANES), F32)],
        compiler_params=_cparams(("arbitrary",)),
        name="expert_rank",
    )(eid, gates)


def _dispatch_kernel(n_tiles, cnt_ref, pend_ref, nused_ref, pos_ref, run_ref, run_prev_ref, hp_ref, xs_ref, sbuf,
                     zero_ref, sem, zsem):
    i = pl.program_id(0)
    slot = lax.rem(i, 2)
    tm = hp_ref.shape[0] // HP_TILES
    zb = zero_ref.shape[0]
    n_blocks = xs_ref.shape[0] // zb

    @pl.when(pl.program_id(0) == 0)
    def _():
        zero_ref[...] = jnp.zeros_like(zero_ref)
        for b in range(2):
            sbuf[b, pl.ds(TOP_K * tm * HP_TILES, PIECE * HP_TILES), :] = jnp.zeros((PIECE * HP_TILES, LANES), U32)

        def zero_copy(block):
            return pltpu.make_async_copy(zero_ref, xs_ref.at[pl.ds(pl.multiple_of(block * zb, zb), zb)], zsem)

        def tail_start(j, carry):
            zero_copy(j).start()
            return carry

        def tail_wait(j, carry):
            zero_copy(j).wait()
            return carry

        for e in range(N_EXPERTS):
            @pl.when(cnt_ref[e] > 0)
            def _():
                zero_copy(pend_ref[e] - 1).start()
        lax.fori_loop(nused_ref[0], n_blocks, tail_start, 0)
        for e in range(N_EXPERTS):
            @pl.when(cnt_ref[e] > 0)
            def _():
                zero_copy(pend_ref[e] - 1).wait()
        lax.fori_loop(nused_ref[0], n_blocks, tail_wait, 0)

    def place(t, carry):
        row = hp_ref[pl.ds(pl.multiple_of(t * HP_TILES, HP_TILES), HP_TILES), :]
        for k in range(TOP_K):
            sbuf[slot, pl.ds(pl.multiple_of(pos_ref[0, k, t] * HP_TILES, HP_TILES), HP_TILES), :] = row
        return carry

    lax.fori_loop(0, tm, place, 0, unroll=8)

    def piece_copy(src_tok, dst_slot, buf):
        rows = PIECE * HP_TILES
        return pltpu.make_async_copy(sbuf.at[buf, pl.ds(pl.multiple_of(src_tok * HP_TILES, HP_TILES), rows)],
                                     xs_ref.at[pl.ds(pl.multiple_of(dst_slot * HP_TILES, HP_TILES), rows)], sem.at[buf])

    def drain(runs, buf):
        def wait_piece(j, carry):
            piece_copy(0, 0, buf).wait()
            return carry
        lax.fori_loop(0, runs[0, 0, 3 * N_EXPERTS], wait_piece, 0)

    @pl.when(i > 0)
    def _():
        drain(run_prev_ref, 1 - slot)

    for e in range(N_EXPERTS):
        slot0 = run_ref[0, 0, e]
        first = run_ref[0, 0, 2 * N_EXPERTS + e]

        def run_piece(j, carry):
            piece_copy(first + PIECE * j, slot0 + PIECE * j, slot).start()
            return carry

        lax.fori_loop(0, run_ref[0, 0, N_EXPERTS + e], run_piece, 0)

    @pl.when(i == n_tiles - 1)
    def _():
        drain(run_ref, slot)


def _dispatch(cnt, pend_blocks, nused, pos, runs, hp, n_slots):
    n_tiles, _, tm = pos.shape
    grid_spec = pltpu.PrefetchScalarGridSpec(
        num_scalar_prefetch=3,
        grid=(n_tiles,),
        in_specs=[pl.BlockSpec((1, TOP_K, tm), lambda i, *_: (i, 0, 0), memory_space=pltpu.SMEM),
                  pl.BlockSpec((1, 1, LANES), lambda i, *_: (i, 0, 0), memory_space=pltpu.SMEM),
                  pl.BlockSpec((1, 1, LANES), lambda i, *_: (jnp.maximum(i - 1, 0), 0, 0), memory_space=pltpu.SMEM),
                  pl.BlockSpec((tm * HP_TILES, LANES), lambda i, *_: (i, 0))],
        out_specs=pl.BlockSpec(memory_space=pl.ANY),
        scratch_shapes=[pltpu.VMEM((2, (TOP_K * tm + PIECE) * HP_TILES, LANES), U32),
                        pltpu.VMEM((MOE_TM * HP_TILES, LANES), U32),
                        pltpu.SemaphoreType.DMA((2,)), pltpu.SemaphoreType.DMA(())],
    )
    return pl.pallas_call(
        functools.partial(_dispatch_kernel, n_tiles),
        grid_spec=grid_spec,
        out_shape=jax.ShapeDtypeStruct((n_slots * HP_TILES, LANES), U32),
        compiler_params=_cparams(("arbitrary",)),
        name="moe_dispatch",
    )(cnt, pend_blocks, nused, pos, runs, runs, hp)


MOE_DOWN_CHUNKS = 4


def _moe_kernel(bexp_ref, nused_ref, xs_ref, wg_ref, wu_ref, wd_ref, o_ref, wgb, wub, wdb):
    i = pl.program_id(0)
    nused = nused_ref[0]
    tm = xs_ref.shape[0] // HP_TILES

    @pl.when((i < nused) & ((i == 0) | (bexp_ref[i] != bexp_ref[jnp.maximum(i - 1, 0)])))
    def _():
        wgb[...] = wg_ref[0].astype(BF16)
        wub[...] = wu_ref[0].astype(BF16)
        wdb[...] = wd_ref[0].astype(BF16)

    @pl.when(i < nused)
    def _():
        x = _unpack_halves(_load_row_tiles(xs_ref, tm, HP_TILES))
        hg = jnp.dot(x, wgb[...], preferred_element_type=F32)
        hu = jnp.dot(x, wub[...], preferred_element_type=F32)
        hdn = ((hg * jax.nn.sigmoid(hg)) * hu).astype(BF16)
        cw = D_MODEL // MOE_DOWN_CHUNKS
        for c in range(MOE_DOWN_CHUNKS):
            y = jnp.dot(hdn, wdb[:, c * cw:(c + 1) * cw], preferred_element_type=F32)
            for j in range(cw // LANES):
                o_ref[pl.ds(c * (cw // LANES) + j, tm, stride=YS_TILES), :] = y[:, j * LANES:(j + 1) * LANES]

    @pl.when(i >= nused)
    def _():
        o_ref[...] = jnp.zeros_like(o_ref)


def _moe(bexp, nused, xs, wg, wu, wd):
    tm = MOE_TM
    n_slots = xs.shape[0] // HP_TILES
    n_blocks = n_slots // tm
    grid_spec = pltpu.PrefetchScalarGridSpec(
        num_scalar_prefetch=2,
        grid=(n_blocks,),
        in_specs=[pl.BlockSpec((tm * HP_TILES, LANES), lambda i, be, nu: (jnp.minimum(i, nu[0] - 1), 0)),
                  pl.BlockSpec((1, D_MODEL, D_EXPERT), lambda i, be, nu: (be[i], 0, 0)),
                  pl.BlockSpec((1, D_MODEL, D_EXPERT), lambda i, be, nu: (be[i], 0, 0)),
                  pl.BlockSpec((1, D_EXPERT, D_MODEL), lambda i, be, nu: (be[i], 0, 0))],
        out_specs=pl.BlockSpec((tm * YS_TILES, LANES), lambda i, be, nu: (i, 0)),
        scratch_shapes=[pltpu.VMEM((D_MODEL, D_EXPERT), BF16), pltpu.VMEM((D_MODEL, D_EXPERT), BF16),
                        pltpu.VMEM((D_EXPERT, D_MODEL), BF16)],
    )
    return pl.pallas_call(
        _moe_kernel,
        grid_spec=grid_spec,
        out_shape=jax.ShapeDtypeStruct((n_slots * YS_TILES, LANES), F32),
        compiler_params=_cparams(("arbitrary",)),
        name="moe_experts",
    )(bexp, nused, xs, wg, wu, wd)


def _combine_kernel(n_tiles, pos_ref, run_ref, run_next_ref, gate_ref, x_ref, g_ref, ys_ref, o_ref, ybuf, mbuf, sem):
    i = pl.program_id(0)
    tm = x_ref.shape[0]
    slot = lax.rem(i, 2)

    def piece_copy(src_slot, dst_pos, buf):
        rows = PIECE * YS_TILES
        return pltpu.make_async_copy(ys_ref.at[pl.ds(pl.multiple_of(src_slot * YS_TILES, YS_TILES), rows)],
                                     ybuf.at[buf, pl.ds(pl.multiple_of(dst_pos * YS_TILES, YS_TILES), rows)],
                                     sem.at[buf])

    def request(runs, buf):
        for e in range(N_EXPERTS):
            slot0 = runs[0, 0, e]
            first = runs[0, 0, 2 * N_EXPERTS + e]

            def run_piece(j, carry):
                piece_copy(slot0 + PIECE * j, first + PIECE * j, buf).start()
                return carry

            lax.fori_loop(0, runs[0, 0, N_EXPERTS + e], run_piece, 0)

    @pl.when(i == 0)
    def _():
        request(run_ref, 0)

    @pl.when(i + 1 < n_tiles)
    def _():
        request(run_next_ref, 1 - slot)

    def drain(j, carry):
        piece_copy(0, 0, slot).wait()
        return carry

    lax.fori_loop(0, run_ref[0, 0, 3 * N_EXPERTS], drain, 0)

    def token(t, carry):
        acc = None
        for k in range(TOP_K):
            rows = pl.ds(pl.multiple_of(pos_ref[0, k, t] * YS_TILES, YS_TILES), YS_TILES)
            term = ybuf[slot, rows, :] * gate_ref[0, k, t]
            acc = term if acc is None else acc + term
        mbuf[pl.ds(pl.multiple_of(t * YS_TILES, YS_TILES), YS_TILES), :] = acc
        return carry

    lax.fori_loop(0, tm, token, 0, unroll=8)
    moe = _load_row_tiles(mbuf, tm, YS_TILES)
    o_ref[...] = _rms_scale(x_ref[...] + moe, g_ref[...], NORM_EPS)


def _combine(pos, runs, gidx, x2, g, ys, row0, n_rows):
    _, _, tm = pos.shape
    tile0 = row0 // tm
    n = n_rows // tm
    smem = lambda rows: pl.BlockSpec((1, rows, tm), lambda i: (tile0 + i, 0, 0), memory_space=pltpu.SMEM)
    run_spec = lambda ahead: pl.BlockSpec((1, 1, LANES), lambda i: (tile0 + jnp.minimum(i + ahead, n - 1), 0, 0),
                                          memory_space=pltpu.SMEM)
    return pl.pallas_call(
        functools.partial(_combine_kernel, n),
        grid=(n,),
        in_specs=[smem(TOP_K), run_spec(0), run_spec(1), smem(TOP_K),
                  pl.BlockSpec((tm, D_MODEL), lambda i: (tile0 + i, 0)),
                  pl.BlockSpec((1, D_MODEL), lambda i: (0, 0)),
                  pl.BlockSpec(memory_space=pl.ANY)],
        out_specs=pl.BlockSpec((tm, D_MODEL), lambda i: (i, 0)),
        out_shape=jax.ShapeDtypeStruct((n_rows, D_MODEL), F32),
        scratch_shapes=[pltpu.VMEM((2, (TOP_K * tm + N_EXPERTS * PIECE) * YS_TILES, LANES), F32),
                        pltpu.VMEM((tm * YS_TILES, LANES), F32),
                        pltpu.SemaphoreType.DMA((2,))],
        compiler_params=_cparams(("arbitrary",)),
        name="moe_combine",
    )(pos, runs, runs, gidx[:, 0:TOP_K, :], x2, g, ys)


def _routing_plan(idx, tiles, counts):
    cnt = counts[0, :N_EXPERTS].astype(I32)
    padded = jnp.where(cnt > 0, (cnt + PIECE - 1 + MOE_TM - 1) // MOE_TM * MOE_TM, 0)
    pends = jnp.cumsum(padded)
    pstarts = pends - padded
    n_tiles, _, tm = idx.shape
    n_blocks = n_tiles * tm * TOP_K // MOE_TM + N_EXPERTS + 1
    blk0 = jnp.arange(n_blocks, dtype=I32) * MOE_TM
    bexp = jnp.minimum(jnp.sum(pends[None, :] <= blk0[:, None], axis=1), N_EXPERTS - 1).astype(I32)
    nused = (pends[-1:] // MOE_TM).astype(I32)
    before = tiles[:, 0, :N_EXPERTS].astype(I32)
    inside = tiles[:, 1, :N_EXPERTS].astype(I32)
    first = jnp.cumsum(inside, axis=1) - inside
    pieces = (inside + PIECE - 1) // PIECE
    first_piece = (jnp.cumsum(pieces, axis=1) - pieces) * PIECE

    def run_table(first_pos):
        return jnp.concatenate([pstarts[None, :] + before, pieces, first_pos, jnp.sum(pieces, axis=1, keepdims=True),
                                jnp.zeros((n_tiles, LANES - 3 * N_EXPERTS - 1), I32)], axis=1).reshape(n_tiles, 1, LANES)

    pos = idx[:, 0:TOP_K, :]
    pos_piece = idx[:, 3 * TOP_K:4 * TOP_K, :]
    return pos, run_table(first), pos_piece, run_table(first_piece), cnt, (pends // MOE_TM).astype(I32), bexp, nused, n_blocks * MOE_TM


def kernel(x_prompt, x_sample, mem_prompt, mem_sample, mix_norm_g, w_in, na_rpb, diff_lambda_q1, diff_lambda_k1, diff_lambda_q2, diff_lambda_k2, diff_subln_g, w_branch_na, w_branch_diff, w_out, xa_norm_g, mem_norm_g, xa_w_q, xa_w_kv, xa_w_o, ffn_norm_g, router_group_w, router_group_b, router_expert_w, router_expert_b, w_gate, w_up, w_down, final_norm_g):
    nb_p, nb_s = x_prompt.shape[0], x_sample.shape[0]
    nb = nb_p + nb_s
    t = nb * SEQ
    xp = x_prompt.reshape(nb_p * SEQ, D_MODEL)
    xs_in = x_sample.reshape(nb_s * SEQ, D_MODEL)
    mem = jnp.concatenate([mem_prompt, mem_sample], axis=0).reshape(nb * MEM_LEN, D_MODEL)
    row = lambda v: v.reshape(1, -1).astype(F32)
    bf = lambda w: w.astype(BF16)

    qcol = 3 * NA_WIDTH
    colscale = jnp.ones((1, IN_COLS), F32).at[:, qcol:qcol + DIFF_QK_WIDTH].set(DIFF_QK_DIM ** -0.5 * LOG2E)
    proj = _inproj(xp, xs_in, row(mix_norm_g[0]), bf(w_in[0]), colscale)
    proj3 = proj.reshape(nb, SEQ, IN_COLS)
    a = _na_attention(proj3, _na_bias_blocks(na_rpb[0]))
    b = _diff_attention(proj3, row(diff_lambda_q1[0]), row(diff_lambda_k1[0]),
                        row(diff_lambda_q2[0]), row(diff_lambda_k2[0]), row(diff_subln_g[0]))
    x1 = _merge(a.reshape(t, NA_WIDTH), b.reshape(t, DIFF_V_WIDTH), proj, xp, xs_in,
                bf(w_branch_na[0]), bf(w_branch_diff[0]), bf(w_out[0]))
    kv = _memkv(mem, row(mem_norm_g[0]), bf(xa_w_kv[0]))
    w_router = jnp.zeros((D_MODEL, LANES), F32)
    w_router = w_router.at[:, :N_GROUPS].set(router_group_w[0]).at[:, N_GROUPS:N_GROUPS + N_EXPERTS].set(router_expert_w[0])
    b_router = jnp.zeros((1, LANES), F32)
    b_router = b_router.at[0, :N_GROUPS].set(router_group_b[0]).at[0, N_GROUPS:N_GROUPS + N_EXPERTS].set(router_expert_b[0])
    x2, hp, gates, eid = _xattn(x1, kv, row(xa_norm_g[0]), bf(xa_w_q[0]), bf(xa_w_o[0]),
                                row(ffn_norm_g[0]), bf(w_router), b_router)
    idx, gidx, tiles, counts = _rank(eid, gates)
    pos, runs, pos_c, runs_c, cnt, pend_blocks, bexp, nused, n_slots = _routing_plan(idx, tiles, counts)
    xs = _dispatch(cnt, pend_blocks, nused, pos, runs, hp, n_slots)
    ys = _moe(bexp, nused, xs, w_gate[0], w_up[0], w_down[0])
    fg = row(final_norm_g)
    y_p = _combine(pos_c, runs_c, gidx, x2, fg, ys, 0, nb_p * SEQ).reshape(nb_p, SEQ, D_MODEL)
    y_s = _combine(pos_c, runs_c, gidx, x2, fg, ys, nb_p * SEQ, nb_s * SEQ).reshape(nb_s, SEQ, D_MODEL)
    return (y_p, y_s)
```

```python
import functools
import math

import jax
import jax.numpy as jnp
from jax import lax
from jax.experimental import pallas as pl
from jax.experimental.pallas import tpu as pltpu

F32 = jnp.float32
BF16 = jnp.bfloat16
I32 = jnp.int32
U32 = jnp.uint32

D_MODEL = 2048
SEQ = 2048
GRID_W = 64
ROWS = SEQ // GRID_W
NA_HEADS = 8
NA_HEAD_DIM = 128
NA_WIDTH = NA_HEADS * NA_HEAD_DIM
NA_KH = 8
NA_KW = 16
DIFF_HEADS = 8
DIFF_QK_DIM = 64
DIFF_V_DIM = 128
DIFF_QK_WIDTH = DIFF_HEADS * 2 * DIFF_QK_DIM
DIFF_V_WIDTH = DIFF_HEADS * DIFF_V_DIM
IN_COLS = 3 * NA_WIDTH + 2 * DIFF_QK_WIDTH + DIFF_V_WIDTH + 2 * D_MODEL
MEM_LEN = 256
XA_HEADS = 4
XA_HEAD_DIM = 128
XA_WIDTH = XA_HEADS * XA_HEAD_DIM
N_GROUPS = 4
EXPERTS_PER_GROUP = 8
N_EXPERTS = N_GROUPS * EXPERTS_PER_GROUP
TOP_K = 2
D_EXPERT = 512
NORM_EPS = 1e-6
SUBLN_EPS = 1e-5
NEG_INF = -1e30
LAMBDA_INIT = 0.8 - 0.6 * math.exp(-0.3 * 0)
LOG2E = math.log2(math.e)

LANES = 128
HP_TILES = D_MODEL // 2 // LANES
YS_TILES = D_MODEL // LANES
VMEM_LIMIT = 56 * 1024 * 1024

INPROJ_TM = 1024
INPROJ_TN = 1024
NORM_CHUNK = 128
DIFF_TQ = 512
DIFF_SUB = 256
MERGE_TM = 256
XA_TM = 512
XA_SUB = 256
RANK_TM = 512
MOE_TM = 512
ROW_TM = 256


def _cparams(sem):
    return pltpu.CompilerParams(dimension_semantics=sem, vmem_limit_bytes=VMEM_LIMIT)


def _rms_scale(x, g, eps):
    ms = jnp.mean(x * x, axis=-1, keepdims=True)
    return x * lax.rsqrt(ms + eps) * g


def _dot_nt(a, b):
    return lax.dot_general(a, b, (((1,), (1,)), ((), ())), preferred_element_type=F32)


def _two_group_specs(tm, width, n_first):
    first = pl.BlockSpec((tm, width), lambda i, *_: (jnp.minimum(i, n_first - 1), 0))
    second = pl.BlockSpec((tm, width), lambda i, *_: (jnp.maximum(i - n_first, 0), 0))
    return first, second


def _inproj_kernel(n_first, xp_ref, xs_ref, g_ref, w_ref, cs_ref, o_ref, h_ref):
    def normalise(x_ref):
        def chunk(c, carry):
            rows = pl.ds(pl.multiple_of(c * NORM_CHUNK, NORM_CHUNK), NORM_CHUNK)
            h_ref[rows, :] = _rms_scale(x_ref[rows, :], g_ref[...], NORM_EPS).astype(BF16)
            return carry
        lax.fori_loop(0, x_ref.shape[0] // NORM_CHUNK, chunk, 0)

    first_col = pl.program_id(1) == 0
    in_first = pl.program_id(0) < n_first
    pl.when(first_col & in_first)(lambda: normalise(xp_ref))
    pl.when(first_col & jnp.logical_not(in_first))(lambda: normalise(xs_ref))
    acc = jnp.dot(h_ref[...], w_ref[...], preferred_element_type=F32)
    o_ref[...] = (acc * cs_ref[...]).astype(o_ref.dtype)


def _inproj(xp, xs, g, w, colscale):
    d = xp.shape[1]
    t = xp.shape[0] + xs.shape[0]
    n = w.shape[1]
    tm = INPROJ_TM
    n_first = xp.shape[0] // tm
    spec_p, spec_s = _two_group_specs(tm, d, n_first)
    return pl.pallas_call(
        functools.partial(_inproj_kernel, n_first),
        grid=(t // tm, n // INPROJ_TN),
        in_specs=[spec_p, spec_s,
                  pl.BlockSpec((1, d), lambda i, j: (0, 0)),
                  pl.BlockSpec((d, INPROJ_TN), lambda i, j: (0, j)),
                  pl.BlockSpec((1, INPROJ_TN), lambda i, j: (0, j))],
        out_specs=pl.BlockSpec((tm, INPROJ_TN), lambda i, j: (i, j)),
        out_shape=jax.ShapeDtypeStruct((t, n), BF16),
        scratch_shapes=[pltpu.VMEM((tm, d), BF16)],
        compiler_params=_cparams(("parallel", "arbitrary")),
        name="inproj",
    )(xp, xs, g, w, colscale)


NA_QROWS = 4
NA_KROWS = 12
NA_TILES = ROWS // NA_QROWS


def _na_tile_key_row(ti):
    return min(max(NA_QROWS * ti - NA_KH // 2, 0), ROWS - NA_KROWS)


def _na_bias_blocks(rpb):
    cols = jnp.arange(GRID_W)
    cstart = jnp.clip(cols - NA_KW // 2, 0, GRID_W - NA_KW)
    col_ok = (cols[None, :] >= cstart[:, None]) & (cols[None, :] < cstart[:, None] + NA_KW)
    dc = jnp.clip(cols[None, :] - cols[:, None], 1 - NA_KW, NA_KW - 1) + NA_KW - 1
    pick = (dc[None] == jnp.arange(2 * NA_KW - 1)[:, None, None]).astype(F32)
    t = jnp.einsum("hdx,xck->hdck", rpb.astype(F32), pick, precision=lax.Precision.HIGHEST)
    return jnp.where(col_ok[None, None], t, NEG_INF)


def _na_kernel(q_ref, k_ref, v_ref, t_ref, o_ref, bias_ref):
    scale = NA_HEAD_DIM ** -0.5
    tq = NA_QROWS * GRID_W
    tk = NA_KROWS * GRID_W

    @pl.when(pl.program_id(1) == 0)
    def _():
        masked = jnp.full((GRID_W, GRID_W), NEG_INF, F32)
        for variant, ti in enumerate((0, 1, NA_TILES - 1)):
            for jr in range(NA_QROWS):
                r = NA_QROWS * ti + jr
                rs = min(max(r - NA_KH // 2, 0), ROWS - NA_KH)
                blocks = []
                for i in range(NA_KROWS):
                    kr = _na_tile_key_row(ti) + i
                    blocks.append(t_ref[0, kr - r + NA_KH - 1] if rs <= kr < rs + NA_KH else masked)
                bias_ref[variant, jr * GRID_W:(jr + 1) * GRID_W, :] = jnp.concatenate(blocks, axis=1)

    def key_rows(ti):
        k0 = _na_tile_key_row(ti) * GRID_W
        return slice(k0, k0 + tk)

    def scores(ti):
        return _dot_nt(q_ref[0, ti * tq:(ti + 1) * tq, :], k_ref[0, key_rows(ti), :])

    pending = scores(0)
    for ti in range(NA_TILES):
        variant = 0 if ti == 0 else (2 if ti == NA_TILES - 1 else 1)
        qrows = slice(ti * tq, (ti + 1) * tq)
        krows = key_rows(ti)
        s = pending * scale + bias_ref[variant]
        if ti + 1 < NA_TILES:
            pending = scores(ti + 1)
        m = jnp.max(s, axis=-1, keepdims=True)
        e = jnp.exp(s - m)
        l = jnp.sum(e, axis=-1, keepdims=True)
        o = jnp.dot(e.astype(BF16), v_ref[0, krows, :], preferred_element_type=F32)
        o_ref[0, qrows, :] = (o / l).astype(o_ref.dtype)


def _na_attention(proj3, blocks):
    b = proj3.shape[0]
    hd = NA_HEAD_DIM
    return pl.pallas_call(
        _na_kernel,
        grid=(NA_HEADS, b),
        in_specs=[pl.BlockSpec((1, SEQ, hd), lambda h, i: (i, 0, h)),
                  pl.BlockSpec((1, SEQ, hd), lambda h, i: (i, 0, NA_HEADS + h)),
                  pl.BlockSpec((1, SEQ, hd), lambda h, i: (i, 0, 2 * NA_HEADS + h)),
                  pl.BlockSpec((1,) + blocks.shape[1:], lambda h, i: (h, 0, 0, 0))],
        out_specs=pl.BlockSpec((1, SEQ, hd), lambda h, i: (i, 0, h)),
        out_shape=jax.ShapeDtypeStruct((b, SEQ, NA_WIDTH), BF16),
        scratch_shapes=[pltpu.VMEM((3, NA_QROWS * GRID_W, NA_KROWS * GRID_W), F32)],
        compiler_params=_cparams(("arbitrary", "arbitrary")),
        name="na_attn",
    )(proj3, proj3, proj3, blocks)


def _alibi_table(tq):
    slopes = jnp.exp2(-8.0 * jnp.arange(1, DIFF_HEADS + 1, dtype=F32) / DIFF_HEADS)
    r = jnp.arange(tq)[:, None]
    x = jnp.arange(2 * SEQ - tq)[None, :]
    dist = jnp.abs(r - x + (SEQ - tq)).astype(F32)
    return (slopes * LOG2E)[:, None, None] * dist[None]


def _diff_kernel(lq1_ref, lk1_ref, lq2_ref, lk2_ref, q_ref, k_ref, v_ref, g_ref, alibi_ref, o_ref):
    qi = pl.program_id(2)
    ts = alibi_ref.shape[1]
    nsub = q_ref.shape[1] // ts
    lam = (jnp.exp(jnp.sum(lq1_ref[...] * lk1_ref[...], axis=-1, keepdims=True))
           - jnp.exp(jnp.sum(lq2_ref[...] * lk2_ref[...], axis=-1, keepdims=True)) + LAMBDA_INIT)
    k = k_ref[0]

    def scores(u):
        q = q_ref[0, u * ts:(u + 1) * ts, :]
        return [_dot_nt(q[:, mi * DIFF_QK_DIM:(mi + 1) * DIFF_QK_DIM], k[:, mi * DIFF_QK_DIM:(mi + 1) * DIFF_QK_DIM])
                for mi in range(2)]

    def expo(s, bias):
        s = s - bias
        m = jnp.max(s, axis=-1, keepdims=True)
        e = jnp.exp2(s - m)
        return e, jnp.sum(e, axis=-1, keepdims=True)

    pending = scores(0)
    for u in range(nsub):
        s0, s1 = pending
        if u + 1 < nsub:
            pending = scores(u + 1)
        off = (SEQ - ts) - (qi * nsub + u) * ts
        bias = alibi_ref[0, :, pl.ds(pl.multiple_of(off, ts), SEQ)]
        e0, l0 = expo(s0, bias)
        e1, l1 = expo(s1, bias)
        a = (e0 - e1 * (lam * l0 / l1)).astype(BF16)
        o = jnp.dot(a, v_ref[0], preferred_element_type=F32) / l0
        o = _rms_scale(o, g_ref[...], SUBLN_EPS) * (1.0 - LAMBDA_INIT)
        o_ref[0, u * ts:(u + 1) * ts, :] = o.astype(o_ref.dtype)


def _diff_attention(proj3, lq1, lk1, lq2, lk2, subln_g):
    b = proj3.shape[0]
    tq = DIFF_TQ
    qoff = 3 * NA_WIDTH // LANES
    koff = qoff + DIFF_QK_WIDTH // LANES
    voff = koff + DIFF_QK_WIDTH // LANES
    vec = lambda n: pl.BlockSpec((1, n), lambda h, i, j: (0, 0))
    return pl.pallas_call(
        _diff_kernel,
        grid=(DIFF_HEADS, b, SEQ // tq),
        in_specs=[vec(DIFF_QK_DIM), vec(DIFF_QK_DIM), vec(DIFF_QK_DIM), vec(DIFF_QK_DIM),
                  pl.BlockSpec((1, tq, LANES), lambda h, i, j: (i, j, qoff + h)),
                  pl.BlockSpec((1, SEQ, LANES), lambda h, i, j: (i, 0, koff + h)),
                  pl.BlockSpec((1, SEQ, LANES), lambda h, i, j: (i, 0, voff + h)),
                  vec(DIFF_V_DIM),
                  pl.BlockSpec((1, DIFF_SUB, 2 * SEQ - DIFF_SUB), lambda h, i, j: (h, 0, 0))],
        out_specs=pl.BlockSpec((1, tq, DIFF_V_DIM), lambda h, i, j: (i, j, h)),
        out_shape=jax.ShapeDtypeStruct((b, SEQ, DIFF_V_WIDTH), BF16),
        compiler_params=_cparams(("arbitrary", "arbitrary", "arbitrary")),
        name="diff_attn",
    )(lq1, lk1, lq2, lk2, proj3, proj3, proj3, subln_g, _alibi_table(DIFF_SUB))


def _merge_kernel(n_first, a_ref, b_ref, gna_ref, gdf_ref, xp_ref, xs_ref, wna_ref, wdf_ref, wout_ref, o_ref):
    pa = jnp.dot(a_ref[...], wna_ref[...], preferred_element_type=F32)
    pb = jnp.dot(b_ref[...], wdf_ref[...], preferred_element_type=F32)
    merged = (jax.nn.sigmoid(gna_ref[...].astype(F32)) * pa
              + jax.nn.sigmoid(gdf_ref[...].astype(F32)) * pb)
    delta = jnp.dot(merged.astype(BF16), wout_ref[...], preferred_element_type=F32)
    in_first = pl.program_id(0) < n_first

    @pl.when(in_first)
    def _():
        o_ref[...] = xp_ref[...] + delta

    @pl.when(jnp.logical_not(in_first))
    def _():
        o_ref[...] = xs_ref[...] + delta


def _merge(a, b, proj, xp, xs, wna, wdf, wout):
    t = xp.shape[0] + xs.shape[0]
    tm = MERGE_TM
    gna_blk = (3 * NA_WIDTH + 2 * DIFF_QK_WIDTH + DIFF_V_WIDTH) // D_MODEL
    const = lambda shape: pl.BlockSpec(shape, lambda i: (0, 0))
    n_first = xp.shape[0] // tm
    spec_p, spec_s = _two_group_specs(tm, D_MODEL, n_first)
    return pl.pallas_call(
        functools.partial(_merge_kernel, n_first),
        grid=(t // tm,),
        in_specs=[pl.BlockSpec((tm, NA_WIDTH), lambda i: (i, 0)),
                  pl.BlockSpec((tm, DIFF_V_WIDTH), lambda i: (i, 0)),
                  pl.BlockSpec((tm, D_MODEL), lambda i: (i, gna_blk)),
                  pl.BlockSpec((tm, D_MODEL), lambda i: (i, gna_blk + 1)),
                  spec_p, spec_s,
                  const((NA_WIDTH, D_MODEL)), const((DIFF_V_WIDTH, D_MODEL)), const((D_MODEL, D_MODEL))],
        out_specs=pl.BlockSpec((tm, D_MODEL), lambda i: (i, 0)),
        out_shape=jax.ShapeDtypeStruct((t, D_MODEL), F32),
        compiler_params=_cparams(("parallel",)),
        name="merge_outproj",
    )(a, b, proj, proj, xp, xs, wna, wdf, wout)


def _memkv_kernel(m_ref, g_ref, w_ref, o_ref):
    h = _rms_scale(m_ref[...], g_ref[...], NORM_EPS).astype(BF16)
    o_ref[...] = jnp.dot(h, w_ref[...], preferred_element_type=F32).astype(o_ref.dtype)


def _memkv(mem, g, w):
    t = mem.shape[0]
    return pl.pallas_call(
        _memkv_kernel,
        grid=(t // MEM_LEN,),
        in_specs=[pl.BlockSpec((MEM_LEN, D_MODEL), lambda i: (i, 0)),
                  pl.BlockSpec((1, D_MODEL), lambda i: (0, 0)),
                  pl.BlockSpec((D_MODEL, 2 * XA_WIDTH), lambda i: (0, 0))],
        out_specs=pl.BlockSpec((MEM_LEN, 2 * XA_WIDTH), lambda i: (i, 0)),
        out_shape=jax.ShapeDtypeStruct((t, 2 * XA_WIDTH), BF16),
        compiler_params=_cparams(("parallel",)),
        name="mem_kv",
    )(mem, g, w)


def _route(logits):
    lane = lax.broadcasted_iota(I32, logits.shape, 1).astype(F32)
    ninf = -jnp.inf
    big = float(LANES)
    gl = jnp.where(lane < N_GROUPS, logits, ninf)
    gmax = jnp.max(gl, axis=-1, keepdims=True)
    g = jnp.min(jnp.where(gl == gmax, lane, big), axis=-1, keepdims=True)
    pg = 1.0 / jnp.sum(jnp.exp(gl - gmax), axis=-1, keepdims=True)
    lo = N_GROUPS + EXPERTS_PER_GROUP * g
    el = jnp.where((lane >= lo) & (lane < lo + EXPERTS_PER_GROUP), logits, ninf)
    v1 = jnp.max(el, axis=-1, keepdims=True)
    i1 = jnp.min(jnp.where(el == v1, lane, big), axis=-1, keepdims=True)
    el2 = jnp.where(lane == i1, ninf, el)
    v2 = jnp.max(el2, axis=-1, keepdims=True)
    i2 = jnp.min(jnp.where(el2 == v2, lane, big), axis=-1, keepdims=True)
    t = jnp.exp(v2 - v1)
    den = 1.0 + t
    gate1 = pg * (1.0 / den)
    gate2 = pg * (t / den)
    gates = jnp.where(lane == 0.0, gate1, jnp.where(lane == 1.0, gate2, 0.0))
    eids = jnp.where(lane == 0.0, i1 - N_GROUPS, jnp.where(lane == 1.0, i2 - N_GROUPS, 0.0)).astype(I32)
    return gates, eids


def _pack_halves(h):
    half = h.shape[1] // 2
    hi = pltpu.bitcast(h[:, :half].astype(BF16).astype(F32), U32)
    lo = pltpu.bitcast(h[:, half:].astype(BF16).astype(F32), U32)
    return hi | (lo >> 16)


def _unpack_halves(w):
    hi = pltpu.bitcast(w & jnp.uint32(0xFFFF0000), F32).astype(BF16)
    lo = pltpu.bitcast(w << 16, F32).astype(BF16)
    return jnp.concatenate([hi, lo], axis=1)


def _store_row_tiles(ref, x):
    n, width = x.shape
    c = width // LANES
    for j in range(c):
        ref[pl.ds(j, n, stride=c), :] = x[:, j * LANES:(j + 1) * LANES]


def _load_row_tiles(ref, n, c):
    return jnp.concatenate([ref[pl.ds(j, n, stride=c), :] for j in range(c)], axis=1)


def _xattn_kernel(x_ref, kv_ref, gx_ref, wq_ref, wo_ref, gf_ref, wr_ref, br_ref,
                  x2_ref, hp_ref, gate_ref, eid_ref):
    scale = XA_HEAD_DIM ** -0.5
    heads = [slice(h * XA_HEAD_DIM, (h + 1) * XA_HEAD_DIM) for h in range(XA_HEADS)]
    nsub = x_ref.shape[0] // XA_SUB

    def rows(u):
        return slice(u * XA_SUB, (u + 1) * XA_SUB)

    def scores(u):
        hq = _rms_scale(x_ref[rows(u), :], gx_ref[...], NORM_EPS).astype(BF16)
        q = jnp.dot(hq, wq_ref[...], preferred_element_type=F32).astype(BF16)
        return [_dot_nt(q[:, sl], kv_ref[:, sl]) for sl in heads]

    def finish(u, sc):
        outs = []
        for h in range(XA_HEADS):
            vsl = slice(XA_WIDTH + h * XA_HEAD_DIM, XA_WIDTH + (h + 1) * XA_HEAD_DIM)
            s = sc[h] * scale
            m = jnp.max(s, axis=-1, keepdims=True)
            e = jnp.exp(s - m)
            l = jnp.sum(e, axis=-1, keepdims=True)
            o = jnp.dot(e.astype(BF16), kv_ref[:, vsl], preferred_element_type=F32)
            outs.append((o / l).astype(BF16))
        o = jnp.concatenate(outs, axis=1)
        x2 = x_ref[rows(u), :] + jnp.dot(o, wo_ref[...], preferred_element_type=F32)
        x2_ref[rows(u), :] = x2
        h3 = _rms_scale(x2, gf_ref[...], NORM_EPS)
        _store_row_tiles(hp_ref.at[pl.ds(u * XA_SUB * HP_TILES, XA_SUB * HP_TILES)], _pack_halves(h3))
        logits = jnp.dot(h3.astype(BF16), wr_ref[...], preferred_element_type=F32) + br_ref[...]
        gates, eids = _route(logits)
        gate_ref[rows(u), :] = gates
        eid_ref[rows(u), :] = eids

    pending = scores(0)
    for u in range(nsub):
        sc = pending
        if u + 1 < nsub:
            pending = scores(u + 1)
        finish(u, sc)


def _xattn(x1, kv, gx, wq, wo, gf, wr, br):
    t = x1.shape[0]
    tm = XA_TM
    per_batch = SEQ // tm
    const = lambda shape: pl.BlockSpec(shape, lambda i: (0, 0))
    tile = lambda n: pl.BlockSpec((tm, n), lambda i: (i, 0))
    return pl.pallas_call(
        _xattn_kernel,
        grid=(t // tm,),
        in_specs=[tile(D_MODEL),
                  pl.BlockSpec((MEM_LEN, 2 * XA_WIDTH), lambda i: (i // per_batch, 0)),
                  const((1, D_MODEL)), const((D_MODEL, XA_WIDTH)), const((XA_WIDTH, D_MODEL)),
                  const((1, D_MODEL)), const((D_MODEL, LANES)), const((1, LANES))],
        out_specs=[tile(D_MODEL), pl.BlockSpec((tm * HP_TILES, LANES), lambda i: (i, 0)), tile(LANES), tile(LANES)],
        out_shape=[jax.ShapeDtypeStruct((t, D_MODEL), F32),
                   jax.ShapeDtypeStruct((t * HP_TILES, LANES), U32),
                   jax.ShapeDtypeStruct((t, LANES), F32),
                   jax.ShapeDtypeStruct((t, LANES), I32)],
        compiler_params=_cparams(("parallel",)),
        name="xattn_router",
    )(x1, kv, gx, wq, wo, gf, wr, br)


IDX_ROWS = 8
PIECE = 8


def _rank_kernel(eid_ref, gate_ref, idx_ref, gidx_ref, tile_ref, cnt_ref, carry_ref):
    @pl.when(pl.program_id(0) == 0)
    def _():
        carry_ref[...] = jnp.zeros_like(carry_ref)

    tm = eid_ref.shape[0]
    eid = eid_ref[...]
    lane = lax.broadcasted_iota(I32, (tm, LANES), 1)
    e1 = eid[:, 0:1]
    e2 = eid[:, 1:2]
    m1 = lane == e1
    m2 = lane == e2
    onehot = jnp.where(m1, 1.0, jnp.where(m2, 1.0, 0.0))
    r = lax.broadcasted_iota(I32, (tm, tm), 0)
    c = lax.broadcasted_iota(I32, (tm, tm), 1)
    tri = jnp.where(c < r, 1.0, 0.0).astype(BF16)
    local = jnp.dot(tri, onehot.astype(BF16), preferred_element_type=F32)
    carry = carry_ref[0:1, :]
    n_tile = jnp.sum(onehot, axis=0, keepdims=True)

    n_piece = jnp.floor((n_tile + (PIECE - 1)) * (1.0 / PIECE)) * PIECE

    def assignment(m, e):
        r_local = jnp.sum(jnp.where(m, local, 0.0), axis=-1, keepdims=True)
        r_global = r_local + jnp.sum(jnp.where(m, carry, 0.0), axis=-1, keepdims=True)
        first = jnp.sum(jnp.where(lane < e, n_tile, 0.0), axis=-1, keepdims=True)
        first_piece = jnp.sum(jnp.where(lane < e, n_piece, 0.0), axis=-1, keepdims=True)
        return first + r_local, r_global, first_piece + r_local

    p1, g1, q1 = assignment(m1, e1)
    p2, g2, q2 = assignment(m2, e2)
    cols = [p1, p2, g1, g2, e1.astype(F32), e2.astype(F32), q1, q2]
    table = jnp.zeros((tm, LANES), F32)
    for j, col in enumerate(cols):
        table = jnp.where(lane == j, col, table)
    idx_ref[0] = jnp.transpose(table)[0:IDX_ROWS, :].astype(I32)
    gidx_ref[0] = jnp.transpose(gate_ref[...])[0:IDX_ROWS, :]
    tile_ref[0] = jnp.where(lax.broadcasted_iota(I32, (8, LANES), 0) == 0, carry, n_tile)
    carry_ref[...] = carry_ref[...] + n_tile
    cnt_ref[...] = carry_ref[...]


def _rank(eid, gates):
    t = eid.shape[0]
    tm = RANK_TM
    n_tiles = t // tm
    return pl.pallas_call(
        _rank_kernel,
        grid=(n_tiles,),
        in_specs=[pl.BlockSpec((tm, LANES), lambda i: (i, 0)), pl.BlockSpec((tm, LANES), lambda i: (i, 0))],
        out_specs=[pl.BlockSpec((1, IDX_ROWS, tm), lambda i: (i, 0, 0)),
                   pl.BlockSpec((1, IDX_ROWS, tm), lambda i: (i, 0, 0)),
                   pl.BlockSpec((1, 8, LANES), lambda i: (i, 0, 0)),
                   pl.BlockSpec((8, LANES), lambda i: (0, 0))],
        out_shape=[jax.ShapeDtypeStruct((n_tiles, IDX_ROWS, tm), I32),
                   jax.ShapeDtypeStruct((n_tiles, IDX_ROWS, tm), F32),
                   jax.ShapeDtypeStruct((n_tiles, 8, LANES), F32),
                   jax.ShapeDtypeStruct((8, LANES), F32)],
        scratch_shapes=[pltpu.VMEM((8, LANES), F32)],
        compiler_params=_cparams(("arbitrary",)),
        name="expert_rank",
    )(eid, gates)


def _dispatch_kernel(n_tiles, cnt_ref, pend_ref, nused_ref, pos_ref, run_ref, run_prev_ref, hp_ref, xs_ref, sbuf,
                     zero_ref, sem, zsem):
    i = pl.program_id(0)
    slot = lax.rem(i, 2)
    tm = hp_ref.shape[0] // HP_TILES
    zb = zero_ref.shape[0]
    n_blocks = xs_ref.shape[0] // zb

    @pl.when(pl.program_id(0) == 0)
    def _():
        zero_ref[...] = jnp.zeros_like(zero_ref)
        for b in range(2):
            sbuf[b, pl.ds(TOP_K * tm * HP_TILES, PIECE * HP_TILES), :] = jnp.zeros((PIECE * HP_TILES, LANES), U32)

        def zero_copy(block):
            return pltpu.make_async_copy(zero_ref, xs_ref.at[pl.ds(pl.multiple_of(block * zb, zb), zb)], zsem)

        def tail_start(j, carry):
            zero_copy(j).start()
            return carry

        def tail_wait(j, carry):
            zero_copy(j).wait()
            return carry

        for e in range(N_EXPERTS):
            @pl.when(cnt_ref[e] > 0)
            def _():
                zero_copy(pend_ref[e] - 1).start()
        lax.fori_loop(nused_ref[0], n_blocks, tail_start, 0)
        for e in range(N_EXPERTS):
            @pl.when(cnt_ref[e] > 0)
            def _():
                zero_copy(pend_ref[e] - 1).wait()
        lax.fori_loop(nused_ref[0], n_blocks, tail_wait, 0)

    def place(t, carry):
        row = hp_ref[pl.ds(pl.multiple_of(t * HP_TILES, HP_TILES), HP_TILES), :]
        for k in range(TOP_K):
            sbuf[slot, pl.ds(pl.multiple_of(pos_ref[0, k, t] * HP_TILES, HP_TILES), HP_TILES), :] = row
        return carry

    lax.fori_loop(0, tm, place, 0, unroll=8)

    def piece_copy(src_tok, dst_slot, buf):
        rows = PIECE * HP_TILES
        return pltpu.make_async_copy(sbuf.at[buf, pl.ds(pl.multiple_of(src_tok * HP_TILES, HP_TILES), rows)],
                                     xs_ref.at[pl.ds(pl.multiple_of(dst_slot * HP_TILES, HP_TILES), rows)], sem.at[buf])

    def drain(runs, buf):
        def wait_piece(j, carry):
            piece_copy(0, 0, buf).wait()
            return carry
        lax.fori_loop(0, runs[0, 0, 3 * N_EXPERTS], wait_piece, 0)

    @pl.when(i > 0)
    def _():
        drain(run_prev_ref, 1 - slot)

    for e in range(N_EXPERTS):
        slot0 = run_ref[0, 0, e]
        first = run_ref[0, 0, 2 * N_EXPERTS + e]

        def run_piece(j, carry):
            piece_copy(first + PIECE * j, slot0 + PIECE * j, slot).start()
            return carry

        lax.fori_loop(0, run_ref[0, 0, N_EXPERTS + e], run_piece, 0)

    @pl.when(i == n_tiles - 1)
    def _():
        drain(run_ref, slot)


def _dispatch(cnt, pend_blocks, nused, pos, runs, hp, n_slots):
    n_tiles, _, tm = pos.shape
    grid_spec = pltpu.PrefetchScalarGridSpec(
        num_scalar_prefetch=3,
        grid=(n_tiles,),
        in_specs=[pl.BlockSpec((1, TOP_K, tm), lambda i, *_: (i, 0, 0), memory_space=pltpu.SMEM),
                  pl.BlockSpec((1, 1, LANES), lambda i, *_: (i, 0, 0), memory_space=pltpu.SMEM),
                  pl.BlockSpec((1, 1, LANES), lambda i, *_: (jnp.maximum(i - 1, 0), 0, 0), memory_space=pltpu.SMEM),
                  pl.BlockSpec((tm * HP_TILES, LANES), lambda i, *_: (i, 0))],
        out_specs=pl.BlockSpec(memory_space=pl.ANY),
        scratch_shapes=[pltpu.VMEM((2, (TOP_K * tm + PIECE) * HP_TILES, LANES), U32),
                        pltpu.VMEM((MOE_TM * HP_TILES, LANES), U32),
                        pltpu.SemaphoreType.DMA((2,)), pltpu.SemaphoreType.DMA(())],
    )
    return pl.pallas_call(
        functools.partial(_dispatch_kernel, n_tiles),
        grid_spec=grid_spec,
        out_shape=jax.ShapeDtypeStruct((n_slots * HP_TILES, LANES), U32),
        compiler_params=_cparams(("arbitrary",)),
        name="moe_dispatch",
    )(cnt, pend_blocks, nused, pos, runs, runs, hp)


MOE_DOWN_CHUNKS = 4


def _moe_kernel(bexp_ref, nused_ref, xs_ref, wg_ref, wu_ref, wd_ref, o_ref, wgb, wub, wdb):
    i = pl.program_id(0)
    nused = nused_ref[0]
    tm = xs_ref.shape[0] // HP_TILES

    @pl.when((i < nused) & ((i == 0) | (bexp_ref[i] != bexp_ref[jnp.maximum(i - 1, 0)])))
    def _():
        wgb[...] = wg_ref[0].astype(BF16)
        wub[...] = wu_ref[0].astype(BF16)
        wdb[...] = wd_ref[0].astype(BF16)

    @pl.when(i < nused)
    def _():
        x = _unpack_halves(_load_row_tiles(xs_ref, tm, HP_TILES))
        hg = jnp.dot(x, wgb[...], preferred_element_type=F32)
        hu = jnp.dot(x, wub[...], preferred_element_type=F32)
        hdn = ((hg * jax.nn.sigmoid(hg)) * hu).astype(BF16)
        cw = D_MODEL // MOE_DOWN_CHUNKS
        for c in range(MOE_DOWN_CHUNKS):
            y = jnp.dot(hdn, wdb[:, c * cw:(c + 1) * cw], preferred_element_type=F32)
            for j in range(cw // LANES):
                o_ref[pl.ds(c * (cw // LANES) + j, tm, stride=YS_TILES), :] = y[:, j * LANES:(j + 1) * LANES]

    @pl.when(i >= nused)
    def _():
        o_ref[...] = jnp.zeros_like(o_ref)


def _moe(bexp, nused, xs, wg, wu, wd):
    tm = MOE_TM
    n_slots = xs.shape[0] // HP_TILES
    n_blocks = n_slots // tm
    grid_spec = pltpu.PrefetchScalarGridSpec(
        num_scalar_prefetch=2,
        grid=(n_blocks,),
        in_specs=[pl.BlockSpec((tm * HP_TILES, LANES), lambda i, be, nu: (jnp.minimum(i, nu[0] - 1), 0)),
                  pl.BlockSpec((1, D_MODEL, D_EXPERT), lambda i, be, nu: (be[i], 0, 0)),
                  pl.BlockSpec((1, D_MODEL, D_EXPERT), lambda i, be, nu: (be[i], 0, 0)),
                  pl.BlockSpec((1, D_EXPERT, D_MODEL), lambda i, be, nu: (be[i], 0, 0))],
        out_specs=pl.BlockSpec((tm * YS_TILES, LANES), lambda i, be, nu: (i, 0)),
        scratch_shapes=[pltpu.VMEM((D_MODEL, D_EXPERT), BF16), pltpu.VMEM((D_MODEL, D_EXPERT), BF16),
                        pltpu.VMEM((D_EXPERT, D_MODEL), BF16)],
    )
    return pl.pallas_call(
        _moe_kernel,
        grid_spec=grid_spec,
        out_shape=jax.ShapeDtypeStruct((n_slots * YS_TILES, LANES), F32),
        compiler_params=_cparams(("arbitrary",)),
        name="moe_experts",
    )(bexp, nused, xs, wg, wu, wd)


def _combine_kernel(n_tiles, pos_ref, run_ref, run_next_ref, gate_ref, x_ref, g_ref, ys_ref, o_ref, ybuf, mbuf, sem):
    i = pl.program_id(0)
    tm = x_ref.shape[0]
    slot = lax.rem(i, 2)

    def piece_copy(src_slot, dst_pos, buf):
        rows = PIECE * YS_TILES
        return pltpu.make_async_copy(ys_ref.at[pl.ds(pl.multiple_of(src_slot * YS_TILES, YS_TILES), rows)],
                                     ybuf.at[buf, pl.ds(pl.multiple_of(dst_pos * YS_TILES, YS_TILES), rows)],
                                     sem.at[buf])

    def request(runs, buf):
        for e in range(N_EXPERTS):
            slot0 = runs[0, 0, e]
            first = runs[0, 0, 2 * N_EXPERTS + e]

            def run_piece(j, carry):
                piece_copy(slot0 + PIECE * j, first + PIECE * j, buf).start()
                return carry

            lax.fori_loop(0, runs[0, 0, N_EXPERTS + e], run_piece, 0)

    @pl.when(i == 0)
    def _():
        request(run_ref, 0)

    @pl.when(i + 1 < n_tiles)
    def _():
        request(run_next_ref, 1 - slot)

    def drain(j, carry):
        piece_copy(0, 0, slot).wait()
        return carry

    lax.fori_loop(0, run_ref[0, 0, 3 * N_EXPERTS], drain, 0)

    def token(t, carry):
        acc = None
        for k in range(TOP_K):
            rows = pl.ds(pl.multiple_of(pos_ref[0, k, t] * YS_TILES, YS_TILES), YS_TILES)
            term = ybuf[slot, rows, :] * gate_ref[0, k, t]
            acc = term if acc is None else acc + term
        mbuf[pl.ds(pl.multiple_of(t * YS_TILES, YS_TILES), YS_TILES), :] = acc
        return carry

    lax.fori_loop(0, tm, token, 0, unroll=8)
    moe = _load_row_tiles(mbuf, tm, YS_TILES)
    o_ref[...] = _rms_scale(x_ref[...] + moe, g_ref[...], NORM_EPS)


def _combine(pos, runs, gidx, x2, g, ys, row0, n_rows):
    _, _, tm = pos.shape
    tile0 = row0 // tm
    n = n_rows // tm
    smem = lambda rows: pl.BlockSpec((1, rows, tm), lambda i: (tile0 + i, 0, 0), memory_space=pltpu.SMEM)
    run_spec = lambda ahead: pl.BlockSpec((1, 1, LANES), lambda i: (tile0 + jnp.minimum(i + ahead, n - 1), 0, 0),
                                          memory_space=pltpu.SMEM)
    return pl.pallas_call(
        functools.partial(_combine_kernel, n),
        grid=(n,),
        in_specs=[smem(TOP_K), run_spec(0), run_spec(1), smem(TOP_K),
                  pl.BlockSpec((tm, D_MODEL), lambda i: (tile0 + i, 0)),
                  pl.BlockSpec((1, D_MODEL), lambda i: (0, 0)),
                  pl.BlockSpec(memory_space=pl.ANY)],
        out_specs=pl.BlockSpec((tm, D_MODEL), lambda i: (i, 0)),
        out_shape=jax.ShapeDtypeStruct((n_rows, D_MODEL), F32),
        scratch_shapes=[pltpu.VMEM((2, (TOP_K * tm + N_EXPERTS * PIECE) * YS_TILES, LANES), F32),
                        pltpu.VMEM((tm * YS_TILES, LANES), F32),
                        pltpu.SemaphoreType.DMA((2,))],
        compiler_params=_cparams(("arbitrary",)),
        name="moe_combine",
    )(pos, runs, runs, gidx[:, 0:TOP_K, :], x2, g, ys)


def _routing_plan(idx, tiles, counts):
    cnt = counts[0, :N_EXPERTS].astype(I32)
    padded = jnp.where(cnt > 0, (cnt + PIECE - 1 + MOE_TM - 1) // MOE_TM * MOE_TM, 0)
    pends = jnp.cumsum(padded)
    pstarts = pends - padded
    n_tiles, _, tm = idx.shape
    n_blocks = n_tiles * tm * TOP_K // MOE_TM + N_EXPERTS + 1
    blk0 = jnp.arange(n_blocks, dtype=I32) * MOE_TM
    bexp = jnp.minimum(jnp.sum(pends[None, :] <= blk0[:, None], axis=1), N_EXPERTS - 1).astype(I32)
    nused = (pends[-1:] // MOE_TM).astype(I32)
    before = tiles[:, 0, :N_EXPERTS].astype(I32)
    inside = tiles[:, 1, :N_EXPERTS].astype(I32)
    first = jnp.cumsum(inside, axis=1) - inside
    pieces = (inside + PIECE - 1) // PIECE
    first_piece = (jnp.cumsum(pieces, axis=1) - pieces) * PIECE

    def run_table(first_pos):
        return jnp.concatenate([pstarts[None, :] + before, pieces, first_pos, jnp.sum(pieces, axis=1, keepdims=True),
                                jnp.zeros((n_tiles, LANES - 3 * N_EXPERTS - 1), I32)], axis=1).reshape(n_tiles, 1, LANES)

    pos = idx[:, 0:TOP_K, :]
    pos_piece = idx[:, 3 * TOP_K:4 * TOP_K, :]
    return pos, run_table(first), pos_piece, run_table(first_piece), cnt, (pends // MOE_TM).astype(I32), bexp, nused, n_blocks * MOE_TM


def kernel(x_prompt, x_sample, mem_prompt, mem_sample, mix_norm_g, w_in, na_rpb, diff_lambda_q1, diff_lambda_k1, diff_lambda_q2, diff_lambda_k2, diff_subln_g, w_branch_na, w_branch_diff, w_out, xa_norm_g, mem_norm_g, xa_w_q, xa_w_kv, xa_w_o, ffn_norm_g, router_group_w, router_group_b, router_expert_w, router_expert_b, w_gate, w_up, w_down, final_norm_g):
    nb_p, nb_s = x_prompt.shape[0], x_sample.shape[0]
    nb = nb_p + nb_s
    t = nb * SEQ
    xp = x_prompt.reshape(nb_p * SEQ, D_MODEL)
    xs_in = x_sample.reshape(nb_s * SEQ, D_MODEL)
    mem = jnp.concatenate([mem_prompt, mem_sample], axis=0).reshape(nb * MEM_LEN, D_MODEL)
    row = lambda v: v.reshape(1, -1).astype(F32)
    bf = lambda w: w.astype(BF16)

    qcol = 3 * NA_WIDTH
    colscale = jnp.ones((1, IN_COLS), F32).at[:, qcol:qcol + DIFF_QK_WIDTH].set(DIFF_QK_DIM ** -0.5 * LOG2E)
    proj = _inproj(xp, xs_in, row(mix_norm_g[0]), bf(w_in[0]), colscale)
    proj3 = proj.reshape(nb, SEQ, IN_COLS)
    a = _na_attention(proj3, _na_bias_blocks(na_rpb[0]))
    b = _diff_attention(proj3, row(diff_lambda_q1[0]), row(diff_lambda_k1[0]),
                        row(diff_lambda_q2[0]), row(diff_lambda_k2[0]), row(diff_subln_g[0]))
    x1 = _merge(a.reshape(t, NA_WIDTH), b.reshape(t, DIFF_V_WIDTH), proj, xp, xs_in,
                bf(w_branch_na[0]), bf(w_branch_diff[0]), bf(w_out[0]))
    kv = _memkv(mem, row(mem_norm_g[0]), bf(xa_w_kv[0]))
    w_router = jnp.zeros((D_MODEL, LANES), F32)
    w_router = w_router.at[:, :N_GROUPS].set(router_group_w[0]).at[:, N_GROUPS:N_GROUPS + N_EXPERTS].set(router_expert_w[0])
    b_router = jnp.zeros((1, LANES), F32)
    b_router = b_router.at[0, :N_GROUPS].set(router_group_b[0]).at[0, N_GROUPS:N_GROUPS + N_EXPERTS].set(router_expert_b[0])
    x2, hp, gates, eid = _xattn(x1, kv, row(xa_norm_g[0]), bf(xa_w_q[0]), bf(xa_w_o[0]),
                                row(ffn_norm_g[0]), bf(w_router), b_router)
    idx, gidx, tiles, counts = _rank(eid, gates)
    pos, runs, pos_c, runs_c, cnt, pend_blocks, bexp, nused, n_slots = _routing_plan(idx, tiles, counts)
    xs = _dispatch(cnt, pend_blocks, nused, pos, runs, hp, n_slots)
    ys = _moe(bexp, nused, xs, w_gate[0], w_up[0], w_down[0])
    fg = row(final_norm_g)
    y_p = _combine(pos_c, runs_c, gidx, x2, fg, ys, 0, nb_p * SEQ).reshape(nb_p, SEQ, D_MODEL)
    y_s = _combine(pos_c, runs_c, gidx, x2, fg, ys, nb_p * SEQ, nb_s * SEQ).reshape(nb_s, SEQ, D_MODEL)
    return (y_p, y_s)
```

```python
import functools
import math

import jax
import jax.numpy as jnp
from jax import lax
from jax.experimental import pallas as pl
from jax.experimental.pallas import tpu as pltpu

F32 = jnp.float32
BF16 = jnp.bfloat16
I32 = jnp.int32
U32 = jnp.uint32

D_MODEL = 2048
SEQ = 2048
GRID_W = 64
ROWS = SEQ // GRID_W
NA_HEADS = 8
NA_HEAD_DIM = 128
NA_WIDTH = NA_HEADS * NA_HEAD_DIM
NA_KH = 8
NA_KW = 16
DIFF_HEADS = 8
DIFF_QK_DIM = 64
DIFF_V_DIM = 128
DIFF_QK_WIDTH = DIFF_HEADS * 2 * DIFF_QK_DIM
DIFF_V_WIDTH = DIFF_HEADS * DIFF_V_DIM
IN_COLS = 3 * NA_WIDTH + 2 * DIFF_QK_WIDTH + DIFF_V_WIDTH + 2 * D_MODEL
MEM_LEN = 256
XA_HEADS = 4
XA_HEAD_DIM = 128
XA_WIDTH = XA_HEADS * XA_HEAD_DIM
N_GROUPS = 4
EXPERTS_PER_GROUP = 8
N_EXPERTS = N_GROUPS * EXPERTS_PER_GROUP
TOP_K = 2
D_EXPERT = 512
NORM_EPS = 1e-6
SUBLN_EPS = 1e-5
NEG_INF = -1e30
LAMBDA_INIT = 0.8 - 0.6 * math.exp(-0.3 * 0)
LOG2E = math.log2(math.e)

LANES = 128
HP_TILES = D_MODEL // 2 // LANES
YS_TILES = D_MODEL // LANES
VMEM_LIMIT = 56 * 1024 * 1024

INPROJ_TM = 1024
INPROJ_TN = 1024
NORM_CHUNK = 128
DIFF_TQ = 1024
DIFF_SUB = 256
MERGE_TM = 256
XA_TM = 512
XA_SUB = 256
RANK_TM = 512
MOE_TM = 512
ROW_TM = 256


def _cparams(sem):
    return pltpu.CompilerParams(dimension_semantics=sem, vmem_limit_bytes=VMEM_LIMIT)


def _rms_scale(x, g, eps):
    ms = jnp.mean(x * x, axis=-1, keepdims=True)
    return x * lax.rsqrt(ms + eps) * g


def _dot_nt(a, b):
    return lax.dot_general(a, b, (((1,), (1,)), ((), ())), preferred_element_type=F32)


def _two_group_specs(tm, width, n_first):
    first = pl.BlockSpec((tm, width), lambda i, *_: (jnp.minimum(i, n_first - 1), 0))
    second = pl.BlockSpec((tm, width), lambda i, *_: (jnp.maximum(i - n_first, 0), 0))
    return first, second


def _inproj_kernel(n_first, xp_ref, xs_ref, g_ref, w_ref, cs_ref, o_ref, h_ref):
    def normalise(x_ref):
        def chunk(c, carry):
            rows = pl.ds(pl.multiple_of(c * NORM_CHUNK, NORM_CHUNK), NORM_CHUNK)
            h_ref[rows, :] = _rms_scale(x_ref[rows, :], g_ref[...], NORM_EPS).astype(BF16)
            return carry
        lax.fori_loop(0, x_ref.shape[0] // NORM_CHUNK, chunk, 0)

    first_col = pl.program_id(1) == 0
    in_first = pl.program_id(0) < n_first
    pl.when(first_col & in_first)(lambda: normalise(xp_ref))
    pl.when(first_col & jnp.logical_not(in_first))(lambda: normalise(xs_ref))
    acc = jnp.dot(h_ref[...], w_ref[...], preferred_element_type=F32)
    o_ref[...] = (acc * cs_ref[...]).astype(o_ref.dtype)


def _inproj(xp, xs, g, w, colscale):
    d = xp.shape[1]
    t = xp.shape[0] + xs.shape[0]
    n = w.shape[1]
    tm = INPROJ_TM
    n_first = xp.shape[0] // tm
    spec_p, spec_s = _two_group_specs(tm, d, n_first)
    return pl.pallas_call(
        functools.partial(_inproj_kernel, n_first),
        grid=(t // tm, n // INPROJ_TN),
        in_specs=[spec_p, spec_s,
                  pl.BlockSpec((1, d), lambda i, j: (0, 0)),
                  pl.BlockSpec((d, INPROJ_TN), lambda i, j: (0, j)),
                  pl.BlockSpec((1, INPROJ_TN), lambda i, j: (0, j))],
        out_specs=pl.BlockSpec((tm, INPROJ_TN), lambda i, j: (i, j)),
        out_shape=jax.ShapeDtypeStruct((t, n), BF16),
        scratch_shapes=[pltpu.VMEM((tm, d), BF16)],
        compiler_params=_cparams(("parallel", "arbitrary")),
        name="inproj",
    )(xp, xs, g, w, colscale)


NA_QROWS = 4
NA_KROWS = 12
NA_TILES = ROWS // NA_QROWS


def _na_tile_key_row(ti):
    return min(max(NA_QROWS * ti - NA_KH // 2, 0), ROWS - NA_KROWS)


def _na_bias_blocks(rpb):
    cols = jnp.arange(GRID_W)
    cstart = jnp.clip(cols - NA_KW // 2, 0, GRID_W - NA_KW)
    col_ok = (cols[None, :] >= cstart[:, None]) & (cols[None, :] < cstart[:, None] + NA_KW)
    dc = jnp.clip(cols[None, :] - cols[:, None], 1 - NA_KW, NA_KW - 1) + NA_KW - 1
    pick = (dc[None] == jnp.arange(2 * NA_KW - 1)[:, None, None]).astype(F32)
    t = jnp.einsum("hdx,xck->hdck", rpb.astype(F32), pick, precision=lax.Precision.HIGHEST)
    return jnp.where(col_ok[None, None], t, NEG_INF)


def _na_kernel(q_ref, k_ref, v_ref, t_ref, o_ref, bias_ref):
    scale = NA_HEAD_DIM ** -0.5
    tq = NA_QROWS * GRID_W
    tk = NA_KROWS * GRID_W

    @pl.when(pl.program_id(1) == 0)
    def _():
        masked = jnp.full((GRID_W, GRID_W), NEG_INF, F32)
        for variant, ti in enumerate((0, 1, NA_TILES - 1)):
            for jr in range(NA_QROWS):
                r = NA_QROWS * ti + jr
                rs = min(max(r - NA_KH // 2, 0), ROWS - NA_KH)
                blocks = []
                for i in range(NA_KROWS):
                    kr = _na_tile_key_row(ti) + i
                    blocks.append(t_ref[0, kr - r + NA_KH - 1] if rs <= kr < rs + NA_KH else masked)
                bias_ref[variant, jr * GRID_W:(jr + 1) * GRID_W, :] = jnp.concatenate(blocks, axis=1) * LOG2E

    def key_rows(ti):
        k0 = _na_tile_key_row(ti) * GRID_W
        return slice(k0, k0 + tk)

    def scores(ti):
        return _dot_nt(q_ref[0, ti * tq:(ti + 1) * tq, :], k_ref[0, key_rows(ti), :])

    pending = scores(0)
    for ti in range(NA_TILES):
        variant = 0 if ti == 0 else (2 if ti == NA_TILES - 1 else 1)
        qrows = slice(ti * tq, (ti + 1) * tq)
        krows = key_rows(ti)
        s = pending * (scale * LOG2E) + bias_ref[variant]
        if ti + 1 < NA_TILES:
            pending = scores(ti + 1)
        m = jnp.max(s, axis=-1, keepdims=True)
        e = jnp.exp2(s - m)
        l = jnp.sum(e, axis=-1, keepdims=True)
        o = jnp.dot(e.astype(BF16), v_ref[0, krows, :], preferred_element_type=F32)
        o_ref[0, qrows, :] = (o / l).astype(o_ref.dtype)


def _na_attention(proj3, blocks):
    b = proj3.shape[0]
    hd = NA_HEAD_DIM
    return pl.pallas_call(
        _na_kernel,
        grid=(NA_HEADS, b),
        in_specs=[pl.BlockSpec((1, SEQ, hd), lambda h, i: (i, 0, h)),
                  pl.BlockSpec((1, SEQ, hd), lambda h, i: (i, 0, NA_HEADS + h)),
                  pl.BlockSpec((1, SEQ, hd), lambda h, i: (i, 0, 2 * NA_HEADS + h)),
                  pl.BlockSpec((1,) + blocks.shape[1:], lambda h, i: (h, 0, 0, 0))],
        out_specs=pl.BlockSpec((1, SEQ, hd), lambda h, i: (i, 0, h)),
        out_shape=jax.ShapeDtypeStruct((b, SEQ, NA_WIDTH), BF16),
        scratch_shapes=[pltpu.VMEM((3, NA_QROWS * GRID_W, NA_KROWS * GRID_W), F32)],
        compiler_params=_cparams(("arbitrary", "arbitrary")),
        name="na_attn",
    )(proj3, proj3, proj3, blocks)


def _alibi_table(tq):
    slopes = jnp.exp2(-8.0 * jnp.arange(1, DIFF_HEADS + 1, dtype=F32) / DIFF_HEADS)
    r = jnp.arange(tq)[:, None]
    x = jnp.arange(2 * SEQ - tq)[None, :]
    dist = jnp.abs(r - x + (SEQ - tq)).astype(F32)
    return (slopes * LOG2E)[:, None, None] * dist[None]


def _diff_kernel(lq1_ref, lk1_ref, lq2_ref, lk2_ref, q_ref, k_ref, v_ref, g_ref, alibi_ref, o_ref):
    qi = pl.program_id(2)
    ts = alibi_ref.shape[1]
    nsub = q_ref.shape[1] // ts
    lam = (jnp.exp(jnp.sum(lq1_ref[...] * lk1_ref[...], axis=-1, keepdims=True))
           - jnp.exp(jnp.sum(lq2_ref[...] * lk2_ref[...], axis=-1, keepdims=True)) + LAMBDA_INIT)
    k = k_ref[0]

    def scores(u):
        q = q_ref[0, u * ts:(u + 1) * ts, :]
        return [_dot_nt(q[:, mi * DIFF_QK_DIM:(mi + 1) * DIFF_QK_DIM], k[:, mi * DIFF_QK_DIM:(mi + 1) * DIFF_QK_DIM])
                for mi in range(2)]

    def expo(s, bias):
        s = s - bias
        m = jnp.max(s, axis=-1, keepdims=True)
        e = jnp.exp2(s - m)
        return e, jnp.sum(e, axis=-1, keepdims=True)

    pending = scores(0)
    for u in range(nsub):
        s0, s1 = pending
        if u + 1 < nsub:
            pending = scores(u + 1)
        off = (SEQ - ts) - (qi * nsub + u) * ts
        bias = alibi_ref[0, :, pl.ds(pl.multiple_of(off, ts), SEQ)]
        e0, l0 = expo(s0, bias)
        e1, l1 = expo(s1, bias)
        a = (e0 - e1 * (lam * l0 / l1)).astype(BF16)
        o = jnp.dot(a, v_ref[0], preferred_element_type=F32) / l0
        o = _rms_scale(o, g_ref[...], SUBLN_EPS) * (1.0 - LAMBDA_INIT)
        o_ref[0, u * ts:(u + 1) * ts, :] = o.astype(o_ref.dtype)


def _diff_attention(proj3, lq1, lk1, lq2, lk2, subln_g):
    b = proj3.shape[0]
    tq = DIFF_TQ
    qoff = 3 * NA_WIDTH // LANES
    koff = qoff + DIFF_QK_WIDTH // LANES
    voff = koff + DIFF_QK_WIDTH // LANES
    vec = lambda n: pl.BlockSpec((1, n), lambda h, i, j: (0, 0))
    return pl.pallas_call(
        _diff_kernel,
        grid=(DIFF_HEADS, b, SEQ // tq),
        in_specs=[vec(DIFF_QK_DIM), vec(DIFF_QK_DIM), vec(DIFF_QK_DIM), vec(DIFF_QK_DIM),
                  pl.BlockSpec((1, tq, LANES), lambda h, i, j: (i, j, qoff + h)),
                  pl.BlockSpec((1, SEQ, LANES), lambda h, i, j: (i, 0, koff + h)),
                  pl.BlockSpec((1, SEQ, LANES), lambda h, i, j: (i, 0, voff + h)),
                  vec(DIFF_V_DIM),
                  pl.BlockSpec((1, DIFF_SUB, 2 * SEQ - DIFF_SUB), lambda h, i, j: (h, 0, 0))],
        out_specs=pl.BlockSpec((1, tq, DIFF_V_DIM), lambda h, i, j: (i, j, h)),
        out_shape=jax.ShapeDtypeStruct((b, SEQ, DIFF_V_WIDTH), BF16),
        compiler_params=_cparams(("arbitrary", "arbitrary", "arbitrary")),
        name="diff_attn",
    )(lq1, lk1, lq2, lk2, proj3, proj3, proj3, subln_g, _alibi_table(DIFF_SUB))


def _merge_kernel(n_first, a_ref, b_ref, gna_ref, gdf_ref, xp_ref, xs_ref, wna_ref, wdf_ref, wout_ref, o_ref):
    pa = jnp.dot(a_ref[...], wna_ref[...], preferred_element_type=F32)
    pb = jnp.dot(b_ref[...], wdf_ref[...], preferred_element_type=F32)
    merged = (jax.nn.sigmoid(gna_ref[...].astype(F32)) * pa
              + jax.nn.sigmoid(gdf_ref[...].astype(F32)) * pb)
    delta = jnp.dot(merged.astype(BF16), wout_ref[...], preferred_element_type=F32)
    in_first = pl.program_id(0) < n_first

    @pl.when(in_first)
    def _():
        o_ref[...] = xp_ref[...] + delta

    @pl.when(jnp.logical_not(in_first))
    def _():
        o_ref[...] = xs_ref[...] + delta


def _merge(a, b, proj, xp, xs, wna, wdf, wout):
    t = xp.shape[0] + xs.shape[0]
    tm = MERGE_TM
    gna_blk = (3 * NA_WIDTH + 2 * DIFF_QK_WIDTH + DIFF_V_WIDTH) // D_MODEL
    const = lambda shape: pl.BlockSpec(shape, lambda i: (0, 0))
    n_first = xp.shape[0] // tm
    spec_p, spec_s = _two_group_specs(tm, D_MODEL, n_first)
    return pl.pallas_call(
        functools.partial(_merge_kernel, n_first),
        grid=(t // tm,),
        in_specs=[pl.BlockSpec((tm, NA_WIDTH), lambda i: (i, 0)),
                  pl.BlockSpec((tm, DIFF_V_WIDTH), lambda i: (i, 0)),
                  pl.BlockSpec((tm, D_MODEL), lambda i: (i, gna_blk)),
                  pl.BlockSpec((tm, D_MODEL), lambda i: (i, gna_blk + 1)),
                  spec_p, spec_s,
                  const((NA_WIDTH, D_MODEL)), const((DIFF_V_WIDTH, D_MODEL)), const((D_MODEL, D_MODEL))],
        out_specs=pl.BlockSpec((tm, D_MODEL), lambda i: (i, 0)),
        out_shape=jax.ShapeDtypeStruct((t, D_MODEL), F32),
        compiler_params=_cparams(("parallel",)),
        name="merge_outproj",
    )(a, b, proj, proj, xp, xs, wna, wdf, wout)


def _memkv_kernel(n_first, mp_ref, ms_ref, g_ref, w_ref, o_ref):
    def project(m_ref):
        h = _rms_scale(m_ref[...], g_ref[...], NORM_EPS).astype(BF16)
        o_ref[...] = jnp.dot(h, w_ref[...], preferred_element_type=F32).astype(o_ref.dtype)

    in_first = pl.program_id(0) < n_first
    pl.when(in_first)(lambda: project(mp_ref))
    pl.when(jnp.logical_not(in_first))(lambda: project(ms_ref))


def _memkv(mem_p, mem_s, g, w):
    t = mem_p.shape[0] + mem_s.shape[0]
    n_first = mem_p.shape[0] // MEM_LEN
    spec_p, spec_s = _two_group_specs(MEM_LEN, D_MODEL, n_first)
    return pl.pallas_call(
        functools.partial(_memkv_kernel, n_first),
        grid=(t // MEM_LEN,),
        in_specs=[spec_p, spec_s,
                  pl.BlockSpec((1, D_MODEL), lambda i: (0, 0)),
                  pl.BlockSpec((D_MODEL, 2 * XA_WIDTH), lambda i: (0, 0))],
        out_specs=pl.BlockSpec((MEM_LEN, 2 * XA_WIDTH), lambda i: (i, 0)),
        out_shape=jax.ShapeDtypeStruct((t, 2 * XA_WIDTH), BF16),
        compiler_params=_cparams(("parallel",)),
        name="mem_kv",
    )(mem_p, mem_s, g, w)


def _route(logits):
    lane = lax.broadcasted_iota(I32, logits.shape, 1).astype(F32)
    ninf = -jnp.inf
    big = float(LANES)
    gl = jnp.where(lane < N_GROUPS, logits, ninf)
    gmax = jnp.max(gl, axis=-1, keepdims=True)
    g = jnp.min(jnp.where(gl == gmax, lane, big), axis=-1, keepdims=True)
    pg = 1.0 / jnp.sum(jnp.exp(gl - gmax), axis=-1, keepdims=True)
    lo = N_GROUPS + EXPERTS_PER_GROUP * g
    el = jnp.where((lane >= lo) & (lane < lo + EXPERTS_PER_GROUP), logits, ninf)
    v1 = jnp.max(el, axis=-1, keepdims=True)
    i1 = jnp.min(jnp.where(el == v1, lane, big), axis=-1, keepdims=True)
    el2 = jnp.where(lane == i1, ninf, el)
    v2 = jnp.max(el2, axis=-1, keepdims=True)
    i2 = jnp.min(jnp.where(el2 == v2, lane, big), axis=-1, keepdims=True)
    t = jnp.exp(v2 - v1)
    den = 1.0 + t
    gate1 = pg * (1.0 / den)
    gate2 = pg * (t / den)
    gates = jnp.where(lane == 0.0, gate1, jnp.where(lane == 1.0, gate2, 0.0))
    eids = jnp.where(lane == 0.0, i1 - N_GROUPS, jnp.where(lane == 1.0, i2 - N_GROUPS, 0.0)).astype(I32)
    return gates, eids


def _pack_halves(h):
    half = h.shape[1] // 2
    hi = pltpu.bitcast(h[:, :half].astype(BF16).astype(F32), U32)
    lo = pltpu.bitcast(h[:, half:].astype(BF16).astype(F32), U32)
    return hi | (lo >> 16)


def _unpack_halves(w):
    hi = pltpu.bitcast(w & jnp.uint32(0xFFFF0000), F32).astype(BF16)
    lo = pltpu.bitcast(w << 16, F32).astype(BF16)
    return jnp.concatenate([hi, lo], axis=1)


def _store_row_tiles(ref, x):
    n, width = x.shape
    c = width // LANES
    for j in range(c):
        ref[pl.ds(j, n, stride=c), :] = x[:, j * LANES:(j + 1) * LANES]


def _load_row_tiles(ref, n, c):
    return jnp.concatenate([ref[pl.ds(j, n, stride=c), :] for j in range(c)], axis=1)


def _xattn_kernel(x_ref, kv_ref, gx_ref, wq_ref, wo_ref, gf_ref, wr_ref, br_ref,
                  x2_ref, hp_ref, gate_ref, eid_ref):
    scale = XA_HEAD_DIM ** -0.5
    heads = [slice(h * XA_HEAD_DIM, (h + 1) * XA_HEAD_DIM) for h in range(XA_HEADS)]
    nsub = x_ref.shape[0] // XA_SUB

    def rows(u):
        return slice(u * XA_SUB, (u + 1) * XA_SUB)

    def scores(u):
        hq = _rms_scale(x_ref[rows(u), :], gx_ref[...], NORM_EPS).astype(BF16)
        q = jnp.dot(hq, wq_ref[...], preferred_element_type=F32).astype(BF16)
        return [_dot_nt(q[:, sl], kv_ref[:, sl]) for sl in heads]

    def finish(u, sc):
        outs = []
        for h in range(XA_HEADS):
            vsl = slice(XA_WIDTH + h * XA_HEAD_DIM, XA_WIDTH + (h + 1) * XA_HEAD_DIM)
            s = sc[h] * (scale * LOG2E)
            m = jnp.max(s, axis=-1, keepdims=True)
            e = jnp.exp2(s - m)
            l = jnp.sum(e, axis=-1, keepdims=True)
            o = jnp.dot(e.astype(BF16), kv_ref[:, vsl], preferred_element_type=F32)
            outs.append((o / l).astype(BF16))
        o = jnp.concatenate(outs, axis=1)
        x2 = x_ref[rows(u), :] + jnp.dot(o, wo_ref[...], preferred_element_type=F32)
        x2_ref[rows(u), :] = x2
        h3 = _rms_scale(x2, gf_ref[...], NORM_EPS)
        _store_row_tiles(hp_ref.at[pl.ds(u * XA_SUB * HP_TILES, XA_SUB * HP_TILES)], _pack_halves(h3))
        logits = jnp.dot(h3.astype(BF16), wr_ref[...], preferred_element_type=F32) + br_ref[...]
        gates, eids = _route(logits)
        gate_ref[rows(u), :] = gates
        eid_ref[rows(u), :] = eids

    pending = scores(0)
    for u in range(nsub):
        sc = pending
        if u + 1 < nsub:
            pending = scores(u + 1)
        finish(u, sc)


def _xattn(x1, kv, gx, wq, wo, gf, wr, br):
    t = x1.shape[0]
    tm = XA_TM
    per_batch = SEQ // tm
    const = lambda shape: pl.BlockSpec(shape, lambda i: (0, 0))
    tile = lambda n: pl.BlockSpec((tm, n), lambda i: (i, 0))
    return pl.pallas_call(
        _xattn_kernel,
        grid=(t // tm,),
        in_specs=[tile(D_MODEL),
                  pl.BlockSpec((MEM_LEN, 2 * XA_WIDTH), lambda i: (i // per_batch, 0)),
                  const((1, D_MODEL)), const((D_MODEL, XA_WIDTH)), const((XA_WIDTH, D_MODEL)),
                  const((1, D_MODEL)), const((D_MODEL, LANES)), const((1, LANES))],
        out_specs=[tile(D_MODEL), pl.BlockSpec((tm * HP_TILES, LANES), lambda i: (i, 0)), tile(LANES), tile(LANES)],
        out_shape=[jax.ShapeDtypeStruct((t, D_MODEL), F32),
                   jax.ShapeDtypeStruct((t * HP_TILES, LANES), U32),
                   jax.ShapeDtypeStruct((t, LANES), F32),
                   jax.ShapeDtypeStruct((t, LANES), I32)],
        compiler_params=_cparams(("parallel",)),
        name="xattn_router",
    )(x1, kv, gx, wq, wo, gf, wr, br)


IDX_ROWS = 8
PIECE = 8


def _rank_kernel(eid_ref, gate_ref, idx_ref, gidx_ref, tile_ref, cnt_ref, carry_ref):
    @pl.when(pl.program_id(0) == 0)
    def _():
        carry_ref[...] = jnp.zeros_like(carry_ref)

    tm = eid_ref.shape[0]
    eid = eid_ref[...]
    lane = lax.broadcasted_iota(I32, (tm, LANES), 1)
    e1 = eid[:, 0:1]
    e2 = eid[:, 1:2]
    m1 = lane == e1
    m2 = lane == e2
    onehot = jnp.where(m1, 1.0, jnp.where(m2, 1.0, 0.0))
    r = lax.broadcasted_iota(I32, (tm, tm), 0)
    c = lax.broadcasted_iota(I32, (tm, tm), 1)
    tri = jnp.where(c < r, 1.0, 0.0).astype(BF16)
    local = jnp.dot(tri, onehot.astype(BF16), preferred_element_type=F32)
    carry = carry_ref[0:1, :]
    n_tile = jnp.sum(onehot, axis=0, keepdims=True)

    n_piece = jnp.floor((n_tile + (PIECE - 1)) * (1.0 / PIECE)) * PIECE

    def assignment(m, e):
        r_local = jnp.sum(jnp.where(m, local, 0.0), axis=-1, keepdims=True)
        r_global = r_local + jnp.sum(jnp.where(m, carry, 0.0), axis=-1, keepdims=True)
        first = jnp.sum(jnp.where(lane < e, n_tile, 0.0), axis=-1, keepdims=True)
        first_piece = jnp.sum(jnp.where(lane < e, n_piece, 0.0), axis=-1, keepdims=True)
        return first + r_local, r_global, first_piece + r_local

    p1, g1, q1 = assignment(m1, e1)
    p2, g2, q2 = assignment(m2, e2)
    cols = [p1, p2, g1, g2, e1.astype(F32), e2.astype(F32), q1, q2]
    table = jnp.zeros((tm, LANES), F32)
    for j, col in enumerate(cols):
        table = jnp.where(lane == j, col, table)
    idx_ref[0] = jnp.transpose(table)[0:IDX_ROWS, :].astype(I32)
    gidx_ref[0] = jnp.transpose(gate_ref[...])[0:IDX_ROWS, :]
    tile_ref[0] = jnp.where(lax.broadcasted_iota(I32, (8, LANES), 0) == 0, carry, n_tile)
    carry_ref[...] = carry_ref[...] + n_tile
    cnt_ref[...] = carry_ref[...]


def _rank(eid, gates):
    t = eid.shape[0]
    tm = RANK_TM
    n_tiles = t // tm
    return pl.pallas_call(
        _rank_kernel,
        grid=(n_tiles,),
        in_specs=[pl.BlockSpec((tm, LANES), lambda i: (i, 0)), pl.BlockSpec((tm, LANES), lambda i: (i, 0))],
        out_specs=[pl.BlockSpec((1, IDX_ROWS, tm), lambda i: (i, 0, 0)),
                   pl.BlockSpec((1, IDX_ROWS, tm), lambda i: (i, 0, 0)),
                   pl.BlockSpec((1, 8, LANES), lambda i: (i, 0, 0)),
                   pl.BlockSpec((8, LANES), lambda i: (0, 0))],
        out_shape=[jax.ShapeDtypeStruct((n_tiles, IDX_ROWS, tm), I32),
                   jax.ShapeDtypeStruct((n_tiles, IDX_ROWS, tm), F32),
                   jax.ShapeDtypeStruct((n_tiles, 8, LANES), F32),
                   jax.ShapeDtypeStruct((8, LANES), F32)],
        scratch_shapes=[pltpu.VMEM((8, LANES), F32)],
        compiler_params=_cparams(("arbitrary",)),
        name="expert_rank",
    )(eid, gates)


def _dispatch_kernel(n_tiles, cnt_ref, pend_ref, nused_ref, pos_ref, run_ref, run_prev_ref, hp_ref, xs_ref, sbuf,
                     zero_ref, sem, zsem):
    i = pl.program_id(0)
    slot = lax.rem(i, 2)
    tm = hp_ref.shape[0] // HP_TILES
    zb = zero_ref.shape[0]
    n_blocks = xs_ref.shape[0] // zb

    @pl.when(pl.program_id(0) == 0)
    def _():
        zero_ref[...] = jnp.zeros_like(zero_ref)
        for b in range(2):
            sbuf[b, pl.ds(TOP_K * tm * HP_TILES, PIECE * HP_TILES), :] = jnp.zeros((PIECE * HP_TILES, LANES), U32)

        def zero_copy(block):
            return pltpu.make_async_copy(zero_ref, xs_ref.at[pl.ds(pl.multiple_of(block * zb, zb), zb)], zsem)

        def tail_start(j, carry):
            zero_copy(j).start()
            return carry

        def tail_wait(j, carry):
            zero_copy(j).wait()
            return carry

        for e in range(N_EXPERTS):
            @pl.when(cnt_ref[e] > 0)
            def _():
                zero_copy(pend_ref[e] - 1).start()
        lax.fori_loop(nused_ref[0], n_blocks, tail_start, 0)
        for e in range(N_EXPERTS):
            @pl.when(cnt_ref[e] > 0)
            def _():
                zero_copy(pend_ref[e] - 1).wait()
        lax.fori_loop(nused_ref[0], n_blocks, tail_wait, 0)

    def place(t, carry):
        row = hp_ref[pl.ds(pl.multiple_of(t * HP_TILES, HP_TILES), HP_TILES), :]
        for k in range(TOP_K):
            sbuf[slot, pl.ds(pl.multiple_of(pos_ref[0, k, t] * HP_TILES, HP_TILES), HP_TILES), :] = row
        return carry

    lax.fori_loop(0, tm, place, 0, unroll=8)

    def piece_copy(src_tok, dst_slot, buf):
        rows = PIECE * HP_TILES
        return pltpu.make_async_copy(sbuf.at[buf, pl.ds(pl.multiple_of(src_tok * HP_TILES, HP_TILES), rows)],
                                     xs_ref.at[pl.ds(pl.multiple_of(dst_slot * HP_TILES, HP_TILES), rows)], sem.at[buf])

    def drain(runs, buf):
        def wait_piece(j, carry):
            piece_copy(0, 0, buf).wait()
            return carry
        lax.fori_loop(0, runs[0, 0, 3 * N_EXPERTS], wait_piece, 0)

    @pl.when(i > 0)
    def _():
        drain(run_prev_ref, 1 - slot)

    for e in range(N_EXPERTS):
        slot0 = run_ref[0, 0, e]
        first = run_ref[0, 0, 2 * N_EXPERTS + e]

        def run_piece(j, carry):
            piece_copy(first + PIECE * j, slot0 + PIECE * j, slot).start()
            return carry

        lax.fori_loop(0, run_ref[0, 0, N_EXPERTS + e], run_piece, 0)

    @pl.when(i == n_tiles - 1)
    def _():
        drain(run_ref, slot)


def _dispatch(cnt, pend_blocks, nused, pos, runs, hp, n_slots):
    n_tiles, _, tm = pos.shape
    grid_spec = pltpu.PrefetchScalarGridSpec(
        num_scalar_prefetch=3,
        grid=(n_tiles,),
        in_specs=[pl.BlockSpec((1, TOP_K, tm), lambda i, *_: (i, 0, 0), memory_space=pltpu.SMEM),
                  pl.BlockSpec((1, 1, LANES), lambda i, *_: (i, 0, 0), memory_space=pltpu.SMEM),
                  pl.BlockSpec((1, 1, LANES), lambda i, *_: (jnp.maximum(i - 1, 0), 0, 0), memory_space=pltpu.SMEM),
                  pl.BlockSpec((tm * HP_TILES, LANES), lambda i, *_: (i, 0))],
        out_specs=pl.BlockSpec(memory_space=pl.ANY),
        scratch_shapes=[pltpu.VMEM((2, (TOP_K * tm + PIECE) * HP_TILES, LANES), U32),
                        pltpu.VMEM((MOE_TM * HP_TILES, LANES), U32),
                        pltpu.SemaphoreType.DMA((2,)), pltpu.SemaphoreType.DMA(())],
    )
    return pl.pallas_call(
        functools.partial(_dispatch_kernel, n_tiles),
        grid_spec=grid_spec,
        out_shape=jax.ShapeDtypeStruct((n_slots * HP_TILES, LANES), U32),
        compiler_params=_cparams(("arbitrary",)),
        name="moe_dispatch",
    )(cnt, pend_blocks, nused, pos, runs, runs, hp)


MOE_DOWN_CHUNKS = 4


def _moe_kernel(bexp_ref, nused_ref, xs_ref, wg_ref, wu_ref, wd_ref, o_ref, wgb, wub, wdb):
    i = pl.program_id(0)
    nused = nused_ref[0]
    tm = xs_ref.shape[0] // HP_TILES

    @pl.when((i < nused) & ((i == 0) | (bexp_ref[i] != bexp_ref[jnp.maximum(i - 1, 0)])))
    def _():
        wgb[...] = wg_ref[0].astype(BF16)
        wub[...] = wu_ref[0].astype(BF16)
        wdb[...] = wd_ref[0].astype(BF16)

    @pl.when(i < nused)
    def _():
        x = _unpack_halves(_load_row_tiles(xs_ref, tm, HP_TILES))
        hg = jnp.dot(x, wgb[...], preferred_element_type=F32)
        hu = jnp.dot(x, wub[...], preferred_element_type=F32)
        hdn = ((hg * jax.nn.sigmoid(hg)) * hu).astype(BF16)
        cw = D_MODEL // MOE_DOWN_CHUNKS
        for c in range(MOE_DOWN_CHUNKS):
            y = jnp.dot(hdn, wdb[:, c * cw:(c + 1) * cw], preferred_element_type=F32)
            for j in range(cw // LANES):
                o_ref[pl.ds(c * (cw // LANES) + j, tm, stride=YS_TILES), :] = y[:, j * LANES:(j + 1) * LANES]

    @pl.when(i >= nused)
    def _():
        o_ref[...] = jnp.zeros_like(o_ref)


def _moe(bexp, nused, xs, wg, wu, wd):
    tm = MOE_TM
    n_slots = xs.shape[0] // HP_TILES
    n_blocks = n_slots // tm
    grid_spec = pltpu.PrefetchScalarGridSpec(
        num_scalar_prefetch=2,
        grid=(n_blocks,),
        in_specs=[pl.BlockSpec((tm * HP_TILES, LANES), lambda i, be, nu: (jnp.minimum(i, nu[0] - 1), 0)),
                  pl.BlockSpec((1, D_MODEL, D_EXPERT), lambda i, be, nu: (be[i], 0, 0)),
                  pl.BlockSpec((1, D_MODEL, D_EXPERT), lambda i, be, nu: (be[i], 0, 0)),
                  pl.BlockSpec((1, D_EXPERT, D_MODEL), lambda i, be, nu: (be[i], 0, 0))],
        out_specs=pl.BlockSpec((tm * YS_TILES, LANES), lambda i, be, nu: (i, 0)),
        scratch_shapes=[pltpu.VMEM((D_MODEL, D_EXPERT), BF16), pltpu.VMEM((D_MODEL, D_EXPERT), BF16),
                        pltpu.VMEM((D_EXPERT, D_MODEL), BF16)],
    )
    return pl.pallas_call(
        _moe_kernel,
        grid_spec=grid_spec,
        out_shape=jax.ShapeDtypeStruct((n_slots * YS_TILES, LANES), F32),
        compiler_params=_cparams(("arbitrary",)),
        name="moe_experts",
    )(bexp, nused, xs, wg, wu, wd)


def _combine_kernel(n_tiles, pos_ref, run_ref, run_next_ref, gate_ref, x_ref, g_ref, ys_ref, o_ref, ybuf, mbuf, sem):
    i = pl.program_id(0)
    tm = x_ref.shape[0]
    slot = lax.rem(i, 2)

    def piece_copy(src_slot, dst_pos, buf):
        rows = PIECE * YS_TILES
        return pltpu.make_async_copy(ys_ref.at[pl.ds(pl.multiple_of(src_slot * YS_TILES, YS_TILES), rows)],
                                     ybuf.at[buf, pl.ds(pl.multiple_of(dst_pos * YS_TILES, YS_TILES), rows)],
                                     sem.at[buf])

    def request(runs, buf):
        for e in range(N_EXPERTS):
            slot0 = runs[0, 0, e]
            first = runs[0, 0, 2 * N_EXPERTS + e]

            def run_piece(j, carry):
                piece_copy(slot0 + PIECE * j, first + PIECE * j, buf).start()
                return carry

            lax.fori_loop(0, runs[0, 0, N_EXPERTS + e], run_piece, 0)

    @pl.when(i == 0)
    def _():
        request(run_ref, 0)

    @pl.when(i + 1 < n_tiles)
    def _():
        request(run_next_ref, 1 - slot)

    def drain(j, carry):
        piece_copy(0, 0, slot).wait()
        return carry

    lax.fori_loop(0, run_ref[0, 0, 3 * N_EXPERTS], drain, 0)

    def token(t, carry):
        acc = None
        for k in range(TOP_K):
            rows = pl.ds(pl.multiple_of(pos_ref[0, k, t] * YS_TILES, YS_TILES), YS_TILES)
            term = ybuf[slot, rows, :] * gate_ref[0, k, t]
            acc = term if acc is None else acc + term
        mbuf[pl.ds(pl.multiple_of(t * YS_TILES, YS_TILES), YS_TILES), :] = acc
        return carry

    lax.fori_loop(0, tm, token, 0, unroll=8)
    moe = _load_row_tiles(mbuf, tm, YS_TILES)
    o_ref[...] = _rms_scale(x_ref[...] + moe, g_ref[...], NORM_EPS)


def _combine(pos, runs, gidx, x2, g, ys, row0, n_rows):
    _, _, tm = pos.shape
    tile0 = row0 // tm
    n = n_rows // tm
    smem = lambda rows: pl.BlockSpec((1, rows, tm), lambda i: (tile0 + i, 0, 0), memory_space=pltpu.SMEM)
    run_spec = lambda ahead: pl.BlockSpec((1, 1, LANES), lambda i: (tile0 + jnp.minimum(i + ahead, n - 1), 0, 0),
                                          memory_space=pltpu.SMEM)
    return pl.pallas_call(
        functools.partial(_combine_kernel, n),
        grid=(n,),
        in_specs=[smem(TOP_K), run_spec(0), run_spec(1), smem(TOP_K),
                  pl.BlockSpec((tm, D_MODEL), lambda i: (tile0 + i, 0)),
                  pl.BlockSpec((1, D_MODEL), lambda i: (0, 0)),
                  pl.BlockSpec(memory_space=pl.ANY)],
        out_specs=pl.BlockSpec((tm, D_MODEL), lambda i: (i, 0)),
        out_shape=jax.ShapeDtypeStruct((n_rows, D_MODEL), F32),
        scratch_shapes=[pltpu.VMEM((2, (TOP_K * tm + N_EXPERTS * PIECE) * YS_TILES, LANES), F32),
                        pltpu.VMEM((tm * YS_TILES, LANES), F32),
                        pltpu.SemaphoreType.DMA((2,))],
        compiler_params=_cparams(("arbitrary",)),
        name="moe_combine",
    )(pos, runs, runs, gidx[:, 0:TOP_K, :], x2, g, ys)


def _routing_plan(idx, tiles, counts):
    cnt = counts[0, :N_EXPERTS].astype(I32)
    padded = jnp.where(cnt > 0, (cnt + PIECE - 1 + MOE_TM - 1) // MOE_TM * MOE_TM, 0)
    pends = jnp.cumsum(padded)
    pstarts = pends - padded
    n_tiles, _, tm = idx.shape
    n_blocks = n_tiles * tm * TOP_K // MOE_TM + N_EXPERTS + 1
    blk0 = jnp.arange(n_blocks, dtype=I32) * MOE_TM
    bexp = jnp.minimum(jnp.sum(pends[None, :] <= blk0[:, None], axis=1), N_EXPERTS - 1).astype(I32)
    nused = (pends[-1:] // MOE_TM).astype(I32)
    before = tiles[:, 0, :N_EXPERTS].astype(I32)
    inside = tiles[:, 1, :N_EXPERTS].astype(I32)
    first = jnp.cumsum(inside, axis=1) - inside
    pieces = (inside + PIECE - 1) // PIECE
    first_piece = (jnp.cumsum(pieces, axis=1) - pieces) * PIECE

    def run_table(first_pos):
        return jnp.concatenate([pstarts[None, :] + before, pieces, first_pos, jnp.sum(pieces, axis=1, keepdims=True),
                                jnp.zeros((n_tiles, LANES - 3 * N_EXPERTS - 1), I32)], axis=1).reshape(n_tiles, 1, LANES)

    pos = idx[:, 0:TOP_K, :]
    pos_piece = idx[:, 3 * TOP_K:4 * TOP_K, :]
    return pos, run_table(first), pos_piece, run_table(first_piece), cnt, (pends // MOE_TM).astype(I32), bexp, nused, n_blocks * MOE_TM


def kernel(x_prompt, x_sample, mem_prompt, mem_sample, mix_norm_g, w_in, na_rpb, diff_lambda_q1, diff_lambda_k1, diff_lambda_q2, diff_lambda_k2, diff_subln_g, w_branch_na, w_branch_diff, w_out, xa_norm_g, mem_norm_g, xa_w_q, xa_w_kv, xa_w_o, ffn_norm_g, router_group_w, router_group_b, router_expert_w, router_expert_b, w_gate, w_up, w_down, final_norm_g):
    nb_p, nb_s = x_prompt.shape[0], x_sample.shape[0]
    nb = nb_p + nb_s
    t = nb * SEQ
    xp = x_prompt.reshape(nb_p * SEQ, D_MODEL)
    xs_in = x_sample.reshape(nb_s * SEQ, D_MODEL)
    mem_p = mem_prompt.reshape(nb_p * MEM_LEN, D_MODEL)
    mem_s = mem_sample.reshape(nb_s * MEM_LEN, D_MODEL)
    row = lambda v: v.reshape(1, -1).astype(F32)
    bf = lambda w: w.astype(BF16)

    qcol = 3 * NA_WIDTH
    colscale = jnp.ones((1, IN_COLS), F32).at[:, qcol:qcol + DIFF_QK_WIDTH].set(DIFF_QK_DIM ** -0.5 * LOG2E)
    proj = _inproj(xp, xs_in, row(mix_norm_g[0]), bf(w_in[0]), colscale)
    proj3 = proj.reshape(nb, SEQ, IN_COLS)
    a = _na_attention(proj3, _na_bias_blocks(na_rpb[0]))
    b = _diff_attention(proj3, row(diff_lambda_q1[0]), row(diff_lambda_k1[0]),
                        row(diff_lambda_q2[0]), row(diff_lambda_k2[0]), row(diff_subln_g[0]))
    x1 = _merge(a.reshape(t, NA_WIDTH), b.reshape(t, DIFF_V_WIDTH), proj, xp, xs_in,
                bf(w_branch_na[0]), bf(w_branch_diff[0]), bf(w_out[0]))
    kv = _memkv(mem_p, mem_s, row(mem_norm_g[0]), bf(xa_w_kv[0]))
    w_router = jnp.zeros((D_MODEL, LANES), F32)
    w_router = w_router.at[:, :N_GROUPS].set(router_group_w[0]).at[:, N_GROUPS:N_GROUPS + N_EXPERTS].set(router_expert_w[0])
    b_router = jnp.zeros((1, LANES), F32)
    b_router = b_router.at[0, :N_GROUPS].set(router_group_b[0]).at[0, N_GROUPS:N_GROUPS + N_EXPERTS].set(router_expert_b[0])
    x2, hp, gates, eid = _xattn(x1, kv, row(xa_norm_g[0]), bf(xa_w_q[0]), bf(xa_w_o[0]),
                                row(ffn_norm_g[0]), bf(w_router), b_router)
    idx, gidx, tiles, counts = _rank(eid, gates)
    pos, runs, pos_c, runs_c, cnt, pend_blocks, bexp, nused, n_slots = _routing_plan(idx, tiles, counts)
    xs = _dispatch(cnt, pend_blocks, nused, pos, runs, hp, n_slots)
    ys = _moe(bexp, nused, xs, w_gate[0], w_up[0], w_down[0])
    fg = row(final_norm_g)
    y_p = _combine(pos_c, runs_c, gidx, x2, fg, ys, 0, nb_p * SEQ).reshape(nb_p, SEQ, D_MODEL)
    y_s = _combine(pos_c, runs_c, gidx, x2, fg, ys, nb_p * SEQ, nb_s * SEQ).reshape(nb_s, SEQ, D_MODEL)
    return (y_p, y_s)
```

```python
import functools
import math

import jax
import jax.numpy as jnp
from jax import lax
from jax.experimental import pallas as pl
from jax.experimental.pallas import tpu as pltpu

F32 = jnp.float32
BF16 = jnp.bfloat16
I32 = jnp.int32
U32 = jnp.uint32

D_MODEL = 2048
SEQ = 2048
GRID_W = 64
ROWS = SEQ // GRID_W
NA_HEADS = 8
NA_HEAD_DIM = 128
NA_WIDTH = NA_HEADS * NA_HEAD_DIM
NA_KH = 8
NA_KW = 16
DIFF_HEADS = 8
DIFF_QK_DIM = 64
DIFF_V_DIM = 128
DIFF_QK_WIDTH = DIFF_HEADS * 2 * DIFF_QK_DIM
DIFF_V_WIDTH = DIFF_HEADS * DIFF_V_DIM
IN_COLS = 3 * NA_WIDTH + 2 * DIFF_QK_WIDTH + DIFF_V_WIDTH + 2 * D_MODEL
MEM_LEN = 256
XA_HEADS = 4
XA_HEAD_DIM = 128
XA_WIDTH = XA_HEADS * XA_HEAD_DIM
N_GROUPS = 4
EXPERTS_PER_GROUP = 8
N_EXPERTS = N_GROUPS * EXPERTS_PER_GROUP
TOP_K = 2
D_EXPERT = 512
NORM_EPS = 1e-6
SUBLN_EPS = 1e-5
NEG_INF = -1e30
LAMBDA_INIT = 0.8 - 0.6 * math.exp(-0.3 * 0)
LOG2E = math.log2(math.e)

LANES = 128
HP_TILES = D_MODEL // 2 // LANES
YS_TILES = D_MODEL // LANES
VMEM_LIMIT = 56 * 1024 * 1024

INPROJ_TM = 1024
INPROJ_TN = 1024
NORM_CHUNK = 128
DIFF_TQ = 2048
DIFF_SUB = 256
MERGE_TM = 256
XA_TM = 512
XA_SUB = 256
RANK_TM = 512
MOE_TM = 512
ROW_TM = 256


def _cparams(sem):
    return pltpu.CompilerParams(dimension_semantics=sem, vmem_limit_bytes=VMEM_LIMIT)


def _rms_scale(x, g, eps):
    ms = jnp.mean(x * x, axis=-1, keepdims=True)
    return x * lax.rsqrt(ms + eps) * g


def _dot_nt(a, b):
    return lax.dot_general(a, b, (((1,), (1,)), ((), ())), preferred_element_type=F32)


def _two_group_specs(tm, width, n_first):
    first = pl.BlockSpec((tm, width), lambda i, *_: (jnp.minimum(i, n_first - 1), 0))
    second = pl.BlockSpec((tm, width), lambda i, *_: (jnp.maximum(i - n_first, 0), 0))
    return first, second


def _inproj_kernel(n_first, xp_ref, xs_ref, g_ref, w_ref, cs_ref, o_ref, h_ref):
    def normalise(x_ref):
        def chunk(c, carry):
            rows = pl.ds(pl.multiple_of(c * NORM_CHUNK, NORM_CHUNK), NORM_CHUNK)
            h_ref[rows, :] = _rms_scale(x_ref[rows, :], g_ref[...], NORM_EPS).astype(BF16)
            return carry
        lax.fori_loop(0, x_ref.shape[0] // NORM_CHUNK, chunk, 0)

    first_col = pl.program_id(1) == 0
    in_first = pl.program_id(0) < n_first
    pl.when(first_col & in_first)(lambda: normalise(xp_ref))
    pl.when(first_col & jnp.logical_not(in_first))(lambda: normalise(xs_ref))
    acc = jnp.dot(h_ref[...], w_ref[...], preferred_element_type=F32)
    o_ref[...] = (acc * cs_ref[...]).astype(o_ref.dtype)


def _inproj(xp, xs, g, w, colscale):
    d = xp.shape[1]
    t = xp.shape[0] + xs.shape[0]
    n = w.shape[1]
    tm = INPROJ_TM
    n_first = xp.shape[0] // tm
    spec_p, spec_s = _two_group_specs(tm, d, n_first)
    return pl.pallas_call(
        functools.partial(_inproj_kernel, n_first),
        grid=(t // tm, n // INPROJ_TN),
        in_specs=[spec_p, spec_s,
                  pl.BlockSpec((1, d), lambda i, j: (0, 0)),
                  pl.BlockSpec((d, INPROJ_TN), lambda i, j: (0, j)),
                  pl.BlockSpec((1, INPROJ_TN), lambda i, j: (0, j))],
        out_specs=pl.BlockSpec((tm, INPROJ_TN), lambda i, j: (i, j)),
        out_shape=jax.ShapeDtypeStruct((t, n), BF16),
        scratch_shapes=[pltpu.VMEM((tm, d), BF16)],
        compiler_params=_cparams(("parallel", "arbitrary")),
        name="inproj",
    )(xp, xs, g, w, colscale)


NA_QROWS = 4
NA_KROWS = 12
NA_TILES = ROWS // NA_QROWS


def _na_tile_key_row(ti):
    return min(max(NA_QROWS * ti - NA_KH // 2, 0), ROWS - NA_KROWS)


def _na_bias_blocks(rpb):
    cols = jnp.arange(GRID_W)
    cstart = jnp.clip(cols - NA_KW // 2, 0, GRID_W - NA_KW)
    col_ok = (cols[None, :] >= cstart[:, None]) & (cols[None, :] < cstart[:, None] + NA_KW)
    dc = jnp.clip(cols[None, :] - cols[:, None], 1 - NA_KW, NA_KW - 1) + NA_KW - 1
    pick = (dc[None] == jnp.arange(2 * NA_KW - 1)[:, None, None]).astype(F32)
    t = jnp.einsum("hdx,xck->hdck", rpb.astype(F32), pick, precision=lax.Precision.HIGHEST)
    return jnp.where(col_ok[None, None], t, NEG_INF)


def _na_kernel(q_ref, k_ref, v_ref, t_ref, o_ref, bias_ref):
    scale = NA_HEAD_DIM ** -0.5
    tq = NA_QROWS * GRID_W
    tk = NA_KROWS * GRID_W

    @pl.when(pl.program_id(1) == 0)
    def _():
        masked = jnp.full((GRID_W, GRID_W), NEG_INF, F32)
        for variant, ti in enumerate((0, 1, NA_TILES - 1)):
            for jr in range(NA_QROWS):
                r = NA_QROWS * ti + jr
                rs = min(max(r - NA_KH // 2, 0), ROWS - NA_KH)
                blocks = []
                for i in range(NA_KROWS):
                    kr = _na_tile_key_row(ti) + i
                    blocks.append(t_ref[0, kr - r + NA_KH - 1] if rs <= kr < rs + NA_KH else masked)
                bias_ref[variant, jr * GRID_W:(jr + 1) * GRID_W, :] = jnp.concatenate(blocks, axis=1) * LOG2E

    def key_rows(ti):
        k0 = _na_tile_key_row(ti) * GRID_W
        return slice(k0, k0 + tk)

    def scores(ti):
        return _dot_nt(q_ref[0, ti * tq:(ti + 1) * tq, :], k_ref[0, key_rows(ti), :])

    pending = scores(0)
    for ti in range(NA_TILES):
        variant = 0 if ti == 0 else (2 if ti == NA_TILES - 1 else 1)
        qrows = slice(ti * tq, (ti + 1) * tq)
        krows = key_rows(ti)
        s = pending * (scale * LOG2E) + bias_ref[variant]
        if ti + 1 < NA_TILES:
            pending = scores(ti + 1)
        m = jnp.max(s, axis=-1, keepdims=True)
        e = jnp.exp2(s - m)
        l = jnp.sum(e, axis=-1, keepdims=True)
        o = jnp.dot(e.astype(BF16), v_ref[0, krows, :], preferred_element_type=F32)
        o_ref[0, qrows, :] = (o / l).astype(o_ref.dtype)


def _na_attention(proj3, blocks):
    b = proj3.shape[0]
    hd = NA_HEAD_DIM
    return pl.pallas_call(
        _na_kernel,
        grid=(NA_HEADS, b),
        in_specs=[pl.BlockSpec((1, SEQ, hd), lambda h, i: (i, 0, h)),
                  pl.BlockSpec((1, SEQ, hd), lambda h, i: (i, 0, NA_HEADS + h)),
                  pl.BlockSpec((1, SEQ, hd), lambda h, i: (i, 0, 2 * NA_HEADS + h)),
                  pl.BlockSpec((1,) + blocks.shape[1:], lambda h, i: (h, 0, 0, 0))],
        out_specs=pl.BlockSpec((1, SEQ, hd), lambda h, i: (i, 0, h)),
        out_shape=jax.ShapeDtypeStruct((b, SEQ, NA_WIDTH), BF16),
        scratch_shapes=[pltpu.VMEM((3, NA_QROWS * GRID_W, NA_KROWS * GRID_W), F32)],
        compiler_params=_cparams(("arbitrary", "arbitrary")),
        name="na_attn",
    )(proj3, proj3, proj3, blocks)


def _alibi_table(tq):
    slopes = jnp.exp2(-8.0 * jnp.arange(1, DIFF_HEADS + 1, dtype=F32) / DIFF_HEADS)
    r = jnp.arange(tq)[:, None]
    x = jnp.arange(2 * SEQ - tq)[None, :]
    dist = jnp.abs(r - x + (SEQ - tq)).astype(F32)
    return (slopes * LOG2E)[:, None, None] * dist[None]


def _diff_kernel(lq1_ref, lk1_ref, lq2_ref, lk2_ref, q_ref, k_ref, v_ref, g_ref, alibi_ref, o_ref):
    qi = pl.program_id(2)
    ts = alibi_ref.shape[1]
    nsub = q_ref.shape[1] // ts
    lam = (jnp.exp(jnp.sum(lq1_ref[...] * lk1_ref[...], axis=-1, keepdims=True))
           - jnp.exp(jnp.sum(lq2_ref[...] * lk2_ref[...], axis=-1, keepdims=True)) + LAMBDA_INIT)
    k = k_ref[0]

    def scores(u):
        q = q_ref[0, u * ts:(u + 1) * ts, :]
        return [_dot_nt(q[:, mi * DIFF_QK_DIM:(mi + 1) * DIFF_QK_DIM], k[:, mi * DIFF_QK_DIM:(mi + 1) * DIFF_QK_DIM])
                for mi in range(2)]

    def expo(s, bias):
        s = s - bias
        m = jnp.max(s, axis=-1, keepdims=True)
        e = jnp.exp2(s - m)
        return e, jnp.sum(e, axis=-1, keepdims=True)

    pending = scores(0)
    for u in range(nsub):
        s0, s1 = pending
        if u + 1 < nsub:
            pending = scores(u + 1)
        off = (SEQ - ts) - (qi * nsub + u) * ts
        bias = alibi_ref[0, :, pl.ds(pl.multiple_of(off, ts), SEQ)]
        e0, l0 = expo(s0, bias)
        e1, l1 = expo(s1, bias)
        a = (e0 - e1 * (lam * l0 / l1)).astype(BF16)
        o = jnp.dot(a, v_ref[0], preferred_element_type=F32) / l0
        o = _rms_scale(o, g_ref[...], SUBLN_EPS) * (1.0 - LAMBDA_INIT)
        o_ref[0, u * ts:(u + 1) * ts, :] = o.astype(o_ref.dtype)


def _diff_attention(proj3, lq1, lk1, lq2, lk2, subln_g):
    b = proj3.shape[0]
    tq = DIFF_TQ
    qoff = 3 * NA_WIDTH // LANES
    koff = qoff + DIFF_QK_WIDTH // LANES
    voff = koff + DIFF_QK_WIDTH // LANES
    vec = lambda n: pl.BlockSpec((1, n), lambda h, i, j: (0, 0))
    return pl.pallas_call(
        _diff_kernel,
        grid=(DIFF_HEADS, b, SEQ // tq),
        in_specs=[vec(DIFF_QK_DIM), vec(DIFF_QK_DIM), vec(DIFF_QK_DIM), vec(DIFF_QK_DIM),
                  pl.BlockSpec((1, tq, LANES), lambda h, i, j: (i, j, qoff + h)),
                  pl.BlockSpec((1, SEQ, LANES), lambda h, i, j: (i, 0, koff + h)),
                  pl.BlockSpec((1, SEQ, LANES), lambda h, i, j: (i, 0, voff + h)),
                  vec(DIFF_V_DIM),
                  pl.BlockSpec((1, DIFF_SUB, 2 * SEQ - DIFF_SUB), lambda h, i, j: (h, 0, 0))],
        out_specs=pl.BlockSpec((1, tq, DIFF_V_DIM), lambda h, i, j: (i, j, h)),
        out_shape=jax.ShapeDtypeStruct((b, SEQ, DIFF_V_WIDTH), BF16),
        compiler_params=_cparams(("arbitrary", "arbitrary", "arbitrary")),
        name="diff_attn",
    )(lq1, lk1, lq2, lk2, proj3, proj3, proj3, subln_g, _alibi_table(DIFF_SUB))


def _merge_kernel(n_first, a_ref, b_ref, gna_ref, gdf_ref, xp_ref, xs_ref, wna_ref, wdf_ref, wout_ref, o_ref):
    pa = jnp.dot(a_ref[...], wna_ref[...], preferred_element_type=F32)
    pb = jnp.dot(b_ref[...], wdf_ref[...], preferred_element_type=F32)
    merged = (jax.nn.sigmoid(gna_ref[...].astype(F32)) * pa
              + jax.nn.sigmoid(gdf_ref[...].astype(F32)) * pb)
    delta = jnp.dot(merged.astype(BF16), wout_ref[...], preferred_element_type=F32)
    in_first = pl.program_id(0) < n_first

    @pl.when(in_first)
    def _():
        o_ref[...] = xp_ref[...] + delta

    @pl.when(jnp.logical_not(in_first))
    def _():
        o_ref[...] = xs_ref[...] + delta


def _merge(a, b, proj, xp, xs, wna, wdf, wout):
    t = xp.shape[0] + xs.shape[0]
    tm = MERGE_TM
    gna_blk = (3 * NA_WIDTH + 2 * DIFF_QK_WIDTH + DIFF_V_WIDTH) // D_MODEL
    const = lambda shape: pl.BlockSpec(shape, lambda i: (0, 0))
    n_first = xp.shape[0] // tm
    spec_p, spec_s = _two_group_specs(tm, D_MODEL, n_first)
    return pl.pallas_call(
        functools.partial(_merge_kernel, n_first),
        grid=(t // tm,),
        in_specs=[pl.BlockSpec((tm, NA_WIDTH), lambda i: (i, 0)),
                  pl.BlockSpec((tm, DIFF_V_WIDTH), lambda i: (i, 0)),
                  pl.BlockSpec((tm, D_MODEL), lambda i: (i, gna_blk)),
                  pl.BlockSpec((tm, D_MODEL), lambda i: (i, gna_blk + 1)),
                  spec_p, spec_s,
                  const((NA_WIDTH, D_MODEL)), const((DIFF_V_WIDTH, D_MODEL)), const((D_MODEL, D_MODEL))],
        out_specs=pl.BlockSpec((tm, D_MODEL), lambda i: (i, 0)),
        out_shape=jax.ShapeDtypeStruct((t, D_MODEL), F32),
        compiler_params=_cparams(("parallel",)),
        name="merge_outproj",
    )(a, b, proj, proj, xp, xs, wna, wdf, wout)


def _memkv_kernel(n_first, mp_ref, ms_ref, g_ref, w_ref, o_ref):
    def project(m_ref):
        h = _rms_scale(m_ref[...], g_ref[...], NORM_EPS).astype(BF16)
        o_ref[...] = jnp.dot(h, w_ref[...], preferred_element_type=F32).astype(o_ref.dtype)

    in_first = pl.program_id(0) < n_first
    pl.when(in_first)(lambda: project(mp_ref))
    pl.when(jnp.logical_not(in_first))(lambda: project(ms_ref))


def _memkv(mem_p, mem_s, g, w):
    t = mem_p.shape[0] + mem_s.shape[0]
    n_first = mem_p.shape[0] // MEM_LEN
    spec_p, spec_s = _two_group_specs(MEM_LEN, D_MODEL, n_first)
    return pl.pallas_call(
        functools.partial(_memkv_kernel, n_first),
        grid=(t // MEM_LEN,),
        in_specs=[spec_p, spec_s,
                  pl.BlockSpec((1, D_MODEL), lambda i: (0, 0)),
                  pl.BlockSpec((D_MODEL, 2 * XA_WIDTH), lambda i: (0, 0))],
        out_specs=pl.BlockSpec((MEM_LEN, 2 * XA_WIDTH), lambda i: (i, 0)),
        out_shape=jax.ShapeDtypeStruct((t, 2 * XA_WIDTH), BF16),
        compiler_params=_cparams(("parallel",)),
        name="mem_kv",
    )(mem_p, mem_s, g, w)


def _route(logits):
    lane = lax.broadcasted_iota(I32, logits.shape, 1).astype(F32)
    ninf = -jnp.inf
    big = float(LANES)
    gl = jnp.where(lane < N_GROUPS, logits, ninf)
    gmax = jnp.max(gl, axis=-1, keepdims=True)
    g = jnp.min(jnp.where(gl == gmax, lane, big), axis=-1, keepdims=True)
    pg = 1.0 / jnp.sum(jnp.exp(gl - gmax), axis=-1, keepdims=True)
    lo = N_GROUPS + EXPERTS_PER_GROUP * g
    el = jnp.where((lane >= lo) & (lane < lo + EXPERTS_PER_GROUP), logits, ninf)
    v1 = jnp.max(el, axis=-1, keepdims=True)
    i1 = jnp.min(jnp.where(el == v1, lane, big), axis=-1, keepdims=True)
    el2 = jnp.where(lane == i1, ninf, el)
    v2 = jnp.max(el2, axis=-1, keepdims=True)
    i2 = jnp.min(jnp.where(el2 == v2, lane, big), axis=-1, keepdims=True)
    t = jnp.exp(v2 - v1)
    den = 1.0 + t
    gate1 = pg * (1.0 / den)
    gate2 = pg * (t / den)
    gates = jnp.where(lane == 0.0, gate1, jnp.where(lane == 1.0, gate2, 0.0))
    eids = jnp.where(lane == 0.0, i1 - N_GROUPS, jnp.where(lane == 1.0, i2 - N_GROUPS, 0.0)).astype(I32)
    return gates, eids


def _pack_halves(h):
    half = h.shape[1] // 2
    hi = pltpu.bitcast(h[:, :half].astype(BF16).astype(F32), U32)
    lo = pltpu.bitcast(h[:, half:].astype(BF16).astype(F32), U32)
    return hi | (lo >> 16)


def _unpack_halves(w):
    hi = pltpu.bitcast(w & jnp.uint32(0xFFFF0000), F32).astype(BF16)
    lo = pltpu.bitcast(w << 16, F32).astype(BF16)
    return jnp.concatenate([hi, lo], axis=1)


def _store_row_tiles(ref, x):
    n, width = x.shape
    c = width // LANES
    for j in range(c):
        ref[pl.ds(j, n, stride=c), :] = x[:, j * LANES:(j + 1) * LANES]


def _load_row_tiles(ref, n, c):
    return jnp.concatenate([ref[pl.ds(j, n, stride=c), :] for j in range(c)], axis=1)


def _xattn_kernel(x_ref, kv_ref, gx_ref, wq_ref, wo_ref, gf_ref, wr_ref, br_ref,
                  x2_ref, hp_ref, gate_ref, eid_ref):
    scale = XA_HEAD_DIM ** -0.5
    heads = [slice(h * XA_HEAD_DIM, (h + 1) * XA_HEAD_DIM) for h in range(XA_HEADS)]
    nsub = x_ref.shape[0] // XA_SUB

    def rows(u):
        return slice(u * XA_SUB, (u + 1) * XA_SUB)

    def scores(u):
        hq = _rms_scale(x_ref[rows(u), :], gx_ref[...], NORM_EPS).astype(BF16)
        q = jnp.dot(hq, wq_ref[...], preferred_element_type=F32).astype(BF16)
        return [_dot_nt(q[:, sl], kv_ref[:, sl]) for sl in heads]

    def finish(u, sc):
        outs = []
        for h in range(XA_HEADS):
            vsl = slice(XA_WIDTH + h * XA_HEAD_DIM, XA_WIDTH + (h + 1) * XA_HEAD_DIM)
            s = sc[h] * (scale * LOG2E)
            m = jnp.max(s, axis=-1, keepdims=True)
            e = jnp.exp2(s - m)
            l = jnp.sum(e, axis=-1, keepdims=True)
            o = jnp.dot(e.astype(BF16), kv_ref[:, vsl], preferred_element_type=F32)
            outs.append((o / l).astype(BF16))
        o = jnp.concatenate(outs, axis=1)
        x2 = x_ref[rows(u), :] + jnp.dot(o, wo_ref[...], preferred_element_type=F32)
        x2_ref[rows(u), :] = x2
        h3 = _rms_scale(x2, gf_ref[...], NORM_EPS)
        _store_row_tiles(hp_ref.at[pl.ds(u * XA_SUB * HP_TILES, XA_SUB * HP_TILES)], _pack_halves(h3))
        logits = jnp.dot(h3.astype(BF16), wr_ref[...], preferred_element_type=F32) + br_ref[...]
        gates, eids = _route(logits)
        gate_ref[rows(u), :] = gates
        eid_ref[rows(u), :] = eids

    pending = scores(0)
    for u in range(nsub):
        sc = pending
        if u + 1 < nsub:
            pending = scores(u + 1)
        finish(u, sc)


def _xattn(x1, kv, gx, wq, wo, gf, wr, br):
    t = x1.shape[0]
    tm = XA_TM
    per_batch = SEQ // tm
    const = lambda shape: pl.BlockSpec(shape, lambda i: (0, 0))
    tile = lambda n: pl.BlockSpec((tm, n), lambda i: (i, 0))
    return pl.pallas_call(
        _xattn_kernel,
        grid=(t // tm,),
        in_specs=[tile(D_MODEL),
                  pl.BlockSpec((MEM_LEN, 2 * XA_WIDTH), lambda i: (i // per_batch, 0)),
                  const((1, D_MODEL)), const((D_MODEL, XA_WIDTH)), const((XA_WIDTH, D_MODEL)),
                  const((1, D_MODEL)), const((D_MODEL, LANES)), const((1, LANES))],
        out_specs=[tile(D_MODEL), pl.BlockSpec((tm * HP_TILES, LANES), lambda i: (i, 0)), tile(LANES), tile(LANES)],
        out_shape=[jax.ShapeDtypeStruct((t, D_MODEL), F32),
                   jax.ShapeDtypeStruct((t * HP_TILES, LANES), U32),
                   jax.ShapeDtypeStruct((t, LANES), F32),
                   jax.ShapeDtypeStruct((t, LANES), I32)],
        compiler_params=_cparams(("parallel",)),
        name="xattn_router",
    )(x1, kv, gx, wq, wo, gf, wr, br)


IDX_ROWS = 8
PIECE = 8


def _rank_kernel(eid_ref, gate_ref, idx_ref, gidx_ref, tile_ref, cnt_ref, carry_ref):
    @pl.when(pl.program_id(0) == 0)
    def _():
        carry_ref[...] = jnp.zeros_like(carry_ref)

    tm = eid_ref.shape[0]
    eid = eid_ref[...]
    lane = lax.broadcasted_iota(I32, (tm, LANES), 1)
    e1 = eid[:, 0:1]
    e2 = eid[:, 1:2]
    m1 = lane == e1
    m2 = lane == e2
    onehot = jnp.where(m1, 1.0, jnp.where(m2, 1.0, 0.0))
    r = lax.broadcasted_iota(I32, (tm, tm), 0)
    c = lax.broadcasted_iota(I32, (tm, tm), 1)
    tri = jnp.where(c < r, 1.0, 0.0).astype(BF16)
    local = jnp.dot(tri, onehot.astype(BF16), preferred_element_type=F32)
    carry = carry_ref[0:1, :]
    n_tile = jnp.sum(onehot, axis=0, keepdims=True)

    n_piece = jnp.floor((n_tile + (PIECE - 1)) * (1.0 / PIECE)) * PIECE

    def assignment(m, e):
        r_local = jnp.sum(jnp.where(m, local, 0.0), axis=-1, keepdims=True)
        r_global = r_local + jnp.sum(jnp.where(m, carry, 0.0), axis=-1, keepdims=True)
        first = jnp.sum(jnp.where(lane < e, n_tile, 0.0), axis=-1, keepdims=True)
        first_piece = jnp.sum(jnp.where(lane < e, n_piece, 0.0), axis=-1, keepdims=True)
        return first + r_local, r_global, first_piece + r_local

    p1, g1, q1 = assignment(m1, e1)
    p2, g2, q2 = assignment(m2, e2)
    cols = [p1, p2, g1, g2, e1.astype(F32), e2.astype(F32), q1, q2]
    table = jnp.zeros((tm, LANES), F32)
    for j, col in enumerate(cols):
        table = jnp.where(lane == j, col, table)
    idx_ref[0] = jnp.transpose(table)[0:IDX_ROWS, :].astype(I32)
    gidx_ref[0] = jnp.transpose(gate_ref[...])[0:IDX_ROWS, :]
    tile_ref[0] = jnp.where(lax.broadcasted_iota(I32, (8, LANES), 0) == 0, carry, n_tile)
    carry_ref[...] = carry_ref[...] + n_tile
    cnt_ref[...] = carry_ref[...]


def _rank(eid, gates):
    t = eid.shape[0]
    tm = RANK_TM
    n_tiles = t // tm
    return pl.pallas_call(
        _rank_kernel,
        grid=(n_tiles,),
        in_specs=[pl.BlockSpec((tm, LANES), lambda i: (i, 0)), pl.BlockSpec((tm, LANES), lambda i: (i, 0))],
        out_specs=[pl.BlockSpec((1, IDX_ROWS, tm), lambda i: (i, 0, 0)),
                   pl.BlockSpec((1, IDX_ROWS, tm), lambda i: (i, 0, 0)),
                   pl.BlockSpec((1, 8, LANES), lambda i: (i, 0, 0)),
                   pl.BlockSpec((8, LANES), lambda i: (0, 0))],
        out_shape=[jax.ShapeDtypeStruct((n_tiles, IDX_ROWS, tm), I32),
                   jax.ShapeDtypeStruct((n_tiles, IDX_ROWS, tm), F32),
                   jax.ShapeDtypeStruct((n_tiles, 8, LANES), F32),
                   jax.ShapeDtypeStruct((8, LANES), F32)],
        scratch_shapes=[pltpu.VMEM((8, LANES), F32)],
        compiler_params=_cparams(("arbitrary",)),
        name="expert_rank",
    )(eid, gates)


def _dispatch_kernel(n_tiles, cnt_ref, pend_ref, nused_ref, pos_ref, run_ref, run_prev_ref, hp_ref, xs_ref, sbuf,
                     zero_ref, sem, zsem):
    i = pl.program_id(0)
    slot = lax.rem(i, 2)
    tm = hp_ref.shape[0] // HP_TILES
    zb = zero_ref.shape[0]
    n_blocks = xs_ref.shape[0] // zb

    @pl.when(pl.program_id(0) == 0)
    def _():
        zero_ref[...] = jnp.zeros_like(zero_ref)
        for b in range(2):
            sbuf[b, pl.ds(TOP_K * tm * HP_TILES, PIECE * HP_TILES), :] = jnp.zeros((PIECE * HP_TILES, LANES), U32)

        def zero_copy(block):
            return pltpu.make_async_copy(zero_ref, xs_ref.at[pl.ds(pl.multiple_of(block * zb, zb), zb)], zsem)

        def tail_start(j, carry):
            zero_copy(j).start()
            return carry

        def tail_wait(j, carry):
            zero_copy(j).wait()
            return carry

        for e in range(N_EXPERTS):
            @pl.when(cnt_ref[e] > 0)
            def _():
                zero_copy(pend_ref[e] - 1).start()
        lax.fori_loop(nused_ref[0], n_blocks, tail_start, 0)
        for e in range(N_EXPERTS):
            @pl.when(cnt_ref[e] > 0)
            def _():
                zero_copy(pend_ref[e] - 1).wait()
        lax.fori_loop(nused_ref[0], n_blocks, tail_wait, 0)

    def place(t, carry):
        row = hp_ref[pl.ds(pl.multiple_of(t * HP_TILES, HP_TILES), HP_TILES), :]
        for k in range(TOP_K):
            sbuf[slot, pl.ds(pl.multiple_of(pos_ref[0, k, t] * HP_TILES, HP_TILES), HP_TILES), :] = row
        return carry

    lax.fori_loop(0, tm, place, 0, unroll=8)

    def piece_copy(src_tok, dst_slot, buf):
        rows = PIECE * HP_TILES
        return pltpu.make_async_copy(sbuf.at[buf, pl.ds(pl.multiple_of(src_tok * HP_TILES, HP_TILES), rows)],
                                     xs_ref.at[pl.ds(pl.multiple_of(dst_slot * HP_TILES, HP_TILES), rows)], sem.at[buf])

    def drain(runs, buf):
        def wait_piece(j, carry):
            piece_copy(0, 0, buf).wait()
            return carry
        lax.fori_loop(0, runs[0, 0, 3 * N_EXPERTS], wait_piece, 0)

    @pl.when(i > 0)
    def _():
        drain(run_prev_ref, 1 - slot)

    for e in range(N_EXPERTS):
        slot0 = run_ref[0, 0, e]
        first = run_ref[0, 0, 2 * N_EXPERTS + e]

        def run_piece(j, carry):
            piece_copy(first + PIECE * j, slot0 + PIECE * j, slot).start()
            return carry

        lax.fori_loop(0, run_ref[0, 0, N_EXPERTS + e], run_piece, 0)

    @pl.when(i == n_tiles - 1)
    def _():
        drain(run_ref, slot)


def _dispatch(cnt, pend_blocks, nused, pos, runs, hp, n_slots):
    n_tiles, _, tm = pos.shape
    grid_spec = pltpu.PrefetchScalarGridSpec(
        num_scalar_prefetch=3,
        grid=(n_tiles,),
        in_specs=[pl.BlockSpec((1, TOP_K, tm), lambda i, *_: (i, 0, 0), memory_space=pltpu.SMEM),
                  pl.BlockSpec((1, 1, LANES), lambda i, *_: (i, 0, 0), memory_space=pltpu.SMEM),
                  pl.BlockSpec((1, 1, LANES), lambda i, *_: (jnp.maximum(i - 1, 0), 0, 0), memory_space=pltpu.SMEM),
                  pl.BlockSpec((tm * HP_TILES, LANES), lambda i, *_: (i, 0))],
        out_specs=pl.BlockSpec(memory_space=pl.ANY),
        scratch_shapes=[pltpu.VMEM((2, (TOP_K * tm + PIECE) * HP_TILES, LANES), U32),
                        pltpu.VMEM((MOE_TM * HP_TILES, LANES), U32),
                        pltpu.SemaphoreType.DMA((2,)), pltpu.SemaphoreType.DMA(())],
    )
    return pl.pallas_call(
        functools.partial(_dispatch_kernel, n_tiles),
        grid_spec=grid_spec,
        out_shape=jax.ShapeDtypeStruct((n_slots * HP_TILES, LANES), U32),
        compiler_params=_cparams(("arbitrary",)),
        name="moe_dispatch",
    )(cnt, pend_blocks, nused, pos, runs, runs, hp)


MOE_DOWN_CHUNKS = 4


def _moe_kernel(bexp_ref, nused_ref, xs_ref, wg_ref, wu_ref, wd_ref, o_ref, wgb, wub, wdb):
    i = pl.program_id(0)
    nused = nused_ref[0]
    tm = xs_ref.shape[0] // HP_TILES

    @pl.when((i < nused) & ((i == 0) | (bexp_ref[i] != bexp_ref[jnp.maximum(i - 1, 0)])))
    def _():
        wgb[...] = wg_ref[0].astype(BF16)
        wub[...] = wu_ref[0].astype(BF16)
        wdb[...] = wd_ref[0].astype(BF16)

    @pl.when(i < nused)
    def _():
        x = _unpack_halves(_load_row_tiles(xs_ref, tm, HP_TILES))
        hg = jnp.dot(x, wgb[...], preferred_element_type=F32)
        hu = jnp.dot(x, wub[...], preferred_element_type=F32)
        hdn = ((hg * jax.nn.sigmoid(hg)) * hu).astype(BF16)
        cw = D_MODEL // MOE_DOWN_CHUNKS
        for c in range(MOE_DOWN_CHUNKS):
            y = jnp.dot(hdn, wdb[:, c * cw:(c + 1) * cw], preferred_element_type=F32)
            for j in range(cw // LANES):
                o_ref[pl.ds(c * (cw // LANES) + j, tm, stride=YS_TILES), :] = y[:, j * LANES:(j + 1) * LANES]

    @pl.when(i >= nused)
    def _():
        o_ref[...] = jnp.zeros_like(o_ref)


def _moe(bexp, nused, xs, wg, wu, wd):
    tm = MOE_TM
    n_slots = xs.shape[0] // HP_TILES
    n_blocks = n_slots // tm
    grid_spec = pltpu.PrefetchScalarGridSpec(
        num_scalar_prefetch=2,
        grid=(n_blocks,),
        in_specs=[pl.BlockSpec((tm * HP_TILES, LANES), lambda i, be, nu: (jnp.minimum(i, nu[0] - 1), 0)),
                  pl.BlockSpec((1, D_MODEL, D_EXPERT), lambda i, be, nu: (be[i], 0, 0)),
                  pl.BlockSpec((1, D_MODEL, D_EXPERT), lambda i, be, nu: (be[i], 0, 0)),
                  pl.BlockSpec((1, D_EXPERT, D_MODEL), lambda i, be, nu: (be[i], 0, 0))],
        out_specs=pl.BlockSpec((tm * YS_TILES, LANES), lambda i, be, nu: (i, 0)),
        scratch_shapes=[pltpu.VMEM((D_MODEL, D_EXPERT), BF16), pltpu.VMEM((D_MODEL, D_EXPERT), BF16),
                        pltpu.VMEM((D_EXPERT, D_MODEL), BF16)],
    )
    return pl.pallas_call(
        _moe_kernel,
        grid_spec=grid_spec,
        out_shape=jax.ShapeDtypeStruct((n_slots * YS_TILES, LANES), F32),
        compiler_params=_cparams(("arbitrary",)),
        name="moe_experts",
    )(bexp, nused, xs, wg, wu, wd)


def _combine_kernel(n_tiles, pos_ref, run_ref, run_next_ref, gate_ref, x_ref, g_ref, ys_ref, o_ref, ybuf, mbuf, sem):
    i = pl.program_id(0)
    tm = x_ref.shape[0]
    slot = lax.rem(i, 2)

    def piece_copy(src_slot, dst_pos, buf):
        rows = PIECE * YS_TILES
        return pltpu.make_async_copy(ys_ref.at[pl.ds(pl.multiple_of(src_slot * YS_TILES, YS_TILES), rows)],
                                     ybuf.at[buf, pl.ds(pl.multiple_of(dst_pos * YS_TILES, YS_TILES), rows)],
                                     sem.at[buf])

    def request(runs, buf):
        for e in range(N_EXPERTS):
            slot0 = runs[0, 0, e]
            first = runs[0, 0, 2 * N_EXPERTS + e]

            def run_piece(j, carry):
                piece_copy(slot0 + PIECE * j, first + PIECE * j, buf).start()
                return carry

            lax.fori_loop(0, runs[0, 0, N_EXPERTS + e], run_piece, 0)

    @pl.when(i == 0)
    def _():
        request(run_ref, 0)

    @pl.when(i + 1 < n_tiles)
    def _():
        request(run_next_ref, 1 - slot)

    def drain(j, carry):
        piece_copy(0, 0, slot).wait()
        return carry

    lax.fori_loop(0, run_ref[0, 0, 3 * N_EXPERTS], drain, 0)

    def token(t, carry):
        acc = None
        for k in range(TOP_K):
            rows = pl.ds(pl.multiple_of(pos_ref[0, k, t] * YS_TILES, YS_TILES), YS_TILES)
            term = ybuf[slot, rows, :] * gate_ref[0, k, t]
            acc = term if acc is None else acc + term
        mbuf[pl.ds(pl.multiple_of(t * YS_TILES, YS_TILES), YS_TILES), :] = acc
        return carry

    lax.fori_loop(0, tm, token, 0, unroll=8)
    moe = _load_row_tiles(mbuf, tm, YS_TILES)
    o_ref[...] = _rms_scale(x_ref[...] + moe, g_ref[...], NORM_EPS)


def _combine(pos, runs, gidx, x2, g, ys, row0, n_rows):
    _, _, tm = pos.shape
    tile0 = row0 // tm
    n = n_rows // tm
    smem = lambda rows: pl.BlockSpec((1, rows, tm), lambda i: (tile0 + i, 0, 0), memory_space=pltpu.SMEM)
    run_spec = lambda ahead: pl.BlockSpec((1, 1, LANES), lambda i: (tile0 + jnp.minimum(i + ahead, n - 1), 0, 0),
                                          memory_space=pltpu.SMEM)
    return pl.pallas_call(
        functools.partial(_combine_kernel, n),
        grid=(n,),
        in_specs=[smem(TOP_K), run_spec(0), run_spec(1), smem(TOP_K),
                  pl.BlockSpec((tm, D_MODEL), lambda i: (tile0 + i, 0)),
                  pl.BlockSpec((1, D_MODEL), lambda i: (0, 0)),
                  pl.BlockSpec(memory_space=pl.ANY)],
        out_specs=pl.BlockSpec((tm, D_MODEL), lambda i: (i, 0)),
        out_shape=jax.ShapeDtypeStruct((n_rows, D_MODEL), F32),
        scratch_shapes=[pltpu.VMEM((2, (TOP_K * tm + N_EXPERTS * PIECE) * YS_TILES, LANES), F32),
                        pltpu.VMEM((tm * YS_TILES, LANES), F32),
                        pltpu.SemaphoreType.DMA((2,))],
        compiler_params=_cparams(("arbitrary",)),
        name="moe_combine",
    )(pos, runs, runs, gidx[:, 0:TOP_K, :], x2, g, ys)


def _routing_plan(idx, tiles, counts):
    cnt = counts[0, :N_EXPERTS].astype(I32)
    padded = jnp.where(cnt > 0, (cnt + PIECE - 1 + MOE_TM - 1) // MOE_TM * MOE_TM, 0)
    pends = jnp.cumsum(padded)
    pstarts = pends - padded
    n_tiles, _, tm = idx.shape
    n_blocks = n_tiles * tm * TOP_K // MOE_TM + N_EXPERTS + 1
    blk0 = jnp.arange(n_blocks, dtype=I32) * MOE_TM
    bexp = jnp.minimum(jnp.sum(pends[None, :] <= blk0[:, None], axis=1), N_EXPERTS - 1).astype(I32)
    nused = (pends[-1:] // MOE_TM).astype(I32)
    before = tiles[:, 0, :N_EXPERTS].astype(I32)
    inside = tiles[:, 1, :N_EXPERTS].astype(I32)
    first = jnp.cumsum(inside, axis=1) - inside
    pieces = (inside + PIECE - 1) // PIECE
    first_piece = (jnp.cumsum(pieces, axis=1) - pieces) * PIECE

    def run_table(first_pos):
        return jnp.concatenate([pstarts[None, :] + before, pieces, first_pos, jnp.sum(pieces, axis=1, keepdims=True),
                                jnp.zeros((n_tiles, LANES - 3 * N_EXPERTS - 1), I32)], axis=1).reshape(n_tiles, 1, LANES)

    pos = idx[:, 0:TOP_K, :]
    pos_piece = idx[:, 3 * TOP_K:4 * TOP_K, :]
    return pos, run_table(first), pos_piece, run_table(first_piece), cnt, (pends // MOE_TM).astype(I32), bexp, nused, n_blocks * MOE_TM


def kernel(x_prompt, x_sample, mem_prompt, mem_sample, mix_norm_g, w_in, na_rpb, diff_lambda_q1, diff_lambda_k1, diff_lambda_q2, diff_lambda_k2, diff_subln_g, w_branch_na, w_branch_diff, w_out, xa_norm_g, mem_norm_g, xa_w_q, xa_w_kv, xa_w_o, ffn_norm_g, router_group_w, router_group_b, router_expert_w, router_expert_b, w_gate, w_up, w_down, final_norm_g):
    nb_p, nb_s = x_prompt.shape[0], x_sample.shape[0]
    nb = nb_p + nb_s
    t = nb * SEQ
    xp = x_prompt.reshape(nb_p * SEQ, D_MODEL)
    xs_in = x_sample.reshape(nb_s * SEQ, D_MODEL)
    mem_p = mem_prompt.reshape(nb_p * MEM_LEN, D_MODEL)
    mem_s = mem_sample.reshape(nb_s * MEM_LEN, D_MODEL)
    row = lambda v: v.reshape(1, -1).astype(F32)
    bf = lambda w: w.astype(BF16)

    qcol = 3 * NA_WIDTH
    colscale = jnp.ones((1, IN_COLS), F32).at[:, qcol:qcol + DIFF_QK_WIDTH].set(DIFF_QK_DIM ** -0.5 * LOG2E)
    proj = _inproj(xp, xs_in, row(mix_norm_g[0]), bf(w_in[0]), colscale)
    proj3 = proj.reshape(nb, SEQ, IN_COLS)
    a = _na_attention(proj3, _na_bias_blocks(na_rpb[0]))
    b = _diff_attention(proj3, row(diff_lambda_q1[0]), row(diff_lambda_k1[0]),
                        row(diff_lambda_q2[0]), row(diff_lambda_k2[0]), row(diff_subln_g[0]))
    x1 = _merge(a.reshape(t, NA_WIDTH), b.reshape(t, DIFF_V_WIDTH), proj, xp, xs_in,
                bf(w_branch_na[0]), bf(w_branch_diff[0]), bf(w_out[0]))
    kv = _memkv(mem_p, mem_s, row(mem_norm_g[0]), bf(xa_w_kv[0]))
    w_router = jnp.zeros((D_MODEL, LANES), F32)
    w_router = w_router.at[:, :N_GROUPS].set(router_group_w[0]).at[:, N_GROUPS:N_GROUPS + N_EXPERTS].set(router_expert_w[0])
    b_router = jnp.zeros((1, LANES), F32)
    b_router = b_router.at[0, :N_GROUPS].set(router_group_b[0]).at[0, N_GROUPS:N_GROUPS + N_EXPERTS].set(router_expert_b[0])
    x2, hp, gates, eid = _xattn(x1, kv, row(xa_norm_g[0]), bf(xa_w_q[0]), bf(xa_w_o[0]),
                                row(ffn_norm_g[0]), bf(w_router), b_router)
    idx, gidx, tiles, counts = _rank(eid, gates)
    pos, runs, pos_c, runs_c, cnt, pend_blocks, bexp, nused, n_slots = _routing_plan(idx, tiles, counts)
    xs = _dispatch(cnt, pend_blocks, nused, pos, runs, hp, n_slots)
    ys = _moe(bexp, nused, xs, w_gate[0], w_up[0], w_down[0])
    fg = row(final_norm_g)
    y_p = _combine(pos_c, runs_c, gidx, x2, fg, ys, 0, nb_p * SEQ).reshape(nb_p, SEQ, D_MODEL)
    y_s = _combine(pos_c, runs_c, gidx, x2, fg, ys, nb_p * SEQ, nb_s * SEQ).reshape(nb_s, SEQ, D_MODEL)
    return (y_p, y_s)
```

```python
import functools
import math

import jax
import jax.numpy as jnp
from jax import lax
from jax.experimental import pallas as pl
from jax.experimental.pallas import tpu as pltpu

F32 = jnp.float32
BF16 = jnp.bfloat16
I32 = jnp.int32
U32 = jnp.uint32

D_MODEL = 2048
SEQ = 2048
GRID_W = 64
ROWS = SEQ // GRID_W
NA_HEADS = 8
NA_HEAD_DIM = 128
NA_WIDTH = NA_HEADS * NA_HEAD_DIM
NA_KH = 8
NA_KW = 16
DIFF_HEADS = 8
DIFF_QK_DIM = 64
DIFF_V_DIM = 128
DIFF_QK_WIDTH = DIFF_HEADS * 2 * DIFF_QK_DIM
DIFF_V_WIDTH = DIFF_HEADS * DIFF_V_DIM
IN_COLS = 3 * NA_WIDTH + 2 * DIFF_QK_WIDTH + DIFF_V_WIDTH + 2 * D_MODEL
MEM_LEN = 256
XA_HEADS = 4
XA_HEAD_DIM = 128
XA_WIDTH = XA_HEADS * XA_HEAD_DIM
N_GROUPS = 4
EXPERTS_PER_GROUP = 8
N_EXPERTS = N_GROUPS * EXPERTS_PER_GROUP
TOP_K = 2
D_EXPERT = 512
NORM_EPS = 1e-6
SUBLN_EPS = 1e-5
NEG_INF = -1e30
LAMBDA_INIT = 0.8 - 0.6 * math.exp(-0.3 * 0)
LOG2E = math.log2(math.e)

LANES = 128
HP_TILES = D_MODEL // 2 // LANES
YS_TILES = D_MODEL // LANES
VMEM_LIMIT = 56 * 1024 * 1024

INPROJ_TM = 1024
INPROJ_TN = 1024
NORM_CHUNK = 128
DIFF_TQ = 2048
DIFF_SUB = 128
MERGE_TM = 256
XA_TM = 512
XA_SUB = 256
RANK_TM = 512
MOE_TM = 512
ROW_TM = 256


def _cparams(sem):
    return pltpu.CompilerParams(dimension_semantics=sem, vmem_limit_bytes=VMEM_LIMIT)


def _rms_scale(x, g, eps):
    ms = jnp.mean(x * x, axis=-1, keepdims=True)
    return x * lax.rsqrt(ms + eps) * g


def _dot_nt(a, b):
    return lax.dot_general(a, b, (((1,), (1,)), ((), ())), preferred_element_type=F32)


def _two_group_specs(tm, width, n_first):
    first = pl.BlockSpec((tm, width), lambda i, *_: (jnp.minimum(i, n_first - 1), 0))
    second = pl.BlockSpec((tm, width), lambda i, *_: (jnp.maximum(i - n_first, 0), 0))
    return first, second


def _inproj_kernel(n_first, xp_ref, xs_ref, g_ref, w_ref, cs_ref, o_ref, h_ref):
    def normalise(x_ref):
        def chunk(c, carry):
            rows = pl.ds(pl.multiple_of(c * NORM_CHUNK, NORM_CHUNK), NORM_CHUNK)
            h_ref[rows, :] = _rms_scale(x_ref[rows, :], g_ref[...], NORM_EPS).astype(BF16)
            return carry
        lax.fori_loop(0, x_ref.shape[0] // NORM_CHUNK, chunk, 0)

    first_col = pl.program_id(1) == 0
    in_first = pl.program_id(0) < n_first
    pl.when(first_col & in_first)(lambda: normalise(xp_ref))
    pl.when(first_col & jnp.logical_not(in_first))(lambda: normalise(xs_ref))
    acc = jnp.dot(h_ref[...], w_ref[...], preferred_element_type=F32)
    o_ref[...] = (acc * cs_ref[...]).astype(o_ref.dtype)


def _inproj(xp, xs, g, w, colscale):
    d = xp.shape[1]
    t = xp.shape[0] + xs.shape[0]
    n = w.shape[1]
    tm = INPROJ_TM
    n_first = xp.shape[0] // tm
    spec_p, spec_s = _two_group_specs(tm, d, n_first)
    return pl.pallas_call(
        functools.partial(_inproj_kernel, n_first),
        grid=(t // tm, n // INPROJ_TN),
        in_specs=[spec_p, spec_s,
                  pl.BlockSpec((1, d), lambda i, j: (0, 0)),
                  pl.BlockSpec((d, INPROJ_TN), lambda i, j: (0, j)),
                  pl.BlockSpec((1, INPROJ_TN), lambda i, j: (0, j))],
        out_specs=pl.BlockSpec((tm, INPROJ_TN), lambda i, j: (i, j)),
        out_shape=jax.ShapeDtypeStruct((t, n), BF16),
        scratch_shapes=[pltpu.VMEM((tm, d), BF16)],
        compiler_params=_cparams(("parallel", "arbitrary")),
        name="inproj",
    )(xp, xs, g, w, colscale)


NA_QROWS = 4
NA_KROWS = 12
NA_TILES = ROWS // NA_QROWS


def _na_tile_key_row(ti):
    return min(max(NA_QROWS * ti - NA_KH // 2, 0), ROWS - NA_KROWS)


def _na_bias_blocks(rpb):
    cols = jnp.arange(GRID_W)
    cstart = jnp.clip(cols - NA_KW // 2, 0, GRID_W - NA_KW)
    col_ok = (cols[None, :] >= cstart[:, None]) & (cols[None, :] < cstart[:, None] + NA_KW)
    dc = jnp.clip(cols[None, :] - cols[:, None], 1 - NA_KW, NA_KW - 1) + NA_KW - 1
    pick = (dc[None] == jnp.arange(2 * NA_KW - 1)[:, None, None]).astype(F32)
    t = jnp.einsum("hdx,xck->hdck", rpb.astype(F32), pick, precision=lax.Precision.HIGHEST)
    return jnp.where(col_ok[None, None], t, NEG_INF)


def _na_kernel(q_ref, k_ref, v_ref, t_ref, o_ref, bias_ref):
    scale = NA_HEAD_DIM ** -0.5
    tq = NA_QROWS * GRID_W
    tk = NA_KROWS * GRID_W

    @pl.when(pl.program_id(1) == 0)
    def _():
        masked = jnp.full((GRID_W, GRID_W), NEG_INF, F32)
        for variant, ti in enumerate((0, 1, NA_TILES - 1)):
            for jr in range(NA_QROWS):
                r = NA_QROWS * ti + jr
                rs = min(max(r - NA_KH // 2, 0), ROWS - NA_KH)
                blocks = []
                for i in range(NA_KROWS):
                    kr = _na_tile_key_row(ti) + i
                    blocks.append(t_ref[0, kr - r + NA_KH - 1] if rs <= kr < rs + NA_KH else masked)
                bias_ref[variant, jr * GRID_W:(jr + 1) * GRID_W, :] = jnp.concatenate(blocks, axis=1) * LOG2E

    def key_rows(ti):
        k0 = _na_tile_key_row(ti) * GRID_W
        return slice(k0, k0 + tk)

    def scores(ti):
        return _dot_nt(q_ref[0, ti * tq:(ti + 1) * tq, :], k_ref[0, key_rows(ti), :])

    pending = scores(0)
    for ti in range(NA_TILES):
        variant = 0 if ti == 0 else (2 if ti == NA_TILES - 1 else 1)
        qrows = slice(ti * tq, (ti + 1) * tq)
        krows = key_rows(ti)
        s = pending * (scale * LOG2E) + bias_ref[variant]
        if ti + 1 < NA_TILES:
            pending = scores(ti + 1)
        m = jnp.max(s, axis=-1, keepdims=True)
        e = jnp.exp2(s - m)
        l = jnp.sum(e, axis=-1, keepdims=True)
        o = jnp.dot(e.astype(BF16), v_ref[0, krows, :], preferred_element_type=F32)
        o_ref[0, qrows, :] = (o / l).astype(o_ref.dtype)


def _na_attention(proj3, blocks):
    b = proj3.shape[0]
    hd = NA_HEAD_DIM
    return pl.pallas_call(
        _na_kernel,
        grid=(NA_HEADS, b),
        in_specs=[pl.BlockSpec((1, SEQ, hd), lambda h, i: (i, 0, h)),
                  pl.BlockSpec((1, SEQ, hd), lambda h, i: (i, 0, NA_HEADS + h)),
                  pl.BlockSpec((1, SEQ, hd), lambda h, i: (i, 0, 2 * NA_HEADS + h)),
                  pl.BlockSpec((1,) + blocks.shape[1:], lambda h, i: (h, 0, 0, 0))],
        out_specs=pl.BlockSpec((1, SEQ, hd), lambda h, i: (i, 0, h)),
        out_shape=jax.ShapeDtypeStruct((b, SEQ, NA_WIDTH), BF16),
        scratch_shapes=[pltpu.VMEM((3, NA_QROWS * GRID_W, NA_KROWS * GRID_W), F32)],
        compiler_params=_cparams(("arbitrary", "arbitrary")),
        name="na_attn",
    )(proj3, proj3, proj3, blocks)


def _alibi_table(tq):
    slopes = jnp.exp2(-8.0 * jnp.arange(1, DIFF_HEADS + 1, dtype=F32) / DIFF_HEADS)
    r = jnp.arange(tq)[:, None]
    x = jnp.arange(2 * SEQ - tq)[None, :]
    dist = jnp.abs(r - x + (SEQ - tq)).astype(F32)
    return (slopes * LOG2E)[:, None, None] * dist[None]


def _diff_kernel(lq1_ref, lk1_ref, lq2_ref, lk2_ref, q_ref, k_ref, v_ref, g_ref, alibi_ref, o_ref):
    qi = pl.program_id(2)
    ts = alibi_ref.shape[1]
    nsub = q_ref.shape[1] // ts
    lam = (jnp.exp(jnp.sum(lq1_ref[...] * lk1_ref[...], axis=-1, keepdims=True))
           - jnp.exp(jnp.sum(lq2_ref[...] * lk2_ref[...], axis=-1, keepdims=True)) + LAMBDA_INIT)
    k = k_ref[0]

    def scores(u):
        q = q_ref[0, u * ts:(u + 1) * ts, :]
        return [_dot_nt(q[:, mi * DIFF_QK_DIM:(mi + 1) * DIFF_QK_DIM], k[:, mi * DIFF_QK_DIM:(mi + 1) * DIFF_QK_DIM])
                for mi in range(2)]

    def expo(s, bias):
        s = s - bias
        m = jnp.max(s, axis=-1, keepdims=True)
        e = jnp.exp2(s - m)
        return e, jnp.sum(e, axis=-1, keepdims=True)

    pending = scores(0)
    for u in range(nsub):
        s0, s1 = pending
        if u + 1 < nsub:
            pending = scores(u + 1)
        off = (SEQ - ts) - (qi * nsub + u) * ts
        bias = alibi_ref[0, :, pl.ds(pl.multiple_of(off, ts), SEQ)]
        e0, l0 = expo(s0, bias)
        e1, l1 = expo(s1, bias)
        a = (e0 - e1 * (lam * l0 / l1)).astype(BF16)
        o = jnp.dot(a, v_ref[0], preferred_element_type=F32) / l0
        o = _rms_scale(o, g_ref[...], SUBLN_EPS) * (1.0 - LAMBDA_INIT)
        o_ref[0, u * ts:(u + 1) * ts, :] = o.astype(o_ref.dtype)


def _diff_attention(proj3, lq1, lk1, lq2, lk2, subln_g):
    b = proj3.shape[0]
    tq = DIFF_TQ
    qoff = 3 * NA_WIDTH // LANES
    koff = qoff + DIFF_QK_WIDTH // LANES
    voff = koff + DIFF_QK_WIDTH // LANES
    vec = lambda n: pl.BlockSpec((1, n), lambda h, i, j: (0, 0))
    return pl.pallas_call(
        _diff_kernel,
        grid=(DIFF_HEADS, b, SEQ // tq),
        in_specs=[vec(DIFF_QK_DIM), vec(DIFF_QK_DIM), vec(DIFF_QK_DIM), vec(DIFF_QK_DIM),
                  pl.BlockSpec((1, tq, LANES), lambda h, i, j: (i, j, qoff + h)),
                  pl.BlockSpec((1, SEQ, LANES), lambda h, i, j: (i, 0, koff + h)),
                  pl.BlockSpec((1, SEQ, LANES), lambda h, i, j: (i, 0, voff + h)),
                  vec(DIFF_V_DIM),
                  pl.BlockSpec((1, DIFF_SUB, 2 * SEQ - DIFF_SUB), lambda h, i, j: (h, 0, 0))],
        out_specs=pl.BlockSpec((1, tq, DIFF_V_DIM), lambda h, i, j: (i, j, h)),
        out_shape=jax.ShapeDtypeStruct((b, SEQ, DIFF_V_WIDTH), BF16),
        compiler_params=_cparams(("arbitrary", "arbitrary", "arbitrary")),
        name="diff_attn",
    )(lq1, lk1, lq2, lk2, proj3, proj3, proj3, subln_g, _alibi_table(DIFF_SUB))


def _merge_kernel(n_first, a_ref, b_ref, gna_ref, gdf_ref, xp_ref, xs_ref, wna_ref, wdf_ref, wout_ref, o_ref):
    pa = jnp.dot(a_ref[...], wna_ref[...], preferred_element_type=F32)
    pb = jnp.dot(b_ref[...], wdf_ref[...], preferred_element_type=F32)
    merged = (jax.nn.sigmoid(gna_ref[...].astype(F32)) * pa
              + jax.nn.sigmoid(gdf_ref[...].astype(F32)) * pb)
    delta = jnp.dot(merged.astype(BF16), wout_ref[...], preferred_element_type=F32)
    in_first = pl.program_id(0) < n_first

    @pl.when(in_first)
    def _():
        o_ref[...] = xp_ref[...] + delta

    @pl.when(jnp.logical_not(in_first))
    def _():
        o_ref[...] = xs_ref[...] + delta


def _merge(a, b, proj, xp, xs, wna, wdf, wout):
    t = xp.shape[0] + xs.shape[0]
    tm = MERGE_TM
    gna_blk = (3 * NA_WIDTH + 2 * DIFF_QK_WIDTH + DIFF_V_WIDTH) // D_MODEL
    const = lambda shape: pl.BlockSpec(shape, lambda i: (0, 0))
    n_first = xp.shape[0] // tm
    spec_p, spec_s = _two_group_specs(tm, D_MODEL, n_first)
    return pl.pallas_call(
        functools.partial(_merge_kernel, n_first),
        grid=(t // tm,),
        in_specs=[pl.BlockSpec((tm, NA_WIDTH), lambda i: (i, 0)),
                  pl.BlockSpec((tm, DIFF_V_WIDTH), lambda i: (i, 0)),
                  pl.BlockSpec((tm, D_MODEL), lambda i: (i, gna_blk)),
                  pl.BlockSpec((tm, D_MODEL), lambda i: (i, gna_blk + 1)),
                  spec_p, spec_s,
                  const((NA_WIDTH, D_MODEL)), const((DIFF_V_WIDTH, D_MODEL)), const((D_MODEL, D_MODEL))],
        out_specs=pl.BlockSpec((tm, D_MODEL), lambda i: (i, 0)),
        out_shape=jax.ShapeDtypeStruct((t, D_MODEL), F32),
        compiler_params=_cparams(("parallel",)),
        name="merge_outproj",
    )(a, b, proj, proj, xp, xs, wna, wdf, wout)


def _memkv_kernel(n_first, mp_ref, ms_ref, g_ref, w_ref, o_ref):
    def project(m_ref):
        h = _rms_scale(m_ref[...], g_ref[...], NORM_EPS).astype(BF16)
        o_ref[...] = jnp.dot(h, w_ref[...], preferred_element_type=F32).astype(o_ref.dtype)

    in_first = pl.program_id(0) < n_first
    pl.when(in_first)(lambda: project(mp_ref))
    pl.when(jnp.logical_not(in_first))(lambda: project(ms_ref))


def _memkv(mem_p, mem_s, g, w):
    t = mem_p.shape[0] + mem_s.shape[0]
    n_first = mem_p.shape[0] // MEM_LEN
    spec_p, spec_s = _two_group_specs(MEM_LEN, D_MODEL, n_first)
    return pl.pallas_call(
        functools.partial(_memkv_kernel, n_first),
        grid=(t // MEM_LEN,),
        in_specs=[spec_p, spec_s,
                  pl.BlockSpec((1, D_MODEL), lambda i: (0, 0)),
                  pl.BlockSpec((D_MODEL, 2 * XA_WIDTH), lambda i: (0, 0))],
        out_specs=pl.BlockSpec((MEM_LEN, 2 * XA_WIDTH), lambda i: (i, 0)),
        out_shape=jax.ShapeDtypeStruct((t, 2 * XA_WIDTH), BF16),
        compiler_params=_cparams(("parallel",)),
        name="mem_kv",
    )(mem_p, mem_s, g, w)


def _route(logits):
    lane = lax.broadcasted_iota(I32, logits.shape, 1).astype(F32)
    ninf = -jnp.inf
    big = float(LANES)
    gl = jnp.where(lane < N_GROUPS, logits, ninf)
    gmax = jnp.max(gl, axis=-1, keepdims=True)
    g = jnp.min(jnp.where(gl == gmax, lane, big), axis=-1, keepdims=True)
    pg = 1.0 / jnp.sum(jnp.exp(gl - gmax), axis=-1, keepdims=True)
    lo = N_GROUPS + EXPERTS_PER_GROUP * g
    el = jnp.where((lane >= lo) & (lane < lo + EXPERTS_PER_GROUP), logits, ninf)
    v1 = jnp.max(el, axis=-1, keepdims=True)
    i1 = jnp.min(jnp.where(el == v1, lane, big), axis=-1, keepdims=True)
    el2 = jnp.where(lane == i1, ninf, el)
    v2 = jnp.max(el2, axis=-1, keepdims=True)
    i2 = jnp.min(jnp.where(el2 == v2, lane, big), axis=-1, keepdims=True)
    t = jnp.exp(v2 - v1)
    den = 1.0 + t
    gate1 = pg * (1.0 / den)
    gate2 = pg * (t / den)
    gates = jnp.where(lane == 0.0, gate1, jnp.where(lane == 1.0, gate2, 0.0))
    eids = jnp.where(lane == 0.0, i1 - N_GROUPS, jnp.where(lane == 1.0, i2 - N_GROUPS, 0.0)).astype(I32)
    return gates, eids


def _pack_halves(h):
    half = h.shape[1] // 2
    hi = pltpu.bitcast(h[:, :half].astype(BF16).astype(F32), U32)
    lo = pltpu.bitcast(h[:, half:].astype(BF16).astype(F32), U32)
    return hi | (lo >> 16)


def _unpack_halves(w):
    hi = pltpu.bitcast(w & jnp.uint32(0xFFFF0000), F32).astype(BF16)
    lo = pltpu.bitcast(w << 16, F32).astype(BF16)
    return jnp.concatenate([hi, lo], axis=1)


def _store_row_tiles(ref, x):
    n, width = x.shape
    c = width // LANES
    for j in range(c):
        ref[pl.ds(j, n, stride=c), :] = x[:, j * LANES:(j + 1) * LANES]


def _load_row_tiles(ref, n, c):
    return jnp.concatenate([ref[pl.ds(j, n, stride=c), :] for j in range(c)], axis=1)


def _xattn_kernel(x_ref, kv_ref, gx_ref, wq_ref, wo_ref, gf_ref, wr_ref, br_ref,
                  x2_ref, hp_ref, gate_ref, eid_ref):
    scale = XA_HEAD_DIM ** -0.5
    heads = [slice(h * XA_HEAD_DIM, (h + 1) * XA_HEAD_DIM) for h in range(XA_HEADS)]
    nsub = x_ref.shape[0] // XA_SUB

    def rows(u):
        return slice(u * XA_SUB, (u + 1) * XA_SUB)

    def scores(u):
        hq = _rms_scale(x_ref[rows(u), :], gx_ref[...], NORM_EPS).astype(BF16)
        q = jnp.dot(hq, wq_ref[...], preferred_element_type=F32).astype(BF16)
        return [_dot_nt(q[:, sl], kv_ref[:, sl]) for sl in heads]

    def finish(u, sc):
        outs = []
        for h in range(XA_HEADS):
            vsl = slice(XA_WIDTH + h * XA_HEAD_DIM, XA_WIDTH + (h + 1) * XA_HEAD_DIM)
            s = sc[h] * (scale * LOG2E)
            m = jnp.max(s, axis=-1, keepdims=True)
            e = jnp.exp2(s - m)
            l = jnp.sum(e, axis=-1, keepdims=True)
            o = jnp.dot(e.astype(BF16), kv_ref[:, vsl], preferred_element_type=F32)
            outs.append((o / l).astype(BF16))
        o = jnp.concatenate(outs, axis=1)
        x2 = x_ref[rows(u), :] + jnp.dot(o, wo_ref[...], preferred_element_type=F32)
        x2_ref[rows(u), :] = x2
        h3 = _rms_scale(x2, gf_ref[...], NORM_EPS)
        _store_row_tiles(hp_ref.at[pl.ds(u * XA_SUB * HP_TILES, XA_SUB * HP_TILES)], _pack_halves(h3))
        logits = jnp.dot(h3.astype(BF16), wr_ref[...], preferred_element_type=F32) + br_ref[...]
        gates, eids = _route(logits)
        gate_ref[rows(u), :] = gates
        eid_ref[rows(u), :] = eids

    pending = scores(0)
    for u in range(nsub):
        sc = pending
        if u + 1 < nsub:
            pending = scores(u + 1)
        finish(u, sc)


def _xattn(x1, kv, gx, wq, wo, gf, wr, br):
    t = x1.shape[0]
    tm = XA_TM
    per_batch = SEQ // tm
    const = lambda shape: pl.BlockSpec(shape, lambda i: (0, 0))
    tile = lambda n: pl.BlockSpec((tm, n), lambda i: (i, 0))
    return pl.pallas_call(
        _xattn_kernel,
        grid=(t // tm,),
        in_specs=[tile(D_MODEL),
                  pl.BlockSpec((MEM_LEN, 2 * XA_WIDTH), lambda i: (i // per_batch, 0)),
                  const((1, D_MODEL)), const((D_MODEL, XA_WIDTH)), const((XA_WIDTH, D_MODEL)),
                  const((1, D_MODEL)), const((D_MODEL, LANES)), const((1, LANES))],
        out_specs=[tile(D_MODEL), pl.BlockSpec((tm * HP_TILES, LANES), lambda i: (i, 0)), tile(LANES), tile(LANES)],
        out_shape=[jax.ShapeDtypeStruct((t, D_MODEL), F32),
                   jax.ShapeDtypeStruct((t * HP_TILES, LANES), U32),
                   jax.ShapeDtypeStruct((t, LANES), F32),
                   jax.ShapeDtypeStruct((t, LANES), I32)],
        compiler_params=_cparams(("parallel",)),
        name="xattn_router",
    )(x1, kv, gx, wq, wo, gf, wr, br)


IDX_ROWS = 8
PIECE = 8


def _rank_kernel(eid_ref, gate_ref, idx_ref, gidx_ref, tile_ref, cnt_ref, carry_ref):
    @pl.when(pl.program_id(0) == 0)
    def _():
        carry_ref[...] = jnp.zeros_like(carry_ref)

    tm = eid_ref.shape[0]
    eid = eid_ref[...]
    lane = lax.broadcasted_iota(I32, (tm, LANES), 1)
    e1 = eid[:, 0:1]
    e2 = eid[:, 1:2]
    m1 = lane == e1
    m2 = lane == e2
    onehot = jnp.where(m1, 1.0, jnp.where(m2, 1.0, 0.0))
    r = lax.broadcasted_iota(I32, (tm, tm), 0)
    c = lax.broadcasted_iota(I32, (tm, tm), 1)
    tri = jnp.where(c < r, 1.0, 0.0).astype(BF16)
    local = jnp.dot(tri, onehot.astype(BF16), preferred_element_type=F32)
    carry = carry_ref[0:1, :]
    n_tile = jnp.sum(onehot, axis=0, keepdims=True)

    n_piece = jnp.floor((n_tile + (PIECE - 1)) * (1.0 / PIECE)) * PIECE

    def assignment(m, e):
        r_local = jnp.sum(jnp.where(m, local, 0.0), axis=-1, keepdims=True)
        r_global = r_local + jnp.sum(jnp.where(m, carry, 0.0), axis=-1, keepdims=True)
        first = jnp.sum(jnp.where(lane < e, n_tile, 0.0), axis=-1, keepdims=True)
        first_piece = jnp.sum(jnp.where(lane < e, n_piece, 0.0), axis=-1, keepdims=True)
        return first + r_local, r_global, first_piece + r_local

    p1, g1, q1 = assignment(m1, e1)
    p2, g2, q2 = assignment(m2, e2)
    cols = [p1, p2, g1, g2, e1.astype(F32), e2.astype(F32), q1, q2]
    table = jnp.zeros((tm, LANES), F32)
    for j, col in enumerate(cols):
        table = jnp.where(lane == j, col, table)
    idx_ref[0] = jnp.transpose(table)[0:IDX_ROWS, :].astype(I32)
    gidx_ref[0] = jnp.transpose(gate_ref[...])[0:IDX_ROWS, :]
    tile_ref[0] = jnp.where(lax.broadcasted_iota(I32, (8, LANES), 0) == 0, carry, n_tile)
    carry_ref[...] = carry_ref[...] + n_tile
    cnt_ref[...] = carry_ref[...]


def _rank(eid, gates):
    t = eid.shape[0]
    tm = RANK_TM
    n_tiles = t // tm
    return pl.pallas_call(
        _rank_kernel,
        grid=(n_tiles,),
        in_specs=[pl.BlockSpec((tm, LANES), lambda i: (i, 0)), pl.BlockSpec((tm, LANES), lambda i: (i, 0))],
        out_specs=[pl.BlockSpec((1, IDX_ROWS, tm), lambda i: (i, 0, 0)),
                   pl.BlockSpec((1, IDX_ROWS, tm), lambda i: (i, 0, 0)),
                   pl.BlockSpec((1, 8, LANES), lambda i: (i, 0, 0)),
                   pl.BlockSpec((8, LANES), lambda i: (0, 0))],
        out_shape=[jax.ShapeDtypeStruct((n_tiles, IDX_ROWS, tm), I32),
                   jax.ShapeDtypeStruct((n_tiles, IDX_ROWS, tm), F32),
                   jax.ShapeDtypeStruct((n_tiles, 8, LANES), F32),
                   jax.ShapeDtypeStruct((8, LANES), F32)],
        scratch_shapes=[pltpu.VMEM((8, LANES), F32)],
        compiler_params=_cparams(("arbitrary",)),
        name="expert_rank",
    )(eid, gates)


def _dispatch_kernel(n_tiles, cnt_ref, pend_ref, nused_ref, pos_ref, run_ref, run_prev_ref, hp_ref, xs_ref, sbuf,
                     zero_ref, sem, zsem):
    i = pl.program_id(0)
    slot = lax.rem(i, 2)
    tm = hp_ref.shape[0] // HP_TILES
    zb = zero_ref.shape[0]
    n_blocks = xs_ref.shape[0] // zb

    @pl.when(pl.program_id(0) == 0)
    def _():
        zero_ref[...] = jnp.zeros_like(zero_ref)
        for b in range(2):
            sbuf[b, pl.ds(TOP_K * tm * HP_TILES, PIECE * HP_TILES), :] = jnp.zeros((PIECE * HP_TILES, LANES), U32)

        def zero_copy(block):
            return pltpu.make_async_copy(zero_ref, xs_ref.at[pl.ds(pl.multiple_of(block * zb, zb), zb)], zsem)

        def tail_start(j, carry):
            zero_copy(j).start()
            return carry

        def tail_wait(j, carry):
            zero_copy(j).wait()
            return carry

        for e in range(N_EXPERTS):
            @pl.when(cnt_ref[e] > 0)
            def _():
                zero_copy(pend_ref[e] - 1).start()
        lax.fori_loop(nused_ref[0], n_blocks, tail_start, 0)
        for e in range(N_EXPERTS):
            @pl.when(cnt_ref[e] > 0)
            def _():
                zero_copy(pend_ref[e] - 1).wait()
        lax.fori_loop(nused_ref[0], n_blocks, tail_wait, 0)

    def place(t, carry):
        row = hp_ref[pl.ds(pl.multiple_of(t * HP_TILES, HP_TILES), HP_TILES), :]
        for k in range(TOP_K):
            sbuf[slot, pl.ds(pl.multiple_of(pos_ref[0, k, t] * HP_TILES, HP_TILES), HP_TILES), :] = row
        return carry

    lax.fori_loop(0, tm, place, 0, unroll=8)

    def piece_copy(src_tok, dst_slot, buf):
        rows = PIECE * HP_TILES
        return pltpu.make_async_copy(sbuf.at[buf, pl.ds(pl.multiple_of(src_tok * HP_TILES, HP_TILES), rows)],
                                     xs_ref.at[pl.ds(pl.multiple_of(dst_slot * HP_TILES, HP_TILES), rows)], sem.at[buf])

    def drain(runs, buf):
        def wait_piece(j, carry):
            piece_copy(0, 0, buf).wait()
            return carry
        lax.fori_loop(0, runs[0, 0, 3 * N_EXPERTS], wait_piece, 0)

    @pl.when(i > 0)
    def _():
        drain(run_prev_ref, 1 - slot)

    for e in range(N_EXPERTS):
        slot0 = run_ref[0, 0, e]
        first = run_ref[0, 0, 2 * N_EXPERTS + e]

        def run_piece(j, carry):
            piece_copy(first + PIECE * j, slot0 + PIECE * j, slot).start()
            return carry

        lax.fori_loop(0, run_ref[0, 0, N_EXPERTS + e], run_piece, 0)

    @pl.when(i == n_tiles - 1)
    def _():
        drain(run_ref, slot)


def _dispatch(cnt, pend_blocks, nused, pos, runs, hp, n_slots):
    n_tiles, _, tm = pos.shape
    grid_spec = pltpu.PrefetchScalarGridSpec(
        num_scalar_prefetch=3,
        grid=(n_tiles,),
        in_specs=[pl.BlockSpec((1, TOP_K, tm), lambda i, *_: (i, 0, 0), memory_space=pltpu.SMEM),
                  pl.BlockSpec((1, 1, LANES), lambda i, *_: (i, 0, 0), memory_space=pltpu.SMEM),
                  pl.BlockSpec((1, 1, LANES), lambda i, *_: (jnp.maximum(i - 1, 0), 0, 0), memory_space=pltpu.SMEM),
                  pl.BlockSpec((tm * HP_TILES, LANES), lambda i, *_: (i, 0))],
        out_specs=pl.BlockSpec(memory_space=pl.ANY),
        scratch_shapes=[pltpu.VMEM((2, (TOP_K * tm + PIECE) * HP_TILES, LANES), U32),
                        pltpu.VMEM((MOE_TM * HP_TILES, LANES), U32),
                        pltpu.SemaphoreType.DMA((2,)), pltpu.SemaphoreType.DMA(())],
    )
    return pl.pallas_call(
        functools.partial(_dispatch_kernel, n_tiles),
        grid_spec=grid_spec,
        out_shape=jax.ShapeDtypeStruct((n_slots * HP_TILES, LANES), U32),
        compiler_params=_cparams(("arbitrary",)),
        name="moe_dispatch",
    )(cnt, pend_blocks, nused, pos, runs, runs, hp)


MOE_DOWN_CHUNKS = 4


def _moe_kernel(bexp_ref, nused_ref, xs_ref, wg_ref, wu_ref, wd_ref, o_ref, wgb, wub, wdb):
    i = pl.program_id(0)
    nused = nused_ref[0]
    tm = xs_ref.shape[0] // HP_TILES

    @pl.when((i < nused) & ((i == 0) | (bexp_ref[i] != bexp_ref[jnp.maximum(i - 1, 0)])))
    def _():
        wgb[...] = wg_ref[0].astype(BF16)
        wub[...] = wu_ref[0].astype(BF16)
        wdb[...] = wd_ref[0].astype(BF16)

    @pl.when(i < nused)
    def _():
        x = _unpack_halves(_load_row_tiles(xs_ref, tm, HP_TILES))
        hg = jnp.dot(x, wgb[...], preferred_element_type=F32)
        hu = jnp.dot(x, wub[...], preferred_element_type=F32)
        hdn = ((hg * jax.nn.sigmoid(hg)) * hu).astype(BF16)
        cw = D_MODEL // MOE_DOWN_CHUNKS
        for c in range(MOE_DOWN_CHUNKS):
            y = jnp.dot(hdn, wdb[:, c * cw:(c + 1) * cw], preferred_element_type=F32)
            for j in range(cw // LANES):
                o_ref[pl.ds(c * (cw // LANES) + j, tm, stride=YS_TILES), :] = y[:, j * LANES:(j + 1) * LANES]

    @pl.when(i >= nused)
    def _():
        o_ref[...] = jnp.zeros_like(o_ref)


def _moe(bexp, nused, xs, wg, wu, wd):
    tm = MOE_TM
    n_slots = xs.shape[0] // HP_TILES
    n_blocks = n_slots // tm
    grid_spec = pltpu.PrefetchScalarGridSpec(
        num_scalar_prefetch=2,
        grid=(n_blocks,),
        in_specs=[pl.BlockSpec((tm * HP_TILES, LANES), lambda i, be, nu: (jnp.minimum(i, nu[0] - 1), 0)),
                  pl.BlockSpec((1, D_MODEL, D_EXPERT), lambda i, be, nu: (be[i], 0, 0)),
                  pl.BlockSpec((1, D_MODEL, D_EXPERT), lambda i, be, nu: (be[i], 0, 0)),
                  pl.BlockSpec((1, D_EXPERT, D_MODEL), lambda i, be, nu: (be[i], 0, 0))],
        out_specs=pl.BlockSpec((tm * YS_TILES, LANES), lambda i, be, nu: (i, 0)),
        scratch_shapes=[pltpu.VMEM((D_MODEL, D_EXPERT), BF16), pltpu.VMEM((D_MODEL, D_EXPERT), BF16),
                        pltpu.VMEM((D_EXPERT, D_MODEL), BF16)],
    )
    return pl.pallas_call(
        _moe_kernel,
        grid_spec=grid_spec,
        out_shape=jax.ShapeDtypeStruct((n_slots * YS_TILES, LANES), F32),
        compiler_params=_cparams(("arbitrary",)),
        name="moe_experts",
    )(bexp, nused, xs, wg, wu, wd)


def _combine_kernel(n_tiles, pos_ref, run_ref, run_next_ref, gate_ref, x_ref, g_ref, ys_ref, o_ref, ybuf, mbuf, sem):
    i = pl.program_id(0)
    tm = x_ref.shape[0]
    slot = lax.rem(i, 2)

    def piece_copy(src_slot, dst_pos, buf):
        rows = PIECE * YS_TILES
        return pltpu.make_async_copy(ys_ref.at[pl.ds(pl.multiple_of(src_slot * YS_TILES, YS_TILES), rows)],
                                     ybuf.at[buf, pl.ds(pl.multiple_of(dst_pos * YS_TILES, YS_TILES), rows)],
                                     sem.at[buf])

    def request(runs, buf):
        for e in range(N_EXPERTS):
            slot0 = runs[0, 0, e]
            first = runs[0, 0, 2 * N_EXPERTS + e]

            def run_piece(j, carry):
                piece_copy(slot0 + PIECE * j, first + PIECE * j, buf).start()
                return carry

            lax.fori_loop(0, runs[0, 0, N_EXPERTS + e], run_piece, 0)

    @pl.when(i == 0)
    def _():
        request(run_ref, 0)

    @pl.when(i + 1 < n_tiles)
    def _():
        request(run_next_ref, 1 - slot)

    def drain(j, carry):
        piece_copy(0, 0, slot).wait()
        return carry

    lax.fori_loop(0, run_ref[0, 0, 3 * N_EXPERTS], drain, 0)

    def token(t, carry):
        acc = None
        for k in range(TOP_K):
            rows = pl.ds(pl.multiple_of(pos_ref[0, k, t] * YS_TILES, YS_TILES), YS_TILES)
            term = ybuf[slot, rows, :] * gate_ref[0, k, t]
            acc = term if acc is None else acc + term
        mbuf[pl.ds(pl.multiple_of(t * YS_TILES, YS_TILES), YS_TILES), :] = acc
        return carry

    lax.fori_loop(0, tm, token, 0, unroll=8)
    moe = _load_row_tiles(mbuf, tm, YS_TILES)
    o_ref[...] = _rms_scale(x_ref[...] + moe, g_ref[...], NORM_EPS)


def _combine(pos, runs, gidx, x2, g, ys, row0, n_rows):
    _, _, tm = pos.shape
    tile0 = row0 // tm
    n = n_rows // tm
    smem = lambda rows: pl.BlockSpec((1, rows, tm), lambda i: (tile0 + i, 0, 0), memory_space=pltpu.SMEM)
    run_spec = lambda ahead: pl.BlockSpec((1, 1, LANES), lambda i: (tile0 + jnp.minimum(i + ahead, n - 1), 0, 0),
                                          memory_space=pltpu.SMEM)
    return pl.pallas_call(
        functools.partial(_combine_kernel, n),
        grid=(n,),
        in_specs=[smem(TOP_K), run_spec(0), run_spec(1), smem(TOP_K),
                  pl.BlockSpec((tm, D_MODEL), lambda i: (tile0 + i, 0)),
                  pl.BlockSpec((1, D_MODEL), lambda i: (0, 0)),
                  pl.BlockSpec(memory_space=pl.ANY)],
        out_specs=pl.BlockSpec((tm, D_MODEL), lambda i: (i, 0)),
        out_shape=jax.ShapeDtypeStruct((n_rows, D_MODEL), F32),
        scratch_shapes=[pltpu.VMEM((2, (TOP_K * tm + N_EXPERTS * PIECE) * YS_TILES, LANES), F32),
                        pltpu.VMEM((tm * YS_TILES, LANES), F32),
                        pltpu.SemaphoreType.DMA((2,))],
        compiler_params=_cparams(("arbitrary",)),
        name="moe_combine",
    )(pos, runs, runs, gidx[:, 0:TOP_K, :], x2, g, ys)


def _routing_plan(idx, tiles, counts):
    cnt = counts[0, :N_EXPERTS].astype(I32)
    padded = jnp.where(cnt > 0, (cnt + PIECE - 1 + MOE_TM - 1) // MOE_TM * MOE_TM, 0)
    pends = jnp.cumsum(padded)
    pstarts = pends - padded
    n_tiles, _, tm = idx.shape
    n_blocks = n_tiles * tm * TOP_K // MOE_TM + N_EXPERTS + 1
    blk0 = jnp.arange(n_blocks, dtype=I32) * MOE_TM
    bexp = jnp.minimum(jnp.sum(pends[None, :] <= blk0[:, None], axis=1), N_EXPERTS - 1).astype(I32)
    nused = (pends[-1:] // MOE_TM).astype(I32)
    before = tiles[:, 0, :N_EXPERTS].astype(I32)
    inside = tiles[:, 1, :N_EXPERTS].astype(I32)
    first = jnp.cumsum(inside, axis=1) - inside
    pieces = (inside + PIECE - 1) // PIECE
    first_piece = (jnp.cumsum(pieces, axis=1) - pieces) * PIECE

    def run_table(first_pos):
        return jnp.concatenate([pstarts[None, :] + before, pieces, first_pos, jnp.sum(pieces, axis=1, keepdims=True),
                                jnp.zeros((n_tiles, LANES - 3 * N_EXPERTS - 1), I32)], axis=1).reshape(n_tiles, 1, LANES)

    pos = idx[:, 0:TOP_K, :]
    pos_piece = idx[:, 3 * TOP_K:4 * TOP_K, :]
    return pos, run_table(first), pos_piece, run_table(first_piece), cnt, (pends // MOE_TM).astype(I32), bexp, nused, n_blocks * MOE_TM


def kernel(x_prompt, x_sample, mem_prompt, mem_sample, mix_norm_g, w_in, na_rpb, diff_lambda_q1, diff_lambda_k1, diff_lambda_q2, diff_lambda_k2, diff_subln_g, w_branch_na, w_branch_diff, w_out, xa_norm_g, mem_norm_g, xa_w_q, xa_w_kv, xa_w_o, ffn_norm_g, router_group_w, router_group_b, router_expert_w, router_expert_b, w_gate, w_up, w_down, final_norm_g):
    nb_p, nb_s = x_prompt.shape[0], x_sample.shape[0]
    nb = nb_p + nb_s
    t = nb * SEQ
    xp = x_prompt.reshape(nb_p * SEQ, D_MODEL)
    xs_in = x_sample.reshape(nb_s * SEQ, D_MODEL)
    mem_p = mem_prompt.reshape(nb_p * MEM_LEN, D_MODEL)
    mem_s = mem_sample.reshape(nb_s * MEM_LEN, D_MODEL)
    row = lambda v: v.reshape(1, -1).astype(F32)
    bf = lambda w: w.astype(BF16)

    qcol = 3 * NA_WIDTH
    colscale = jnp.ones((1, IN_COLS), F32).at[:, qcol:qcol + DIFF_QK_WIDTH].set(DIFF_QK_DIM ** -0.5 * LOG2E)
    proj = _inproj(xp, xs_in, row(mix_norm_g[0]), bf(w_in[0]), colscale)
    proj3 = proj.reshape(nb, SEQ, IN_COLS)
    a = _na_attention(proj3, _na_bias_blocks(na_rpb[0]))
    b = _diff_attention(proj3, row(diff_lambda_q1[0]), row(diff_lambda_k1[0]),
                        row(diff_lambda_q2[0]), row(diff_lambda_k2[0]), row(diff_subln_g[0]))
    x1 = _merge(a.reshape(t, NA_WIDTH), b.reshape(t, DIFF_V_WIDTH), proj, xp, xs_in,
                bf(w_branch_na[0]), bf(w_branch_diff[0]), bf(w_out[0]))
    kv = _memkv(mem_p, mem_s, row(mem_norm_g[0]), bf(xa_w_kv[0]))
    w_router = jnp.zeros((D_MODEL, LANES), F32)
    w_router = w_router.at[:, :N_GROUPS].set(router_group_w[0]).at[:, N_GROUPS:N_GROUPS + N_EXPERTS].set(router_expert_w[0])
    b_router = jnp.zeros((1, LANES), F32)
    b_router = b_router.at[0, :N_GROUPS].set(router_group_b[0]).at[0, N_GROUPS:N_GROUPS + N_EXPERTS].set(router_expert_b[0])
    x2, hp, gates, eid = _xattn(x1, kv, row(xa_norm_g[0]), bf(xa_w_q[0]), bf(xa_w_o[0]),
                                row(ffn_norm_g[0]), bf(w_router), b_router)
    idx, gidx, tiles, counts = _rank(eid, gates)
    pos, runs, pos_c, runs_c, cnt, pend_blocks, bexp, nused, n_slots = _routing_plan(idx, tiles, counts)
    xs = _dispatch(cnt, pend_blocks, nused, pos, runs, hp, n_slots)
    ys = _moe(bexp, nused, xs, w_gate[0], w_up[0], w_down[0])
    fg = row(final_norm_g)
    y_p = _combine(pos_c, runs_c, gidx, x2, fg, ys, 0, nb_p * SEQ).reshape(nb_p, SEQ, D_MODEL)
    y_s = _combine(pos_c, runs_c, gidx, x2, fg, ys, nb_p * SEQ, nb_s * SEQ).reshape(nb_s, SEQ, D_MODEL)
    return (y_p, y_s)
```

```python
import functools
import math

import jax
import jax.numpy as jnp
from jax import lax
from jax.experimental import pallas as pl
from jax.experimental.pallas import tpu as pltpu

F32 = jnp.float32
BF16 = jnp.bfloat16
I32 = jnp.int32
U32 = jnp.uint32

D_MODEL = 2048
SEQ = 2048
GRID_W = 64
ROWS = SEQ // GRID_W
NA_HEADS = 8
NA_HEAD_DIM = 128
NA_WIDTH = NA_HEADS * NA_HEAD_DIM
NA_KH = 8
NA_KW = 16
DIFF_HEADS = 8
DIFF_QK_DIM = 64
DIFF_V_DIM = 128
DIFF_QK_WIDTH = DIFF_HEADS * 2 * DIFF_QK_DIM
DIFF_V_WIDTH = DIFF_HEADS * DIFF_V_DIM
IN_COLS = 3 * NA_WIDTH + 2 * DIFF_QK_WIDTH + DIFF_V_WIDTH + 2 * D_MODEL
MEM_LEN = 256
XA_HEADS = 4
XA_HEAD_DIM = 128
XA_WIDTH = XA_HEADS * XA_HEAD_DIM
N_GROUPS = 4
EXPERTS_PER_GROUP = 8
N_EXPERTS = N_GROUPS * EXPERTS_PER_GROUP
TOP_K = 2
D_EXPERT = 512
NORM_EPS = 1e-6
SUBLN_EPS = 1e-5
NEG_INF = -1e30
LAMBDA_INIT = 0.8 - 0.6 * math.exp(-0.3 * 0)
LOG2E = math.log2(math.e)

LANES = 128
HP_TILES = D_MODEL // 2 // LANES
YS_TILES = D_MODEL // LANES
VMEM_LIMIT = 56 * 1024 * 1024

INPROJ_TM = 1024
INPROJ_TN = 1024
NORM_CHUNK = 128
DIFF_TQ = 2048
DIFF_SUB = 128
MERGE_TM = 256
XA_TM = 512
XA_SUB = 256
RANK_TM = 512
MOE_TM = 512
ROW_TM = 256


def _cparams(sem):
    return pltpu.CompilerParams(dimension_semantics=sem, vmem_limit_bytes=VMEM_LIMIT)


def _rms_scale(x, g, eps):
    ms = jnp.mean(x * x, axis=-1, keepdims=True)
    return x * lax.rsqrt(ms + eps) * g


def _dot_nt(a, b):
    return lax.dot_general(a, b, (((1,), (1,)), ((), ())), preferred_element_type=F32)


def _two_group_specs(tm, width, n_first):
    first = pl.BlockSpec((tm, width), lambda i, *_: (jnp.minimum(i, n_first - 1), 0))
    second = pl.BlockSpec((tm, width), lambda i, *_: (jnp.maximum(i - n_first, 0), 0))
    return first, second


def _inproj_kernel(n_first, xp_ref, xs_ref, g_ref, w_ref, cs_ref, o_ref, h_ref):
    def normalise(x_ref):
        def chunk(c, carry):
            rows = pl.ds(pl.multiple_of(c * NORM_CHUNK, NORM_CHUNK), NORM_CHUNK)
            h_ref[rows, :] = _rms_scale(x_ref[rows, :], g_ref[...], NORM_EPS).astype(BF16)
            return carry
        lax.fori_loop(0, x_ref.shape[0] // NORM_CHUNK, chunk, 0)

    first_col = pl.program_id(1) == 0
    in_first = pl.program_id(0) < n_first
    pl.when(first_col & in_first)(lambda: normalise(xp_ref))
    pl.when(first_col & jnp.logical_not(in_first))(lambda: normalise(xs_ref))
    acc = jnp.dot(h_ref[...], w_ref[...], preferred_element_type=F32)
    o_ref[...] = (acc * cs_ref[...]).astype(o_ref.dtype)


def _inproj(xp, xs, g, w, colscale):
    d = xp.shape[1]
    t = xp.shape[0] + xs.shape[0]
    n = w.shape[1]
    tm = INPROJ_TM
    n_first = xp.shape[0] // tm
    spec_p, spec_s = _two_group_specs(tm, d, n_first)
    return pl.pallas_call(
        functools.partial(_inproj_kernel, n_first),
        grid=(t // tm, n // INPROJ_TN),
        in_specs=[spec_p, spec_s,
                  pl.BlockSpec((1, d), lambda i, j: (0, 0)),
                  pl.BlockSpec((d, INPROJ_TN), lambda i, j: (0, j)),
                  pl.BlockSpec((1, INPROJ_TN), lambda i, j: (0, j))],
        out_specs=pl.BlockSpec((tm, INPROJ_TN), lambda i, j: (i, j)),
        out_shape=jax.ShapeDtypeStruct((t, n), BF16),
        scratch_shapes=[pltpu.VMEM((tm, d), BF16)],
        compiler_params=_cparams(("parallel", "arbitrary")),
        name="inproj",
    )(xp, xs, g, w, colscale)


NA_QROWS = 4
NA_KROWS = 12
NA_TILES = ROWS // NA_QROWS


def _na_tile_key_row(ti):
    return min(max(NA_QROWS * ti - NA_KH // 2, 0), ROWS - NA_KROWS)


def _na_bias_blocks(rpb):
    cols = jnp.arange(GRID_W)
    cstart = jnp.clip(cols - NA_KW // 2, 0, GRID_W - NA_KW)
    col_ok = (cols[None, :] >= cstart[:, None]) & (cols[None, :] < cstart[:, None] + NA_KW)
    dc = jnp.clip(cols[None, :] - cols[:, None], 1 - NA_KW, NA_KW - 1) + NA_KW - 1
    pick = (dc[None] == jnp.arange(2 * NA_KW - 1)[:, None, None]).astype(F32)
    t = jnp.einsum("hdx,xck->hdck", rpb.astype(F32), pick, precision=lax.Precision.HIGHEST)
    return jnp.where(col_ok[None, None], t, NEG_INF)


def _na_kernel(q_ref, k_ref, v_ref, t_ref, o_ref, bias_ref):
    scale = NA_HEAD_DIM ** -0.5
    tq = NA_QROWS * GRID_W
    tk = NA_KROWS * GRID_W

    @pl.when(pl.program_id(1) == 0)
    def _():
        masked = jnp.full((GRID_W, GRID_W), NEG_INF, F32)
        for variant, ti in enumerate((0, 1, NA_TILES - 1)):
            for jr in range(NA_QROWS):
                r = NA_QROWS * ti + jr
                rs = min(max(r - NA_KH // 2, 0), ROWS - NA_KH)
                blocks = []
                for i in range(NA_KROWS):
                    kr = _na_tile_key_row(ti) + i
                    blocks.append(t_ref[0, kr - r + NA_KH - 1] if rs <= kr < rs + NA_KH else masked)
                bias_ref[variant, jr * GRID_W:(jr + 1) * GRID_W, :] = jnp.concatenate(blocks, axis=1) * LOG2E

    def key_rows(ti):
        k0 = _na_tile_key_row(ti) * GRID_W
        return slice(k0, k0 + tk)

    def scores(ti):
        return _dot_nt(q_ref[0, ti * tq:(ti + 1) * tq, :], k_ref[0, key_rows(ti), :])

    pending = scores(0)
    for ti in range(NA_TILES):
        variant = 0 if ti == 0 else (2 if ti == NA_TILES - 1 else 1)
        qrows = slice(ti * tq, (ti + 1) * tq)
        krows = key_rows(ti)
        s = pending * (scale * LOG2E) + bias_ref[variant]
        if ti + 1 < NA_TILES:
            pending = scores(ti + 1)
        m = jnp.max(s, axis=-1, keepdims=True)
        e = jnp.exp2(s - m)
        l = jnp.sum(e, axis=-1, keepdims=True)
        o = jnp.dot(e.astype(BF16), v_ref[0, krows, :], preferred_element_type=F32)
        o_ref[0, qrows, :] = (o / l).astype(o_ref.dtype)


def _na_attention(proj3, blocks):
    b = proj3.shape[0]
    hd = NA_HEAD_DIM
    return pl.pallas_call(
        _na_kernel,
        grid=(NA_HEADS, b),
        in_specs=[pl.BlockSpec((1, SEQ, hd), lambda h, i: (i, 0, h)),
                  pl.BlockSpec((1, SEQ, hd), lambda h, i: (i, 0, NA_HEADS + h)),
                  pl.BlockSpec((1, SEQ, hd), lambda h, i: (i, 0, 2 * NA_HEADS + h)),
                  pl.BlockSpec((1,) + blocks.shape[1:], lambda h, i: (h, 0, 0, 0))],
        out_specs=pl.BlockSpec((1, SEQ, hd), lambda h, i: (i, 0, h)),
        out_shape=jax.ShapeDtypeStruct((b, SEQ, NA_WIDTH), BF16),
        scratch_shapes=[pltpu.VMEM((3, NA_QROWS * GRID_W, NA_KROWS * GRID_W), F32)],
        compiler_params=_cparams(("arbitrary", "arbitrary")),
        name="na_attn",
    )(proj3, proj3, proj3, blocks)


def _alibi_table(tq):
    slopes = jnp.exp2(-8.0 * jnp.arange(1, DIFF_HEADS + 1, dtype=F32) / DIFF_HEADS)
    r = jnp.arange(tq)[:, None]
    x = jnp.arange(2 * SEQ - tq)[None, :]
    dist = jnp.abs(r - x + (SEQ - tq)).astype(F32)
    return (slopes * LOG2E)[:, None, None] * dist[None]


def _diff_kernel(lq1_ref, lk1_ref, lq2_ref, lk2_ref, q_ref, k_ref, v_ref, g_ref, alibi_ref, o_ref):
    qi = pl.program_id(2)
    ts = alibi_ref.shape[1]
    nsub = q_ref.shape[1] // ts
    lam = (jnp.exp(jnp.sum(lq1_ref[...] * lk1_ref[...], axis=-1, keepdims=True))
           - jnp.exp(jnp.sum(lq2_ref[...] * lk2_ref[...], axis=-1, keepdims=True)) + LAMBDA_INIT)
    k = k_ref[0]

    def scores(u):
        q = q_ref[0, u * ts:(u + 1) * ts, :]
        return [_dot_nt(q[:, mi * DIFF_QK_DIM:(mi + 1) * DIFF_QK_DIM], k[:, mi * DIFF_QK_DIM:(mi + 1) * DIFF_QK_DIM])
                for mi in range(2)]

    def expo(s, bias):
        s = s - bias
        m = jnp.max(s, axis=-1, keepdims=True)
        e = jnp.exp2(s - m)
        return e, jnp.sum(e, axis=-1, keepdims=True)

    pending = scores(0)
    for u in range(nsub):
        s0, s1 = pending
        if u + 1 < nsub:
            pending = scores(u + 1)
        off = (SEQ - ts) - (qi * nsub + u) * ts
        bias = alibi_ref[0, :, pl.ds(pl.multiple_of(off, ts), SEQ)]
        e0, l0 = expo(s0, bias)
        e1, l1 = expo(s1, bias)
        a = (e0 - e1 * (lam * l0 / l1)).astype(BF16)
        o = jnp.dot(a, v_ref[0], preferred_element_type=F32) / l0
        o = _rms_scale(o, g_ref[...], SUBLN_EPS) * (1.0 - LAMBDA_INIT)
        o_ref[0, u * ts:(u + 1) * ts, :] = o.astype(o_ref.dtype)


def _diff_attention(proj3, lq1, lk1, lq2, lk2, subln_g):
    b = proj3.shape[0]
    tq = DIFF_TQ
    qoff = 3 * NA_WIDTH // LANES
    koff = qoff + DIFF_QK_WIDTH // LANES
    voff = koff + DIFF_QK_WIDTH // LANES
    vec = lambda n: pl.BlockSpec((1, n), lambda h, i, j: (0, 0))
    return pl.pallas_call(
        _diff_kernel,
        grid=(DIFF_HEADS, b, SEQ // tq),
        in_specs=[vec(DIFF_QK_DIM), vec(DIFF_QK_DIM), vec(DIFF_QK_DIM), vec(DIFF_QK_DIM),
                  pl.BlockSpec((1, tq, LANES), lambda h, i, j: (i, j, qoff + h)),
                  pl.BlockSpec((1, SEQ, LANES), lambda h, i, j: (i, 0, koff + h)),
                  pl.BlockSpec((1, SEQ, LANES), lambda h, i, j: (i, 0, voff + h)),
                  vec(DIFF_V_DIM),
                  pl.BlockSpec((1, DIFF_SUB, 2 * SEQ - DIFF_SUB), lambda h, i, j: (h, 0, 0))],
        out_specs=pl.BlockSpec((1, tq, DIFF_V_DIM), lambda h, i, j: (i, j, h)),
        out_shape=jax.ShapeDtypeStruct((b, SEQ, DIFF_V_WIDTH), BF16),
        compiler_params=_cparams(("arbitrary", "arbitrary", "arbitrary")),
        name="diff_attn",
    )(lq1, lk1, lq2, lk2, proj3, proj3, proj3, subln_g, _alibi_table(DIFF_SUB))


def _merge_kernel(n_first, a_ref, b_ref, gna_ref, gdf_ref, xp_ref, xs_ref, wna_ref, wdf_ref, wout_ref, o_ref):
    pa = jnp.dot(a_ref[...], wna_ref[...], preferred_element_type=F32)
    pb = jnp.dot(b_ref[...], wdf_ref[...], preferred_element_type=F32)
    merged = (jax.nn.sigmoid(gna_ref[...].astype(F32)) * pa
              + jax.nn.sigmoid(gdf_ref[...].astype(F32)) * pb)
    delta = jnp.dot(merged.astype(BF16), wout_ref[...], preferred_element_type=F32)
    in_first = pl.program_id(0) < n_first

    @pl.when(in_first)
    def _():
        o_ref[...] = xp_ref[...] + delta

    @pl.when(jnp.logical_not(in_first))
    def _():
        o_ref[...] = xs_ref[...] + delta


def _merge(a, b, proj, xp, xs, wna, wdf, wout):
    t = xp.shape[0] + xs.shape[0]
    tm = MERGE_TM
    gna_blk = (3 * NA_WIDTH + 2 * DIFF_QK_WIDTH + DIFF_V_WIDTH) // D_MODEL
    const = lambda shape: pl.BlockSpec(shape, lambda i: (0, 0))
    n_first = xp.shape[0] // tm
    spec_p, spec_s = _two_group_specs(tm, D_MODEL, n_first)
    return pl.pallas_call(
        functools.partial(_merge_kernel, n_first),
        grid=(t // tm,),
        in_specs=[pl.BlockSpec((tm, NA_WIDTH), lambda i: (i, 0)),
                  pl.BlockSpec((tm, DIFF_V_WIDTH), lambda i: (i, 0)),
                  pl.BlockSpec((tm, D_MODEL), lambda i: (i, gna_blk)),
                  pl.BlockSpec((tm, D_MODEL), lambda i: (i, gna_blk + 1)),
                  spec_p, spec_s,
                  const((NA_WIDTH, D_MODEL)), const((DIFF_V_WIDTH, D_MODEL)), const((D_MODEL, D_MODEL))],
        out_specs=pl.BlockSpec((tm, D_MODEL), lambda i: (i, 0)),
        out_shape=jax.ShapeDtypeStruct((t, D_MODEL), F32),
        compiler_params=_cparams(("parallel",)),
        name="merge_outproj",
    )(a, b, proj, proj, xp, xs, wna, wdf, wout)


def _memkv_kernel(n_first, mp_ref, ms_ref, g_ref, w_ref, o_ref):
    def project(m_ref):
        h = _rms_scale(m_ref[...], g_ref[...], NORM_EPS).astype(BF16)
        o_ref[...] = jnp.dot(h, w_ref[...], preferred_element_type=F32).astype(o_ref.dtype)

    in_first = pl.program_id(0) < n_first
    pl.when(in_first)(lambda: project(mp_ref))
    pl.when(jnp.logical_not(in_first))(lambda: project(ms_ref))


def _memkv(mem_p, mem_s, g, w):
    t = mem_p.shape[0] + mem_s.shape[0]
    n_first = mem_p.shape[0] // MEM_LEN
    spec_p, spec_s = _two_group_specs(MEM_LEN, D_MODEL, n_first)
    return pl.pallas_call(
        functools.partial(_memkv_kernel, n_first),
        grid=(t // MEM_LEN,),
        in_specs=[spec_p, spec_s,
                  pl.BlockSpec((1, D_MODEL), lambda i: (0, 0)),
                  pl.BlockSpec((D_MODEL, 2 * XA_WIDTH), lambda i: (0, 0))],
        out_specs=pl.BlockSpec((MEM_LEN, 2 * XA_WIDTH), lambda i: (i, 0)),
        out_shape=jax.ShapeDtypeStruct((t, 2 * XA_WIDTH), BF16),
        compiler_params=_cparams(("parallel",)),
        name="mem_kv",
    )(mem_p, mem_s, g, w)


def _route(logits):
    lane = lax.broadcasted_iota(I32, logits.shape, 1).astype(F32)
    ninf = -jnp.inf
    big = float(LANES)
    gl = jnp.where(lane < N_GROUPS, logits, ninf)
    gmax = jnp.max(gl, axis=-1, keepdims=True)
    g = jnp.min(jnp.where(gl == gmax, lane, big), axis=-1, keepdims=True)
    pg = 1.0 / jnp.sum(jnp.exp(gl - gmax), axis=-1, keepdims=True)
    lo = N_GROUPS + EXPERTS_PER_GROUP * g
    el = jnp.where((lane >= lo) & (lane < lo + EXPERTS_PER_GROUP), logits, ninf)
    v1 = jnp.max(el, axis=-1, keepdims=True)
    i1 = jnp.min(jnp.where(el == v1, lane, big), axis=-1, keepdims=True)
    el2 = jnp.where(lane == i1, ninf, el)
    v2 = jnp.max(el2, axis=-1, keepdims=True)
    i2 = jnp.min(jnp.where(el2 == v2, lane, big), axis=-1, keepdims=True)
    t = jnp.exp(v2 - v1)
    den = 1.0 + t
    gate1 = pg * (1.0 / den)
    gate2 = pg * (t / den)
    gates = jnp.where(lane == 0.0, gate1, jnp.where(lane == 1.0, gate2, 0.0))
    eids = jnp.where(lane == 0.0, i1 - N_GROUPS, jnp.where(lane == 1.0, i2 - N_GROUPS, 0.0)).astype(I32)
    return gates, eids


def _pack_halves(h):
    half = h.shape[1] // 2
    hi = pltpu.bitcast(h[:, :half].astype(BF16).astype(F32), U32)
    lo = pltpu.bitcast(h[:, half:].astype(BF16).astype(F32), U32)
    return hi | (lo >> 16)


def _unpack_halves(w):
    hi = pltpu.bitcast(w & jnp.uint32(0xFFFF0000), F32).astype(BF16)
    lo = pltpu.bitcast(w << 16, F32).astype(BF16)
    return jnp.concatenate([hi, lo], axis=1)


def _store_row_tiles(ref, x):
    n, width = x.shape
    c = width // LANES
    for j in range(c):
        ref[pl.ds(j, n, stride=c), :] = x[:, j * LANES:(j + 1) * LANES]


def _load_row_tiles(ref, n, c):
    return jnp.concatenate([ref[pl.ds(j, n, stride=c), :] for j in range(c)], axis=1)


def _xattn_kernel(x_ref, kv_ref, gx_ref, wq_ref, wo_ref, gf_ref, wr_ref, br_ref,
                  x2_ref, hp_ref, gate_ref, eid_ref):
    scale = XA_HEAD_DIM ** -0.5
    heads = [slice(h * XA_HEAD_DIM, (h + 1) * XA_HEAD_DIM) for h in range(XA_HEADS)]
    nsub = x_ref.shape[0] // XA_SUB

    def rows(u):
        return slice(u * XA_SUB, (u + 1) * XA_SUB)

    def scores(u):
        hq = _rms_scale(x_ref[rows(u), :], gx_ref[...], NORM_EPS).astype(BF16)
        q = jnp.dot(hq, wq_ref[...], preferred_element_type=F32).astype(BF16)
        return [_dot_nt(q[:, sl], kv_ref[:, sl]) for sl in heads]

    def finish(u, sc):
        outs = []
        for h in range(XA_HEADS):
            vsl = slice(XA_WIDTH + h * XA_HEAD_DIM, XA_WIDTH + (h + 1) * XA_HEAD_DIM)
            s = sc[h] * (scale * LOG2E)
            m = jnp.max(s, axis=-1, keepdims=True)
            e = jnp.exp2(s - m)
            l = jnp.sum(e, axis=-1, keepdims=True)
            o = jnp.dot(e.astype(BF16), kv_ref[:, vsl], preferred_element_type=F32)
            outs.append((o / l).astype(BF16))
        o = jnp.concatenate(outs, axis=1)
        x2 = x_ref[rows(u), :] + jnp.dot(o, wo_ref[...], preferred_element_type=F32)
        x2_ref[rows(u), :] = x2
        h3 = _rms_scale(x2, gf_ref[...], NORM_EPS)
        _store_row_tiles(hp_ref.at[pl.ds(u * XA_SUB * HP_TILES, XA_SUB * HP_TILES)], _pack_halves(h3))
        logits = jnp.dot(h3.astype(BF16), wr_ref[...], preferred_element_type=F32) + br_ref[...]
        gates, eids = _route(logits)
        gate_ref[rows(u), :] = gates
        eid_ref[rows(u), :] = eids

    pending = scores(0)
    for u in range(nsub):
        sc = pending
        if u + 1 < nsub:
            pending = scores(u + 1)
        finish(u, sc)


def _xattn(x1, kv, gx, wq, wo, gf, wr, br):
    t = x1.shape[0]
    tm = XA_TM
    per_batch = SEQ // tm
    const = lambda shape: pl.BlockSpec(shape, lambda i: (0, 0))
    tile = lambda n: pl.BlockSpec((tm, n), lambda i: (i, 0))
    return pl.pallas_call(
        _xattn_kernel,
        grid=(t // tm,),
        in_specs=[tile(D_MODEL),
                  pl.BlockSpec((MEM_LEN, 2 * XA_WIDTH), lambda i: (i // per_batch, 0)),
                  const((1, D_MODEL)), const((D_MODEL, XA_WIDTH)), const((XA_WIDTH, D_MODEL)),
                  const((1, D_MODEL)), const((D_MODEL, LANES)), const((1, LANES))],
        out_specs=[tile(D_MODEL), pl.BlockSpec((tm * HP_TILES, LANES), lambda i: (i, 0)), tile(LANES), tile(LANES)],
        out_shape=[jax.ShapeDtypeStruct((t, D_MODEL), F32),
                   jax.ShapeDtypeStruct((t * HP_TILES, LANES), U32),
                   jax.ShapeDtypeStruct((t, LANES), F32),
                   jax.ShapeDtypeStruct((t, LANES), I32)],
        compiler_params=_cparams(("parallel",)),
        name="xattn_router",
    )(x1, kv, gx, wq, wo, gf, wr, br)


IDX_ROWS = 8
PIECE = 8


def _rank_kernel(eid_ref, gate_ref, idx_ref, gidx_ref, tile_ref, cnt_ref, carry_ref):
    @pl.when(pl.program_id(0) == 0)
    def _():
        carry_ref[...] = jnp.zeros_like(carry_ref)

    tm = eid_ref.shape[0]
    eid = eid_ref[...]
    lane = lax.broadcasted_iota(I32, (tm, LANES), 1)
    e1 = eid[:, 0:1]
    e2 = eid[:, 1:2]
    m1 = lane == e1
    m2 = lane == e2
    onehot = jnp.where(m1, 1.0, jnp.where(m2, 1.0, 0.0))
    r = lax.broadcasted_iota(I32, (tm, tm), 0)
    c = lax.broadcasted_iota(I32, (tm, tm), 1)
    tri = jnp.where(c < r, 1.0, 0.0).astype(BF16)
    local = jnp.dot(tri, onehot.astype(BF16), preferred_element_type=F32)
    carry = carry_ref[0:1, :]
    n_tile = jnp.sum(onehot, axis=0, keepdims=True)

    n_piece = jnp.floor((n_tile + (PIECE - 1)) * (1.0 / PIECE)) * PIECE

    def assignment(m, e):
        r_local = jnp.sum(jnp.where(m, local, 0.0), axis=-1, keepdims=True)
        r_global = r_local + jnp.sum(jnp.where(m, carry, 0.0), axis=-1, keepdims=True)
        first = jnp.sum(jnp.where(lane < e, n_tile, 0.0), axis=-1, keepdims=True)
        first_piece = jnp.sum(jnp.where(lane < e, n_piece, 0.0), axis=-1, keepdims=True)
        return first + r_local, r_global, first_piece + r_local

    p1, g1, q1 = assignment(m1, e1)
    p2, g2, q2 = assignment(m2, e2)
    cols = [p1, p2, g1, g2, e1.astype(F32), e2.astype(F32), q1, q2]
    table = jnp.zeros((tm, LANES), F32)
    for j, col in enumerate(cols):
        table = jnp.where(lane == j, col, table)
    idx_ref[0] = jnp.transpose(table)[0:IDX_ROWS, :].astype(I32)
    gidx_ref[0] = jnp.transpose(gate_ref[...])[0:IDX_ROWS, :]
    tile_ref[0] = jnp.where(lax.broadcasted_iota(I32, (8, LANES), 0) == 0, carry, n_tile)
    carry_ref[...] = carry_ref[...] + n_tile
    cnt_ref[...] = carry_ref[...]


def _rank(eid, gates):
    t = eid.shape[0]
    tm = RANK_TM
    n_tiles = t // tm
    return pl.pallas_call(
        _rank_kernel,
        grid=(n_tiles,),
        in_specs=[pl.BlockSpec((tm, LANES), lambda i: (i, 0)), pl.BlockSpec((tm, LANES), lambda i: (i, 0))],
        out_specs=[pl.BlockSpec((1, IDX_ROWS, tm), lambda i: (i, 0, 0)),
                   pl.BlockSpec((1, IDX_ROWS, tm), lambda i: (i, 0, 0)),
                   pl.BlockSpec((1, 8, LANES), lambda i: (i, 0, 0)),
                   pl.BlockSpec((8, LANES), lambda i: (0, 0))],
        out_shape=[jax.ShapeDtypeStruct((n_tiles, IDX_ROWS, tm), I32),
                   jax.ShapeDtypeStruct((n_tiles, IDX_ROWS, tm), F32),
                   jax.ShapeDtypeStruct((n_tiles, 8, LANES), F32),
                   jax.ShapeDtypeStruct((8, LANES), F32)],
        scratch_shapes=[pltpu.VMEM((8, LANES), F32)],
        compiler_params=_cparams(("arbitrary",)),
        name="expert_rank",
    )(eid, gates)


def _dispatch_kernel(n_steps, cnt_ref, pend_ref, nused_ref, pos_ref, run_ref, run_prev_ref, hp_ref, xs_ref, sbuf,
                     zero_ref, sem, zsem):
    i = pl.program_id(0)
    tm = hp_ref.shape[0] // (2 * HP_TILES)
    zb = zero_ref.shape[0]
    n_blocks = xs_ref.shape[0] // zb

    @pl.when(i == 0)
    def _():
        zero_ref[...] = jnp.zeros_like(zero_ref)
        for b in range(2):
            sbuf[b, pl.ds(TOP_K * tm * HP_TILES, PIECE * HP_TILES), :] = jnp.zeros((PIECE * HP_TILES, LANES), U32)

        def zero_copy(block):
            return pltpu.make_async_copy(zero_ref, xs_ref.at[pl.ds(pl.multiple_of(block * zb, zb), zb)], zsem)

        def tail_start(j, carry):
            zero_copy(j).start()
            return carry

        def tail_wait(j, carry):
            zero_copy(j).wait()
            return carry

        for e in range(N_EXPERTS):
            @pl.when(cnt_ref[e] > 0)
            def _():
                zero_copy(pend_ref[e] - 1).start()
        lax.fori_loop(nused_ref[0], n_blocks, tail_start, 0)
        for e in range(N_EXPERTS):
            @pl.when(cnt_ref[e] > 0)
            def _():
                zero_copy(pend_ref[e] - 1).wait()
        lax.fori_loop(nused_ref[0], n_blocks, tail_wait, 0)

    def place(half):
        def body(t, carry):
            row = hp_ref[pl.ds(pl.multiple_of((half * tm + t) * HP_TILES, HP_TILES), HP_TILES), :]
            for k in range(TOP_K):
                sbuf[half, pl.ds(pl.multiple_of(pos_ref[half, k, t] * HP_TILES, HP_TILES), HP_TILES), :] = row
            return carry
        lax.fori_loop(0, tm, body, 0, unroll=8)

    def piece_copy(src_tok, dst_slot, half):
        rows = PIECE * HP_TILES
        return pltpu.make_async_copy(sbuf.at[half, pl.ds(pl.multiple_of(src_tok * HP_TILES, HP_TILES), rows)],
                                     xs_ref.at[pl.ds(pl.multiple_of(dst_slot * HP_TILES, HP_TILES), rows)],
                                     sem.at[half])

    def issue(half):
        for e in range(N_EXPERTS):
            slot0 = run_ref[half, 0, e]
            first = run_ref[half, 0, 2 * N_EXPERTS + e]

            def run_piece(j, carry):
                piece_copy(first + PIECE * j, slot0 + PIECE * j, half).start()
                return carry

            lax.fori_loop(0, run_ref[half, 0, N_EXPERTS + e], run_piece, 0)

    def drain(runs, half):
        def wait_piece(j, carry):
            piece_copy(0, 0, half).wait()
            return carry
        lax.fori_loop(0, runs[half, 0, 3 * N_EXPERTS], wait_piece, 0)

    place(0)

    @pl.when(i > 0)
    def _():
        drain(run_prev_ref, 1)
    issue(0)
    place(1)
    drain(run_ref, 0)
    issue(1)

    @pl.when(i == n_steps - 1)
    def _():
        drain(run_ref, 1)


def _dispatch(cnt, pend_blocks, nused, pos, runs, hp, n_slots):
    n_tiles, _, tm = pos.shape
    n_steps = n_tiles // 2
    grid_spec = pltpu.PrefetchScalarGridSpec(
        num_scalar_prefetch=3,
        grid=(n_steps,),
        in_specs=[pl.BlockSpec((2, TOP_K, tm), lambda i, *_: (i, 0, 0), memory_space=pltpu.SMEM),
                  pl.BlockSpec((2, 1, LANES), lambda i, *_: (i, 0, 0), memory_space=pltpu.SMEM),
                  pl.BlockSpec((2, 1, LANES), lambda i, *_: (jnp.maximum(i - 1, 0), 0, 0), memory_space=pltpu.SMEM),
                  pl.BlockSpec((2 * tm * HP_TILES, LANES), lambda i, *_: (i, 0))],
        out_specs=pl.BlockSpec(memory_space=pl.ANY),
        scratch_shapes=[pltpu.VMEM((2, (TOP_K * tm + PIECE) * HP_TILES, LANES), U32),
                        pltpu.VMEM((MOE_TM * HP_TILES, LANES), U32),
                        pltpu.SemaphoreType.DMA((2,)), pltpu.SemaphoreType.DMA(())],
    )
    return pl.pallas_call(
        functools.partial(_dispatch_kernel, n_steps),
        grid_spec=grid_spec,
        out_shape=jax.ShapeDtypeStruct((n_slots * HP_TILES, LANES), U32),
        compiler_params=_cparams(("arbitrary",)),
        name="moe_dispatch",
    )(cnt, pend_blocks, nused, pos, runs, runs, hp)


MOE_DOWN_CHUNKS = 4


def _moe_kernel(bexp_ref, nused_ref, xs_ref, wg_ref, wu_ref, wd_ref, o_ref, wgb, wub, wdb):
    i = pl.program_id(0)
    nused = nused_ref[0]
    tm = xs_ref.shape[0] // HP_TILES

    @pl.when((i < nused) & ((i == 0) | (bexp_ref[i] != bexp_ref[jnp.maximum(i - 1, 0)])))
    def _():
        wgb[...] = wg_ref[0].astype(BF16)
        wub[...] = wu_ref[0].astype(BF16)
        wdb[...] = wd_ref[0].astype(BF16)

    @pl.when(i < nused)
    def _():
        x = _unpack_halves(_load_row_tiles(xs_ref, tm, HP_TILES))
        hg = jnp.dot(x, wgb[...], preferred_element_type=F32)
        hu = jnp.dot(x, wub[...], preferred_element_type=F32)
        hdn = ((hg * jax.nn.sigmoid(hg)) * hu).astype(BF16)
        cw = D_MODEL // MOE_DOWN_CHUNKS
        for c in range(MOE_DOWN_CHUNKS):
            y = jnp.dot(hdn, wdb[:, c * cw:(c + 1) * cw], preferred_element_type=F32)
            for j in range(cw // LANES):
                o_ref[pl.ds(c * (cw // LANES) + j, tm, stride=YS_TILES), :] = y[:, j * LANES:(j + 1) * LANES]

    @pl.when(i >= nused)
    def _():
        o_ref[...] = jnp.zeros_like(o_ref)


def _moe(bexp, nused, xs, wg, wu, wd):
    tm = MOE_TM
    n_slots = xs.shape[0] // HP_TILES
    n_blocks = n_slots // tm
    grid_spec = pltpu.PrefetchScalarGridSpec(
        num_scalar_prefetch=2,
        grid=(n_blocks,),
        in_specs=[pl.BlockSpec((tm * HP_TILES, LANES), lambda i, be, nu: (jnp.minimum(i, nu[0] - 1), 0)),
                  pl.BlockSpec((1, D_MODEL, D_EXPERT), lambda i, be, nu: (be[i], 0, 0)),
                  pl.BlockSpec((1, D_MODEL, D_EXPERT), lambda i, be, nu: (be[i], 0, 0)),
                  pl.BlockSpec((1, D_EXPERT, D_MODEL), lambda i, be, nu: (be[i], 0, 0))],
        out_specs=pl.BlockSpec((tm * YS_TILES, LANES), lambda i, be, nu: (i, 0)),
        scratch_shapes=[pltpu.VMEM((D_MODEL, D_EXPERT), BF16), pltpu.VMEM((D_MODEL, D_EXPERT), BF16),
                        pltpu.VMEM((D_EXPERT, D_MODEL), BF16)],
    )
    return pl.pallas_call(
        _moe_kernel,
        grid_spec=grid_spec,
        out_shape=jax.ShapeDtypeStruct((n_slots * YS_TILES, LANES), F32),
        compiler_params=_cparams(("arbitrary",)),
        name="moe_experts",
    )(bexp, nused, xs, wg, wu, wd)


def _combine_kernel(n_tiles, pos_ref, run_ref, run_next_ref, gate_ref, x_ref, g_ref, ys_ref, o_ref, ybuf, mbuf, sem):
    i = pl.program_id(0)
    tm = x_ref.shape[0]
    slot = lax.rem(i, 2)

    def piece_copy(src_slot, dst_pos, buf):
        rows = PIECE * YS_TILES
        return pltpu.make_async_copy(ys_ref.at[pl.ds(pl.multiple_of(src_slot * YS_TILES, YS_TILES), rows)],
                                     ybuf.at[buf, pl.ds(pl.multiple_of(dst_pos * YS_TILES, YS_TILES), rows)],
                                     sem.at[buf])

    def request(runs, buf):
        for e in range(N_EXPERTS):
            slot0 = runs[0, 0, e]
            first = runs[0, 0, 2 * N_EXPERTS + e]

            def run_piece(j, carry):
                piece_copy(slot0 + PIECE * j, first + PIECE * j, buf).start()
                return carry

            lax.fori_loop(0, runs[0, 0, N_EXPERTS + e], run_piece, 0)

    @pl.when(i == 0)
    def _():
        request(run_ref, 0)

    @pl.when(i + 1 < n_tiles)
    def _():
        request(run_next_ref, 1 - slot)

    def drain(j, carry):
        piece_copy(0, 0, slot).wait()
        return carry

    lax.fori_loop(0, run_ref[0, 0, 3 * N_EXPERTS], drain, 0)

    def token(t, carry):
        acc = None
        for k in range(TOP_K):
            rows = pl.ds(pl.multiple_of(pos_ref[0, k, t] * YS_TILES, YS_TILES), YS_TILES)
            term = ybuf[slot, rows, :] * gate_ref[0, k, t]
            acc = term if acc is None else acc + term
        mbuf[pl.ds(pl.multiple_of(t * YS_TILES, YS_TILES), YS_TILES), :] = acc
        return carry

    lax.fori_loop(0, tm, token, 0, unroll=8)
    moe = _load_row_tiles(mbuf, tm, YS_TILES)
    o_ref[...] = _rms_scale(x_ref[...] + moe, g_ref[...], NORM_EPS)


def _combine(pos, runs, gidx, x2, g, ys, row0, n_rows):
    _, _, tm = pos.shape
    tile0 = row0 // tm
    n = n_rows // tm
    smem = lambda rows: pl.BlockSpec((1, rows, tm), lambda i: (tile0 + i, 0, 0), memory_space=pltpu.SMEM)
    run_spec = lambda ahead: pl.BlockSpec((1, 1, LANES), lambda i: (tile0 + jnp.minimum(i + ahead, n - 1), 0, 0),
                                          memory_space=pltpu.SMEM)
    return pl.pallas_call(
        functools.partial(_combine_kernel, n),
        grid=(n,),
        in_specs=[smem(TOP_K), run_spec(0), run_spec(1), smem(TOP_K),
                  pl.BlockSpec((tm, D_MODEL), lambda i: (tile0 + i, 0)),
                  pl.BlockSpec((1, D_MODEL), lambda i: (0, 0)),
                  pl.BlockSpec(memory_space=pl.ANY)],
        out_specs=pl.BlockSpec((tm, D_MODEL), lambda i: (i, 0)),
        out_shape=jax.ShapeDtypeStruct((n_rows, D_MODEL), F32),
        scratch_shapes=[pltpu.VMEM((2, (TOP_K * tm + N_EXPERTS * PIECE) * YS_TILES, LANES), F32),
                        pltpu.VMEM((tm * YS_TILES, LANES), F32),
                        pltpu.SemaphoreType.DMA((2,))],
        compiler_params=_cparams(("arbitrary",)),
        name="moe_combine",
    )(pos, runs, runs, gidx[:, 0:TOP_K, :], x2, g, ys)


def _routing_plan(idx, tiles, counts):
    cnt = counts[0, :N_EXPERTS].astype(I32)
    padded = jnp.where(cnt > 0, (cnt + PIECE - 1 + MOE_TM - 1) // MOE_TM * MOE_TM, 0)
    pends = jnp.cumsum(padded)
    pstarts = pends - padded
    n_tiles, _, tm = idx.shape
    n_blocks = n_tiles * tm * TOP_K // MOE_TM + N_EXPERTS + 1
    blk0 = jnp.arange(n_blocks, dtype=I32) * MOE_TM
    bexp = jnp.minimum(jnp.sum(pends[None, :] <= blk0[:, None], axis=1), N_EXPERTS - 1).astype(I32)
    nused = (pends[-1:] // MOE_TM).astype(I32)
    before = tiles[:, 0, :N_EXPERTS].astype(I32)
    inside = tiles[:, 1, :N_EXPERTS].astype(I32)
    first = jnp.cumsum(inside, axis=1) - inside
    pieces = (inside + PIECE - 1) // PIECE
    first_piece = (jnp.cumsum(pieces, axis=1) - pieces) * PIECE

    def run_table(first_pos):
        return jnp.concatenate([pstarts[None, :] + before, pieces, first_pos, jnp.sum(pieces, axis=1, keepdims=True),
                                jnp.zeros((n_tiles, LANES - 3 * N_EXPERTS - 1), I32)], axis=1).reshape(n_tiles, 1, LANES)

    pos = idx[:, 0:TOP_K, :]
    pos_piece = idx[:, 3 * TOP_K:4 * TOP_K, :]
    return pos, run_table(first), pos_piece, run_table(first_piece), cnt, (pends // MOE_TM).astype(I32), bexp, nused, n_blocks * MOE_TM


def kernel(x_prompt, x_sample, mem_prompt, mem_sample, mix_norm_g, w_in, na_rpb, diff_lambda_q1, diff_lambda_k1, diff_lambda_q2, diff_lambda_k2, diff_subln_g, w_branch_na, w_branch_diff, w_out, xa_norm_g, mem_norm_g, xa_w_q, xa_w_kv, xa_w_o, ffn_norm_g, router_group_w, router_group_b, router_expert_w, router_expert_b, w_gate, w_up, w_down, final_norm_g):
    nb_p, nb_s = x_prompt.shape[0], x_sample.shape[0]
    nb = nb_p + nb_s
    t = nb * SEQ
    xp = x_prompt.reshape(nb_p * SEQ, D_MODEL)
    xs_in = x_sample.reshape(nb_s * SEQ, D_MODEL)
    mem_p = mem_prompt.reshape(nb_p * MEM_LEN, D_MODEL)
    mem_s = mem_sample.reshape(nb_s * MEM_LEN, D_MODEL)
    row = lambda v: v.reshape(1, -1).astype(F32)
    bf = lambda w: w.astype(BF16)

    qcol = 3 * NA_WIDTH
    colscale = jnp.ones((1, IN_COLS), F32).at[:, qcol:qcol + DIFF_QK_WIDTH].set(DIFF_QK_DIM ** -0.5 * LOG2E)
    proj = _inproj(xp, xs_in, row(mix_norm_g[0]), bf(w_in[0]), colscale)
    proj3 = proj.reshape(nb, SEQ, IN_COLS)
    a = _na_attention(proj3, _na_bias_blocks(na_rpb[0]))
    b = _diff_attention(proj3, row(diff_lambda_q1[0]), row(diff_lambda_k1[0]),
                        row(diff_lambda_q2[0]), row(diff_lambda_k2[0]), row(diff_subln_g[0]))
    x1 = _merge(a.reshape(t, NA_WIDTH), b.reshape(t, DIFF_V_WIDTH), proj, xp, xs_in,
                bf(w_branch_na[0]), bf(w_branch_diff[0]), bf(w_out[0]))
    kv = _memkv(mem_p, mem_s, row(mem_norm_g[0]), bf(xa_w_kv[0]))
    w_router = jnp.zeros((D_MODEL, LANES), F32)
    w_router = w_router.at[:, :N_GROUPS].set(router_group_w[0]).at[:, N_GROUPS:N_GROUPS + N_EXPERTS].set(router_expert_w[0])
    b_router = jnp.zeros((1, LANES), F32)
    b_router = b_router.at[0, :N_GROUPS].set(router_group_b[0]).at[0, N_GROUPS:N_GROUPS + N_EXPERTS].set(router_expert_b[0])
    x2, hp, gates, eid = _xattn(x1, kv, row(xa_norm_g[0]), bf(xa_w_q[0]), bf(xa_w_o[0]),
                                row(ffn_norm_g[0]), bf(w_router), b_router)
    idx, gidx, tiles, counts = _rank(eid, gates)
    pos, runs, pos_c, runs_c, cnt, pend_blocks, bexp, nused, n_slots = _routing_plan(idx, tiles, counts)
    xs = _dispatch(cnt, pend_blocks, nused, pos, runs, hp, n_slots)
    ys = _moe(bexp, nused, xs, w_gate[0], w_up[0], w_down[0])
    fg = row(final_norm_g)
    y_p = _combine(pos_c, runs_c, gidx, x2, fg, ys, 0, nb_p * SEQ).reshape(nb_p, SEQ, D_MODEL)
    y_s = _combine(pos_c, runs_c, gidx, x2, fg, ys, nb_p * SEQ, nb_s * SEQ).reshape(nb_s, SEQ, D_MODEL)
    return (y_p, y_s)
```

```python
import functools
import math

import jax
import jax.numpy as jnp
from jax import lax
from jax.experimental import pallas as pl
from jax.experimental.pallas import tpu as pltpu

F32 = jnp.float32
BF16 = jnp.bfloat16
I32 = jnp.int32
U32 = jnp.uint32

D_MODEL = 2048
SEQ = 2048
GRID_W = 64
ROWS = SEQ // GRID_W
NA_HEADS = 8
NA_HEAD_DIM = 128
NA_WIDTH = NA_HEADS * NA_HEAD_DIM
NA_KH = 8
NA_KW = 16
DIFF_HEADS = 8
DIFF_QK_DIM = 64
DIFF_V_DIM = 128
DIFF_QK_WIDTH = DIFF_HEADS * 2 * DIFF_QK_DIM
DIFF_V_WIDTH = DIFF_HEADS * DIFF_V_DIM
IN_COLS = 3 * NA_WIDTH + 2 * DIFF_QK_WIDTH + DIFF_V_WIDTH + 2 * D_MODEL
MEM_LEN = 256
XA_HEADS = 4
XA_HEAD_DIM = 128
XA_WIDTH = XA_HEADS * XA_HEAD_DIM
N_GROUPS = 4
EXPERTS_PER_GROUP = 8
N_EXPERTS = N_GROUPS * EXPERTS_PER_GROUP
TOP_K = 2
D_EXPERT = 512
NORM_EPS = 1e-6
SUBLN_EPS = 1e-5
NEG_INF = -1e30
LAMBDA_INIT = 0.8 - 0.6 * math.exp(-0.3 * 0)
LOG2E = math.log2(math.e)

LANES = 128
HP_TILES = D_MODEL // 2 // LANES
YS_TILES = D_MODEL // LANES
VMEM_LIMIT = 56 * 1024 * 1024
MERGE_VMEM_LIMIT = 60 * 1024 * 1024

INPROJ_TM = 1024
INPROJ_TN = 1024
NORM_CHUNK = 128
DIFF_TQ = 2048
DIFF_SUB = 128
MERGE_TM = 512
XA_TM = 512
XA_SUB = 256
RANK_TM = 512
MOE_TM = 512
ROW_TM = 256


def _cparams(sem):
    return pltpu.CompilerParams(dimension_semantics=sem, vmem_limit_bytes=VMEM_LIMIT)


def _rms_scale(x, g, eps):
    ms = jnp.mean(x * x, axis=-1, keepdims=True)
    return x * lax.rsqrt(ms + eps) * g


def _dot_nt(a, b):
    return lax.dot_general(a, b, (((1,), (1,)), ((), ())), preferred_element_type=F32)


def _two_group_specs(tm, width, n_first):
    first = pl.BlockSpec((tm, width), lambda i, *_: (jnp.minimum(i, n_first - 1), 0))
    second = pl.BlockSpec((tm, width), lambda i, *_: (jnp.maximum(i - n_first, 0), 0))
    return first, second


def _inproj_kernel(n_first, xp_ref, xs_ref, g_ref, w_ref, cs_ref, o_ref, h_ref):
    def normalise(x_ref):
        def chunk(c, carry):
            rows = pl.ds(pl.multiple_of(c * NORM_CHUNK, NORM_CHUNK), NORM_CHUNK)
            h_ref[rows, :] = _rms_scale(x_ref[rows, :], g_ref[...], NORM_EPS).astype(BF16)
            return carry
        lax.fori_loop(0, x_ref.shape[0] // NORM_CHUNK, chunk, 0)

    first_col = pl.program_id(1) == 0
    in_first = pl.program_id(0) < n_first
    pl.when(first_col & in_first)(lambda: normalise(xp_ref))
    pl.when(first_col & jnp.logical_not(in_first))(lambda: normalise(xs_ref))
    acc = jnp.dot(h_ref[...], w_ref[...], preferred_element_type=F32)
    o_ref[...] = (acc * cs_ref[...]).astype(o_ref.dtype)


def _inproj(xp, xs, g, w, colscale):
    d = xp.shape[1]
    t = xp.shape[0] + xs.shape[0]
    n = w.shape[1]
    tm = INPROJ_TM
    n_first = xp.shape[0] // tm
    spec_p, spec_s = _two_group_specs(tm, d, n_first)
    return pl.pallas_call(
        functools.partial(_inproj_kernel, n_first),
        grid=(t // tm, n // INPROJ_TN),
        in_specs=[spec_p, spec_s,
                  pl.BlockSpec((1, d), lambda i, j: (0, 0)),
                  pl.BlockSpec((d, INPROJ_TN), lambda i, j: (0, j)),
                  pl.BlockSpec((1, INPROJ_TN), lambda i, j: (0, j))],
        out_specs=pl.BlockSpec((tm, INPROJ_TN), lambda i, j: (i, j)),
        out_shape=jax.ShapeDtypeStruct((t, n), BF16),
        scratch_shapes=[pltpu.VMEM((tm, d), BF16)],
        compiler_params=_cparams(("parallel", "arbitrary")),
        name="inproj",
    )(xp, xs, g, w, colscale)


NA_QROWS = 4
NA_KROWS = 12
NA_TILES = ROWS // NA_QROWS


def _na_tile_key_row(ti):
    return min(max(NA_QROWS * ti - NA_KH // 2, 0), ROWS - NA_KROWS)


def _na_bias_blocks(rpb):
    cols = jnp.arange(GRID_W)
    cstart = jnp.clip(cols - NA_KW // 2, 0, GRID_W - NA_KW)
    col_ok = (cols[None, :] >= cstart[:, None]) & (cols[None, :] < cstart[:, None] + NA_KW)
    dc = jnp.clip(cols[None, :] - cols[:, None], 1 - NA_KW, NA_KW - 1) + NA_KW - 1
    pick = (dc[None] == jnp.arange(2 * NA_KW - 1)[:, None, None]).astype(F32)
    t = jnp.einsum("hdx,xck->hdck", rpb.astype(F32), pick, precision=lax.Precision.HIGHEST)
    return jnp.where(col_ok[None, None], t, NEG_INF)


def _na_kernel(q_ref, k_ref, v_ref, t_ref, o_ref, bias_ref):
    scale = NA_HEAD_DIM ** -0.5
    tq = NA_QROWS * GRID_W
    tk = NA_KROWS * GRID_W

    @pl.when(pl.program_id(1) == 0)
    def _():
        masked = jnp.full((GRID_W, GRID_W), NEG_INF, F32)
        for variant, ti in enumerate((0, 1, NA_TILES - 1)):
            for jr in range(NA_QROWS):
                r = NA_QROWS * ti + jr
                rs = min(max(r - NA_KH // 2, 0), ROWS - NA_KH)
                blocks = []
                for i in range(NA_KROWS):
                    kr = _na_tile_key_row(ti) + i
                    blocks.append(t_ref[0, kr - r + NA_KH - 1] if rs <= kr < rs + NA_KH else masked)
                bias_ref[variant, jr * GRID_W:(jr + 1) * GRID_W, :] = jnp.concatenate(blocks, axis=1) * LOG2E

    def key_rows(ti):
        k0 = _na_tile_key_row(ti) * GRID_W
        return slice(k0, k0 + tk)

    def scores(ti):
        return _dot_nt(q_ref[0, ti * tq:(ti + 1) * tq, :], k_ref[0, key_rows(ti), :])

    pending = scores(0)
    for ti in range(NA_TILES):
        variant = 0 if ti == 0 else (2 if ti == NA_TILES - 1 else 1)
        qrows = slice(ti * tq, (ti + 1) * tq)
        krows = key_rows(ti)
        s = pending * (scale * LOG2E) + bias_ref[variant]
        if ti + 1 < NA_TILES:
            pending = scores(ti + 1)
        m = jnp.max(s, axis=-1, keepdims=True)
        e = jnp.exp2(s - m)
        l = jnp.sum(e, axis=-1, keepdims=True)
        o = jnp.dot(e.astype(BF16), v_ref[0, krows, :], preferred_element_type=F32)
        o_ref[0, qrows, :] = (o / l).astype(o_ref.dtype)


def _na_attention(proj3, blocks):
    b = proj3.shape[0]
    hd = NA_HEAD_DIM
    return pl.pallas_call(
        _na_kernel,
        grid=(NA_HEADS, b),
        in_specs=[pl.BlockSpec((1, SEQ, hd), lambda h, i: (i, 0, h)),
                  pl.BlockSpec((1, SEQ, hd), lambda h, i: (i, 0, NA_HEADS + h)),
                  pl.BlockSpec((1, SEQ, hd), lambda h, i: (i, 0, 2 * NA_HEADS + h)),
                  pl.BlockSpec((1,) + blocks.shape[1:], lambda h, i: (h, 0, 0, 0))],
        out_specs=pl.BlockSpec((1, SEQ, hd), lambda h, i: (i, 0, h)),
        out_shape=jax.ShapeDtypeStruct((b, SEQ, NA_WIDTH), BF16),
        scratch_shapes=[pltpu.VMEM((3, NA_QROWS * GRID_W, NA_KROWS * GRID_W), F32)],
        compiler_params=_cparams(("arbitrary", "arbitrary")),
        name="na_attn",
    )(proj3, proj3, proj3, blocks)


def _alibi_table(tq):
    slopes = jnp.exp2(-8.0 * jnp.arange(1, DIFF_HEADS + 1, dtype=F32) / DIFF_HEADS)
    r = jnp.arange(tq)[:, None]
    x = jnp.arange(2 * SEQ - tq)[None, :]
    dist = jnp.abs(r - x + (SEQ - tq)).astype(F32)
    return (slopes * LOG2E)[:, None, None] * dist[None]


def _diff_kernel(lq1_ref, lk1_ref, lq2_ref, lk2_ref, q_ref, k_ref, v_ref, g_ref, alibi_ref, o_ref):
    qi = pl.program_id(2)
    ts = alibi_ref.shape[1]
    nsub = q_ref.shape[1] // ts
    lam = (jnp.exp(jnp.sum(lq1_ref[...] * lk1_ref[...], axis=-1, keepdims=True))
           - jnp.exp(jnp.sum(lq2_ref[...] * lk2_ref[...], axis=-1, keepdims=True)) + LAMBDA_INIT)
    k = k_ref[0]

    def scores(u):
        q = q_ref[0, u * ts:(u + 1) * ts, :]
        return [_dot_nt(q[:, mi * DIFF_QK_DIM:(mi + 1) * DIFF_QK_DIM], k[:, mi * DIFF_QK_DIM:(mi + 1) * DIFF_QK_DIM])
                for mi in range(2)]

    def expo(s, bias):
        s = s - bias
        m = jnp.max(s, axis=-1, keepdims=True)
        e = jnp.exp2(s - m)
        return e, jnp.sum(e, axis=-1, keepdims=True)

    pending = scores(0)
    for u in range(nsub):
        s0, s1 = pending
        if u + 1 < nsub:
            pending = scores(u + 1)
        off = (SEQ - ts) - (qi * nsub + u) * ts
        bias = alibi_ref[0, :, pl.ds(pl.multiple_of(off, ts), SEQ)]
        e0, l0 = expo(s0, bias)
        e1, l1 = expo(s1, bias)
        a = (e0 - e1 * (lam * l0 / l1)).astype(BF16)
        o = jnp.dot(a, v_ref[0], preferred_element_type=F32) / l0
        o = _rms_scale(o, g_ref[...], SUBLN_EPS) * (1.0 - LAMBDA_INIT)
        o_ref[0, u * ts:(u + 1) * ts, :] = o.astype(o_ref.dtype)


def _diff_attention(proj3, lq1, lk1, lq2, lk2, subln_g):
    b = proj3.shape[0]
    tq = DIFF_TQ
    qoff = 3 * NA_WIDTH // LANES
    koff = qoff + DIFF_QK_WIDTH // LANES
    voff = koff + DIFF_QK_WIDTH // LANES
    vec = lambda n: pl.BlockSpec((1, n), lambda h, i, j: (0, 0))
    return pl.pallas_call(
        _diff_kernel,
        grid=(DIFF_HEADS, b, SEQ // tq),
        in_specs=[vec(DIFF_QK_DIM), vec(DIFF_QK_DIM), vec(DIFF_QK_DIM), vec(DIFF_QK_DIM),
                  pl.BlockSpec((1, tq, LANES), lambda h, i, j: (i, j, qoff + h)),
                  pl.BlockSpec((1, SEQ, LANES), lambda h, i, j: (i, 0, koff + h)),
                  pl.BlockSpec((1, SEQ, LANES), lambda h, i, j: (i, 0, voff + h)),
                  vec(DIFF_V_DIM),
                  pl.BlockSpec((1, DIFF_SUB, 2 * SEQ - DIFF_SUB), lambda h, i, j: (h, 0, 0))],
        out_specs=pl.BlockSpec((1, tq, DIFF_V_DIM), lambda h, i, j: (i, j, h)),
        out_shape=jax.ShapeDtypeStruct((b, SEQ, DIFF_V_WIDTH), BF16),
        compiler_params=_cparams(("arbitrary", "arbitrary", "arbitrary")),
        name="diff_attn",
    )(lq1, lk1, lq2, lk2, proj3, proj3, proj3, subln_g, _alibi_table(DIFF_SUB))


def _merge_kernel(n_first, a_ref, b_ref, gna_ref, gdf_ref, xp_ref, xs_ref, wna_ref, wdf_ref, wout_ref, o_ref):
    pa = jnp.dot(a_ref[...], wna_ref[...], preferred_element_type=F32)
    pb = jnp.dot(b_ref[...], wdf_ref[...], preferred_element_type=F32)
    merged = (jax.nn.sigmoid(gna_ref[...].astype(F32)) * pa
              + jax.nn.sigmoid(gdf_ref[...].astype(F32)) * pb)
    delta = jnp.dot(merged.astype(BF16), wout_ref[...], preferred_element_type=F32)
    in_first = pl.program_id(0) < n_first

    @pl.when(in_first)
    def _():
        o_ref[...] = xp_ref[...] + delta

    @pl.when(jnp.logical_not(in_first))
    def _():
        o_ref[...] = xs_ref[...] + delta


def _merge(a, b, proj, xp, xs, wna, wdf, wout):
    t = xp.shape[0] + xs.shape[0]
    tm = MERGE_TM
    gna_blk = (3 * NA_WIDTH + 2 * DIFF_QK_WIDTH + DIFF_V_WIDTH) // D_MODEL
    const = lambda shape: pl.BlockSpec(shape, lambda i: (0, 0), pipeline_mode=pl.Buffered(1))
    n_first = xp.shape[0] // tm
    spec_p, spec_s = _two_group_specs(tm, D_MODEL, n_first)
    return pl.pallas_call(
        functools.partial(_merge_kernel, n_first),
        grid=(t // tm,),
        in_specs=[pl.BlockSpec((tm, NA_WIDTH), lambda i: (i, 0)),
                  pl.BlockSpec((tm, DIFF_V_WIDTH), lambda i: (i, 0)),
                  pl.BlockSpec((tm, D_MODEL), lambda i: (i, gna_blk)),
                  pl.BlockSpec((tm, D_MODEL), lambda i: (i, gna_blk + 1)),
                  spec_p, spec_s,
                  const((NA_WIDTH, D_MODEL)), const((DIFF_V_WIDTH, D_MODEL)), const((D_MODEL, D_MODEL))],
        out_specs=pl.BlockSpec((tm, D_MODEL), lambda i: (i, 0)),
        out_shape=jax.ShapeDtypeStruct((t, D_MODEL), F32),
        compiler_params=pltpu.CompilerParams(dimension_semantics=("parallel",), vmem_limit_bytes=MERGE_VMEM_LIMIT),
        name="merge_outproj",
    )(a, b, proj, proj, xp, xs, wna, wdf, wout)


def _memkv_kernel(n_first, mp_ref, ms_ref, g_ref, w_ref, o_ref):
    def project(m_ref):
        h = _rms_scale(m_ref[...], g_ref[...], NORM_EPS).astype(BF16)
        o_ref[...] = jnp.dot(h, w_ref[...], preferred_element_type=F32).astype(o_ref.dtype)

    in_first = pl.program_id(0) < n_first
    pl.when(in_first)(lambda: project(mp_ref))
    pl.when(jnp.logical_not(in_first))(lambda: project(ms_ref))


def _memkv(mem_p, mem_s, g, w):
    t = mem_p.shape[0] + mem_s.shape[0]
    n_first = mem_p.shape[0] // MEM_LEN
    spec_p, spec_s = _two_group_specs(MEM_LEN, D_MODEL, n_first)
    return pl.pallas_call(
        functools.partial(_memkv_kernel, n_first),
        grid=(t // MEM_LEN,),
        in_specs=[spec_p, spec_s,
                  pl.BlockSpec((1, D_MODEL), lambda i: (0, 0)),
                  pl.BlockSpec((D_MODEL, 2 * XA_WIDTH), lambda i: (0, 0))],
        out_specs=pl.BlockSpec((MEM_LEN, 2 * XA_WIDTH), lambda i: (i, 0)),
        out_shape=jax.ShapeDtypeStruct((t, 2 * XA_WIDTH), BF16),
        compiler_params=_cparams(("parallel",)),
        name="mem_kv",
    )(mem_p, mem_s, g, w)


def _route(logits):
    lane = lax.broadcasted_iota(I32, logits.shape, 1).astype(F32)
    ninf = -jnp.inf
    big = float(LANES)
    gl = jnp.where(lane < N_GROUPS, logits, ninf)
    gmax = jnp.max(gl, axis=-1, keepdims=True)
    g = jnp.min(jnp.where(gl == gmax, lane, big), axis=-1, keepdims=True)
    pg = 1.0 / jnp.sum(jnp.exp(gl - gmax), axis=-1, keepdims=True)
    lo = N_GROUPS + EXPERTS_PER_GROUP * g
    el = jnp.where((lane >= lo) & (lane < lo + EXPERTS_PER_GROUP), logits, ninf)
    v1 = jnp.max(el, axis=-1, keepdims=True)
    i1 = jnp.min(jnp.where(el == v1, lane, big), axis=-1, keepdims=True)
    el2 = jnp.where(lane == i1, ninf, el)
    v2 = jnp.max(el2, axis=-1, keepdims=True)
    i2 = jnp.min(jnp.where(el2 == v2, lane, big), axis=-1, keepdims=True)
    t = jnp.exp(v2 - v1)
    den = 1.0 + t
    gate1 = pg * (1.0 / den)
    gate2 = pg * (t / den)
    gates = jnp.where(lane == 0.0, gate1, jnp.where(lane == 1.0, gate2, 0.0))
    eids = jnp.where(lane == 0.0, i1 - N_GROUPS, jnp.where(lane == 1.0, i2 - N_GROUPS, 0.0)).astype(I32)
    return gates, eids


def _pack_halves(h):
    half = h.shape[1] // 2
    hi = pltpu.bitcast(h[:, :half].astype(BF16).astype(F32), U32)
    lo = pltpu.bitcast(h[:, half:].astype(BF16).astype(F32), U32)
    return hi | (lo >> 16)


def _unpack_halves(w):
    hi = pltpu.bitcast(w & jnp.uint32(0xFFFF0000), F32).astype(BF16)
    lo = pltpu.bitcast(w << 16, F32).astype(BF16)
    return jnp.concatenate([hi, lo], axis=1)


def _store_row_tiles(ref, x):
    n, width = x.shape
    c = width // LANES
    for j in range(c):
        ref[pl.ds(j, n, stride=c), :] = x[:, j * LANES:(j + 1) * LANES]


def _load_row_tiles(ref, n, c):
    return jnp.concatenate([ref[pl.ds(j, n, stride=c), :] for j in range(c)], axis=1)


def _xattn_kernel(x_ref, kv_ref, gx_ref, wq_ref, wo_ref, gf_ref, wr_ref, br_ref,
                  x2_ref, hp_ref, gate_ref, eid_ref):
    scale = XA_HEAD_DIM ** -0.5
    heads = [slice(h * XA_HEAD_DIM, (h + 1) * XA_HEAD_DIM) for h in range(XA_HEADS)]
    nsub = x_ref.shape[0] // XA_SUB

    def rows(u):
        return slice(u * XA_SUB, (u + 1) * XA_SUB)

    def scores(u):
        hq = _rms_scale(x_ref[rows(u), :], gx_ref[...], NORM_EPS).astype(BF16)
        q = jnp.dot(hq, wq_ref[...], preferred_element_type=F32).astype(BF16)
        return [_dot_nt(q[:, sl], kv_ref[:, sl]) for sl in heads]

    def finish(u, sc):
        outs = []
        for h in range(XA_HEADS):
            vsl = slice(XA_WIDTH + h * XA_HEAD_DIM, XA_WIDTH + (h + 1) * XA_HEAD_DIM)
            s = sc[h] * (scale * LOG2E)
            m = jnp.max(s, axis=-1, keepdims=True)
            e = jnp.exp2(s - m)
            l = jnp.sum(e, axis=-1, keepdims=True)
            o = jnp.dot(e.astype(BF16), kv_ref[:, vsl], preferred_element_type=F32)
            outs.append((o / l).astype(BF16))
        o = jnp.concatenate(outs, axis=1)
        x2 = x_ref[rows(u), :] + jnp.dot(o, wo_ref[...], preferred_element_type=F32)
        x2_ref[rows(u), :] = x2
        h3 = _rms_scale(x2, gf_ref[...], NORM_EPS)
        _store_row_tiles(hp_ref.at[pl.ds(u * XA_SUB * HP_TILES, XA_SUB * HP_TILES)], _pack_halves(h3))
        logits = jnp.dot(h3.astype(BF16), wr_ref[...], preferred_element_type=F32) + br_ref[...]
        gates, eids = _route(logits)
        gate_ref[rows(u), :] = gates
        eid_ref[rows(u), :] = eids

    pending = scores(0)
    for u in range(nsub):
        sc = pending
        if u + 1 < nsub:
            pending = scores(u + 1)
        finish(u, sc)


def _xattn(x1, kv, gx, wq, wo, gf, wr, br):
    t = x1.shape[0]
    tm = XA_TM
    per_batch = SEQ // tm
    const = lambda shape: pl.BlockSpec(shape, lambda i: (0, 0))
    tile = lambda n: pl.BlockSpec((tm, n), lambda i: (i, 0))
    return pl.pallas_call(
        _xattn_kernel,
        grid=(t // tm,),
        in_specs=[tile(D_MODEL),
                  pl.BlockSpec((MEM_LEN, 2 * XA_WIDTH), lambda i: (i // per_batch, 0)),
                  const((1, D_MODEL)), const((D_MODEL, XA_WIDTH)), const((XA_WIDTH, D_MODEL)),
                  const((1, D_MODEL)), const((D_MODEL, LANES)), const((1, LANES))],
        out_specs=[tile(D_MODEL), pl.BlockSpec((tm * HP_TILES, LANES), lambda i: (i, 0)), tile(LANES), tile(LANES)],
        out_shape=[jax.ShapeDtypeStruct((t, D_MODEL), F32),
                   jax.ShapeDtypeStruct((t * HP_TILES, LANES), U32),
                   jax.ShapeDtypeStruct((t, LANES), F32),
                   jax.ShapeDtypeStruct((t, LANES), I32)],
        compiler_params=_cparams(("parallel",)),
        name="xattn_router",
    )(x1, kv, gx, wq, wo, gf, wr, br)


IDX_ROWS = 8
PIECE = 8


def _rank_kernel(eid_ref, gate_ref, idx_ref, gidx_ref, tile_ref, cnt_ref, carry_ref):
    @pl.when(pl.program_id(0) == 0)
    def _():
        carry_ref[...] = jnp.zeros_like(carry_ref)

    tm = eid_ref.shape[0]
    eid = eid_ref[...]
    lane = lax.broadcasted_iota(I32, (tm, LANES), 1)
    e1 = eid[:, 0:1]
    e2 = eid[:, 1:2]
    m1 = lane == e1
    m2 = lane == e2
    onehot = jnp.where(m1, 1.0, jnp.where(m2, 1.0, 0.0))
    r = lax.broadcasted_iota(I32, (tm, tm), 0)
    c = lax.broadcasted_iota(I32, (tm, tm), 1)
    tri = jnp.where(c < r, 1.0, 0.0).astype(BF16)
    local = jnp.dot(tri, onehot.astype(BF16), preferred_element_type=F32)
    carry = carry_ref[0:1, :]
    n_tile = jnp.sum(onehot, axis=0, keepdims=True)

    n_piece = jnp.floor((n_tile + (PIECE - 1)) * (1.0 / PIECE)) * PIECE

    def assignment(m, e):
        r_local = jnp.sum(jnp.where(m, local, 0.0), axis=-1, keepdims=True)
        r_global = r_local + jnp.sum(jnp.where(m, carry, 0.0), axis=-1, keepdims=True)
        first = jnp.sum(jnp.where(lane < e, n_tile, 0.0), axis=-1, keepdims=True)
        first_piece = jnp.sum(jnp.where(lane < e, n_piece, 0.0), axis=-1, keepdims=True)
        return first + r_local, r_global, first_piece + r_local

    p1, g1, q1 = assignment(m1, e1)
    p2, g2, q2 = assignment(m2, e2)
    cols = [p1, p2, g1, g2, e1.astype(F32), e2.astype(F32), q1, q2]
    table = jnp.zeros((tm, LANES), F32)
    for j, col in enumerate(cols):
        table = jnp.where(lane == j, col, table)
    idx_ref[0] = jnp.transpose(table)[0:IDX_ROWS, :].astype(I32)
    gidx_ref[0] = jnp.transpose(gate_ref[...])[0:IDX_ROWS, :]
    tile_ref[0] = jnp.where(lax.broadcasted_iota(I32, (8, LANES), 0) == 0, carry, n_tile)
    carry_ref[...] = carry_ref[...] + n_tile
    cnt_ref[...] = carry_ref[...]


def _rank(eid, gates):
    t = eid.shape[0]
    tm = RANK_TM
    n_tiles = t // tm
    return pl.pallas_call(
        _rank_kernel,
        grid=(n_tiles,),
        in_specs=[pl.BlockSpec((tm, LANES), lambda i: (i, 0)), pl.BlockSpec((tm, LANES), lambda i: (i, 0))],
        out_specs=[pl.BlockSpec((1, IDX_ROWS, tm), lambda i: (i, 0, 0)),
                   pl.BlockSpec((1, IDX_ROWS, tm), lambda i: (i, 0, 0)),
                   pl.BlockSpec((1, 8, LANES), lambda i: (i, 0, 0)),
                   pl.BlockSpec((8, LANES), lambda i: (0, 0))],
        out_shape=[jax.ShapeDtypeStruct((n_tiles, IDX_ROWS, tm), I32),
                   jax.ShapeDtypeStruct((n_tiles, IDX_ROWS, tm), F32),
                   jax.ShapeDtypeStruct((n_tiles, 8, LANES), F32),
                   jax.ShapeDtypeStruct((8, LANES), F32)],
        scratch_shapes=[pltpu.VMEM((8, LANES), F32)],
        compiler_params=_cparams(("arbitrary",)),
        name="expert_rank",
    )(eid, gates)


def _dispatch_kernel(n_steps, cnt_ref, pend_ref, nused_ref, pos_ref, run_ref, run_prev_ref, hp_ref, xs_ref, sbuf,
                     zero_ref, sem, zsem):
    i = pl.program_id(0)
    tm = hp_ref.shape[0] // (2 * HP_TILES)
    zb = zero_ref.shape[0]
    n_blocks = xs_ref.shape[0] // zb

    @pl.when(i == 0)
    def _():
        zero_ref[...] = jnp.zeros_like(zero_ref)
        for b in range(2):
            sbuf[b, pl.ds(TOP_K * tm * HP_TILES, PIECE * HP_TILES), :] = jnp.zeros((PIECE * HP_TILES, LANES), U32)

        def zero_copy(block):
            return pltpu.make_async_copy(zero_ref, xs_ref.at[pl.ds(pl.multiple_of(block * zb, zb), zb)], zsem)

        def tail_start(j, carry):
            zero_copy(j).start()
            return carry

        def tail_wait(j, carry):
            zero_copy(j).wait()
            return carry

        for e in range(N_EXPERTS):
            @pl.when(cnt_ref[e] > 0)
            def _():
                zero_copy(pend_ref[e] - 1).start()
        lax.fori_loop(nused_ref[0], n_blocks, tail_start, 0)
        for e in range(N_EXPERTS):
            @pl.when(cnt_ref[e] > 0)
            def _():
                zero_copy(pend_ref[e] - 1).wait()
        lax.fori_loop(nused_ref[0], n_blocks, tail_wait, 0)

    def place(half):
        def body(t, carry):
            row = hp_ref[pl.ds(pl.multiple_of((half * tm + t) * HP_TILES, HP_TILES), HP_TILES), :]
            for k in range(TOP_K):
                sbuf[half, pl.ds(pl.multiple_of(pos_ref[half, k, t] * HP_TILES, HP_TILES), HP_TILES), :] = row
            return carry
        lax.fori_loop(0, tm, body, 0, unroll=8)

    def piece_copy(src_tok, dst_slot, half):
        rows = PIECE * HP_TILES
        return pltpu.make_async_copy(sbuf.at[half, pl.ds(pl.multiple_of(src_tok * HP_TILES, HP_TILES), rows)],
                                     xs_ref.at[pl.ds(pl.multiple_of(dst_slot * HP_TILES, HP_TILES), rows)],
                                     sem.at[half])

    def issue(half):
        for e in range(N_EXPERTS):
            slot0 = run_ref[half, 0, e]
            first = run_ref[half, 0, 2 * N_EXPERTS + e]

            def run_piece(j, carry):
                piece_copy(first + PIECE * j, slot0 + PIECE * j, half).start()
                return carry

            lax.fori_loop(0, run_ref[half, 0, N_EXPERTS + e], run_piece, 0)

    def drain(runs, half):
        def wait_piece(j, carry):
            piece_copy(0, 0, half).wait()
            return carry
        lax.fori_loop(0, runs[half, 0, 3 * N_EXPERTS], wait_piece, 0)

    place(0)

    @pl.when(i > 0)
    def _():
        drain(run_prev_ref, 1)
    issue(0)
    place(1)
    drain(run_ref, 0)
    issue(1)

    @pl.when(i == n_steps - 1)
    def _():
        drain(run_ref, 1)


def _dispatch(cnt, pend_blocks, nused, pos, runs, hp, n_slots):
    n_tiles, _, tm = pos.shape
    n_steps = n_tiles // 2
    grid_spec = pltpu.PrefetchScalarGridSpec(
        num_scalar_prefetch=3,
        grid=(n_steps,),
        in_specs=[pl.BlockSpec((2, TOP_K, tm), lambda i, *_: (i, 0, 0), memory_space=pltpu.SMEM),
                  pl.BlockSpec((2, 1, LANES), lambda i, *_: (i, 0, 0), memory_space=pltpu.SMEM),
                  pl.BlockSpec((2, 1, LANES), lambda i, *_: (jnp.maximum(i - 1, 0), 0, 0), memory_space=pltpu.SMEM),
                  pl.BlockSpec((2 * tm * HP_TILES, LANES), lambda i, *_: (i, 0))],
        out_specs=pl.BlockSpec(memory_space=pl.ANY),
        scratch_shapes=[pltpu.VMEM((2, (TOP_K * tm + PIECE) * HP_TILES, LANES), U32),
                        pltpu.VMEM((MOE_TM * HP_TILES, LANES), U32),
                        pltpu.SemaphoreType.DMA((2,)), pltpu.SemaphoreType.DMA(())],
    )
    return pl.pallas_call(
        functools.partial(_dispatch_kernel, n_steps),
        grid_spec=grid_spec,
        out_shape=jax.ShapeDtypeStruct((n_slots * HP_TILES, LANES), U32),
        compiler_params=_cparams(("arbitrary",)),
        name="moe_dispatch",
    )(cnt, pend_blocks, nused, pos, runs, runs, hp)


MOE_DOWN_CHUNKS = 4


def _moe_kernel(bexp_ref, nused_ref, xs_ref, wg_ref, wu_ref, wd_ref, o_ref, wgb, wub, wdb):
    i = pl.program_id(0)
    nused = nused_ref[0]
    tm = xs_ref.shape[0] // HP_TILES

    @pl.when((i < nused) & ((i == 0) | (bexp_ref[i] != bexp_ref[jnp.maximum(i - 1, 0)])))
    def _():
        wgb[...] = wg_ref[0].astype(BF16)
        wub[...] = wu_ref[0].astype(BF16)
        wdb[...] = wd_ref[0].astype(BF16)

    @pl.when(i < nused)
    def _():
        x = _unpack_halves(_load_row_tiles(xs_ref, tm, HP_TILES))
        hg = jnp.dot(x, wgb[...], preferred_element_type=F32)
        hu = jnp.dot(x, wub[...], preferred_element_type=F32)
        hdn = ((hg * jax.nn.sigmoid(hg)) * hu).astype(BF16)
        cw = D_MODEL // MOE_DOWN_CHUNKS
        for c in range(MOE_DOWN_CHUNKS):
            y = jnp.dot(hdn, wdb[:, c * cw:(c + 1) * cw], preferred_element_type=F32)
            for j in range(cw // LANES):
                o_ref[pl.ds(c * (cw // LANES) + j, tm, stride=YS_TILES), :] = y[:, j * LANES:(j + 1) * LANES]

    @pl.when(i >= nused)
    def _():
        o_ref[...] = jnp.zeros_like(o_ref)


def _moe(bexp, nused, xs, wg, wu, wd):
    tm = MOE_TM
    n_slots = xs.shape[0] // HP_TILES
    n_blocks = n_slots // tm
    grid_spec = pltpu.PrefetchScalarGridSpec(
        num_scalar_prefetch=2,
        grid=(n_blocks,),
        in_specs=[pl.BlockSpec((tm * HP_TILES, LANES), lambda i, be, nu: (jnp.minimum(i, nu[0] - 1), 0)),
                  pl.BlockSpec((1, D_MODEL, D_EXPERT), lambda i, be, nu: (be[i], 0, 0)),
                  pl.BlockSpec((1, D_MODEL, D_EXPERT), lambda i, be, nu: (be[i], 0, 0)),
                  pl.BlockSpec((1, D_EXPERT, D_MODEL), lambda i, be, nu: (be[i], 0, 0))],
        out_specs=pl.BlockSpec((tm * YS_TILES, LANES), lambda i, be, nu: (i, 0)),
        scratch_shapes=[pltpu.VMEM((D_MODEL, D_EXPERT), BF16), pltpu.VMEM((D_MODEL, D_EXPERT), BF16),
                        pltpu.VMEM((D_EXPERT, D_MODEL), BF16)],
    )
    return pl.pallas_call(
        _moe_kernel,
        grid_spec=grid_spec,
        out_shape=jax.ShapeDtypeStruct((n_slots * YS_TILES, LANES), F32),
        compiler_params=_cparams(("arbitrary",)),
        name="moe_experts",
    )(bexp, nused, xs, wg, wu, wd)


def _combine_kernel(n_tiles, pos_ref, run_ref, run_next_ref, gate_ref, x_ref, g_ref, ys_ref, o_ref, ybuf, mbuf, sem):
    i = pl.program_id(0)
    tm = x_ref.shape[0]
    slot = lax.rem(i, 2)

    def piece_copy(src_slot, dst_pos, buf):
        rows = PIECE * YS_TILES
        return pltpu.make_async_copy(ys_ref.at[pl.ds(pl.multiple_of(src_slot * YS_TILES, YS_TILES), rows)],
                                     ybuf.at[buf, pl.ds(pl.multiple_of(dst_pos * YS_TILES, YS_TILES), rows)],
                                     sem.at[buf])

    def request(runs, buf):
        for e in range(N_EXPERTS):
            slot0 = runs[0, 0, e]
            first = runs[0, 0, 2 * N_EXPERTS + e]

            def run_piece(j, carry):
                piece_copy(slot0 + PIECE * j, first + PIECE * j, buf).start()
                return carry

            lax.fori_loop(0, runs[0, 0, N_EXPERTS + e], run_piece, 0)

    @pl.when(i == 0)
    def _():
        request(run_ref, 0)

    @pl.when(i + 1 < n_tiles)
    def _():
        request(run_next_ref, 1 - slot)

    def drain(j, carry):
        piece_copy(0, 0, slot).wait()
        return carry

    lax.fori_loop(0, run_ref[0, 0, 3 * N_EXPERTS], drain, 0)

    def token(t, carry):
        acc = None
        for k in range(TOP_K):
            rows = pl.ds(pl.multiple_of(pos_ref[0, k, t] * YS_TILES, YS_TILES), YS_TILES)
            term = ybuf[slot, rows, :] * gate_ref[0, k, t]
            acc = term if acc is None else acc + term
        mbuf[pl.ds(pl.multiple_of(t * YS_TILES, YS_TILES), YS_TILES), :] = acc
        return carry

    lax.fori_loop(0, tm, token, 0, unroll=8)
    moe = _load_row_tiles(mbuf, tm, YS_TILES)
    o_ref[...] = _rms_scale(x_ref[...] + moe, g_ref[...], NORM_EPS)


def _combine(pos, runs, gidx, x2, g, ys, row0, n_rows):
    _, _, tm = pos.shape
    tile0 = row0 // tm
    n = n_rows // tm
    smem = lambda rows: pl.BlockSpec((1, rows, tm), lambda i: (tile0 + i, 0, 0), memory_space=pltpu.SMEM)
    run_spec = lambda ahead: pl.BlockSpec((1, 1, LANES), lambda i: (tile0 + jnp.minimum(i + ahead, n - 1), 0, 0),
                                          memory_space=pltpu.SMEM)
    return pl.pallas_call(
        functools.partial(_combine_kernel, n),
        grid=(n,),
        in_specs=[smem(TOP_K), run_spec(0), run_spec(1), smem(TOP_K),
                  pl.BlockSpec((tm, D_MODEL), lambda i: (tile0 + i, 0)),
                  pl.BlockSpec((1, D_MODEL), lambda i: (0, 0)),
                  pl.BlockSpec(memory_space=pl.ANY)],
        out_specs=pl.BlockSpec((tm, D_MODEL), lambda i: (i, 0)),
        out_shape=jax.ShapeDtypeStruct((n_rows, D_MODEL), F32),
        scratch_shapes=[pltpu.VMEM((2, (TOP_K * tm + N_EXPERTS * PIECE) * YS_TILES, LANES), F32),
                        pltpu.VMEM((tm * YS_TILES, LANES), F32),
                        pltpu.SemaphoreType.DMA((2,))],
        compiler_params=_cparams(("arbitrary",)),
        name="moe_combine",
    )(pos, runs, runs, gidx[:, 0:TOP_K, :], x2, g, ys)


def _routing_plan(idx, tiles, counts):
    cnt = counts[0, :N_EXPERTS].astype(I32)
    padded = jnp.where(cnt > 0, (cnt + PIECE - 1 + MOE_TM - 1) // MOE_TM * MOE_TM, 0)
    pends = jnp.cumsum(padded)
    pstarts = pends - padded
    n_tiles, _, tm = idx.shape
    n_blocks = n_tiles * tm * TOP_K // MOE_TM + N_EXPERTS + 1
    blk0 = jnp.arange(n_blocks, dtype=I32) * MOE_TM
    bexp = jnp.minimum(jnp.sum(pends[None, :] <= blk0[:, None], axis=1), N_EXPERTS - 1).astype(I32)
    nused = (pends[-1:] // MOE_TM).astype(I32)
    before = tiles[:, 0, :N_EXPERTS].astype(I32)
    inside = tiles[:, 1, :N_EXPERTS].astype(I32)
    first = jnp.cumsum(inside, axis=1) - inside
    pieces = (inside + PIECE - 1) // PIECE
    first_piece = (jnp.cumsum(pieces, axis=1) - pieces) * PIECE

    def run_table(first_pos):
        return jnp.concatenate([pstarts[None, :] + before, pieces, first_pos, jnp.sum(pieces, axis=1, keepdims=True),
                                jnp.zeros((n_tiles, LANES - 3 * N_EXPERTS - 1), I32)], axis=1).reshape(n_tiles, 1, LANES)

    pos = idx[:, 0:TOP_K, :]
    pos_piece = idx[:, 3 * TOP_K:4 * TOP_K, :]
    return pos, run_table(first), pos_piece, run_table(first_piece), cnt, (pends // MOE_TM).astype(I32), bexp, nused, n_blocks * MOE_TM


def kernel(x_prompt, x_sample, mem_prompt, mem_sample, mix_norm_g, w_in, na_rpb, diff_lambda_q1, diff_lambda_k1, diff_lambda_q2, diff_lambda_k2, diff_subln_g, w_branch_na, w_branch_diff, w_out, xa_norm_g, mem_norm_g, xa_w_q, xa_w_kv, xa_w_o, ffn_norm_g, router_group_w, router_group_b, router_expert_w, router_expert_b, w_gate, w_up, w_down, final_norm_g):
    nb_p, nb_s = x_prompt.shape[0], x_sample.shape[0]
    nb = nb_p + nb_s
    t = nb * SEQ
    xp = x_prompt.reshape(nb_p * SEQ, D_MODEL)
    xs_in = x_sample.reshape(nb_s * SEQ, D_MODEL)
    mem_p = mem_prompt.reshape(nb_p * MEM_LEN, D_MODEL)
    mem_s = mem_sample.reshape(nb_s * MEM_LEN, D_MODEL)
    row = lambda v: v.reshape(1, -1).astype(F32)
    bf = lambda w: w.astype(BF16)

    qcol = 3 * NA_WIDTH
    colscale = jnp.ones((1, IN_COLS), F32).at[:, qcol:qcol + DIFF_QK_WIDTH].set(DIFF_QK_DIM ** -0.5 * LOG2E)
    proj = _inproj(xp, xs_in, row(mix_norm_g[0]), bf(w_in[0]), colscale)
    proj3 = proj.reshape(nb, SEQ, IN_COLS)
    a = _na_attention(proj3, _na_bias_blocks(na_rpb[0]))
    b = _diff_attention(proj3, row(diff_lambda_q1[0]), row(diff_lambda_k1[0]),
                        row(diff_lambda_q2[0]), row(diff_lambda_k2[0]), row(diff_subln_g[0]))
    x1 = _merge(a.reshape(t, NA_WIDTH), b.reshape(t, DIFF_V_WIDTH), proj, xp, xs_in,
                bf(w_branch_na[0]), bf(w_branch_diff[0]), bf(w_out[0]))
    kv = _memkv(mem_p, mem_s, row(mem_norm_g[0]), bf(xa_w_kv[0]))
    w_router = jnp.zeros((D_MODEL, LANES), F32)
    w_router = w_router.at[:, :N_GROUPS].set(router_group_w[0]).at[:, N_GROUPS:N_GROUPS + N_EXPERTS].set(router_expert_w[0])
    b_router = jnp.zeros((1, LANES), F32)
    b_router = b_router.at[0, :N_GROUPS].set(router_group_b[0]).at[0, N_GROUPS:N_GROUPS + N_EXPERTS].set(router_expert_b[0])
    x2, hp, gates, eid = _xattn(x1, kv, row(xa_norm_g[0]), bf(xa_w_q[0]), bf(xa_w_o[0]),
                                row(ffn_norm_g[0]), bf(w_router), b_router)
    idx, gidx, tiles, counts = _rank(eid, gates)
    pos, runs, pos_c, runs_c, cnt, pend_blocks, bexp, nused, n_slots = _routing_plan(idx, tiles, counts)
    xs = _dispatch(cnt, pend_blocks, nused, pos, runs, hp, n_slots)
    ys = _moe(bexp, nused, xs, w_gate[0], w_up[0], w_down[0])
    fg = row(final_norm_g)
    y_p = _combine(pos_c, runs_c, gidx, x2, fg, ys, 0, nb_p * SEQ).reshape(nb_p, SEQ, D_MODEL)
    y_s = _combine(pos_c, runs_c, gidx, x2, fg, ys, nb_p * SEQ, nb_s * SEQ).reshape(nb_s, SEQ, D_MODEL)
    return (y_p, y_s)
```

```python
import functools
import math

import jax
import jax.numpy as jnp
from jax import lax
from jax.experimental import pallas as pl
from jax.experimental.pallas import tpu as pltpu

F32 = jnp.float32
BF16 = jnp.bfloat16
I32 = jnp.int32
U32 = jnp.uint32

D_MODEL = 2048
SEQ = 2048
GRID_W = 64
ROWS = SEQ // GRID_W
NA_HEADS = 8
NA_HEAD_DIM = 128
NA_WIDTH = NA_HEADS * NA_HEAD_DIM
NA_KH = 8
NA_KW = 16
DIFF_HEADS = 8
DIFF_QK_DIM = 64
DIFF_V_DIM = 128
DIFF_QK_WIDTH = DIFF_HEADS * 2 * DIFF_QK_DIM
DIFF_V_WIDTH = DIFF_HEADS * DIFF_V_DIM
IN_COLS = 3 * NA_WIDTH + 2 * DIFF_QK_WIDTH + DIFF_V_WIDTH + 2 * D_MODEL
MEM_LEN = 256
XA_HEADS = 4
XA_HEAD_DIM = 128
XA_WIDTH = XA_HEADS * XA_HEAD_DIM
N_GROUPS = 4
EXPERTS_PER_GROUP = 8
N_EXPERTS = N_GROUPS * EXPERTS_PER_GROUP
TOP_K = 2
D_EXPERT = 512
NORM_EPS = 1e-6
SUBLN_EPS = 1e-5
NEG_INF = -1e30
LAMBDA_INIT = 0.8 - 0.6 * math.exp(-0.3 * 0)
LOG2E = math.log2(math.e)

LANES = 128
HP_TILES = D_MODEL // 2 // LANES
YS_TILES = D_MODEL // LANES
VMEM_LIMIT = 56 * 1024 * 1024
BIG_VMEM_LIMIT = 60 * 1024 * 1024

INPROJ_TM = 1024
INPROJ_TN = 1024
NORM_CHUNK = 128
DIFF_TQ = 2048
DIFF_SUB = 128
MERGE_TM = 512
XA_TM = 1024
XA_SUB = 256
RANK_TM = 512
MOE_TM = 512


def _cparams(sem):
    return pltpu.CompilerParams(dimension_semantics=sem, vmem_limit_bytes=VMEM_LIMIT)


def _rms_scale(x, g, eps):
    ms = jnp.mean(x * x, axis=-1, keepdims=True)
    return x * lax.rsqrt(ms + eps) * g


def _dot_nt(a, b):
    return lax.dot_general(a, b, (((1,), (1,)), ((), ())), preferred_element_type=F32)


def _two_group_specs(tm, width, n_first):
    first = pl.BlockSpec((tm, width), lambda i, *_: (jnp.minimum(i, n_first - 1), 0))
    second = pl.BlockSpec((tm, width), lambda i, *_: (jnp.maximum(i - n_first, 0), 0))
    return first, second


def _inproj_kernel(n_first, xp_ref, xs_ref, g_ref, w_ref, cs_ref, o_ref, h_ref):
    def normalise(x_ref):
        def chunk(c, carry):
            rows = pl.ds(pl.multiple_of(c * NORM_CHUNK, NORM_CHUNK), NORM_CHUNK)
            h_ref[rows, :] = _rms_scale(x_ref[rows, :], g_ref[...], NORM_EPS).astype(BF16)
            return carry
        lax.fori_loop(0, x_ref.shape[0] // NORM_CHUNK, chunk, 0)

    first_col = pl.program_id(1) == 0
    in_first = pl.program_id(0) < n_first
    pl.when(first_col & in_first)(lambda: normalise(xp_ref))
    pl.when(first_col & jnp.logical_not(in_first))(lambda: normalise(xs_ref))
    acc = jnp.dot(h_ref[...], w_ref[...], preferred_element_type=F32)
    o_ref[...] = (acc * cs_ref[...]).astype(o_ref.dtype)


def _inproj(xp, xs, g, w, colscale):
    d = xp.shape[1]
    t = xp.shape[0] + xs.shape[0]
    n = w.shape[1]
    tm = INPROJ_TM
    n_first = xp.shape[0] // tm
    spec_p, spec_s = _two_group_specs(tm, d, n_first)
    return pl.pallas_call(
        functools.partial(_inproj_kernel, n_first),
        grid=(t // tm, n // INPROJ_TN),
        in_specs=[spec_p, spec_s,
                  pl.BlockSpec((1, d), lambda i, j: (0, 0)),
                  pl.BlockSpec((d, INPROJ_TN), lambda i, j: (0, j)),
                  pl.BlockSpec((1, INPROJ_TN), lambda i, j: (0, j))],
        out_specs=pl.BlockSpec((tm, INPROJ_TN), lambda i, j: (i, j)),
        out_shape=jax.ShapeDtypeStruct((t, n), BF16),
        scratch_shapes=[pltpu.VMEM((tm, d), BF16)],
        compiler_params=_cparams(("parallel", "arbitrary")),
        name="inproj",
    )(xp, xs, g, w, colscale)


NA_QROWS = 4
NA_KROWS = 12
NA_TILES = ROWS // NA_QROWS


def _na_tile_key_row(ti):
    return min(max(NA_QROWS * ti - NA_KH // 2, 0), ROWS - NA_KROWS)


def _na_bias_blocks(rpb):
    cols = jnp.arange(GRID_W)
    cstart = jnp.clip(cols - NA_KW // 2, 0, GRID_W - NA_KW)
    col_ok = (cols[None, :] >= cstart[:, None]) & (cols[None, :] < cstart[:, None] + NA_KW)
    dc = jnp.clip(cols[None, :] - cols[:, None], 1 - NA_KW, NA_KW - 1) + NA_KW - 1
    pick = (dc[None] == jnp.arange(2 * NA_KW - 1)[:, None, None]).astype(F32)
    t = jnp.einsum("hdx,xck->hdck", rpb.astype(F32), pick, precision=lax.Precision.HIGHEST)
    return jnp.where(col_ok[None, None], t, NEG_INF)


def _na_kernel(q_ref, k_ref, v_ref, t_ref, o_ref, bias_ref):
    scale = NA_HEAD_DIM ** -0.5
    tq = NA_QROWS * GRID_W
    tk = NA_KROWS * GRID_W

    @pl.when(pl.program_id(1) == 0)
    def _():
        masked = jnp.full((GRID_W, GRID_W), NEG_INF, F32)
        for variant, ti in enumerate((0, 1, NA_TILES - 1)):
            for jr in range(NA_QROWS):
                r = NA_QROWS * ti + jr
                rs = min(max(r - NA_KH // 2, 0), ROWS - NA_KH)
                blocks = []
                for i in range(NA_KROWS):
                    kr = _na_tile_key_row(ti) + i
                    blocks.append(t_ref[0, kr - r + NA_KH - 1] if rs <= kr < rs + NA_KH else masked)
                bias_ref[variant, jr * GRID_W:(jr + 1) * GRID_W, :] = jnp.concatenate(blocks, axis=1) * LOG2E

    def key_rows(ti):
        k0 = _na_tile_key_row(ti) * GRID_W
        return slice(k0, k0 + tk)

    def scores(ti):
        return _dot_nt(q_ref[0, ti * tq:(ti + 1) * tq, :], k_ref[0, key_rows(ti), :])

    pending = scores(0)
    for ti in range(NA_TILES):
        variant = 0 if ti == 0 else (2 if ti == NA_TILES - 1 else 1)
        qrows = slice(ti * tq, (ti + 1) * tq)
        krows = key_rows(ti)
        s = pending * (scale * LOG2E) + bias_ref[variant]
        if ti + 1 < NA_TILES:
            pending = scores(ti + 1)
        m = jnp.max(s, axis=-1, keepdims=True)
        e = jnp.exp2(s - m)
        l = jnp.sum(e, axis=-1, keepdims=True)
        o = jnp.dot(e.astype(BF16), v_ref[0, krows, :], preferred_element_type=F32)
        o_ref[0, qrows, :] = (o / l).astype(o_ref.dtype)


def _na_attention(proj3, blocks):
    b = proj3.shape[0]
    hd = NA_HEAD_DIM
    return pl.pallas_call(
        _na_kernel,
        grid=(NA_HEADS, b),
        in_specs=[pl.BlockSpec((1, SEQ, hd), lambda h, i: (i, 0, h)),
                  pl.BlockSpec((1, SEQ, hd), lambda h, i: (i, 0, NA_HEADS + h)),
                  pl.BlockSpec((1, SEQ, hd), lambda h, i: (i, 0, 2 * NA_HEADS + h)),
                  pl.BlockSpec((1,) + blocks.shape[1:], lambda h, i: (h, 0, 0, 0))],
        out_specs=pl.BlockSpec((1, SEQ, hd), lambda h, i: (i, 0, h)),
        out_shape=jax.ShapeDtypeStruct((b, SEQ, NA_WIDTH), BF16),
        scratch_shapes=[pltpu.VMEM((3, NA_QROWS * GRID_W, NA_KROWS * GRID_W), F32)],
        compiler_params=_cparams(("arbitrary", "arbitrary")),
        name="na_attn",
    )(proj3, proj3, proj3, blocks)


def _alibi_table(tq):
    slopes = jnp.exp2(-8.0 * jnp.arange(1, DIFF_HEADS + 1, dtype=F32) / DIFF_HEADS)
    r = jnp.arange(tq)[:, None]
    x = jnp.arange(2 * SEQ - tq)[None, :]
    dist = jnp.abs(r - x + (SEQ - tq)).astype(F32)
    return (slopes * LOG2E)[:, None, None] * dist[None]


def _diff_kernel(lq1_ref, lk1_ref, lq2_ref, lk2_ref, q_ref, k_ref, v_ref, g_ref, alibi_ref, o_ref):
    qi = pl.program_id(2)
    ts = alibi_ref.shape[1]
    nsub = q_ref.shape[1] // ts
    lam = (jnp.exp(jnp.sum(lq1_ref[...] * lk1_ref[...], axis=-1, keepdims=True))
           - jnp.exp(jnp.sum(lq2_ref[...] * lk2_ref[...], axis=-1, keepdims=True)) + LAMBDA_INIT)
    k = k_ref[0]

    def scores(u):
        q = q_ref[0, u * ts:(u + 1) * ts, :]
        return [_dot_nt(q[:, mi * DIFF_QK_DIM:(mi + 1) * DIFF_QK_DIM], k[:, mi * DIFF_QK_DIM:(mi + 1) * DIFF_QK_DIM])
                for mi in range(2)]

    def expo(s, bias):
        s = s - bias
        m = jnp.max(s, axis=-1, keepdims=True)
        e = jnp.exp2(s - m)
        return e, jnp.sum(e, axis=-1, keepdims=True)

    pending = scores(0)
    for u in range(nsub):
        s0, s1 = pending
        if u + 1 < nsub:
            pending = scores(u + 1)
        off = (SEQ - ts) - (qi * nsub + u) * ts
        bias = alibi_ref[0, :, pl.ds(pl.multiple_of(off, ts), SEQ)]
        e0, l0 = expo(s0, bias)
        e1, l1 = expo(s1, bias)
        a = (e0 - e1 * (lam * l0 / l1)).astype(BF16)
        o = jnp.dot(a, v_ref[0], preferred_element_type=F32) / l0
        o = _rms_scale(o, g_ref[...], SUBLN_EPS) * (1.0 - LAMBDA_INIT)
        o_ref[0, u * ts:(u + 1) * ts, :] = o.astype(o_ref.dtype)


def _diff_attention(proj3, lq1, lk1, lq2, lk2, subln_g):
    b = proj3.shape[0]
    tq = DIFF_TQ
    qoff = 3 * NA_WIDTH // LANES
    koff = qoff + DIFF_QK_WIDTH // LANES
    voff = koff + DIFF_QK_WIDTH // LANES
    vec = lambda n: pl.BlockSpec((1, n), lambda h, i, j: (0, 0))
    return pl.pallas_call(
        _diff_kernel,
        grid=(DIFF_HEADS, b, SEQ // tq),
        in_specs=[vec(DIFF_QK_DIM), vec(DIFF_QK_DIM), vec(DIFF_QK_DIM), vec(DIFF_QK_DIM),
                  pl.BlockSpec((1, tq, LANES), lambda h, i, j: (i, j, qoff + h)),
                  pl.BlockSpec((1, SEQ, LANES), lambda h, i, j: (i, 0, koff + h)),
                  pl.BlockSpec((1, SEQ, LANES), lambda h, i, j: (i, 0, voff + h)),
                  vec(DIFF_V_DIM),
                  pl.BlockSpec((1, DIFF_SUB, 2 * SEQ - DIFF_SUB), lambda h, i, j: (h, 0, 0))],
        out_specs=pl.BlockSpec((1, tq, DIFF_V_DIM), lambda h, i, j: (i, j, h)),
        out_shape=jax.ShapeDtypeStruct((b, SEQ, DIFF_V_WIDTH), BF16),
        compiler_params=_cparams(("arbitrary", "arbitrary", "arbitrary")),
        name="diff_attn",
    )(lq1, lk1, lq2, lk2, proj3, proj3, proj3, subln_g, _alibi_table(DIFF_SUB))


def _merge_kernel(n_first, a_ref, b_ref, gna_ref, gdf_ref, xp_ref, xs_ref, wna_ref, wdf_ref, wout_ref, o_ref):
    pa = jnp.dot(a_ref[...], wna_ref[...], preferred_element_type=F32)
    pb = jnp.dot(b_ref[...], wdf_ref[...], preferred_element_type=F32)
    merged = (jax.nn.sigmoid(gna_ref[...].astype(F32)) * pa
              + jax.nn.sigmoid(gdf_ref[...].astype(F32)) * pb)
    delta = jnp.dot(merged.astype(BF16), wout_ref[...], preferred_element_type=F32)
    in_first = pl.program_id(0) < n_first

    @pl.when(in_first)
    def _():
        o_ref[...] = xp_ref[...] + delta

    @pl.when(jnp.logical_not(in_first))
    def _():
        o_ref[...] = xs_ref[...] + delta


def _merge(a, b, proj, xp, xs, wna, wdf, wout):
    t = xp.shape[0] + xs.shape[0]
    tm = MERGE_TM
    gna_blk = (3 * NA_WIDTH + 2 * DIFF_QK_WIDTH + DIFF_V_WIDTH) // D_MODEL
    const = lambda shape: pl.BlockSpec(shape, lambda i: (0, 0), pipeline_mode=pl.Buffered(1))
    n_first = xp.shape[0] // tm
    spec_p, spec_s = _two_group_specs(tm, D_MODEL, n_first)
    return pl.pallas_call(
        functools.partial(_merge_kernel, n_first),
        grid=(t // tm,),
        in_specs=[pl.BlockSpec((tm, NA_WIDTH), lambda i: (i, 0)),
                  pl.BlockSpec((tm, DIFF_V_WIDTH), lambda i: (i, 0)),
                  pl.BlockSpec((tm, D_MODEL), lambda i: (i, gna_blk)),
                  pl.BlockSpec((tm, D_MODEL), lambda i: (i, gna_blk + 1)),
                  spec_p, spec_s,
                  const((NA_WIDTH, D_MODEL)), const((DIFF_V_WIDTH, D_MODEL)), const((D_MODEL, D_MODEL))],
        out_specs=pl.BlockSpec((tm, D_MODEL), lambda i: (i, 0)),
        out_shape=jax.ShapeDtypeStruct((t, D_MODEL), F32),
        compiler_params=pltpu.CompilerParams(dimension_semantics=("parallel",), vmem_limit_bytes=BIG_VMEM_LIMIT),
        name="merge_outproj",
    )(a, b, proj, proj, xp, xs, wna, wdf, wout)


def _memkv_kernel(n_first, mp_ref, ms_ref, g_ref, w_ref, o_ref):
    def project(m_ref):
        h = _rms_scale(m_ref[...], g_ref[...], NORM_EPS).astype(BF16)
        o_ref[...] = jnp.dot(h, w_ref[...], preferred_element_type=F32).astype(o_ref.dtype)

    in_first = pl.program_id(0) < n_first
    pl.when(in_first)(lambda: project(mp_ref))
    pl.when(jnp.logical_not(in_first))(lambda: project(ms_ref))


def _memkv(mem_p, mem_s, g, w):
    t = mem_p.shape[0] + mem_s.shape[0]
    n_first = mem_p.shape[0] // MEM_LEN
    spec_p, spec_s = _two_group_specs(MEM_LEN, D_MODEL, n_first)
    return pl.pallas_call(
        functools.partial(_memkv_kernel, n_first),
        grid=(t // MEM_LEN,),
        in_specs=[spec_p, spec_s,
                  pl.BlockSpec((1, D_MODEL), lambda i: (0, 0)),
                  pl.BlockSpec((D_MODEL, 2 * XA_WIDTH), lambda i: (0, 0))],
        out_specs=pl.BlockSpec((MEM_LEN, 2 * XA_WIDTH), lambda i: (i, 0)),
        out_shape=jax.ShapeDtypeStruct((t, 2 * XA_WIDTH), BF16),
        compiler_params=_cparams(("parallel",)),
        name="mem_kv",
    )(mem_p, mem_s, g, w)


def _route(logits):
    lane = lax.broadcasted_iota(I32, logits.shape, 1).astype(F32)
    ninf = -jnp.inf
    big = float(LANES)
    gl = jnp.where(lane < N_GROUPS, logits, ninf)
    gmax = jnp.max(gl, axis=-1, keepdims=True)
    g = jnp.min(jnp.where(gl == gmax, lane, big), axis=-1, keepdims=True)
    pg = 1.0 / jnp.sum(jnp.exp(gl - gmax), axis=-1, keepdims=True)
    lo = N_GROUPS + EXPERTS_PER_GROUP * g
    el = jnp.where((lane >= lo) & (lane < lo + EXPERTS_PER_GROUP), logits, ninf)
    v1 = jnp.max(el, axis=-1, keepdims=True)
    i1 = jnp.min(jnp.where(el == v1, lane, big), axis=-1, keepdims=True)
    el2 = jnp.where(lane == i1, ninf, el)
    v2 = jnp.max(el2, axis=-1, keepdims=True)
    i2 = jnp.min(jnp.where(el2 == v2, lane, big), axis=-1, keepdims=True)
    t = jnp.exp(v2 - v1)
    den = 1.0 + t
    gate1 = pg * (1.0 / den)
    gate2 = pg * (t / den)
    gates = jnp.where(lane == 0.0, gate1, jnp.where(lane == 1.0, gate2, 0.0))
    eids = jnp.where(lane == 0.0, i1 - N_GROUPS, jnp.where(lane == 1.0, i2 - N_GROUPS, 0.0)).astype(I32)
    return gates, eids


def _pack_halves(h):
    half = h.shape[1] // 2
    hi = pltpu.bitcast(h[:, :half].astype(BF16).astype(F32), U32)
    lo = pltpu.bitcast(h[:, half:].astype(BF16).astype(F32), U32)
    return hi | (lo >> 16)


def _unpack_halves(w):
    hi = pltpu.bitcast(w & jnp.uint32(0xFFFF0000), F32).astype(BF16)
    lo = pltpu.bitcast(w << 16, F32).astype(BF16)
    return jnp.concatenate([hi, lo], axis=1)


def _store_row_tiles(ref, x):
    n, width = x.shape
    c = width // LANES
    for j in range(c):
        ref[pl.ds(j, n, stride=c), :] = x[:, j * LANES:(j + 1) * LANES]


def _load_row_tiles(ref, n, c):
    return jnp.concatenate([ref[pl.ds(j, n, stride=c), :] for j in range(c)], axis=1)


def _xattn_kernel(x_ref, kv_ref, gx_ref, wq_ref, wo_ref, gf_ref, wr_ref, br_ref,
                  x2_ref, hp_ref, gate_ref, eid_ref):
    scale = XA_HEAD_DIM ** -0.5
    heads = [slice(h * XA_HEAD_DIM, (h + 1) * XA_HEAD_DIM) for h in range(XA_HEADS)]
    nsub = x_ref.shape[0] // XA_SUB

    def rows(u):
        return slice(u * XA_SUB, (u + 1) * XA_SUB)

    def scores(u):
        hq = _rms_scale(x_ref[rows(u), :], gx_ref[...], NORM_EPS).astype(BF16)
        q = jnp.dot(hq, wq_ref[...], preferred_element_type=F32).astype(BF16)
        return [_dot_nt(q[:, sl], kv_ref[:, sl]) for sl in heads]

    def finish(u, sc):
        outs = []
        for h in range(XA_HEADS):
            vsl = slice(XA_WIDTH + h * XA_HEAD_DIM, XA_WIDTH + (h + 1) * XA_HEAD_DIM)
            s = sc[h] * (scale * LOG2E)
            m = jnp.max(s, axis=-1, keepdims=True)
            e = jnp.exp2(s - m)
            l = jnp.sum(e, axis=-1, keepdims=True)
            o = jnp.dot(e.astype(BF16), kv_ref[:, vsl], preferred_element_type=F32)
            outs.append((o / l).astype(BF16))
        o = jnp.concatenate(outs, axis=1)
        x2 = x_ref[rows(u), :] + jnp.dot(o, wo_ref[...], preferred_element_type=F32)
        x2_ref[rows(u), :] = x2
        h3 = _rms_scale(x2, gf_ref[...], NORM_EPS)
        _store_row_tiles(hp_ref.at[pl.ds(u * XA_SUB * HP_TILES, XA_SUB * HP_TILES)], _pack_halves(h3))
        logits = jnp.dot(h3.astype(BF16), wr_ref[...], preferred_element_type=F32) + br_ref[...]
        gates, eids = _route(logits)
        gate_ref[rows(u), :] = gates
        eid_ref[rows(u), :] = eids

    pending = scores(0)
    for u in range(nsub):
        sc = pending
        if u + 1 < nsub:
            pending = scores(u + 1)
        finish(u, sc)


def _xattn(x1, kv, gx, wq, wo, gf, wr, br):
    t = x1.shape[0]
    tm = XA_TM
    per_batch = SEQ // tm
    const = lambda shape: pl.BlockSpec(shape, lambda i: (0, 0), pipeline_mode=pl.Buffered(1))
    tile = lambda n: pl.BlockSpec((tm, n), lambda i: (i, 0))
    return pl.pallas_call(
        _xattn_kernel,
        grid=(t // tm,),
        in_specs=[tile(D_MODEL),
                  pl.BlockSpec((MEM_LEN, 2 * XA_WIDTH), lambda i: (i // per_batch, 0)),
                  const((1, D_MODEL)), const((D_MODEL, XA_WIDTH)), const((XA_WIDTH, D_MODEL)),
                  const((1, D_MODEL)), const((D_MODEL, LANES)), const((1, LANES))],
        out_specs=[tile(D_MODEL), pl.BlockSpec((tm * HP_TILES, LANES), lambda i: (i, 0)), tile(LANES), tile(LANES)],
        out_shape=[jax.ShapeDtypeStruct((t, D_MODEL), F32),
                   jax.ShapeDtypeStruct((t * HP_TILES, LANES), U32),
                   jax.ShapeDtypeStruct((t, LANES), F32),
                   jax.ShapeDtypeStruct((t, LANES), I32)],
        compiler_params=pltpu.CompilerParams(dimension_semantics=("parallel",), vmem_limit_bytes=BIG_VMEM_LIMIT),
        name="xattn_router",
    )(x1, kv, gx, wq, wo, gf, wr, br)


IDX_ROWS = 8
PIECE = 8


def _rank_kernel(eid_ref, gate_ref, idx_ref, gidx_ref, tile_ref, cnt_ref, carry_ref):
    @pl.when(pl.program_id(0) == 0)
    def _():
        carry_ref[...] = jnp.zeros_like(carry_ref)

    tm = eid_ref.shape[0]
    eid = eid_ref[...]
    lane = lax.broadcasted_iota(I32, (tm, LANES), 1)
    e1 = eid[:, 0:1]
    e2 = eid[:, 1:2]
    m1 = lane == e1
    m2 = lane == e2
    onehot = jnp.where(m1, 1.0, jnp.where(m2, 1.0, 0.0))
    r = lax.broadcasted_iota(I32, (tm, tm), 0)
    c = lax.broadcasted_iota(I32, (tm, tm), 1)
    tri = jnp.where(c < r, 1.0, 0.0).astype(BF16)
    local = jnp.dot(tri, onehot.astype(BF16), preferred_element_type=F32)
    carry = carry_ref[0:1, :]
    n_tile = jnp.sum(onehot, axis=0, keepdims=True)

    n_piece = jnp.floor((n_tile + (PIECE - 1)) * (1.0 / PIECE)) * PIECE

    def assignment(m, e):
        r_local = jnp.sum(jnp.where(m, local, 0.0), axis=-1, keepdims=True)
        r_global = r_local + jnp.sum(jnp.where(m, carry, 0.0), axis=-1, keepdims=True)
        first = jnp.sum(jnp.where(lane < e, n_tile, 0.0), axis=-1, keepdims=True)
        first_piece = jnp.sum(jnp.where(lane < e, n_piece, 0.0), axis=-1, keepdims=True)
        return first + r_local, r_global, first_piece + r_local

    p1, g1, q1 = assignment(m1, e1)
    p2, g2, q2 = assignment(m2, e2)
    cols = [p1, p2, g1, g2, e1.astype(F32), e2.astype(F32), q1, q2]
    table = jnp.zeros((tm, LANES), F32)
    for j, col in enumerate(cols):
        table = jnp.where(lane == j, col, table)
    idx_ref[0] = jnp.transpose(table)[0:IDX_ROWS, :].astype(I32)
    gidx_ref[0] = jnp.transpose(gate_ref[...])[0:IDX_ROWS, :]
    tile_ref[0] = jnp.where(lax.broadcasted_iota(I32, (8, LANES), 0) == 0, carry, n_tile)
    carry_ref[...] = carry_ref[...] + n_tile
    cnt_ref[...] = carry_ref[...]


def _rank(eid, gates):
    t = eid.shape[0]
    tm = RANK_TM
    n_tiles = t // tm
    return pl.pallas_call(
        _rank_kernel,
        grid=(n_tiles,),
        in_specs=[pl.BlockSpec((tm, LANES), lambda i: (i, 0)), pl.BlockSpec((tm, LANES), lambda i: (i, 0))],
        out_specs=[pl.BlockSpec((1, IDX_ROWS, tm), lambda i: (i, 0, 0)),
                   pl.BlockSpec((1, IDX_ROWS, tm), lambda i: (i, 0, 0)),
                   pl.BlockSpec((1, 8, LANES), lambda i: (i, 0, 0)),
                   pl.BlockSpec((8, LANES), lambda i: (0, 0))],
        out_shape=[jax.ShapeDtypeStruct((n_tiles, IDX_ROWS, tm), I32),
                   jax.ShapeDtypeStruct((n_tiles, IDX_ROWS, tm), F32),
                   jax.ShapeDtypeStruct((n_tiles, 8, LANES), F32),
                   jax.ShapeDtypeStruct((8, LANES), F32)],
        scratch_shapes=[pltpu.VMEM((8, LANES), F32)],
        compiler_params=_cparams(("arbitrary",)),
        name="expert_rank",
    )(eid, gates)


def _dispatch_kernel(n_steps, cnt_ref, pend_ref, nused_ref, pos_ref, run_ref, run_prev_ref, hp_ref, xs_ref, sbuf,
                     zero_ref, sem, zsem):
    i = pl.program_id(0)
    tm = hp_ref.shape[0] // (2 * HP_TILES)
    zb = zero_ref.shape[0]
    n_blocks = xs_ref.shape[0] // zb

    @pl.when(i == 0)
    def _():
        zero_ref[...] = jnp.zeros_like(zero_ref)
        for b in range(2):
            sbuf[b, pl.ds(TOP_K * tm * HP_TILES, PIECE * HP_TILES), :] = jnp.zeros((PIECE * HP_TILES, LANES), U32)

        def zero_copy(block):
            return pltpu.make_async_copy(zero_ref, xs_ref.at[pl.ds(pl.multiple_of(block * zb, zb), zb)], zsem)

        def tail_start(j, carry):
            zero_copy(j).start()
            return carry

        def tail_wait(j, carry):
            zero_copy(j).wait()
            return carry

        for e in range(N_EXPERTS):
            @pl.when(cnt_ref[e] > 0)
            def _():
                zero_copy(pend_ref[e] - 1).start()
        lax.fori_loop(nused_ref[0], n_blocks, tail_start, 0)
        for e in range(N_EXPERTS):
            @pl.when(cnt_ref[e] > 0)
            def _():
                zero_copy(pend_ref[e] - 1).wait()
        lax.fori_loop(nused_ref[0], n_blocks, tail_wait, 0)

    def place(half):
        def body(t, carry):
            row = hp_ref[pl.ds(pl.multiple_of((half * tm + t) * HP_TILES, HP_TILES), HP_TILES), :]
            for k in range(TOP_K):
                sbuf[half, pl.ds(pl.multiple_of(pos_ref[half, k, t] * HP_TILES, HP_TILES), HP_TILES), :] = row
            return carry
        lax.fori_loop(0, tm, body, 0, unroll=8)

    def piece_copy(src_tok, dst_slot, half):
        rows = PIECE * HP_TILES
        return pltpu.make_async_copy(sbuf.at[half, pl.ds(pl.multiple_of(src_tok * HP_TILES, HP_TILES), rows)],
                                     xs_ref.at[pl.ds(pl.multiple_of(dst_slot * HP_TILES, HP_TILES), rows)],
                                     sem.at[half])

    def issue(half):
        for e in range(N_EXPERTS):
            slot0 = run_ref[half, 0, e]
            first = run_ref[half, 0, 2 * N_EXPERTS + e]

            def run_piece(j, carry):
                piece_copy(first + PIECE * j, slot0 + PIECE * j, half).start()
                return carry

            lax.fori_loop(0, run_ref[half, 0, N_EXPERTS + e], run_piece, 0)

    def drain(runs, half):
        def wait_piece(j, carry):
            piece_copy(0, 0, half).wait()
            return carry
        lax.fori_loop(0, runs[half, 0, 3 * N_EXPERTS], wait_piece, 0)

    place(0)

    @pl.when(i > 0)
    def _():
        drain(run_prev_ref, 1)
    issue(0)
    place(1)
    drain(run_ref, 0)
    issue(1)

    @pl.when(i == n_steps - 1)
    def _():
        drain(run_ref, 1)


def _dispatch(cnt, pend_blocks, nused, pos, runs, hp, n_slots):
    n_tiles, _, tm = pos.shape
    n_steps = n_tiles // 2
    grid_spec = pltpu.PrefetchScalarGridSpec(
        num_scalar_prefetch=3,
        grid=(n_steps,),
        in_specs=[pl.BlockSpec((2, TOP_K, tm), lambda i, *_: (i, 0, 0), memory_space=pltpu.SMEM),
                  pl.BlockSpec((2, 1, LANES), lambda i, *_: (i, 0, 0), memory_space=pltpu.SMEM),
                  pl.BlockSpec((2, 1, LANES), lambda i, *_: (jnp.maximum(i - 1, 0), 0, 0), memory_space=pltpu.SMEM),
                  pl.BlockSpec((2 * tm * HP_TILES, LANES), lambda i, *_: (i, 0))],
        out_specs=pl.BlockSpec(memory_space=pl.ANY),
        scratch_shapes=[pltpu.VMEM((2, (TOP_K * tm + PIECE) * HP_TILES, LANES), U32),
                        pltpu.VMEM((MOE_TM * HP_TILES, LANES), U32),
                        pltpu.SemaphoreType.DMA((2,)), pltpu.SemaphoreType.DMA(())],
    )
    return pl.pallas_call(
        functools.partial(_dispatch_kernel, n_steps),
        grid_spec=grid_spec,
        out_shape=jax.ShapeDtypeStruct((n_slots * HP_TILES, LANES), U32),
        compiler_params=_cparams(("arbitrary",)),
        name="moe_dispatch",
    )(cnt, pend_blocks, nused, pos, runs, runs, hp)


MOE_DOWN_CHUNKS = 4


def _moe_kernel(bexp_ref, nused_ref, xs_ref, wg_ref, wu_ref, wd_ref, o_ref, wgb, wub, wdb):
    i = pl.program_id(0)
    nused = nused_ref[0]
    tm = xs_ref.shape[0] // HP_TILES

    @pl.when((i < nused) & ((i == 0) | (bexp_ref[i] != bexp_ref[jnp.maximum(i - 1, 0)])))
    def _():
        wgb[...] = wg_ref[0].astype(BF16)
        wub[...] = wu_ref[0].astype(BF16)
        wdb[...] = wd_ref[0].astype(BF16)

    @pl.when(i < nused)
    def _():
        x = _unpack_halves(_load_row_tiles(xs_ref, tm, HP_TILES))
        hg = jnp.dot(x, wgb[...], preferred_element_type=F32)
        hu = jnp.dot(x, wub[...], preferred_element_type=F32)
        hdn = ((hg * jax.nn.sigmoid(hg)) * hu).astype(BF16)
        cw = D_MODEL // MOE_DOWN_CHUNKS
        for c in range(MOE_DOWN_CHUNKS):
            y = jnp.dot(hdn, wdb[:, c * cw:(c + 1) * cw], preferred_element_type=F32)
            for j in range(cw // LANES):
                o_ref[pl.ds(c * (cw // LANES) + j, tm, stride=YS_TILES), :] = y[:, j * LANES:(j + 1) * LANES]

    @pl.when(i >= nused)
    def _():
        o_ref[...] = jnp.zeros_like(o_ref)


def _moe(bexp, nused, xs, wg, wu, wd):
    tm = MOE_TM
    n_slots = xs.shape[0] // HP_TILES
    n_blocks = n_slots // tm
    grid_spec = pltpu.PrefetchScalarGridSpec(
        num_scalar_prefetch=2,
        grid=(n_blocks,),
        in_specs=[pl.BlockSpec((tm * HP_TILES, LANES), lambda i, be, nu: (jnp.minimum(i, nu[0] - 1), 0)),
                  pl.BlockSpec((1, D_MODEL, D_EXPERT), lambda i, be, nu: (be[i], 0, 0)),
                  pl.BlockSpec((1, D_MODEL, D_EXPERT), lambda i, be, nu: (be[i], 0, 0)),
                  pl.BlockSpec((1, D_EXPERT, D_MODEL), lambda i, be, nu: (be[i], 0, 0))],
        out_specs=pl.BlockSpec((tm * YS_TILES, LANES), lambda i, be, nu: (i, 0)),
        scratch_shapes=[pltpu.VMEM((D_MODEL, D_EXPERT), BF16), pltpu.VMEM((D_MODEL, D_EXPERT), BF16),
                        pltpu.VMEM((D_EXPERT, D_MODEL), BF16)],
    )
    return pl.pallas_call(
        _moe_kernel,
        grid_spec=grid_spec,
        out_shape=jax.ShapeDtypeStruct((n_slots * YS_TILES, LANES), F32),
        compiler_params=_cparams(("arbitrary",)),
        name="moe_experts",
    )(bexp, nused, xs, wg, wu, wd)


def _combine_kernel(n_tiles, pos_ref, run_ref, run_next_ref, gate_ref, x_ref, g_ref, ys_ref, o_ref, ybuf, mbuf, sem):
    i = pl.program_id(0)
    tm = x_ref.shape[0]
    slot = lax.rem(i, 2)

    def piece_copy(src_slot, dst_pos, buf):
        rows = PIECE * YS_TILES
        return pltpu.make_async_copy(ys_ref.at[pl.ds(pl.multiple_of(src_slot * YS_TILES, YS_TILES), rows)],
                                     ybuf.at[buf, pl.ds(pl.multiple_of(dst_pos * YS_TILES, YS_TILES), rows)],
                                     sem.at[buf])

    def request(runs, buf):
        for e in range(N_EXPERTS):
            slot0 = runs[0, 0, e]
            first = runs[0, 0, 2 * N_EXPERTS + e]

            def run_piece(j, carry):
                piece_copy(slot0 + PIECE * j, first + PIECE * j, buf).start()
                return carry

            lax.fori_loop(0, runs[0, 0, N_EXPERTS + e], run_piece, 0)

    @pl.when(i == 0)
    def _():
        request(run_ref, 0)

    @pl.when(i + 1 < n_tiles)
    def _():
        request(run_next_ref, 1 - slot)

    def drain(j, carry):
        piece_copy(0, 0, slot).wait()
        return carry

    lax.fori_loop(0, run_ref[0, 0, 3 * N_EXPERTS], drain, 0)

    def token(t, carry):
        acc = None
        for k in range(TOP_K):
            rows = pl.ds(pl.multiple_of(pos_ref[0, k, t] * YS_TILES, YS_TILES), YS_TILES)
            term = ybuf[slot, rows, :] * gate_ref[0, k, t]
            acc = term if acc is None else acc + term
        mbuf[pl.ds(pl.multiple_of(t * YS_TILES, YS_TILES), YS_TILES), :] = acc
        return carry

    lax.fori_loop(0, tm, token, 0, unroll=8)
    moe = _load_row_tiles(mbuf, tm, YS_TILES)
    o_ref[...] = _rms_scale(x_ref[...] + moe, g_ref[...], NORM_EPS)


def _combine(pos, runs, gidx, x2, g, ys, row0, n_rows):
    _, _, tm = pos.shape
    tile0 = row0 // tm
    n = n_rows // tm
    smem = lambda rows: pl.BlockSpec((1, rows, tm), lambda i: (tile0 + i, 0, 0), memory_space=pltpu.SMEM)
    run_spec = lambda ahead: pl.BlockSpec((1, 1, LANES), lambda i: (tile0 + jnp.minimum(i + ahead, n - 1), 0, 0),
                                          memory_space=pltpu.SMEM)
    return pl.pallas_call(
        functools.partial(_combine_kernel, n),
        grid=(n,),
        in_specs=[smem(TOP_K), run_spec(0), run_spec(1), smem(TOP_K),
                  pl.BlockSpec((tm, D_MODEL), lambda i: (tile0 + i, 0)),
                  pl.BlockSpec((1, D_MODEL), lambda i: (0, 0)),
                  pl.BlockSpec(memory_space=pl.ANY)],
        out_specs=pl.BlockSpec((tm, D_MODEL), lambda i: (i, 0)),
        out_shape=jax.ShapeDtypeStruct((n_rows, D_MODEL), F32),
        scratch_shapes=[pltpu.VMEM((2, (TOP_K * tm + N_EXPERTS * PIECE) * YS_TILES, LANES), F32),
                        pltpu.VMEM((tm * YS_TILES, LANES), F32),
                        pltpu.SemaphoreType.DMA((2,))],
        compiler_params=_cparams(("arbitrary",)),
        name="moe_combine",
    )(pos, runs, runs, gidx[:, 0:TOP_K, :], x2, g, ys)


def _routing_plan(idx, tiles, counts):
    cnt = counts[0, :N_EXPERTS].astype(I32)
    padded = jnp.where(cnt > 0, (cnt + PIECE - 1 + MOE_TM - 1) // MOE_TM * MOE_TM, 0)
    pends = jnp.cumsum(padded)
    pstarts = pends - padded
    n_tiles, _, tm = idx.shape
    n_blocks = n_tiles * tm * TOP_K // MOE_TM + N_EXPERTS + 1
    blk0 = jnp.arange(n_blocks, dtype=I32) * MOE_TM
    bexp = jnp.minimum(jnp.sum(pends[None, :] <= blk0[:, None], axis=1), N_EXPERTS - 1).astype(I32)
    nused = (pends[-1:] // MOE_TM).astype(I32)
    before = tiles[:, 0, :N_EXPERTS].astype(I32)
    inside = tiles[:, 1, :N_EXPERTS].astype(I32)
    first = jnp.cumsum(inside, axis=1) - inside
    pieces = (inside + PIECE - 1) // PIECE
    first_piece = (jnp.cumsum(pieces, axis=1) - pieces) * PIECE

    def run_table(first_pos):
        return jnp.concatenate([pstarts[None, :] + before, pieces, first_pos, jnp.sum(pieces, axis=1, keepdims=True),
                                jnp.zeros((n_tiles, LANES - 3 * N_EXPERTS - 1), I32)], axis=1).reshape(n_tiles, 1, LANES)

    pos = idx[:, 0:TOP_K, :]
    pos_piece = idx[:, 3 * TOP_K:4 * TOP_K, :]
    return pos, run_table(first), pos_piece, run_table(first_piece), cnt, (pends // MOE_TM).astype(I32), bexp, nused, n_blocks * MOE_TM


def kernel(x_prompt, x_sample, mem_prompt, mem_sample, mix_norm_g, w_in, na_rpb, diff_lambda_q1, diff_lambda_k1, diff_lambda_q2, diff_lambda_k2, diff_subln_g, w_branch_na, w_branch_diff, w_out, xa_norm_g, mem_norm_g, xa_w_q, xa_w_kv, xa_w_o, ffn_norm_g, router_group_w, router_group_b, router_expert_w, router_expert_b, w_gate, w_up, w_down, final_norm_g):
    nb_p, nb_s = x_prompt.shape[0], x_sample.shape[0]
    nb = nb_p + nb_s
    t = nb * SEQ
    xp = x_prompt.reshape(nb_p * SEQ, D_MODEL)
    xs_in = x_sample.reshape(nb_s * SEQ, D_MODEL)
    mem_p = mem_prompt.reshape(nb_p * MEM_LEN, D_MODEL)
    mem_s = mem_sample.reshape(nb_s * MEM_LEN, D_MODEL)
    row = lambda v: v.reshape(1, -1).astype(F32)
    bf = lambda w: w.astype(BF16)

    qcol = 3 * NA_WIDTH
    colscale = jnp.ones((1, IN_COLS), F32).at[:, qcol:qcol + DIFF_QK_WIDTH].set(DIFF_QK_DIM ** -0.5 * LOG2E)
    proj = _inproj(xp, xs_in, row(mix_norm_g[0]), bf(w_in[0]), colscale)
    proj3 = proj.reshape(nb, SEQ, IN_COLS)
    a = _na_attention(proj3, _na_bias_blocks(na_rpb[0]))
    b = _diff_attention(proj3, row(diff_lambda_q1[0]), row(diff_lambda_k1[0]),
                        row(diff_lambda_q2[0]), row(diff_lambda_k2[0]), row(diff_subln_g[0]))
    x1 = _merge(a.reshape(t, NA_WIDTH), b.reshape(t, DIFF_V_WIDTH), proj, xp, xs_in,
                bf(w_branch_na[0]), bf(w_branch_diff[0]), bf(w_out[0]))
    kv = _memkv(mem_p, mem_s, row(mem_norm_g[0]), bf(xa_w_kv[0]))
    w_router = jnp.zeros((D_MODEL, LANES), F32)
    w_router = w_router.at[:, :N_GROUPS].set(router_group_w[0]).at[:, N_GROUPS:N_GROUPS + N_EXPERTS].set(router_expert_w[0])
    b_router = jnp.zeros((1, LANES), F32)
    b_router = b_router.at[0, :N_GROUPS].set(router_group_b[0]).at[0, N_GROUPS:N_GROUPS + N_EXPERTS].set(router_expert_b[0])
    x2, hp, gates, eid = _xattn(x1, kv, row(xa_norm_g[0]), bf(xa_w_q[0]), bf(xa_w_o[0]),
                                row(ffn_norm_g[0]), bf(w_router), b_router)
    idx, gidx, tiles, counts = _rank(eid, gates)
    pos, runs, pos_c, runs_c, cnt, pend_blocks, bexp, nused, n_slots = _routing_plan(idx, tiles, counts)
    xs = _dispatch(cnt, pend_blocks, nused, pos, runs, hp, n_slots)
    ys = _moe(bexp, nused, xs, w_gate[0], w_up[0], w_down[0])
    fg = row(final_norm_g)
    y_p = _combine(pos_c, runs_c, gidx, x2, fg, ys, 0, nb_p * SEQ).reshape(nb_p, SEQ, D_MODEL)
    y_s = _combine(pos_c, runs_c, gidx, x2, fg, ys, nb_p * SEQ, nb_s * SEQ).reshape(nb_s, SEQ, D_MODEL)
    return (y_p, y_s)
```

```python
import functools
import math

import jax
import jax.numpy as jnp
from jax import lax
from jax.experimental import pallas as pl
from jax.experimental.pallas import tpu as pltpu

F32 = jnp.float32
BF16 = jnp.bfloat16
I32 = jnp.int32
U32 = jnp.uint32

D_MODEL = 2048
SEQ = 2048
GRID_W = 64
ROWS = SEQ // GRID_W
NA_HEADS = 8
NA_HEAD_DIM = 128
NA_WIDTH = NA_HEADS * NA_HEAD_DIM
NA_KH = 8
NA_KW = 16
DIFF_HEADS = 8
DIFF_QK_DIM = 64
DIFF_V_DIM = 128
DIFF_QK_WIDTH = DIFF_HEADS * 2 * DIFF_QK_DIM
DIFF_V_WIDTH = DIFF_HEADS * DIFF_V_DIM
IN_COLS = 3 * NA_WIDTH + 2 * DIFF_QK_WIDTH + DIFF_V_WIDTH + 2 * D_MODEL
MEM_LEN = 256
XA_HEADS = 4
XA_HEAD_DIM = 128
XA_WIDTH = XA_HEADS * XA_HEAD_DIM
N_GROUPS = 4
EXPERTS_PER_GROUP = 8
N_EXPERTS = N_GROUPS * EXPERTS_PER_GROUP
TOP_K = 2
D_EXPERT = 512
NORM_EPS = 1e-6
SUBLN_EPS = 1e-5
NEG_INF = -1e30
LAMBDA_INIT = 0.8 - 0.6 * math.exp(-0.3 * 0)
LOG2E = math.log2(math.e)

LANES = 128
HP_TILES = D_MODEL // 2 // LANES
YS_TILES = D_MODEL // LANES
VMEM_LIMIT = 56 * 1024 * 1024
BIG_VMEM_LIMIT = 60 * 1024 * 1024

INPROJ_TM = 1024
INPROJ_TN = 1280
NORM_CHUNK = 128
DIFF_TQ = 2048
DIFF_SUB = 128
MERGE_TM = 512
XA_TM = 1024
XA_SUB = 256
RANK_TM = 512
MOE_TM = 512


def _cparams(sem):
    return pltpu.CompilerParams(dimension_semantics=sem, vmem_limit_bytes=VMEM_LIMIT)


def _rms_scale(x, g, eps):
    ms = jnp.mean(x * x, axis=-1, keepdims=True)
    return x * lax.rsqrt(ms + eps) * g


def _dot_nt(a, b):
    return lax.dot_general(a, b, (((1,), (1,)), ((), ())), preferred_element_type=F32)


def _two_group_specs(tm, width, n_first):
    first = pl.BlockSpec((tm, width), lambda i, *_: (jnp.minimum(i, n_first - 1), 0))
    second = pl.BlockSpec((tm, width), lambda i, *_: (jnp.maximum(i - n_first, 0), 0))
    return first, second


def _inproj_kernel(n_first, xp_ref, xs_ref, g_ref, w_ref, cs_ref, o_ref, h_ref):
    def normalise(x_ref):
        def chunk(c, carry):
            rows = pl.ds(pl.multiple_of(c * NORM_CHUNK, NORM_CHUNK), NORM_CHUNK)
            h_ref[rows, :] = _rms_scale(x_ref[rows, :], g_ref[...], NORM_EPS).astype(BF16)
            return carry
        lax.fori_loop(0, x_ref.shape[0] // NORM_CHUNK, chunk, 0)

    first_col = pl.program_id(1) == 0
    in_first = pl.program_id(0) < n_first
    pl.when(first_col & in_first)(lambda: normalise(xp_ref))
    pl.when(first_col & jnp.logical_not(in_first))(lambda: normalise(xs_ref))
    acc = jnp.dot(h_ref[...], w_ref[...], preferred_element_type=F32)
    o_ref[...] = (acc * cs_ref[...]).astype(o_ref.dtype)


def _inproj(xp, xs, g, w, colscale):
    d = xp.shape[1]
    t = xp.shape[0] + xs.shape[0]
    n = w.shape[1]
    tm = INPROJ_TM
    n_first = xp.shape[0] // tm
    spec_p, spec_s = _two_group_specs(tm, d, n_first)
    return pl.pallas_call(
        functools.partial(_inproj_kernel, n_first),
        grid=(t // tm, n // INPROJ_TN),
        in_specs=[spec_p, spec_s,
                  pl.BlockSpec((1, d), lambda i, j: (0, 0)),
                  pl.BlockSpec((d, INPROJ_TN), lambda i, j: (0, j)),
                  pl.BlockSpec((1, INPROJ_TN), lambda i, j: (0, j))],
        out_specs=pl.BlockSpec((tm, INPROJ_TN), lambda i, j: (i, j)),
        out_shape=jax.ShapeDtypeStruct((t, n), BF16),
        scratch_shapes=[pltpu.VMEM((tm, d), BF16)],
        compiler_params=pltpu.CompilerParams(dimension_semantics=("parallel", "arbitrary"),
                                             vmem_limit_bytes=BIG_VMEM_LIMIT),
        name="inproj",
    )(xp, xs, g, w, colscale)


NA_QROWS = 4
NA_KROWS = 12
NA_TILES = ROWS // NA_QROWS


def _na_tile_key_row(ti):
    return min(max(NA_QROWS * ti - NA_KH // 2, 0), ROWS - NA_KROWS)


def _na_bias_blocks(rpb):
    cols = jnp.arange(GRID_W)
    cstart = jnp.clip(cols - NA_KW // 2, 0, GRID_W - NA_KW)
    col_ok = (cols[None, :] >= cstart[:, None]) & (cols[None, :] < cstart[:, None] + NA_KW)
    dc = jnp.clip(cols[None, :] - cols[:, None], 1 - NA_KW, NA_KW - 1) + NA_KW - 1
    pick = (dc[None] == jnp.arange(2 * NA_KW - 1)[:, None, None]).astype(F32)
    t = jnp.einsum("hdx,xck->hdck", rpb.astype(F32), pick, precision=lax.Precision.HIGHEST)
    return jnp.where(col_ok[None, None], t, NEG_INF)


def _na_kernel(q_ref, k_ref, v_ref, t_ref, o_ref, bias_ref):
    scale = NA_HEAD_DIM ** -0.5
    tq = NA_QROWS * GRID_W
    tk = NA_KROWS * GRID_W

    @pl.when(pl.program_id(1) == 0)
    def _():
        masked = jnp.full((GRID_W, GRID_W), NEG_INF, F32)
        for variant, ti in enumerate((0, 1, NA_TILES - 1)):
            for jr in range(NA_QROWS):
                r = NA_QROWS * ti + jr
                rs = min(max(r - NA_KH // 2, 0), ROWS - NA_KH)
                blocks = []
                for i in range(NA_KROWS):
                    kr = _na_tile_key_row(ti) + i
                    blocks.append(t_ref[0, kr - r + NA_KH - 1] if rs <= kr < rs + NA_KH else masked)
                bias_ref[variant, jr * GRID_W:(jr + 1) * GRID_W, :] = jnp.concatenate(blocks, axis=1) * LOG2E

    def key_rows(ti):
        k0 = _na_tile_key_row(ti) * GRID_W
        return slice(k0, k0 + tk)

    def scores(ti):
        return _dot_nt(q_ref[0, ti * tq:(ti + 1) * tq, :], k_ref[0, key_rows(ti), :])

    pending = scores(0)
    for ti in range(NA_TILES):
        variant = 0 if ti == 0 else (2 if ti == NA_TILES - 1 else 1)
        qrows = slice(ti * tq, (ti + 1) * tq)
        krows = key_rows(ti)
        s = pending * (scale * LOG2E) + bias_ref[variant]
        if ti + 1 < NA_TILES:
            pending = scores(ti + 1)
        m = jnp.max(s, axis=-1, keepdims=True)
        e = jnp.exp2(s - m)
        l = jnp.sum(e, axis=-1, keepdims=True)
        o = jnp.dot(e.astype(BF16), v_ref[0, krows, :], preferred_element_type=F32)
        o_ref[0, qrows, :] = (o / l).astype(o_ref.dtype)


def _na_attention(proj3, blocks):
    b = proj3.shape[0]
    hd = NA_HEAD_DIM
    return pl.pallas_call(
        _na_kernel,
        grid=(NA_HEADS, b),
        in_specs=[pl.BlockSpec((1, SEQ, hd), lambda h, i: (i, 0, h)),
                  pl.BlockSpec((1, SEQ, hd), lambda h, i: (i, 0, NA_HEADS + h)),
                  pl.BlockSpec((1, SEQ, hd), lambda h, i: (i, 0, 2 * NA_HEADS + h)),
                  pl.BlockSpec((1,) + blocks.shape[1:], lambda h, i: (h, 0, 0, 0))],
        out_specs=pl.BlockSpec((1, SEQ, hd), lambda h, i: (i, 0, h)),
        out_shape=jax.ShapeDtypeStruct((b, SEQ, NA_WIDTH), BF16),
        scratch_shapes=[pltpu.VMEM((3, NA_QROWS * GRID_W, NA_KROWS * GRID_W), F32)],
        compiler_params=_cparams(("arbitrary", "arbitrary")),
        name="na_attn",
    )(proj3, proj3, proj3, blocks)


def _alibi_table(tq):
    slopes = jnp.exp2(-8.0 * jnp.arange(1, DIFF_HEADS + 1, dtype=F32) / DIFF_HEADS)
    r = jnp.arange(tq)[:, None]
    x = jnp.arange(2 * SEQ - tq)[None, :]
    dist = jnp.abs(r - x + (SEQ - tq)).astype(F32)
    return (slopes * LOG2E)[:, None, None] * dist[None]


def _diff_kernel(lq1_ref, lk1_ref, lq2_ref, lk2_ref, q_ref, k_ref, v_ref, g_ref, alibi_ref, o_ref):
    qi = pl.program_id(2)
    ts = alibi_ref.shape[1]
    nsub = q_ref.shape[1] // ts
    lam = (jnp.exp(jnp.sum(lq1_ref[...] * lk1_ref[...], axis=-1, keepdims=True))
           - jnp.exp(jnp.sum(lq2_ref[...] * lk2_ref[...], axis=-1, keepdims=True)) + LAMBDA_INIT)
    k = k_ref[0]

    def scores(u):
        q = q_ref[0, u * ts:(u + 1) * ts, :]
        return [_dot_nt(q[:, mi * DIFF_QK_DIM:(mi + 1) * DIFF_QK_DIM], k[:, mi * DIFF_QK_DIM:(mi + 1) * DIFF_QK_DIM])
                for mi in range(2)]

    def expo(s, bias):
        s = s - bias
        m = jnp.max(s, axis=-1, keepdims=True)
        e = jnp.exp2(s - m)
        return e, jnp.sum(e, axis=-1, keepdims=True)

    pending = scores(0)
    for u in range(nsub):
        s0, s1 = pending
        if u + 1 < nsub:
            pending = scores(u + 1)
        off = (SEQ - ts) - (qi * nsub + u) * ts
        bias = alibi_ref[0, :, pl.ds(pl.multiple_of(off, ts), SEQ)]
        e0, l0 = expo(s0, bias)
        e1, l1 = expo(s1, bias)
        a = (e0 - e1 * (lam * l0 / l1)).astype(BF16)
        o = jnp.dot(a, v_ref[0], preferred_element_type=F32) / l0
        o = _rms_scale(o, g_ref[...], SUBLN_EPS) * (1.0 - LAMBDA_INIT)
        o_ref[0, u * ts:(u + 1) * ts, :] = o.astype(o_ref.dtype)


def _diff_attention(proj3, lq1, lk1, lq2, lk2, subln_g):
    b = proj3.shape[0]
    tq = DIFF_TQ
    qoff = 3 * NA_WIDTH // LANES
    koff = qoff + DIFF_QK_WIDTH // LANES
    voff = koff + DIFF_QK_WIDTH // LANES
    vec = lambda n: pl.BlockSpec((1, n), lambda h, i, j: (0, 0))
    return pl.pallas_call(
        _diff_kernel,
        grid=(DIFF_HEADS, b, SEQ // tq),
        in_specs=[vec(DIFF_QK_DIM), vec(DIFF_QK_DIM), vec(DIFF_QK_DIM), vec(DIFF_QK_DIM),
                  pl.BlockSpec((1, tq, LANES), lambda h, i, j: (i, j, qoff + h)),
                  pl.BlockSpec((1, SEQ, LANES), lambda h, i, j: (i, 0, koff + h)),
                  pl.BlockSpec((1, SEQ, LANES), lambda h, i, j: (i, 0, voff + h)),
                  vec(DIFF_V_DIM),
                  pl.BlockSpec((1, DIFF_SUB, 2 * SEQ - DIFF_SUB), lambda h, i, j: (h, 0, 0))],
        out_specs=pl.BlockSpec((1, tq, DIFF_V_DIM), lambda h, i, j: (i, j, h)),
        out_shape=jax.ShapeDtypeStruct((b, SEQ, DIFF_V_WIDTH), BF16),
        compiler_params=_cparams(("arbitrary", "arbitrary", "arbitrary")),
        name="diff_attn",
    )(lq1, lk1, lq2, lk2, proj3, proj3, proj3, subln_g, _alibi_table(DIFF_SUB))


def _merge_kernel(n_first, a_ref, b_ref, gna_ref, gdf_ref, xp_ref, xs_ref, wna_ref, wdf_ref, wout_ref, o_ref):
    pa = jnp.dot(a_ref[...], wna_ref[...], preferred_element_type=F32)
    pb = jnp.dot(b_ref[...], wdf_ref[...], preferred_element_type=F32)
    merged = (jax.nn.sigmoid(gna_ref[...].astype(F32)) * pa
              + jax.nn.sigmoid(gdf_ref[...].astype(F32)) * pb)
    delta = jnp.dot(merged.astype(BF16), wout_ref[...], preferred_element_type=F32)
    in_first = pl.program_id(0) < n_first

    @pl.when(in_first)
    def _():
        o_ref[...] = xp_ref[...] + delta

    @pl.when(jnp.logical_not(in_first))
    def _():
        o_ref[...] = xs_ref[...] + delta


def _merge(a, b, proj, xp, xs, wna, wdf, wout):
    t = xp.shape[0] + xs.shape[0]
    tm = MERGE_TM
    gna_blk = (3 * NA_WIDTH + 2 * DIFF_QK_WIDTH + DIFF_V_WIDTH) // D_MODEL
    const = lambda shape: pl.BlockSpec(shape, lambda i: (0, 0), pipeline_mode=pl.Buffered(1))
    n_first = xp.shape[0] // tm
    spec_p, spec_s = _two_group_specs(tm, D_MODEL, n_first)
    return pl.pallas_call(
        functools.partial(_merge_kernel, n_first),
        grid=(t // tm,),
        in_specs=[pl.BlockSpec((tm, NA_WIDTH), lambda i: (i, 0)),
                  pl.BlockSpec((tm, DIFF_V_WIDTH), lambda i: (i, 0)),
                  pl.BlockSpec((tm, D_MODEL), lambda i: (i, gna_blk)),
                  pl.BlockSpec((tm, D_MODEL), lambda i: (i, gna_blk + 1)),
                  spec_p, spec_s,
                  const((NA_WIDTH, D_MODEL)), const((DIFF_V_WIDTH, D_MODEL)), const((D_MODEL, D_MODEL))],
        out_specs=pl.BlockSpec((tm, D_MODEL), lambda i: (i, 0)),
        out_shape=jax.ShapeDtypeStruct((t, D_MODEL), F32),
        compiler_params=pltpu.CompilerParams(dimension_semantics=("parallel",), vmem_limit_bytes=BIG_VMEM_LIMIT),
        name="merge_outproj",
    )(a, b, proj, proj, xp, xs, wna, wdf, wout)


def _memkv_kernel(n_first, mp_ref, ms_ref, g_ref, w_ref, o_ref):
    def project(m_ref):
        h = _rms_scale(m_ref[...], g_ref[...], NORM_EPS).astype(BF16)
        o_ref[...] = jnp.dot(h, w_ref[...], preferred_element_type=F32).astype(o_ref.dtype)

    in_first = pl.program_id(0) < n_first
    pl.when(in_first)(lambda: project(mp_ref))
    pl.when(jnp.logical_not(in_first))(lambda: project(ms_ref))


def _memkv(mem_p, mem_s, g, w):
    t = mem_p.shape[0] + mem_s.shape[0]
    n_first = mem_p.shape[0] // MEM_LEN
    spec_p, spec_s = _two_group_specs(MEM_LEN, D_MODEL, n_first)
    return pl.pallas_call(
        functools.partial(_memkv_kernel, n_first),
        grid=(t // MEM_LEN,),
        in_specs=[spec_p, spec_s,
                  pl.BlockSpec((1, D_MODEL), lambda i: (0, 0)),
                  pl.BlockSpec((D_MODEL, 2 * XA_WIDTH), lambda i: (0, 0))],
        out_specs=pl.BlockSpec((MEM_LEN, 2 * XA_WIDTH), lambda i: (i, 0)),
        out_shape=jax.ShapeDtypeStruct((t, 2 * XA_WIDTH), BF16),
        compiler_params=_cparams(("parallel",)),
        name="mem_kv",
    )(mem_p, mem_s, g, w)


def _route(logits):
    lane = lax.broadcasted_iota(I32, logits.shape, 1).astype(F32)
    ninf = -jnp.inf
    big = float(LANES)
    gl = jnp.where(lane < N_GROUPS, logits, ninf)
    gmax = jnp.max(gl, axis=-1, keepdims=True)
    g = jnp.min(jnp.where(gl == gmax, lane, big), axis=-1, keepdims=True)
    pg = 1.0 / jnp.sum(jnp.exp(gl - gmax), axis=-1, keepdims=True)
    lo = N_GROUPS + EXPERTS_PER_GROUP * g
    el = jnp.where((lane >= lo) & (lane < lo + EXPERTS_PER_GROUP), logits, ninf)
    v1 = jnp.max(el, axis=-1, keepdims=True)
    i1 = jnp.min(jnp.where(el == v1, lane, big), axis=-1, keepdims=True)
    el2 = jnp.where(lane == i1, ninf, el)
    v2 = jnp.max(el2, axis=-1, keepdims=True)
    i2 = jnp.min(jnp.where(el2 == v2, lane, big), axis=-1, keepdims=True)
    t = jnp.exp(v2 - v1)
    den = 1.0 + t
    gate1 = pg * (1.0 / den)
    gate2 = pg * (t / den)
    gates = jnp.where(lane == 0.0, gate1, jnp.where(lane == 1.0, gate2, 0.0))
    eids = jnp.where(lane == 0.0, i1 - N_GROUPS, jnp.where(lane == 1.0, i2 - N_GROUPS, 0.0)).astype(I32)
    return gates, eids


def _pack_halves(h):
    half = h.shape[1] // 2
    hi = pltpu.bitcast(h[:, :half].astype(BF16).astype(F32), U32)
    lo = pltpu.bitcast(h[:, half:].astype(BF16).astype(F32), U32)
    return hi | (lo >> 16)


def _unpack_halves(w):
    hi = pltpu.bitcast(w & jnp.uint32(0xFFFF0000), F32).astype(BF16)
    lo = pltpu.bitcast(w << 16, F32).astype(BF16)
    return jnp.concatenate([hi, lo], axis=1)


def _store_row_tiles(ref, x):
    n, width = x.shape
    c = width // LANES
    for j in range(c):
        ref[pl.ds(j, n, stride=c), :] = x[:, j * LANES:(j + 1) * LANES]


def _load_row_tiles(ref, n, c):
    return jnp.concatenate([ref[pl.ds(j, n, stride=c), :] for j in range(c)], axis=1)


def _xattn_kernel(x_ref, kv_ref, gx_ref, wq_ref, wo_ref, gf_ref, wr_ref, br_ref,
                  x2_ref, hp_ref, gate_ref, eid_ref):
    scale = XA_HEAD_DIM ** -0.5
    heads = [slice(h * XA_HEAD_DIM, (h + 1) * XA_HEAD_DIM) for h in range(XA_HEADS)]
    nsub = x_ref.shape[0] // XA_SUB

    def rows(u):
        return slice(u * XA_SUB, (u + 1) * XA_SUB)

    def scores(u):
        hq = _rms_scale(x_ref[rows(u), :], gx_ref[...], NORM_EPS).astype(BF16)
        q = jnp.dot(hq, wq_ref[...], preferred_element_type=F32).astype(BF16)
        return [_dot_nt(q[:, sl], kv_ref[:, sl]) for sl in heads]

    def finish(u, sc):
        outs = []
        for h in range(XA_HEADS):
            vsl = slice(XA_WIDTH + h * XA_HEAD_DIM, XA_WIDTH + (h + 1) * XA_HEAD_DIM)
            s = sc[h] * (scale * LOG2E)
            m = jnp.max(s, axis=-1, keepdims=True)
            e = jnp.exp2(s - m)
            l = jnp.sum(e, axis=-1, keepdims=True)
            o = jnp.dot(e.astype(BF16), kv_ref[:, vsl], preferred_element_type=F32)
            outs.append((o / l).astype(BF16))
        o = jnp.concatenate(outs, axis=1)
        x2 = x_ref[rows(u), :] + jnp.dot(o, wo_ref[...], preferred_element_type=F32)
        x2_ref[rows(u), :] = x2
        h3 = _rms_scale(x2, gf_ref[...], NORM_EPS)
        _store_row_tiles(hp_ref.at[pl.ds(u * XA_SUB * HP_TILES, XA_SUB * HP_TILES)], _pack_halves(h3))
        logits = jnp.dot(h3.astype(BF16), wr_ref[...], preferred_element_type=F32) + br_ref[...]
        gates, eids = _route(logits)
        gate_ref[rows(u), :] = gates
        eid_ref[rows(u), :] = eids

    pending = scores(0)
    for u in range(nsub):
        sc = pending
        if u + 1 < nsub:
            pending = scores(u + 1)
        finish(u, sc)


def _xattn(x1, kv, gx, wq, wo, gf, wr, br):
    t = x1.shape[0]
    tm = XA_TM
    per_batch = SEQ // tm
    const = lambda shape: pl.BlockSpec(shape, lambda i: (0, 0), pipeline_mode=pl.Buffered(1))
    tile = lambda n: pl.BlockSpec((tm, n), lambda i: (i, 0))
    return pl.pallas_call(
        _xattn_kernel,
        grid=(t // tm,),
        in_specs=[tile(D_MODEL),
                  pl.BlockSpec((MEM_LEN, 2 * XA_WIDTH), lambda i: (i // per_batch, 0)),
                  const((1, D_MODEL)), const((D_MODEL, XA_WIDTH)), const((XA_WIDTH, D_MODEL)),
                  const((1, D_MODEL)), const((D_MODEL, LANES)), const((1, LANES))],
        out_specs=[tile(D_MODEL), pl.BlockSpec((tm * HP_TILES, LANES), lambda i: (i, 0)), tile(LANES), tile(LANES)],
        out_shape=[jax.ShapeDtypeStruct((t, D_MODEL), F32),
                   jax.ShapeDtypeStruct((t * HP_TILES, LANES), U32),
                   jax.ShapeDtypeStruct((t, LANES), F32),
                   jax.ShapeDtypeStruct((t, LANES), I32)],
        compiler_params=pltpu.CompilerParams(dimension_semantics=("parallel",), vmem_limit_bytes=BIG_VMEM_LIMIT),
        name="xattn_router",
    )(x1, kv, gx, wq, wo, gf, wr, br)


IDX_ROWS = 8
PIECE = 8


def _rank_kernel(eid_ref, gate_ref, idx_ref, gidx_ref, tile_ref, cnt_ref, carry_ref):
    @pl.when(pl.program_id(0) == 0)
    def _():
        carry_ref[...] = jnp.zeros_like(carry_ref)

    tm = eid_ref.shape[0]
    eid = eid_ref[...]
    lane = lax.broadcasted_iota(I32, (tm, LANES), 1)
    e1 = eid[:, 0:1]
    e2 = eid[:, 1:2]
    m1 = lane == e1
    m2 = lane == e2
    onehot = jnp.where(m1, 1.0, jnp.where(m2, 1.0, 0.0))
    r = lax.broadcasted_iota(I32, (tm, tm), 0)
    c = lax.broadcasted_iota(I32, (tm, tm), 1)
    tri = jnp.where(c < r, 1.0, 0.0).astype(BF16)
    local = jnp.dot(tri, onehot.astype(BF16), preferred_element_type=F32)
    carry = carry_ref[0:1, :]
    n_tile = jnp.sum(onehot, axis=0, keepdims=True)

    n_piece = jnp.floor((n_tile + (PIECE - 1)) * (1.0 / PIECE)) * PIECE

    def assignment(m, e):
        r_local = jnp.sum(jnp.where(m, local, 0.0), axis=-1, keepdims=True)
        r_global = r_local + jnp.sum(jnp.where(m, carry, 0.0), axis=-1, keepdims=True)
        first = jnp.sum(jnp.where(lane < e, n_tile, 0.0), axis=-1, keepdims=True)
        first_piece = jnp.sum(jnp.where(lane < e, n_piece, 0.0), axis=-1, keepdims=True)
        return first + r_local, r_global, first_piece + r_local

    p1, g1, q1 = assignment(m1, e1)
    p2, g2, q2 = assignment(m2, e2)
    cols = [p1, p2, g1, g2, e1.astype(F32), e2.astype(F32), q1, q2]
    table = jnp.zeros((tm, LANES), F32)
    for j, col in enumerate(cols):
        table = jnp.where(lane == j, col, table)
    idx_ref[0] = jnp.transpose(table)[0:IDX_ROWS, :].astype(I32)
    gidx_ref[0] = jnp.transpose(gate_ref[...])[0:IDX_ROWS, :]
    tile_ref[0] = jnp.where(lax.broadcasted_iota(I32, (8, LANES), 0) == 0, carry, n_tile)
    carry_ref[...] = carry_ref[...] + n_tile
    cnt_ref[...] = carry_ref[...]


def _rank(eid, gates):
    t = eid.shape[0]
    tm = RANK_TM
    n_tiles = t // tm
    return pl.pallas_call(
        _rank_kernel,
        grid=(n_tiles,),
        in_specs=[pl.BlockSpec((tm, LANES), lambda i: (i, 0)), pl.BlockSpec((tm, LANES), lambda i: (i, 0))],
        out_specs=[pl.BlockSpec((1, IDX_ROWS, tm), lambda i: (i, 0, 0)),
                   pl.BlockSpec((1, IDX_ROWS, tm), lambda i: (i, 0, 0)),
                   pl.BlockSpec((1, 8, LANES), lambda i: (i, 0, 0)),
                   pl.BlockSpec((8, LANES), lambda i: (0, 0))],
        out_shape=[jax.ShapeDtypeStruct((n_tiles, IDX_ROWS, tm), I32),
                   jax.ShapeDtypeStruct((n_tiles, IDX_ROWS, tm), F32),
                   jax.ShapeDtypeStruct((n_tiles, 8, LANES), F32),
                   jax.ShapeDtypeStruct((8, LANES), F32)],
        scratch_shapes=[pltpu.VMEM((8, LANES), F32)],
        compiler_params=_cparams(("arbitrary",)),
        name="expert_rank",
    )(eid, gates)


def _dispatch_kernel(n_steps, cnt_ref, pend_ref, nused_ref, pos_ref, run_ref, run_prev_ref, hp_ref, xs_ref, sbuf,
                     zero_ref, sem, zsem):
    i = pl.program_id(0)
    tm = hp_ref.shape[0] // (2 * HP_TILES)
    zb = zero_ref.shape[0]
    n_blocks = xs_ref.shape[0] // zb

    @pl.when(i == 0)
    def _():
        zero_ref[...] = jnp.zeros_like(zero_ref)
        for b in range(2):
            sbuf[b, pl.ds(TOP_K * tm * HP_TILES, PIECE * HP_TILES), :] = jnp.zeros((PIECE * HP_TILES, LANES), U32)

        def zero_copy(block):
            return pltpu.make_async_copy(zero_ref, xs_ref.at[pl.ds(pl.multiple_of(block * zb, zb), zb)], zsem)

        def tail_start(j, carry):
            zero_copy(j).start()
            return carry

        def tail_wait(j, carry):
            zero_copy(j).wait()
            return carry

        for e in range(N_EXPERTS):
            @pl.when(cnt_ref[e] > 0)
            def _():
                zero_copy(pend_ref[e] - 1).start()
        lax.fori_loop(nused_ref[0], n_blocks, tail_start, 0)
        for e in range(N_EXPERTS):
            @pl.when(cnt_ref[e] > 0)
            def _():
                zero_copy(pend_ref[e] - 1).wait()
        lax.fori_loop(nused_ref[0], n_blocks, tail_wait, 0)

    def place(half):
        def body(t, carry):
            row = hp_ref[pl.ds(pl.multiple_of((half * tm + t) * HP_TILES, HP_TILES), HP_TILES), :]
            for k in range(TOP_K):
                sbuf[half, pl.ds(pl.multiple_of(pos_ref[half, k, t] * HP_TILES, HP_TILES), HP_TILES), :] = row
            return carry
        lax.fori_loop(0, tm, body, 0, unroll=8)

    def piece_copy(src_tok, dst_slot, half):
        rows = PIECE * HP_TILES
        return pltpu.make_async_copy(sbuf.at[half, pl.ds(pl.multiple_of(src_tok * HP_TILES, HP_TILES), rows)],
                                     xs_ref.at[pl.ds(pl.multiple_of(dst_slot * HP_TILES, HP_TILES), rows)],
                                     sem.at[half])

    def issue(half):
        for e in range(N_EXPERTS):
            slot0 = run_ref[half, 0, e]
            first = run_ref[half, 0, 2 * N_EXPERTS + e]

            def run_piece(j, carry):
                piece_copy(first + PIECE * j, slot0 + PIECE * j, half).start()
                return carry

            lax.fori_loop(0, run_ref[half, 0, N_EXPERTS + e], run_piece, 0)

    def drain(runs, half):
        def wait_piece(j, carry):
            piece_copy(0, 0, half).wait()
            return carry
        lax.fori_loop(0, runs[half, 0, 3 * N_EXPERTS], wait_piece, 0)

    place(0)

    @pl.when(i > 0)
    def _():
        drain(run_prev_ref, 1)
    issue(0)
    place(1)
    drain(run_ref, 0)
    issue(1)

    @pl.when(i == n_steps - 1)
    def _():
        drain(run_ref, 1)


def _dispatch(cnt, pend_blocks, nused, pos, runs, hp, n_slots):
    n_tiles, _, tm = pos.shape
    n_steps = n_tiles // 2
    grid_spec = pltpu.PrefetchScalarGridSpec(
        num_scalar_prefetch=3,
        grid=(n_steps,),
        in_specs=[pl.BlockSpec((2, TOP_K, tm), lambda i, *_: (i, 0, 0), memory_space=pltpu.SMEM),
                  pl.BlockSpec((2, 1, LANES), lambda i, *_: (i, 0, 0), memory_space=pltpu.SMEM),
                  pl.BlockSpec((2, 1, LANES), lambda i, *_: (jnp.maximum(i - 1, 0), 0, 0), memory_space=pltpu.SMEM),
                  pl.BlockSpec((2 * tm * HP_TILES, LANES), lambda i, *_: (i, 0))],
        out_specs=pl.BlockSpec(memory_space=pl.ANY),
        scratch_shapes=[pltpu.VMEM((2, (TOP_K * tm + PIECE) * HP_TILES, LANES), U32),
                        pltpu.VMEM((MOE_TM * HP_TILES, LANES), U32),
                        pltpu.SemaphoreType.DMA((2,)), pltpu.SemaphoreType.DMA(())],
    )
    return pl.pallas_call(
        functools.partial(_dispatch_kernel, n_steps),
        grid_spec=grid_spec,
        out_shape=jax.ShapeDtypeStruct((n_slots * HP_TILES, LANES), U32),
        compiler_params=_cparams(("arbitrary",)),
        name="moe_dispatch",
    )(cnt, pend_blocks, nused, pos, runs, runs, hp)


MOE_DOWN_CHUNKS = 4


def _moe_kernel(bexp_ref, nused_ref, xs_ref, wg_ref, wu_ref, wd_ref, o_ref, wgb, wub, wdb):
    i = pl.program_id(0)
    nused = nused_ref[0]
    tm = xs_ref.shape[0] // HP_TILES

    @pl.when((i < nused) & ((i == 0) | (bexp_ref[i] != bexp_ref[jnp.maximum(i - 1, 0)])))
    def _():
        wgb[...] = wg_ref[0].astype(BF16)
        wub[...] = wu_ref[0].astype(BF16)
        wdb[...] = wd_ref[0].astype(BF16)

    @pl.when(i < nused)
    def _():
        x = _unpack_halves(_load_row_tiles(xs_ref, tm, HP_TILES))
        hg = jnp.dot(x, wgb[...], preferred_element_type=F32)
        hu = jnp.dot(x, wub[...], preferred_element_type=F32)
        hdn = ((hg * jax.nn.sigmoid(hg)) * hu).astype(BF16)
        cw = D_MODEL // MOE_DOWN_CHUNKS
        for c in range(MOE_DOWN_CHUNKS):
            y = jnp.dot(hdn, wdb[:, c * cw:(c + 1) * cw], preferred_element_type=F32)
            for j in range(cw // LANES):
                o_ref[pl.ds(c * (cw // LANES) + j, tm, stride=YS_TILES), :] = y[:, j * LANES:(j + 1) * LANES]

    @pl.when(i >= nused)
    def _():
        o_ref[...] = jnp.zeros_like(o_ref)


def _moe(bexp, nused, xs, wg, wu, wd):
    tm = MOE_TM
    n_slots = xs.shape[0] // HP_TILES
    n_blocks = n_slots // tm
    grid_spec = pltpu.PrefetchScalarGridSpec(
        num_scalar_prefetch=2,
        grid=(n_blocks,),
        in_specs=[pl.BlockSpec((tm * HP_TILES, LANES), lambda i, be, nu: (jnp.minimum(i, nu[0] - 1), 0)),
                  pl.BlockSpec((1, D_MODEL, D_EXPERT), lambda i, be, nu: (be[i], 0, 0)),
                  pl.BlockSpec((1, D_MODEL, D_EXPERT), lambda i, be, nu: (be[i], 0, 0)),
                  pl.BlockSpec((1, D_EXPERT, D_MODEL), lambda i, be, nu: (be[i], 0, 0))],
        out_specs=pl.BlockSpec((tm * YS_TILES, LANES), lambda i, be, nu: (i, 0)),
        scratch_shapes=[pltpu.VMEM((D_MODEL, D_EXPERT), BF16), pltpu.VMEM((D_MODEL, D_EXPERT), BF16),
                        pltpu.VMEM((D_EXPERT, D_MODEL), BF16)],
    )
    return pl.pallas_call(
        _moe_kernel,
        grid_spec=grid_spec,
        out_shape=jax.ShapeDtypeStruct((n_slots * YS_TILES, LANES), F32),
        compiler_params=_cparams(("arbitrary",)),
        name="moe_experts",
    )(bexp, nused, xs, wg, wu, wd)


def _combine_kernel(n_tiles, pos_ref, run_ref, run_next_ref, gate_ref, x_ref, g_ref, ys_ref, o_ref, ybuf, mbuf, sem):
    i = pl.program_id(0)
    tm = x_ref.shape[0]
    slot = lax.rem(i, 2)

    def piece_copy(src_slot, dst_pos, buf):
        rows = PIECE * YS_TILES
        return pltpu.make_async_copy(ys_ref.at[pl.ds(pl.multiple_of(src_slot * YS_TILES, YS_TILES), rows)],
                                     ybuf.at[buf, pl.ds(pl.multiple_of(dst_pos * YS_TILES, YS_TILES), rows)],
                                     sem.at[buf])

    def request(runs, buf):
        for e in range(N_EXPERTS):
            slot0 = runs[0, 0, e]
            first = runs[0, 0, 2 * N_EXPERTS + e]

            def run_piece(j, carry):
                piece_copy(slot0 + PIECE * j, first + PIECE * j, buf).start()
                return carry

            lax.fori_loop(0, runs[0, 0, N_EXPERTS + e], run_piece, 0)

    @pl.when(i == 0)
    def _():
        request(run_ref, 0)

    @pl.when(i + 1 < n_tiles)
    def _():
        request(run_next_ref, 1 - slot)

    def drain(j, carry):
        piece_copy(0, 0, slot).wait()
        return carry

    lax.fori_loop(0, run_ref[0, 0, 3 * N_EXPERTS], drain, 0)

    def token(t, carry):
        acc = None
        for k in range(TOP_K):
            rows = pl.ds(pl.multiple_of(pos_ref[0, k, t] * YS_TILES, YS_TILES), YS_TILES)
            term = ybuf[slot, rows, :] * gate_ref[0, k, t]
            acc = term if acc is None else acc + term
        mbuf[pl.ds(pl.multiple_of(t * YS_TILES, YS_TILES), YS_TILES), :] = acc
        return carry

    lax.fori_loop(0, tm, token, 0, unroll=8)
    moe = _load_row_tiles(mbuf, tm, YS_TILES)
    o_ref[...] = _rms_scale(x_ref[...] + moe, g_ref[...], NORM_EPS)


def _combine(pos, runs, gidx, x2, g, ys, row0, n_rows):
    _, _, tm = pos.shape
    tile0 = row0 // tm
    n = n_rows // tm
    smem = lambda rows: pl.BlockSpec((1, rows, tm), lambda i: (tile0 + i, 0, 0), memory_space=pltpu.SMEM)
    run_spec = lambda ahead: pl.BlockSpec((1, 1, LANES), lambda i: (tile0 + jnp.minimum(i + ahead, n - 1), 0, 0),
                                          memory_space=pltpu.SMEM)
    return pl.pallas_call(
        functools.partial(_combine_kernel, n),
        grid=(n,),
        in_specs=[smem(TOP_K), run_spec(0), run_spec(1), smem(TOP_K),
                  pl.BlockSpec((tm, D_MODEL), lambda i: (tile0 + i, 0)),
                  pl.BlockSpec((1, D_MODEL), lambda i: (0, 0)),
                  pl.BlockSpec(memory_space=pl.ANY)],
        out_specs=pl.BlockSpec((tm, D_MODEL), lambda i: (i, 0)),
        out_shape=jax.ShapeDtypeStruct((n_rows, D_MODEL), F32),
        scratch_shapes=[pltpu.VMEM((2, (TOP_K * tm + N_EXPERTS * PIECE) * YS_TILES, LANES), F32),
                        pltpu.VMEM((tm * YS_TILES, LANES), F32),
                        pltpu.SemaphoreType.DMA((2,))],
        compiler_params=_cparams(("arbitrary",)),
        name="moe_combine",
    )(pos, runs, runs, gidx[:, 0:TOP_K, :], x2, g, ys)


def _routing_plan(idx, tiles, counts):
    cnt = counts[0, :N_EXPERTS].astype(I32)
    padded = jnp.where(cnt > 0, (cnt + PIECE - 1 + MOE_TM - 1) // MOE_TM * MOE_TM, 0)
    pends = jnp.cumsum(padded)
    pstarts = pends - padded
    n_tiles, _, tm = idx.shape
    n_blocks = n_tiles * tm * TOP_K // MOE_TM + N_EXPERTS + 1
    blk0 = jnp.arange(n_blocks, dtype=I32) * MOE_TM
    bexp = jnp.minimum(jnp.sum(pends[None, :] <= blk0[:, None], axis=1), N_EXPERTS - 1).astype(I32)
    nused = (pends[-1:] // MOE_TM).astype(I32)
    before = tiles[:, 0, :N_EXPERTS].astype(I32)
    inside = tiles[:, 1, :N_EXPERTS].astype(I32)
    first = jnp.cumsum(inside, axis=1) - inside
    pieces = (inside + PIECE - 1) // PIECE
    first_piece = (jnp.cumsum(pieces, axis=1) - pieces) * PIECE

    def run_table(first_pos):
        return jnp.concatenate([pstarts[None, :] + before, pieces, first_pos, jnp.sum(pieces, axis=1, keepdims=True),
                                jnp.zeros((n_tiles, LANES - 3 * N_EXPERTS - 1), I32)], axis=1).reshape(n_tiles, 1, LANES)

    pos = idx[:, 0:TOP_K, :]
    pos_piece = idx[:, 3 * TOP_K:4 * TOP_K, :]
    return pos, run_table(first), pos_piece, run_table(first_piece), cnt, (pends // MOE_TM).astype(I32), bexp, nused, n_blocks * MOE_TM


def kernel(x_prompt, x_sample, mem_prompt, mem_sample, mix_norm_g, w_in, na_rpb, diff_lambda_q1, diff_lambda_k1, diff_lambda_q2, diff_lambda_k2, diff_subln_g, w_branch_na, w_branch_diff, w_out, xa_norm_g, mem_norm_g, xa_w_q, xa_w_kv, xa_w_o, ffn_norm_g, router_group_w, router_group_b, router_expert_w, router_expert_b, w_gate, w_up, w_down, final_norm_g):
    nb_p, nb_s = x_prompt.shape[0], x_sample.shape[0]
    nb = nb_p + nb_s
    t = nb * SEQ
    xp = x_prompt.reshape(nb_p * SEQ, D_MODEL)
    xs_in = x_sample.reshape(nb_s * SEQ, D_MODEL)
    mem_p = mem_prompt.reshape(nb_p * MEM_LEN, D_MODEL)
    mem_s = mem_sample.reshape(nb_s * MEM_LEN, D_MODEL)
    row = lambda v: v.reshape(1, -1).astype(F32)
    bf = lambda w: w.astype(BF16)

    qcol = 3 * NA_WIDTH
    colscale = jnp.ones((1, IN_COLS), F32).at[:, qcol:qcol + DIFF_QK_WIDTH].set(DIFF_QK_DIM ** -0.5 * LOG2E)
    proj = _inproj(xp, xs_in, row(mix_norm_g[0]), bf(w_in[0]), colscale)
    proj3 = proj.reshape(nb, SEQ, IN_COLS)
    a = _na_attention(proj3, _na_bias_blocks(na_rpb[0]))
    b = _diff_attention(proj3, row(diff_lambda_q1[0]), row(diff_lambda_k1[0]),
                        row(diff_lambda_q2[0]), row(diff_lambda_k2[0]), row(diff_subln_g[0]))
    x1 = _merge(a.reshape(t, NA_WIDTH), b.reshape(t, DIFF_V_WIDTH), proj, xp, xs_in,
                bf(w_branch_na[0]), bf(w_branch_diff[0]), bf(w_out[0]))
    kv = _memkv(mem_p, mem_s, row(mem_norm_g[0]), bf(xa_w_kv[0]))
    w_router = jnp.zeros((D_MODEL, LANES), F32)
    w_router = w_router.at[:, :N_GROUPS].set(router_group_w[0]).at[:, N_GROUPS:N_GROUPS + N_EXPERTS].set(router_expert_w[0])
    b_router = jnp.zeros((1, LANES), F32)
    b_router = b_router.at[0, :N_GROUPS].set(router_group_b[0]).at[0, N_GROUPS:N_GROUPS + N_EXPERTS].set(router_expert_b[0])
    x2, hp, gates, eid = _xattn(x1, kv, row(xa_norm_g[0]), bf(xa_w_q[0]), bf(xa_w_o[0]),
                                row(ffn_norm_g[0]), bf(w_router), b_router)
    idx, gidx, tiles, counts = _rank(eid, gates)
    pos, runs, pos_c, runs_c, cnt, pend_blocks, bexp, nused, n_slots = _routing_plan(idx, tiles, counts)
    xs = _dispatch(cnt, pend_blocks, nused, pos, runs, hp, n_slots)
    ys = _moe(bexp, nused, xs, w_gate[0], w_up[0], w_down[0])
    fg = row(final_norm_g)
    y_p = _combine(pos_c, runs_c, gidx, x2, fg, ys, 0, nb_p * SEQ).reshape(nb_p, SEQ, D_MODEL)
    y_s = _combine(pos_c, runs_c, gidx, x2, fg, ys, nb_p * SEQ, nb_s * SEQ).reshape(nb_s, SEQ, D_MODEL)
    return (y_p, y_s)
```

```python
import functools
import math

import jax
import jax.numpy as jnp
from jax import lax
from jax.experimental import pallas as pl
from jax.experimental.pallas import tpu as pltpu

F32 = jnp.float32
BF16 = jnp.bfloat16
I32 = jnp.int32
U32 = jnp.uint32

D_MODEL = 2048
SEQ = 2048
GRID_W = 64
ROWS = SEQ // GRID_W
NA_HEADS = 8
NA_HEAD_DIM = 128
NA_WIDTH = NA_HEADS * NA_HEAD_DIM
NA_KH = 8
NA_KW = 16
DIFF_HEADS = 8
DIFF_QK_DIM = 64
DIFF_V_DIM = 128
DIFF_QK_WIDTH = DIFF_HEADS * 2 * DIFF_QK_DIM
DIFF_V_WIDTH = DIFF_HEADS * DIFF_V_DIM
IN_COLS = 3 * NA_WIDTH + 2 * DIFF_QK_WIDTH + DIFF_V_WIDTH + 2 * D_MODEL
MEM_LEN = 256
XA_HEADS = 4
XA_HEAD_DIM = 128
XA_WIDTH = XA_HEADS * XA_HEAD_DIM
N_GROUPS = 4
EXPERTS_PER_GROUP = 8
N_EXPERTS = N_GROUPS * EXPERTS_PER_GROUP
TOP_K = 2
D_EXPERT = 512
NORM_EPS = 1e-6
SUBLN_EPS = 1e-5
NEG_INF = -1e30
LAMBDA_INIT = 0.8 - 0.6 * math.exp(-0.3 * 0)
LOG2E = math.log2(math.e)

LANES = 128
HP_TILES = D_MODEL // 2 // LANES
YS_TILES = D_MODEL // LANES
VMEM_LIMIT = 56 * 1024 * 1024
BIG_VMEM_LIMIT = 60 * 1024 * 1024

INPROJ_TM = 1024
INPROJ_TN = 1280
NORM_CHUNK = 128
DIFF_TQ = 2048
DIFF_SUB = 128
MERGE_TM = 512
XA_TM = 1024
XA_SUB = 256
RANK_TM = 512
MOE_TM = 512


def _cparams(sem):
    return pltpu.CompilerParams(dimension_semantics=sem, vmem_limit_bytes=VMEM_LIMIT)


def _rms_scale(x, g, eps):
    ms = jnp.mean(x * x, axis=-1, keepdims=True)
    return x * lax.rsqrt(ms + eps) * g


def _dot_nt(a, b):
    return lax.dot_general(a, b, (((1,), (1,)), ((), ())), preferred_element_type=F32)


def _two_group_specs(tm, width, n_first):
    first = pl.BlockSpec((tm, width), lambda i, *_: (jnp.minimum(i, n_first - 1), 0))
    second = pl.BlockSpec((tm, width), lambda i, *_: (jnp.maximum(i - n_first, 0), 0))
    return first, second


def _inproj_kernel(n_first, xp_ref, xs_ref, g_ref, w_ref, cs_ref, o_ref, h_ref):
    def normalise(x_ref):
        def chunk(c, carry):
            rows = pl.ds(pl.multiple_of(c * NORM_CHUNK, NORM_CHUNK), NORM_CHUNK)
            h_ref[rows, :] = _rms_scale(x_ref[rows, :], g_ref[...], NORM_EPS).astype(BF16)
            return carry
        lax.fori_loop(0, x_ref.shape[0] // NORM_CHUNK, chunk, 0)

    first_col = pl.program_id(1) == 0
    in_first = pl.program_id(0) < n_first
    pl.when(first_col & in_first)(lambda: normalise(xp_ref))
    pl.when(first_col & jnp.logical_not(in_first))(lambda: normalise(xs_ref))
    acc = jnp.dot(h_ref[...], w_ref[...], preferred_element_type=F32)
    o_ref[...] = (acc * cs_ref[...]).astype(o_ref.dtype)


def _inproj(xp, xs, g, w, colscale):
    d = xp.shape[1]
    t = xp.shape[0] + xs.shape[0]
    n = w.shape[1]
    tm = INPROJ_TM
    n_first = xp.shape[0] // tm
    spec_p, spec_s = _two_group_specs(tm, d, n_first)
    return pl.pallas_call(
        functools.partial(_inproj_kernel, n_first),
        grid=(t // tm, n // INPROJ_TN),
        in_specs=[spec_p, spec_s,
                  pl.BlockSpec((1, d), lambda i, j: (0, 0)),
                  pl.BlockSpec((d, INPROJ_TN), lambda i, j: (0, j)),
                  pl.BlockSpec((1, INPROJ_TN), lambda i, j: (0, j))],
        out_specs=pl.BlockSpec((tm, INPROJ_TN), lambda i, j: (i, j)),
        out_shape=jax.ShapeDtypeStruct((t, n), BF16),
        scratch_shapes=[pltpu.VMEM((tm, d), BF16)],
        compiler_params=pltpu.CompilerParams(dimension_semantics=("parallel", "arbitrary"),
                                             vmem_limit_bytes=BIG_VMEM_LIMIT),
        name="inproj",
    )(xp, xs, g, w, colscale)


NA_QROWS = 4
NA_KROWS = 12
NA_TILES = ROWS // NA_QROWS


def _na_tile_key_row(ti):
    return min(max(NA_QROWS * ti - NA_KH // 2, 0), ROWS - NA_KROWS)


def _na_bias_blocks(rpb):
    cols = jnp.arange(GRID_W)
    cstart = jnp.clip(cols - NA_KW // 2, 0, GRID_W - NA_KW)
    col_ok = (cols[None, :] >= cstart[:, None]) & (cols[None, :] < cstart[:, None] + NA_KW)
    dc = jnp.clip(cols[None, :] - cols[:, None], 1 - NA_KW, NA_KW - 1) + NA_KW - 1
    pick = (dc[None] == jnp.arange(2 * NA_KW - 1)[:, None, None]).astype(F32)
    t = jnp.einsum("hdx,xck->hdck", rpb.astype(F32), pick, precision=lax.Precision.HIGHEST)
    return jnp.where(col_ok[None, None], t, NEG_INF)


def _na_kernel(q_ref, k_ref, v_ref, t_ref, o_ref, bias_ref):
    scale = NA_HEAD_DIM ** -0.5
    tq = NA_QROWS * GRID_W
    tk = NA_KROWS * GRID_W

    @pl.when(pl.program_id(1) == 0)
    def _():
        masked = jnp.full((GRID_W, GRID_W), NEG_INF, F32)
        for variant, ti in enumerate((0, 1, NA_TILES - 1)):
            for jr in range(NA_QROWS):
                r = NA_QROWS * ti + jr
                rs = min(max(r - NA_KH // 2, 0), ROWS - NA_KH)
                blocks = []
                for i in range(NA_KROWS):
                    kr = _na_tile_key_row(ti) + i
                    blocks.append(t_ref[0, kr - r + NA_KH - 1] if rs <= kr < rs + NA_KH else masked)
                bias_ref[variant, jr * GRID_W:(jr + 1) * GRID_W, :] = jnp.concatenate(blocks, axis=1) * LOG2E

    def key_rows(ti):
        k0 = _na_tile_key_row(ti) * GRID_W
        return slice(k0, k0 + tk)

    def scores(ti):
        return _dot_nt(q_ref[0, ti * tq:(ti + 1) * tq, :], k_ref[0, key_rows(ti), :])

    pending = scores(0)
    for ti in range(NA_TILES):
        variant = 0 if ti == 0 else (2 if ti == NA_TILES - 1 else 1)
        qrows = slice(ti * tq, (ti + 1) * tq)
        krows = key_rows(ti)
        s = pending * (scale * LOG2E) + bias_ref[variant]
        if ti + 1 < NA_TILES:
            pending = scores(ti + 1)
        m = jnp.max(s, axis=-1, keepdims=True)
        e = jnp.exp2(s - m)
        l = jnp.sum(e, axis=-1, keepdims=True)
        o = jnp.dot(e.astype(BF16), v_ref[0, krows, :], preferred_element_type=F32)
        o_ref[0, qrows, :] = (o / l).astype(o_ref.dtype)


def _na_attention(proj3, blocks):
    b = proj3.shape[0]
    hd = NA_HEAD_DIM
    return pl.pallas_call(
        _na_kernel,
        grid=(NA_HEADS, b),
        in_specs=[pl.BlockSpec((1, SEQ, hd), lambda h, i: (i, 0, h)),
                  pl.BlockSpec((1, SEQ, hd), lambda h, i: (i, 0, NA_HEADS + h)),
                  pl.BlockSpec((1, SEQ, hd), lambda h, i: (i, 0, 2 * NA_HEADS + h)),
                  pl.BlockSpec((1,) + blocks.shape[1:], lambda h, i: (h, 0, 0, 0))],
        out_specs=pl.BlockSpec((1, SEQ, hd), lambda h, i: (i, 0, h)),
        out_shape=jax.ShapeDtypeStruct((b, SEQ, NA_WIDTH), BF16),
        scratch_shapes=[pltpu.VMEM((3, NA_QROWS * GRID_W, NA_KROWS * GRID_W), F32)],
        compiler_params=_cparams(("arbitrary", "arbitrary")),
        name="na_attn",
    )(proj3, proj3, proj3, blocks)


def _alibi_table(tq):
    slopes = jnp.exp2(-8.0 * jnp.arange(1, DIFF_HEADS + 1, dtype=F32) / DIFF_HEADS)
    r = jnp.arange(tq)[:, None]
    x = jnp.arange(2 * SEQ - tq)[None, :]
    dist = jnp.abs(r - x + (SEQ - tq)).astype(F32)
    return (slopes * LOG2E)[:, None, None] * dist[None]


def _diff_kernel(lq1_ref, lk1_ref, lq2_ref, lk2_ref, q_ref, k_ref, v_ref, g_ref, alibi_ref, o_ref):
    qi = pl.program_id(2)
    ts = alibi_ref.shape[1]
    nsub = q_ref.shape[1] // ts
    lam = (jnp.exp(jnp.sum(lq1_ref[...] * lk1_ref[...], axis=-1, keepdims=True))
           - jnp.exp(jnp.sum(lq2_ref[...] * lk2_ref[...], axis=-1, keepdims=True)) + LAMBDA_INIT)
    k = k_ref[0]

    def scores(u):
        q = q_ref[0, u * ts:(u + 1) * ts, :]
        return [_dot_nt(q[:, mi * DIFF_QK_DIM:(mi + 1) * DIFF_QK_DIM], k[:, mi * DIFF_QK_DIM:(mi + 1) * DIFF_QK_DIM])
                for mi in range(2)]

    def expo(s, bias):
        s = s - bias
        m = jnp.max(s, axis=-1, keepdims=True)
        e = jnp.exp2(s - m)
        return e, jnp.sum(e, axis=-1, keepdims=True)

    pending = scores(0)
    for u in range(nsub):
        s0, s1 = pending
        if u + 1 < nsub:
            pending = scores(u + 1)
        off = (SEQ - ts) - (qi * nsub + u) * ts
        bias = alibi_ref[0, :, pl.ds(pl.multiple_of(off, ts), SEQ)]
        e0, l0 = expo(s0, bias)
        e1, l1 = expo(s1, bias)
        a = (e0 - e1 * (lam * l0 / l1)).astype(BF16)
        o = jnp.dot(a, v_ref[0], preferred_element_type=F32) / l0
        o = _rms_scale(o, g_ref[...], SUBLN_EPS) * (1.0 - LAMBDA_INIT)
        o_ref[0, u * ts:(u + 1) * ts, :] = o.astype(o_ref.dtype)


def _diff_attention(proj3, lq1, lk1, lq2, lk2, subln_g):
    b = proj3.shape[0]
    tq = DIFF_TQ
    qoff = 3 * NA_WIDTH // LANES
    koff = qoff + DIFF_QK_WIDTH // LANES
    voff = koff + DIFF_QK_WIDTH // LANES
    vec = lambda n: pl.BlockSpec((1, n), lambda h, i, j: (0, 0))
    return pl.pallas_call(
        _diff_kernel,
        grid=(DIFF_HEADS, b, SEQ // tq),
        in_specs=[vec(DIFF_QK_DIM), vec(DIFF_QK_DIM), vec(DIFF_QK_DIM), vec(DIFF_QK_DIM),
                  pl.BlockSpec((1, tq, LANES), lambda h, i, j: (i, j, qoff + h)),
                  pl.BlockSpec((1, SEQ, LANES), lambda h, i, j: (i, 0, koff + h)),
                  pl.BlockSpec((1, SEQ, LANES), lambda h, i, j: (i, 0, voff + h)),
                  vec(DIFF_V_DIM),
                  pl.BlockSpec((1, DIFF_SUB, 2 * SEQ - DIFF_SUB), lambda h, i, j: (h, 0, 0))],
        out_specs=pl.BlockSpec((1, tq, DIFF_V_DIM), lambda h, i, j: (i, j, h)),
        out_shape=jax.ShapeDtypeStruct((b, SEQ, DIFF_V_WIDTH), BF16),
        compiler_params=_cparams(("arbitrary", "arbitrary", "arbitrary")),
        name="diff_attn",
    )(lq1, lk1, lq2, lk2, proj3, proj3, proj3, subln_g, _alibi_table(DIFF_SUB))


def _merge_kernel(n_first, a_ref, b_ref, gna_ref, gdf_ref, xp_ref, xs_ref, wna_ref, wdf_ref, wout_ref, o_ref):
    pa = jnp.dot(a_ref[...], wna_ref[...], preferred_element_type=F32)
    pb = jnp.dot(b_ref[...], wdf_ref[...], preferred_element_type=F32)
    merged = (jax.nn.sigmoid(gna_ref[...].astype(F32)) * pa
              + jax.nn.sigmoid(gdf_ref[...].astype(F32)) * pb)
    delta = jnp.dot(merged.astype(BF16), wout_ref[...], preferred_element_type=F32)
    in_first = pl.program_id(0) < n_first

    @pl.when(in_first)
    def _():
        o_ref[...] = xp_ref[...] + delta

    @pl.when(jnp.logical_not(in_first))
    def _():
        o_ref[...] = xs_ref[...] + delta


def _merge(a, b, proj, xp, xs, wna, wdf, wout):
    t = xp.shape[0] + xs.shape[0]
    tm = MERGE_TM
    gna_blk = (3 * NA_WIDTH + 2 * DIFF_QK_WIDTH + DIFF_V_WIDTH) // D_MODEL
    const = lambda shape: pl.BlockSpec(shape, lambda i: (0, 0), pipeline_mode=pl.Buffered(1))
    n_first = xp.shape[0] // tm
    spec_p, spec_s = _two_group_specs(tm, D_MODEL, n_first)
    return pl.pallas_call(
        functools.partial(_merge_kernel, n_first),
        grid=(t // tm,),
        in_specs=[pl.BlockSpec((tm, NA_WIDTH), lambda i: (i, 0)),
                  pl.BlockSpec((tm, DIFF_V_WIDTH), lambda i: (i, 0)),
                  pl.BlockSpec((tm, D_MODEL), lambda i: (i, gna_blk)),
                  pl.BlockSpec((tm, D_MODEL), lambda i: (i, gna_blk + 1)),
                  spec_p, spec_s,
                  const((NA_WIDTH, D_MODEL)), const((DIFF_V_WIDTH, D_MODEL)), const((D_MODEL, D_MODEL))],
        out_specs=pl.BlockSpec((tm, D_MODEL), lambda i: (i, 0)),
        out_shape=jax.ShapeDtypeStruct((t, D_MODEL), F32),
        compiler_params=pltpu.CompilerParams(dimension_semantics=("parallel",), vmem_limit_bytes=BIG_VMEM_LIMIT),
        name="merge_outproj",
    )(a, b, proj, proj, xp, xs, wna, wdf, wout)


def _memkv_kernel(n_first, mp_ref, ms_ref, g_ref, w_ref, o_ref):
    def project(m_ref):
        h = _rms_scale(m_ref[...], g_ref[...], NORM_EPS).astype(BF16)
        o_ref[...] = jnp.dot(h, w_ref[...], preferred_element_type=F32).astype(o_ref.dtype)

    in_first = pl.program_id(0) < n_first
    pl.when(in_first)(lambda: project(mp_ref))
    pl.when(jnp.logical_not(in_first))(lambda: project(ms_ref))


def _memkv(mem_p, mem_s, g, w):
    t = mem_p.shape[0] + mem_s.shape[0]
    n_first = mem_p.shape[0] // MEM_LEN
    spec_p, spec_s = _two_group_specs(MEM_LEN, D_MODEL, n_first)
    return pl.pallas_call(
        functools.partial(_memkv_kernel, n_first),
        grid=(t // MEM_LEN,),
        in_specs=[spec_p, spec_s,
                  pl.BlockSpec((1, D_MODEL), lambda i: (0, 0)),
                  pl.BlockSpec((D_MODEL, 2 * XA_WIDTH), lambda i: (0, 0))],
        out_specs=pl.BlockSpec((MEM_LEN, 2 * XA_WIDTH), lambda i: (i, 0)),
        out_shape=jax.ShapeDtypeStruct((t, 2 * XA_WIDTH), BF16),
        compiler_params=_cparams(("parallel",)),
        name="mem_kv",
    )(mem_p, mem_s, g, w)


def _route(logits):
    lane = lax.broadcasted_iota(I32, logits.shape, 1).astype(F32)
    ninf = -jnp.inf
    big = float(LANES)
    gl = jnp.where(lane < N_GROUPS, logits, ninf)
    gmax = jnp.max(gl, axis=-1, keepdims=True)
    g = jnp.min(jnp.where(gl == gmax, lane, big), axis=-1, keepdims=True)
    pg = 1.0 / jnp.sum(jnp.exp(gl - gmax), axis=-1, keepdims=True)
    lo = N_GROUPS + EXPERTS_PER_GROUP * g
    el = jnp.where((lane >= lo) & (lane < lo + EXPERTS_PER_GROUP), logits, ninf)
    v1 = jnp.max(el, axis=-1, keepdims=True)
    i1 = jnp.min(jnp.where(el == v1, lane, big), axis=-1, keepdims=True)
    el2 = jnp.where(lane == i1, ninf, el)
    v2 = jnp.max(el2, axis=-1, keepdims=True)
    i2 = jnp.min(jnp.where(el2 == v2, lane, big), axis=-1, keepdims=True)
    t = jnp.exp(v2 - v1)
    den = 1.0 + t
    gate1 = pg * (1.0 / den)
    gate2 = pg * (t / den)
    gates = jnp.where(lane == 0.0, gate1, jnp.where(lane == 1.0, gate2, 0.0))
    eids = jnp.where(lane == 0.0, i1 - N_GROUPS, jnp.where(lane == 1.0, i2 - N_GROUPS, 0.0)).astype(I32)
    return gates, eids


def _pack_halves(h):
    half = h.shape[1] // 2
    hi = pltpu.bitcast(h[:, :half].astype(BF16).astype(F32), U32)
    lo = pltpu.bitcast(h[:, half:].astype(BF16).astype(F32), U32)
    return hi | (lo >> 16)


def _unpack_halves(w):
    hi = pltpu.bitcast(w & jnp.uint32(0xFFFF0000), F32).astype(BF16)
    lo = pltpu.bitcast(w << 16, F32).astype(BF16)
    return jnp.concatenate([hi, lo], axis=1)


def _store_row_tiles(ref, x):
    n, width = x.shape
    c = width // LANES
    for j in range(c):
        ref[pl.ds(j, n, stride=c), :] = x[:, j * LANES:(j + 1) * LANES]


def _load_row_tiles(ref, n, c):
    return jnp.concatenate([ref[pl.ds(j, n, stride=c), :] for j in range(c)], axis=1)


def _xattn_kernel(x_ref, kv_ref, gx_ref, wq_ref, wo_ref, gf_ref, wr_ref, br_ref,
                  x2_ref, hp_ref, gate_ref, eid_ref):
    scale = XA_HEAD_DIM ** -0.5
    heads = [slice(h * XA_HEAD_DIM, (h + 1) * XA_HEAD_DIM) for h in range(XA_HEADS)]
    nsub = x_ref.shape[0] // XA_SUB

    def rows(u):
        return slice(u * XA_SUB, (u + 1) * XA_SUB)

    def scores(u):
        hq = _rms_scale(x_ref[rows(u), :], gx_ref[...], NORM_EPS).astype(BF16)
        q = jnp.dot(hq, wq_ref[...], preferred_element_type=F32).astype(BF16)
        return [_dot_nt(q[:, sl], kv_ref[:, sl]) for sl in heads]

    def finish(u, sc):
        outs = []
        for h in range(XA_HEADS):
            vsl = slice(XA_WIDTH + h * XA_HEAD_DIM, XA_WIDTH + (h + 1) * XA_HEAD_DIM)
            s = sc[h] * (scale * LOG2E)
            m = jnp.max(s, axis=-1, keepdims=True)
            e = jnp.exp2(s - m)
            l = jnp.sum(e, axis=-1, keepdims=True)
            o = jnp.dot(e.astype(BF16), kv_ref[:, vsl], preferred_element_type=F32)
            outs.append((o / l).astype(BF16))
        o = jnp.concatenate(outs, axis=1)
        x2 = x_ref[rows(u), :] + jnp.dot(o, wo_ref[...], preferred_element_type=F32)
        x2_ref[rows(u), :] = x2
        h3 = _rms_scale(x2, gf_ref[...], NORM_EPS)
        _store_row_tiles(hp_ref.at[pl.ds(u * XA_SUB * HP_TILES, XA_SUB * HP_TILES)], _pack_halves(h3))
        logits = jnp.dot(h3.astype(BF16), wr_ref[...], preferred_element_type=F32) + br_ref[...]
        gates, eids = _route(logits)
        gate_ref[rows(u), :] = gates
        eid_ref[rows(u), :] = eids

    pending = scores(0)
    for u in range(nsub):
        sc = pending
        if u + 1 < nsub:
            pending = scores(u + 1)
        finish(u, sc)


def _xattn(x1, kv, gx, wq, wo, gf, wr, br):
    t = x1.shape[0]
    tm = XA_TM
    per_batch = SEQ // tm
    const = lambda shape: pl.BlockSpec(shape, lambda i: (0, 0), pipeline_mode=pl.Buffered(1))
    tile = lambda n: pl.BlockSpec((tm, n), lambda i: (i, 0))
    return pl.pallas_call(
        _xattn_kernel,
        grid=(t // tm,),
        in_specs=[tile(D_MODEL),
                  pl.BlockSpec((MEM_LEN, 2 * XA_WIDTH), lambda i: (i // per_batch, 0)),
                  const((1, D_MODEL)), const((D_MODEL, XA_WIDTH)), const((XA_WIDTH, D_MODEL)),
                  const((1, D_MODEL)), const((D_MODEL, LANES)), const((1, LANES))],
        out_specs=[tile(D_MODEL), pl.BlockSpec((tm * HP_TILES, LANES), lambda i: (i, 0)), tile(LANES), tile(LANES)],
        out_shape=[jax.ShapeDtypeStruct((t, D_MODEL), F32),
                   jax.ShapeDtypeStruct((t * HP_TILES, LANES), U32),
                   jax.ShapeDtypeStruct((t, LANES), F32),
                   jax.ShapeDtypeStruct((t, LANES), I32)],
        compiler_params=pltpu.CompilerParams(dimension_semantics=("parallel",), vmem_limit_bytes=BIG_VMEM_LIMIT),
        name="xattn_router",
    )(x1, kv, gx, wq, wo, gf, wr, br)


IDX_ROWS = 8
PIECE = 8


def _rank_kernel(eid_ref, gate_ref, idx_ref, gidx_ref, tile_ref, cnt_ref, carry_ref):
    @pl.when(pl.program_id(0) == 0)
    def _():
        carry_ref[...] = jnp.zeros_like(carry_ref)

    tm = eid_ref.shape[0]
    eid = eid_ref[...]
    lane = lax.broadcasted_iota(I32, (tm, LANES), 1)
    e1 = eid[:, 0:1]
    e2 = eid[:, 1:2]
    m1 = lane == e1
    m2 = lane == e2
    onehot = jnp.where(m1, 1.0, jnp.where(m2, 1.0, 0.0))
    r = lax.broadcasted_iota(I32, (tm, tm), 0)
    c = lax.broadcasted_iota(I32, (tm, tm), 1)
    tri = jnp.where(c < r, 1.0, 0.0).astype(BF16)
    local = jnp.dot(tri, onehot.astype(BF16), preferred_element_type=F32)
    carry = carry_ref[0:1, :]
    n_tile = jnp.sum(onehot, axis=0, keepdims=True)

    n_piece = jnp.floor((n_tile + (PIECE - 1)) * (1.0 / PIECE)) * PIECE

    def assignment(m, e):
        r_local = jnp.sum(jnp.where(m, local, 0.0), axis=-1, keepdims=True)
        r_global = r_local + jnp.sum(jnp.where(m, carry, 0.0), axis=-1, keepdims=True)
        first = jnp.sum(jnp.where(lane < e, n_tile, 0.0), axis=-1, keepdims=True)
        first_piece = jnp.sum(jnp.where(lane < e, n_piece, 0.0), axis=-1, keepdims=True)
        return first + r_local, r_global, first_piece + r_local

    p1, g1, q1 = assignment(m1, e1)
    p2, g2, q2 = assignment(m2, e2)
    cols = [p1, p2, g1, g2, e1.astype(F32), e2.astype(F32), q1, q2]
    table = jnp.zeros((tm, LANES), F32)
    for j, col in enumerate(cols):
        table = jnp.where(lane == j, col, table)
    idx_ref[0] = jnp.transpose(table)[0:IDX_ROWS, :].astype(I32)
    gidx_ref[0] = jnp.transpose(gate_ref[...])[0:IDX_ROWS, :]
    tile_ref[0] = jnp.where(lax.broadcasted_iota(I32, (8, LANES), 0) == 0, carry, n_tile)
    carry_ref[...] = carry_ref[...] + n_tile
    cnt_ref[...] = carry_ref[...]


def _rank(eid, gates):
    t = eid.shape[0]
    tm = RANK_TM
    n_tiles = t // tm
    return pl.pallas_call(
        _rank_kernel,
        grid=(n_tiles,),
        in_specs=[pl.BlockSpec((tm, LANES), lambda i: (i, 0)), pl.BlockSpec((tm, LANES), lambda i: (i, 0))],
        out_specs=[pl.BlockSpec((1, IDX_ROWS, tm), lambda i: (i, 0, 0)),
                   pl.BlockSpec((1, IDX_ROWS, tm), lambda i: (i, 0, 0)),
                   pl.BlockSpec((1, 8, LANES), lambda i: (i, 0, 0)),
                   pl.BlockSpec((8, LANES), lambda i: (0, 0))],
        out_shape=[jax.ShapeDtypeStruct((n_tiles, IDX_ROWS, tm), I32),
                   jax.ShapeDtypeStruct((n_tiles, IDX_ROWS, tm), F32),
                   jax.ShapeDtypeStruct((n_tiles, 8, LANES), F32),
                   jax.ShapeDtypeStruct((8, LANES), F32)],
        scratch_shapes=[pltpu.VMEM((8, LANES), F32)],
        compiler_params=_cparams(("arbitrary",)),
        name="expert_rank",
    )(eid, gates)


def _dispatch_kernel(n_steps, cnt_ref, pend_ref, nused_ref, pos_ref, run_ref, run_prev_ref, hp_ref, xs_ref, sbuf,
                     zero_ref, sem, zsem):
    i = pl.program_id(0)
    tm = hp_ref.shape[0] // (2 * HP_TILES)
    zb = zero_ref.shape[0]
    n_blocks = xs_ref.shape[0] // zb

    @pl.when(i == 0)
    def _():
        zero_ref[...] = jnp.zeros_like(zero_ref)
        for b in range(2):
            sbuf[b, pl.ds(TOP_K * tm * HP_TILES, PIECE * HP_TILES), :] = jnp.zeros((PIECE * HP_TILES, LANES), U32)

        def zero_copy(block):
            return pltpu.make_async_copy(zero_ref, xs_ref.at[pl.ds(pl.multiple_of(block * zb, zb), zb)], zsem)

        def tail_start(j, carry):
            zero_copy(j).start()
            return carry

        def tail_wait(j, carry):
            zero_copy(j).wait()
            return carry

        for e in range(N_EXPERTS):
            @pl.when(cnt_ref[e] > 0)
            def _():
                zero_copy(pend_ref[e] - 1).start()
        lax.fori_loop(nused_ref[0], n_blocks, tail_start, 0)
        for e in range(N_EXPERTS):
            @pl.when(cnt_ref[e] > 0)
            def _():
                zero_copy(pend_ref[e] - 1).wait()
        lax.fori_loop(nused_ref[0], n_blocks, tail_wait, 0)

    def place(half):
        def body(t, carry):
            row = hp_ref[pl.ds(pl.multiple_of((half * tm + t) * HP_TILES, HP_TILES), HP_TILES), :]
            for k in range(TOP_K):
                sbuf[half, pl.ds(pl.multiple_of(pos_ref[half, k, t] * HP_TILES, HP_TILES), HP_TILES), :] = row
            return carry
        lax.fori_loop(0, tm, body, 0, unroll=16)

    def piece_copy(src_tok, dst_slot, half):
        rows = PIECE * HP_TILES
        return pltpu.make_async_copy(sbuf.at[half, pl.ds(pl.multiple_of(src_tok * HP_TILES, HP_TILES), rows)],
                                     xs_ref.at[pl.ds(pl.multiple_of(dst_slot * HP_TILES, HP_TILES), rows)],
                                     sem.at[half])

    def issue(half):
        for e in range(N_EXPERTS):
            slot0 = run_ref[half, 0, e]
            first = run_ref[half, 0, 2 * N_EXPERTS + e]

            def run_piece(j, carry):
                piece_copy(first + PIECE * j, slot0 + PIECE * j, half).start()
                return carry

            lax.fori_loop(0, run_ref[half, 0, N_EXPERTS + e], run_piece, 0)

    def drain(runs, half):
        def wait_piece(j, carry):
            piece_copy(0, 0, half).wait()
            return carry
        lax.fori_loop(0, runs[half, 0, 3 * N_EXPERTS], wait_piece, 0)

    place(0)

    @pl.when(i > 0)
    def _():
        drain(run_prev_ref, 1)
    issue(0)
    place(1)
    drain(run_ref, 0)
    issue(1)

    @pl.when(i == n_steps - 1)
    def _():
        drain(run_ref, 1)


def _dispatch(cnt, pend_blocks, nused, pos, runs, hp, n_slots):
    n_tiles, _, tm = pos.shape
    n_steps = n_tiles // 2
    grid_spec = pltpu.PrefetchScalarGridSpec(
        num_scalar_prefetch=3,
        grid=(n_steps,),
        in_specs=[pl.BlockSpec((2, TOP_K, tm), lambda i, *_: (i, 0, 0), memory_space=pltpu.SMEM),
                  pl.BlockSpec((2, 1, LANES), lambda i, *_: (i, 0, 0), memory_space=pltpu.SMEM),
                  pl.BlockSpec((2, 1, LANES), lambda i, *_: (jnp.maximum(i - 1, 0), 0, 0), memory_space=pltpu.SMEM),
                  pl.BlockSpec((2 * tm * HP_TILES, LANES), lambda i, *_: (i, 0))],
        out_specs=pl.BlockSpec(memory_space=pl.ANY),
        scratch_shapes=[pltpu.VMEM((2, (TOP_K * tm + PIECE) * HP_TILES, LANES), U32),
                        pltpu.VMEM((MOE_TM * HP_TILES, LANES), U32),
                        pltpu.SemaphoreType.DMA((2,)), pltpu.SemaphoreType.DMA(())],
    )
    return pl.pallas_call(
        functools.partial(_dispatch_kernel, n_steps),
        grid_spec=grid_spec,
        out_shape=jax.ShapeDtypeStruct((n_slots * HP_TILES, LANES), U32),
        compiler_params=_cparams(("arbitrary",)),
        name="moe_dispatch",
    )(cnt, pend_blocks, nused, pos, runs, runs, hp)


MOE_DOWN_CHUNKS = 4


def _moe_kernel(bexp_ref, nused_ref, xs_ref, wg_ref, wu_ref, wd_ref, o_ref, wgb, wub, wdb):
    i = pl.program_id(0)
    nused = nused_ref[0]
    tm = xs_ref.shape[0] // HP_TILES

    @pl.when((i < nused) & ((i == 0) | (bexp_ref[i] != bexp_ref[jnp.maximum(i - 1, 0)])))
    def _():
        wgb[...] = wg_ref[0].astype(BF16)
        wub[...] = wu_ref[0].astype(BF16)
        wdb[...] = wd_ref[0].astype(BF16)

    @pl.when(i < nused)
    def _():
        x = _unpack_halves(_load_row_tiles(xs_ref, tm, HP_TILES))
        hg = jnp.dot(x, wgb[...], preferred_element_type=F32)
        hu = jnp.dot(x, wub[...], preferred_element_type=F32)
        hdn = ((hg * jax.nn.sigmoid(hg)) * hu).astype(BF16)
        cw = D_MODEL // MOE_DOWN_CHUNKS
        for c in range(MOE_DOWN_CHUNKS):
            y = jnp.dot(hdn, wdb[:, c * cw:(c + 1) * cw], preferred_element_type=F32)
            for j in range(cw // LANES):
                o_ref[pl.ds(c * (cw // LANES) + j, tm, stride=YS_TILES), :] = y[:, j * LANES:(j + 1) * LANES]

    @pl.when(i >= nused)
    def _():
        o_ref[...] = jnp.zeros_like(o_ref)


def _moe(bexp, nused, xs, wg, wu, wd):
    tm = MOE_TM
    n_slots = xs.shape[0] // HP_TILES
    n_blocks = n_slots // tm
    grid_spec = pltpu.PrefetchScalarGridSpec(
        num_scalar_prefetch=2,
        grid=(n_blocks,),
        in_specs=[pl.BlockSpec((tm * HP_TILES, LANES), lambda i, be, nu: (jnp.minimum(i, nu[0] - 1), 0)),
                  pl.BlockSpec((1, D_MODEL, D_EXPERT), lambda i, be, nu: (be[i], 0, 0)),
                  pl.BlockSpec((1, D_MODEL, D_EXPERT), lambda i, be, nu: (be[i], 0, 0)),
                  pl.BlockSpec((1, D_EXPERT, D_MODEL), lambda i, be, nu: (be[i], 0, 0))],
        out_specs=pl.BlockSpec((tm * YS_TILES, LANES), lambda i, be, nu: (i, 0)),
        scratch_shapes=[pltpu.VMEM((D_MODEL, D_EXPERT), BF16), pltpu.VMEM((D_MODEL, D_EXPERT), BF16),
                        pltpu.VMEM((D_EXPERT, D_MODEL), BF16)],
    )
    return pl.pallas_call(
        _moe_kernel,
        grid_spec=grid_spec,
        out_shape=jax.ShapeDtypeStruct((n_slots * YS_TILES, LANES), F32),
        compiler_params=_cparams(("arbitrary",)),
        name="moe_experts",
    )(bexp, nused, xs, wg, wu, wd)


def _combine_kernel(n_tiles, pos_ref, run_ref, run_next_ref, gate_ref, x_ref, g_ref, ys_ref, o_ref, ybuf, mbuf, sem):
    i = pl.program_id(0)
    tm = x_ref.shape[0]
    slot = lax.rem(i, 2)

    def piece_copy(src_slot, dst_pos, buf):
        rows = PIECE * YS_TILES
        return pltpu.make_async_copy(ys_ref.at[pl.ds(pl.multiple_of(src_slot * YS_TILES, YS_TILES), rows)],
                                     ybuf.at[buf, pl.ds(pl.multiple_of(dst_pos * YS_TILES, YS_TILES), rows)],
                                     sem.at[buf])

    def request(runs, buf):
        for e in range(N_EXPERTS):
            slot0 = runs[0, 0, e]
            first = runs[0, 0, 2 * N_EXPERTS + e]

            def run_piece(j, carry):
                piece_copy(slot0 + PIECE * j, first + PIECE * j, buf).start()
                return carry

            lax.fori_loop(0, runs[0, 0, N_EXPERTS + e], run_piece, 0)

    @pl.when(i == 0)
    def _():
        request(run_ref, 0)

    @pl.when(i + 1 < n_tiles)
    def _():
        request(run_next_ref, 1 - slot)

    def drain(j, carry):
        piece_copy(0, 0, slot).wait()
        return carry

    lax.fori_loop(0, run_ref[0, 0, 3 * N_EXPERTS], drain, 0)

    def token(t, carry):
        acc = None
        for k in range(TOP_K):
            rows = pl.ds(pl.multiple_of(pos_ref[0, k, t] * YS_TILES, YS_TILES), YS_TILES)
            term = ybuf[slot, rows, :] * gate_ref[0, k, t]
            acc = term if acc is None else acc + term
        mbuf[pl.ds(pl.multiple_of(t * YS_TILES, YS_TILES), YS_TILES), :] = acc
        return carry

    lax.fori_loop(0, tm, token, 0, unroll=16)
    moe = _load_row_tiles(mbuf, tm, YS_TILES)
    o_ref[...] = _rms_scale(x_ref[...] + moe, g_ref[...], NORM_EPS)


def _combine(pos, runs, gidx, x2, g, ys, row0, n_rows):
    _, _, tm = pos.shape
    tile0 = row0 // tm
    n = n_rows // tm
    smem = lambda rows: pl.BlockSpec((1, rows, tm), lambda i: (tile0 + i, 0, 0), memory_space=pltpu.SMEM)
    run_spec = lambda ahead: pl.BlockSpec((1, 1, LANES), lambda i: (tile0 + jnp.minimum(i + ahead, n - 1), 0, 0),
                                          memory_space=pltpu.SMEM)
    return pl.pallas_call(
        functools.partial(_combine_kernel, n),
        grid=(n,),
        in_specs=[smem(TOP_K), run_spec(0), run_spec(1), smem(TOP_K),
                  pl.BlockSpec((tm, D_MODEL), lambda i: (tile0 + i, 0)),
                  pl.BlockSpec((1, D_MODEL), lambda i: (0, 0)),
                  pl.BlockSpec(memory_space=pl.ANY)],
        out_specs=pl.BlockSpec((tm, D_MODEL), lambda i: (i, 0)),
        out_shape=jax.ShapeDtypeStruct((n_rows, D_MODEL), F32),
        scratch_shapes=[pltpu.VMEM((2, (TOP_K * tm + N_EXPERTS * PIECE) * YS_TILES, LANES), F32),
                        pltpu.VMEM((tm * YS_TILES, LANES), F32),
                        pltpu.SemaphoreType.DMA((2,))],
        compiler_params=_cparams(("arbitrary",)),
        name="moe_combine",
    )(pos, runs, runs, gidx[:, 0:TOP_K, :], x2, g, ys)


def _routing_plan(idx, tiles, counts):
    cnt = counts[0, :N_EXPERTS].astype(I32)
    padded = jnp.where(cnt > 0, (cnt + PIECE - 1 + MOE_TM - 1) // MOE_TM * MOE_TM, 0)
    pends = jnp.cumsum(padded)
    pstarts = pends - padded
    n_tiles, _, tm = idx.shape
    n_blocks = n_tiles * tm * TOP_K // MOE_TM + N_EXPERTS + 1
    blk0 = jnp.arange(n_blocks, dtype=I32) * MOE_TM
    bexp = jnp.minimum(jnp.sum(pends[None, :] <= blk0[:, None], axis=1), N_EXPERTS - 1).astype(I32)
    nused = (pends[-1:] // MOE_TM).astype(I32)
    before = tiles[:, 0, :N_EXPERTS].astype(I32)
    inside = tiles[:, 1, :N_EXPERTS].astype(I32)
    first = jnp.cumsum(inside, axis=1) - inside
    pieces = (inside + PIECE - 1) // PIECE
    first_piece = (jnp.cumsum(pieces, axis=1) - pieces) * PIECE

    def run_table(first_pos):
        return jnp.concatenate([pstarts[None, :] + before, pieces, first_pos, jnp.sum(pieces, axis=1, keepdims=True),
                                jnp.zeros((n_tiles, LANES - 3 * N_EXPERTS - 1), I32)], axis=1).reshape(n_tiles, 1, LANES)

    pos = idx[:, 0:TOP_K, :]
    pos_piece = idx[:, 3 * TOP_K:4 * TOP_K, :]
    return pos, run_table(first), pos_piece, run_table(first_piece), cnt, (pends // MOE_TM).astype(I32), bexp, nused, n_blocks * MOE_TM


def kernel(x_prompt, x_sample, mem_prompt, mem_sample, mix_norm_g, w_in, na_rpb, diff_lambda_q1, diff_lambda_k1, diff_lambda_q2, diff_lambda_k2, diff_subln_g, w_branch_na, w_branch_diff, w_out, xa_norm_g, mem_norm_g, xa_w_q, xa_w_kv, xa_w_o, ffn_norm_g, router_group_w, router_group_b, router_expert_w, router_expert_b, w_gate, w_up, w_down, final_norm_g):
    nb_p, nb_s = x_prompt.shape[0], x_sample.shape[0]
    nb = nb_p + nb_s
    t = nb * SEQ
    xp = x_prompt.reshape(nb_p * SEQ, D_MODEL)
    xs_in = x_sample.reshape(nb_s * SEQ, D_MODEL)
    mem_p = mem_prompt.reshape(nb_p * MEM_LEN, D_MODEL)
    mem_s = mem_sample.reshape(nb_s * MEM_LEN, D_MODEL)
    row = lambda v: v.reshape(1, -1).astype(F32)
    bf = lambda w: w.astype(BF16)

    qcol = 3 * NA_WIDTH
    colscale = jnp.ones((1, IN_COLS), F32).at[:, qcol:qcol + DIFF_QK_WIDTH].set(DIFF_QK_DIM ** -0.5 * LOG2E)
    proj = _inproj(xp, xs_in, row(mix_norm_g[0]), bf(w_in[0]), colscale)
    proj3 = proj.reshape(nb, SEQ, IN_COLS)
    a = _na_attention(proj3, _na_bias_blocks(na_rpb[0]))
    b = _diff_attention(proj3, row(diff_lambda_q1[0]), row(diff_lambda_k1[0]),
                        row(diff_lambda_q2[0]), row(diff_lambda_k2[0]), row(diff_subln_g[0]))
    x1 = _merge(a.reshape(t, NA_WIDTH), b.reshape(t, DIFF_V_WIDTH), proj, xp, xs_in,
                bf(w_branch_na[0]), bf(w_branch_diff[0]), bf(w_out[0]))
    kv = _memkv(mem_p, mem_s, row(mem_norm_g[0]), bf(xa_w_kv[0]))
    w_router = jnp.zeros((D_MODEL, LANES), F32)
    w_router = w_router.at[:, :N_GROUPS].set(router_group_w[0]).at[:, N_GROUPS:N_GROUPS + N_EXPERTS].set(router_expert_w[0])
    b_router = jnp.zeros((1, LANES), F32)
    b_router = b_router.at[0, :N_GROUPS].set(router_group_b[0]).at[0, N_GROUPS:N_GROUPS + N_EXPERTS].set(router_expert_b[0])
    x2, hp, gates, eid = _xattn(x1, kv, row(xa_norm_g[0]), bf(xa_w_q[0]), bf(xa_w_o[0]),
                                row(ffn_norm_g[0]), bf(w_router), b_router)
    idx, gidx, tiles, counts = _rank(eid, gates)
    pos, runs, pos_c, runs_c, cnt, pend_blocks, bexp, nused, n_slots = _routing_plan(idx, tiles, counts)
    xs = _dispatch(cnt, pend_blocks, nused, pos, runs, hp, n_slots)
    ys = _moe(bexp, nused, xs, w_gate[0], w_up[0], w_down[0])
    fg = row(final_norm_g)
    y_p = _combine(pos_c, runs_c, gidx, x2, fg, ys, 0, nb_p * SEQ).reshape(nb_p, SEQ, D_MODEL)
    y_s = _combine(pos_c, runs_c, gidx, x2, fg, ys, nb_p * SEQ, nb_s * SEQ).reshape(nb_s, SEQ, D_MODEL)
    return (y_p, y_s)
```

```python
import functools
import math

import jax
import jax.numpy as jnp
from jax import lax
from jax.experimental import pallas as pl
from jax.experimental.pallas import tpu as pltpu

F32 = jnp.float32
BF16 = jnp.bfloat16
I32 = jnp.int32
U32 = jnp.uint32

D_MODEL = 2048
SEQ = 2048
GRID_W = 64
ROWS = SEQ // GRID_W
NA_HEADS = 8
NA_HEAD_DIM = 128
NA_WIDTH = NA_HEADS * NA_HEAD_DIM
NA_KH = 8
NA_KW = 16
DIFF_HEADS = 8
DIFF_QK_DIM = 64
DIFF_V_DIM = 128
DIFF_QK_WIDTH = DIFF_HEADS * 2 * DIFF_QK_DIM
DIFF_V_WIDTH = DIFF_HEADS * DIFF_V_DIM
IN_COLS = 3 * NA_WIDTH + 2 * DIFF_QK_WIDTH + DIFF_V_WIDTH + 2 * D_MODEL
MEM_LEN = 256
XA_HEADS = 4
XA_HEAD_DIM = 128
XA_WIDTH = XA_HEADS * XA_HEAD_DIM
N_GROUPS = 4
EXPERTS_PER_GROUP = 8
N_EXPERTS = N_GROUPS * EXPERTS_PER_GROUP
TOP_K = 2
D_EXPERT = 512
NORM_EPS = 1e-6
SUBLN_EPS = 1e-5
NEG_INF = -1e30
LAMBDA_INIT = 0.8 - 0.6 * math.exp(-0.3 * 0)
LOG2E = math.log2(math.e)

LANES = 128
HP_TILES = D_MODEL // 2 // LANES
YS_TILES = D_MODEL // LANES
VMEM_LIMIT = 56 * 1024 * 1024
BIG_VMEM_LIMIT = 60 * 1024 * 1024

INPROJ_TM = 1024
INPROJ_TN = 1280
NORM_CHUNK = 128
DIFF_TQ = 2048
DIFF_SUB = 128
MERGE_TM = 512
XA_TM = 1024
XA_SUB = 256
RANK_TM = 512
MOE_TM = 512


def _cparams(sem):
    return pltpu.CompilerParams(dimension_semantics=sem, vmem_limit_bytes=VMEM_LIMIT)


def _rms_scale(x, g, eps):
    ms = jnp.mean(x * x, axis=-1, keepdims=True)
    return x * lax.rsqrt(ms + eps) * g


def _dot_nt(a, b):
    return lax.dot_general(a, b, (((1,), (1,)), ((), ())), preferred_element_type=F32)


def _two_group_specs(tm, width, n_first):
    first = pl.BlockSpec((tm, width), lambda i, *_: (jnp.minimum(i, n_first - 1), 0))
    second = pl.BlockSpec((tm, width), lambda i, *_: (jnp.maximum(i - n_first, 0), 0))
    return first, second


def _inproj_kernel(n_first, xp_ref, xs_ref, g_ref, w_ref, cs_ref, o_ref, h_ref):
    def normalise(x_ref):
        def chunk(c, carry):
            rows = pl.ds(pl.multiple_of(c * NORM_CHUNK, NORM_CHUNK), NORM_CHUNK)
            h_ref[rows, :] = _rms_scale(x_ref[rows, :], g_ref[...], NORM_EPS).astype(BF16)
            return carry
        lax.fori_loop(0, x_ref.shape[0] // NORM_CHUNK, chunk, 0)

    first_col = pl.program_id(1) == 0
    in_first = pl.program_id(0) < n_first
    pl.when(first_col & in_first)(lambda: normalise(xp_ref))
    pl.when(first_col & jnp.logical_not(in_first))(lambda: normalise(xs_ref))
    acc = jnp.dot(h_ref[...], w_ref[...], preferred_element_type=F32)
    o_ref[...] = (acc * cs_ref[...]).astype(o_ref.dtype)


def _inproj(xp, xs, g, w, colscale):
    d = xp.shape[1]
    t = xp.shape[0] + xs.shape[0]
    n = w.shape[1]
    tm = INPROJ_TM
    n_first = xp.shape[0] // tm
    spec_p, spec_s = _two_group_specs(tm, d, n_first)
    return pl.pallas_call(
        functools.partial(_inproj_kernel, n_first),
        grid=(t // tm, n // INPROJ_TN),
        in_specs=[spec_p, spec_s,
                  pl.BlockSpec((1, d), lambda i, j: (0, 0)),
                  pl.BlockSpec((d, INPROJ_TN), lambda i, j: (0, j)),
                  pl.BlockSpec((1, INPROJ_TN), lambda i, j: (0, j))],
        out_specs=pl.BlockSpec((tm, INPROJ_TN), lambda i, j: (i, j)),
        out_shape=jax.ShapeDtypeStruct((t, n), BF16),
        scratch_shapes=[pltpu.VMEM((tm, d), BF16)],
        compiler_params=pltpu.CompilerParams(dimension_semantics=("parallel", "arbitrary"),
                                             vmem_limit_bytes=BIG_VMEM_LIMIT),
        name="inproj",
    )(xp, xs, g, w, colscale)


NA_QROWS = 4
NA_KROWS = 12
NA_TILES = ROWS // NA_QROWS
NA_HEADS_PER_STEP = 2


def _na_tile_key_row(ti):
    return min(max(NA_QROWS * ti - NA_KH // 2, 0), ROWS - NA_KROWS)


def _na_bias_blocks(rpb):
    cols = jnp.arange(GRID_W)
    cstart = jnp.clip(cols - NA_KW // 2, 0, GRID_W - NA_KW)
    col_ok = (cols[None, :] >= cstart[:, None]) & (cols[None, :] < cstart[:, None] + NA_KW)
    dc = jnp.clip(cols[None, :] - cols[:, None], 1 - NA_KW, NA_KW - 1) + NA_KW - 1
    pick = (dc[None] == jnp.arange(2 * NA_KW - 1)[:, None, None]).astype(F32)
    t = jnp.einsum("hdx,xck->hdck", rpb.astype(F32), pick, precision=lax.Precision.HIGHEST)
    return jnp.where(col_ok[None, None], t, NEG_INF)


def _na_kernel(q_ref, k_ref, v_ref, t_ref, o_ref, bias_ref):
    scale = NA_HEAD_DIM ** -0.5
    tq = NA_QROWS * GRID_W
    tk = NA_KROWS * GRID_W
    hd = NA_HEAD_DIM

    @pl.when(pl.program_id(1) == 0)
    def _():
        masked = jnp.full((GRID_W, GRID_W), NEG_INF, F32)
        for hh in range(NA_HEADS_PER_STEP):
            for variant, ti in enumerate((0, 1, NA_TILES - 1)):
                for jr in range(NA_QROWS):
                    r = NA_QROWS * ti + jr
                    rs = min(max(r - NA_KH // 2, 0), ROWS - NA_KH)
                    blocks = []
                    for i in range(NA_KROWS):
                        kr = _na_tile_key_row(ti) + i
                        blocks.append(t_ref[hh, kr - r + NA_KH - 1] if rs <= kr < rs + NA_KH else masked)
                    bias_ref[hh, variant, jr * GRID_W:(jr + 1) * GRID_W, :] = jnp.concatenate(blocks, axis=1) * LOG2E

    def key_rows(ti):
        k0 = _na_tile_key_row(ti) * GRID_W
        return slice(k0, k0 + tk)

    work = [(hh, ti) for hh in range(NA_HEADS_PER_STEP) for ti in range(NA_TILES)]

    def scores(item):
        hh, ti = item
        cols = slice(hh * hd, (hh + 1) * hd)
        return _dot_nt(q_ref[0, ti * tq:(ti + 1) * tq, cols], k_ref[0, key_rows(ti), cols])

    pending = scores(work[0])
    for n, (hh, ti) in enumerate(work):
        variant = 0 if ti == 0 else (2 if ti == NA_TILES - 1 else 1)
        cols = slice(hh * hd, (hh + 1) * hd)
        qrows = slice(ti * tq, (ti + 1) * tq)
        s = pending * (scale * LOG2E) + bias_ref[hh, variant]
        if n + 1 < len(work):
            pending = scores(work[n + 1])
        m = jnp.max(s, axis=-1, keepdims=True)
        e = jnp.exp2(s - m)
        l = jnp.sum(e, axis=-1, keepdims=True)
        o = jnp.dot(e.astype(BF16), v_ref[0, key_rows(ti), cols], preferred_element_type=F32)
        o_ref[0, qrows, cols] = (o / l).astype(o_ref.dtype)


def _na_attention(proj3, blocks):
    b = proj3.shape[0]
    hp = NA_HEADS_PER_STEP
    width = hp * NA_HEAD_DIM
    groups = NA_HEADS // hp
    return pl.pallas_call(
        _na_kernel,
        grid=(groups, b),
        in_specs=[pl.BlockSpec((1, SEQ, width), lambda h, i: (i, 0, h)),
                  pl.BlockSpec((1, SEQ, width), lambda h, i: (i, 0, groups + h)),
                  pl.BlockSpec((1, SEQ, width), lambda h, i: (i, 0, 2 * groups + h)),
                  pl.BlockSpec((hp,) + blocks.shape[1:], lambda h, i: (h, 0, 0, 0))],
        out_specs=pl.BlockSpec((1, SEQ, width), lambda h, i: (i, 0, h)),
        out_shape=jax.ShapeDtypeStruct((b, SEQ, NA_WIDTH), BF16),
        scratch_shapes=[pltpu.VMEM((hp, 3, NA_QROWS * GRID_W, NA_KROWS * GRID_W), F32)],
        compiler_params=_cparams(("arbitrary", "arbitrary")),
        name="na_attn",
    )(proj3, proj3, proj3, blocks)


def _alibi_table(tq):
    slopes = jnp.exp2(-8.0 * jnp.arange(1, DIFF_HEADS + 1, dtype=F32) / DIFF_HEADS)
    r = jnp.arange(tq)[:, None]
    x = jnp.arange(2 * SEQ - tq)[None, :]
    dist = jnp.abs(r - x + (SEQ - tq)).astype(F32)
    return (slopes * LOG2E)[:, None, None] * dist[None]


def _diff_kernel(lq1_ref, lk1_ref, lq2_ref, lk2_ref, q_ref, k_ref, v_ref, g_ref, alibi_ref, o_ref):
    qi = pl.program_id(2)
    ts = alibi_ref.shape[1]
    nsub = q_ref.shape[1] // ts
    lam = (jnp.exp(jnp.sum(lq1_ref[...] * lk1_ref[...], axis=-1, keepdims=True))
           - jnp.exp(jnp.sum(lq2_ref[...] * lk2_ref[...], axis=-1, keepdims=True)) + LAMBDA_INIT)
    k = k_ref[0]

    def scores(u):
        q = q_ref[0, u * ts:(u + 1) * ts, :]
        return [_dot_nt(q[:, mi * DIFF_QK_DIM:(mi + 1) * DIFF_QK_DIM], k[:, mi * DIFF_QK_DIM:(mi + 1) * DIFF_QK_DIM])
                for mi in range(2)]

    def expo(s, bias):
        s = s - bias
        m = jnp.max(s, axis=-1, keepdims=True)
        e = jnp.exp2(s - m)
        return e, jnp.sum(e, axis=-1, keepdims=True)

    pending = scores(0)
    for u in range(nsub):
        s0, s1 = pending
        if u + 1 < nsub:
            pending = scores(u + 1)
        off = (SEQ - ts) - (qi * nsub + u) * ts
        bias = alibi_ref[0, :, pl.ds(pl.multiple_of(off, ts), SEQ)]
        e0, l0 = expo(s0, bias)
        e1, l1 = expo(s1, bias)
        a = (e0 - e1 * (lam * l0 / l1)).astype(BF16)
        o = jnp.dot(a, v_ref[0], preferred_element_type=F32) / l0
        o = _rms_scale(o, g_ref[...], SUBLN_EPS) * (1.0 - LAMBDA_INIT)
        o_ref[0, u * ts:(u + 1) * ts, :] = o.astype(o_ref.dtype)


def _diff_attention(proj3, lq1, lk1, lq2, lk2, subln_g):
    b = proj3.shape[0]
    tq = DIFF_TQ
    qoff = 3 * NA_WIDTH // LANES
    koff = qoff + DIFF_QK_WIDTH // LANES
    voff = koff + DIFF_QK_WIDTH // LANES
    vec = lambda n: pl.BlockSpec((1, n), lambda h, i, j: (0, 0))
    return pl.pallas_call(
        _diff_kernel,
        grid=(DIFF_HEADS, b, SEQ // tq),
        in_specs=[vec(DIFF_QK_DIM), vec(DIFF_QK_DIM), vec(DIFF_QK_DIM), vec(DIFF_QK_DIM),
                  pl.BlockSpec((1, tq, LANES), lambda h, i, j: (i, j, qoff + h)),
                  pl.BlockSpec((1, SEQ, LANES), lambda h, i, j: (i, 0, koff + h)),
                  pl.BlockSpec((1, SEQ, LANES), lambda h, i, j: (i, 0, voff + h)),
                  vec(DIFF_V_DIM),
                  pl.BlockSpec((1, DIFF_SUB, 2 * SEQ - DIFF_SUB), lambda h, i, j: (h, 0, 0))],
        out_specs=pl.BlockSpec((1, tq, DIFF_V_DIM), lambda h, i, j: (i, j, h)),
        out_shape=jax.ShapeDtypeStruct((b, SEQ, DIFF_V_WIDTH), BF16),
        compiler_params=_cparams(("arbitrary", "arbitrary", "arbitrary")),
        name="diff_attn",
    )(lq1, lk1, lq2, lk2, proj3, proj3, proj3, subln_g, _alibi_table(DIFF_SUB))


def _merge_kernel(n_first, a_ref, b_ref, gna_ref, gdf_ref, xp_ref, xs_ref, wna_ref, wdf_ref, wout_ref, o_ref):
    pa = jnp.dot(a_ref[...], wna_ref[...], preferred_element_type=F32)
    pb = jnp.dot(b_ref[...], wdf_ref[...], preferred_element_type=F32)
    merged = (jax.nn.sigmoid(gna_ref[...].astype(F32)) * pa
              + jax.nn.sigmoid(gdf_ref[...].astype(F32)) * pb)
    delta = jnp.dot(merged.astype(BF16), wout_ref[...], preferred_element_type=F32)
    in_first = pl.program_id(0) < n_first

    @pl.when(in_first)
    def _():
        o_ref[...] = xp_ref[...] + delta

    @pl.when(jnp.logical_not(in_first))
    def _():
        o_ref[...] = xs_ref[...] + delta


def _merge(a, b, proj, xp, xs, wna, wdf, wout):
    t = xp.shape[0] + xs.shape[0]
    tm = MERGE_TM
    gna_blk = (3 * NA_WIDTH + 2 * DIFF_QK_WIDTH + DIFF_V_WIDTH) // D_MODEL
    const = lambda shape: pl.BlockSpec(shape, lambda i: (0, 0), pipeline_mode=pl.Buffered(1))
    n_first = xp.shape[0] // tm
    spec_p, spec_s = _two_group_specs(tm, D_MODEL, n_first)
    return pl.pallas_call(
        functools.partial(_merge_kernel, n_first),
        grid=(t // tm,),
        in_specs=[pl.BlockSpec((tm, NA_WIDTH), lambda i: (i, 0)),
                  pl.BlockSpec((tm, DIFF_V_WIDTH), lambda i: (i, 0)),
                  pl.BlockSpec((tm, D_MODEL), lambda i: (i, gna_blk)),
                  pl.BlockSpec((tm, D_MODEL), lambda i: (i, gna_blk + 1)),
                  spec_p, spec_s,
                  const((NA_WIDTH, D_MODEL)), const((DIFF_V_WIDTH, D_MODEL)), const((D_MODEL, D_MODEL))],
        out_specs=pl.BlockSpec((tm, D_MODEL), lambda i: (i, 0)),
        out_shape=jax.ShapeDtypeStruct((t, D_MODEL), F32),
        compiler_params=pltpu.CompilerParams(dimension_semantics=("parallel",), vmem_limit_bytes=BIG_VMEM_LIMIT),
        name="merge_outproj",
    )(a, b, proj, proj, xp, xs, wna, wdf, wout)


def _memkv_kernel(n_first, mp_ref, ms_ref, g_ref, w_ref, o_ref):
    def project(m_ref):
        h = _rms_scale(m_ref[...], g_ref[...], NORM_EPS).astype(BF16)
        o_ref[...] = jnp.dot(h, w_ref[...], preferred_element_type=F32).astype(o_ref.dtype)

    in_first = pl.program_id(0) < n_first
    pl.when(in_first)(lambda: project(mp_ref))
    pl.when(jnp.logical_not(in_first))(lambda: project(ms_ref))


def _memkv(mem_p, mem_s, g, w):
    t = mem_p.shape[0] + mem_s.shape[0]
    n_first = mem_p.shape[0] // MEM_LEN
    spec_p, spec_s = _two_group_specs(MEM_LEN, D_MODEL, n_first)
    return pl.pallas_call(
        functools.partial(_memkv_kernel, n_first),
        grid=(t // MEM_LEN,),
        in_specs=[spec_p, spec_s,
                  pl.BlockSpec((1, D_MODEL), lambda i: (0, 0)),
                  pl.BlockSpec((D_MODEL, 2 * XA_WIDTH), lambda i: (0, 0))],
        out_specs=pl.BlockSpec((MEM_LEN, 2 * XA_WIDTH), lambda i: (i, 0)),
        out_shape=jax.ShapeDtypeStruct((t, 2 * XA_WIDTH), BF16),
        compiler_params=_cparams(("parallel",)),
        name="mem_kv",
    )(mem_p, mem_s, g, w)


def _route(logits):
    lane = lax.broadcasted_iota(I32, logits.shape, 1).astype(F32)
    ninf = -jnp.inf
    big = float(LANES)
    gl = jnp.where(lane < N_GROUPS, logits, ninf)
    gmax = jnp.max(gl, axis=-1, keepdims=True)
    g = jnp.min(jnp.where(gl == gmax, lane, big), axis=-1, keepdims=True)
    pg = 1.0 / jnp.sum(jnp.exp(gl - gmax), axis=-1, keepdims=True)
    lo = N_GROUPS + EXPERTS_PER_GROUP * g
    el = jnp.where((lane >= lo) & (lane < lo + EXPERTS_PER_GROUP), logits, ninf)
    v1 = jnp.max(el, axis=-1, keepdims=True)
    i1 = jnp.min(jnp.where(el == v1, lane, big), axis=-1, keepdims=True)
    el2 = jnp.where(lane == i1, ninf, el)
    v2 = jnp.max(el2, axis=-1, keepdims=True)
    i2 = jnp.min(jnp.where(el2 == v2, lane, big), axis=-1, keepdims=True)
    t = jnp.exp(v2 - v1)
    den = 1.0 + t
    gate1 = pg * (1.0 / den)
    gate2 = pg * (t / den)
    gates = jnp.where(lane == 0.0, gate1, jnp.where(lane == 1.0, gate2, 0.0))
    eids = jnp.where(lane == 0.0, i1 - N_GROUPS, jnp.where(lane == 1.0, i2 - N_GROUPS, 0.0)).astype(I32)
    return gates, eids


def _pack_halves(h):
    half = h.shape[1] // 2
    hi = pltpu.bitcast(h[:, :half].astype(BF16).astype(F32), U32)
    lo = pltpu.bitcast(h[:, half:].astype(BF16).astype(F32), U32)
    return hi | (lo >> 16)


def _unpack_halves(w):
    hi = pltpu.bitcast(w & jnp.uint32(0xFFFF0000), F32).astype(BF16)
    lo = pltpu.bitcast(w << 16, F32).astype(BF16)
    return jnp.concatenate([hi, lo], axis=1)


def _store_row_tiles(ref, x):
    n, width = x.shape
    c = width // LANES
    for j in range(c):
        ref[pl.ds(j, n, stride=c), :] = x[:, j * LANES:(j + 1) * LANES]


def _load_row_tiles(ref, n, c):
    return jnp.concatenate([ref[pl.ds(j, n, stride=c), :] for j in range(c)], axis=1)


def _xattn_kernel(x_ref, kv_ref, gx_ref, wq_ref, wo_ref, gf_ref, wr_ref, br_ref,
                  x2_ref, hp_ref, gate_ref, eid_ref):
    scale = XA_HEAD_DIM ** -0.5
    heads = [slice(h * XA_HEAD_DIM, (h + 1) * XA_HEAD_DIM) for h in range(XA_HEADS)]
    nsub = x_ref.shape[0] // XA_SUB

    def rows(u):
        return slice(u * XA_SUB, (u + 1) * XA_SUB)

    def scores(u):
        hq = _rms_scale(x_ref[rows(u), :], gx_ref[...], NORM_EPS).astype(BF16)
        q = jnp.dot(hq, wq_ref[...], preferred_element_type=F32).astype(BF16)
        return [_dot_nt(q[:, sl], kv_ref[:, sl]) for sl in heads]

    def finish(u, sc):
        outs = []
        for h in range(XA_HEADS):
            vsl = slice(XA_WIDTH + h * XA_HEAD_DIM, XA_WIDTH + (h + 1) * XA_HEAD_DIM)
            s = sc[h] * (scale * LOG2E)
            m = jnp.max(s, axis=-1, keepdims=True)
            e = jnp.exp2(s - m)
            l = jnp.sum(e, axis=-1, keepdims=True)
            o = jnp.dot(e.astype(BF16), kv_ref[:, vsl], preferred_element_type=F32)
            outs.append((o / l).astype(BF16))
        o = jnp.concatenate(outs, axis=1)
        x2 = x_ref[rows(u), :] + jnp.dot(o, wo_ref[...], preferred_element_type=F32)
        x2_ref[rows(u), :] = x2
        h3 = _rms_scale(x2, gf_ref[...], NORM_EPS)
        _store_row_tiles(hp_ref.at[pl.ds(u * XA_SUB * HP_TILES, XA_SUB * HP_TILES)], _pack_halves(h3))
        logits = jnp.dot(h3.astype(BF16), wr_ref[...], preferred_element_type=F32) + br_ref[...]
        gates, eids = _route(logits)
        gate_ref[rows(u), :] = gates
        eid_ref[rows(u), :] = eids

    pending = scores(0)
    for u in range(nsub):
        sc = pending
        if u + 1 < nsub:
            pending = scores(u + 1)
        finish(u, sc)


def _xattn(x1, kv, gx, wq, wo, gf, wr, br):
    t = x1.shape[0]
    tm = XA_TM
    per_batch = SEQ // tm
    const = lambda shape: pl.BlockSpec(shape, lambda i: (0, 0), pipeline_mode=pl.Buffered(1))
    tile = lambda n: pl.BlockSpec((tm, n), lambda i: (i, 0))
    return pl.pallas_call(
        _xattn_kernel,
        grid=(t // tm,),
        in_specs=[tile(D_MODEL),
                  pl.BlockSpec((MEM_LEN, 2 * XA_WIDTH), lambda i: (i // per_batch, 0)),
                  const((1, D_MODEL)), const((D_MODEL, XA_WIDTH)), const((XA_WIDTH, D_MODEL)),
                  const((1, D_MODEL)), const((D_MODEL, LANES)), const((1, LANES))],
        out_specs=[tile(D_MODEL), pl.BlockSpec((tm * HP_TILES, LANES), lambda i: (i, 0)), tile(LANES), tile(LANES)],
        out_shape=[jax.ShapeDtypeStruct((t, D_MODEL), F32),
                   jax.ShapeDtypeStruct((t * HP_TILES, LANES), U32),
                   jax.ShapeDtypeStruct((t, LANES), F32),
                   jax.ShapeDtypeStruct((t, LANES), I32)],
        compiler_params=pltpu.CompilerParams(dimension_semantics=("parallel",), vmem_limit_bytes=BIG_VMEM_LIMIT),
        name="xattn_router",
    )(x1, kv, gx, wq, wo, gf, wr, br)


IDX_ROWS = 8
PIECE = 8


def _rank_kernel(eid_ref, gate_ref, idx_ref, gidx_ref, tile_ref, cnt_ref, carry_ref):
    @pl.when(pl.program_id(0) == 0)
    def _():
        carry_ref[...] = jnp.zeros_like(carry_ref)

    tm = eid_ref.shape[0]
    eid = eid_ref[...]
    lane = lax.broadcasted_iota(I32, (tm, LANES), 1)
    e1 = eid[:, 0:1]
    e2 = eid[:, 1:2]
    m1 = lane == e1
    m2 = lane == e2
    onehot = jnp.where(m1, 1.0, jnp.where(m2, 1.0, 0.0))
    r = lax.broadcasted_iota(I32, (tm, tm), 0)
    c = lax.broadcasted_iota(I32, (tm, tm), 1)
    tri = jnp.where(c < r, 1.0, 0.0).astype(BF16)
    local = jnp.dot(tri, onehot.astype(BF16), preferred_element_type=F32)
    carry = carry_ref[0:1, :]
    n_tile = jnp.sum(onehot, axis=0, keepdims=True)

    n_piece = jnp.floor((n_tile + (PIECE - 1)) * (1.0 / PIECE)) * PIECE

    def assignment(m, e):
        r_local = jnp.sum(jnp.where(m, local, 0.0), axis=-1, keepdims=True)
        r_global = r_local + jnp.sum(jnp.where(m, carry, 0.0), axis=-1, keepdims=True)
        first = jnp.sum(jnp.where(lane < e, n_tile, 0.0), axis=-1, keepdims=True)
        first_piece = jnp.sum(jnp.where(lane < e, n_piece, 0.0), axis=-1, keepdims=True)
        return first + r_local, r_global, first_piece + r_local

    p1, g1, q1 = assignment(m1, e1)
    p2, g2, q2 = assignment(m2, e2)
    cols = [p1, p2, g1, g2, e1.astype(F32), e2.astype(F32), q1, q2]
    table = jnp.zeros((tm, LANES), F32)
    for j, col in enumerate(cols):
        table = jnp.where(lane == j, col, table)
    idx_ref[0] = jnp.transpose(table)[0:IDX_ROWS, :].astype(I32)
    gidx_ref[0] = jnp.transpose(gate_ref[...])[0:IDX_ROWS, :]
    tile_ref[0] = jnp.where(lax.broadcasted_iota(I32, (8, LANES), 0) == 0, carry, n_tile)
    carry_ref[...] = carry_ref[...] + n_tile
    cnt_ref[...] = carry_ref[...]


def _rank(eid, gates):
    t = eid.shape[0]
    tm = RANK_TM
    n_tiles = t // tm
    return pl.pallas_call(
        _rank_kernel,
        grid=(n_tiles,),
        in_specs=[pl.BlockSpec((tm, LANES), lambda i: (i, 0)), pl.BlockSpec((tm, LANES), lambda i: (i, 0))],
        out_specs=[pl.BlockSpec((1, IDX_ROWS, tm), lambda i: (i, 0, 0)),
                   pl.BlockSpec((1, IDX_ROWS, tm), lambda i: (i, 0, 0)),
                   pl.BlockSpec((1, 8, LANES), lambda i: (i, 0, 0)),
                   pl.BlockSpec((8, LANES), lambda i: (0, 0))],
        out_shape=[jax.ShapeDtypeStruct((n_tiles, IDX_ROWS, tm), I32),
                   jax.ShapeDtypeStruct((n_tiles, IDX_ROWS, tm), F32),
                   jax.ShapeDtypeStruct((n_tiles, 8, LANES), F32),
                   jax.ShapeDtypeStruct((8, LANES), F32)],
        scratch_shapes=[pltpu.VMEM((8, LANES), F32)],
        compiler_params=_cparams(("arbitrary",)),
        name="expert_rank",
    )(eid, gates)


def _dispatch_kernel(n_steps, cnt_ref, pend_ref, nused_ref, pos_ref, run_ref, run_prev_ref, hp_ref, xs_ref, sbuf,
                     zero_ref, sem, zsem):
    i = pl.program_id(0)
    tm = hp_ref.shape[0] // (2 * HP_TILES)
    zb = zero_ref.shape[0]
    n_blocks = xs_ref.shape[0] // zb

    @pl.when(i == 0)
    def _():
        zero_ref[...] = jnp.zeros_like(zero_ref)
        for b in range(2):
            sbuf[b, pl.ds(TOP_K * tm * HP_TILES, PIECE * HP_TILES), :] = jnp.zeros((PIECE * HP_TILES, LANES), U32)

        def zero_copy(block):
            return pltpu.make_async_copy(zero_ref, xs_ref.at[pl.ds(pl.multiple_of(block * zb, zb), zb)], zsem)

        def tail_start(j, carry):
            zero_copy(j).start()
            return carry

        def tail_wait(j, carry):
            zero_copy(j).wait()
            return carry

        for e in range(N_EXPERTS):
            @pl.when(cnt_ref[e] > 0)
            def _():
                zero_copy(pend_ref[e] - 1).start()
        lax.fori_loop(nused_ref[0], n_blocks, tail_start, 0)
        for e in range(N_EXPERTS):
            @pl.when(cnt_ref[e] > 0)
            def _():
                zero_copy(pend_ref[e] - 1).wait()
        lax.fori_loop(nused_ref[0], n_blocks, tail_wait, 0)

    def place(half):
        def body(t, carry):
            row = hp_ref[pl.ds(pl.multiple_of((half * tm + t) * HP_TILES, HP_TILES), HP_TILES), :]
            for k in range(TOP_K):
                sbuf[half, pl.ds(pl.multiple_of(pos_ref[half, k, t] * HP_TILES, HP_TILES), HP_TILES), :] = row
            return carry
        lax.fori_loop(0, tm, body, 0, unroll=16)

    def piece_copy(src_tok, dst_slot, half):
        rows = PIECE * HP_TILES
        return pltpu.make_async_copy(sbuf.at[half, pl.ds(pl.multiple_of(src_tok * HP_TILES, HP_TILES), rows)],
                                     xs_ref.at[pl.ds(pl.multiple_of(dst_slot * HP_TILES, HP_TILES), rows)],
                                     sem.at[half])

    def issue(half):
        for e in range(N_EXPERTS):
            slot0 = run_ref[half, 0, e]
            first = run_ref[half, 0, 2 * N_EXPERTS + e]

            def run_piece(j, carry):
                piece_copy(first + PIECE * j, slot0 + PIECE * j, half).start()
                return carry

            lax.fori_loop(0, run_ref[half, 0, N_EXPERTS + e], run_piece, 0)

    def drain(runs, half):
        def wait_piece(j, carry):
            piece_copy(0, 0, half).wait()
            return carry
        lax.fori_loop(0, runs[half, 0, 3 * N_EXPERTS], wait_piece, 0)

    place(0)

    @pl.when(i > 0)
    def _():
        drain(run_prev_ref, 1)
    issue(0)
    place(1)
    drain(run_ref, 0)
    issue(1)

    @pl.when(i == n_steps - 1)
    def _():
        drain(run_ref, 1)


def _dispatch(cnt, pend_blocks, nused, pos, runs, hp, n_slots):
    n_tiles, _, tm = pos.shape
    n_steps = n_tiles // 2
    grid_spec = pltpu.PrefetchScalarGridSpec(
        num_scalar_prefetch=3,
        grid=(n_steps,),
        in_specs=[pl.BlockSpec((2, TOP_K, tm), lambda i, *_: (i, 0, 0), memory_space=pltpu.SMEM),
                  pl.BlockSpec((2, 1, LANES), lambda i, *_: (i, 0, 0), memory_space=pltpu.SMEM),
                  pl.BlockSpec((2, 1, LANES), lambda i, *_: (jnp.maximum(i - 1, 0), 0, 0), memory_space=pltpu.SMEM),
                  pl.BlockSpec((2 * tm * HP_TILES, LANES), lambda i, *_: (i, 0))],
        out_specs=pl.BlockSpec(memory_space=pl.ANY),
        scratch_shapes=[pltpu.VMEM((2, (TOP_K * tm + PIECE) * HP_TILES, LANES), U32),
                        pltpu.VMEM((MOE_TM * HP_TILES, LANES), U32),
                        pltpu.SemaphoreType.DMA((2,)), pltpu.SemaphoreType.DMA(())],
    )
    return pl.pallas_call(
        functools.partial(_dispatch_kernel, n_steps),
        grid_spec=grid_spec,
        out_shape=jax.ShapeDtypeStruct((n_slots * HP_TILES, LANES), U32),
        compiler_params=_cparams(("arbitrary",)),
        name="moe_dispatch",
    )(cnt, pend_blocks, nused, pos, runs, runs, hp)


MOE_DOWN_CHUNKS = 4


def _moe_kernel(bexp_ref, nused_ref, xs_ref, wg_ref, wu_ref, wd_ref, o_ref, wgb, wub, wdb):
    i = pl.program_id(0)
    nused = nused_ref[0]
    tm = xs_ref.shape[0] // HP_TILES

    @pl.when((i < nused) & ((i == 0) | (bexp_ref[i] != bexp_ref[jnp.maximum(i - 1, 0)])))
    def _():
        wgb[...] = wg_ref[0].astype(BF16)
        wub[...] = wu_ref[0].astype(BF16)
        wdb[...] = wd_ref[0].astype(BF16)

    @pl.when(i < nused)
    def _():
        x = _unpack_halves(_load_row_tiles(xs_ref, tm, HP_TILES))
        hg = jnp.dot(x, wgb[...], preferred_element_type=F32)
        hu = jnp.dot(x, wub[...], preferred_element_type=F32)
        hdn = ((hg * jax.nn.sigmoid(hg)) * hu).astype(BF16)
        cw = D_MODEL // MOE_DOWN_CHUNKS
        for c in range(MOE_DOWN_CHUNKS):
            y = jnp.dot(hdn, wdb[:, c * cw:(c + 1) * cw], preferred_element_type=F32)
            for j in range(cw // LANES):
                o_ref[pl.ds(c * (cw // LANES) + j, tm, stride=YS_TILES), :] = y[:, j * LANES:(j + 1) * LANES]

    @pl.when(i >= nused)
    def _():
        o_ref[...] = jnp.zeros_like(o_ref)


def _moe(bexp, nused, xs, wg, wu, wd):
    tm = MOE_TM
    n_slots = xs.shape[0] // HP_TILES
    n_blocks = n_slots // tm
    grid_spec = pltpu.PrefetchScalarGridSpec(
        num_scalar_prefetch=2,
        grid=(n_blocks,),
        in_specs=[pl.BlockSpec((tm * HP_TILES, LANES), lambda i, be, nu: (jnp.minimum(i, nu[0] - 1), 0)),
                  pl.BlockSpec((1, D_MODEL, D_EXPERT), lambda i, be, nu: (be[i], 0, 0)),
                  pl.BlockSpec((1, D_MODEL, D_EXPERT), lambda i, be, nu: (be[i], 0, 0)),
                  pl.BlockSpec((1, D_EXPERT, D_MODEL), lambda i, be, nu: (be[i], 0, 0))],
        out_specs=pl.BlockSpec((tm * YS_TILES, LANES), lambda i, be, nu: (i, 0)),
        scratch_shapes=[pltpu.VMEM((D_MODEL, D_EXPERT), BF16), pltpu.VMEM((D_MODEL, D_EXPERT), BF16),
                        pltpu.VMEM((D_EXPERT, D_MODEL), BF16)],
    )
    return pl.pallas_call(
        _moe_kernel,
        grid_spec=grid_spec,
        out_shape=jax.ShapeDtypeStruct((n_slots * YS_TILES, LANES), F32),
        compiler_params=_cparams(("arbitrary",)),
        name="moe_experts",
    )(bexp, nused, xs, wg, wu, wd)


def _combine_kernel(n_tiles, pos_ref, run_ref, run_next_ref, gate_ref, x_ref, g_ref, ys_ref, o_ref, ybuf, mbuf, sem):
    i = pl.program_id(0)
    tm = x_ref.shape[0]
    slot = lax.rem(i, 2)

    def piece_copy(src_slot, dst_pos, buf):
        rows = PIECE * YS_TILES
        return pltpu.make_async_copy(ys_ref.at[pl.ds(pl.multiple_of(src_slot * YS_TILES, YS_TILES), rows)],
                                     ybuf.at[buf, pl.ds(pl.multiple_of(dst_pos * YS_TILES, YS_TILES), rows)],
                                     sem.at[buf])

    def request(runs, buf):
        for e in range(N_EXPERTS):
            slot0 = runs[0, 0, e]
            first = runs[0, 0, 2 * N_EXPERTS + e]

            def run_piece(j, carry):
                piece_copy(slot0 + PIECE * j, first + PIECE * j, buf).start()
                return carry

            lax.fori_loop(0, runs[0, 0, N_EXPERTS + e], run_piece, 0)

    @pl.when(i == 0)
    def _():
        request(run_ref, 0)

    @pl.when(i + 1 < n_tiles)
    def _():
        request(run_next_ref, 1 - slot)

    def drain(j, carry):
        piece_copy(0, 0, slot).wait()
        return carry

    lax.fori_loop(0, run_ref[0, 0, 3 * N_EXPERTS], drain, 0)

    def token(t, carry):
        acc = None
        for k in range(TOP_K):
            rows = pl.ds(pl.multiple_of(pos_ref[0, k, t] * YS_TILES, YS_TILES), YS_TILES)
            term = ybuf[slot, rows, :] * gate_ref[0, k, t]
            acc = term if acc is None else acc + term
        mbuf[pl.ds(pl.multiple_of(t * YS_TILES, YS_TILES), YS_TILES), :] = acc
        return carry

    lax.fori_loop(0, tm, token, 0, unroll=16)
    moe = _load_row_tiles(mbuf, tm, YS_TILES)
    o_ref[...] = _rms_scale(x_ref[...] + moe, g_ref[...], NORM_EPS)


def _combine(pos, runs, gidx, x2, g, ys, row0, n_rows):
    _, _, tm = pos.shape
    tile0 = row0 // tm
    n = n_rows // tm
    smem = lambda rows: pl.BlockSpec((1, rows, tm), lambda i: (tile0 + i, 0, 0), memory_space=pltpu.SMEM)
    run_spec = lambda ahead: pl.BlockSpec((1, 1, LANES), lambda i: (tile0 + jnp.minimum(i + ahead, n - 1), 0, 0),
                                          memory_space=pltpu.SMEM)
    return pl.pallas_call(
        functools.partial(_combine_kernel, n),
        grid=(n,),
        in_specs=[smem(TOP_K), run_spec(0), run_spec(1), smem(TOP_K),
                  pl.BlockSpec((tm, D_MODEL), lambda i: (tile0 + i, 0)),
                  pl.BlockSpec((1, D_MODEL), lambda i: (0, 0)),
                  pl.BlockSpec(memory_space=pl.ANY)],
        out_specs=pl.BlockSpec((tm, D_MODEL), lambda i: (i, 0)),
        out_shape=jax.ShapeDtypeStruct((n_rows, D_MODEL), F32),
        scratch_shapes=[pltpu.VMEM((2, (TOP_K * tm + N_EXPERTS * PIECE) * YS_TILES, LANES), F32),
                        pltpu.VMEM((tm * YS_TILES, LANES), F32),
                        pltpu.SemaphoreType.DMA((2,))],
        compiler_params=_cparams(("arbitrary",)),
        name="moe_combine",
    )(pos, runs, runs, gidx[:, 0:TOP_K, :], x2, g, ys)


def _routing_plan(idx, tiles, counts):
    cnt = counts[0, :N_EXPERTS].astype(I32)
    padded = jnp.where(cnt > 0, (cnt + PIECE - 1 + MOE_TM - 1) // MOE_TM * MOE_TM, 0)
    pends = jnp.cumsum(padded)
    pstarts = pends - padded
    n_tiles, _, tm = idx.shape
    n_blocks = n_tiles * tm * TOP_K // MOE_TM + N_EXPERTS + 1
    blk0 = jnp.arange(n_blocks, dtype=I32) * MOE_TM
    bexp = jnp.minimum(jnp.sum(pends[None, :] <= blk0[:, None], axis=1), N_EXPERTS - 1).astype(I32)
    nused = (pends[-1:] // MOE_TM).astype(I32)
    before = tiles[:, 0, :N_EXPERTS].astype(I32)
    inside = tiles[:, 1, :N_EXPERTS].astype(I32)
    first = jnp.cumsum(inside, axis=1) - inside
    pieces = (inside + PIECE - 1) // PIECE
    first_piece = (jnp.cumsum(pieces, axis=1) - pieces) * PIECE

    def run_table(first_pos):
        return jnp.concatenate([pstarts[None, :] + before, pieces, first_pos, jnp.sum(pieces, axis=1, keepdims=True),
                                jnp.zeros((n_tiles, LANES - 3 * N_EXPERTS - 1), I32)], axis=1).reshape(n_tiles, 1, LANES)

    pos = idx[:, 0:TOP_K, :]
    pos_piece = idx[:, 3 * TOP_K:4 * TOP_K, :]
    return pos, run_table(first), pos_piece, run_table(first_piece), cnt, (pends // MOE_TM).astype(I32), bexp, nused, n_blocks * MOE_TM


def kernel(x_prompt, x_sample, mem_prompt, mem_sample, mix_norm_g, w_in, na_rpb, diff_lambda_q1, diff_lambda_k1, diff_lambda_q2, diff_lambda_k2, diff_subln_g, w_branch_na, w_branch_diff, w_out, xa_norm_g, mem_norm_g, xa_w_q, xa_w_kv, xa_w_o, ffn_norm_g, router_group_w, router_group_b, router_expert_w, router_expert_b, w_gate, w_up, w_down, final_norm_g):
    nb_p, nb_s = x_prompt.shape[0], x_sample.shape[0]
    nb = nb_p + nb_s
    t = nb * SEQ
    xp = x_prompt.reshape(nb_p * SEQ, D_MODEL)
    xs_in = x_sample.reshape(nb_s * SEQ, D_MODEL)
    mem_p = mem_prompt.reshape(nb_p * MEM_LEN, D_MODEL)
    mem_s = mem_sample.reshape(nb_s * MEM_LEN, D_MODEL)
    row = lambda v: v.reshape(1, -1).astype(F32)
    bf = lambda w: w.astype(BF16)

    qcol = 3 * NA_WIDTH
    colscale = jnp.ones((1, IN_COLS), F32).at[:, qcol:qcol + DIFF_QK_WIDTH].set(DIFF_QK_DIM ** -0.5 * LOG2E)
    proj = _inproj(xp, xs_in, row(mix_norm_g[0]), bf(w_in[0]), colscale)
    proj3 = proj.reshape(nb, SEQ, IN_COLS)
    a = _na_attention(proj3, _na_bias_blocks(na_rpb[0]))
    b = _diff_attention(proj3, row(diff_lambda_q1[0]), row(diff_lambda_k1[0]),
                        row(diff_lambda_q2[0]), row(diff_lambda_k2[0]), row(diff_subln_g[0]))
    x1 = _merge(a.reshape(t, NA_WIDTH), b.reshape(t, DIFF_V_WIDTH), proj, xp, xs_in,
                bf(w_branch_na[0]), bf(w_branch_diff[0]), bf(w_out[0]))
    kv = _memkv(mem_p, mem_s, row(mem_norm_g[0]), bf(xa_w_kv[0]))
    w_router = jnp.zeros((D_MODEL, LANES), F32)
    w_router = w_router.at[:, :N_GROUPS].set(router_group_w[0]).at[:, N_GROUPS:N_GROUPS + N_EXPERTS].set(router_expert_w[0])
    b_router = jnp.zeros((1, LANES), F32)
    b_router = b_router.at[0, :N_GROUPS].set(router_group_b[0]).at[0, N_GROUPS:N_GROUPS + N_EXPERTS].set(router_expert_b[0])
    x2, hp, gates, eid = _xattn(x1, kv, row(xa_norm_g[0]), bf(xa_w_q[0]), bf(xa_w_o[0]),
                                row(ffn_norm_g[0]), bf(w_router), b_router)
    idx, gidx, tiles, counts = _rank(eid, gates)
    pos, runs, pos_c, runs_c, cnt, pend_blocks, bexp, nused, n_slots = _routing_plan(idx, tiles, counts)
    xs = _dispatch(cnt, pend_blocks, nused, pos, runs, hp, n_slots)
    ys = _moe(bexp, nused, xs, w_gate[0], w_up[0], w_down[0])
    fg = row(final_norm_g)
    y_p = _combine(pos_c, runs_c, gidx, x2, fg, ys, 0, nb_p * SEQ).reshape(nb_p, SEQ, D_MODEL)
    y_s = _combine(pos_c, runs_c, gidx, x2, fg, ys, nb_p * SEQ, nb_s * SEQ).reshape(nb_s, SEQ, D_MODEL)
    return (y_p, y_s)
```

```python
import functools
import math

import jax
import jax.numpy as jnp
from jax import lax
from jax.experimental import pallas as pl
from jax.experimental.pallas import tpu as pltpu

F32 = jnp.float32
BF16 = jnp.bfloat16
I32 = jnp.int32
U32 = jnp.uint32

D_MODEL = 2048
SEQ = 2048
GRID_W = 64
ROWS = SEQ // GRID_W
NA_HEADS = 8
NA_HEAD_DIM = 128
NA_WIDTH = NA_HEADS * NA_HEAD_DIM
NA_KH = 8
NA_KW = 16
DIFF_HEADS = 8
DIFF_QK_DIM = 64
DIFF_V_DIM = 128
DIFF_QK_WIDTH = DIFF_HEADS * 2 * DIFF_QK_DIM
DIFF_V_WIDTH = DIFF_HEADS * DIFF_V_DIM
IN_COLS = 3 * NA_WIDTH + 2 * DIFF_QK_WIDTH + DIFF_V_WIDTH + 2 * D_MODEL
MEM_LEN = 256
XA_HEADS = 4
XA_HEAD_DIM = 128
XA_WIDTH = XA_HEADS * XA_HEAD_DIM
N_GROUPS = 4
EXPERTS_PER_GROUP = 8
N_EXPERTS = N_GROUPS * EXPERTS_PER_GROUP
TOP_K = 2
D_EXPERT = 512
NORM_EPS = 1e-6
SUBLN_EPS = 1e-5
NEG_INF = -1e30
LAMBDA_INIT = 0.8 - 0.6 * math.exp(-0.3 * 0)
LOG2E = math.log2(math.e)

LANES = 128
HP_TILES = D_MODEL // 2 // LANES
YS_TILES = D_MODEL // LANES
VMEM_LIMIT = 56 * 1024 * 1024
BIG_VMEM_LIMIT = 60 * 1024 * 1024

INPROJ_TM = 1024
INPROJ_TN = 1280
NORM_CHUNK = 128
DIFF_TQ = 2048
DIFF_SUB = 128
MERGE_TM = 512
XA_TM = 1024
XA_SUB = 256
RANK_TM = 512
MOE_TM = 512


def _cparams(sem):
    return pltpu.CompilerParams(dimension_semantics=sem, vmem_limit_bytes=VMEM_LIMIT)


def _rms_scale(x, g, eps):
    ms = jnp.mean(x * x, axis=-1, keepdims=True)
    return x * lax.rsqrt(ms + eps) * g


def _dot_nt(a, b):
    return lax.dot_general(a, b, (((1,), (1,)), ((), ())), preferred_element_type=F32)


def _two_group_specs(tm, width, n_first):
    first = pl.BlockSpec((tm, width), lambda i, *_: (jnp.minimum(i, n_first - 1), 0))
    second = pl.BlockSpec((tm, width), lambda i, *_: (jnp.maximum(i - n_first, 0), 0))
    return first, second


def _inproj_kernel(n_first, xp_ref, xs_ref, g_ref, w_ref, cs_ref, o_ref, h_ref):
    def normalise(x_ref):
        def chunk(c, carry):
            rows = pl.ds(pl.multiple_of(c * NORM_CHUNK, NORM_CHUNK), NORM_CHUNK)
            h_ref[rows, :] = _rms_scale(x_ref[rows, :], g_ref[...], NORM_EPS).astype(BF16)
            return carry
        lax.fori_loop(0, x_ref.shape[0] // NORM_CHUNK, chunk, 0)

    first_col = pl.program_id(1) == 0
    in_first = pl.program_id(0) < n_first
    pl.when(first_col & in_first)(lambda: normalise(xp_ref))
    pl.when(first_col & jnp.logical_not(in_first))(lambda: normalise(xs_ref))
    acc = jnp.dot(h_ref[...], w_ref[...], preferred_element_type=F32)
    o_ref[...] = (acc * cs_ref[...]).astype(o_ref.dtype)


def _inproj(xp, xs, g, w, colscale):
    d = xp.shape[1]
    t = xp.shape[0] + xs.shape[0]
    n = w.shape[1]
    tm = INPROJ_TM
    n_first = xp.shape[0] // tm
    spec_p, spec_s = _two_group_specs(tm, d, n_first)
    return pl.pallas_call(
        functools.partial(_inproj_kernel, n_first),
        grid=(t // tm, n // INPROJ_TN),
        in_specs=[spec_p, spec_s,
                  pl.BlockSpec((1, d), lambda i, j: (0, 0)),
                  pl.BlockSpec((d, INPROJ_TN), lambda i, j: (0, j)),
                  pl.BlockSpec((1, INPROJ_TN), lambda i, j: (0, j))],
        out_specs=pl.BlockSpec((tm, INPROJ_TN), lambda i, j: (i, j)),
        out_shape=jax.ShapeDtypeStruct((t, n), BF16),
        scratch_shapes=[pltpu.VMEM((tm, d), BF16)],
        compiler_params=pltpu.CompilerParams(dimension_semantics=("parallel", "arbitrary"),
                                             vmem_limit_bytes=BIG_VMEM_LIMIT),
        name="inproj",
    )(xp, xs, g, w, colscale)


NA_QROWS = 4
NA_KROWS = 12
NA_TILES = ROWS // NA_QROWS
NA_HEADS_PER_STEP = 4


def _na_tile_key_row(ti):
    return min(max(NA_QROWS * ti - NA_KH // 2, 0), ROWS - NA_KROWS)


def _na_bias_blocks(rpb):
    cols = jnp.arange(GRID_W)
    cstart = jnp.clip(cols - NA_KW // 2, 0, GRID_W - NA_KW)
    col_ok = (cols[None, :] >= cstart[:, None]) & (cols[None, :] < cstart[:, None] + NA_KW)
    dc = jnp.clip(cols[None, :] - cols[:, None], 1 - NA_KW, NA_KW - 1) + NA_KW - 1
    pick = (dc[None] == jnp.arange(2 * NA_KW - 1)[:, None, None]).astype(F32)
    t = jnp.einsum("hdx,xck->hdck", rpb.astype(F32), pick, precision=lax.Precision.HIGHEST)
    return jnp.where(col_ok[None, None], t, NEG_INF)


def _na_kernel(q_ref, k_ref, v_ref, t_ref, o_ref, bias_ref):
    scale = NA_HEAD_DIM ** -0.5
    tq = NA_QROWS * GRID_W
    tk = NA_KROWS * GRID_W
    hd = NA_HEAD_DIM

    @pl.when(pl.program_id(1) == 0)
    def _():
        masked = jnp.full((GRID_W, GRID_W), NEG_INF, F32)
        for hh in range(NA_HEADS_PER_STEP):
            for variant, ti in enumerate((0, 1, NA_TILES - 1)):
                for jr in range(NA_QROWS):
                    r = NA_QROWS * ti + jr
                    rs = min(max(r - NA_KH // 2, 0), ROWS - NA_KH)
                    blocks = []
                    for i in range(NA_KROWS):
                        kr = _na_tile_key_row(ti) + i
                        blocks.append(t_ref[hh, kr - r + NA_KH - 1] if rs <= kr < rs + NA_KH else masked)
                    bias_ref[hh, variant, jr * GRID_W:(jr + 1) * GRID_W, :] = jnp.concatenate(blocks, axis=1) * LOG2E

    def key_rows(ti):
        k0 = _na_tile_key_row(ti) * GRID_W
        return slice(k0, k0 + tk)

    work = [(hh, ti) for hh in range(NA_HEADS_PER_STEP) for ti in range(NA_TILES)]

    def scores(item):
        hh, ti = item
        cols = slice(hh * hd, (hh + 1) * hd)
        return _dot_nt(q_ref[0, ti * tq:(ti + 1) * tq, cols], k_ref[0, key_rows(ti), cols])

    pending = scores(work[0])
    for n, (hh, ti) in enumerate(work):
        variant = 0 if ti == 0 else (2 if ti == NA_TILES - 1 else 1)
        cols = slice(hh * hd, (hh + 1) * hd)
        qrows = slice(ti * tq, (ti + 1) * tq)
        s = pending * (scale * LOG2E) + bias_ref[hh, variant]
        if n + 1 < len(work):
            pending = scores(work[n + 1])
        m = jnp.max(s, axis=-1, keepdims=True)
        e = jnp.exp2(s - m)
        l = jnp.sum(e, axis=-1, keepdims=True)
        o = jnp.dot(e.astype(BF16), v_ref[0, key_rows(ti), cols], preferred_element_type=F32)
        o_ref[0, qrows, cols] = (o / l).astype(o_ref.dtype)


def _na_attention(proj3, blocks):
    b = proj3.shape[0]
    hp = NA_HEADS_PER_STEP
    width = hp * NA_HEAD_DIM
    groups = NA_HEADS // hp
    return pl.pallas_call(
        _na_kernel,
        grid=(groups, b),
        in_specs=[pl.BlockSpec((1, SEQ, width), lambda h, i: (i, 0, h)),
                  pl.BlockSpec((1, SEQ, width), lambda h, i: (i, 0, groups + h)),
                  pl.BlockSpec((1, SEQ, width), lambda h, i: (i, 0, 2 * groups + h)),
                  pl.BlockSpec((hp,) + blocks.shape[1:], lambda h, i: (h, 0, 0, 0))],
        out_specs=pl.BlockSpec((1, SEQ, width), lambda h, i: (i, 0, h)),
        out_shape=jax.ShapeDtypeStruct((b, SEQ, NA_WIDTH), BF16),
        scratch_shapes=[pltpu.VMEM((hp, 3, NA_QROWS * GRID_W, NA_KROWS * GRID_W), F32)],
        compiler_params=_cparams(("arbitrary", "arbitrary")),
        name="na_attn",
    )(proj3, proj3, proj3, blocks)


def _alibi_table(tq):
    slopes = jnp.exp2(-8.0 * jnp.arange(1, DIFF_HEADS + 1, dtype=F32) / DIFF_HEADS)
    r = jnp.arange(tq)[:, None]
    x = jnp.arange(2 * SEQ - tq)[None, :]
    dist = jnp.abs(r - x + (SEQ - tq)).astype(F32)
    return (slopes * LOG2E)[:, None, None] * dist[None]


def _diff_kernel(lq1_ref, lk1_ref, lq2_ref, lk2_ref, q_ref, k_ref, v_ref, g_ref, alibi_ref, o_ref):
    qi = pl.program_id(2)
    ts = alibi_ref.shape[1]
    nsub = q_ref.shape[1] // ts
    lam = (jnp.exp(jnp.sum(lq1_ref[...] * lk1_ref[...], axis=-1, keepdims=True))
           - jnp.exp(jnp.sum(lq2_ref[...] * lk2_ref[...], axis=-1, keepdims=True)) + LAMBDA_INIT)
    k = k_ref[0]

    def scores(u):
        q = q_ref[0, u * ts:(u + 1) * ts, :]
        return [_dot_nt(q[:, mi * DIFF_QK_DIM:(mi + 1) * DIFF_QK_DIM], k[:, mi * DIFF_QK_DIM:(mi + 1) * DIFF_QK_DIM])
                for mi in range(2)]

    def expo(s, bias):
        s = s - bias
        m = jnp.max(s, axis=-1, keepdims=True)
        e = jnp.exp2(s - m)
        return e, jnp.sum(e, axis=-1, keepdims=True)

    pending = scores(0)
    for u in range(nsub):
        s0, s1 = pending
        if u + 1 < nsub:
            pending = scores(u + 1)
        off = (SEQ - ts) - (qi * nsub + u) * ts
        bias = alibi_ref[0, :, pl.ds(pl.multiple_of(off, ts), SEQ)]
        e0, l0 = expo(s0, bias)
        e1, l1 = expo(s1, bias)
        a = (e0 - e1 * (lam * l0 / l1)).astype(BF16)
        o = jnp.dot(a, v_ref[0], preferred_element_type=F32) / l0
        o = _rms_scale(o, g_ref[...], SUBLN_EPS) * (1.0 - LAMBDA_INIT)
        o_ref[0, u * ts:(u + 1) * ts, :] = o.astype(o_ref.dtype)


def _diff_attention(proj3, lq1, lk1, lq2, lk2, subln_g):
    b = proj3.shape[0]
    tq = DIFF_TQ
    qoff = 3 * NA_WIDTH // LANES
    koff = qoff + DIFF_QK_WIDTH // LANES
    voff = koff + DIFF_QK_WIDTH // LANES
    vec = lambda n: pl.BlockSpec((1, n), lambda h, i, j: (0, 0))
    return pl.pallas_call(
        _diff_kernel,
        grid=(DIFF_HEADS, b, SEQ // tq),
        in_specs=[vec(DIFF_QK_DIM), vec(DIFF_QK_DIM), vec(DIFF_QK_DIM), vec(DIFF_QK_DIM),
                  pl.BlockSpec((1, tq, LANES), lambda h, i, j: (i, j, qoff + h)),
                  pl.BlockSpec((1, SEQ, LANES), lambda h, i, j: (i, 0, koff + h)),
                  pl.BlockSpec((1, SEQ, LANES), lambda h, i, j: (i, 0, voff + h)),
                  vec(DIFF_V_DIM),
                  pl.BlockSpec((1, DIFF_SUB, 2 * SEQ - DIFF_SUB), lambda h, i, j: (h, 0, 0))],
        out_specs=pl.BlockSpec((1, tq, DIFF_V_DIM), lambda h, i, j: (i, j, h)),
        out_shape=jax.ShapeDtypeStruct((b, SEQ, DIFF_V_WIDTH), BF16),
        compiler_params=_cparams(("arbitrary", "arbitrary", "arbitrary")),
        name="diff_attn",
    )(lq1, lk1, lq2, lk2, proj3, proj3, proj3, subln_g, _alibi_table(DIFF_SUB))


def _merge_kernel(n_first, a_ref, b_ref, gna_ref, gdf_ref, xp_ref, xs_ref, wna_ref, wdf_ref, wout_ref, o_ref):
    pa = jnp.dot(a_ref[...], wna_ref[...], preferred_element_type=F32)
    pb = jnp.dot(b_ref[...], wdf_ref[...], preferred_element_type=F32)
    merged = (jax.nn.sigmoid(gna_ref[...].astype(F32)) * pa
              + jax.nn.sigmoid(gdf_ref[...].astype(F32)) * pb)
    delta = jnp.dot(merged.astype(BF16), wout_ref[...], preferred_element_type=F32)
    in_first = pl.program_id(0) < n_first

    @pl.when(in_first)
    def _():
        o_ref[...] = xp_ref[...] + delta

    @pl.when(jnp.logical_not(in_first))
    def _():
        o_ref[...] = xs_ref[...] + delta


def _merge(a, b, proj, xp, xs, wna, wdf, wout):
    t = xp.shape[0] + xs.shape[0]
    tm = MERGE_TM
    gna_blk = (3 * NA_WIDTH + 2 * DIFF_QK_WIDTH + DIFF_V_WIDTH) // D_MODEL
    const = lambda shape: pl.BlockSpec(shape, lambda i: (0, 0), pipeline_mode=pl.Buffered(1))
    n_first = xp.shape[0] // tm
    spec_p, spec_s = _two_group_specs(tm, D_MODEL, n_first)
    return pl.pallas_call(
        functools.partial(_merge_kernel, n_first),
        grid=(t // tm,),
        in_specs=[pl.BlockSpec((tm, NA_WIDTH), lambda i: (i, 0)),
                  pl.BlockSpec((tm, DIFF_V_WIDTH), lambda i: (i, 0)),
                  pl.BlockSpec((tm, D_MODEL), lambda i: (i, gna_blk)),
                  pl.BlockSpec((tm, D_MODEL), lambda i: (i, gna_blk + 1)),
                  spec_p, spec_s,
                  const((NA_WIDTH, D_MODEL)), const((DIFF_V_WIDTH, D_MODEL)), const((D_MODEL, D_MODEL))],
        out_specs=pl.BlockSpec((tm, D_MODEL), lambda i: (i, 0)),
        out_shape=jax.ShapeDtypeStruct((t, D_MODEL), F32),
        compiler_params=pltpu.CompilerParams(dimension_semantics=("parallel",), vmem_limit_bytes=BIG_VMEM_LIMIT),
        name="merge_outproj",
    )(a, b, proj, proj, xp, xs, wna, wdf, wout)


def _memkv_kernel(n_first, mp_ref, ms_ref, g_ref, w_ref, o_ref):
    def project(m_ref):
        h = _rms_scale(m_ref[...], g_ref[...], NORM_EPS).astype(BF16)
        o_ref[...] = jnp.dot(h, w_ref[...], preferred_element_type=F32).astype(o_ref.dtype)

    in_first = pl.program_id(0) < n_first
    pl.when(in_first)(lambda: project(mp_ref))
    pl.when(jnp.logical_not(in_first))(lambda: project(ms_ref))


def _memkv(mem_p, mem_s, g, w):
    t = mem_p.shape[0] + mem_s.shape[0]
    n_first = mem_p.shape[0] // MEM_LEN
    spec_p, spec_s = _two_group_specs(MEM_LEN, D_MODEL, n_first)
    return pl.pallas_call(
        functools.partial(_memkv_kernel, n_first),
        grid=(t // MEM_LEN,),
        in_specs=[spec_p, spec_s,
                  pl.BlockSpec((1, D_MODEL), lambda i: (0, 0)),
                  pl.BlockSpec((D_MODEL, 2 * XA_WIDTH), lambda i: (0, 0))],
        out_specs=pl.BlockSpec((MEM_LEN, 2 * XA_WIDTH), lambda i: (i, 0)),
        out_shape=jax.ShapeDtypeStruct((t, 2 * XA_WIDTH), BF16),
        compiler_params=_cparams(("parallel",)),
        name="mem_kv",
    )(mem_p, mem_s, g, w)


def _route(logits):
    lane = lax.broadcasted_iota(I32, logits.shape, 1).astype(F32)
    ninf = -jnp.inf
    big = float(LANES)
    gl = jnp.where(lane < N_GROUPS, logits, ninf)
    gmax = jnp.max(gl, axis=-1, keepdims=True)
    g = jnp.min(jnp.where(gl == gmax, lane, big), axis=-1, keepdims=True)
    pg = 1.0 / jnp.sum(jnp.exp(gl - gmax), axis=-1, keepdims=True)
    lo = N_GROUPS + EXPERTS_PER_GROUP * g
    el = jnp.where((lane >= lo) & (lane < lo + EXPERTS_PER_GROUP), logits, ninf)
    v1 = jnp.max(el, axis=-1, keepdims=True)
    i1 = jnp.min(jnp.where(el == v1, lane, big), axis=-1, keepdims=True)
    el2 = jnp.where(lane == i1, ninf, el)
    v2 = jnp.max(el2, axis=-1, keepdims=True)
    i2 = jnp.min(jnp.where(el2 == v2, lane, big), axis=-1, keepdims=True)
    t = jnp.exp(v2 - v1)
    den = 1.0 + t
    gate1 = pg * (1.0 / den)
    gate2 = pg * (t / den)
    gates = jnp.where(lane == 0.0, gate1, jnp.where(lane == 1.0, gate2, 0.0))
    eids = jnp.where(lane == 0.0, i1 - N_GROUPS, jnp.where(lane == 1.0, i2 - N_GROUPS, 0.0)).astype(I32)
    return gates, eids


def _pack_halves(h):
    half = h.shape[1] // 2
    hi = pltpu.bitcast(h[:, :half].astype(BF16).astype(F32), U32)
    lo = pltpu.bitcast(h[:, half:].astype(BF16).astype(F32), U32)
    return hi | (lo >> 16)


def _unpack_halves(w):
    hi = pltpu.bitcast(w & jnp.uint32(0xFFFF0000), F32).astype(BF16)
    lo = pltpu.bitcast(w << 16, F32).astype(BF16)
    return jnp.concatenate([hi, lo], axis=1)


def _store_row_tiles(ref, x):
    n, width = x.shape
    c = width // LANES
    for j in range(c):
        ref[pl.ds(j, n, stride=c), :] = x[:, j * LANES:(j + 1) * LANES]


def _load_row_tiles(ref, n, c):
    return jnp.concatenate([ref[pl.ds(j, n, stride=c), :] for j in range(c)], axis=1)


def _xattn_kernel(x_ref, kv_ref, gx_ref, wq_ref, wo_ref, gf_ref, wr_ref, br_ref,
                  x2_ref, hp_ref, gate_ref, eid_ref):
    scale = XA_HEAD_DIM ** -0.5
    heads = [slice(h * XA_HEAD_DIM, (h + 1) * XA_HEAD_DIM) for h in range(XA_HEADS)]
    nsub = x_ref.shape[0] // XA_SUB

    def rows(u):
        return slice(u * XA_SUB, (u + 1) * XA_SUB)

    def scores(u):
        hq = _rms_scale(x_ref[rows(u), :], gx_ref[...], NORM_EPS).astype(BF16)
        q = jnp.dot(hq, wq_ref[...], preferred_element_type=F32).astype(BF16)
        return [_dot_nt(q[:, sl], kv_ref[:, sl]) for sl in heads]

    def finish(u, sc):
        outs = []
        for h in range(XA_HEADS):
            vsl = slice(XA_WIDTH + h * XA_HEAD_DIM, XA_WIDTH + (h + 1) * XA_HEAD_DIM)
            s = sc[h] * (scale * LOG2E)
            m = jnp.max(s, axis=-1, keepdims=True)
            e = jnp.exp2(s - m)
            l = jnp.sum(e, axis=-1, keepdims=True)
            o = jnp.dot(e.astype(BF16), kv_ref[:, vsl], preferred_element_type=F32)
            outs.append((o / l).astype(BF16))
        o = jnp.concatenate(outs, axis=1)
        x2 = x_ref[rows(u), :] + jnp.dot(o, wo_ref[...], preferred_element_type=F32)
        x2_ref[rows(u), :] = x2
        h3 = _rms_scale(x2, gf_ref[...], NORM_EPS)
        _store_row_tiles(hp_ref.at[pl.ds(u * XA_SUB * HP_TILES, XA_SUB * HP_TILES)], _pack_halves(h3))
        logits = jnp.dot(h3.astype(BF16), wr_ref[...], preferred_element_type=F32) + br_ref[...]
        gates, eids = _route(logits)
        gate_ref[rows(u), :] = gates
        eid_ref[rows(u), :] = eids

    pending = scores(0)
    for u in range(nsub):
        sc = pending
        if u + 1 < nsub:
            pending = scores(u + 1)
        finish(u, sc)


def _xattn(x1, kv, gx, wq, wo, gf, wr, br):
    t = x1.shape[0]
    tm = XA_TM
    per_batch = SEQ // tm
    const = lambda shape: pl.BlockSpec(shape, lambda i: (0, 0), pipeline_mode=pl.Buffered(1))
    tile = lambda n: pl.BlockSpec((tm, n), lambda i: (i, 0))
    return pl.pallas_call(
        _xattn_kernel,
        grid=(t // tm,),
        in_specs=[tile(D_MODEL),
                  pl.BlockSpec((MEM_LEN, 2 * XA_WIDTH), lambda i: (i // per_batch, 0)),
                  const((1, D_MODEL)), const((D_MODEL, XA_WIDTH)), const((XA_WIDTH, D_MODEL)),
                  const((1, D_MODEL)), const((D_MODEL, LANES)), const((1, LANES))],
        out_specs=[tile(D_MODEL), pl.BlockSpec((tm * HP_TILES, LANES), lambda i: (i, 0)), tile(LANES), tile(LANES)],
        out_shape=[jax.ShapeDtypeStruct((t, D_MODEL), F32),
                   jax.ShapeDtypeStruct((t * HP_TILES, LANES), U32),
                   jax.ShapeDtypeStruct((t, LANES), F32),
                   jax.ShapeDtypeStruct((t, LANES), I32)],
        compiler_params=pltpu.CompilerParams(dimension_semantics=("parallel",), vmem_limit_bytes=BIG_VMEM_LIMIT),
        name="xattn_router",
    )(x1, kv, gx, wq, wo, gf, wr, br)


IDX_ROWS = 8
PIECE = 8


def _rank_kernel(eid_ref, gate_ref, idx_ref, gidx_ref, tile_ref, cnt_ref, carry_ref):
    @pl.when(pl.program_id(0) == 0)
    def _():
        carry_ref[...] = jnp.zeros_like(carry_ref)

    tm = eid_ref.shape[0]
    eid = eid_ref[...]
    lane = lax.broadcasted_iota(I32, (tm, LANES), 1)
    e1 = eid[:, 0:1]
    e2 = eid[:, 1:2]
    m1 = lane == e1
    m2 = lane == e2
    onehot = jnp.where(m1, 1.0, jnp.where(m2, 1.0, 0.0))
    r = lax.broadcasted_iota(I32, (tm, tm), 0)
    c = lax.broadcasted_iota(I32, (tm, tm), 1)
    tri = jnp.where(c < r, 1.0, 0.0).astype(BF16)
    local = jnp.dot(tri, onehot.astype(BF16), preferred_element_type=F32)
    carry = carry_ref[0:1, :]
    n_tile = jnp.sum(onehot, axis=0, keepdims=True)

    n_piece = jnp.floor((n_tile + (PIECE - 1)) * (1.0 / PIECE)) * PIECE

    def assignment(m, e):
        r_local = jnp.sum(jnp.where(m, local, 0.0), axis=-1, keepdims=True)
        r_global = r_local + jnp.sum(jnp.where(m, carry, 0.0), axis=-1, keepdims=True)
        first = jnp.sum(jnp.where(lane < e, n_tile, 0.0), axis=-1, keepdims=True)
        first_piece = jnp.sum(jnp.where(lane < e, n_piece, 0.0), axis=-1, keepdims=True)
        return first + r_local, r_global, first_piece + r_local

    p1, g1, q1 = assignment(m1, e1)
    p2, g2, q2 = assignment(m2, e2)
    cols = [p1, p2, g1, g2, e1.astype(F32), e2.astype(F32), q1, q2]
    table = jnp.zeros((tm, LANES), F32)
    for j, col in enumerate(cols):
        table = jnp.where(lane == j, col, table)
    idx_ref[0] = jnp.transpose(table)[0:IDX_ROWS, :].astype(I32)
    gidx_ref[0] = jnp.transpose(gate_ref[...])[0:IDX_ROWS, :]
    tile_ref[0] = jnp.where(lax.broadcasted_iota(I32, (8, LANES), 0) == 0, carry, n_tile)
    carry_ref[...] = carry_ref[...] + n_tile
    cnt_ref[...] = carry_ref[...]


def _rank(eid, gates):
    t = eid.shape[0]
    tm = RANK_TM
    n_tiles = t // tm
    return pl.pallas_call(
        _rank_kernel,
        grid=(n_tiles,),
        in_specs=[pl.BlockSpec((tm, LANES), lambda i: (i, 0)), pl.BlockSpec((tm, LANES), lambda i: (i, 0))],
        out_specs=[pl.BlockSpec((1, IDX_ROWS, tm), lambda i: (i, 0, 0)),
                   pl.BlockSpec((1, IDX_ROWS, tm), lambda i: (i, 0, 0)),
                   pl.BlockSpec((1, 8, LANES), lambda i: (i, 0, 0)),
                   pl.BlockSpec((8, LANES), lambda i: (0, 0))],
        out_shape=[jax.ShapeDtypeStruct((n_tiles, IDX_ROWS, tm), I32),
                   jax.ShapeDtypeStruct((n_tiles, IDX_ROWS, tm), F32),
                   jax.ShapeDtypeStruct((n_tiles, 8, LANES), F32),
                   jax.ShapeDtypeStruct((8, LANES), F32)],
        scratch_shapes=[pltpu.VMEM((8, LANES), F32)],
        compiler_params=_cparams(("arbitrary",)),
        name="expert_rank",
    )(eid, gates)


def _dispatch_kernel(n_steps, cnt_ref, pend_ref, nused_ref, pos_ref, run_ref, run_prev_ref, hp_ref, xs_ref, sbuf,
                     zero_ref, sem, zsem):
    i = pl.program_id(0)
    tm = hp_ref.shape[0] // (2 * HP_TILES)
    zb = zero_ref.shape[0]
    n_blocks = xs_ref.shape[0] // zb

    @pl.when(i == 0)
    def _():
        zero_ref[...] = jnp.zeros_like(zero_ref)
        for b in range(2):
            sbuf[b, pl.ds(TOP_K * tm * HP_TILES, PIECE * HP_TILES), :] = jnp.zeros((PIECE * HP_TILES, LANES), U32)

        def zero_copy(block):
            return pltpu.make_async_copy(zero_ref, xs_ref.at[pl.ds(pl.multiple_of(block * zb, zb), zb)], zsem)

        def tail_start(j, carry):
            zero_copy(j).start()
            return carry

        def tail_wait(j, carry):
            zero_copy(j).wait()
            return carry

        for e in range(N_EXPERTS):
            @pl.when(cnt_ref[e] > 0)
            def _():
                zero_copy(pend_ref[e] - 1).start()
        lax.fori_loop(nused_ref[0], n_blocks, tail_start, 0)
        for e in range(N_EXPERTS):
            @pl.when(cnt_ref[e] > 0)
            def _():
                zero_copy(pend_ref[e] - 1).wait()
        lax.fori_loop(nused_ref[0], n_blocks, tail_wait, 0)

    def place(half):
        def body(t, carry):
            row = hp_ref[pl.ds(pl.multiple_of((half * tm + t) * HP_TILES, HP_TILES), HP_TILES), :]
            for k in range(TOP_K):
                sbuf[half, pl.ds(pl.multiple_of(pos_ref[half, k, t] * HP_TILES, HP_TILES), HP_TILES), :] = row
            return carry
        lax.fori_loop(0, tm, body, 0, unroll=16)

    def piece_copy(src_tok, dst_slot, half):
        rows = PIECE * HP_TILES
        return pltpu.make_async_copy(sbuf.at[half, pl.ds(pl.multiple_of(src_tok * HP_TILES, HP_TILES), rows)],
                                     xs_ref.at[pl.ds(pl.multiple_of(dst_slot * HP_TILES, HP_TILES), rows)],
                                     sem.at[half])

    def issue(half):
        for e in range(N_EXPERTS):
            slot0 = run_ref[half, 0, e]
            first = run_ref[half, 0, 2 * N_EXPERTS + e]

            def run_piece(j, carry):
                piece_copy(first + PIECE * j, slot0 + PIECE * j, half).start()
                return carry

            lax.fori_loop(0, run_ref[half, 0, N_EXPERTS + e], run_piece, 0)

    def drain(runs, half):
        def wait_piece(j, carry):
            piece_copy(0, 0, half).wait()
            return carry
        lax.fori_loop(0, runs[half, 0, 3 * N_EXPERTS], wait_piece, 0)

    place(0)

    @pl.when(i > 0)
    def _():
        drain(run_prev_ref, 1)
    issue(0)
    place(1)
    drain(run_ref, 0)
    issue(1)

    @pl.when(i == n_steps - 1)
    def _():
        drain(run_ref, 1)


def _dispatch(cnt, pend_blocks, nused, pos, runs, hp, n_slots):
    n_tiles, _, tm = pos.shape
    n_steps = n_tiles // 2
    grid_spec = pltpu.PrefetchScalarGridSpec(
        num_scalar_prefetch=3,
        grid=(n_steps,),
        in_specs=[pl.BlockSpec((2, TOP_K, tm), lambda i, *_: (i, 0, 0), memory_space=pltpu.SMEM),
                  pl.BlockSpec((2, 1, LANES), lambda i, *_: (i, 0, 0), memory_space=pltpu.SMEM),
                  pl.BlockSpec((2, 1, LANES), lambda i, *_: (jnp.maximum(i - 1, 0), 0, 0), memory_space=pltpu.SMEM),
                  pl.BlockSpec((2 * tm * HP_TILES, LANES), lambda i, *_: (i, 0))],
        out_specs=pl.BlockSpec(memory_space=pl.ANY),
        scratch_shapes=[pltpu.VMEM((2, (TOP_K * tm + PIECE) * HP_TILES, LANES), U32),
                        pltpu.VMEM((MOE_TM * HP_TILES, LANES), U32),
                        pltpu.SemaphoreType.DMA((2,)), pltpu.SemaphoreType.DMA(())],
    )
    return pl.pallas_call(
        functools.partial(_dispatch_kernel, n_steps),
        grid_spec=grid_spec,
        out_shape=jax.ShapeDtypeStruct((n_slots * HP_TILES, LANES), U32),
        compiler_params=_cparams(("arbitrary",)),
        name="moe_dispatch",
    )(cnt, pend_blocks, nused, pos, runs, runs, hp)


MOE_DOWN_CHUNKS = 4


def _moe_kernel(bexp_ref, nused_ref, xs_ref, wg_ref, wu_ref, wd_ref, o_ref, wgb, wub, wdb):
    i = pl.program_id(0)
    nused = nused_ref[0]
    tm = xs_ref.shape[0] // HP_TILES

    @pl.when((i < nused) & ((i == 0) | (bexp_ref[i] != bexp_ref[jnp.maximum(i - 1, 0)])))
    def _():
        wgb[...] = wg_ref[0].astype(BF16)
        wub[...] = wu_ref[0].astype(BF16)
        wdb[...] = wd_ref[0].astype(BF16)

    @pl.when(i < nused)
    def _():
        x = _unpack_halves(_load_row_tiles(xs_ref, tm, HP_TILES))
        hg = jnp.dot(x, wgb[...], preferred_element_type=F32)
        hu = jnp.dot(x, wub[...], preferred_element_type=F32)
        hdn = ((hg * jax.nn.sigmoid(hg)) * hu).astype(BF16)
        cw = D_MODEL // MOE_DOWN_CHUNKS
        for c in range(MOE_DOWN_CHUNKS):
            y = jnp.dot(hdn, wdb[:, c * cw:(c + 1) * cw], preferred_element_type=F32)
            for j in range(cw // LANES):
                o_ref[pl.ds(c * (cw // LANES) + j, tm, stride=YS_TILES), :] = y[:, j * LANES:(j + 1) * LANES]

    @pl.when(i >= nused)
    def _():
        o_ref[...] = jnp.zeros_like(o_ref)


def _moe(bexp, nused, xs, wg, wu, wd):
    tm = MOE_TM
    n_slots = xs.shape[0] // HP_TILES
    n_blocks = n_slots // tm
    grid_spec = pltpu.PrefetchScalarGridSpec(
        num_scalar_prefetch=2,
        grid=(n_blocks,),
        in_specs=[pl.BlockSpec((tm * HP_TILES, LANES), lambda i, be, nu: (jnp.minimum(i, nu[0] - 1), 0)),
                  pl.BlockSpec((1, D_MODEL, D_EXPERT), lambda i, be, nu: (be[i], 0, 0)),
                  pl.BlockSpec((1, D_MODEL, D_EXPERT), lambda i, be, nu: (be[i], 0, 0)),
                  pl.BlockSpec((1, D_EXPERT, D_MODEL), lambda i, be, nu: (be[i], 0, 0))],
        out_specs=pl.BlockSpec((tm * YS_TILES, LANES), lambda i, be, nu: (i, 0)),
        scratch_shapes=[pltpu.VMEM((D_MODEL, D_EXPERT), BF16), pltpu.VMEM((D_MODEL, D_EXPERT), BF16),
                        pltpu.VMEM((D_EXPERT, D_MODEL), BF16)],
    )
    return pl.pallas_call(
        _moe_kernel,
        grid_spec=grid_spec,
        out_shape=jax.ShapeDtypeStruct((n_slots * YS_TILES, LANES), F32),
        compiler_params=_cparams(("arbitrary",)),
        name="moe_experts",
    )(bexp, nused, xs, wg, wu, wd)


def _combine_kernel(n_tiles, pos_ref, run_ref, run_next_ref, gate_ref, x_ref, g_ref, ys_ref, o_ref, ybuf, mbuf, sem):
    i = pl.program_id(0)
    tm = x_ref.shape[0]
    slot = lax.rem(i, 2)

    def piece_copy(src_slot, dst_pos, buf):
        rows = PIECE * YS_TILES
        return pltpu.make_async_copy(ys_ref.at[pl.ds(pl.multiple_of(src_slot * YS_TILES, YS_TILES), rows)],
                                     ybuf.at[buf, pl.ds(pl.multiple_of(dst_pos * YS_TILES, YS_TILES), rows)],
                                     sem.at[buf])

    def request(runs, buf):
        for e in range(N_EXPERTS):
            slot0 = runs[0, 0, e]
            first = runs[0, 0, 2 * N_EXPERTS + e]

            def run_piece(j, carry):
                piece_copy(slot0 + PIECE * j, first + PIECE * j, buf).start()
                return carry

            lax.fori_loop(0, runs[0, 0, N_EXPERTS + e], run_piece, 0)

    @pl.when(i == 0)
    def _():
        request(run_ref, 0)

    @pl.when(i + 1 < n_tiles)
    def _():
        request(run_next_ref, 1 - slot)

    def drain(j, carry):
        piece_copy(0, 0, slot).wait()
        return carry

    lax.fori_loop(0, run_ref[0, 0, 3 * N_EXPERTS], drain, 0)

    def token(t, carry):
        acc = None
        for k in range(TOP_K):
            rows = pl.ds(pl.multiple_of(pos_ref[0, k, t] * YS_TILES, YS_TILES), YS_TILES)
            term = ybuf[slot, rows, :] * gate_ref[0, k, t]
            acc = term if acc is None else acc + term
        mbuf[pl.ds(pl.multiple_of(t * YS_TILES, YS_TILES), YS_TILES), :] = acc
        return carry

    lax.fori_loop(0, tm, token, 0, unroll=16)
    moe = _load_row_tiles(mbuf, tm, YS_TILES)
    o_ref[...] = _rms_scale(x_ref[...] + moe, g_ref[...], NORM_EPS)


def _combine(pos, runs, gidx, x2, g, ys, row0, n_rows):
    _, _, tm = pos.shape
    tile0 = row0 // tm
    n = n_rows // tm
    smem = lambda rows: pl.BlockSpec((1, rows, tm), lambda i: (tile0 + i, 0, 0), memory_space=pltpu.SMEM)
    run_spec = lambda ahead: pl.BlockSpec((1, 1, LANES), lambda i: (tile0 + jnp.minimum(i + ahead, n - 1), 0, 0),
                                          memory_space=pltpu.SMEM)
    return pl.pallas_call(
        functools.partial(_combine_kernel, n),
        grid=(n,),
        in_specs=[smem(TOP_K), run_spec(0), run_spec(1), smem(TOP_K),
                  pl.BlockSpec((tm, D_MODEL), lambda i: (tile0 + i, 0)),
                  pl.BlockSpec((1, D_MODEL), lambda i: (0, 0)),
                  pl.BlockSpec(memory_space=pl.ANY)],
        out_specs=pl.BlockSpec((tm, D_MODEL), lambda i: (i, 0)),
        out_shape=jax.ShapeDtypeStruct((n_rows, D_MODEL), F32),
        scratch_shapes=[pltpu.VMEM((2, (TOP_K * tm + N_EXPERTS * PIECE) * YS_TILES, LANES), F32),
                        pltpu.VMEM((tm * YS_TILES, LANES), F32),
                        pltpu.SemaphoreType.DMA((2,))],
        compiler_params=_cparams(("arbitrary",)),
        name="moe_combine",
    )(pos, runs, runs, gidx[:, 0:TOP_K, :], x2, g, ys)


def _routing_plan(idx, tiles, counts):
    cnt = counts[0, :N_EXPERTS].astype(I32)
    padded = jnp.where(cnt > 0, (cnt + PIECE - 1 + MOE_TM - 1) // MOE_TM * MOE_TM, 0)
    pends = jnp.cumsum(padded)
    pstarts = pends - padded
    n_tiles, _, tm = idx.shape
    n_blocks = n_tiles * tm * TOP_K // MOE_TM + N_EXPERTS + 1
    blk0 = jnp.arange(n_blocks, dtype=I32) * MOE_TM
    bexp = jnp.minimum(jnp.sum(pends[None, :] <= blk0[:, None], axis=1), N_EXPERTS - 1).astype(I32)
    nused = (pends[-1:] // MOE_TM).astype(I32)
    before = tiles[:, 0, :N_EXPERTS].astype(I32)
    inside = tiles[:, 1, :N_EXPERTS].astype(I32)
    first = jnp.cumsum(inside, axis=1) - inside
    pieces = (inside + PIECE - 1) // PIECE
    first_piece = (jnp.cumsum(pieces, axis=1) - pieces) * PIECE

    def run_table(first_pos):
        return jnp.concatenate([pstarts[None, :] + before, pieces, first_pos, jnp.sum(pieces, axis=1, keepdims=True),
                                jnp.zeros((n_tiles, LANES - 3 * N_EXPERTS - 1), I32)], axis=1).reshape(n_tiles, 1, LANES)

    pos = idx[:, 0:TOP_K, :]
    pos_piece = idx[:, 3 * TOP_K:4 * TOP_K, :]
    return pos, run_table(first), pos_piece, run_table(first_piece), cnt, (pends // MOE_TM).astype(I32), bexp, nused, n_blocks * MOE_TM


def kernel(x_prompt, x_sample, mem_prompt, mem_sample, mix_norm_g, w_in, na_rpb, diff_lambda_q1, diff_lambda_k1, diff_lambda_q2, diff_lambda_k2, diff_subln_g, w_branch_na, w_branch_diff, w_out, xa_norm_g, mem_norm_g, xa_w_q, xa_w_kv, xa_w_o, ffn_norm_g, router_group_w, router_group_b, router_expert_w, router_expert_b, w_gate, w_up, w_down, final_norm_g):
    nb_p, nb_s = x_prompt.shape[0], x_sample.shape[0]
    nb = nb_p + nb_s
    t = nb * SEQ
    xp = x_prompt.reshape(nb_p * SEQ, D_MODEL)
    xs_in = x_sample.reshape(nb_s * SEQ, D_MODEL)
    mem_p = mem_prompt.reshape(nb_p * MEM_LEN, D_MODEL)
    mem_s = mem_sample.reshape(nb_s * MEM_LEN, D_MODEL)
    row = lambda v: v.reshape(1, -1).astype(F32)
    bf = lambda w: w.astype(BF16)

    qcol = 3 * NA_WIDTH
    colscale = jnp.ones((1, IN_COLS), F32).at[:, qcol:qcol + DIFF_QK_WIDTH].set(DIFF_QK_DIM ** -0.5 * LOG2E)
    proj = _inproj(xp, xs_in, row(mix_norm_g[0]), bf(w_in[0]), colscale)
    proj3 = proj.reshape(nb, SEQ, IN_COLS)
    a = _na_attention(proj3, _na_bias_blocks(na_rpb[0]))
    b = _diff_attention(proj3, row(diff_lambda_q1[0]), row(diff_lambda_k1[0]),
                        row(diff_lambda_q2[0]), row(diff_lambda_k2[0]), row(diff_subln_g[0]))
    x1 = _merge(a.reshape(t, NA_WIDTH), b.reshape(t, DIFF_V_WIDTH), proj, xp, xs_in,
                bf(w_branch_na[0]), bf(w_branch_diff[0]), bf(w_out[0]))
    kv = _memkv(mem_p, mem_s, row(mem_norm_g[0]), bf(xa_w_kv[0]))
    w_router = jnp.zeros((D_MODEL, LANES), F32)
    w_router = w_router.at[:, :N_GROUPS].set(router_group_w[0]).at[:, N_GROUPS:N_GROUPS + N_EXPERTS].set(router_expert_w[0])
    b_router = jnp.zeros((1, LANES), F32)
    b_router = b_router.at[0, :N_GROUPS].set(router_group_b[0]).at[0, N_GROUPS:N_GROUPS + N_EXPERTS].set(router_expert_b[0])
    x2, hp, gates, eid = _xattn(x1, kv, row(xa_norm_g[0]), bf(xa_w_q[0]), bf(xa_w_o[0]),
                                row(ffn_norm_g[0]), bf(w_router), b_router)
    idx, gidx, tiles, counts = _rank(eid, gates)
    pos, runs, pos_c, runs_c, cnt, pend_blocks, bexp, nused, n_slots = _routing_plan(idx, tiles, counts)
    xs = _dispatch(cnt, pend_blocks, nused, pos, runs, hp, n_slots)
    ys = _moe(bexp, nused, xs, w_gate[0], w_up[0], w_down[0])
    fg = row(final_norm_g)
    y_p = _combine(pos_c, runs_c, gidx, x2, fg, ys, 0, nb_p * SEQ).reshape(nb_p, SEQ, D_MODEL)
    y_s = _combine(pos_c, runs_c, gidx, x2, fg, ys, nb_p * SEQ, nb_s * SEQ).reshape(nb_s, SEQ, D_MODEL)
    return (y_p, y_s)
```

```python
import functools
import math

import jax
import jax.numpy as jnp
from jax import lax
from jax.experimental import pallas as pl
from jax.experimental.pallas import tpu as pltpu

F32 = jnp.float32
BF16 = jnp.bfloat16
I32 = jnp.int32
U32 = jnp.uint32

D_MODEL = 2048
SEQ = 2048
GRID_W = 64
ROWS = SEQ // GRID_W
NA_HEADS = 8
NA_HEAD_DIM = 128
NA_WIDTH = NA_HEADS * NA_HEAD_DIM
NA_KH = 8
NA_KW = 16
DIFF_HEADS = 8
DIFF_QK_DIM = 64
DIFF_V_DIM = 128
DIFF_QK_WIDTH = DIFF_HEADS * 2 * DIFF_QK_DIM
DIFF_V_WIDTH = DIFF_HEADS * DIFF_V_DIM
IN_COLS = 3 * NA_WIDTH + 2 * DIFF_QK_WIDTH + DIFF_V_WIDTH + 2 * D_MODEL
MEM_LEN = 256
XA_HEADS = 4
XA_HEAD_DIM = 128
XA_WIDTH = XA_HEADS * XA_HEAD_DIM
N_GROUPS = 4
EXPERTS_PER_GROUP = 8
N_EXPERTS = N_GROUPS * EXPERTS_PER_GROUP
TOP_K = 2
D_EXPERT = 512
NORM_EPS = 1e-6
SUBLN_EPS = 1e-5
NEG_INF = -1e30
LAMBDA_INIT = 0.8 - 0.6 * math.exp(-0.3 * 0)
LOG2E = math.log2(math.e)

LANES = 128
HP_TILES = D_MODEL // 2 // LANES
YS_TILES = D_MODEL // LANES
VMEM_LIMIT = 56 * 1024 * 1024
BIG_VMEM_LIMIT = 60 * 1024 * 1024

INPROJ_TM = 1024
INPROJ_TN = 1280
NORM_CHUNK = 128
DIFF_TQ = 2048
DIFF_SUB = 128
MERGE_TM = 512
XA_TM = 1024
XA_SUB = 512
RANK_TM = 512
MOE_TM = 512


def _cparams(sem):
    return pltpu.CompilerParams(dimension_semantics=sem, vmem_limit_bytes=VMEM_LIMIT)


def _rms_scale(x, g, eps):
    ms = jnp.mean(x * x, axis=-1, keepdims=True)
    return x * lax.rsqrt(ms + eps) * g


def _dot_nt(a, b):
    return lax.dot_general(a, b, (((1,), (1,)), ((), ())), preferred_element_type=F32)


def _two_group_specs(tm, width, n_first):
    first = pl.BlockSpec((tm, width), lambda i, *_: (jnp.minimum(i, n_first - 1), 0))
    second = pl.BlockSpec((tm, width), lambda i, *_: (jnp.maximum(i - n_first, 0), 0))
    return first, second


def _inproj_kernel(n_first, xp_ref, xs_ref, g_ref, w_ref, cs_ref, o_ref, h_ref):
    def normalise(x_ref):
        def chunk(c, carry):
            rows = pl.ds(pl.multiple_of(c * NORM_CHUNK, NORM_CHUNK), NORM_CHUNK)
            h_ref[rows, :] = _rms_scale(x_ref[rows, :], g_ref[...], NORM_EPS).astype(BF16)
            return carry
        lax.fori_loop(0, x_ref.shape[0] // NORM_CHUNK, chunk, 0)

    first_col = pl.program_id(1) == 0
    in_first = pl.program_id(0) < n_first
    pl.when(first_col & in_first)(lambda: normalise(xp_ref))
    pl.when(first_col & jnp.logical_not(in_first))(lambda: normalise(xs_ref))
    acc = jnp.dot(h_ref[...], w_ref[...], preferred_element_type=F32)
    o_ref[...] = (acc * cs_ref[...]).astype(o_ref.dtype)


def _inproj(xp, xs, g, w, colscale):
    d = xp.shape[1]
    t = xp.shape[0] + xs.shape[0]
    n = w.shape[1]
    tm = INPROJ_TM
    n_first = xp.shape[0] // tm
    spec_p, spec_s = _two_group_specs(tm, d, n_first)
    return pl.pallas_call(
        functools.partial(_inproj_kernel, n_first),
        grid=(t // tm, n // INPROJ_TN),
        in_specs=[spec_p, spec_s,
                  pl.BlockSpec((1, d), lambda i, j: (0, 0)),
                  pl.BlockSpec((d, INPROJ_TN), lambda i, j: (0, j)),
                  pl.BlockSpec((1, INPROJ_TN), lambda i, j: (0, j))],
        out_specs=pl.BlockSpec((tm, INPROJ_TN), lambda i, j: (i, j)),
        out_shape=jax.ShapeDtypeStruct((t, n), BF16),
        scratch_shapes=[pltpu.VMEM((tm, d), BF16)],
        compiler_params=pltpu.CompilerParams(dimension_semantics=("parallel", "arbitrary"),
                                             vmem_limit_bytes=BIG_VMEM_LIMIT),
        name="inproj",
    )(xp, xs, g, w, colscale)


NA_QROWS = 4
NA_KROWS = 12
NA_TILES = ROWS // NA_QROWS
NA_HEADS_PER_STEP = 4


def _na_tile_key_row(ti):
    return min(max(NA_QROWS * ti - NA_KH // 2, 0), ROWS - NA_KROWS)


def _na_bias_blocks(rpb):
    cols = jnp.arange(GRID_W)
    cstart = jnp.clip(cols - NA_KW // 2, 0, GRID_W - NA_KW)
    col_ok = (cols[None, :] >= cstart[:, None]) & (cols[None, :] < cstart[:, None] + NA_KW)
    dc = jnp.clip(cols[None, :] - cols[:, None], 1 - NA_KW, NA_KW - 1) + NA_KW - 1
    pick = (dc[None] == jnp.arange(2 * NA_KW - 1)[:, None, None]).astype(F32)
    t = jnp.einsum("hdx,xck->hdck", rpb.astype(F32), pick, precision=lax.Precision.HIGHEST)
    return jnp.where(col_ok[None, None], t, NEG_INF)


def _na_kernel(q_ref, k_ref, v_ref, t_ref, o_ref, bias_ref):
    scale = NA_HEAD_DIM ** -0.5
    tq = NA_QROWS * GRID_W
    tk = NA_KROWS * GRID_W
    hd = NA_HEAD_DIM

    @pl.when(pl.program_id(1) == 0)
    def _():
        masked = jnp.full((GRID_W, GRID_W), NEG_INF, F32)
        for hh in range(NA_HEADS_PER_STEP):
            for variant, ti in enumerate((0, 1, NA_TILES - 1)):
                for jr in range(NA_QROWS):
                    r = NA_QROWS * ti + jr
                    rs = min(max(r - NA_KH // 2, 0), ROWS - NA_KH)
                    blocks = []
                    for i in range(NA_KROWS):
                        kr = _na_tile_key_row(ti) + i
                        blocks.append(t_ref[hh, kr - r + NA_KH - 1] if rs <= kr < rs + NA_KH else masked)
                    bias_ref[hh, variant, jr * GRID_W:(jr + 1) * GRID_W, :] = jnp.concatenate(blocks, axis=1) * LOG2E

    def key_rows(ti):
        k0 = _na_tile_key_row(ti) * GRID_W
        return slice(k0, k0 + tk)

    work = [(hh, ti) for hh in range(NA_HEADS_PER_STEP) for ti in range(NA_TILES)]

    def scores(item):
        hh, ti = item
        cols = slice(hh * hd, (hh + 1) * hd)
        return _dot_nt(q_ref[0, ti * tq:(ti + 1) * tq, cols], k_ref[0, key_rows(ti), cols])

    pending = scores(work[0])
    for n, (hh, ti) in enumerate(work):
        variant = 0 if ti == 0 else (2 if ti == NA_TILES - 1 else 1)
        cols = slice(hh * hd, (hh + 1) * hd)
        qrows = slice(ti * tq, (ti + 1) * tq)
        s = pending * (scale * LOG2E) + bias_ref[hh, variant]
        if n + 1 < len(work):
            pending = scores(work[n + 1])
        m = jnp.max(s, axis=-1, keepdims=True)
        e = jnp.exp2(s - m)
        l = jnp.sum(e, axis=-1, keepdims=True)
        o = jnp.dot(e.astype(BF16), v_ref[0, key_rows(ti), cols], preferred_element_type=F32)
        o_ref[0, qrows, cols] = (o / l).astype(o_ref.dtype)


def _na_attention(proj3, blocks):
    b = proj3.shape[0]
    hp = NA_HEADS_PER_STEP
    width = hp * NA_HEAD_DIM
    groups = NA_HEADS // hp
    return pl.pallas_call(
        _na_kernel,
        grid=(groups, b),
        in_specs=[pl.BlockSpec((1, SEQ, width), lambda h, i: (i, 0, h)),
                  pl.BlockSpec((1, SEQ, width), lambda h, i: (i, 0, groups + h)),
                  pl.BlockSpec((1, SEQ, width), lambda h, i: (i, 0, 2 * groups + h)),
                  pl.BlockSpec((hp,) + blocks.shape[1:], lambda h, i: (h, 0, 0, 0))],
        out_specs=pl.BlockSpec((1, SEQ, width), lambda h, i: (i, 0, h)),
        out_shape=jax.ShapeDtypeStruct((b, SEQ, NA_WIDTH), BF16),
        scratch_shapes=[pltpu.VMEM((hp, 3, NA_QROWS * GRID_W, NA_KROWS * GRID_W), F32)],
        compiler_params=_cparams(("arbitrary", "arbitrary")),
        name="na_attn",
    )(proj3, proj3, proj3, blocks)


def _alibi_table(tq):
    slopes = jnp.exp2(-8.0 * jnp.arange(1, DIFF_HEADS + 1, dtype=F32) / DIFF_HEADS)
    r = jnp.arange(tq)[:, None]
    x = jnp.arange(2 * SEQ - tq)[None, :]
    dist = jnp.abs(r - x + (SEQ - tq)).astype(F32)
    return (slopes * LOG2E)[:, None, None] * dist[None]


def _diff_kernel(lq1_ref, lk1_ref, lq2_ref, lk2_ref, q_ref, k_ref, v_ref, g_ref, alibi_ref, o_ref):
    qi = pl.program_id(2)
    ts = alibi_ref.shape[1]
    nsub = q_ref.shape[1] // ts
    lam = (jnp.exp(jnp.sum(lq1_ref[...] * lk1_ref[...], axis=-1, keepdims=True))
           - jnp.exp(jnp.sum(lq2_ref[...] * lk2_ref[...], axis=-1, keepdims=True)) + LAMBDA_INIT)
    k = k_ref[0]

    def scores(u):
        q = q_ref[0, u * ts:(u + 1) * ts, :]
        return [_dot_nt(q[:, mi * DIFF_QK_DIM:(mi + 1) * DIFF_QK_DIM], k[:, mi * DIFF_QK_DIM:(mi + 1) * DIFF_QK_DIM])
                for mi in range(2)]

    def expo(s, bias):
        s = s - bias
        m = jnp.max(s, axis=-1, keepdims=True)
        e = jnp.exp2(s - m)
        return e, jnp.sum(e, axis=-1, keepdims=True)

    pending = scores(0)
    for u in range(nsub):
        s0, s1 = pending
        if u + 1 < nsub:
            pending = scores(u + 1)
        off = (SEQ - ts) - (qi * nsub + u) * ts
        bias = alibi_ref[0, :, pl.ds(pl.multiple_of(off, ts), SEQ)]
        e0, l0 = expo(s0, bias)
        e1, l1 = expo(s1, bias)
        a = (e0 - e1 * (lam * l0 / l1)).astype(BF16)
        o = jnp.dot(a, v_ref[0], preferred_element_type=F32) / l0
        o = _rms_scale(o, g_ref[...], SUBLN_EPS) * (1.0 - LAMBDA_INIT)
        o_ref[0, u * ts:(u + 1) * ts, :] = o.astype(o_ref.dtype)


def _diff_attention(proj3, lq1, lk1, lq2, lk2, subln_g):
    b = proj3.shape[0]
    tq = DIFF_TQ
    qoff = 3 * NA_WIDTH // LANES
    koff = qoff + DIFF_QK_WIDTH // LANES
    voff = koff + DIFF_QK_WIDTH // LANES
    vec = lambda n: pl.BlockSpec((1, n), lambda h, i, j: (0, 0))
    return pl.pallas_call(
        _diff_kernel,
        grid=(DIFF_HEADS, b, SEQ // tq),
        in_specs=[vec(DIFF_QK_DIM), vec(DIFF_QK_DIM), vec(DIFF_QK_DIM), vec(DIFF_QK_DIM),
                  pl.BlockSpec((1, tq, LANES), lambda h, i, j: (i, j, qoff + h)),
                  pl.BlockSpec((1, SEQ, LANES), lambda h, i, j: (i, 0, koff + h)),
                  pl.BlockSpec((1, SEQ, LANES), lambda h, i, j: (i, 0, voff + h)),
                  vec(DIFF_V_DIM),
                  pl.BlockSpec((1, DIFF_SUB, 2 * SEQ - DIFF_SUB), lambda h, i, j: (h, 0, 0))],
        out_specs=pl.BlockSpec((1, tq, DIFF_V_DIM), lambda h, i, j: (i, j, h)),
        out_shape=jax.ShapeDtypeStruct((b, SEQ, DIFF_V_WIDTH), BF16),
        compiler_params=_cparams(("arbitrary", "arbitrary", "arbitrary")),
        name="diff_attn",
    )(lq1, lk1, lq2, lk2, proj3, proj3, proj3, subln_g, _alibi_table(DIFF_SUB))


def _merge_kernel(n_first, a_ref, b_ref, gna_ref, gdf_ref, xp_ref, xs_ref, wna_ref, wdf_ref, wout_ref, o_ref):
    pa = jnp.dot(a_ref[...], wna_ref[...], preferred_element_type=F32)
    pb = jnp.dot(b_ref[...], wdf_ref[...], preferred_element_type=F32)
    merged = (jax.nn.sigmoid(gna_ref[...].astype(F32)) * pa
              + jax.nn.sigmoid(gdf_ref[...].astype(F32)) * pb)
    delta = jnp.dot(merged.astype(BF16), wout_ref[...], preferred_element_type=F32)
    in_first = pl.program_id(0) < n_first

    @pl.when(in_first)
    def _():
        o_ref[...] = xp_ref[...] + delta

    @pl.when(jnp.logical_not(in_first))
    def _():
        o_ref[...] = xs_ref[...] + delta


def _merge(a, b, proj, xp, xs, wna, wdf, wout):
    t = xp.shape[0] + xs.shape[0]
    tm = MERGE_TM
    gna_blk = (3 * NA_WIDTH + 2 * DIFF_QK_WIDTH + DIFF_V_WIDTH) // D_MODEL
    const = lambda shape: pl.BlockSpec(shape, lambda i: (0, 0), pipeline_mode=pl.Buffered(1))
    n_first = xp.shape[0] // tm
    spec_p, spec_s = _two_group_specs(tm, D_MODEL, n_first)
    return pl.pallas_call(
        functools.partial(_merge_kernel, n_first),
        grid=(t // tm,),
        in_specs=[pl.BlockSpec((tm, NA_WIDTH), lambda i: (i, 0)),
                  pl.BlockSpec((tm, DIFF_V_WIDTH), lambda i: (i, 0)),
                  pl.BlockSpec((tm, D_MODEL), lambda i: (i, gna_blk)),
                  pl.BlockSpec((tm, D_MODEL), lambda i: (i, gna_blk + 1)),
                  spec_p, spec_s,
                  const((NA_WIDTH, D_MODEL)), const((DIFF_V_WIDTH, D_MODEL)), const((D_MODEL, D_MODEL))],
        out_specs=pl.BlockSpec((tm, D_MODEL), lambda i: (i, 0)),
        out_shape=jax.ShapeDtypeStruct((t, D_MODEL), F32),
        compiler_params=pltpu.CompilerParams(dimension_semantics=("parallel",), vmem_limit_bytes=BIG_VMEM_LIMIT),
        name="merge_outproj",
    )(a, b, proj, proj, xp, xs, wna, wdf, wout)


def _memkv_kernel(n_first, mp_ref, ms_ref, g_ref, w_ref, o_ref):
    def project(m_ref):
        h = _rms_scale(m_ref[...], g_ref[...], NORM_EPS).astype(BF16)
        o_ref[...] = jnp.dot(h, w_ref[...], preferred_element_type=F32).astype(o_ref.dtype)

    in_first = pl.program_id(0) < n_first
    pl.when(in_first)(lambda: project(mp_ref))
    pl.when(jnp.logical_not(in_first))(lambda: project(ms_ref))


def _memkv(mem_p, mem_s, g, w):
    t = mem_p.shape[0] + mem_s.shape[0]
    n_first = mem_p.shape[0] // MEM_LEN
    spec_p, spec_s = _two_group_specs(MEM_LEN, D_MODEL, n_first)
    return pl.pallas_call(
        functools.partial(_memkv_kernel, n_first),
        grid=(t // MEM_LEN,),
        in_specs=[spec_p, spec_s,
                  pl.BlockSpec((1, D_MODEL), lambda i: (0, 0)),
                  pl.BlockSpec((D_MODEL, 2 * XA_WIDTH), lambda i: (0, 0))],
        out_specs=pl.BlockSpec((MEM_LEN, 2 * XA_WIDTH), lambda i: (i, 0)),
        out_shape=jax.ShapeDtypeStruct((t, 2 * XA_WIDTH), BF16),
        compiler_params=_cparams(("parallel",)),
        name="mem_kv",
    )(mem_p, mem_s, g, w)


def _route(logits):
    lane = lax.broadcasted_iota(I32, logits.shape, 1).astype(F32)
    ninf = -jnp.inf
    big = float(LANES)
    gl = jnp.where(lane < N_GROUPS, logits, ninf)
    gmax = jnp.max(gl, axis=-1, keepdims=True)
    g = jnp.min(jnp.where(gl == gmax, lane, big), axis=-1, keepdims=True)
    pg = 1.0 / jnp.sum(jnp.exp(gl - gmax), axis=-1, keepdims=True)
    lo = N_GROUPS + EXPERTS_PER_GROUP * g
    el = jnp.where((lane >= lo) & (lane < lo + EXPERTS_PER_GROUP), logits, ninf)
    v1 = jnp.max(el, axis=-1, keepdims=True)
    i1 = jnp.min(jnp.where(el == v1, lane, big), axis=-1, keepdims=True)
    el2 = jnp.where(lane == i1, ninf, el)
    v2 = jnp.max(el2, axis=-1, keepdims=True)
    i2 = jnp.min(jnp.where(el2 == v2, lane, big), axis=-1, keepdims=True)
    t = jnp.exp(v2 - v1)
    den = 1.0 + t
    gate1 = pg * (1.0 / den)
    gate2 = pg * (t / den)
    gates = jnp.where(lane == 0.0, gate1, jnp.where(lane == 1.0, gate2, 0.0))
    eids = jnp.where(lane == 0.0, i1 - N_GROUPS, jnp.where(lane == 1.0, i2 - N_GROUPS, 0.0)).astype(I32)
    return gates, eids


def _pack_halves(h):
    half = h.shape[1] // 2
    hi = pltpu.bitcast(h[:, :half].astype(BF16).astype(F32), U32)
    lo = pltpu.bitcast(h[:, half:].astype(BF16).astype(F32), U32)
    return hi | (lo >> 16)


def _unpack_halves(w):
    hi = pltpu.bitcast(w & jnp.uint32(0xFFFF0000), F32).astype(BF16)
    lo = pltpu.bitcast(w << 16, F32).astype(BF16)
    return jnp.concatenate([hi, lo], axis=1)


def _store_row_tiles(ref, x):
    n, width = x.shape
    c = width // LANES
    for j in range(c):
        ref[pl.ds(j, n, stride=c), :] = x[:, j * LANES:(j + 1) * LANES]


def _load_row_tiles(ref, n, c):
    return jnp.concatenate([ref[pl.ds(j, n, stride=c), :] for j in range(c)], axis=1)


def _xattn_kernel(x_ref, kv_ref, gx_ref, wq_ref, wo_ref, gf_ref, wr_ref, br_ref,
                  x2_ref, hp_ref, gate_ref, eid_ref):
    scale = XA_HEAD_DIM ** -0.5
    heads = [slice(h * XA_HEAD_DIM, (h + 1) * XA_HEAD_DIM) for h in range(XA_HEADS)]
    nsub = x_ref.shape[0] // XA_SUB

    def rows(u):
        return slice(u * XA_SUB, (u + 1) * XA_SUB)

    def scores(u):
        hq = _rms_scale(x_ref[rows(u), :], gx_ref[...], NORM_EPS).astype(BF16)
        q = jnp.dot(hq, wq_ref[...], preferred_element_type=F32).astype(BF16)
        return [_dot_nt(q[:, sl], kv_ref[:, sl]) for sl in heads]

    def finish(u, sc):
        outs = []
        for h in range(XA_HEADS):
            vsl = slice(XA_WIDTH + h * XA_HEAD_DIM, XA_WIDTH + (h + 1) * XA_HEAD_DIM)
            s = sc[h] * (scale * LOG2E)
            m = jnp.max(s, axis=-1, keepdims=True)
            e = jnp.exp2(s - m)
            l = jnp.sum(e, axis=-1, keepdims=True)
            o = jnp.dot(e.astype(BF16), kv_ref[:, vsl], preferred_element_type=F32)
            outs.append((o / l).astype(BF16))
        o = jnp.concatenate(outs, axis=1)
        x2 = x_ref[rows(u), :] + jnp.dot(o, wo_ref[...], preferred_element_type=F32)
        x2_ref[rows(u), :] = x2
        h3 = _rms_scale(x2, gf_ref[...], NORM_EPS)
        _store_row_tiles(hp_ref.at[pl.ds(u * XA_SUB * HP_TILES, XA_SUB * HP_TILES)], _pack_halves(h3))
        logits = jnp.dot(h3.astype(BF16), wr_ref[...], preferred_element_type=F32) + br_ref[...]
        gates, eids = _route(logits)
        gate_ref[rows(u), :] = gates
        eid_ref[rows(u), :] = eids

    pending = scores(0)
    for u in range(nsub):
        sc = pending
        if u + 1 < nsub:
            pending = scores(u + 1)
        finish(u, sc)


def _xattn(x1, kv, gx, wq, wo, gf, wr, br):
    t = x1.shape[0]
    tm = XA_TM
    per_batch = SEQ // tm
    const = lambda shape: pl.BlockSpec(shape, lambda i: (0, 0), pipeline_mode=pl.Buffered(1))
    tile = lambda n: pl.BlockSpec((tm, n), lambda i: (i, 0))
    return pl.pallas_call(
        _xattn_kernel,
        grid=(t // tm,),
        in_specs=[tile(D_MODEL),
                  pl.BlockSpec((MEM_LEN, 2 * XA_WIDTH), lambda i: (i // per_batch, 0)),
                  const((1, D_MODEL)), const((D_MODEL, XA_WIDTH)), const((XA_WIDTH, D_MODEL)),
                  const((1, D_MODEL)), const((D_MODEL, LANES)), const((1, LANES))],
        out_specs=[tile(D_MODEL), pl.BlockSpec((tm * HP_TILES, LANES), lambda i: (i, 0)), tile(LANES), tile(LANES)],
        out_shape=[jax.ShapeDtypeStruct((t, D_MODEL), F32),
                   jax.ShapeDtypeStruct((t * HP_TILES, LANES), U32),
                   jax.ShapeDtypeStruct((t, LANES), F32),
                   jax.ShapeDtypeStruct((t, LANES), I32)],
        compiler_params=pltpu.CompilerParams(dimension_semantics=("parallel",), vmem_limit_bytes=BIG_VMEM_LIMIT),
        name="xattn_router",
    )(x1, kv, gx, wq, wo, gf, wr, br)


IDX_ROWS = 8
PIECE = 8


def _rank_kernel(eid_ref, gate_ref, idx_ref, gidx_ref, tile_ref, cnt_ref, carry_ref):
    @pl.when(pl.program_id(0) == 0)
    def _():
        carry_ref[...] = jnp.zeros_like(carry_ref)

    tm = eid_ref.shape[0]
    eid = eid_ref[...]
    lane = lax.broadcasted_iota(I32, (tm, LANES), 1)
    e1 = eid[:, 0:1]
    e2 = eid[:, 1:2]
    m1 = lane == e1
    m2 = lane == e2
    onehot = jnp.where(m1, 1.0, jnp.where(m2, 1.0, 0.0))
    r = lax.broadcasted_iota(I32, (tm, tm), 0)
    c = lax.broadcasted_iota(I32, (tm, tm), 1)
    tri = jnp.where(c < r, 1.0, 0.0).astype(BF16)
    local = jnp.dot(tri, onehot.astype(BF16), preferred_element_type=F32)
    carry = carry_ref[0:1, :]
    n_tile = jnp.sum(onehot, axis=0, keepdims=True)

    n_piece = jnp.floor((n_tile + (PIECE - 1)) * (1.0 / PIECE)) * PIECE

    def assignment(m, e):
        r_local = jnp.sum(jnp.where(m, local, 0.0), axis=-1, keepdims=True)
        r_global = r_local + jnp.sum(jnp.where(m, carry, 0.0), axis=-1, keepdims=True)
        first = jnp.sum(jnp.where(lane < e, n_tile, 0.0), axis=-1, keepdims=True)
        first_piece = jnp.sum(jnp.where(lane < e, n_piece, 0.0), axis=-1, keepdims=True)
        return first + r_local, r_global, first_piece + r_local

    p1, g1, q1 = assignment(m1, e1)
    p2, g2, q2 = assignment(m2, e2)
    cols = [p1, p2, g1, g2, e1.astype(F32), e2.astype(F32), q1, q2]
    table = jnp.zeros((tm, LANES), F32)
    for j, col in enumerate(cols):
        table = jnp.where(lane == j, col, table)
    idx_ref[0] = jnp.transpose(table)[0:IDX_ROWS, :].astype(I32)
    gidx_ref[0] = jnp.transpose(gate_ref[...])[0:IDX_ROWS, :]
    tile_ref[0] = jnp.where(lax.broadcasted_iota(I32, (8, LANES), 0) == 0, carry, n_tile)
    carry_ref[...] = carry_ref[...] + n_tile
    cnt_ref[...] = carry_ref[...]


def _rank(eid, gates):
    t = eid.shape[0]
    tm = RANK_TM
    n_tiles = t // tm
    return pl.pallas_call(
        _rank_kernel,
        grid=(n_tiles,),
        in_specs=[pl.BlockSpec((tm, LANES), lambda i: (i, 0)), pl.BlockSpec((tm, LANES), lambda i: (i, 0))],
        out_specs=[pl.BlockSpec((1, IDX_ROWS, tm), lambda i: (i, 0, 0)),
                   pl.BlockSpec((1, IDX_ROWS, tm), lambda i: (i, 0, 0)),
                   pl.BlockSpec((1, 8, LANES), lambda i: (i, 0, 0)),
                   pl.BlockSpec((8, LANES), lambda i: (0, 0))],
        out_shape=[jax.ShapeDtypeStruct((n_tiles, IDX_ROWS, tm), I32),
                   jax.ShapeDtypeStruct((n_tiles, IDX_ROWS, tm), F32),
                   jax.ShapeDtypeStruct((n_tiles, 8, LANES), F32),
                   jax.ShapeDtypeStruct((8, LANES), F32)],
        scratch_shapes=[pltpu.VMEM((8, LANES), F32)],
        compiler_params=_cparams(("arbitrary",)),
        name="expert_rank",
    )(eid, gates)


def _dispatch_kernel(n_steps, cnt_ref, pend_ref, nused_ref, pos_ref, run_ref, run_prev_ref, hp_ref, xs_ref, sbuf,
                     zero_ref, sem, zsem):
    i = pl.program_id(0)
    tm = hp_ref.shape[0] // (2 * HP_TILES)
    zb = zero_ref.shape[0]
    n_blocks = xs_ref.shape[0] // zb

    @pl.when(i == 0)
    def _():
        zero_ref[...] = jnp.zeros_like(zero_ref)
        for b in range(2):
            sbuf[b, pl.ds(TOP_K * tm * HP_TILES, PIECE * HP_TILES), :] = jnp.zeros((PIECE * HP_TILES, LANES), U32)

        def zero_copy(block):
            return pltpu.make_async_copy(zero_ref, xs_ref.at[pl.ds(pl.multiple_of(block * zb, zb), zb)], zsem)

        def tail_start(j, carry):
            zero_copy(j).start()
            return carry

        def tail_wait(j, carry):
            zero_copy(j).wait()
            return carry

        for e in range(N_EXPERTS):
            @pl.when(cnt_ref[e] > 0)
            def _():
                zero_copy(pend_ref[e] - 1).start()
        lax.fori_loop(nused_ref[0], n_blocks, tail_start, 0)
        for e in range(N_EXPERTS):
            @pl.when(cnt_ref[e] > 0)
            def _():
                zero_copy(pend_ref[e] - 1).wait()
        lax.fori_loop(nused_ref[0], n_blocks, tail_wait, 0)

    def place(half):
        def body(t, carry):
            row = hp_ref[pl.ds(pl.multiple_of((half * tm + t) * HP_TILES, HP_TILES), HP_TILES), :]
            for k in range(TOP_K):
                sbuf[half, pl.ds(pl.multiple_of(pos_ref[half, k, t] * HP_TILES, HP_TILES), HP_TILES), :] = row
            return carry
        lax.fori_loop(0, tm, body, 0, unroll=16)

    def piece_copy(src_tok, dst_slot, half):
        rows = PIECE * HP_TILES
        return pltpu.make_async_copy(sbuf.at[half, pl.ds(pl.multiple_of(src_tok * HP_TILES, HP_TILES), rows)],
                                     xs_ref.at[pl.ds(pl.multiple_of(dst_slot * HP_TILES, HP_TILES), rows)],
                                     sem.at[half])

    def issue(half):
        for e in range(N_EXPERTS):
            slot0 = run_ref[half, 0, e]
            first = run_ref[half, 0, 2 * N_EXPERTS + e]

            def run_piece(j, carry):
                piece_copy(first + PIECE * j, slot0 + PIECE * j, half).start()
                return carry

            lax.fori_loop(0, run_ref[half, 0, N_EXPERTS + e], run_piece, 0)

    def drain(runs, half):
        def wait_piece(j, carry):
            piece_copy(0, 0, half).wait()
            return carry
        lax.fori_loop(0, runs[half, 0, 3 * N_EXPERTS], wait_piece, 0)

    place(0)

    @pl.when(i > 0)
    def _():
        drain(run_prev_ref, 1)
    issue(0)
    place(1)
    drain(run_ref, 0)
    issue(1)

    @pl.when(i == n_steps - 1)
    def _():
        drain(run_ref, 1)


def _dispatch(cnt, pend_blocks, nused, pos, runs, hp, n_slots):
    n_tiles, _, tm = pos.shape
    n_steps = n_tiles // 2
    grid_spec = pltpu.PrefetchScalarGridSpec(
        num_scalar_prefetch=3,
        grid=(n_steps,),
        in_specs=[pl.BlockSpec((2, TOP_K, tm), lambda i, *_: (i, 0, 0), memory_space=pltpu.SMEM),
                  pl.BlockSpec((2, 1, LANES), lambda i, *_: (i, 0, 0), memory_space=pltpu.SMEM),
                  pl.BlockSpec((2, 1, LANES), lambda i, *_: (jnp.maximum(i - 1, 0), 0, 0), memory_space=pltpu.SMEM),
                  pl.BlockSpec((2 * tm * HP_TILES, LANES), lambda i, *_: (i, 0))],
        out_specs=pl.BlockSpec(memory_space=pl.ANY),
        scratch_shapes=[pltpu.VMEM((2, (TOP_K * tm + PIECE) * HP_TILES, LANES), U32),
                        pltpu.VMEM((MOE_TM * HP_TILES, LANES), U32),
                        pltpu.SemaphoreType.DMA((2,)), pltpu.SemaphoreType.DMA(())],
    )
    return pl.pallas_call(
        functools.partial(_dispatch_kernel, n_steps),
        grid_spec=grid_spec,
        out_shape=jax.ShapeDtypeStruct((n_slots * HP_TILES, LANES), U32),
        compiler_params=_cparams(("arbitrary",)),
        name="moe_dispatch",
    )(cnt, pend_blocks, nused, pos, runs, runs, hp)


MOE_DOWN_CHUNKS = 4


def _moe_kernel(bexp_ref, nused_ref, xs_ref, wg_ref, wu_ref, wd_ref, o_ref, wgb, wub, wdb):
    i = pl.program_id(0)
    nused = nused_ref[0]
    tm = xs_ref.shape[0] // HP_TILES

    @pl.when((i < nused) & ((i == 0) | (bexp_ref[i] != bexp_ref[jnp.maximum(i - 1, 0)])))
    def _():
        wgb[...] = wg_ref[0].astype(BF16)
        wub[...] = wu_ref[0].astype(BF16)
        wdb[...] = wd_ref[0].astype(BF16)

    @pl.when(i < nused)
    def _():
        x = _unpack_halves(_load_row_tiles(xs_ref, tm, HP_TILES))
        hg = jnp.dot(x, wgb[...], preferred_element_type=F32)
        hu = jnp.dot(x, wub[...], preferred_element_type=F32)
        hdn = ((hg * jax.nn.sigmoid(hg)) * hu).astype(BF16)
        cw = D_MODEL // MOE_DOWN_CHUNKS
        for c in range(MOE_DOWN_CHUNKS):
            y = jnp.dot(hdn, wdb[:, c * cw:(c + 1) * cw], preferred_element_type=F32)
            for j in range(cw // LANES):
                o_ref[pl.ds(c * (cw // LANES) + j, tm, stride=YS_TILES), :] = y[:, j * LANES:(j + 1) * LANES]

    @pl.when(i >= nused)
    def _():
        o_ref[...] = jnp.zeros_like(o_ref)


def _moe(bexp, nused, xs, wg, wu, wd):
    tm = MOE_TM
    n_slots = xs.shape[0] // HP_TILES
    n_blocks = n_slots // tm
    grid_spec = pltpu.PrefetchScalarGridSpec(
        num_scalar_prefetch=2,
        grid=(n_blocks,),
        in_specs=[pl.BlockSpec((tm * HP_TILES, LANES), lambda i, be, nu: (jnp.minimum(i, nu[0] - 1), 0)),
                  pl.BlockSpec((1, D_MODEL, D_EXPERT), lambda i, be, nu: (be[i], 0, 0)),
                  pl.BlockSpec((1, D_MODEL, D_EXPERT), lambda i, be, nu: (be[i], 0, 0)),
                  pl.BlockSpec((1, D_EXPERT, D_MODEL), lambda i, be, nu: (be[i], 0, 0))],
        out_specs=pl.BlockSpec((tm * YS_TILES, LANES), lambda i, be, nu: (i, 0)),
        scratch_shapes=[pltpu.VMEM((D_MODEL, D_EXPERT), BF16), pltpu.VMEM((D_MODEL, D_EXPERT), BF16),
                        pltpu.VMEM((D_EXPERT, D_MODEL), BF16)],
    )
    return pl.pallas_call(
        _moe_kernel,
        grid_spec=grid_spec,
        out_shape=jax.ShapeDtypeStruct((n_slots * YS_TILES, LANES), F32),
        compiler_params=_cparams(("arbitrary",)),
        name="moe_experts",
    )(bexp, nused, xs, wg, wu, wd)


def _combine_kernel(n_tiles, pos_ref, run_ref, run_next_ref, gate_ref, x_ref, g_ref, ys_ref, o_ref, ybuf, mbuf, sem):
    i = pl.program_id(0)
    tm = x_ref.shape[0]
    slot = lax.rem(i, 2)

    def piece_copy(src_slot, dst_pos, buf):
        rows = PIECE * YS_TILES
        return pltpu.make_async_copy(ys_ref.at[pl.ds(pl.multiple_of(src_slot * YS_TILES, YS_TILES), rows)],
                                     ybuf.at[buf, pl.ds(pl.multiple_of(dst_pos * YS_TILES, YS_TILES), rows)],
                                     sem.at[buf])

    def request(runs, buf):
        for e in range(N_EXPERTS):
            slot0 = runs[0, 0, e]
            first = runs[0, 0, 2 * N_EXPERTS + e]

            def run_piece(j, carry):
                piece_copy(slot0 + PIECE * j, first + PIECE * j, buf).start()
                return carry

            lax.fori_loop(0, runs[0, 0, N_EXPERTS + e], run_piece, 0)

    @pl.when(i == 0)
    def _():
        request(run_ref, 0)

    @pl.when(i + 1 < n_tiles)
    def _():
        request(run_next_ref, 1 - slot)

    def drain(j, carry):
        piece_copy(0, 0, slot).wait()
        return carry

    lax.fori_loop(0, run_ref[0, 0, 3 * N_EXPERTS], drain, 0)

    def token(t, carry):
        acc = None
        for k in range(TOP_K):
            rows = pl.ds(pl.multiple_of(pos_ref[0, k, t] * YS_TILES, YS_TILES), YS_TILES)
            term = ybuf[slot, rows, :] * gate_ref[0, k, t]
            acc = term if acc is None else acc + term
        mbuf[pl.ds(pl.multiple_of(t * YS_TILES, YS_TILES), YS_TILES), :] = acc
        return carry

    lax.fori_loop(0, tm, token, 0, unroll=16)
    moe = _load_row_tiles(mbuf, tm, YS_TILES)
    o_ref[...] = _rms_scale(x_ref[...] + moe, g_ref[...], NORM_EPS)


def _combine(pos, runs, gidx, x2, g, ys, row0, n_rows):
    _, _, tm = pos.shape
    tile0 = row0 // tm
    n = n_rows // tm
    smem = lambda rows: pl.BlockSpec((1, rows, tm), lambda i: (tile0 + i, 0, 0), memory_space=pltpu.SMEM)
    run_spec = lambda ahead: pl.BlockSpec((1, 1, LANES), lambda i: (tile0 + jnp.minimum(i + ahead, n - 1), 0, 0),
                                          memory_space=pltpu.SMEM)
    return pl.pallas_call(
        functools.partial(_combine_kernel, n),
        grid=(n,),
        in_specs=[smem(TOP_K), run_spec(0), run_spec(1), smem(TOP_K),
                  pl.BlockSpec((tm, D_MODEL), lambda i: (tile0 + i, 0)),
                  pl.BlockSpec((1, D_MODEL), lambda i: (0, 0)),
                  pl.BlockSpec(memory_space=pl.ANY)],
        out_specs=pl.BlockSpec((tm, D_MODEL), lambda i: (i, 0)),
        out_shape=jax.ShapeDtypeStruct((n_rows, D_MODEL), F32),
        scratch_shapes=[pltpu.VMEM((2, (TOP_K * tm + N_EXPERTS * PIECE) * YS_TILES, LANES), F32),
                        pltpu.VMEM((tm * YS_TILES, LANES), F32),
                        pltpu.SemaphoreType.DMA((2,))],
        compiler_params=_cparams(("arbitrary",)),
        name="moe_combine",
    )(pos, runs, runs, gidx[:, 0:TOP_K, :], x2, g, ys)


def _routing_plan(idx, tiles, counts):
    cnt = counts[0, :N_EXPERTS].astype(I32)
    padded = jnp.where(cnt > 0, (cnt + PIECE - 1 + MOE_TM - 1) // MOE_TM * MOE_TM, 0)
    pends = jnp.cumsum(padded)
    pstarts = pends - padded
    n_tiles, _, tm = idx.shape
    n_blocks = n_tiles * tm * TOP_K // MOE_TM + N_EXPERTS + 1
    blk0 = jnp.arange(n_blocks, dtype=I32) * MOE_TM
    bexp = jnp.minimum(jnp.sum(pends[None, :] <= blk0[:, None], axis=1), N_EXPERTS - 1).astype(I32)
    nused = (pends[-1:] // MOE_TM).astype(I32)
    before = tiles[:, 0, :N_EXPERTS].astype(I32)
    inside = tiles[:, 1, :N_EXPERTS].astype(I32)
    first = jnp.cumsum(inside, axis=1) - inside
    pieces = (inside + PIECE - 1) // PIECE
    first_piece = (jnp.cumsum(pieces, axis=1) - pieces) * PIECE

    def run_table(first_pos):
        return jnp.concatenate([pstarts[None, :] + before, pieces, first_pos, jnp.sum(pieces, axis=1, keepdims=True),
                                jnp.zeros((n_tiles, LANES - 3 * N_EXPERTS - 1), I32)], axis=1).reshape(n_tiles, 1, LANES)

    pos = idx[:, 0:TOP_K, :]
    pos_piece = idx[:, 3 * TOP_K:4 * TOP_K, :]
    return pos, run_table(first), pos_piece, run_table(first_piece), cnt, (pends // MOE_TM).astype(I32), bexp, nused, n_blocks * MOE_TM


def kernel(x_prompt, x_sample, mem_prompt, mem_sample, mix_norm_g, w_in, na_rpb, diff_lambda_q1, diff_lambda_k1, diff_lambda_q2, diff_lambda_k2, diff_subln_g, w_branch_na, w_branch_diff, w_out, xa_norm_g, mem_norm_g, xa_w_q, xa_w_kv, xa_w_o, ffn_norm_g, router_group_w, router_group_b, router_expert_w, router_expert_b, w_gate, w_up, w_down, final_norm_g):
    nb_p, nb_s = x_prompt.shape[0], x_sample.shape[0]
    nb = nb_p + nb_s
    t = nb * SEQ
    xp = x_prompt.reshape(nb_p * SEQ, D_MODEL)
    xs_in = x_sample.reshape(nb_s * SEQ, D_MODEL)
    mem_p = mem_prompt.reshape(nb_p * MEM_LEN, D_MODEL)
    mem_s = mem_sample.reshape(nb_s * MEM_LEN, D_MODEL)
    row = lambda v: v.reshape(1, -1).astype(F32)
    bf = lambda w: w.astype(BF16)

    qcol = 3 * NA_WIDTH
    colscale = jnp.ones((1, IN_COLS), F32).at[:, qcol:qcol + DIFF_QK_WIDTH].set(DIFF_QK_DIM ** -0.5 * LOG2E)
    proj = _inproj(xp, xs_in, row(mix_norm_g[0]), bf(w_in[0]), colscale)
    proj3 = proj.reshape(nb, SEQ, IN_COLS)
    a = _na_attention(proj3, _na_bias_blocks(na_rpb[0]))
    b = _diff_attention(proj3, row(diff_lambda_q1[0]), row(diff_lambda_k1[0]),
                        row(diff_lambda_q2[0]), row(diff_lambda_k2[0]), row(diff_subln_g[0]))
    x1 = _merge(a.reshape(t, NA_WIDTH), b.reshape(t, DIFF_V_WIDTH), proj, xp, xs_in,
                bf(w_branch_na[0]), bf(w_branch_diff[0]), bf(w_out[0]))
    kv = _memkv(mem_p, mem_s, row(mem_norm_g[0]), bf(xa_w_kv[0]))
    w_router = jnp.zeros((D_MODEL, LANES), F32)
    w_router = w_router.at[:, :N_GROUPS].set(router_group_w[0]).at[:, N_GROUPS:N_GROUPS + N_EXPERTS].set(router_expert_w[0])
    b_router = jnp.zeros((1, LANES), F32)
    b_router = b_router.at[0, :N_GROUPS].set(router_group_b[0]).at[0, N_GROUPS:N_GROUPS + N_EXPERTS].set(router_expert_b[0])
    x2, hp, gates, eid = _xattn(x1, kv, row(xa_norm_g[0]), bf(xa_w_q[0]), bf(xa_w_o[0]),
                                row(ffn_norm_g[0]), bf(w_router), b_router)
    idx, gidx, tiles, counts = _rank(eid, gates)
    pos, runs, pos_c, runs_c, cnt, pend_blocks, bexp, nused, n_slots = _routing_plan(idx, tiles, counts)
    xs = _dispatch(cnt, pend_blocks, nused, pos, runs, hp, n_slots)
    ys = _moe(bexp, nused, xs, w_gate[0], w_up[0], w_down[0])
    fg = row(final_norm_g)
    y_p = _combine(pos_c, runs_c, gidx, x2, fg, ys, 0, nb_p * SEQ).reshape(nb_p, SEQ, D_MODEL)
    y_s = _combine(pos_c, runs_c, gidx, x2, fg, ys, nb_p * SEQ, nb_s * SEQ).reshape(nb_s, SEQ, D_MODEL)
    return (y_p, y_s)
```

```python
import functools
import math

import jax
import jax.numpy as jnp
from jax import lax
from jax.experimental import pallas as pl
from jax.experimental.pallas import tpu as pltpu

F32 = jnp.float32
BF16 = jnp.bfloat16
I32 = jnp.int32
U32 = jnp.uint32

D_MODEL = 2048
SEQ = 2048
GRID_W = 64
ROWS = SEQ // GRID_W
NA_HEADS = 8
NA_HEAD_DIM = 128
NA_WIDTH = NA_HEADS * NA_HEAD_DIM
NA_KH = 8
NA_KW = 16
DIFF_HEADS = 8
DIFF_QK_DIM = 64
DIFF_V_DIM = 128
DIFF_QK_WIDTH = DIFF_HEADS * 2 * DIFF_QK_DIM
DIFF_V_WIDTH = DIFF_HEADS * DIFF_V_DIM
IN_COLS = 3 * NA_WIDTH + 2 * DIFF_QK_WIDTH + DIFF_V_WIDTH + 2 * D_MODEL
MEM_LEN = 256
XA_HEADS = 4
XA_HEAD_DIM = 128
XA_WIDTH = XA_HEADS * XA_HEAD_DIM
N_GROUPS = 4
EXPERTS_PER_GROUP = 8
N_EXPERTS = N_GROUPS * EXPERTS_PER_GROUP
TOP_K = 2
D_EXPERT = 512
NORM_EPS = 1e-6
SUBLN_EPS = 1e-5
NEG_INF = -1e30
LAMBDA_INIT = 0.8 - 0.6 * math.exp(-0.3 * 0)
LOG2E = math.log2(math.e)

LANES = 128
HP_TILES = D_MODEL // 2 // LANES
YS_TILES = D_MODEL // LANES
VMEM_LIMIT = 56 * 1024 * 1024
BIG_VMEM_LIMIT = 60 * 1024 * 1024

INPROJ_TM = 1024
INPROJ_TN = 1280
NORM_CHUNK = 128
DIFF_TQ = 2048
DIFF_SUB = 128
MERGE_TM = 512
XA_TM = 1024
XA_SUB = 512
RANK_TM = 512
MOE_TM = 512


def _cparams(sem):
    return pltpu.CompilerParams(dimension_semantics=sem, vmem_limit_bytes=VMEM_LIMIT)


def _rms_scale(x, g, eps):
    ms = jnp.mean(x * x, axis=-1, keepdims=True)
    return x * lax.rsqrt(ms + eps) * g


def _dot_nt(a, b):
    return lax.dot_general(a, b, (((1,), (1,)), ((), ())), preferred_element_type=F32)


def _two_group_specs(tm, width, n_first):
    first = pl.BlockSpec((tm, width), lambda i, *_: (jnp.minimum(i, n_first - 1), 0))
    second = pl.BlockSpec((tm, width), lambda i, *_: (jnp.maximum(i - n_first, 0), 0))
    return first, second


def _inproj_kernel(n_first, xp_ref, xs_ref, g_ref, w_ref, cs_ref, o_ref, h_ref):
    def normalise(x_ref):
        def chunk(c, carry):
            rows = pl.ds(pl.multiple_of(c * NORM_CHUNK, NORM_CHUNK), NORM_CHUNK)
            h_ref[rows, :] = _rms_scale(x_ref[rows, :], g_ref[...], NORM_EPS).astype(BF16)
            return carry
        lax.fori_loop(0, x_ref.shape[0] // NORM_CHUNK, chunk, 0)

    first_col = pl.program_id(1) == 0
    in_first = pl.program_id(0) < n_first
    pl.when(first_col & in_first)(lambda: normalise(xp_ref))
    pl.when(first_col & jnp.logical_not(in_first))(lambda: normalise(xs_ref))
    acc = jnp.dot(h_ref[...], w_ref[...], preferred_element_type=F32)
    o_ref[...] = (acc * cs_ref[...]).astype(o_ref.dtype)


def _inproj(xp, xs, g, w, colscale):
    d = xp.shape[1]
    t = xp.shape[0] + xs.shape[0]
    n = w.shape[1]
    tm = INPROJ_TM
    n_first = xp.shape[0] // tm
    spec_p, spec_s = _two_group_specs(tm, d, n_first)
    return pl.pallas_call(
        functools.partial(_inproj_kernel, n_first),
        grid=(t // tm, n // INPROJ_TN),
        in_specs=[spec_p, spec_s,
                  pl.BlockSpec((1, d), lambda i, j: (0, 0)),
                  pl.BlockSpec((d, INPROJ_TN), lambda i, j: (0, j)),
                  pl.BlockSpec((1, INPROJ_TN), lambda i, j: (0, j))],
        out_specs=pl.BlockSpec((tm, INPROJ_TN), lambda i, j: (i, j)),
        out_shape=jax.ShapeDtypeStruct((t, n), BF16),
        scratch_shapes=[pltpu.VMEM((tm, d), BF16)],
        compiler_params=pltpu.CompilerParams(dimension_semantics=("parallel", "arbitrary"),
                                             vmem_limit_bytes=BIG_VMEM_LIMIT),
        name="inproj",
    )(xp, xs, g, w, colscale)


NA_QROWS = 4
NA_KROWS = 12
NA_TILES = ROWS // NA_QROWS
NA_HEADS_PER_STEP = 4


def _na_tile_key_row(ti):
    return min(max(NA_QROWS * ti - NA_KH // 2, 0), ROWS - NA_KROWS)


def _na_bias_blocks(rpb):
    cols = jnp.arange(GRID_W)
    cstart = jnp.clip(cols - NA_KW // 2, 0, GRID_W - NA_KW)
    col_ok = (cols[None, :] >= cstart[:, None]) & (cols[None, :] < cstart[:, None] + NA_KW)
    dc = jnp.clip(cols[None, :] - cols[:, None], 1 - NA_KW, NA_KW - 1) + NA_KW - 1
    pick = (dc[None] == jnp.arange(2 * NA_KW - 1)[:, None, None]).astype(F32)
    t = jnp.einsum("hdx,xck->hdck", rpb.astype(F32), pick, precision=lax.Precision.HIGHEST)
    return jnp.where(col_ok[None, None], t, NEG_INF)


def _na_kernel(q_ref, k_ref, v_ref, t_ref, o_ref, bias_ref):
    scale = NA_HEAD_DIM ** -0.5
    tq = NA_QROWS * GRID_W
    tk = NA_KROWS * GRID_W
    hd = NA_HEAD_DIM

    @pl.when(pl.program_id(1) == 0)
    def _():
        masked = jnp.full((GRID_W, GRID_W), NEG_INF, F32)
        for hh in range(NA_HEADS_PER_STEP):
            for variant, ti in enumerate((0, 1, NA_TILES - 1)):
                for jr in range(NA_QROWS):
                    r = NA_QROWS * ti + jr
                    rs = min(max(r - NA_KH // 2, 0), ROWS - NA_KH)
                    blocks = []
                    for i in range(NA_KROWS):
                        kr = _na_tile_key_row(ti) + i
                        blocks.append(t_ref[hh, kr - r + NA_KH - 1] if rs <= kr < rs + NA_KH else masked)
                    bias_ref[hh, variant, jr * GRID_W:(jr + 1) * GRID_W, :] = jnp.concatenate(blocks, axis=1) * LOG2E

    def key_rows(ti):
        k0 = _na_tile_key_row(ti) * GRID_W
        return slice(k0, k0 + tk)

    work = [(hh, ti) for hh in range(NA_HEADS_PER_STEP) for ti in range(NA_TILES)]

    def scores(item):
        hh, ti = item
        cols = slice(hh * hd, (hh + 1) * hd)
        return _dot_nt(q_ref[0, ti * tq:(ti + 1) * tq, cols], k_ref[0, key_rows(ti), cols])

    pending = scores(work[0])
    for n, (hh, ti) in enumerate(work):
        variant = 0 if ti == 0 else (2 if ti == NA_TILES - 1 else 1)
        cols = slice(hh * hd, (hh + 1) * hd)
        qrows = slice(ti * tq, (ti + 1) * tq)
        s = pending * (scale * LOG2E) + bias_ref[hh, variant]
        if n + 1 < len(work):
            pending = scores(work[n + 1])
        m = jnp.max(s, axis=-1, keepdims=True)
        e = jnp.exp2(s - m)
        l = jnp.sum(e, axis=-1, keepdims=True)
        o = jnp.dot(e.astype(BF16), v_ref[0, key_rows(ti), cols], preferred_element_type=F32)
        o_ref[0, qrows, cols] = (o / l).astype(o_ref.dtype)


def _na_attention(proj3, blocks):
    b = proj3.shape[0]
    hp = NA_HEADS_PER_STEP
    width = hp * NA_HEAD_DIM
    groups = NA_HEADS // hp
    return pl.pallas_call(
        _na_kernel,
        grid=(groups, b),
        in_specs=[pl.BlockSpec((1, SEQ, width), lambda h, i: (i, 0, h)),
                  pl.BlockSpec((1, SEQ, width), lambda h, i: (i, 0, groups + h)),
                  pl.BlockSpec((1, SEQ, width), lambda h, i: (i, 0, 2 * groups + h)),
                  pl.BlockSpec((hp,) + blocks.shape[1:], lambda h, i: (h, 0, 0, 0))],
        out_specs=pl.BlockSpec((1, SEQ, width), lambda h, i: (i, 0, h)),
        out_shape=jax.ShapeDtypeStruct((b, SEQ, NA_WIDTH), BF16),
        scratch_shapes=[pltpu.VMEM((hp, 3, NA_QROWS * GRID_W, NA_KROWS * GRID_W), F32)],
        compiler_params=_cparams(("arbitrary", "arbitrary")),
        name="na_attn",
    )(proj3, proj3, proj3, blocks)


def _alibi_table(tq):
    slopes = jnp.exp2(-8.0 * jnp.arange(1, DIFF_HEADS + 1, dtype=F32) / DIFF_HEADS)
    r = jnp.arange(tq)[:, None]
    x = jnp.arange(2 * SEQ - tq)[None, :]
    dist = jnp.abs(r - x + (SEQ - tq)).astype(F32)
    return (slopes * LOG2E)[:, None, None] * dist[None]


def _diff_kernel(lq1_ref, lk1_ref, lq2_ref, lk2_ref, q_ref, k_ref, v_ref, g_ref, alibi_ref, o_ref):
    qi = pl.program_id(2)
    ts = alibi_ref.shape[1]
    nsub = q_ref.shape[1] // ts
    lam = (jnp.exp(jnp.sum(lq1_ref[...] * lk1_ref[...], axis=-1, keepdims=True))
           - jnp.exp(jnp.sum(lq2_ref[...] * lk2_ref[...], axis=-1, keepdims=True)) + LAMBDA_INIT)
    k = k_ref[0]

    def scores(u):
        q = q_ref[0, u * ts:(u + 1) * ts, :]
        return [_dot_nt(q[:, mi * DIFF_QK_DIM:(mi + 1) * DIFF_QK_DIM], k[:, mi * DIFF_QK_DIM:(mi + 1) * DIFF_QK_DIM])
                for mi in range(2)]

    def expo(s, bias):
        s = s - bias
        m = jnp.max(s, axis=-1, keepdims=True)
        e = jnp.exp2(s - m)
        return e, jnp.sum(e, axis=-1, keepdims=True)

    pending = scores(0)
    for u in range(nsub):
        s0, s1 = pending
        if u + 1 < nsub:
            pending = scores(u + 1)
        off = (SEQ - ts) - (qi * nsub + u) * ts
        bias = alibi_ref[0, :, pl.ds(pl.multiple_of(off, ts), SEQ)]
        e0, l0 = expo(s0, bias)
        e1, l1 = expo(s1, bias)
        a = (e0 - e1 * (lam * l0 / l1)).astype(BF16)
        o = jnp.dot(a, v_ref[0], preferred_element_type=F32) / l0
        o = _rms_scale(o, g_ref[...], SUBLN_EPS) * (1.0 - LAMBDA_INIT)
        o_ref[0, u * ts:(u + 1) * ts, :] = o.astype(o_ref.dtype)


def _diff_attention(proj3, lq1, lk1, lq2, lk2, subln_g):
    b = proj3.shape[0]
    tq = DIFF_TQ
    qoff = 3 * NA_WIDTH // LANES
    koff = qoff + DIFF_QK_WIDTH // LANES
    voff = koff + DIFF_QK_WIDTH // LANES
    vec = lambda n: pl.BlockSpec((1, n), lambda h, i, j: (0, 0))
    return pl.pallas_call(
        _diff_kernel,
        grid=(DIFF_HEADS, b, SEQ // tq),
        in_specs=[vec(DIFF_QK_DIM), vec(DIFF_QK_DIM), vec(DIFF_QK_DIM), vec(DIFF_QK_DIM),
                  pl.BlockSpec((1, tq, LANES), lambda h, i, j: (i, j, qoff + h)),
                  pl.BlockSpec((1, SEQ, LANES), lambda h, i, j: (i, 0, koff + h)),
                  pl.BlockSpec((1, SEQ, LANES), lambda h, i, j: (i, 0, voff + h)),
                  vec(DIFF_V_DIM),
                  pl.BlockSpec((1, DIFF_SUB, 2 * SEQ - DIFF_SUB), lambda h, i, j: (h, 0, 0))],
        out_specs=pl.BlockSpec((1, tq, DIFF_V_DIM), lambda h, i, j: (i, j, h)),
        out_shape=jax.ShapeDtypeStruct((b, SEQ, DIFF_V_WIDTH), BF16),
        compiler_params=_cparams(("arbitrary", "arbitrary", "arbitrary")),
        name="diff_attn",
    )(lq1, lk1, lq2, lk2, proj3, proj3, proj3, subln_g, _alibi_table(DIFF_SUB))


def _merge_kernel(n_first, a_ref, b_ref, gna_ref, gdf_ref, xp_ref, xs_ref, wna_ref, wdf_ref, wout_ref, o_ref):
    pa = jnp.dot(a_ref[...], wna_ref[...], preferred_element_type=F32)
    pb = jnp.dot(b_ref[...], wdf_ref[...], preferred_element_type=F32)
    merged = (jax.nn.sigmoid(gna_ref[...].astype(F32)) * pa
              + jax.nn.sigmoid(gdf_ref[...].astype(F32)) * pb)
    delta = jnp.dot(merged.astype(BF16), wout_ref[...], preferred_element_type=F32)
    in_first = pl.program_id(0) < n_first

    @pl.when(in_first)
    def _():
        o_ref[...] = xp_ref[...] + delta

    @pl.when(jnp.logical_not(in_first))
    def _():
        o_ref[...] = xs_ref[...] + delta


def _merge(a, b, proj, xp, xs, wna, wdf, wout):
    t = xp.shape[0] + xs.shape[0]
    tm = MERGE_TM
    gna_blk = (3 * NA_WIDTH + 2 * DIFF_QK_WIDTH + DIFF_V_WIDTH) // D_MODEL
    const = lambda shape: pl.BlockSpec(shape, lambda i: (0, 0), pipeline_mode=pl.Buffered(1))
    n_first = xp.shape[0] // tm
    spec_p, spec_s = _two_group_specs(tm, D_MODEL, n_first)
    return pl.pallas_call(
        functools.partial(_merge_kernel, n_first),
        grid=(t // tm,),
        in_specs=[pl.BlockSpec((tm, NA_WIDTH), lambda i: (i, 0)),
                  pl.BlockSpec((tm, DIFF_V_WIDTH), lambda i: (i, 0)),
                  pl.BlockSpec((tm, D_MODEL), lambda i: (i, gna_blk)),
                  pl.BlockSpec((tm, D_MODEL), lambda i: (i, gna_blk + 1)),
                  spec_p, spec_s,
                  const((NA_WIDTH, D_MODEL)), const((DIFF_V_WIDTH, D_MODEL)), const((D_MODEL, D_MODEL))],
        out_specs=pl.BlockSpec((tm, D_MODEL), lambda i: (i, 0)),
        out_shape=jax.ShapeDtypeStruct((t, D_MODEL), F32),
        compiler_params=pltpu.CompilerParams(dimension_semantics=("parallel",), vmem_limit_bytes=BIG_VMEM_LIMIT),
        name="merge_outproj",
    )(a, b, proj, proj, xp, xs, wna, wdf, wout)


def _memkv_kernel(n_first, mp_ref, ms_ref, g_ref, w_ref, o_ref):
    def project(m_ref):
        h = _rms_scale(m_ref[...], g_ref[...], NORM_EPS).astype(BF16)
        o_ref[...] = jnp.dot(h, w_ref[...], preferred_element_type=F32).astype(o_ref.dtype)

    in_first = pl.program_id(0) < n_first
    pl.when(in_first)(lambda: project(mp_ref))
    pl.when(jnp.logical_not(in_first))(lambda: project(ms_ref))


def _memkv(mem_p, mem_s, g, w):
    t = mem_p.shape[0] + mem_s.shape[0]
    n_first = mem_p.shape[0] // MEM_LEN
    spec_p, spec_s = _two_group_specs(MEM_LEN, D_MODEL, n_first)
    return pl.pallas_call(
        functools.partial(_memkv_kernel, n_first),
        grid=(t // MEM_LEN,),
        in_specs=[spec_p, spec_s,
                  pl.BlockSpec((1, D_MODEL), lambda i: (0, 0)),
                  pl.BlockSpec((D_MODEL, 2 * XA_WIDTH), lambda i: (0, 0))],
        out_specs=pl.BlockSpec((MEM_LEN, 2 * XA_WIDTH), lambda i: (i, 0)),
        out_shape=jax.ShapeDtypeStruct((t, 2 * XA_WIDTH), BF16),
        compiler_params=_cparams(("parallel",)),
        name="mem_kv",
    )(mem_p, mem_s, g, w)


def _route(logits):
    lane = lax.broadcasted_iota(I32, logits.shape, 1).astype(F32)
    ninf = -jnp.inf
    big = float(LANES)
    gl = jnp.where(lane < N_GROUPS, logits, ninf)
    gmax = jnp.max(gl, axis=-1, keepdims=True)
    g = jnp.min(jnp.where(gl == gmax, lane, big), axis=-1, keepdims=True)
    pg = 1.0 / jnp.sum(jnp.exp(gl - gmax), axis=-1, keepdims=True)
    lo = N_GROUPS + EXPERTS_PER_GROUP * g
    el = jnp.where((lane >= lo) & (lane < lo + EXPERTS_PER_GROUP), logits, ninf)
    v1 = jnp.max(el, axis=-1, keepdims=True)
    i1 = jnp.min(jnp.where(el == v1, lane, big), axis=-1, keepdims=True)
    el2 = jnp.where(lane == i1, ninf, el)
    v2 = jnp.max(el2, axis=-1, keepdims=True)
    i2 = jnp.min(jnp.where(el2 == v2, lane, big), axis=-1, keepdims=True)
    t = jnp.exp(v2 - v1)
    den = 1.0 + t
    gate1 = pg * (1.0 / den)
    gate2 = pg * (t / den)
    gates = jnp.where(lane == 0.0, gate1, jnp.where(lane == 1.0, gate2, 0.0))
    eids = jnp.where(lane == 0.0, i1 - N_GROUPS, jnp.where(lane == 1.0, i2 - N_GROUPS, 0.0)).astype(I32)
    return gates, eids


def _pack_halves(h):
    half = h.shape[1] // 2
    hi = pltpu.bitcast(h[:, :half].astype(BF16).astype(F32), U32)
    lo = pltpu.bitcast(h[:, half:].astype(BF16).astype(F32), U32)
    return hi | (lo >> 16)


def _unpack_halves(w):
    hi = pltpu.bitcast(w & jnp.uint32(0xFFFF0000), F32).astype(BF16)
    lo = pltpu.bitcast(w << 16, F32).astype(BF16)
    return jnp.concatenate([hi, lo], axis=1)


def _store_row_tiles(ref, x):
    n, width = x.shape
    c = width // LANES
    for j in range(c):
        ref[pl.ds(j, n, stride=c), :] = x[:, j * LANES:(j + 1) * LANES]


def _load_row_tiles(ref, n, c):
    return jnp.concatenate([ref[pl.ds(j, n, stride=c), :] for j in range(c)], axis=1)


def _xattn_kernel(x_ref, kv_ref, gx_ref, wq_ref, wo_ref, gf_ref, wr_ref, br_ref,
                  x2_ref, hp_ref, gate_ref, eid_ref):
    scale = XA_HEAD_DIM ** -0.5
    heads = [slice(h * XA_HEAD_DIM, (h + 1) * XA_HEAD_DIM) for h in range(XA_HEADS)]
    nsub = x_ref.shape[0] // XA_SUB

    def rows(u):
        return slice(u * XA_SUB, (u + 1) * XA_SUB)

    def scores(u):
        hq = _rms_scale(x_ref[rows(u), :], gx_ref[...], NORM_EPS).astype(BF16)
        q = jnp.dot(hq, wq_ref[...], preferred_element_type=F32).astype(BF16)
        return [_dot_nt(q[:, sl], kv_ref[:, sl]) for sl in heads]

    def finish(u, sc):
        outs = []
        for h in range(XA_HEADS):
            vsl = slice(XA_WIDTH + h * XA_HEAD_DIM, XA_WIDTH + (h + 1) * XA_HEAD_DIM)
            s = sc[h] * (scale * LOG2E)
            m = jnp.max(s, axis=-1, keepdims=True)
            e = jnp.exp2(s - m)
            l = jnp.sum(e, axis=-1, keepdims=True)
            o = jnp.dot(e.astype(BF16), kv_ref[:, vsl], preferred_element_type=F32)
            outs.append((o / l).astype(BF16))
        o = jnp.concatenate(outs, axis=1)
        x2 = x_ref[rows(u), :] + jnp.dot(o, wo_ref[...], preferred_element_type=F32)
        x2_ref[rows(u), :] = x2
        h3 = _rms_scale(x2, gf_ref[...], NORM_EPS)
        _store_row_tiles(hp_ref.at[pl.ds(u * XA_SUB * HP_TILES, XA_SUB * HP_TILES)], _pack_halves(h3))
        logits = jnp.dot(h3.astype(BF16), wr_ref[...], preferred_element_type=F32) + br_ref[...]
        gates, eids = _route(logits)
        gate_ref[rows(u), :] = gates
        eid_ref[rows(u), :] = eids

    pending = scores(0)
    for u in range(nsub):
        sc = pending
        if u + 1 < nsub:
            pending = scores(u + 1)
        finish(u, sc)


def _xattn(x1, kv, gx, wq, wo, gf, wr, br):
    t = x1.shape[0]
    tm = XA_TM
    per_batch = SEQ // tm
    const = lambda shape: pl.BlockSpec(shape, lambda i: (0, 0), pipeline_mode=pl.Buffered(1))
    tile = lambda n: pl.BlockSpec((tm, n), lambda i: (i, 0))
    return pl.pallas_call(
        _xattn_kernel,
        grid=(t // tm,),
        in_specs=[tile(D_MODEL),
                  pl.BlockSpec((MEM_LEN, 2 * XA_WIDTH), lambda i: (i // per_batch, 0)),
                  const((1, D_MODEL)), const((D_MODEL, XA_WIDTH)), const((XA_WIDTH, D_MODEL)),
                  const((1, D_MODEL)), const((D_MODEL, LANES)), const((1, LANES))],
        out_specs=[tile(D_MODEL), pl.BlockSpec((tm * HP_TILES, LANES), lambda i: (i, 0)), tile(LANES), tile(LANES)],
        out_shape=[jax.ShapeDtypeStruct((t, D_MODEL), F32),
                   jax.ShapeDtypeStruct((t * HP_TILES, LANES), U32),
                   jax.ShapeDtypeStruct((t, LANES), F32),
                   jax.ShapeDtypeStruct((t, LANES), I32)],
        compiler_params=pltpu.CompilerParams(dimension_semantics=("parallel",), vmem_limit_bytes=BIG_VMEM_LIMIT),
        name="xattn_router",
    )(x1, kv, gx, wq, wo, gf, wr, br)


IDX_ROWS = 8
PIECE = 8


def _rank_kernel(eid_ref, gate_ref, idx_ref, gidx_ref, tile_ref, cnt_ref, carry_ref):
    @pl.when(pl.program_id(0) == 0)
    def _():
        carry_ref[...] = jnp.zeros_like(carry_ref)

    tm = eid_ref.shape[0]
    eid = eid_ref[...]
    lane = lax.broadcasted_iota(I32, (tm, LANES), 1)
    e1 = eid[:, 0:1]
    e2 = eid[:, 1:2]
    m1 = lane == e1
    m2 = lane == e2
    onehot = jnp.where(m1, 1.0, jnp.where(m2, 1.0, 0.0))
    r = lax.broadcasted_iota(I32, (tm, tm), 0)
    c = lax.broadcasted_iota(I32, (tm, tm), 1)
    tri = jnp.where(c < r, 1.0, 0.0).astype(BF16)
    local = jnp.dot(tri, onehot.astype(BF16), preferred_element_type=F32)
    carry = carry_ref[0:1, :]
    n_tile = jnp.sum(onehot, axis=0, keepdims=True)

    n_piece = jnp.floor((n_tile + (PIECE - 1)) * (1.0 / PIECE)) * PIECE

    def assignment(m, e):
        r_local = jnp.sum(jnp.where(m, local, 0.0), axis=-1, keepdims=True)
        r_global = r_local + jnp.sum(jnp.where(m, carry, 0.0), axis=-1, keepdims=True)
        first = jnp.sum(jnp.where(lane < e, n_tile, 0.0), axis=-1, keepdims=True)
        first_piece = jnp.sum(jnp.where(lane < e, n_piece, 0.0), axis=-1, keepdims=True)
        return first + r_local, r_global, first_piece + r_local

    p1, g1, q1 = assignment(m1, e1)
    p2, g2, q2 = assignment(m2, e2)
    cols = [p1, p2, g1, g2, e1.astype(F32), e2.astype(F32), q1, q2]
    table = jnp.zeros((tm, LANES), F32)
    for j, col in enumerate(cols):
        table = jnp.where(lane == j, col, table)
    idx_ref[0] = jnp.transpose(table)[0:IDX_ROWS, :].astype(I32)
    gidx_ref[0] = jnp.transpose(gate_ref[...])[0:IDX_ROWS, :]
    tile_ref[0] = jnp.where(lax.broadcasted_iota(I32, (8, LANES), 0) == 0, carry, n_tile)
    carry_ref[...] = carry_ref[...] + n_tile
    cnt_ref[...] = carry_ref[...]


def _rank(eid, gates):
    t = eid.shape[0]
    tm = RANK_TM
    n_tiles = t // tm
    return pl.pallas_call(
        _rank_kernel,
        grid=(n_tiles,),
        in_specs=[pl.BlockSpec((tm, LANES), lambda i: (i, 0)), pl.BlockSpec((tm, LANES), lambda i: (i, 0))],
        out_specs=[pl.BlockSpec((1, IDX_ROWS, tm), lambda i: (i, 0, 0)),
                   pl.BlockSpec((1, IDX_ROWS, tm), lambda i: (i, 0, 0)),
                   pl.BlockSpec((1, 8, LANES), lambda i: (i, 0, 0)),
                   pl.BlockSpec((8, LANES), lambda i: (0, 0))],
        out_shape=[jax.ShapeDtypeStruct((n_tiles, IDX_ROWS, tm), I32),
                   jax.ShapeDtypeStruct((n_tiles, IDX_ROWS, tm), F32),
                   jax.ShapeDtypeStruct((n_tiles, 8, LANES), F32),
                   jax.ShapeDtypeStruct((8, LANES), F32)],
        scratch_shapes=[pltpu.VMEM((8, LANES), F32)],
        compiler_params=_cparams(("arbitrary",)),
        name="expert_rank",
    )(eid, gates)


def _dispatch_kernel(n_steps, cnt_ref, pend_ref, nused_ref, pos_ref, run_ref, run_prev_ref, hp_ref, xs_ref, sbuf,
                     zero_ref, sem, zsem):
    i = pl.program_id(0)
    tm = hp_ref.shape[0] // (2 * HP_TILES)
    zb = zero_ref.shape[0]
    n_blocks = xs_ref.shape[0] // zb

    @pl.when(i == 0)
    def _():
        zero_ref[...] = jnp.zeros_like(zero_ref)
        for b in range(2):
            sbuf[b, pl.ds(TOP_K * tm * HP_TILES, PIECE * HP_TILES), :] = jnp.zeros((PIECE * HP_TILES, LANES), U32)

        def zero_copy(block):
            return pltpu.make_async_copy(zero_ref, xs_ref.at[pl.ds(pl.multiple_of(block * zb, zb), zb)], zsem)

        def tail_start(j, carry):
            zero_copy(j).start()
            return carry

        def tail_wait(j, carry):
            zero_copy(j).wait()
            return carry

        for e in range(N_EXPERTS):
            @pl.when(cnt_ref[e] > 0)
            def _():
                zero_copy(pend_ref[e] - 1).start()
        lax.fori_loop(nused_ref[0], n_blocks, tail_start, 0)
        for e in range(N_EXPERTS):
            @pl.when(cnt_ref[e] > 0)
            def _():
                zero_copy(pend_ref[e] - 1).wait()
        lax.fori_loop(nused_ref[0], n_blocks, tail_wait, 0)

    def place(half):
        def body(t, carry):
            row = hp_ref[pl.ds(pl.multiple_of((half * tm + t) * HP_TILES, HP_TILES), HP_TILES), :]
            for k in range(TOP_K):
                sbuf[half, pl.ds(pl.multiple_of(pos_ref[half, k, t] * HP_TILES, HP_TILES), HP_TILES), :] = row
            return carry
        lax.fori_loop(0, tm, body, 0, unroll=16)

    def piece_copy(src_tok, dst_slot, half):
        rows = PIECE * HP_TILES
        return pltpu.make_async_copy(sbuf.at[half, pl.ds(pl.multiple_of(src_tok * HP_TILES, HP_TILES), rows)],
                                     xs_ref.at[pl.ds(pl.multiple_of(dst_slot * HP_TILES, HP_TILES), rows)],
                                     sem.at[half])

    def issue(half):
        for e in range(N_EXPERTS):
            slot0 = run_ref[half, 0, e]
            first = run_ref[half, 0, 2 * N_EXPERTS + e]

            def run_piece(j, carry):
                piece_copy(first + PIECE * j, slot0 + PIECE * j, half).start(priority=e % 2)
                return carry

            lax.fori_loop(0, run_ref[half, 0, N_EXPERTS + e], run_piece, 0)

    def drain(runs, half):
        def wait_piece(j, carry):
            piece_copy(0, 0, half).wait()
            return carry
        lax.fori_loop(0, runs[half, 0, 3 * N_EXPERTS], wait_piece, 0)

    place(0)

    @pl.when(i > 0)
    def _():
        drain(run_prev_ref, 1)
    issue(0)
    place(1)
    drain(run_ref, 0)
    issue(1)

    @pl.when(i == n_steps - 1)
    def _():
        drain(run_ref, 1)


def _dispatch(cnt, pend_blocks, nused, pos, runs, hp, n_slots):
    n_tiles, _, tm = pos.shape
    n_steps = n_tiles // 2
    grid_spec = pltpu.PrefetchScalarGridSpec(
        num_scalar_prefetch=3,
        grid=(n_steps,),
        in_specs=[pl.BlockSpec((2, TOP_K, tm), lambda i, *_: (i, 0, 0), memory_space=pltpu.SMEM),
                  pl.BlockSpec((2, 1, LANES), lambda i, *_: (i, 0, 0), memory_space=pltpu.SMEM),
                  pl.BlockSpec((2, 1, LANES), lambda i, *_: (jnp.maximum(i - 1, 0), 0, 0), memory_space=pltpu.SMEM),
                  pl.BlockSpec((2 * tm * HP_TILES, LANES), lambda i, *_: (i, 0))],
        out_specs=pl.BlockSpec(memory_space=pl.ANY),
        scratch_shapes=[pltpu.VMEM((2, (TOP_K * tm + PIECE) * HP_TILES, LANES), U32),
                        pltpu.VMEM((MOE_TM * HP_TILES, LANES), U32),
                        pltpu.SemaphoreType.DMA((2,)), pltpu.SemaphoreType.DMA(())],
    )
    return pl.pallas_call(
        functools.partial(_dispatch_kernel, n_steps),
        grid_spec=grid_spec,
        out_shape=jax.ShapeDtypeStruct((n_slots * HP_TILES, LANES), U32),
        compiler_params=_cparams(("arbitrary",)),
        name="moe_dispatch",
    )(cnt, pend_blocks, nused, pos, runs, runs, hp)


MOE_DOWN_CHUNKS = 4


def _moe_kernel(bexp_ref, nused_ref, xs_ref, wg_ref, wu_ref, wd_ref, o_ref, wgb, wub, wdb):
    i = pl.program_id(0)
    nused = nused_ref[0]
    tm = xs_ref.shape[0] // HP_TILES

    @pl.when((i < nused) & ((i == 0) | (bexp_ref[i] != bexp_ref[jnp.maximum(i - 1, 0)])))
    def _():
        wgb[...] = wg_ref[0].astype(BF16)
        wub[...] = wu_ref[0].astype(BF16)
        wdb[...] = wd_ref[0].astype(BF16)

    @pl.when(i < nused)
    def _():
        x = _unpack_halves(_load_row_tiles(xs_ref, tm, HP_TILES))
        hg = jnp.dot(x, wgb[...], preferred_element_type=F32)
        hu = jnp.dot(x, wub[...], preferred_element_type=F32)
        hdn = ((hg * jax.nn.sigmoid(hg)) * hu).astype(BF16)
        cw = D_MODEL // MOE_DOWN_CHUNKS
        for c in range(MOE_DOWN_CHUNKS):
            y = jnp.dot(hdn, wdb[:, c * cw:(c + 1) * cw], preferred_element_type=F32)
            for j in range(cw // LANES):
                o_ref[pl.ds(c * (cw // LANES) + j, tm, stride=YS_TILES), :] = y[:, j * LANES:(j + 1) * LANES]

    @pl.when(i >= nused)
    def _():
        o_ref[...] = jnp.zeros_like(o_ref)


def _moe(bexp, nused, xs, wg, wu, wd):
    tm = MOE_TM
    n_slots = xs.shape[0] // HP_TILES
    n_blocks = n_slots // tm
    grid_spec = pltpu.PrefetchScalarGridSpec(
        num_scalar_prefetch=2,
        grid=(n_blocks,),
        in_specs=[pl.BlockSpec((tm * HP_TILES, LANES), lambda i, be, nu: (jnp.minimum(i, nu[0] - 1), 0)),
                  pl.BlockSpec((1, D_MODEL, D_EXPERT), lambda i, be, nu: (be[i], 0, 0)),
                  pl.BlockSpec((1, D_MODEL, D_EXPERT), lambda i, be, nu: (be[i], 0, 0)),
                  pl.BlockSpec((1, D_EXPERT, D_MODEL), lambda i, be, nu: (be[i], 0, 0))],
        out_specs=pl.BlockSpec((tm * YS_TILES, LANES), lambda i, be, nu: (i, 0)),
        scratch_shapes=[pltpu.VMEM((D_MODEL, D_EXPERT), BF16), pltpu.VMEM((D_MODEL, D_EXPERT), BF16),
                        pltpu.VMEM((D_EXPERT, D_MODEL), BF16)],
    )
    return pl.pallas_call(
        _moe_kernel,
        grid_spec=grid_spec,
        out_shape=jax.ShapeDtypeStruct((n_slots * YS_TILES, LANES), F32),
        compiler_params=_cparams(("arbitrary",)),
        name="moe_experts",
    )(bexp, nused, xs, wg, wu, wd)


def _combine_kernel(n_tiles, pos_ref, run_ref, run_next_ref, gate_ref, x_ref, g_ref, ys_ref, o_ref, ybuf, mbuf, sem):
    i = pl.program_id(0)
    tm = x_ref.shape[0]
    slot = lax.rem(i, 2)

    def piece_copy(src_slot, dst_pos, buf):
        rows = PIECE * YS_TILES
        return pltpu.make_async_copy(ys_ref.at[pl.ds(pl.multiple_of(src_slot * YS_TILES, YS_TILES), rows)],
                                     ybuf.at[buf, pl.ds(pl.multiple_of(dst_pos * YS_TILES, YS_TILES), rows)],
                                     sem.at[buf])

    def request(runs, buf):
        for e in range(N_EXPERTS):
            slot0 = runs[0, 0, e]
            first = runs[0, 0, 2 * N_EXPERTS + e]

            def run_piece(j, carry):
                piece_copy(slot0 + PIECE * j, first + PIECE * j, buf).start(priority=e % 2)
                return carry

            lax.fori_loop(0, runs[0, 0, N_EXPERTS + e], run_piece, 0)

    @pl.when(i == 0)
    def _():
        request(run_ref, 0)

    @pl.when(i + 1 < n_tiles)
    def _():
        request(run_next_ref, 1 - slot)

    def drain(j, carry):
        piece_copy(0, 0, slot).wait()
        return carry

    lax.fori_loop(0, run_ref[0, 0, 3 * N_EXPERTS], drain, 0)

    def token(t, carry):
        acc = None
        for k in range(TOP_K):
            rows = pl.ds(pl.multiple_of(pos_ref[0, k, t] * YS_TILES, YS_TILES), YS_TILES)
            term = ybuf[slot, rows, :] * gate_ref[0, k, t]
            acc = term if acc is None else acc + term
        mbuf[pl.ds(pl.multiple_of(t * YS_TILES, YS_TILES), YS_TILES), :] = acc
        return carry

    lax.fori_loop(0, tm, token, 0, unroll=16)
    moe = _load_row_tiles(mbuf, tm, YS_TILES)
    o_ref[...] = _rms_scale(x_ref[...] + moe, g_ref[...], NORM_EPS)


def _combine(pos, runs, gidx, x2, g, ys, row0, n_rows):
    _, _, tm = pos.shape
    tile0 = row0 // tm
    n = n_rows // tm
    smem = lambda rows: pl.BlockSpec((1, rows, tm), lambda i: (tile0 + i, 0, 0), memory_space=pltpu.SMEM)
    run_spec = lambda ahead: pl.BlockSpec((1, 1, LANES), lambda i: (tile0 + jnp.minimum(i + ahead, n - 1), 0, 0),
                                          memory_space=pltpu.SMEM)
    return pl.pallas_call(
        functools.partial(_combine_kernel, n),
        grid=(n,),
        in_specs=[smem(TOP_K), run_spec(0), run_spec(1), smem(TOP_K),
                  pl.BlockSpec((tm, D_MODEL), lambda i: (tile0 + i, 0)),
                  pl.BlockSpec((1, D_MODEL), lambda i: (0, 0)),
                  pl.BlockSpec(memory_space=pl.ANY)],
        out_specs=pl.BlockSpec((tm, D_MODEL), lambda i: (i, 0)),
        out_shape=jax.ShapeDtypeStruct((n_rows, D_MODEL), F32),
        scratch_shapes=[pltpu.VMEM((2, (TOP_K * tm + N_EXPERTS * PIECE) * YS_TILES, LANES), F32),
                        pltpu.VMEM((tm * YS_TILES, LANES), F32),
                        pltpu.SemaphoreType.DMA((2,))],
        compiler_params=_cparams(("arbitrary",)),
        name="moe_combine",
    )(pos, runs, runs, gidx[:, 0:TOP_K, :], x2, g, ys)


def _routing_plan(idx, tiles, counts):
    cnt = counts[0, :N_EXPERTS].astype(I32)
    padded = jnp.where(cnt > 0, (cnt + PIECE - 1 + MOE_TM - 1) // MOE_TM * MOE_TM, 0)
    pends = jnp.cumsum(padded)
    pstarts = pends - padded
    n_tiles, _, tm = idx.shape
    n_blocks = n_tiles * tm * TOP_K // MOE_TM + N_EXPERTS + 1
    blk0 = jnp.arange(n_blocks, dtype=I32) * MOE_TM
    bexp = jnp.minimum(jnp.sum(pends[None, :] <= blk0[:, None], axis=1), N_EXPERTS - 1).astype(I32)
    nused = (pends[-1:] // MOE_TM).astype(I32)
    before = tiles[:, 0, :N_EXPERTS].astype(I32)
    inside = tiles[:, 1, :N_EXPERTS].astype(I32)
    first = jnp.cumsum(inside, axis=1) - inside
    pieces = (inside + PIECE - 1) // PIECE
    first_piece = (jnp.cumsum(pieces, axis=1) - pieces) * PIECE

    def run_table(first_pos):
        return jnp.concatenate([pstarts[None, :] + before, pieces, first_pos, jnp.sum(pieces, axis=1, keepdims=True),
                                jnp.zeros((n_tiles, LANES - 3 * N_EXPERTS - 1), I32)], axis=1).reshape(n_tiles, 1, LANES)

    pos = idx[:, 0:TOP_K, :]
    pos_piece = idx[:, 3 * TOP_K:4 * TOP_K, :]
    return pos, run_table(first), pos_piece, run_table(first_piece), cnt, (pends // MOE_TM).astype(I32), bexp, nused, n_blocks * MOE_TM


def kernel(x_prompt, x_sample, mem_prompt, mem_sample, mix_norm_g, w_in, na_rpb, diff_lambda_q1, diff_lambda_k1, diff_lambda_q2, diff_lambda_k2, diff_subln_g, w_branch_na, w_branch_diff, w_out, xa_norm_g, mem_norm_g, xa_w_q, xa_w_kv, xa_w_o, ffn_norm_g, router_group_w, router_group_b, router_expert_w, router_expert_b, w_gate, w_up, w_down, final_norm_g):
    nb_p, nb_s = x_prompt.shape[0], x_sample.shape[0]
    nb = nb_p + nb_s
    t = nb * SEQ
    xp = x_prompt.reshape(nb_p * SEQ, D_MODEL)
    xs_in = x_sample.reshape(nb_s * SEQ, D_MODEL)
    mem_p = mem_prompt.reshape(nb_p * MEM_LEN, D_MODEL)
    mem_s = mem_sample.reshape(nb_s * MEM_LEN, D_MODEL)
    row = lambda v: v.reshape(1, -1).astype(F32)
    bf = lambda w: w.astype(BF16)

    qcol = 3 * NA_WIDTH
    colscale = jnp.ones((1, IN_COLS), F32).at[:, qcol:qcol + DIFF_QK_WIDTH].set(DIFF_QK_DIM ** -0.5 * LOG2E)
    proj = _inproj(xp, xs_in, row(mix_norm_g[0]), bf(w_in[0]), colscale)
    proj3 = proj.reshape(nb, SEQ, IN_COLS)
    a = _na_attention(proj3, _na_bias_blocks(na_rpb[0]))
    b = _diff_attention(proj3, row(diff_lambda_q1[0]), row(diff_lambda_k1[0]),
                        row(diff_lambda_q2[0]), row(diff_lambda_k2[0]), row(diff_subln_g[0]))
    x1 = _merge(a.reshape(t, NA_WIDTH), b.reshape(t, DIFF_V_WIDTH), proj, xp, xs_in,
                bf(w_branch_na[0]), bf(w_branch_diff[0]), bf(w_out[0]))
    kv = _memkv(mem_p, mem_s, row(mem_norm_g[0]), bf(xa_w_kv[0]))
    w_router = jnp.zeros((D_MODEL, LANES), F32)
    w_router = w_router.at[:, :N_GROUPS].set(router_group_w[0]).at[:, N_GROUPS:N_GROUPS + N_EXPERTS].set(router_expert_w[0])
    b_router = jnp.zeros((1, LANES), F32)
    b_router = b_router.at[0, :N_GROUPS].set(router_group_b[0]).at[0, N_GROUPS:N_GROUPS + N_EXPERTS].set(router_expert_b[0])
    x2, hp, gates, eid = _xattn(x1, kv, row(xa_norm_g[0]), bf(xa_w_q[0]), bf(xa_w_o[0]),
                                row(ffn_norm_g[0]), bf(w_router), b_router)
    idx, gidx, tiles, counts = _rank(eid, gates)
    pos, runs, pos_c, runs_c, cnt, pend_blocks, bexp, nused, n_slots = _routing_plan(idx, tiles, counts)
    xs = _dispatch(cnt, pend_blocks, nused, pos, runs, hp, n_slots)
    ys = _moe(bexp, nused, xs, w_gate[0], w_up[0], w_down[0])
    fg = row(final_norm_g)
    y_p = _combine(pos_c, runs_c, gidx, x2, fg, ys, 0, nb_p * SEQ).reshape(nb_p, SEQ, D_MODEL)
    y_s = _combine(pos_c, runs_c, gidx, x2, fg, ys, nb_p * SEQ, nb_s * SEQ).reshape(nb_s, SEQ, D_MODEL)
    return (y_p, y_s)
```
